```python
import math
import jax, jax.numpy as jnp
from jax import lax
import numpy as np

D_MODEL = 2048
BATCH = 8
SEQ = 2048
DEPTH = 1

HGRN_WIDTH = D_MODEL // 2
HGRN_HEAD_DIM = 128
HGRN_HEADS = HGRN_WIDTH // HGRN_HEAD_DIM
CHUNK = 64
ATTN_WIDTH = D_MODEL - HGRN_WIDTH
ATTN_HEAD_DIM = 128
ATTN_HEADS = ATTN_WIDTH // ATTN_HEAD_DIM
ATTN_KV_HEADS = 2
KV_WIDTH = ATTN_KV_HEADS * ATTN_HEAD_DIM
WINDOW = 128
ATTN_BLOCK = 128
KEY_SPAN = ATTN_BLOCK + 2 * WINDOW
REL_BUCKETS = 32
REL_MAX_DIST = 128
D_FF = 5632
EPS = 1e-6
NEG_INF = -1e30
IN_SPLITS = (HGRN_WIDTH, HGRN_WIDTH, HGRN_WIDTH, HGRN_WIDTH, HGRN_WIDTH, ATTN_WIDTH, KV_WIDTH, KV_WIDTH)
IN_COLS = sum(IN_SPLITS)

kernel_name = "hymba_hgrn2_swa_macaron_sandwich"


def rms_norm(x, gain):
    xf = x.astype(jnp.float32)
    y = xf * lax.rsqrt(jnp.mean(xf * xf, axis=-1, keepdims=True) + EPS)
    return (y * gain.astype(jnp.float32)).astype(x.dtype)


def swiglu(x, w_gate_up, w_down):
    gate, up = jnp.split(x @ w_gate_up, 2, axis=-1)
    return (jax.nn.silu(gate) * up) @ w_down


def hgrn_chunk_scan(q, k, v, log_f):
    b_, h_, l_, dk = q.shape
    dv = v.shape[-1]
    n = l_ // CHUNK
    q = q.reshape(b_, h_, n, CHUNK, dk)
    k = k.reshape(b_, h_, n, CHUNK, dk)
    log_f = log_f.reshape(b_, h_, n, CHUNK, dk)
    v = v.reshape(b_, h_, n, CHUNK, dv)
    cum = jnp.cumsum(log_f, axis=3)
    last = cum[:, :, :, -1:, :]
    q_dec = q * jnp.exp(cum)
    k_dec = k * jnp.exp(-cum)
    k_tail = k * jnp.exp(last - cum)
    lower = jnp.tril(jnp.ones((CHUNK, CHUNK), dtype=bool))
    scores = jnp.einsum('bhnck,bhnsk->bhncs', q_dec, k_dec)
    scores = jnp.where(lower, scores, 0.0)
    o_intra = jnp.einsum('bhncs,bhnsv->bhncv', scores, v)
    kv_chunk = jnp.einsum('bhnsk,bhnsv->bhnkv', k_tail, v)
    chunk_decay = jnp.exp(last[:, :, :, 0, :])

    def step(state, inp):
        kv_n, dec_n = inp
        return dec_n[..., None] * state + kv_n, state

    init = jnp.zeros((b_, h_, dk, dv), kv_chunk.dtype)
    _, prev = lax.scan(step, init, (jnp.moveaxis(kv_chunk, 2, 0), jnp.moveaxis(chunk_decay, 2, 0)))
    prev = jnp.moveaxis(prev, 0, 2)
    o_inter = jnp.einsum('bhnck,bhnkv->bhncv', q_dec, prev)
    return (o_intra + o_inter).reshape(b_, h_, l_, dv)


def hgrn2_mixer(q, i, f_fwd_logit, f_bwd_logit, g, lb_fwd, lb_bwd, out_gain):
    b_, l_, _ = q.shape

    def heads(t):
        return t.astype(jnp.float32).reshape(b_, l_, HGRN_HEADS, HGRN_HEAD_DIM).transpose(0, 2, 1, 3)

    qh, vh = heads(q), heads(i)

    def direction(f_logit, lb, flip):
        lb = lb.astype(jnp.float32).reshape(HGRN_HEADS, 1, HGRN_HEAD_DIM)
        f = lb + (1.0 - lb) * jax.nn.sigmoid(heads(f_logit))
        log_f, k = jnp.log(f), 1.0 - f
        qq, vv = qh, vh
        if flip:
            qq, vv, k, log_f = (jnp.flip(t, axis=2) for t in (qq, vv, k, log_f))
        o = hgrn_chunk_scan(qq, k, vv, log_f)
        return jnp.flip(o, axis=2) if flip else o

    o = direction(f_fwd_logit, lb_fwd, False) + direction(f_bwd_logit, lb_bwd, True)
    o = o.transpose(0, 2, 1, 3)
    o = o * lax.rsqrt(jnp.mean(o * o, axis=-1, keepdims=True) + EPS)
    o = o * out_gain.astype(jnp.float32).reshape(HGRN_HEADS, HGRN_HEAD_DIM)
    o = o.reshape(b_, l_, HGRN_WIDTH) * jax.nn.silu(g.astype(jnp.float32))
    return o.astype(q.dtype)


def t5_buckets(rel):
    nb = REL_BUCKETS // 2
    max_exact = nb // 2
    bucket = (rel > 0).astype(np.int32) * nb
    n = np.abs(rel)
    large = max_exact + (np.log(np.maximum(n, 1) / max_exact) / np.log(REL_MAX_DIST / max_exact)
                         * (nb - max_exact)).astype(np.int32)
    large = np.minimum(large, nb - 1)
    return bucket + np.where(n < max_exact, n, large).astype(np.int32)


def window_attention(q, k, v, sink, rel_table):
    b_, l_, _ = q.shape
    nb = l_ // ATTN_BLOCK
    grp = ATTN_HEADS // ATTN_KV_HEADS
    qb = q.reshape(b_, nb, ATTN_BLOCK, ATTN_KV_HEADS, grp, ATTN_HEAD_DIM)

    def band(t):
        tp = jnp.pad(t, ((0, 0), (WINDOW, WINDOW), (0, 0)))
        tp = tp.reshape(b_, nb + 2, ATTN_BLOCK, ATTN_KV_HEADS, ATTN_HEAD_DIM)
        return jnp.concatenate([tp[:, :-2], tp[:, 1:-1], tp[:, 2:]], axis=2)

    kb, vb = band(k), band(v)
    scores = jnp.einsum('bncxgd,bnsxd->bxgncs', qb, kb).astype(jnp.float32) / math.sqrt(ATTN_HEAD_DIM)
    c = np.arange(ATTN_BLOCK)[:, None]
    s = np.arange(KEY_SPAN)[None, :]
    rel = s - WINDOW - c
    bias = rel_table.astype(jnp.float32)[t5_buckets(rel)]
    bias = jnp.transpose(bias, (2, 0, 1)).reshape(ATTN_KV_HEADS, grp, 1, ATTN_BLOCK, KEY_SPAN)
    key_pos = np.arange(nb)[:, None, None] * ATTN_BLOCK - WINDOW + s[None]
    valid = (np.abs(rel)[None] <= WINDOW) & (key_pos >= 0) & (key_pos < l_)
    scores = jnp.where(valid, scores + bias, NEG_INF)
    sink_col = jnp.broadcast_to(sink.astype(jnp.float32).reshape(1, ATTN_KV_HEADS, grp, 1, 1, 1),
                                scores.shape[:-1] + (1,))
    probs = jax.nn.softmax(jnp.concatenate([scores, sink_col], axis=-1), axis=-1)[..., :KEY_SPAN]
    out = jnp.einsum('bxgncs,bnsxd->bncxgd', probs.astype(v.dtype), vb)
    return out.reshape(b_, l_, ATTN_WIDTH)


def _fwd_setup_inputs(seed: int = 0) -> dict:
    key = jax.random.key(seed)
    ks = jax.random.split(key, 20)
    f32 = jnp.float32

    def w(k, shape, fan_in):
        return jax.random.normal(k, shape, f32) * fan_in ** -0.5

    def gain(k, shape):
        return 1.0 + 0.02 * jax.random.normal(k, shape, f32)

    return {
        "x": jax.random.normal(ks[0], (BATCH, SEQ, D_MODEL), f32),
        "pre_norm_ffn1": gain(ks[1], (DEPTH, D_MODEL)),
        "post_norm_ffn1": gain(ks[2], (DEPTH, D_MODEL)),
        "w_ffn1_gate_up": w(ks[3], (DEPTH, D_MODEL, 2 * D_FF), D_MODEL),
        "w_ffn1_down": w(ks[4], (DEPTH, D_FF, D_MODEL), D_FF),
        "pre_norm_mix": gain(ks[5], (DEPTH, D_MODEL)),
        "post_norm_mix": gain(ks[6], (DEPTH, D_MODEL)),
        "w_mix_in": w(ks[7], (DEPTH, D_MODEL, IN_COLS), D_MODEL),
        "hgrn_lower_bounds_fwd": 0.1 * jax.random.normal(ks[8], (DEPTH + 1, HGRN_WIDTH), f32),
        "hgrn_lower_bounds_bwd": 0.1 * jax.random.normal(ks[9], (DEPTH + 1, HGRN_WIDTH), f32),
        "hgrn_out_norm": gain(ks[10], (DEPTH, HGRN_WIDTH)),
        "attn_sink": 0.5 * jax.random.normal(ks[11], (DEPTH, ATTN_HEADS), f32),
        "w_mix_out": w(ks[12], (DEPTH, HGRN_WIDTH + ATTN_WIDTH, D_MODEL), HGRN_WIDTH + ATTN_WIDTH),
        "pre_norm_ffn2": gain(ks[13], (DEPTH, D_MODEL)),
        "post_norm_ffn2": gain(ks[14], (DEPTH, D_MODEL)),
        "w_ffn2_gate_up": w(ks[15], (DEPTH, D_MODEL, 2 * D_FF), D_MODEL),
        "w_ffn2_down": w(ks[16], (DEPTH, D_FF, D_MODEL), D_FF),
        "rel_bias_table": 0.5 * jax.random.normal(ks[17], (REL_BUCKETS, ATTN_HEADS), f32),
    }


def _fwd_reference(x, pre_norm_ffn1, post_norm_ffn1, w_ffn1_gate_up, w_ffn1_down, pre_norm_mix,
              post_norm_mix, w_mix_in, hgrn_lower_bounds_fwd, hgrn_lower_bounds_bwd, hgrn_out_norm,
              attn_sink, w_mix_out, pre_norm_ffn2, post_norm_ffn2, w_ffn2_gate_up, w_ffn2_down,
              rel_bias_table):
    lb_fwd_all = jnp.cumsum(jax.nn.softmax(hgrn_lower_bounds_fwd.astype(jnp.float32), axis=0), axis=0)
    lb_bwd_all = jnp.cumsum(jax.nn.softmax(hgrn_lower_bounds_bwd.astype(jnp.float32), axis=0), axis=0)
    split_at = np.cumsum(IN_SPLITS)[:-1].tolist()
    for layer in range(DEPTH):
        ff = swiglu(rms_norm(x, pre_norm_ffn1[layer]), w_ffn1_gate_up[layer], w_ffn1_down[layer])
        x = x + 0.5 * rms_norm(ff, post_norm_ffn1[layer])
        h = rms_norm(x, pre_norm_mix[layer])
        q_h, i_h, f_fwd, f_bwd, g_h, q_a, k_a, v_a = jnp.split(h @ w_mix_in[layer], split_at, axis=-1)
        y_h = hgrn2_mixer(q_h, i_h, f_fwd, f_bwd, g_h, lb_fwd_all[layer], lb_bwd_all[layer],
                          hgrn_out_norm[layer])
        y_a = window_attention(q_a, k_a, v_a, attn_sink[layer], rel_bias_table)
        mixed = jnp.concatenate([y_h.astype(x.dtype), y_a.astype(x.dtype)], axis=-1) @ w_mix_out[layer]
        x = x + rms_norm(mixed, post_norm_mix[layer])
        ff = swiglu(rms_norm(x, pre_norm_ffn2[layer]), w_ffn2_gate_up[layer], w_ffn2_down[layer])
        x = x + 0.5 * rms_norm(ff, post_norm_ffn2[layer])
    return x


import jax as _jax
import jax.numpy as _jnp

TWIN_FORMAT = 'train_step'
FWD_PARAMS = ['x', 'pre_norm_ffn1', 'post_norm_ffn1', 'w_ffn1_gate_up', 'w_ffn1_down', 'pre_norm_mix', 'post_norm_mix', 'w_mix_in', 'hgrn_lower_bounds_fwd', 'hgrn_lower_bounds_bwd', 'hgrn_out_norm', 'attn_sink', 'w_mix_out', 'pre_norm_ffn2', 'post_norm_ffn2', 'w_ffn2_gate_up', 'w_ffn2_down', 'rel_bias_table']
TWIN_WEIGHTS = ['pre_norm_ffn1', 'post_norm_ffn1', 'w_ffn1_gate_up', 'w_ffn1_down', 'pre_norm_mix', 'post_norm_mix', 'w_mix_in', 'hgrn_lower_bounds_fwd', 'hgrn_lower_bounds_bwd', 'hgrn_out_norm', 'attn_sink', 'w_mix_out', 'pre_norm_ffn2', 'post_norm_ffn2', 'w_ffn2_gate_up', 'w_ffn2_down', 'rel_bias_table']
TWIN_DIFF_INPUT = 'x'
TWIN_INPUTS = ['x', 'pre_norm_ffn1', 'post_norm_ffn1', 'w_ffn1_gate_up', 'w_ffn1_down', 'pre_norm_mix', 'post_norm_mix', 'w_mix_in', 'hgrn_lower_bounds_fwd', 'hgrn_lower_bounds_bwd', 'hgrn_out_norm', 'attn_sink', 'w_mix_out', 'pre_norm_ffn2', 'post_norm_ffn2', 'w_ffn2_gate_up', 'w_ffn2_down', 'rel_bias_table', 'loss_target', 'm_pre_norm_ffn1', 'm_post_norm_ffn1', 'm_w_ffn1_gate_up', 'm_w_ffn1_down', 'm_pre_norm_mix', 'm_post_norm_mix', 'm_w_mix_in', 'm_hgrn_lower_bounds_fwd', 'm_hgrn_lower_bounds_bwd', 'm_hgrn_out_norm', 'm_attn_sink', 'm_w_mix_out', 'm_pre_norm_ffn2', 'm_post_norm_ffn2', 'm_w_ffn2_gate_up', 'm_w_ffn2_down', 'm_rel_bias_table', 'v_pre_norm_ffn1', 'v_post_norm_ffn1', 'v_w_ffn1_gate_up', 'v_w_ffn1_down', 'v_pre_norm_mix', 'v_post_norm_mix', 'v_w_mix_in', 'v_hgrn_lower_bounds_fwd', 'v_hgrn_lower_bounds_bwd', 'v_hgrn_out_norm', 'v_attn_sink', 'v_w_mix_out', 'v_pre_norm_ffn2', 'v_post_norm_ffn2', 'v_w_ffn2_gate_up', 'v_w_ffn2_down', 'v_rel_bias_table']
TWIN_OUTPUTS = ['loss', 'grad_x', 'grad_pre_norm_ffn1', 'grad_post_norm_ffn1', 'grad_w_ffn1_gate_up', 'grad_w_ffn1_down', 'grad_pre_norm_mix', 'grad_post_norm_mix', 'grad_w_mix_in', 'grad_hgrn_lower_bounds_fwd', 'grad_hgrn_lower_bounds_bwd', 'grad_hgrn_out_norm', 'grad_attn_sink', 'grad_w_mix_out', 'grad_pre_norm_ffn2', 'grad_post_norm_ffn2', 'grad_w_ffn2_gate_up', 'grad_w_ffn2_down', 'grad_rel_bias_table', 'delta_pre_norm_ffn1', 'delta_post_norm_ffn1', 'delta_w_ffn1_gate_up', 'delta_w_ffn1_down', 'delta_pre_norm_mix', 'delta_post_norm_mix', 'delta_w_mix_in', 'delta_hgrn_lower_bounds_fwd', 'delta_hgrn_lower_bounds_bwd', 'delta_hgrn_out_norm', 'delta_attn_sink', 'delta_w_mix_out', 'delta_pre_norm_ffn2', 'delta_post_norm_ffn2', 'delta_w_ffn2_gate_up', 'delta_w_ffn2_down', 'delta_rel_bias_table', 'new_m_pre_norm_ffn1', 'new_m_post_norm_ffn1', 'new_m_w_ffn1_gate_up', 'new_m_w_ffn1_down', 'new_m_pre_norm_mix', 'new_m_post_norm_mix', 'new_m_w_mix_in', 'new_m_hgrn_lower_bounds_fwd', 'new_m_hgrn_lower_bounds_bwd', 'new_m_hgrn_out_norm', 'new_m_attn_sink', 'new_m_w_mix_out', 'new_m_pre_norm_ffn2', 'new_m_post_norm_ffn2', 'new_m_w_ffn2_gate_up', 'new_m_w_ffn2_down', 'new_m_rel_bias_table', 'new_v_pre_norm_ffn1', 'new_v_post_norm_ffn1', 'new_v_w_ffn1_gate_up', 'new_v_w_ffn1_down', 'new_v_pre_norm_mix', 'new_v_post_norm_mix', 'new_v_w_mix_in', 'new_v_hgrn_lower_bounds_fwd', 'new_v_hgrn_lower_bounds_bwd', 'new_v_hgrn_out_norm', 'new_v_attn_sink', 'new_v_w_mix_out', 'new_v_pre_norm_ffn2', 'new_v_post_norm_ffn2', 'new_v_w_ffn2_gate_up', 'new_v_w_ffn2_down', 'new_v_rel_bias_table']
TWIN_LEAF_KINDS = {'loss': 'loss', 'grad_x': 'grad_x', 'grad_pre_norm_ffn1': 'grad_w', 'grad_post_norm_ffn1': 'grad_w', 'grad_w_ffn1_gate_up': 'grad_w', 'grad_w_ffn1_down': 'grad_w', 'grad_pre_norm_mix': 'grad_w', 'grad_post_norm_mix': 'grad_w', 'grad_w_mix_in': 'grad_w', 'grad_hgrn_lower_bounds_fwd': 'grad_w', 'grad_hgrn_lower_bounds_bwd': 'grad_w', 'grad_hgrn_out_norm': 'grad_w', 'grad_attn_sink': 'grad_w', 'grad_w_mix_out': 'grad_w', 'grad_pre_norm_ffn2': 'grad_w', 'grad_post_norm_ffn2': 'grad_w', 'grad_w_ffn2_gate_up': 'grad_w', 'grad_w_ffn2_down': 'grad_w', 'grad_rel_bias_table': 'grad_w', 'delta_pre_norm_ffn1': 'delta_w', 'delta_post_norm_ffn1': 'delta_w', 'delta_w_ffn1_gate_up': 'delta_w', 'delta_w_ffn1_down': 'delta_w', 'delta_pre_norm_mix': 'delta_w', 'delta_post_norm_mix': 'delta_w', 'delta_w_mix_in': 'delta_w', 'delta_hgrn_lower_bounds_fwd': 'delta_w', 'delta_hgrn_lower_bounds_bwd': 'delta_w', 'delta_hgrn_out_norm': 'delta_w', 'delta_attn_sink': 'delta_w', 'delta_w_mix_out': 'delta_w', 'delta_pre_norm_ffn2': 'delta_w', 'delta_post_norm_ffn2': 'delta_w', 'delta_w_ffn2_gate_up': 'delta_w', 'delta_w_ffn2_down': 'delta_w', 'delta_rel_bias_table': 'delta_w', 'new_m_pre_norm_ffn1': 'new_m', 'new_m_post_norm_ffn1': 'new_m', 'new_m_w_ffn1_gate_up': 'new_m', 'new_m_w_ffn1_down': 'new_m', 'new_m_pre_norm_mix': 'new_m', 'new_m_post_norm_mix': 'new_m', 'new_m_w_mix_in': 'new_m', 'new_m_hgrn_lower_bounds_fwd': 'new_m', 'new_m_hgrn_lower_bounds_bwd': 'new_m', 'new_m_hgrn_out_norm': 'new_m', 'new_m_attn_sink': 'new_m', 'new_m_w_mix_out': 'new_m', 'new_m_pre_norm_ffn2': 'new_m', 'new_m_post_norm_ffn2': 'new_m', 'new_m_w_ffn2_gate_up': 'new_m', 'new_m_w_ffn2_down': 'new_m', 'new_m_rel_bias_table': 'new_m', 'new_v_pre_norm_ffn1': 'new_v', 'new_v_post_norm_ffn1': 'new_v', 'new_v_w_ffn1_gate_up': 'new_v', 'new_v_w_ffn1_down': 'new_v', 'new_v_pre_norm_mix': 'new_v', 'new_v_post_norm_mix': 'new_v', 'new_v_w_mix_in': 'new_v', 'new_v_hgrn_lower_bounds_fwd': 'new_v', 'new_v_hgrn_lower_bounds_bwd': 'new_v', 'new_v_hgrn_out_norm': 'new_v', 'new_v_attn_sink': 'new_v', 'new_v_w_mix_out': 'new_v', 'new_v_pre_norm_ffn2': 'new_v', 'new_v_post_norm_ffn2': 'new_v', 'new_v_w_ffn2_gate_up': 'new_v', 'new_v_w_ffn2_down': 'new_v', 'new_v_rel_bias_table': 'new_v'}


def _forward(args):
    return _fwd_reference(*[args[k] for k in FWD_PARAMS])


def _output_shape():
    out = _jax.eval_shape(lambda: _forward(_fwd_setup_inputs(0)))
    return out.shape, out.dtype

N_MICROBATCH = 1
ADAM_LR = 0.001
ADAM_B1 = 0.9
ADAM_B2 = 0.999
ADAM_EPS = 1e-08
ADAM_WD = 0.01
ADAM_STEP = 10
PER_EXAMPLE_BATCH_AXIS = {'x': 0, 'loss_target': 0}
SHARED_INPUTS = []
_WEIGHT_DTYPES = {'pre_norm_ffn1': _jnp.float32, 'post_norm_ffn1': _jnp.float32, 'w_ffn1_gate_up': _jnp.float32, 'w_ffn1_down': _jnp.float32, 'pre_norm_mix': _jnp.float32, 'post_norm_mix': _jnp.float32, 'w_mix_in': _jnp.float32, 'hgrn_lower_bounds_fwd': _jnp.float32, 'hgrn_lower_bounds_bwd': _jnp.float32, 'hgrn_out_norm': _jnp.float32, 'attn_sink': _jnp.float32, 'w_mix_out': _jnp.float32, 'pre_norm_ffn2': _jnp.float32, 'post_norm_ffn2': _jnp.float32, 'w_ffn2_gate_up': _jnp.float32, 'w_ffn2_down': _jnp.float32, 'rel_bias_table': _jnp.float32}
MOMENT_SCALE = {'pre_norm_ffn1': 1.929873e-01, 'post_norm_ffn1': 1.966411e+00, 'w_ffn1_gate_up': 7.904995e-02, 'w_ffn1_down': 1.294537e-01, 'pre_norm_mix': 2.539523e-01, 'post_norm_mix': 7.993613e+00, 'w_mix_in': 1.374948e-01, 'hgrn_lower_bounds_fwd': 7.308429e-02, 'hgrn_lower_bounds_bwd': 6.977601e-02, 'hgrn_out_norm': 1.661645e-01, 'attn_sink': 2.226677e-03, 'w_mix_out': 1.162372e-01, 'pre_norm_ffn2': 8.833269e-02, 'post_norm_ffn2': 1.993508e+00, 'w_ffn2_gate_up': 3.793726e-02, 'w_ffn2_down': 6.744554e-02, 'rel_bias_table': 7.119197e-02}


def _to_microbatches(a, axis):
    t = _jnp.moveaxis(a, axis, 0)
    t = t.reshape((N_MICROBATCH, t.shape[0] // N_MICROBATCH) + t.shape[1:])
    return _jnp.moveaxis(t, 1, axis + 1)


def setup_inputs(seed: int = 0) -> dict:
    inp = _fwd_setup_inputs(seed)
    key = _jax.random.fold_in(_jax.random.key(seed), 7919)
    shape, _ = _output_shape()
    out = dict(inp)
    out["loss_target"] = _jax.random.normal(_jax.random.fold_in(key, 0), shape, _jnp.float32)
    for i, name in enumerate(TWIN_WEIGHTS):
        w = inp[name].astype(_jnp.float32)
        if MOMENT_SCALE is None:
            s = _jnp.sqrt(_jnp.mean(_jnp.square(w)) + 1e-30)
        else:
            s = MOMENT_SCALE[name]
        km, kv = _jax.random.split(_jax.random.fold_in(key, i + 1))
        out[name] = w
        out["m_" + name] = s * _jax.random.normal(km, w.shape, _jnp.float32)
        out["v_" + name] = (s * s) * _jax.random.uniform(kv, w.shape, _jnp.float32, 0.5, 1.5)
    if N_MICROBATCH > 1:
        for name, axis in PER_EXAMPLE_BATCH_AXIS.items():
            out[name] = _to_microbatches(out[name], axis)
    return {'x': out['x'], 'pre_norm_ffn1': out['pre_norm_ffn1'], 'post_norm_ffn1': out['post_norm_ffn1'], 'w_ffn1_gate_up': out['w_ffn1_gate_up'], 'w_ffn1_down': out['w_ffn1_down'], 'pre_norm_mix': out['pre_norm_mix'], 'post_norm_mix': out['post_norm_mix'], 'w_mix_in': out['w_mix_in'], 'hgrn_lower_bounds_fwd': out['hgrn_lower_bounds_fwd'], 'hgrn_lower_bounds_bwd': out['hgrn_lower_bounds_bwd'], 'hgrn_out_norm': out['hgrn_out_norm'], 'attn_sink': out['attn_sink'], 'w_mix_out': out['w_mix_out'], 'pre_norm_ffn2': out['pre_norm_ffn2'], 'post_norm_ffn2': out['post_norm_ffn2'], 'w_ffn2_gate_up': out['w_ffn2_gate_up'], 'w_ffn2_down': out['w_ffn2_down'], 'rel_bias_table': out['rel_bias_table'], 'loss_target': out['loss_target'], 'm_pre_norm_ffn1': out['m_pre_norm_ffn1'], 'm_post_norm_ffn1': out['m_post_norm_ffn1'], 'm_w_ffn1_gate_up': out['m_w_ffn1_gate_up'], 'm_w_ffn1_down': out['m_w_ffn1_down'], 'm_pre_norm_mix': out['m_pre_norm_mix'], 'm_post_norm_mix': out['m_post_norm_mix'], 'm_w_mix_in': out['m_w_mix_in'], 'm_hgrn_lower_bounds_fwd': out['m_hgrn_lower_bounds_fwd'], 'm_hgrn_lower_bounds_bwd': out['m_hgrn_lower_bounds_bwd'], 'm_hgrn_out_norm': out['m_hgrn_out_norm'], 'm_attn_sink': out['m_attn_sink'], 'm_w_mix_out': out['m_w_mix_out'], 'm_pre_norm_ffn2': out['m_pre_norm_ffn2'], 'm_post_norm_ffn2': out['m_post_norm_ffn2'], 'm_w_ffn2_gate_up': out['m_w_ffn2_gate_up'], 'm_w_ffn2_down': out['m_w_ffn2_down'], 'm_rel_bias_table': out['m_rel_bias_table'], 'v_pre_norm_ffn1': out['v_pre_norm_ffn1'], 'v_post_norm_ffn1': out['v_post_norm_ffn1'], 'v_w_ffn1_gate_up': out['v_w_ffn1_gate_up'], 'v_w_ffn1_down': out['v_w_ffn1_down'], 'v_pre_norm_mix': out['v_pre_norm_mix'], 'v_post_norm_mix': out['v_post_norm_mix'], 'v_w_mix_in': out['v_w_mix_in'], 'v_hgrn_lower_bounds_fwd': out['v_hgrn_lower_bounds_fwd'], 'v_hgrn_lower_bounds_bwd': out['v_hgrn_lower_bounds_bwd'], 'v_hgrn_out_norm': out['v_hgrn_out_norm'], 'v_attn_sink': out['v_attn_sink'], 'v_w_mix_out': out['v_w_mix_out'], 'v_pre_norm_ffn2': out['v_pre_norm_ffn2'], 'v_post_norm_ffn2': out['v_post_norm_ffn2'], 'v_w_ffn2_gate_up': out['v_w_ffn2_gate_up'], 'v_w_ffn2_down': out['v_w_ffn2_down'], 'v_rel_bias_table': out['v_rel_bias_table']}


def _loss(weights, diff, rest, loss_target):
    with _jax.named_scope("forward"):
        args = {**rest, TWIN_DIFF_INPUT: diff, **{k: w.astype(_WEIGHT_DTYPES[k]) for k, w in weights.items()}}
        y = _forward(args)
    with _jax.named_scope("loss_head"):
        err = _jnp.square(y.astype(_jnp.float32) - loss_target)
        return 0.5 * _jnp.sum(_jnp.mean(err, axis=-1)) if err.ndim else 0.5 * err


def _adamw(w, g, m, v):
    m = ADAM_B1 * m + (1.0 - ADAM_B1) * g
    v = ADAM_B2 * v + (1.0 - ADAM_B2) * _jnp.square(g)
    m_hat = m / (1.0 - ADAM_B1 ** ADAM_STEP)
    v_hat = v / (1.0 - ADAM_B2 ** ADAM_STEP)
    delta = -ADAM_LR * (m_hat / (_jnp.sqrt(v_hat) + ADAM_EPS) + ADAM_WD * w)
    return delta, m, v


def reference(x, pre_norm_ffn1, post_norm_ffn1, w_ffn1_gate_up, w_ffn1_down, pre_norm_mix, post_norm_mix, w_mix_in, hgrn_lower_bounds_fwd, hgrn_lower_bounds_bwd, hgrn_out_norm, attn_sink, w_mix_out, pre_norm_ffn2, post_norm_ffn2, w_ffn2_gate_up, w_ffn2_down, rel_bias_table, loss_target, m_pre_norm_ffn1, m_post_norm_ffn1, m_w_ffn1_gate_up, m_w_ffn1_down, m_pre_norm_mix, m_post_norm_mix, m_w_mix_in, m_hgrn_lower_bounds_fwd, m_hgrn_lower_bounds_bwd, m_hgrn_out_norm, m_attn_sink, m_w_mix_out, m_pre_norm_ffn2, m_post_norm_ffn2, m_w_ffn2_gate_up, m_w_ffn2_down, m_rel_bias_table, v_pre_norm_ffn1, v_post_norm_ffn1, v_w_ffn1_gate_up, v_w_ffn1_down, v_pre_norm_mix, v_post_norm_mix, v_w_mix_in, v_hgrn_lower_bounds_fwd, v_hgrn_lower_bounds_bwd, v_hgrn_out_norm, v_attn_sink, v_w_mix_out, v_pre_norm_ffn2, v_post_norm_ffn2, v_w_ffn2_gate_up, v_w_ffn2_down, v_rel_bias_table):
    given = dict(x=x, pre_norm_ffn1=pre_norm_ffn1, post_norm_ffn1=post_norm_ffn1, w_ffn1_gate_up=w_ffn1_gate_up, w_ffn1_down=w_ffn1_down, pre_norm_mix=pre_norm_mix, post_norm_mix=post_norm_mix, w_mix_in=w_mix_in, hgrn_lower_bounds_fwd=hgrn_lower_bounds_fwd, hgrn_lower_bounds_bwd=hgrn_lower_bounds_bwd, hgrn_out_norm=hgrn_out_norm, attn_sink=attn_sink, w_mix_out=w_mix_out, pre_norm_ffn2=pre_norm_ffn2, post_norm_ffn2=post_norm_ffn2, w_ffn2_gate_up=w_ffn2_gate_up, w_ffn2_down=w_ffn2_down, rel_bias_table=rel_bias_table, loss_target=loss_target, m_pre_norm_ffn1=m_pre_norm_ffn1, m_post_norm_ffn1=m_post_norm_ffn1, m_w_ffn1_gate_up=m_w_ffn1_gate_up, m_w_ffn1_down=m_w_ffn1_down, m_pre_norm_mix=m_pre_norm_mix, m_post_norm_mix=m_post_norm_mix, m_w_mix_in=m_w_mix_in, m_hgrn_lower_bounds_fwd=m_hgrn_lower_bounds_fwd, m_hgrn_lower_bounds_bwd=m_hgrn_lower_bounds_bwd, m_hgrn_out_norm=m_hgrn_out_norm, m_attn_sink=m_attn_sink, m_w_mix_out=m_w_mix_out, m_pre_norm_ffn2=m_pre_norm_ffn2, m_post_norm_ffn2=m_post_norm_ffn2, m_w_ffn2_gate_up=m_w_ffn2_gate_up, m_w_ffn2_down=m_w_ffn2_down, m_rel_bias_table=m_rel_bias_table, v_pre_norm_ffn1=v_pre_norm_ffn1, v_post_norm_ffn1=v_post_norm_ffn1, v_w_ffn1_gate_up=v_w_ffn1_gate_up, v_w_ffn1_down=v_w_ffn1_down, v_pre_norm_mix=v_pre_norm_mix, v_post_norm_mix=v_post_norm_mix, v_w_mix_in=v_w_mix_in, v_hgrn_lower_bounds_fwd=v_hgrn_lower_bounds_fwd, v_hgrn_lower_bounds_bwd=v_hgrn_lower_bounds_bwd, v_hgrn_out_norm=v_hgrn_out_norm, v_attn_sink=v_attn_sink, v_w_mix_out=v_w_mix_out, v_pre_norm_ffn2=v_pre_norm_ffn2, v_post_norm_ffn2=v_post_norm_ffn2, v_w_ffn2_gate_up=v_w_ffn2_gate_up, v_w_ffn2_down=v_w_ffn2_down, v_rel_bias_table=v_rel_bias_table)
    weights = {n: given[n] for n in TWIN_WEIGHTS}
    shared = {n: given[n] for n in SHARED_INPUTS}
    per_example = {n: given[n] for n in ['x']}
    grad_fn = _jax.value_and_grad(_loss, argnums=(0, 1))

    def one_microbatch(ex, loss_target):
        ex = dict(ex)
        diff = ex.pop(TWIN_DIFF_INPUT)
        return grad_fn(weights, diff, {**shared, **ex}, loss_target)

    if N_MICROBATCH == 1:
        loss, (grad_w, grad_x) = one_microbatch(per_example, given["loss_target"])
    else:
        def body(carry, xs):
            loss_sum, grad_sum = carry
            l_k, (gw_k, gx_k) = one_microbatch(xs[0], xs[1])
            with _jax.named_scope("update"):
                return (loss_sum + l_k, _jax.tree.map(_jnp.add, grad_sum, gw_k)), gx_k

        init = (_jnp.zeros((), _jnp.float32), _jax.tree.map(_jnp.zeros_like, weights))
        (loss, grad_w), grad_x = _jax.lax.scan(body, init, (per_example, given["loss_target"]))
    with _jax.named_scope("update"):
        delta_w, new_m, new_v = {}, {}, {}
        for n in TWIN_WEIGHTS:
            delta_w[n], new_m[n], new_v[n] = _adamw(weights[n], grad_w[n], given["m_" + n], given["v_" + n])
    return (loss, grad_x, *[grad_w[n] for n in TWIN_WEIGHTS], *[delta_w[n] for n in TWIN_WEIGHTS],
            *[new_m[n] for n in TWIN_WEIGHTS], *[new_v[n] for n in TWIN_WEIGHTS])
```

```python
import functools
import math

import numpy as np
import jax
import jax.numpy as jnp
from jax import lax
from jax.experimental import pallas as pl
from jax.experimental.pallas import tpu as pltpu

F32 = jnp.float32
BF16 = jnp.bfloat16
HIGHEST = lax.Precision.HIGHEST
MESH = pl.DeviceIdType.MESH

N_DEV = 8
EPS = 1e-6
NEG_INF = -1e30
HEAD = 128
CHUNK = 64
WINDOW = 128
KEY_SPAN = 3 * WINDOW
KV_HEADS = 2
REL_BUCKETS = 32
REL_MAX_DIST = 128
ADAM_LR, ADAM_B1, ADAM_B2, ADAM_EPS, ADAM_WD, ADAM_STEP = 0.001, 0.9, 0.999, 1e-08, 0.01, 10
LANES = 128
VMEM_LIMIT = 56 * 1024 * 1024


def _cparams(*sem):
    return pltpu.CompilerParams(dimension_semantics=sem if sem else None, vmem_limit_bytes=VMEM_LIMIT)


def _dot(a, b):
    return jnp.dot(a, b, preferred_element_type=F32)


def _dot_nt(a, b):
    return lax.dot_general(a, b, (((1,), (1,)), ((), ())), preferred_element_type=F32)


def _dot_tn(a, b):
    return lax.dot_general(a, b, (((0,), (0,)), ((), ())), preferred_element_type=F32)


def _tile(dim, target):
    for c in (target, 1024, 512, 256, 128):
        if c <= target and dim % c == 0:
            return c
    return dim


def _matmul(name, a, b, *, mode, out_dtype, stack=False, tm=1024, tn=1024, tk=512):
    if mode == "nn":
        m, kd = a.shape
        if stack:
            s, _, n1 = b.shape
            n, tn = s * n1, n1
        else:
            n = b.shape[1]
    elif mode == "nt":
        m = a.shape[0]
        if stack:
            s, n, n1 = b.shape
            kd, tk = s * n1, n1
        else:
            n, kd = b.shape
    else:
        kd, m = a.shape
        if stack:
            n = b.shape[1]
            s, n1 = N_DEV, n // N_DEV
            tn = n1
        else:
            n = b.shape[1]
    tm = _tile(m, tm)
    if not (stack and mode in ("nn", "tn")):
        tn = _tile(n, tn)
    if not (stack and mode == "nt"):
        tk = _tile(kd, tk)
    nk = kd // tk
    grid = (m // tm, n // tn, nk)

    if mode == "nn":
        a_spec = pl.BlockSpec((tm, tk), lambda i, j, k: (i, k))
        if stack:
            b_spec = pl.BlockSpec((None, tk, tn), lambda i, j, k: (j, k, 0))
        else:
            b_spec = pl.BlockSpec((tk, tn), lambda i, j, k: (k, j))
        dot = _dot
    elif mode == "nt":
        a_spec = pl.BlockSpec((tm, tk), lambda i, j, k: (i, k))
        if stack:
            b_spec = pl.BlockSpec((None, tn, tk), lambda i, j, k: (k, j, 0))
        else:
            b_spec = pl.BlockSpec((tn, tk), lambda i, j, k: (j, k))
        dot = _dot_nt
    else:
        a_spec = pl.BlockSpec((tk, tm), lambda i, j, k: (k, i))
        b_spec = pl.BlockSpec((tk, tn), lambda i, j, k: (k, j))
        dot = _dot_tn
    if stack and mode == "tn":
        out_shape = jax.ShapeDtypeStruct((s, m, n1), out_dtype)
        o_spec = pl.BlockSpec((None, tm, tn), lambda i, j, k: (j, i, 0))
    else:
        out_shape = jax.ShapeDtypeStruct((m, n), out_dtype)
        o_spec = pl.BlockSpec((tm, tn), lambda i, j, k: (i, j))

    def body(a_ref, b_ref, o_ref, acc_ref):
        k = pl.program_id(2)

        @pl.when(k == 0)
        def _():
            acc_ref[...] = jnp.zeros_like(acc_ref)

        acc_ref[...] += dot(a_ref[...], b_ref[...])

        @pl.when(k == nk - 1)
        def _():
            o_ref[...] = acc_ref[...].astype(o_ref.dtype)

    return pl.pallas_call(
        body, name=name, grid=grid, in_specs=[a_spec, b_spec], out_specs=o_spec, out_shape=out_shape,
        scratch_shapes=[pltpu.VMEM((tm, tn), F32)],
        compiler_params=_cparams("parallel", "parallel", "arbitrary"),
    )(a, b)


ROWS = 256


def _rstd(xf):
    return lax.rsqrt(jnp.mean(xf * xf, axis=-1, keepdims=True) + EPS)


def _row_spec(t, d):
    return pl.BlockSpec((min(ROWS, t), d), lambda i: (i, 0))


def _vec_spec(d):
    return pl.BlockSpec((1, d), lambda i: (0, 0))


def _pre_norm(name, x, gain):
    t, d = x.shape

    def body(x_ref, g_ref, n_ref):
        xf = x_ref[...]
        n_ref[...] = (xf * _rstd(xf) * g_ref[...]).astype(BF16)

    return pl.pallas_call(
        body, name=name, grid=(t // min(ROWS, t),), in_specs=[_row_spec(t, d), _vec_spec(d)],
        out_specs=_row_spec(t, d), out_shape=jax.ShapeDtypeStruct((t, d), BF16),
        compiler_params=_cparams("parallel"),
    )(x, gain)


def _post_res_pre(name, x, ff, g_post, g_next, scale):
    t, d = x.shape

    def body(x_ref, ff_ref, gp_ref, gn_ref, xo_ref, n_ref):
        ff_ = ff_ref[...]
        xn = x_ref[...] + scale * (ff_ * _rstd(ff_) * gp_ref[...])
        xo_ref[...] = xn
        n_ref[...] = (xn * _rstd(xn) * gn_ref[...]).astype(BF16)

    return pl.pallas_call(
        body, name=name, grid=(t // min(ROWS, t),),
        in_specs=[_row_spec(t, d), _row_spec(t, d), _vec_spec(d), _vec_spec(d)],
        out_specs=[_row_spec(t, d), _row_spec(t, d)],
        out_shape=[jax.ShapeDtypeStruct((t, d), F32), jax.ShapeDtypeStruct((t, d), BF16)],
        compiler_params=_cparams("parallel"),
    )(x, ff, g_post, g_next)


def _post_res_loss(name, x, ff, g_post, target, scale):
    t, d = x.shape

    def body(x_ref, ff_ref, gp_ref, tg_ref, dy_ref, loss_ref):
        ff_ = ff_ref[...]
        err = x_ref[...] + scale * (ff_ * _rstd(ff_) * gp_ref[...]) - tg_ref[...]
        dy_ref[...] = err / d
        part = 0.5 * jnp.sum(jnp.mean(err * err, axis=-1, keepdims=True), axis=0, keepdims=True)

        @pl.when(pl.program_id(0) == 0)
        def _():
            loss_ref[...] = jnp.zeros_like(loss_ref)

        loss_ref[...] += jnp.broadcast_to(part, loss_ref.shape)

    return pl.pallas_call(
        body, name=name, grid=(t // min(ROWS, t),),
        in_specs=[_row_spec(t, d), _row_spec(t, d), _vec_spec(d), _row_spec(t, d)],
        out_specs=[_row_spec(t, d), _vec_spec(LANES)],
        out_shape=[jax.ShapeDtypeStruct((t, d), F32), jax.ShapeDtypeStruct((1, LANES), F32)],
        compiler_params=_cparams("arbitrary"),
    )(x, ff, g_post, target)


def _post_bwd(name, dy, ff, g_post, scale):
    t, d = dy.shape

    def body(dy_ref, ff_ref, gp_ref, dff_ref, dg_ref):
        ff_ = ff_ref[...]
        r = _rstd(ff_)
        xh = ff_ * r
        dyn = scale * dy_ref[...]
        dxh = dyn * gp_ref[...]
        dff_ref[...] = (r * (dxh - xh * jnp.mean(dxh * xh, axis=-1, keepdims=True))).astype(BF16)

        @pl.when(pl.program_id(0) == 0)
        def _():
            dg_ref[...] = jnp.zeros_like(dg_ref)

        dg_ref[...] += jnp.sum(dyn * xh, axis=0, keepdims=True)

    return pl.pallas_call(
        body, name=name, grid=(t // min(ROWS, t),),
        in_specs=[_row_spec(t, d), _row_spec(t, d), _vec_spec(d)],
        out_specs=[_row_spec(t, d), _vec_spec(d)],
        out_shape=[jax.ShapeDtypeStruct((t, d), BF16), jax.ShapeDtypeStruct((1, d), F32)],
        compiler_params=_cparams("arbitrary"),
    )(dy, ff, g_post)


def _pre_bwd(name, dn, x, g_pre, dy):
    t, d = x.shape

    def body(dn_ref, x_ref, g_ref, dy_ref, dx_ref, dg_ref):
        xf = x_ref[...]
        r = _rstd(xf)
        xh = xf * r
        dnf = dn_ref[...].astype(F32)
        dxh = dnf * g_ref[...]
        dx_ref[...] = dy_ref[...] + r * (dxh - xh * jnp.mean(dxh * xh, axis=-1, keepdims=True))

        @pl.when(pl.program_id(0) == 0)
        def _():
            dg_ref[...] = jnp.zeros_like(dg_ref)

        dg_ref[...] += jnp.sum(dnf * xh, axis=0, keepdims=True)

    return pl.pallas_call(
        body, name=name, grid=(t // min(ROWS, t),),
        in_specs=[_row_spec(t, d), _row_spec(t, d), _vec_spec(d), _row_spec(t, d)],
        out_specs=[_row_spec(t, d), _vec_spec(d)],
        out_shape=[jax.ShapeDtypeStruct((t, d), F32), jax.ShapeDtypeStruct((1, d), F32)],
        compiler_params=_cparams("arbitrary"),
    )(dn, x, g_pre, dy)


def _swiglu_fwd(name, gu):
    t, f2 = gu.shape
    f = f2 // 2
    tc = _tile(f, 512)
    nb = f // tc
    spec = lambda off: pl.BlockSpec((min(ROWS, t), tc), lambda i, j: (i, j + off))

    def body(g_ref, u_ref, a_ref):
        g = g_ref[...]
        a_ref[...] = (g * jax.nn.sigmoid(g) * u_ref[...]).astype(BF16)

    return pl.pallas_call(
        body, name=name, grid=(t // min(ROWS, t), nb), in_specs=[spec(0), spec(nb)], out_specs=spec(0),
        out_shape=jax.ShapeDtypeStruct((t, f), BF16), compiler_params=_cparams("parallel", "parallel"),
    )(gu, gu)


def _swiglu_bwd(name, da, gu):
    t, f2 = gu.shape
    f = f2 // 2
    tc = _tile(f, 512)
    nb = f // tc
    spec = lambda off: pl.BlockSpec((min(ROWS, t), tc), lambda i, j: (i, j + off))

    def body(da_ref, g_ref, u_ref, dg_ref, du_ref):
        g = g_ref[...]
        da_ = da_ref[...]
        sg = jax.nn.sigmoid(g)
        dg_ref[...] = (da_ * u_ref[...] * (sg * (1.0 + g * (1.0 - sg)))).astype(BF16)
        du_ref[...] = (da_ * (g * sg)).astype(BF16)

    dg, du = pl.pallas_call(
        body, name=name, grid=(t // min(ROWS, t), nb), in_specs=[spec(0), spec(0), spec(nb)],
        out_specs=[spec(0), spec(0)],
        out_shape=[jax.ShapeDtypeStruct((t, f), BF16), jax.ShapeDtypeStruct((t, f), BF16)],
        compiler_params=_cparams("parallel", "parallel"),
    )(da, gu, gu)
    return jnp.concatenate([dg, du], axis=1)


def _tri_masks():
    row = lax.broadcasted_iota(jnp.int32, (CHUNK, CHUNK), 0)
    col = lax.broadcasted_iota(jnp.int32, (CHUNK, CHUNK), 1)
    return col <= row, col >= row


def _hgrn_chunk(z, lb, q, v, cum_mat, rev):
    sg = jax.nn.sigmoid(z)
    f = lb + (1.0 - lb) * sg
    lf = jnp.log(f)
    k = 1.0 - f
    a = jnp.dot(cum_mat, lf, precision=HIGHEST, preferred_element_type=F32)
    last = jnp.sum(lf, axis=0, keepdims=True)
    e_a = jnp.exp(a)
    e_na = jnp.exp(-a)
    e_t = jnp.exp(last - a)
    return dict(sg=sg, f=f, k=k, last=last, e_a=e_a, e_na=e_na, e_t=e_t,
                qd=q * e_a, kd=k * e_na, kt=k * e_t)


def _hgrn_fwd(p, lb_f, lb_b, gain, n_heads):
    t = p.shape[0]
    w = n_heads * HEAD
    n_chunks = t // CHUNK
    fin_rows = min(256, t)

    def body(q_ref, i_ref, zf_ref, zb_ref, g_ref, lbf_ref, lbb_ref, gain_ref, y_ref, o_ref, st_ref):
        low, up = _tri_masks()
        m_low, m_up = low.astype(F32), up.astype(F32)
        o_ref[...] = jnp.zeros_like(o_ref)
        st_ref[...] = jnp.zeros_like(st_ref)

        def one(r0, z_ref, lb, slot, rev):
            q = q_ref[pl.ds(r0, CHUNK), :]
            v = i_ref[pl.ds(r0, CHUNK), :]
            c = _hgrn_chunk(z_ref[pl.ds(r0, CHUNK), :], lb, q, v, m_up if rev else m_low, rev)
            qd, kd, kt, vb = c["qd"].astype(BF16), c["kd"].astype(BF16), c["kt"].astype(BF16), v.astype(BF16)
            pm = jnp.where(up if rev else low, _dot_nt(qd, kd), 0.0).astype(BF16)
            st = st_ref[slot]
            o = _dot(pm, vb) + _dot_nt(qd, st.astype(BF16))
            st_ref[slot] = jnp.exp(c["last"]) * st + _dot_tn(vb, kt)
            o_ref[pl.ds(r0, CHUNK), :] += o

        def step(n, carry):
            one(pl.multiple_of(n * CHUNK, CHUNK), zf_ref, lbf_ref[...], 0, False)
            one(pl.multiple_of((n_chunks - 1 - n) * CHUNK, CHUNK), zb_ref, lbb_ref[...], 1, True)
            return carry

        lax.fori_loop(0, n_chunks, step, 0)

        def fin(n, carry):
            rows = pl.ds(pl.multiple_of(n * fin_rows, fin_rows), fin_rows)
            o = o_ref[rows, :]
            g = g_ref[rows, :]
            y_ref[rows, :] = (o * _rstd(o) * gain_ref[...] * (g * jax.nn.sigmoid(g))).astype(BF16)
            return carry

        lax.fori_loop(0, t // fin_rows, fin, 0)

    col = lambda grp: pl.BlockSpec((t, HEAD), lambda h: (0, grp * n_heads + h))
    vec = pl.BlockSpec((1, HEAD), lambda h: (0, h))
    out = pl.BlockSpec((t, HEAD), lambda h: (0, h))
    return pl.pallas_call(
        body, name="hgrn_fwd", grid=(n_heads,),
        in_specs=[col(0), col(1), col(2), col(3), col(4), vec, vec, vec],
        out_specs=[out, out],
        out_shape=[jax.ShapeDtypeStruct((t, w), BF16), jax.ShapeDtypeStruct((t, w), F32)],
        scratch_shapes=[pltpu.VMEM((2, HEAD, HEAD), F32)],
        compiler_params=_cparams("parallel"),
    )(p, p, p, p, p, lb_f, lb_b, gain)


def _hgrn_bwd(p, o_raw, dcat, lb_f, lb_b, gain, n_heads):
    t = p.shape[0]
    w = n_heads * HEAD
    n_chunks = t // CHUNK
    rb = min(256, t)

    def body(q_ref, i_ref, zf_ref, zb_ref, g_ref, o_ref, dy_ref, lbf_ref, lbb_ref, gain_ref,
             dq_ref, di_ref, dzf_ref, dzb_ref, dg_ref, dlbf_ref, dlbb_ref, dgain_ref,
             do_s, dq_s, dv_s, st_s, cur_s):
        low, up = _tri_masks()
        m_low, m_up = low.astype(F32), up.astype(F32)
        rowid = lax.broadcasted_iota(jnp.int32, (CHUNK, HEAD), 0)
        gain_v = gain_ref[...]

        def norm_bwd(n, dgain):
            rows = pl.ds(pl.multiple_of(n * rb, rb), rb)
            o = o_ref[rows, :]
            g = g_ref[rows, :]
            dy = dy_ref[rows, :]
            r = _rstd(o)
            oh = o * r
            sg = jax.nn.sigmoid(g)
            dg_ref[rows, :] = (dy * oh * gain_v * (sg * (1.0 + g * (1.0 - sg)))).astype(BF16)
            dno = dy * (g * sg)
            dxh = dno * gain_v
            do_s[rows, :] = r * (dxh - oh * jnp.mean(dxh * oh, axis=-1, keepdims=True))
            return dgain + jnp.sum(dno * oh, axis=0, keepdims=True)

        dgain_ref[...] = lax.fori_loop(0, t // rb, norm_bwd, jnp.zeros((1, HEAD), F32))
        dq_s[...] = jnp.zeros_like(dq_s)
        dv_s[...] = jnp.zeros_like(dv_s)

        def direction(z_ref, lb_ref, dz_ref, dlb_ref, rev):
            lb = lb_ref[...]
            cum_mat = m_up if rev else m_low
            cum_mat_t = m_low if rev else m_up
            mask = up if rev else low
            last_row = 0 if rev else CHUNK - 1

            def rows_of(j):
                cidx = (n_chunks - 1 - j) if rev else j
                return pl.ds(pl.multiple_of(cidx * CHUNK, CHUNK), CHUNK)

            def sweep_fwd(j, carry):
                rows = rows_of(j)
                st = cur_s[0]
                st_s[j] = st
                v = i_ref[rows, :]
                c = _hgrn_chunk(z_ref[rows, :], lb, q_ref[rows, :], v, cum_mat, rev)
                cur_s[0] = jnp.exp(c["last"]) * st + _dot_tn(v.astype(BF16), c["kt"].astype(BF16))
                return carry

            cur_s[...] = jnp.zeros_like(cur_s)
            dlb_ref[...] = jnp.zeros_like(dlb_ref)
            lax.fori_loop(0, n_chunks, sweep_fwd, 0)
            cur_s[...] = jnp.zeros_like(cur_s)

            def sweep_bwd(jj, carry):
                dst = cur_s[1]
                j = n_chunks - 1 - jj
                rows = rows_of(j)
                v = i_ref[rows, :]
                c = _hgrn_chunk(z_ref[rows, :], lb, q_ref[rows, :], v, cum_mat, rev)
                st = st_s[j]
                do = do_s[rows, :]
                qd, kd, kt = c["qd"], c["kd"], c["kt"]
                qd_b, kd_b, kt_b = qd.astype(BF16), kd.astype(BF16), kt.astype(BF16)
                v_b, do_b, st_b, dst_b = v.astype(BF16), do.astype(BF16), st.astype(BF16), dst.astype(BF16)
                pm = jnp.where(mask, _dot_nt(qd_b, kd_b), 0.0).astype(BF16)
                dpm = jnp.where(mask, _dot_nt(do_b, v_b), 0.0).astype(BF16)
                dv = _dot_tn(pm, do_b) + _dot_nt(kt_b, dst_b)
                dqd = _dot(dpm, kd_b) + _dot(do_b, st_b)
                dkd = _dot_tn(dpm, qd_b)
                dkt = _dot(v_b, dst_b)
                e_last = jnp.exp(c["last"])
                dlast = (jnp.sum(dkt * kt, axis=0, keepdims=True)
                         + e_last * jnp.sum(dst * st, axis=0, keepdims=True))
                dst_new = _dot_tn(do_b, qd_b) + e_last * dst
                dq_s[rows, :] += dqd * c["e_a"]
                dv_s[rows, :] += dv
                dk = dkd * c["e_na"] + dkt * c["e_t"]
                da = dqd * qd - dkd * kd - dkt * kt
                da = da + jnp.where(rowid == last_row, dlast, 0.0)
                dlf = jnp.dot(cum_mat_t, da, precision=HIGHEST, preferred_element_type=F32)
                df = dlf / c["f"] - dk
                sg = c["sg"]
                dz_ref[rows, :] = (df * (1.0 - lb) * (sg * (1.0 - sg))).astype(BF16)
                cur_s[1] = dst_new
                dlb_ref[...] += jnp.sum(df * (1.0 - sg), axis=0, keepdims=True)
                return carry

            lax.fori_loop(0, n_chunks, sweep_bwd, 0)

        direction(zf_ref, lbf_ref, dzf_ref, dlbf_ref, False)
        direction(zb_ref, lbb_ref, dzb_ref, dlbb_ref, True)
        dq_ref[...] = dq_s[...].astype(BF16)
        di_ref[...] = dv_s[...].astype(BF16)

    col = lambda grp: pl.BlockSpec((t, HEAD), lambda h: (0, grp * n_heads + h))
    one = pl.BlockSpec((t, HEAD), lambda h: (0, h))
    vec = pl.BlockSpec((1, HEAD), lambda h: (0, h))
    big = jax.ShapeDtypeStruct((t, w), BF16)
    small = jax.ShapeDtypeStruct((1, w), F32)
    return pl.pallas_call(
        body, name="hgrn_bwd", grid=(n_heads,),
        in_specs=[col(0), col(1), col(2), col(3), col(4), one, one, vec, vec, vec],
        out_specs=[one] * 5 + [vec] * 3,
        out_shape=[big] * 5 + [small] * 3,
        scratch_shapes=[pltpu.VMEM((t, HEAD), F32), pltpu.VMEM((t, HEAD), F32), pltpu.VMEM((t, HEAD), F32),
                        pltpu.VMEM((n_chunks, HEAD, HEAD), F32), pltpu.VMEM((2, HEAD, HEAD), F32)],
        compiler_params=_cparams("parallel"),
    )(p, p, p, p, p, o_raw, dcat, lb_f, lb_b, gain)


def _t5_bucket_index():
    c = np.arange(WINDOW)[:, None]
    s = np.arange(KEY_SPAN)[None, :]
    rel = s - WINDOW - c
    nb = REL_BUCKETS // 2
    max_exact = nb // 2
    bucket = (rel > 0).astype(np.int32) * nb
    n = np.abs(rel)
    large = max_exact + (np.log(np.maximum(n, 1) / max_exact) / np.log(REL_MAX_DIST / max_exact)
                         * (nb - max_exact)).astype(np.int32)
    large = np.minimum(large, nb - 1)
    return bucket + np.where(n < max_exact, n, large).astype(np.int32)


def _attn_probs(qg, kb, bias, sink, valid):
    s = _dot_nt(qg, kb) / math.sqrt(HEAD) + bias
    s = jnp.where(valid, s, NEG_INF)
    m = jnp.maximum(jnp.max(s, axis=-1, keepdims=True), sink)
    e = jnp.exp(s - m)
    e_sink = jnp.exp(sink - m)
    den = jnp.sum(e, axis=-1, keepdims=True) + e_sink
    return e / den, e_sink / den


def _attn_valid(n, t):
    c = lax.broadcasted_iota(jnp.int32, (WINDOW, KEY_SPAN), 0)
    s = lax.broadcasted_iota(jnp.int32, (WINDOW, KEY_SPAN), 1)
    rel = s - WINDOW - c
    key_pos = n * WINDOW - WINDOW + s
    return (jnp.abs(rel) <= WINDOW) & (key_pos >= 0) & (key_pos < t)


def _attn_specs(t, n_hgrn, n_attn):
    grp = n_attn // KV_HEADS
    nb = t // WINDOW
    cq = 5 * n_hgrn
    ck = cq + n_attn
    cv = ck + KV_HEADS
    q_spec = pl.BlockSpec((WINDOW, grp * HEAD), lambda x, n: (n, cq // grp + x))
    kv = lambda base, off: pl.BlockSpec(
        (WINDOW, HEAD), lambda x, n: (jnp.clip(n + off, 0, nb - 1), base + x))
    band = [kv(ck, -1), kv(ck, 0), kv(ck, 1), kv(cv, -1), kv(cv, 0), kv(cv, 1)]
    bias_spec = pl.BlockSpec((grp, WINDOW, KEY_SPAN), lambda x, n: (x, 0, 0))
    sink_spec = pl.BlockSpec(memory_space=pltpu.SMEM)
    return grp, nb, q_spec, band, bias_spec, sink_spec


def _attn_fwd(p, bias, sink, n_hgrn, n_attn):
    t = p.shape[0]
    grp, nb, q_spec, band, bias_spec, sink_spec = _attn_specs(t, n_hgrn, n_attn)

    def body(q_ref, kp, kc, kn, vp, vc, vn, bias_ref, sink_ref, y_ref):
        x, n = pl.program_id(0), pl.program_id(1)
        kb = jnp.concatenate([kp[...], kc[...], kn[...]], axis=0).astype(BF16)
        vb = jnp.concatenate([vp[...], vc[...], vn[...]], axis=0).astype(BF16)
        valid = _attn_valid(n, t)
        for g in range(grp):
            qg = q_ref[:, g * HEAD:(g + 1) * HEAD].astype(BF16)
            pr, _ = _attn_probs(qg, kb, bias_ref[g], sink_ref[0, x * grp + g], valid)
            y_ref[:, g * HEAD:(g + 1) * HEAD] = _dot(pr.astype(BF16), vb).astype(BF16)

    return pl.pallas_call(
        body, name="attn_fwd", grid=(KV_HEADS, nb),
        in_specs=[q_spec] + band + [bias_spec, sink_spec],
        out_specs=pl.BlockSpec((WINDOW, grp * HEAD), lambda x, n: (n, x)),
        out_shape=jax.ShapeDtypeStruct((t, n_attn * HEAD), BF16),
        compiler_params=_cparams("parallel", "parallel"),
    )(p, p, p, p, p, p, p, bias, sink)


def _attn_bwd(p, dcat, bias, sink, n_hgrn, n_attn):
    t = p.shape[0]
    grp, nb, q_spec, band, bias_spec, sink_spec = _attn_specs(t, n_hgrn, n_attn)
    inv = 1.0 / math.sqrt(HEAD)

    def body(q_ref, kp, kc, kn, vp, vc, vn, bias_ref, sink_ref, do_ref,
             dq_ref, dk_ref, dv_ref, dbias_ref, dsink_ref, dk_s, dv_s):
        x, n = pl.program_id(0), pl.program_id(1)

        @pl.when(n == 0)
        def _():
            dk_s[...] = jnp.zeros_like(dk_s)
            dv_s[...] = jnp.zeros_like(dv_s)
            dbias_ref[...] = jnp.zeros_like(dbias_ref)
            dsink_ref[...] = jnp.zeros_like(dsink_ref)

        kb = jnp.concatenate([kp[...], kc[...], kn[...]], axis=0).astype(BF16)
        vb = jnp.concatenate([vp[...], vc[...], vn[...]], axis=0).astype(BF16)
        valid = _attn_valid(n, t)
        dkb = jnp.zeros((KEY_SPAN, HEAD), F32)
        dvb = jnp.zeros((KEY_SPAN, HEAD), F32)
        for g in range(grp):
            qg = q_ref[:, g * HEAD:(g + 1) * HEAD].astype(BF16)
            dog = do_ref[:, g * HEAD:(g + 1) * HEAD].astype(BF16)
            pr, p_sink = _attn_probs(qg, kb, bias_ref[g], sink_ref[0, x * grp + g], valid)
            dpr = _dot_nt(dog, vb)
            delta = jnp.sum(pr * dpr, axis=-1, keepdims=True)
            ds = pr * (dpr - delta)
            ds_b = ds.astype(BF16)
            dq_ref[:, g * HEAD:(g + 1) * HEAD] = (_dot(ds_b, kb) * inv).astype(BF16)
            dkb = dkb + _dot_tn(ds_b, qg) * inv
            dvb = dvb + _dot_tn(pr.astype(BF16), dog)
            dbias_ref[g] += ds
            dsink_ref[g:g + 1, :] += jnp.broadcast_to(
                jnp.sum(-p_sink * delta, axis=0, keepdims=True), (1, WINDOW))
        rows = pl.ds(pl.multiple_of(n * WINDOW, WINDOW), KEY_SPAN)
        dk_s[rows, :] += dkb
        dv_s[rows, :] += dvb

        @pl.when(n == nb - 1)
        def _():
            dk_ref[...] = dk_s[pl.ds(WINDOW, t), :].astype(BF16)
            dv_ref[...] = dv_s[pl.ds(WINDOW, t), :].astype(BF16)

    do_spec = pl.BlockSpec((WINDOW, grp * HEAD), lambda x, n: (n, n_hgrn // grp + x))
    kv_out = pl.BlockSpec((t, HEAD), lambda x, n: (0, x))
    return pl.pallas_call(
        body, name="attn_bwd", grid=(KV_HEADS, nb),
        in_specs=[q_spec] + band + [bias_spec, sink_spec, do_spec],
        out_specs=[pl.BlockSpec((WINDOW, grp * HEAD), lambda x, n: (n, x)), kv_out, kv_out,
                   bias_spec, pl.BlockSpec((None, grp, WINDOW), lambda x, n: (x, 0, 0))],
        out_shape=[jax.ShapeDtypeStruct((t, n_attn * HEAD), BF16),
                   jax.ShapeDtypeStruct((t, KV_HEADS * HEAD), BF16),
                   jax.ShapeDtypeStruct((t, KV_HEADS * HEAD), BF16),
                   jax.ShapeDtypeStruct((n_attn, WINDOW, KEY_SPAN), F32),
                   jax.ShapeDtypeStruct((KV_HEADS, grp, WINDOW), F32)],
        scratch_shapes=[pltpu.VMEM((t + 2 * WINDOW, HEAD), F32), pltpu.VMEM((t + 2 * WINDOW, HEAD), F32)],
        compiler_params=_cparams("parallel", "arbitrary"),
    )(p, p, p, p, p, p, p, bias, sink, dcat)


ANY = pl.BlockSpec(memory_space=pl.ANY)


def _position():
    return lax.axis_index("x"), lax.axis_index("y"), lax.axis_index("c")


def _all_gather(name, shard):
    def body(x_ref, out_ref, send_sems, recv_sems, local_sem):
        x, y, c = _position()
        me, sibling = (x, y, c), (x, y, 1 - c)
        chips = [(1 - x, y), (x, 1 - y), (1 - x, 1 - y)]

        def slot(px, py, pc):
            return out_ref.at[4 * px + 2 * py + pc]

        def copy(k, block, to, src=None):
            return pltpu.make_async_remote_copy(
                src_ref=slot(*block) if src is None else src, dst_ref=slot(*block),
                send_sem=send_sems.at[k], recv_sem=recv_sems.at[k], device_id=to, device_id_type=MESH)

        mine = pltpu.make_async_copy(x_ref, slot(*me), local_sem)
        mine.start()
        first = [copy(0, me, sibling, src=x_ref)]
        first += [copy(1 + j, me, (*chip, c), src=x_ref) for j, chip in enumerate(chips)]
        for cp in first:
            cp.start()
        passed = [copy(4 + j, (*chip, c), sibling) for j, chip in enumerate(chips)]
        for j, chip in enumerate(chips):
            copy(1 + j, (*chip, c), me).wait_recv()
            passed[j].start()
        copy(0, sibling, me).wait_recv()
        for j, chip in enumerate(chips):
            copy(4 + j, (*chip, 1 - c), me).wait_recv()
        for cp in first + passed:
            cp.wait_send()
        mine.wait()

    return pl.pallas_call(
        body, name=name, in_specs=[ANY], out_specs=ANY,
        out_shape=jax.ShapeDtypeStruct((N_DEV,) + shard.shape, shard.dtype),
        scratch_shapes=[pltpu.SemaphoreType.DMA((7,)), pltpu.SemaphoreType.DMA((7,)), pltpu.SemaphoreType.DMA],
    )(shard)


def _pair_exchange(name, stack):
    def body(s_ref, out_ref, send_sems, recv_sems):
        x, y, c = _position()
        sibling = (x, y, 1 - c)
        copies = [pltpu.make_async_remote_copy(
            src_ref=s_ref.at[2 * k + (1 - c)], dst_ref=out_ref.at[k], send_sem=send_sems.at[k],
            recv_sem=recv_sems.at[k], device_id=sibling, device_id_type=MESH) for k in range(4)]
        for cp in copies:
            cp.start()
        for cp in copies:
            cp.wait()

    return pl.pallas_call(
        body, name=name, in_specs=[ANY], out_specs=ANY,
        out_shape=jax.ShapeDtypeStruct((4,) + stack.shape[1:], stack.dtype),
        scratch_shapes=[pltpu.SemaphoreType.DMA((4,)), pltpu.SemaphoreType.DMA((4,))],
    )(stack)


def _pair_sum(name, stack, other, core):
    _, r, c = stack.shape
    tr = _tile(r, 256)

    def body(core_ref, a_ref, b_ref, o_ref):
        o_ref[...] = (a_ref[...].astype(F32) + b_ref[...].astype(F32)).astype(o_ref.dtype)

    grid_spec = pltpu.PrefetchScalarGridSpec(
        num_scalar_prefetch=1, grid=(4, r // tr),
        in_specs=[pl.BlockSpec((None, tr, c), lambda k, i, core_ref: (2 * k + core_ref[0], i, 0)),
                  pl.BlockSpec((None, tr, c), lambda k, i, core_ref: (k, i, 0))],
        out_specs=pl.BlockSpec((None, tr, c), lambda k, i, core_ref: (k, i, 0)))
    return pl.pallas_call(
        body, name=name, grid_spec=grid_spec, out_shape=jax.ShapeDtypeStruct((4, r, c), stack.dtype),
        compiler_params=_cparams("parallel", "parallel"),
    )(core, stack, other)


def _chip_exchange(name, sums):
    def body(s_ref, out_ref, send_sems, recv_sems):
        x, y, c = _position()
        chips = [(1 - x, y), (x, 1 - y), (1 - x, 1 - y)]
        copies = [pltpu.make_async_remote_copy(
            src_ref=s_ref.at[2 * px + py], dst_ref=out_ref.at[j], send_sem=send_sems.at[j],
            recv_sem=recv_sems.at[j], device_id=(px, py, c), device_id_type=MESH)
            for j, (px, py) in enumerate(chips)]
        for cp in copies:
            cp.start()
        for cp in copies:
            cp.wait()

    return pl.pallas_call(
        body, name=name, in_specs=[ANY], out_specs=ANY,
        out_shape=jax.ShapeDtypeStruct((3,) + sums.shape[1:], sums.dtype),
        scratch_shapes=[pltpu.SemaphoreType.DMA((3,)), pltpu.SemaphoreType.DMA((3,))],
    )(sums)


def _small_all_reduce(name, part):
    r = part.shape[0]

    def body(x_ref, out_ref, gather, send_sems, recv_sems):
        x, y, c = _position()
        me = 4 * x + 2 * y + c
        gather[me] = x_ref[...]
        copies = []
        for k in range(1, N_DEV):
            peer = (x ^ (k >> 2), y ^ ((k >> 1) & 1), c ^ (k & 1))
            copies.append(pltpu.make_async_remote_copy(
                src_ref=x_ref, dst_ref=gather.at[me], send_sem=send_sems.at[k - 1],
                recv_sem=recv_sems.at[k - 1], device_id=peer, device_id_type=MESH))
        for cp in copies:
            cp.start()
        for k in range(1, N_DEV):
            peer_slot = 4 * (x ^ (k >> 2)) + 2 * (y ^ ((k >> 1) & 1)) + (c ^ (k & 1))
            pltpu.make_async_remote_copy(
                src_ref=x_ref, dst_ref=gather.at[peer_slot], send_sem=send_sems.at[k - 1],
                recv_sem=recv_sems.at[k - 1], device_id=(x, y, c), device_id_type=MESH).wait()
        acc = gather[0]
        for j in range(1, N_DEV):
            acc = acc + gather[j]
        out_ref[...] = acc

    vm = pl.BlockSpec(memory_space=pltpu.VMEM)
    return pl.pallas_call(
        body, name=name, in_specs=[vm], out_specs=vm, out_shape=jax.ShapeDtypeStruct((r, LANES), F32),
        scratch_shapes=[pltpu.VMEM((N_DEV, r, LANES), F32), pltpu.SemaphoreType.DMA((7,)),
                        pltpu.SemaphoreType.DMA((7,))],
    )(part)


def _adam_math(w, g, m, v):
    m = ADAM_B1 * m + (1.0 - ADAM_B1) * g
    v = ADAM_B2 * v + (1.0 - ADAM_B2) * jnp.square(g)
    m_hat = m / (1.0 - ADAM_B1 ** ADAM_STEP)
    v_hat = v / (1.0 - ADAM_B2 ** ADAM_STEP)
    delta = -ADAM_LR * (m_hat / (jnp.sqrt(v_hat) + ADAM_EPS) + ADAM_WD * w)
    return delta, m, v


def _adam_shard(name, w, m, v, sums, recv, chip):
    r, c = w.shape
    tr = _tile(r, 128)

    def body(chip_ref, w_ref, m_ref, v_ref, own_ref, r0_ref, r1_ref, r2_ref, g_out, d_out, m_out, v_out):
        g = ((own_ref[...].astype(F32) + r0_ref[...].astype(F32)) + r1_ref[...].astype(F32)) + r2_ref[...].astype(F32)
        delta, m_new, v_new = _adam_math(w_ref[...], g, m_ref[...], v_ref[...])
        g_out[...] = g
        d_out[...] = delta
        m_out[...] = m_new
        v_out[...] = v_new

    plain = pl.BlockSpec((tr, c), lambda i, chip_ref: (i, 0))
    piece = lambda j: pl.BlockSpec((None, tr, c), lambda i, chip_ref: (j, i, 0))
    grid_spec = pltpu.PrefetchScalarGridSpec(
        num_scalar_prefetch=1, grid=(r // tr,),
        in_specs=[plain, plain, plain,
                  pl.BlockSpec((None, tr, c), lambda i, chip_ref: (chip_ref[0], i, 0)),
                  piece(0), piece(1), piece(2)],
        out_specs=[plain] * 4)
    shape = jax.ShapeDtypeStruct((r, c), F32)
    return pl.pallas_call(
        body, name=name, grid_spec=grid_spec, out_shape=[shape] * 4, compiler_params=_cparams("parallel"),
    )(chip, w, m, v, sums, recv, recv, recv)


def _adam_small(name, w, g, m, v):
    r = w.shape[0]

    def body(w_ref, g_ref, m_ref, v_ref, d_out, m_out, v_out):
        delta, m_new, v_new = _adam_math(w_ref[...], g_ref[...], m_ref[...], v_ref[...])
        d_out[...] = delta
        m_out[...] = m_new
        v_out[...] = v_new

    vm = pl.BlockSpec(memory_space=pltpu.VMEM)
    shape = jax.ShapeDtypeStruct((r, LANES), F32)
    return pl.pallas_call(body, name=name, in_specs=[vm] * 4, out_specs=[vm] * 3, out_shape=[shape] * 3)(w, g, m, v)


def _reduce_scatter_adam(tag, grad_stack, w, m, v, core, chip):
    other = _pair_exchange("rs_pair_" + tag, grad_stack)
    sums = _pair_sum("rs_sum_" + tag, grad_stack, other, core)
    recv = _chip_exchange("rs_chip_" + tag, sums)
    return _adam_shard("adam_" + tag, w, m, v, sums, recv, chip)


def _pack(arrays):
    flat = jnp.concatenate([a.reshape(-1).astype(F32) for a in arrays])
    rows = -(-flat.shape[0] // LANES)
    rows = -(-rows // 8) * 8
    return jnp.pad(flat, (0, rows * LANES - flat.shape[0])).reshape(rows, LANES)


def _unpack(packed, like):
    flat = packed.reshape(-1)
    out, off = [], 0
    for a in like:
        out.append(flat[off:off + a.size].reshape(a.shape))
        off += a.size
    return out


SMALL = ("pre_norm_ffn1", "post_norm_ffn1", "pre_norm_mix", "post_norm_mix", "hgrn_lower_bounds_fwd",
         "hgrn_lower_bounds_bwd", "hgrn_out_norm", "attn_sink", "pre_norm_ffn2", "post_norm_ffn2", "rel_bias_table")
BIG = ("w_ffn1_gate_up", "w_ffn1_down", "w_mix_in", "w_mix_out", "w_ffn2_gate_up", "w_ffn2_down")
ORDER = ("pre_norm_ffn1", "post_norm_ffn1", "w_ffn1_gate_up", "w_ffn1_down", "pre_norm_mix", "post_norm_mix",
         "w_mix_in", "hgrn_lower_bounds_fwd", "hgrn_lower_bounds_bwd", "hgrn_out_norm", "attn_sink", "w_mix_out",
         "pre_norm_ffn2", "post_norm_ffn2", "w_ffn2_gate_up", "w_ffn2_down", "rel_bias_table")


def kernel(x, pre_norm_ffn1, post_norm_ffn1, w_ffn1_gate_up, w_ffn1_down, pre_norm_mix, post_norm_mix, w_mix_in, hgrn_lower_bounds_fwd, hgrn_lower_bounds_bwd, hgrn_out_norm, attn_sink, w_mix_out, pre_norm_ffn2, post_norm_ffn2, w_ffn2_gate_up, w_ffn2_down, rel_bias_table, loss_target, m_pre_norm_ffn1, m_post_norm_ffn1, m_w_ffn1_gate_up, m_w_ffn1_down, m_pre_norm_mix, m_post_norm_mix, m_w_mix_in, m_hgrn_lower_bounds_fwd, m_hgrn_lower_bounds_bwd, m_hgrn_out_norm, m_attn_sink, m_w_mix_out, m_pre_norm_ffn2, m_post_norm_ffn2, m_w_ffn2_gate_up, m_w_ffn2_down, m_rel_bias_table, v_pre_norm_ffn1, v_post_norm_ffn1, v_w_ffn1_gate_up, v_w_ffn1_down, v_pre_norm_mix, v_post_norm_mix, v_w_mix_in, v_hgrn_lower_bounds_fwd, v_hgrn_lower_bounds_bwd, v_hgrn_out_norm, v_attn_sink, v_w_mix_out, v_pre_norm_ffn2, v_post_norm_ffn2, v_w_ffn2_gate_up, v_w_ffn2_down, v_rel_bias_table):
    args = dict(locals())
    wts = {n: args[n] for n in ORDER}
    mom = {n: args["m_" + n] for n in ORDER}
    var = {n: args["v_" + n] for n in ORDER}

    x0 = x[0]
    target = loss_target[0]
    t, d = x0.shape
    n_hgrn = d // 2 // HEAD
    n_attn = (d - d // 2) // HEAD
    core = lax.axis_index("c").astype(jnp.int32).reshape(1)
    chip = (2 * lax.axis_index("x") + lax.axis_index("y")).astype(jnp.int32).reshape(1)

    full = {n: _all_gather("ag_" + n, wts[n][0].astype(BF16)) for n in BIG}
    w_gu1, w_gu2 = full["w_ffn1_gate_up"], full["w_ffn2_gate_up"]
    w_d1 = full["w_ffn1_down"].reshape(-1, d)
    w_d2 = full["w_ffn2_down"].reshape(-1, d)
    w_out = full["w_mix_out"].reshape(-1, d)
    w_in = jnp.transpose(full["w_mix_in"], (1, 0, 2)).reshape(d, -1)

    g = {n: wts[n] for n in SMALL}
    lb_f = jax.nn.softmax(g["hgrn_lower_bounds_fwd"], axis=0)[0:1]
    lb_b = jax.nn.softmax(g["hgrn_lower_bounds_bwd"], axis=0)[0:1]
    bucket_idx = _t5_bucket_index()
    bias = jnp.transpose(g["rel_bias_table"][bucket_idx], (2, 0, 1))

    n1 = _pre_norm("pre_norm1", x0, g["pre_norm_ffn1"])
    gu1 = _matmul("ffn1_gate_up", n1, w_gu1, mode="nn", stack=True, out_dtype=F32)
    a1 = _swiglu_fwd("ffn1_act", gu1)
    ff1 = _matmul("ffn1_down", a1, w_d1, mode="nn", out_dtype=F32)
    x1, h = _post_res_pre("res1", x0, ff1, g["post_norm_ffn1"], g["pre_norm_mix"], 0.5)
    p = _matmul("mix_in", h, w_in, mode="nn", out_dtype=F32)
    y_h, o_raw = _hgrn_fwd(p, lb_f, lb_b, g["hgrn_out_norm"], n_hgrn)
    y_a = _attn_fwd(p, bias, g["attn_sink"], n_hgrn, n_attn)
    cat = jnp.concatenate([y_h, y_a], axis=1)
    mixed = _matmul("mix_out", cat, w_out, mode="nn", out_dtype=F32)
    x2, n2 = _post_res_pre("res2", x1, mixed, g["post_norm_mix"], g["pre_norm_ffn2"], 1.0)
    gu2 = _matmul("ffn2_gate_up", n2, w_gu2, mode="nn", stack=True, out_dtype=F32)
    a2 = _swiglu_fwd("ffn2_act", gu2)
    ff2 = _matmul("ffn2_down", a2, w_d2, mode="nn", out_dtype=F32)
    dy3, loss_part = _post_res_loss("res3_loss", x2, ff2, g["post_norm_ffn2"], target, 0.5)

    small_grad = {}
    big_grad = {}

    def ffn_bwd(tag, dy, ff, a, gu, n_in, x_in, w_gu, w_d, post_name, pre_name, gu_name, d_name):
        dff, small_grad[post_name] = _post_bwd("post_bwd" + tag, dy, ff, g[post_name], 0.5)
        da = _matmul("d_act" + tag, dff, w_d, mode="nt", out_dtype=F32)
        big_grad[d_name] = _matmul("dw_down" + tag, a, dff, mode="tn", out_dtype=BF16).reshape(N_DEV, -1, d)
        dgu = _swiglu_bwd("act_bwd" + tag, da, gu)
        dn = _matmul("d_norm" + tag, dgu, w_gu, mode="nt", stack=True, out_dtype=F32)
        big_grad[gu_name] = _matmul("dw_gate_up" + tag, n_in, dgu, mode="tn", stack=True, out_dtype=BF16)
        dx, small_grad[pre_name] = _pre_bwd("pre_bwd" + tag, dn, x_in, g[pre_name], dy)
        return dx

    dx2 = ffn_bwd("2", dy3, ff2, a2, gu2, n2, x2, w_gu2, w_d2, "post_norm_ffn2", "pre_norm_ffn2",
                  "w_ffn2_gate_up", "w_ffn2_down")

    dmixed, small_grad["post_norm_mix"] = _post_bwd("post_bwd_mix", dx2, mixed, g["post_norm_mix"], 1.0)
    dcat = _matmul("d_cat", dmixed, w_out, mode="nt", out_dtype=F32)
    big_grad["w_mix_out"] = _matmul("dw_mix_out", cat, dmixed, mode="tn", out_dtype=BF16).reshape(N_DEV, -1, d)
    dq_h, di_h, dzf, dzb, dg_h, dlb_f, dlb_b, small_grad["hgrn_out_norm"] = _hgrn_bwd(
        p, o_raw, dcat, lb_f, lb_b, g["hgrn_out_norm"], n_hgrn)
    dq_a, dk_a, dv_a, dbias, dsink_rows = _attn_bwd(p, dcat, bias, g["attn_sink"], n_hgrn, n_attn)
    dp = jnp.concatenate([dq_h, di_h, dzf, dzb, dg_h, dq_a, dk_a, dv_a], axis=1)
    dh = _matmul("d_h", dp, w_in, mode="nt", out_dtype=F32)
    dw_in = _matmul("dw_mix_in", h, dp, mode="tn", out_dtype=BF16)
    big_grad["w_mix_in"] = jnp.transpose(dw_in.reshape(d, N_DEV, -1), (1, 0, 2))
    dx1, small_grad["pre_norm_mix"] = _pre_bwd("pre_bwd_mix", dh, x1, g["pre_norm_mix"], dx2)

    dx0 = ffn_bwd("1", dx1, ff1, a1, gu1, n1, x0, w_gu1, w_d1, "post_norm_ffn1", "pre_norm_ffn1",
                  "w_ffn1_gate_up", "w_ffn1_down")

    def lb_grad(dlb, lb):
        da0 = dlb * lb * (1.0 - lb)
        return jnp.concatenate([da0, -da0], axis=0)

    small_grad["hgrn_lower_bounds_fwd"] = lb_grad(dlb_f, lb_f)
    small_grad["hgrn_lower_bounds_bwd"] = lb_grad(dlb_b, lb_b)
    small_grad["attn_sink"] = dsink_rows[:, :, 0].reshape(1, n_attn)
    onehot = jnp.asarray(np.eye(REL_BUCKETS, dtype=np.float32)[bucket_idx])
    small_grad["rel_bias_table"] = jnp.einsum("hcs,csb->bh", dbias, onehot, precision=HIGHEST)

    parts = [small_grad[n] for n in SMALL] + [loss_part[:, 0:1]]
    red = _small_all_reduce("small_all_reduce", _pack(parts))
    red_list = _unpack(red, parts)
    loss = red_list[-1].reshape(())
    sg = dict(zip(SMALL, red_list[:-1]))
    like = [wts[n] for n in SMALL]
    d_s, m_s, v_s = _adam_small("adam_small", _pack(like), _pack([sg[n] for n in SMALL]),
                                _pack([mom[n] for n in SMALL]), _pack([var[n] for n in SMALL]))
    grads = dict(sg)
    delta = dict(zip(SMALL, _unpack(d_s, like)))
    new_m = dict(zip(SMALL, _unpack(m_s, like)))
    new_v = dict(zip(SMALL, _unpack(v_s, like)))

    for n in BIG:
        gr, de, nm, nv = _reduce_scatter_adam(n, big_grad[n], wts[n][0], mom[n][0], var[n][0], core, chip)
        grads[n], delta[n], new_m[n], new_v[n] = gr[None], de[None], nm[None], nv[None]

    return (loss, dx0[None], *[grads[n] for n in ORDER], *[delta[n] for n in ORDER],
            *[new_m[n] for n in ORDER], *[new_v[n] for n in ORDER])
```

```python
import functools
import math

import numpy as np
import jax
import jax.numpy as jnp
from jax import lax
from jax.experimental import pallas as pl
from jax.experimental.pallas import tpu as pltpu
from jax.experimental.pallas import tpu_sc as plsc

F32 = jnp.float32
BF16 = jnp.bfloat16
HIGHEST = lax.Precision.HIGHEST
MESH = pl.DeviceIdType.MESH

N_DEV = 8
EPS = 1e-6
NEG_INF = -1e30
HEAD = 128
CHUNK = 64
WINDOW = 128
KEY_SPAN = 3 * WINDOW
KV_HEADS = 2
REL_BUCKETS = 32
REL_MAX_DIST = 128
ADAM_LR, ADAM_B1, ADAM_B2, ADAM_EPS, ADAM_WD, ADAM_STEP = 0.001, 0.9, 0.999, 1e-08, 0.01, 10
LANES = 128
VMEM_LIMIT = 56 * 1024 * 1024
ANY = pl.BlockSpec(memory_space=pl.ANY)


def _cparams(*sem):
    return pltpu.CompilerParams(dimension_semantics=sem if sem else None, vmem_limit_bytes=VMEM_LIMIT)


def _dot(a, b):
    return jnp.dot(a, b, preferred_element_type=F32)


def _dot_nt(a, b):
    return lax.dot_general(a, b, (((1,), (1,)), ((), ())), preferred_element_type=F32)


def _dot_tn(a, b):
    return lax.dot_general(a, b, (((0,), (0,)), ((), ())), preferred_element_type=F32)


def _tile(dim, target):
    for c in (target, 1024, 512, 256, 128):
        if c <= target and dim % c == 0:
            return c
    return dim


def _matmul(name, a, b, *, mode, out_dtype, stack=False, tm=1024, tn=1024, tk=512, deps=()):
    grp = 1
    if mode == "nn":
        m, kd = a.shape
        n = b.shape[0] * b.shape[2] if stack else b.shape[1]
    elif mode == "nt":
        m = a.shape[0]
        n, kd = (b.shape[1], b.shape[0] * b.shape[2]) if stack else b.shape
    else:
        kd, m = a.shape
        n = b.shape[1]
    if stack:
        n1 = b.shape[2] if mode != "tn" else n // N_DEV
        grp = 1 if n1 % LANES == 0 else 2
        assert (grp * n1) % LANES == 0
        if mode == "nt":
            tk = grp * n1
        else:
            tn = grp * n1
    tm = _tile(m, tm)
    if not (stack and mode in ("nn", "tn")):
        tn = _tile(n, tn)
    if not (stack and mode == "nt"):
        tk = _tile(kd, tk)
    nk = kd // tk
    grid = (m // tm, n // tn, nk)
    lead = None if grp == 1 else grp

    if mode == "nn":
        a_spec = pl.BlockSpec((tm, tk), lambda i, j, k: (i, k))
        if stack:
            b_spec = pl.BlockSpec((lead, tk, n1), lambda i, j, k: (j, k, 0))
        else:
            b_spec = pl.BlockSpec((tk, tn), lambda i, j, k: (k, j))
        dot = _dot
    elif mode == "nt":
        a_spec = pl.BlockSpec((tm, tk), lambda i, j, k: (i, k))
        if stack:
            b_spec = pl.BlockSpec((lead, tn, n1), lambda i, j, k: (k, j, 0))
        else:
            b_spec = pl.BlockSpec((tn, tk), lambda i, j, k: (j, k))
        dot = _dot_nt
    else:
        a_spec = pl.BlockSpec((tk, tm), lambda i, j, k: (k, i))
        b_spec = pl.BlockSpec((tk, tn), lambda i, j, k: (k, j))
        dot = _dot_tn
    if stack and mode == "tn":
        out_shape = jax.ShapeDtypeStruct((N_DEV, m, n1), out_dtype)
        o_spec = pl.BlockSpec((lead, tm, n1), lambda i, j, k: (j, i, 0))
    else:
        out_shape = jax.ShapeDtypeStruct((m, n), out_dtype)
        o_spec = pl.BlockSpec((tm, tn), lambda i, j, k: (i, j))
    b_grouped = stack and grp > 1 and mode != "tn"
    o_grouped = stack and grp > 1 and mode == "tn"

    def body(a_ref, b_ref, *rest):
        o_ref, acc_ref = rest[-2:]
        k = pl.program_id(2)

        @pl.when(k == 0)
        def _():
            acc_ref[...] = jnp.zeros_like(acc_ref)

        bmat = jnp.concatenate([b_ref[s] for s in range(grp)], axis=1) if b_grouped else b_ref[...]
        acc_ref[...] += dot(a_ref[...], bmat)

        @pl.when(k == nk - 1)
        def _():
            if o_grouped:
                for s in range(grp):
                    o_ref[s] = acc_ref[:, s * n1:(s + 1) * n1].astype(o_ref.dtype)
            else:
                o_ref[...] = acc_ref[...].astype(o_ref.dtype)

    return pl.pallas_call(
        body, name=name, grid=grid, in_specs=[a_spec, b_spec] + [ANY] * len(deps), out_specs=o_spec,
        out_shape=out_shape, scratch_shapes=[pltpu.VMEM((tm, tn), F32)],
        compiler_params=_cparams("parallel", "parallel", "arbitrary"),
    )(a, b, *deps)


ROWS = 256


def _rstd(xf):
    return lax.rsqrt(jnp.mean(xf * xf, axis=-1, keepdims=True) + EPS)


def _row_spec(t, d):
    return pl.BlockSpec((min(ROWS, t), d), lambda i: (i, 0))


def _vec_spec(d):
    return pl.BlockSpec((1, d), lambda i: (0, 0))


def _pre_norm(name, x, gain):
    t, d = x.shape

    def body(x_ref, g_ref, n_ref):
        xf = x_ref[...]
        n_ref[...] = (xf * _rstd(xf) * g_ref[...]).astype(BF16)

    return pl.pallas_call(
        body, name=name, grid=(t // min(ROWS, t),), in_specs=[_row_spec(t, d), _vec_spec(d)],
        out_specs=_row_spec(t, d), out_shape=jax.ShapeDtypeStruct((t, d), BF16),
        compiler_params=_cparams("parallel"),
    )(x, gain)


def _post_res_pre(name, x, ff, g_post, g_next, scale):
    t, d = x.shape

    def body(x_ref, ff_ref, gp_ref, gn_ref, xo_ref, n_ref):
        ff_ = ff_ref[...]
        xn = x_ref[...] + scale * (ff_ * _rstd(ff_) * gp_ref[...])
        xo_ref[...] = xn
        n_ref[...] = (xn * _rstd(xn) * gn_ref[...]).astype(BF16)

    return pl.pallas_call(
        body, name=name, grid=(t // min(ROWS, t),),
        in_specs=[_row_spec(t, d), _row_spec(t, d), _vec_spec(d), _vec_spec(d)],
        out_specs=[_row_spec(t, d), _row_spec(t, d)],
        out_shape=[jax.ShapeDtypeStruct((t, d), F32), jax.ShapeDtypeStruct((t, d), BF16)],
        compiler_params=_cparams("parallel"),
    )(x, ff, g_post, g_next)


def _post_res_loss(name, x, ff, g_post, target, scale):
    t, d = x.shape

    def body(x_ref, ff_ref, gp_ref, tg_ref, dy_ref, loss_ref):
        ff_ = ff_ref[...]
        err = x_ref[...] + scale * (ff_ * _rstd(ff_) * gp_ref[...]) - tg_ref[...]
        dy_ref[...] = err / d
        part = 0.5 * jnp.sum(jnp.mean(err * err, axis=-1, keepdims=True), axis=0, keepdims=True)

        @pl.when(pl.program_id(0) == 0)
        def _():
            loss_ref[...] = jnp.zeros_like(loss_ref)

        loss_ref[...] += jnp.broadcast_to(part, loss_ref.shape)

    return pl.pallas_call(
        body, name=name, grid=(t // min(ROWS, t),),
        in_specs=[_row_spec(t, d), _row_spec(t, d), _vec_spec(d), _row_spec(t, d)],
        out_specs=[_row_spec(t, d), _vec_spec(LANES)],
        out_shape=[jax.ShapeDtypeStruct((t, d), F32), jax.ShapeDtypeStruct((1, LANES), F32)],
        compiler_params=_cparams("arbitrary"),
    )(x, ff, g_post, target)


def _post_bwd(name, dy, ff, g_post, scale):
    t, d = dy.shape

    def body(dy_ref, ff_ref, gp_ref, dff_ref, dg_ref):
        ff_ = ff_ref[...]
        r = _rstd(ff_)
        xh = ff_ * r
        dyn = scale * dy_ref[...]
        dxh = dyn * gp_ref[...]
        dff_ref[...] = (r * (dxh - xh * jnp.mean(dxh * xh, axis=-1, keepdims=True))).astype(BF16)

        @pl.when(pl.program_id(0) == 0)
        def _():
            dg_ref[...] = jnp.zeros_like(dg_ref)

        dg_ref[...] += jnp.sum(dyn * xh, axis=0, keepdims=True)

    return pl.pallas_call(
        body, name=name, grid=(t // min(ROWS, t),),
        in_specs=[_row_spec(t, d), _row_spec(t, d), _vec_spec(d)],
        out_specs=[_row_spec(t, d), _vec_spec(d)],
        out_shape=[jax.ShapeDtypeStruct((t, d), BF16), jax.ShapeDtypeStruct((1, d), F32)],
        compiler_params=_cparams("arbitrary"),
    )(dy, ff, g_post)


def _pre_bwd(name, dn, x, g_pre, dy, deps=()):
    t, d = x.shape

    def body(dn_ref, x_ref, g_ref, dy_ref, *rest):
        dx_ref, dg_ref = rest[-2:]
        xf = x_ref[...]
        r = _rstd(xf)
        xh = xf * r
        dnf = dn_ref[...].astype(F32)
        dxh = dnf * g_ref[...]
        dx_ref[...] = dy_ref[...] + r * (dxh - xh * jnp.mean(dxh * xh, axis=-1, keepdims=True))

        @pl.when(pl.program_id(0) == 0)
        def _():
            dg_ref[...] = jnp.zeros_like(dg_ref)

        dg_ref[...] += jnp.sum(dnf * xh, axis=0, keepdims=True)

    return pl.pallas_call(
        body, name=name, grid=(t // min(ROWS, t),),
        in_specs=[_row_spec(t, d), _row_spec(t, d), _vec_spec(d), _row_spec(t, d)] + [ANY] * len(deps),
        out_specs=[_row_spec(t, d), _vec_spec(d)],
        out_shape=[jax.ShapeDtypeStruct((t, d), F32), jax.ShapeDtypeStruct((1, d), F32)],
        compiler_params=_cparams("arbitrary"),
    )(dn, x, g_pre, dy, *deps)


def _swiglu_fwd(name, gu):
    t, f2 = gu.shape
    f = f2 // 2
    tc = _tile(f, 512)
    nb = f // tc
    spec = lambda off: pl.BlockSpec((min(ROWS, t), tc), lambda i, j: (i, j + off))

    def body(g_ref, u_ref, a_ref):
        g = g_ref[...]
        a_ref[...] = (g * jax.nn.sigmoid(g) * u_ref[...]).astype(BF16)

    return pl.pallas_call(
        body, name=name, grid=(t // min(ROWS, t), nb), in_specs=[spec(0), spec(nb)], out_specs=spec(0),
        out_shape=jax.ShapeDtypeStruct((t, f), BF16), compiler_params=_cparams("parallel", "parallel"),
    )(gu, gu)


def _swiglu_bwd(name, da, gu, deps=()):
    t, f2 = gu.shape
    f = f2 // 2
    tc = _tile(f, 512)
    nb = f // tc
    spec = lambda off: pl.BlockSpec((min(ROWS, t), tc), lambda i, j: (i, j + off))

    def body(da_ref, g_ref, u_ref, *rest):
        dg_ref, du_ref = rest[-2:]
        g = g_ref[...]
        da_ = da_ref[...]
        sg = jax.nn.sigmoid(g)
        dg_ref[...] = (da_ * u_ref[...] * (sg * (1.0 + g * (1.0 - sg)))).astype(BF16)
        du_ref[...] = (da_ * (g * sg)).astype(BF16)

    dg, du = pl.pallas_call(
        body, name=name, grid=(t // min(ROWS, t), nb), in_specs=[spec(0), spec(0), spec(nb)] + [ANY] * len(deps),
        out_specs=[spec(0), spec(0)],
        out_shape=[jax.ShapeDtypeStruct((t, f), BF16), jax.ShapeDtypeStruct((t, f), BF16)],
        compiler_params=_cparams("parallel", "parallel"),
    )(da, gu, gu, *deps)
    return jnp.concatenate([dg, du], axis=1)


def _tri_masks():
    row = lax.broadcasted_iota(jnp.int32, (CHUNK, CHUNK), 0)
    col = lax.broadcasted_iota(jnp.int32, (CHUNK, CHUNK), 1)
    return col <= row, col >= row


def _hgrn_chunk(z, lb, q, v, cum_mat, rev):
    sg = jax.nn.sigmoid(z)
    f = lb + (1.0 - lb) * sg
    lf = jnp.log(f)
    k = 1.0 - f
    a = jnp.dot(cum_mat, lf, precision=HIGHEST, preferred_element_type=F32)
    last = jnp.sum(lf, axis=0, keepdims=True)
    e_a = jnp.exp(a)
    e_na = jnp.exp(-a)
    e_t = jnp.exp(last - a)
    return dict(sg=sg, f=f, k=k, last=last, e_a=e_a, e_na=e_na, e_t=e_t,
                qd=q * e_a, kd=k * e_na, kt=k * e_t)


def _hgrn_fwd(p, lb_f, lb_b, gain, n_heads):
    t = p.shape[0]
    w = n_heads * HEAD
    n_chunks = t // CHUNK
    fin_rows = min(256, t)

    def body(q_ref, i_ref, zf_ref, zb_ref, g_ref, lbf_ref, lbb_ref, gain_ref, y_ref, o_ref, st_ref):
        low, up = _tri_masks()
        m_low, m_up = low.astype(F32), up.astype(F32)
        o_ref[...] = jnp.zeros_like(o_ref)
        st_ref[...] = jnp.zeros_like(st_ref)

        def one(r0, z_ref, lb, slot, rev):
            q = q_ref[pl.ds(r0, CHUNK), :]
            v = i_ref[pl.ds(r0, CHUNK), :]
            c = _hgrn_chunk(z_ref[pl.ds(r0, CHUNK), :], lb, q, v, m_up if rev else m_low, rev)
            qd, kd, kt, vb = c["qd"].astype(BF16), c["kd"].astype(BF16), c["kt"].astype(BF16), v.astype(BF16)
            pm = jnp.where(up if rev else low, _dot_nt(qd, kd), 0.0).astype(BF16)
            st = st_ref[slot]
            o = _dot(pm, vb) + _dot_nt(qd, st.astype(BF16))
            st_ref[slot] = jnp.exp(c["last"]) * st + _dot_tn(vb, kt)
            o_ref[pl.ds(r0, CHUNK), :] += o

        def step(n, carry):
            one(pl.multiple_of(n * CHUNK, CHUNK), zf_ref, lbf_ref[...], 0, False)
            one(pl.multiple_of((n_chunks - 1 - n) * CHUNK, CHUNK), zb_ref, lbb_ref[...], 1, True)
            return carry

        lax.fori_loop(0, n_chunks, step, 0)

        def fin(n, carry):
            rows = pl.ds(pl.multiple_of(n * fin_rows, fin_rows), fin_rows)
            o = o_ref[rows, :]
            g = g_ref[rows, :]
            y_ref[rows, :] = (o * _rstd(o) * gain_ref[...] * (g * jax.nn.sigmoid(g))).astype(BF16)
            return carry

        lax.fori_loop(0, t // fin_rows, fin, 0)

    col = lambda grp: pl.BlockSpec((t, HEAD), lambda h: (0, grp * n_heads + h))
    vec = pl.BlockSpec((1, HEAD), lambda h: (0, h))
    out = pl.BlockSpec((t, HEAD), lambda h: (0, h))
    return pl.pallas_call(
        body, name="hgrn_fwd", grid=(n_heads,),
        in_specs=[col(0), col(1), col(2), col(3), col(4), vec, vec, vec],
        out_specs=[out, out],
        out_shape=[jax.ShapeDtypeStruct((t, w), BF16), jax.ShapeDtypeStruct((t, w), F32)],
        scratch_shapes=[pltpu.VMEM((2, HEAD, HEAD), F32)],
        compiler_params=_cparams("parallel"),
    )(p, p, p, p, p, lb_f, lb_b, gain)


def _hgrn_bwd(p, o_raw, dcat, lb_f, lb_b, gain, n_heads):
    t = p.shape[0]
    w = n_heads * HEAD
    n_chunks = t // CHUNK
    rb = min(256, t)

    def body(q_ref, i_ref, zf_ref, zb_ref, g_ref, o_ref, dy_ref, lbf_ref, lbb_ref, gain_ref,
             dq_ref, di_ref, dzf_ref, dzb_ref, dg_ref, dlbf_ref, dlbb_ref, dgain_ref,
             do_s, dq_s, dv_s, st_s, cur_s):
        low, up = _tri_masks()
        m_low, m_up = low.astype(F32), up.astype(F32)
        rowid = lax.broadcasted_iota(jnp.int32, (CHUNK, HEAD), 0)
        gain_v = gain_ref[...]

        def norm_bwd(n, dgain):
            rows = pl.ds(pl.multiple_of(n * rb, rb), rb)
            o = o_ref[rows, :]
            g = g_ref[rows, :]
            dy = dy_ref[rows, :]
            r = _rstd(o)
            oh = o * r
            sg = jax.nn.sigmoid(g)
            dg_ref[rows, :] = (dy * oh * gain_v * (sg * (1.0 + g * (1.0 - sg)))).astype(BF16)
            dno = dy * (g * sg)
            dxh = dno * gain_v
            do_s[rows, :] = r * (dxh - oh * jnp.mean(dxh * oh, axis=-1, keepdims=True))
            return dgain + jnp.sum(dno * oh, axis=0, keepdims=True)

        dgain_ref[...] = lax.fori_loop(0, t // rb, norm_bwd, jnp.zeros((1, HEAD), F32))
        dq_s[...] = jnp.zeros_like(dq_s)
        dv_s[...] = jnp.zeros_like(dv_s)

        def direction(z_ref, lb_ref, dz_ref, dlb_ref, rev):
            lb = lb_ref[...]
            cum_mat = m_up if rev else m_low
            cum_mat_t = m_low if rev else m_up
            mask = up if rev else low
            last_row = 0 if rev else CHUNK - 1

            def rows_of(j):
                cidx = (n_chunks - 1 - j) if rev else j
                return pl.ds(pl.multiple_of(cidx * CHUNK, CHUNK), CHUNK)

            def sweep_fwd(j, carry):
                rows = rows_of(j)
                st = cur_s[0]
                st_s[j] = st
                v = i_ref[rows, :]
                c = _hgrn_chunk(z_ref[rows, :], lb, q_ref[rows, :], v, cum_mat, rev)
                cur_s[0] = jnp.exp(c["last"]) * st + _dot_tn(v.astype(BF16), c["kt"].astype(BF16))
                return carry

            cur_s[...] = jnp.zeros_like(cur_s)
            dlb_ref[...] = jnp.zeros_like(dlb_ref)
            lax.fori_loop(0, n_chunks, sweep_fwd, 0)
            cur_s[...] = jnp.zeros_like(cur_s)

            def sweep_bwd(jj, carry):
                dst = cur_s[1]
                j = n_chunks - 1 - jj
                rows = rows_of(j)
                v = i_ref[rows, :]
                c = _hgrn_chunk(z_ref[rows, :], lb, q_ref[rows, :], v, cum_mat, rev)
                st = st_s[j]
                do = do_s[rows, :]
                qd, kd, kt = c["qd"], c["kd"], c["kt"]
                qd_b, kd_b, kt_b = qd.astype(BF16), kd.astype(BF16), kt.astype(BF16)
                v_b, do_b, st_b, dst_b = v.astype(BF16), do.astype(BF16), st.astype(BF16), dst.astype(BF16)
                pm = jnp.where(mask, _dot_nt(qd_b, kd_b), 0.0).astype(BF16)
                dpm = jnp.where(mask, _dot_nt(do_b, v_b), 0.0).astype(BF16)
                dv = _dot_tn(pm, do_b) + _dot_nt(kt_b, dst_b)
                dqd = _dot(dpm, kd_b) + _dot(do_b, st_b)
                dkd = _dot_tn(dpm, qd_b)
                dkt = _dot(v_b, dst_b)
                e_last = jnp.exp(c["last"])
                dlast = (jnp.sum(dkt * kt, axis=0, keepdims=True)
                         + e_last * jnp.sum(dst * st, axis=0, keepdims=True))
                dst_new = _dot_tn(do_b, qd_b) + e_last * dst
                dq_s[rows, :] += dqd * c["e_a"]
                dv_s[rows, :] += dv
                dk = dkd * c["e_na"] + dkt * c["e_t"]
                da = dqd * qd - dkd * kd - dkt * kt
                da = da + jnp.where(rowid == last_row, dlast, 0.0)
                dlf = jnp.dot(cum_mat_t, da, precision=HIGHEST, preferred_element_type=F32)
                df = dlf / c["f"] - dk
                sg = c["sg"]
                dz_ref[rows, :] = (df * (1.0 - lb) * (sg * (1.0 - sg))).astype(BF16)
                cur_s[1] = dst_new
                dlb_ref[...] += jnp.sum(df * (1.0 - sg), axis=0, keepdims=True)
                return carry

            lax.fori_loop(0, n_chunks, sweep_bwd, 0)

        direction(zf_ref, lbf_ref, dzf_ref, dlbf_ref, False)
        direction(zb_ref, lbb_ref, dzb_ref, dlbb_ref, True)
        dq_ref[...] = dq_s[...].astype(BF16)
        di_ref[...] = dv_s[...].astype(BF16)

    col = lambda grp: pl.BlockSpec((t, HEAD), lambda h: (0, grp * n_heads + h))
    one = pl.BlockSpec((t, HEAD), lambda h: (0, h))
    vec = pl.BlockSpec((1, HEAD), lambda h: (0, h))
    big = jax.ShapeDtypeStruct((t, w), BF16)
    small = jax.ShapeDtypeStruct((1, w), F32)
    return pl.pallas_call(
        body, name="hgrn_bwd", grid=(n_heads,),
        in_specs=[col(0), col(1), col(2), col(3), col(4), one, one, vec, vec, vec],
        out_specs=[one] * 5 + [vec] * 3,
        out_shape=[big] * 5 + [small] * 3,
        scratch_shapes=[pltpu.VMEM((t, HEAD), F32), pltpu.VMEM((t, HEAD), F32), pltpu.VMEM((t, HEAD), F32),
                        pltpu.VMEM((n_chunks, HEAD, HEAD), F32), pltpu.VMEM((2, HEAD, HEAD), F32)],
        compiler_params=_cparams("parallel"),
    )(p, p, p, p, p, o_raw, dcat, lb_f, lb_b, gain)


def _t5_bucket_index():
    c = np.arange(WINDOW)[:, None]
    s = np.arange(KEY_SPAN)[None, :]
    rel = s - WINDOW - c
    nb = REL_BUCKETS // 2
    max_exact = nb // 2
    bucket = (rel > 0).astype(np.int32) * nb
    n = np.abs(rel)
    large = max_exact + (np.log(np.maximum(n, 1) / max_exact) / np.log(REL_MAX_DIST / max_exact)
                         * (nb - max_exact)).astype(np.int32)
    large = np.minimum(large, nb - 1)
    return bucket + np.where(n < max_exact, n, large).astype(np.int32)


def _bias_build(table, idx):
    n_attn = table.shape[1]

    def body(tab_ref, idx_ref, o_ref):
        h = pl.program_id(0)
        idx_v = idx_ref[...]
        acc = jnp.zeros((WINDOW, KEY_SPAN), F32)
        for b in range(REL_BUCKETS):
            acc = jnp.where(idx_v == b, tab_ref[b, h], acc)
        o_ref[...] = acc

    return pl.pallas_call(
        body, name="bias_build", grid=(n_attn,),
        in_specs=[pl.BlockSpec(memory_space=pltpu.SMEM), pl.BlockSpec((WINDOW, KEY_SPAN), lambda h: (0, 0))],
        out_specs=pl.BlockSpec((None, WINDOW, KEY_SPAN), lambda h: (h, 0, 0)),
        out_shape=jax.ShapeDtypeStruct((n_attn, WINDOW, KEY_SPAN), F32), compiler_params=_cparams("parallel"),
    )(table, idx)


def _bias_reduce(dbias, idx):
    n_attn = dbias.shape[0]

    def body(idx_ref, d_ref, o_ref):
        idx_v = idx_ref[...]
        dv = d_ref[...]
        rows = lax.broadcasted_iota(jnp.int32, (REL_BUCKETS, LANES), 0)
        acc = jnp.zeros((REL_BUCKETS, LANES), F32)
        for b in range(REL_BUCKETS):
            part = jnp.sum(jnp.where(idx_v == b, dv, 0.0), axis=1, keepdims=True)
            acc = jnp.where(rows == b, jnp.sum(part, axis=0, keepdims=True), acc)
        o_ref[...] = acc

    return pl.pallas_call(
        body, name="bias_reduce", grid=(n_attn,),
        in_specs=[pl.BlockSpec((WINDOW, KEY_SPAN), lambda h: (0, 0)),
                  pl.BlockSpec((None, WINDOW, KEY_SPAN), lambda h: (h, 0, 0))],
        out_specs=pl.BlockSpec((None, REL_BUCKETS, LANES), lambda h: (h, 0, 0)),
        out_shape=jax.ShapeDtypeStruct((n_attn, REL_BUCKETS, LANES), F32), compiler_params=_cparams("parallel"),
    )(idx, dbias)


def _attn_probs(qg, kb, bias, sink, valid):
    s = _dot_nt(qg, kb) / math.sqrt(HEAD) + bias
    s = jnp.where(valid, s, NEG_INF)
    m = jnp.maximum(jnp.max(s, axis=-1, keepdims=True), sink)
    e = jnp.exp(s - m)
    e_sink = jnp.exp(sink - m)
    den = jnp.sum(e, axis=-1, keepdims=True) + e_sink
    return e / den, e_sink / den


def _attn_valid(n, t):
    c = lax.broadcasted_iota(jnp.int32, (WINDOW, KEY_SPAN), 0)
    s = lax.broadcasted_iota(jnp.int32, (WINDOW, KEY_SPAN), 1)
    rel = s - WINDOW - c
    key_pos = n * WINDOW - WINDOW + s
    return (jnp.abs(rel) <= WINDOW) & (key_pos >= 0) & (key_pos < t)


def _attn_specs(t, n_hgrn, n_attn):
    grp = n_attn // KV_HEADS
    nb = t // WINDOW
    cq = 5 * n_hgrn
    ck = cq + n_attn
    cv = ck + KV_HEADS
    q_spec = pl.BlockSpec((WINDOW, grp * HEAD), lambda x, n: (n, cq // grp + x))
    kv = lambda base, off: pl.BlockSpec(
        (WINDOW, HEAD), lambda x, n: (jnp.clip(n + off, 0, nb - 1), base + x))
    band = [kv(ck, -1), kv(ck, 0), kv(ck, 1), kv(cv, -1), kv(cv, 0), kv(cv, 1)]
    bias_spec = pl.BlockSpec((grp, WINDOW, KEY_SPAN), lambda x, n: (x, 0, 0))
    sink_spec = pl.BlockSpec(memory_space=pltpu.SMEM)
    return grp, nb, q_spec, band, bias_spec, sink_spec


def _attn_fwd(p, bias, sink, n_hgrn, n_attn):
    t = p.shape[0]
    grp, nb, q_spec, band, bias_spec, sink_spec = _attn_specs(t, n_hgrn, n_attn)

    def body(q_ref, kp, kc, kn, vp, vc, vn, bias_ref, sink_ref, y_ref):
        x, n = pl.program_id(0), pl.program_id(1)
        kb = jnp.concatenate([kp[...], kc[...], kn[...]], axis=0).astype(BF16)
        vb = jnp.concatenate([vp[...], vc[...], vn[...]], axis=0).astype(BF16)
        valid = _attn_valid(n, t)
        for g in range(grp):
            qg = q_ref[:, g * HEAD:(g + 1) * HEAD].astype(BF16)
            pr, _ = _attn_probs(qg, kb, bias_ref[g], sink_ref[0, x * grp + g], valid)
            y_ref[:, g * HEAD:(g + 1) * HEAD] = _dot(pr.astype(BF16), vb).astype(BF16)

    return pl.pallas_call(
        body, name="attn_fwd", grid=(KV_HEADS, nb),
        in_specs=[q_spec] + band + [bias_spec, sink_spec],
        out_specs=pl.BlockSpec((WINDOW, grp * HEAD), lambda x, n: (n, x)),
        out_shape=jax.ShapeDtypeStruct((t, n_attn * HEAD), BF16),
        compiler_params=_cparams("parallel", "parallel"),
    )(p, p, p, p, p, p, p, bias, sink)


def _attn_bwd(p, dcat, bias, sink, n_hgrn, n_attn):
    t = p.shape[0]
    grp, nb, q_spec, band, bias_spec, sink_spec = _attn_specs(t, n_hgrn, n_attn)
    inv = 1.0 / math.sqrt(HEAD)

    def body(q_ref, kp, kc, kn, vp, vc, vn, bias_ref, sink_ref, do_ref,
             dq_ref, dk_ref, dv_ref, dbias_ref, dsink_ref, dk_s, dv_s):
        x, n = pl.program_id(0), pl.program_id(1)

        @pl.when(n == 0)
        def _():
            dk_s[...] = jnp.zeros_like(dk_s)
            dv_s[...] = jnp.zeros_like(dv_s)
            dbias_ref[...] = jnp.zeros_like(dbias_ref)
            dsink_ref[...] = jnp.zeros_like(dsink_ref)

        kb = jnp.concatenate([kp[...], kc[...], kn[...]], axis=0).astype(BF16)
        vb = jnp.concatenate([vp[...], vc[...], vn[...]], axis=0).astype(BF16)
        valid = _attn_valid(n, t)
        dkb = jnp.zeros((KEY_SPAN, HEAD), F32)
        dvb = jnp.zeros((KEY_SPAN, HEAD), F32)
        for g in range(grp):
            qg = q_ref[:, g * HEAD:(g + 1) * HEAD].astype(BF16)
            dog = do_ref[:, g * HEAD:(g + 1) * HEAD].astype(BF16)
            pr, p_sink = _attn_probs(qg, kb, bias_ref[g], sink_ref[0, x * grp + g], valid)
            dpr = _dot_nt(dog, vb)
            delta = jnp.sum(pr * dpr, axis=-1, keepdims=True)
            ds = pr * (dpr - delta)
            ds_b = ds.astype(BF16)
            dq_ref[:, g * HEAD:(g + 1) * HEAD] = (_dot(ds_b, kb) * inv).astype(BF16)
            dkb = dkb + _dot_tn(ds_b, qg) * inv
            dvb = dvb + _dot_tn(pr.astype(BF16), dog)
            dbias_ref[g] += ds
            dsink_ref[g:g + 1, :] += jnp.broadcast_to(
                jnp.sum(-p_sink * delta, axis=0, keepdims=True), (1, WINDOW))
        rows = pl.ds(pl.multiple_of(n * WINDOW, WINDOW), KEY_SPAN)
        dk_s[rows, :] += dkb
        dv_s[rows, :] += dvb

        @pl.when(n == nb - 1)
        def _():
            dk_ref[...] = dk_s[pl.ds(WINDOW, t), :].astype(BF16)
            dv_ref[...] = dv_s[pl.ds(WINDOW, t), :].astype(BF16)

    do_spec = pl.BlockSpec((WINDOW, grp * HEAD), lambda x, n: (n, n_hgrn // grp + x))
    kv_out = pl.BlockSpec((t, HEAD), lambda x, n: (0, x))
    return pl.pallas_call(
        body, name="attn_bwd", grid=(KV_HEADS, nb),
        in_specs=[q_spec] + band + [bias_spec, sink_spec, do_spec],
        out_specs=[pl.BlockSpec((WINDOW, grp * HEAD), lambda x, n: (n, x)), kv_out, kv_out,
                   bias_spec, pl.BlockSpec((None, grp, WINDOW), lambda x, n: (x, 0, 0))],
        out_shape=[jax.ShapeDtypeStruct((t, n_attn * HEAD), BF16),
                   jax.ShapeDtypeStruct((t, KV_HEADS * HEAD), BF16),
                   jax.ShapeDtypeStruct((t, KV_HEADS * HEAD), BF16),
                   jax.ShapeDtypeStruct((n_attn, WINDOW, KEY_SPAN), F32),
                   jax.ShapeDtypeStruct((KV_HEADS, grp, WINDOW), F32)],
        scratch_shapes=[pltpu.VMEM((t + 2 * WINDOW, HEAD), F32), pltpu.VMEM((t + 2 * WINDOW, HEAD), F32)],
        compiler_params=_cparams("parallel", "arbitrary"),
    )(p, p, p, p, p, p, p, bias, sink, dcat)


def _position():
    return lax.axis_index("x"), lax.axis_index("y"), lax.axis_index("c")


def _handshake(peers):
    barrier = pltpu.get_barrier_semaphore()
    for peer in peers:
        pl.semaphore_signal(barrier, inc=1, device_id=peer, device_id_type=MESH)
    pl.semaphore_wait(barrier, len(peers))


def _sequencer(name, collective_id, scratch_types):
    return functools.partial(
        pl.kernel, mesh=plsc.ScalarSubcoreMesh(axis_name="sc", num_cores=1), name=name,
        scratch_types=scratch_types, compiler_params=pltpu.CompilerParams(collective_id=collective_id))


def _all_gather(name, shard, collective_id):
    src = jax.new_ref(shard, memory_space=pltpu.MemorySpace.HBM)
    out = jax.empty_ref(jax.ShapeDtypeStruct((N_DEV,) + shard.shape, shard.dtype),
                        memory_space=pltpu.MemorySpace.HBM)

    @_sequencer(name, collective_id,
                (pltpu.SemaphoreType.DMA((7,)), pltpu.SemaphoreType.DMA((7,)), pltpu.SemaphoreType.DMA))
    def launch(send_sems, recv_sems, local_sem):
        x, y, c = _position()
        me, sibling = (x, y, c), (x, y, 1 - c)
        chips = [(1 - x, y), (x, 1 - y), (1 - x, 1 - y)]
        _handshake([sibling] + [(*chip, c) for chip in chips])

        def slot(px, py, pc):
            return out.at[4 * px + 2 * py + pc]

        def copy(k, block, to, from_src=False):
            return pltpu.make_async_remote_copy(
                src_ref=src if from_src else slot(*block), dst_ref=slot(*block),
                send_sem=send_sems.at[k], recv_sem=recv_sems.at[k], device_id=to, device_id_type=MESH)

        mine = pltpu.make_async_copy(src, slot(*me), local_sem)
        mine.start()
        first = [copy(0, me, sibling, from_src=True)]
        first += [copy(1 + j, me, (*chip, c), from_src=True) for j, chip in enumerate(chips)]
        for cp in first:
            cp.start()
        passed = [copy(4 + j, (*chip, c), sibling) for j, chip in enumerate(chips)]
        for j, chip in enumerate(chips):
            copy(1 + j, (*chip, c), me).wait_recv()
            passed[j].start()
        copy(0, sibling, me).wait_recv()
        for j, chip in enumerate(chips):
            copy(4 + j, (*chip, 1 - c), me).wait_recv()
        for cp in first + passed:
            cp.wait_send()
        mine.wait()

    launch()
    return out[...]


def _pair_exchange(name, stack):
    def body(s_ref, out_ref, send_sems, recv_sems):
        x, y, c = _position()
        sibling = (x, y, 1 - c)
        copies = [pltpu.make_async_remote_copy(
            src_ref=s_ref.at[2 * k + (1 - c)], dst_ref=out_ref.at[k], send_sem=send_sems.at[k],
            recv_sem=recv_sems.at[k], device_id=sibling, device_id_type=MESH) for k in range(4)]
        for cp in copies:
            cp.start()
        for cp in copies:
            cp.wait()

    return pl.pallas_call(
        body, name=name, in_specs=[ANY], out_specs=ANY,
        out_shape=jax.ShapeDtypeStruct((4,) + stack.shape[1:], stack.dtype),
        scratch_shapes=[pltpu.SemaphoreType.DMA((4,)), pltpu.SemaphoreType.DMA((4,))],
    )(stack)


def _pair_sum(name, stack, other, core):
    _, r, c = stack.shape
    tr = _tile(r, 256)

    def body(core_ref, a_ref, b_ref, o_ref):
        o_ref[...] = (a_ref[...].astype(F32) + b_ref[...].astype(F32)).astype(o_ref.dtype)

    grid_spec = pltpu.PrefetchScalarGridSpec(
        num_scalar_prefetch=1, grid=(4, r // tr),
        in_specs=[pl.BlockSpec((None, tr, c), lambda k, i, core_ref: (2 * k + core_ref[0], i, 0)),
                  pl.BlockSpec((None, tr, c), lambda k, i, core_ref: (k, i, 0))],
        out_specs=pl.BlockSpec((None, tr, c), lambda k, i, core_ref: (k, i, 0)))
    return pl.pallas_call(
        body, name=name, grid_spec=grid_spec, out_shape=jax.ShapeDtypeStruct((4, r, c), stack.dtype),
        compiler_params=_cparams("parallel", "parallel"),
    )(core, stack, other)


def _chip_exchange(name, sums, collective_id):
    src = jax.new_ref(sums, memory_space=pltpu.MemorySpace.HBM)
    out = jax.empty_ref(jax.ShapeDtypeStruct((3,) + sums.shape[1:], sums.dtype),
                        memory_space=pltpu.MemorySpace.HBM)

    @_sequencer(name, collective_id, (pltpu.SemaphoreType.DMA((3,)), pltpu.SemaphoreType.DMA((3,))))
    def launch(send_sems, recv_sems):
        x, y, c = _position()
        chips = [(1 - x, y), (x, 1 - y), (1 - x, 1 - y)]
        _handshake([(*chip, c) for chip in chips])
        copies = [pltpu.make_async_remote_copy(
            src_ref=src.at[2 * px + py], dst_ref=out.at[j], send_sem=send_sems.at[j],
            recv_sem=recv_sems.at[j], device_id=(px, py, c), device_id_type=MESH)
            for j, (px, py) in enumerate(chips)]
        for cp in copies:
            cp.start()
        for cp in copies:
            cp.wait()

    launch()
    return out[...]


def _small_all_reduce(name, part):
    r = part.shape[0]

    def body(x_ref, out_ref, gather, send_sems, recv_sems):
        x, y, c = _position()
        me = 4 * x + 2 * y + c
        gather[me] = x_ref[...]
        copies = []
        for k in range(1, N_DEV):
            peer = (x ^ (k >> 2), y ^ ((k >> 1) & 1), c ^ (k & 1))
            copies.append(pltpu.make_async_remote_copy(
                src_ref=x_ref, dst_ref=gather.at[me], send_sem=send_sems.at[k - 1],
                recv_sem=recv_sems.at[k - 1], device_id=peer, device_id_type=MESH))
        for cp in copies:
            cp.start()
        for k in range(1, N_DEV):
            peer_slot = 4 * (x ^ (k >> 2)) + 2 * (y ^ ((k >> 1) & 1)) + (c ^ (k & 1))
            pltpu.make_async_remote_copy(
                src_ref=x_ref, dst_ref=gather.at[peer_slot], send_sem=send_sems.at[k - 1],
                recv_sem=recv_sems.at[k - 1], device_id=(x, y, c), device_id_type=MESH).wait()
        acc = gather[0]
        for j in range(1, N_DEV):
            acc = acc + gather[j]
        out_ref[...] = acc

    vm = pl.BlockSpec(memory_space=pltpu.VMEM)
    return pl.pallas_call(
        body, name=name, in_specs=[vm], out_specs=vm, out_shape=jax.ShapeDtypeStruct((r, LANES), F32),
        scratch_shapes=[pltpu.VMEM((N_DEV, r, LANES), F32), pltpu.SemaphoreType.DMA((7,)),
                        pltpu.SemaphoreType.DMA((7,))],
    )(part)


def _adam_math(w, g, m, v):
    m = ADAM_B1 * m + (1.0 - ADAM_B1) * g
    v = ADAM_B2 * v + (1.0 - ADAM_B2) * jnp.square(g)
    m_hat = m / (1.0 - ADAM_B1 ** ADAM_STEP)
    v_hat = v / (1.0 - ADAM_B2 ** ADAM_STEP)
    delta = -ADAM_LR * (m_hat / (jnp.sqrt(v_hat) + ADAM_EPS) + ADAM_WD * w)
    return delta, m, v


def _adam_shard(name, w, m, v, sums, recv, chip, deps=()):
    r, c = w.shape
    tr = _tile(r, 128)

    def body(chip_ref, w_ref, m_ref, v_ref, own_ref, r0_ref, r1_ref, r2_ref, *rest):
        g_out, d_out, m_out, v_out = rest[-4:]
        g = ((own_ref[...].astype(F32) + r0_ref[...].astype(F32)) + r1_ref[...].astype(F32)) + r2_ref[...].astype(F32)
        delta, m_new, v_new = _adam_math(w_ref[...], g, m_ref[...], v_ref[...])
        g_out[...] = g
        d_out[...] = delta
        m_out[...] = m_new
        v_out[...] = v_new

    plain = pl.BlockSpec((tr, c), lambda i, chip_ref: (i, 0))
    piece = lambda j: pl.BlockSpec((None, tr, c), lambda i, chip_ref: (j, i, 0))
    grid_spec = pltpu.PrefetchScalarGridSpec(
        num_scalar_prefetch=1, grid=(r // tr,),
        in_specs=[plain, plain, plain,
                  pl.BlockSpec((None, tr, c), lambda i, chip_ref: (chip_ref[0], i, 0)),
                  piece(0), piece(1), piece(2)] + [ANY] * len(deps),
        out_specs=[plain] * 4)
    shape = jax.ShapeDtypeStruct((r, c), F32)
    return pl.pallas_call(
        body, name=name, grid_spec=grid_spec, out_shape=[shape] * 4, compiler_params=_cparams("parallel"),
    )(chip, w, m, v, sums, recv, recv, recv, *deps)


def _adam_small(name, w, g, m, v):
    r = w.shape[0]

    def body(w_ref, g_ref, m_ref, v_ref, d_out, m_out, v_out):
        delta, m_new, v_new = _adam_math(w_ref[...], g_ref[...], m_ref[...], v_ref[...])
        d_out[...] = delta
        m_out[...] = m_new
        v_out[...] = v_new

    vm = pl.BlockSpec(memory_space=pltpu.VMEM)
    shape = jax.ShapeDtypeStruct((r, LANES), F32)
    return pl.pallas_call(body, name=name, in_specs=[vm] * 4, out_specs=[vm] * 3, out_shape=[shape] * 3)(w, g, m, v)


def _reduce_scatter(tag, grad_stack, core, collective_id):
    other = _pair_exchange("rs_pair_" + tag, grad_stack)
    sums = _pair_sum("rs_sum_" + tag, grad_stack, other, core)
    return sums, _chip_exchange("rs_chip_" + tag, sums, collective_id)


def _pack(arrays):
    flat = jnp.concatenate([a.reshape(-1).astype(F32) for a in arrays])
    rows = -(-flat.shape[0] // LANES)
    rows = -(-rows // 8) * 8
    return jnp.pad(flat, (0, rows * LANES - flat.shape[0])).reshape(rows, LANES)


def _unpack(packed, like):
    flat = packed.reshape(-1)
    out, off = [], 0
    for a in like:
        out.append(flat[off:off + a.size].reshape(a.shape))
        off += a.size
    return out


SMALL = ("pre_norm_ffn1", "post_norm_ffn1", "pre_norm_mix", "post_norm_mix", "hgrn_lower_bounds_fwd",
         "hgrn_lower_bounds_bwd", "hgrn_out_norm", "attn_sink", "pre_norm_ffn2", "post_norm_ffn2", "rel_bias_table")
BIG = ("w_ffn1_gate_up", "w_ffn1_down", "w_mix_in", "w_mix_out", "w_ffn2_gate_up", "w_ffn2_down")
AG_ID = {n: 1 + i for i, n in enumerate(BIG)}
RS_ID = {n: 7 + i for i, n in enumerate(BIG)}
ORDER = ("pre_norm_ffn1", "post_norm_ffn1", "w_ffn1_gate_up", "w_ffn1_down", "pre_norm_mix", "post_norm_mix",
         "w_mix_in", "hgrn_lower_bounds_fwd", "hgrn_lower_bounds_bwd", "hgrn_out_norm", "attn_sink", "w_mix_out",
         "pre_norm_ffn2", "post_norm_ffn2", "w_ffn2_gate_up", "w_ffn2_down", "rel_bias_table")


def kernel(x, pre_norm_ffn1, post_norm_ffn1, w_ffn1_gate_up, w_ffn1_down, pre_norm_mix, post_norm_mix, w_mix_in, hgrn_lower_bounds_fwd, hgrn_lower_bounds_bwd, hgrn_out_norm, attn_sink, w_mix_out, pre_norm_ffn2, post_norm_ffn2, w_ffn2_gate_up, w_ffn2_down, rel_bias_table, loss_target, m_pre_norm_ffn1, m_post_norm_ffn1, m_w_ffn1_gate_up, m_w_ffn1_down, m_pre_norm_mix, m_post_norm_mix, m_w_mix_in, m_hgrn_lower_bounds_fwd, m_hgrn_lower_bounds_bwd, m_hgrn_out_norm, m_attn_sink, m_w_mix_out, m_pre_norm_ffn2, m_post_norm_ffn2, m_w_ffn2_gate_up, m_w_ffn2_down, m_rel_bias_table, v_pre_norm_ffn1, v_post_norm_ffn1, v_w_ffn1_gate_up, v_w_ffn1_down, v_pre_norm_mix, v_post_norm_mix, v_w_mix_in, v_hgrn_lower_bounds_fwd, v_hgrn_lower_bounds_bwd, v_hgrn_out_norm, v_attn_sink, v_w_mix_out, v_pre_norm_ffn2, v_post_norm_ffn2, v_w_ffn2_gate_up, v_w_ffn2_down, v_rel_bias_table):
    args = dict(locals())
    wts = {n: args[n] for n in ORDER}
    mom = {n: args["m_" + n] for n in ORDER}
    var = {n: args["v_" + n] for n in ORDER}

    x0 = x[0]
    target = loss_target[0]
    t, d = x0.shape
    n_hgrn = d // 2 // HEAD
    n_attn = (d - d // 2) // HEAD
    core = lax.axis_index("c").astype(jnp.int32).reshape(1)
    chip = (2 * lax.axis_index("x") + lax.axis_index("y")).astype(jnp.int32).reshape(1)

    full = {n: _all_gather("ag_" + n, wts[n][0].astype(BF16), AG_ID[n]) for n in BIG}
    w_gu1, w_gu2 = full["w_ffn1_gate_up"], full["w_ffn2_gate_up"]
    w_d1 = full["w_ffn1_down"].reshape(-1, d)
    w_d2 = full["w_ffn2_down"].reshape(-1, d)
    w_out = full["w_mix_out"].reshape(-1, d)
    w_in = full["w_mix_in"]

    g = {n: wts[n] for n in SMALL}
    lb_f = jax.nn.softmax(g["hgrn_lower_bounds_fwd"], axis=0)[0:1]
    lb_b = jax.nn.softmax(g["hgrn_lower_bounds_bwd"], axis=0)[0:1]
    bucket_idx = jnp.asarray(_t5_bucket_index())
    bias = _bias_build(g["rel_bias_table"], bucket_idx)

    n1 = _pre_norm("pre_norm1", x0, g["pre_norm_ffn1"])
    gu1 = _matmul("ffn1_gate_up", n1, w_gu1, mode="nn", stack=True, out_dtype=F32)
    a1 = _swiglu_fwd("ffn1_act", gu1)
    ff1 = _matmul("ffn1_down", a1, w_d1, mode="nn", out_dtype=F32)
    x1, h = _post_res_pre("res1", x0, ff1, g["post_norm_ffn1"], g["pre_norm_mix"], 0.5)
    p = _matmul("mix_in", h, w_in, mode="nn", stack=True, out_dtype=F32)
    y_h, o_raw = _hgrn_fwd(p, lb_f, lb_b, g["hgrn_out_norm"], n_hgrn)
    y_a = _attn_fwd(p, bias, g["attn_sink"], n_hgrn, n_attn)
    cat = jnp.concatenate([y_h, y_a], axis=1)
    mixed = _matmul("mix_out", cat, w_out, mode="nn", out_dtype=F32)
    x2, n2 = _post_res_pre("res2", x1, mixed, g["post_norm_mix"], g["pre_norm_ffn2"], 1.0)
    gu2 = _matmul("ffn2_gate_up", n2, w_gu2, mode="nn", stack=True, out_dtype=F32)
    a2 = _swiglu_fwd("ffn2_act", gu2)
    ff2 = _matmul("ffn2_down", a2, w_d2, mode="nn", out_dtype=F32)
    dy3, loss_part = _post_res_loss("res3_loss", x2, ff2, g["post_norm_ffn2"], target, 0.5)

    small_grad = {}
    scattered = {}

    def scatter(name, grad_stack):
        scattered[name] = _reduce_scatter(name, grad_stack, core, RS_ID[name])
        return [scattered[name][0]]

    def ffn_bwd(tag, dy, ff, a, gu, n_in, x_in, w_gu, w_d, post_name, pre_name, gu_name, d_name):
        dff, small_grad[post_name] = _post_bwd("post_bwd" + tag, dy, ff, g[post_name], 0.5)
        da = _matmul("d_act" + tag, dff, w_d, mode="nt", out_dtype=F32)
        dep = scatter(d_name, _matmul("dw_down" + tag, a, dff, mode="tn", out_dtype=BF16).reshape(N_DEV, -1, d))
        dgu = _swiglu_bwd("act_bwd" + tag, da, gu, deps=dep)
        dn = _matmul("d_norm" + tag, dgu, w_gu, mode="nt", stack=True, out_dtype=F32)
        dep = scatter(gu_name, _matmul("dw_gate_up" + tag, n_in, dgu, mode="tn", stack=True, out_dtype=BF16))
        dx, small_grad[pre_name] = _pre_bwd("pre_bwd" + tag, dn, x_in, g[pre_name], dy, deps=dep)
        return dx

    dx2 = ffn_bwd("2", dy3, ff2, a2, gu2, n2, x2, w_gu2, w_d2, "post_norm_ffn2", "pre_norm_ffn2",
                  "w_ffn2_gate_up", "w_ffn2_down")

    dmixed, small_grad["post_norm_mix"] = _post_bwd("post_bwd_mix", dx2, mixed, g["post_norm_mix"], 1.0)
    dcat = _matmul("d_cat", dmixed, w_out, mode="nt", out_dtype=F32)
    dep = scatter("w_mix_out", _matmul("dw_mix_out", cat, dmixed, mode="tn", out_dtype=BF16).reshape(N_DEV, -1, d))
    dq_h, di_h, dzf, dzb, dg_h, dlb_f, dlb_b, small_grad["hgrn_out_norm"] = _hgrn_bwd(
        p, o_raw, dcat, lb_f, lb_b, g["hgrn_out_norm"], n_hgrn)
    dq_a, dk_a, dv_a, dbias, dsink_rows = _attn_bwd(p, dcat, bias, g["attn_sink"], n_hgrn, n_attn)
    dp = jnp.concatenate([dq_h, di_h, dzf, dzb, dg_h, dq_a, dk_a, dv_a], axis=1)
    dh = _matmul("d_h", dp, w_in, mode="nt", stack=True, out_dtype=F32, deps=dep)
    dep = scatter("w_mix_in", _matmul("dw_mix_in", h, dp, mode="tn", stack=True, out_dtype=BF16))
    dx1, small_grad["pre_norm_mix"] = _pre_bwd("pre_bwd_mix", dh, x1, g["pre_norm_mix"], dx2, deps=dep)

    dx0 = ffn_bwd("1", dx1, ff1, a1, gu1, n1, x0, w_gu1, w_d1, "post_norm_ffn1", "pre_norm_ffn1",
                  "w_ffn1_gate_up", "w_ffn1_down")

    def lb_grad(dlb, lb):
        da0 = dlb * lb * (1.0 - lb)
        return jnp.concatenate([da0, -da0], axis=0)

    small_grad["hgrn_lower_bounds_fwd"] = lb_grad(dlb_f, lb_f)
    small_grad["hgrn_lower_bounds_bwd"] = lb_grad(dlb_b, lb_b)
    small_grad["attn_sink"] = dsink_rows[:, :, 0].reshape(1, n_attn)
    small_grad["rel_bias_table"] = jnp.transpose(_bias_reduce(dbias, bucket_idx)[:, :, 0])

    parts = [small_grad[n] for n in SMALL] + [loss_part[:, 0:1]]
    red = _small_all_reduce("small_all_reduce", _pack(parts))
    red_list = _unpack(red, parts)
    loss = red_list[-1].reshape(())
    sg = dict(zip(SMALL, red_list[:-1]))
    like = [wts[n] for n in SMALL]
    d_s, m_s, v_s = _adam_small("adam_small", _pack(like), _pack([sg[n] for n in SMALL]),
                                _pack([mom[n] for n in SMALL]), _pack([var[n] for n in SMALL]))
    grads = dict(sg)
    delta = dict(zip(SMALL, _unpack(d_s, like)))
    new_m = dict(zip(SMALL, _unpack(m_s, like)))
    new_v = dict(zip(SMALL, _unpack(v_s, like)))

    dep = []
    for n in ("w_ffn2_down", "w_ffn2_gate_up", "w_mix_out", "w_mix_in", "w_ffn1_down", "w_ffn1_gate_up"):
        sums, recv = scattered[n]
        gr, de, nm, nv = _adam_shard("adam_" + n, wts[n][0], mom[n][0], var[n][0], sums, recv, chip, deps=dep)
        grads[n], delta[n], new_m[n], new_v[n] = gr[None], de[None], nm[None], nv[None]
        dep = [gr]

    return (loss, dx0[None], *[grads[n] for n in ORDER], *[delta[n] for n in ORDER],
            *[new_m[n] for n in ORDER], *[new_v[n] for n in ORDER])
```

```python
import functools
import math

import numpy as np
import jax
import jax.numpy as jnp
from jax import lax
from jax.experimental import pallas as pl
from jax.experimental.pallas import tpu as pltpu
from jax.experimental.pallas import tpu_sc as plsc

F32 = jnp.float32
BF16 = jnp.bfloat16
HIGHEST = lax.Precision.HIGHEST
MESH = pl.DeviceIdType.MESH

N_DEV = 8
EPS = 1e-6
NEG_INF = -1e30
HEAD = 128
CHUNK = 64
WINDOW = 128
KEY_SPAN = 3 * WINDOW
KV_HEADS = 2
REL_BUCKETS = 32
REL_MAX_DIST = 128
ADAM_LR, ADAM_B1, ADAM_B2, ADAM_EPS, ADAM_WD, ADAM_STEP = 0.001, 0.9, 0.999, 1e-08, 0.01, 10
LANES = 128
VMEM_LIMIT = 56 * 1024 * 1024
ANY = pl.BlockSpec(memory_space=pl.ANY)


def _cparams(*sem):
    return pltpu.CompilerParams(dimension_semantics=sem if sem else None, vmem_limit_bytes=VMEM_LIMIT)


def _dot(a, b):
    return jnp.dot(a, b, preferred_element_type=F32)


def _dot_nt(a, b):
    return lax.dot_general(a, b, (((1,), (1,)), ((), ())), preferred_element_type=F32)


def _dot_tn(a, b):
    return lax.dot_general(a, b, (((0,), (0,)), ((), ())), preferred_element_type=F32)


def _tile(dim, target):
    for c in (target, 1024, 512, 256, 128):
        if c <= target and dim % c == 0:
            return c
    return dim


def _matmul(name, a, b, *, mode, out_dtype, stack=False, tm=1024, tn=1024, tk=512, deps=()):
    grp = 1
    if mode == "nn":
        m, kd = a.shape
        n = b.shape[0] * b.shape[2] if stack else b.shape[1]
    elif mode == "nt":
        m = a.shape[0]
        n, kd = (b.shape[1], b.shape[0] * b.shape[2]) if stack else b.shape
    else:
        kd, m = a.shape
        n = b.shape[1]
    if stack:
        n1 = b.shape[2] if mode != "tn" else n // N_DEV
        grp = 1 if n1 % LANES == 0 else 2
        assert (grp * n1) % LANES == 0
        if mode == "nt":
            tk = grp * n1
        else:
            tn = grp * n1
    tm = _tile(m, tm)
    if not (stack and mode in ("nn", "tn")):
        tn = _tile(n, tn)
    if not (stack and mode == "nt"):
        tk = _tile(kd, tk)
    nk = kd // tk
    grid = (m // tm, n // tn, nk)
    lead = None if grp == 1 else grp

    if mode == "nn":
        a_spec = pl.BlockSpec((tm, tk), lambda i, j, k: (i, k))
        if stack:
            b_spec = pl.BlockSpec((lead, tk, n1), lambda i, j, k: (j, k, 0))
        else:
            b_spec = pl.BlockSpec((tk, tn), lambda i, j, k: (k, j))
        dot = _dot
    elif mode == "nt":
        a_spec = pl.BlockSpec((tm, tk), lambda i, j, k: (i, k))
        if stack:
            b_spec = pl.BlockSpec((lead, tn, n1), lambda i, j, k: (k, j, 0))
        else:
            b_spec = pl.BlockSpec((tn, tk), lambda i, j, k: (j, k))
        dot = _dot_nt
    else:
        a_spec = pl.BlockSpec((tk, tm), lambda i, j, k: (k, i))
        b_spec = pl.BlockSpec((tk, tn), lambda i, j, k: (k, j))
        dot = _dot_tn
    if stack and mode == "tn":
        out_shape = jax.ShapeDtypeStruct((N_DEV, m, n1), out_dtype)
        o_spec = pl.BlockSpec((lead, tm, n1), lambda i, j, k: (j, i, 0))
    else:
        out_shape = jax.ShapeDtypeStruct((m, n), out_dtype)
        o_spec = pl.BlockSpec((tm, tn), lambda i, j, k: (i, j))
    b_grouped = stack and grp > 1 and mode != "tn"
    o_grouped = stack and grp > 1 and mode == "tn"

    def body(a_ref, b_ref, *rest):
        o_ref, acc_ref = rest[-2:]
        k = pl.program_id(2)

        @pl.when(k == 0)
        def _():
            acc_ref[...] = jnp.zeros_like(acc_ref)

        bmat = jnp.concatenate([b_ref[s] for s in range(grp)], axis=1) if b_grouped else b_ref[...]
        acc_ref[...] += dot(a_ref[...], bmat)

        @pl.when(k == nk - 1)
        def _():
            if o_grouped:
                for s in range(grp):
                    o_ref[s] = acc_ref[:, s * n1:(s + 1) * n1].astype(o_ref.dtype)
            else:
                o_ref[...] = acc_ref[...].astype(o_ref.dtype)

    return pl.pallas_call(
        body, name=name, grid=grid, in_specs=[a_spec, b_spec] + [ANY] * len(deps), out_specs=o_spec,
        out_shape=out_shape, scratch_shapes=[pltpu.VMEM((tm, tn), F32)],
        compiler_params=_cparams("parallel", "parallel", "arbitrary"),
    )(a, b, *deps)


ROWS = 256


def _rstd(xf):
    return lax.rsqrt(jnp.mean(xf * xf, axis=-1, keepdims=True) + EPS)


def _row_spec(t, d):
    return pl.BlockSpec((min(ROWS, t), d), lambda i: (i, 0))


def _vec_spec(d):
    return pl.BlockSpec((1, d), lambda i: (0, 0))


def _pre_norm(name, x, gain):
    t, d = x.shape

    def body(x_ref, g_ref, n_ref):
        xf = x_ref[...]
        n_ref[...] = (xf * _rstd(xf) * g_ref[...]).astype(BF16)

    return pl.pallas_call(
        body, name=name, grid=(t // min(ROWS, t),), in_specs=[_row_spec(t, d), _vec_spec(d)],
        out_specs=_row_spec(t, d), out_shape=jax.ShapeDtypeStruct((t, d), BF16),
        compiler_params=_cparams("parallel"),
    )(x, gain)


def _post_res_pre(name, x, ff, g_post, g_next, scale):
    t, d = x.shape

    def body(x_ref, ff_ref, gp_ref, gn_ref, xo_ref, n_ref):
        ff_ = ff_ref[...]
        xn = x_ref[...] + scale * (ff_ * _rstd(ff_) * gp_ref[...])
        xo_ref[...] = xn
        n_ref[...] = (xn * _rstd(xn) * gn_ref[...]).astype(BF16)

    return pl.pallas_call(
        body, name=name, grid=(t // min(ROWS, t),),
        in_specs=[_row_spec(t, d), _row_spec(t, d), _vec_spec(d), _vec_spec(d)],
        out_specs=[_row_spec(t, d), _row_spec(t, d)],
        out_shape=[jax.ShapeDtypeStruct((t, d), F32), jax.ShapeDtypeStruct((t, d), BF16)],
        compiler_params=_cparams("parallel"),
    )(x, ff, g_post, g_next)


def _post_res_loss(name, x, ff, g_post, target, scale):
    t, d = x.shape

    def body(x_ref, ff_ref, gp_ref, tg_ref, dy_ref, loss_ref):
        ff_ = ff_ref[...]
        err = x_ref[...] + scale * (ff_ * _rstd(ff_) * gp_ref[...]) - tg_ref[...]
        dy_ref[...] = err / d
        part = 0.5 * jnp.sum(jnp.mean(err * err, axis=-1, keepdims=True), axis=0, keepdims=True)

        @pl.when(pl.program_id(0) == 0)
        def _():
            loss_ref[...] = jnp.zeros_like(loss_ref)

        loss_ref[...] += jnp.broadcast_to(part, loss_ref.shape)

    return pl.pallas_call(
        body, name=name, grid=(t // min(ROWS, t),),
        in_specs=[_row_spec(t, d), _row_spec(t, d), _vec_spec(d), _row_spec(t, d)],
        out_specs=[_row_spec(t, d), _vec_spec(LANES)],
        out_shape=[jax.ShapeDtypeStruct((t, d), F32), jax.ShapeDtypeStruct((1, LANES), F32)],
        compiler_params=_cparams("arbitrary"),
    )(x, ff, g_post, target)


def _post_bwd(name, dy, ff, g_post, scale):
    t, d = dy.shape

    def body(dy_ref, ff_ref, gp_ref, dff_ref, dg_ref):
        ff_ = ff_ref[...]
        r = _rstd(ff_)
        xh = ff_ * r
        dyn = scale * dy_ref[...]
        dxh = dyn * gp_ref[...]
        dff_ref[...] = (r * (dxh - xh * jnp.mean(dxh * xh, axis=-1, keepdims=True))).astype(BF16)

        @pl.when(pl.program_id(0) == 0)
        def _():
            dg_ref[...] = jnp.zeros_like(dg_ref)

        dg_ref[...] += jnp.sum(dyn * xh, axis=0, keepdims=True)

    return pl.pallas_call(
        body, name=name, grid=(t // min(ROWS, t),),
        in_specs=[_row_spec(t, d), _row_spec(t, d), _vec_spec(d)],
        out_specs=[_row_spec(t, d), _vec_spec(d)],
        out_shape=[jax.ShapeDtypeStruct((t, d), BF16), jax.ShapeDtypeStruct((1, d), F32)],
        compiler_params=_cparams("arbitrary"),
    )(dy, ff, g_post)


def _pre_bwd(name, dn, x, g_pre, dy, deps=()):
    t, d = x.shape

    def body(dn_ref, x_ref, g_ref, dy_ref, *rest):
        dx_ref, dg_ref = rest[-2:]
        xf = x_ref[...]
        r = _rstd(xf)
        xh = xf * r
        dnf = dn_ref[...].astype(F32)
        dxh = dnf * g_ref[...]
        dx_ref[...] = dy_ref[...] + r * (dxh - xh * jnp.mean(dxh * xh, axis=-1, keepdims=True))

        @pl.when(pl.program_id(0) == 0)
        def _():
            dg_ref[...] = jnp.zeros_like(dg_ref)

        dg_ref[...] += jnp.sum(dnf * xh, axis=0, keepdims=True)

    return pl.pallas_call(
        body, name=name, grid=(t // min(ROWS, t),),
        in_specs=[_row_spec(t, d), _row_spec(t, d), _vec_spec(d), _row_spec(t, d)] + [ANY] * len(deps),
        out_specs=[_row_spec(t, d), _vec_spec(d)],
        out_shape=[jax.ShapeDtypeStruct((t, d), F32), jax.ShapeDtypeStruct((1, d), F32)],
        compiler_params=_cparams("arbitrary"),
    )(dn, x, g_pre, dy, *deps)


def _swiglu_fwd(name, gu):
    t, f2 = gu.shape
    f = f2 // 2
    tc = _tile(f, 512)
    nb = f // tc
    spec = lambda off: pl.BlockSpec((min(ROWS, t), tc), lambda i, j: (i, j + off))

    def body(g_ref, u_ref, a_ref):
        g = g_ref[...]
        a_ref[...] = (g * jax.nn.sigmoid(g) * u_ref[...]).astype(BF16)

    return pl.pallas_call(
        body, name=name, grid=(t // min(ROWS, t), nb), in_specs=[spec(0), spec(nb)], out_specs=spec(0),
        out_shape=jax.ShapeDtypeStruct((t, f), BF16), compiler_params=_cparams("parallel", "parallel"),
    )(gu, gu)


def _swiglu_bwd(name, da, gu, deps=()):
    t, f2 = gu.shape
    f = f2 // 2
    tc = _tile(f, 512)
    nb = f // tc
    spec = lambda off: pl.BlockSpec((min(ROWS, t), tc), lambda i, j: (i, j + off))

    def body(da_ref, g_ref, u_ref, *rest):
        dg_ref, du_ref = rest[-2:]
        g = g_ref[...]
        da_ = da_ref[...]
        sg = jax.nn.sigmoid(g)
        dg_ref[...] = (da_ * u_ref[...] * (sg * (1.0 + g * (1.0 - sg)))).astype(BF16)
        du_ref[...] = (da_ * (g * sg)).astype(BF16)

    dg, du = pl.pallas_call(
        body, name=name, grid=(t // min(ROWS, t), nb), in_specs=[spec(0), spec(0), spec(nb)] + [ANY] * len(deps),
        out_specs=[spec(0), spec(0)],
        out_shape=[jax.ShapeDtypeStruct((t, f), BF16), jax.ShapeDtypeStruct((t, f), BF16)],
        compiler_params=_cparams("parallel", "parallel"),
    )(da, gu, gu, *deps)
    return jnp.concatenate([dg, du], axis=1)


def _bdot(a, b, ca, cb, precision=None):
    return lax.dot_general(a, b, (((ca,), (cb,)), ((0,), (0,))), preferred_element_type=F32, precision=precision)


def _tri_masks(g):
    row = lax.broadcasted_iota(jnp.int32, (g, CHUNK, CHUNK), 1)
    col = lax.broadcasted_iota(jnp.int32, (g, CHUNK, CHUNK), 2)
    return col <= row, col >= row


def _hgrn_block(z, lb, q, v, cum_mat):
    sg = jax.nn.sigmoid(z)
    f = lb + (1.0 - lb) * sg
    lf = jnp.log(f)
    k = 1.0 - f
    a = _bdot(cum_mat, lf, 2, 1, HIGHEST)
    last = jnp.sum(lf, axis=1, keepdims=True)
    e_a = jnp.exp(a)
    e_na = jnp.exp(-a)
    e_t = jnp.exp(last - a)
    return dict(sg=sg, f=f, k=k, decay=jnp.exp(last), e_a=e_a, e_na=e_na, e_t=e_t,
                qd=q * e_a, kd=k * e_na, kt=k * e_t)


def _hgrn_states(state, kv, decay, order):
    entering = [None] * len(order)
    for g in order:
        entering[g] = state
        state = decay[g] * state + kv[g]
    return jnp.stack(entering, axis=0), state


def _hgrn_fwd(p, lb_f, lb_b, gain, n_heads):
    t = p.shape[0]
    w = n_heads * HEAD
    blk = min(8, t // CHUNK)
    rows_blk = blk * CHUNK
    n_blocks = t // rows_blk
    fin_rows = min(256, t)

    def body(q_ref, i_ref, zf_ref, zb_ref, g_ref, lbf_ref, lbb_ref, gain_ref, y_ref, o_ref, st_ref):
        low, up = _tri_masks(blk)
        m_low, m_up = low.astype(F32), up.astype(F32)
        o_ref[...] = jnp.zeros_like(o_ref)
        st_ref[...] = jnp.zeros_like(st_ref)

        def one(r0, z_ref, lb, slot, rev):
            rows = pl.ds(r0, rows_blk)
            split = lambda ref: ref[rows, :].reshape(blk, CHUNK, HEAD)
            q, v = split(q_ref), split(i_ref)
            c = _hgrn_block(split(z_ref), lb, q, v, m_up if rev else m_low)
            qd, kd, kt, vb = c["qd"].astype(BF16), c["kd"].astype(BF16), c["kt"].astype(BF16), v.astype(BF16)
            pm = jnp.where(up if rev else low, _bdot(qd, kd, 2, 2), 0.0).astype(BF16)
            kv = _bdot(vb, kt, 1, 1)
            order = range(blk - 1, -1, -1) if rev else range(blk)
            entering, st_ref[slot] = _hgrn_states(st_ref[slot], kv, c["decay"], order)
            o = _bdot(pm, vb, 2, 1) + _bdot(qd, entering.astype(BF16), 2, 2)
            o_ref[rows, :] += o.reshape(rows_blk, HEAD)

        def step(n, carry):
            one(pl.multiple_of(n * rows_blk, rows_blk), zf_ref, lbf_ref[...], 0, False)
            one(pl.multiple_of((n_blocks - 1 - n) * rows_blk, rows_blk), zb_ref, lbb_ref[...], 1, True)
            return carry

        lax.fori_loop(0, n_blocks, step, 0)

        def fin(n, carry):
            rows = pl.ds(pl.multiple_of(n * fin_rows, fin_rows), fin_rows)
            o = o_ref[rows, :]
            g = g_ref[rows, :]
            y_ref[rows, :] = (o * _rstd(o) * gain_ref[...] * (g * jax.nn.sigmoid(g))).astype(BF16)
            return carry

        lax.fori_loop(0, t // fin_rows, fin, 0)

    col = lambda grp: pl.BlockSpec((t, HEAD), lambda h: (0, grp * n_heads + h))
    vec = pl.BlockSpec((1, HEAD), lambda h: (0, h))
    out = pl.BlockSpec((t, HEAD), lambda h: (0, h))
    return pl.pallas_call(
        body, name="hgrn_fwd", grid=(n_heads,),
        in_specs=[col(0), col(1), col(2), col(3), col(4), vec, vec, vec],
        out_specs=[out, out],
        out_shape=[jax.ShapeDtypeStruct((t, w), BF16), jax.ShapeDtypeStruct((t, w), F32)],
        scratch_shapes=[pltpu.VMEM((2, HEAD, HEAD), F32)],
        compiler_params=_cparams("parallel"),
    )(p, p, p, p, p, lb_f, lb_b, gain)


def _hgrn_bwd(p, o_raw, dcat, lb_f, lb_b, gain, n_heads):
    t = p.shape[0]
    w = n_heads * HEAD
    n_chunks = t // CHUNK
    blk = min(8, n_chunks)
    rows_blk = blk * CHUNK
    n_blocks = t // rows_blk
    rb = min(256, t)

    def body(q_ref, i_ref, zf_ref, zb_ref, g_ref, o_ref, dy_ref, lbf_ref, lbb_ref, gain_ref,
             dq_ref, di_ref, dzf_ref, dzb_ref, dg_ref, dlbf_ref, dlbb_ref, dgain_ref,
             do_s, dq_s, dv_s, st_s, cur_s):
        low, up = _tri_masks(blk)
        m_low, m_up = low.astype(F32), up.astype(F32)
        rowid = lax.broadcasted_iota(jnp.int32, (blk, CHUNK, HEAD), 1)
        gain_v = gain_ref[...]

        def norm_bwd(n, dgain):
            rows = pl.ds(pl.multiple_of(n * rb, rb), rb)
            o = o_ref[rows, :]
            g = g_ref[rows, :]
            dy = dy_ref[rows, :]
            r = _rstd(o)
            oh = o * r
            sg = jax.nn.sigmoid(g)
            dg_ref[rows, :] = (dy * oh * gain_v * (sg * (1.0 + g * (1.0 - sg)))).astype(BF16)
            dno = dy * (g * sg)
            dxh = dno * gain_v
            do_s[rows, :] = r * (dxh - oh * jnp.mean(dxh * oh, axis=-1, keepdims=True))
            return dgain + jnp.sum(dno * oh, axis=0, keepdims=True)

        dgain_ref[...] = lax.fori_loop(0, t // rb, norm_bwd, jnp.zeros((1, HEAD), F32))
        dq_s[...] = jnp.zeros_like(dq_s)
        dv_s[...] = jnp.zeros_like(dv_s)

        def direction(z_ref, lb_ref, dz_ref, dlb_ref, rev):
            lb = lb_ref[...]
            cum_mat = m_up if rev else m_low
            cum_mat_t = m_low if rev else m_up
            mask = up if rev else low
            last_row = 0 if rev else CHUNK - 1

            order = range(blk - 1, -1, -1) if rev else range(blk)

            def rows_of(j):
                bidx = (n_blocks - 1 - j) if rev else j
                return bidx, pl.ds(pl.multiple_of(bidx * rows_blk, rows_blk), rows_blk)

            def load(rows):
                split = lambda ref: ref[rows, :].reshape(blk, CHUNK, HEAD)
                q, v = split(q_ref), split(i_ref)
                return q, v, _hgrn_block(split(z_ref), lb, q, v, cum_mat)

            def sweep_fwd(j, carry):
                bidx, rows = rows_of(j)
                _, v, c = load(rows)
                kv = _bdot(v.astype(BF16), c["kt"].astype(BF16), 1, 1)
                st_s[pl.ds(bidx * blk, blk)], cur_s[0] = _hgrn_states(cur_s[0], kv, c["decay"], order)
                return carry

            cur_s[...] = jnp.zeros_like(cur_s)
            dlb_ref[...] = jnp.zeros_like(dlb_ref)
            lax.fori_loop(0, n_blocks, sweep_fwd, 0)

            def sweep_bwd(jj, carry):
                bidx, rows = rows_of(n_blocks - 1 - jj)
                _, v, c = load(rows)
                st = st_s[pl.ds(bidx * blk, blk)]
                do = do_s[rows, :].reshape(blk, CHUNK, HEAD)
                qd, kd, kt, decay = c["qd"], c["kd"], c["kt"], c["decay"]
                qd_b, kd_b, kt_b = qd.astype(BF16), kd.astype(BF16), kt.astype(BF16)
                v_b, do_b, st_b = v.astype(BF16), do.astype(BF16), st.astype(BF16)
                pm = jnp.where(mask, _bdot(qd_b, kd_b, 2, 2), 0.0).astype(BF16)
                dpm = jnp.where(mask, _bdot(do_b, v_b, 2, 2), 0.0).astype(BF16)
                gq = _bdot(do_b, qd_b, 1, 1)
                dstate = cur_s[1]
                after = [None] * blk
                for g in reversed(order):
                    after[g] = dstate
                    dstate = gq[g] + decay[g] * dstate
                cur_s[1] = dstate
                dst = jnp.stack(after, axis=0)
                dst_b = dst.astype(BF16)
                dv = _bdot(pm, do_b, 1, 1) + _bdot(kt_b, dst_b, 2, 2)
                dqd = _bdot(dpm, kd_b, 2, 1) + _bdot(do_b, st_b, 2, 1)
                dkd = _bdot(dpm, qd_b, 1, 1)
                dkt = _bdot(v_b, dst_b, 2, 1)
                dlast = (jnp.sum(dkt * kt, axis=1, keepdims=True)
                         + decay * jnp.sum(dst * st, axis=1, keepdims=True))
                dq_s[rows, :] += (dqd * c["e_a"]).reshape(rows_blk, HEAD)
                dv_s[rows, :] += dv.reshape(rows_blk, HEAD)
                dk = dkd * c["e_na"] + dkt * c["e_t"]
                da = dqd * qd - dkd * kd - dkt * kt
                da = da + jnp.where(rowid == last_row, dlast, 0.0)
                dlf = _bdot(cum_mat_t, da, 2, 1, HIGHEST)
                df = dlf / c["f"] - dk
                sg = c["sg"]
                dz_ref[rows, :] = (df * (1.0 - lb) * (sg * (1.0 - sg))).reshape(rows_blk, HEAD).astype(BF16)
                dlb_ref[...] += jnp.sum((df * (1.0 - sg)).reshape(rows_blk, HEAD), axis=0, keepdims=True)
                return carry

            lax.fori_loop(0, n_blocks, sweep_bwd, 0)

        direction(zf_ref, lbf_ref, dzf_ref, dlbf_ref, False)
        direction(zb_ref, lbb_ref, dzb_ref, dlbb_ref, True)
        dq_ref[...] = dq_s[...].astype(BF16)
        di_ref[...] = dv_s[...].astype(BF16)

    col = lambda grp: pl.BlockSpec((t, HEAD), lambda h: (0, grp * n_heads + h))
    one = pl.BlockSpec((t, HEAD), lambda h: (0, h))
    vec = pl.BlockSpec((1, HEAD), lambda h: (0, h))
    big = jax.ShapeDtypeStruct((t, w), BF16)
    small = jax.ShapeDtypeStruct((1, w), F32)
    return pl.pallas_call(
        body, name="hgrn_bwd", grid=(n_heads,),
        in_specs=[col(0), col(1), col(2), col(3), col(4), one, one, vec, vec, vec],
        out_specs=[one] * 5 + [vec] * 3,
        out_shape=[big] * 5 + [small] * 3,
        scratch_shapes=[pltpu.VMEM((t, HEAD), F32), pltpu.VMEM((t, HEAD), F32), pltpu.VMEM((t, HEAD), F32),
                        pltpu.VMEM((n_chunks, HEAD, HEAD), F32), pltpu.VMEM((2, HEAD, HEAD), F32)],
        compiler_params=_cparams("parallel"),
    )(p, p, p, p, p, o_raw, dcat, lb_f, lb_b, gain)


def _t5_bucket_index():
    c = np.arange(WINDOW)[:, None]
    s = np.arange(KEY_SPAN)[None, :]
    rel = s - WINDOW - c
    nb = REL_BUCKETS // 2
    max_exact = nb // 2
    bucket = (rel > 0).astype(np.int32) * nb
    n = np.abs(rel)
    large = max_exact + (np.log(np.maximum(n, 1) / max_exact) / np.log(REL_MAX_DIST / max_exact)
                         * (nb - max_exact)).astype(np.int32)
    large = np.minimum(large, nb - 1)
    return bucket + np.where(n < max_exact, n, large).astype(np.int32)


def _bias_build(table, idx):
    n_attn = table.shape[1]

    def body(tab_ref, idx_ref, o_ref):
        h = pl.program_id(0)
        idx_v = idx_ref[...]
        acc = jnp.zeros((WINDOW, KEY_SPAN), F32)
        for b in range(REL_BUCKETS):
            acc = jnp.where(idx_v == b, tab_ref[b, h], acc)
        o_ref[...] = acc

    return pl.pallas_call(
        body, name="bias_build", grid=(n_attn,),
        in_specs=[pl.BlockSpec(memory_space=pltpu.SMEM), pl.BlockSpec((WINDOW, KEY_SPAN), lambda h: (0, 0))],
        out_specs=pl.BlockSpec((None, WINDOW, KEY_SPAN), lambda h: (h, 0, 0)),
        out_shape=jax.ShapeDtypeStruct((n_attn, WINDOW, KEY_SPAN), F32), compiler_params=_cparams("parallel"),
    )(table, idx)


def _bias_reduce(dbias, idx):
    n_attn = dbias.shape[0]

    def body(idx_ref, d_ref, o_ref):
        idx_v = idx_ref[...]
        dv = d_ref[...]
        rows = lax.broadcasted_iota(jnp.int32, (REL_BUCKETS, LANES), 0)
        acc = jnp.zeros((REL_BUCKETS, LANES), F32)
        for b in range(REL_BUCKETS):
            part = jnp.sum(jnp.where(idx_v == b, dv, 0.0), axis=1, keepdims=True)
            acc = jnp.where(rows == b, jnp.sum(part, axis=0, keepdims=True), acc)
        o_ref[...] = acc

    return pl.pallas_call(
        body, name="bias_reduce", grid=(n_attn,),
        in_specs=[pl.BlockSpec((WINDOW, KEY_SPAN), lambda h: (0, 0)),
                  pl.BlockSpec((None, WINDOW, KEY_SPAN), lambda h: (h, 0, 0))],
        out_specs=pl.BlockSpec((None, REL_BUCKETS, LANES), lambda h: (h, 0, 0)),
        out_shape=jax.ShapeDtypeStruct((n_attn, REL_BUCKETS, LANES), F32), compiler_params=_cparams("parallel"),
    )(idx, dbias)


def _attn_probs(qg, kb, bias, sink, valid):
    s = _dot_nt(qg, kb) / math.sqrt(HEAD) + bias
    s = jnp.where(valid, s, NEG_INF)
    m = jnp.maximum(jnp.max(s, axis=-1, keepdims=True), sink)
    e = jnp.exp(s - m)
    e_sink = jnp.exp(sink - m)
    den = jnp.sum(e, axis=-1, keepdims=True) + e_sink
    return e / den, e_sink / den


def _attn_valid(n, t):
    c = lax.broadcasted_iota(jnp.int32, (WINDOW, KEY_SPAN), 0)
    s = lax.broadcasted_iota(jnp.int32, (WINDOW, KEY_SPAN), 1)
    rel = s - WINDOW - c
    key_pos = n * WINDOW - WINDOW + s
    return (jnp.abs(rel) <= WINDOW) & (key_pos >= 0) & (key_pos < t)


def _attn_specs(t, n_hgrn, n_attn):
    grp = n_attn // KV_HEADS
    nb = t // WINDOW
    cq = 5 * n_hgrn
    ck = cq + n_attn
    cv = ck + KV_HEADS
    q_spec = pl.BlockSpec((WINDOW, grp * HEAD), lambda x, n: (n, cq // grp + x))
    kv = lambda base, off: pl.BlockSpec(
        (WINDOW, HEAD), lambda x, n: (jnp.clip(n + off, 0, nb - 1), base + x))
    band = [kv(ck, -1), kv(ck, 0), kv(ck, 1), kv(cv, -1), kv(cv, 0), kv(cv, 1)]
    bias_spec = pl.BlockSpec((grp, WINDOW, KEY_SPAN), lambda x, n: (x, 0, 0))
    sink_spec = pl.BlockSpec(memory_space=pltpu.SMEM)
    return grp, nb, q_spec, band, bias_spec, sink_spec


def _attn_fwd(p, bias, sink, n_hgrn, n_attn):
    t = p.shape[0]
    grp, nb, q_spec, band, bias_spec, sink_spec = _attn_specs(t, n_hgrn, n_attn)

    def body(q_ref, kp, kc, kn, vp, vc, vn, bias_ref, sink_ref, y_ref):
        x, n = pl.program_id(0), pl.program_id(1)
        kb = jnp.concatenate([kp[...], kc[...], kn[...]], axis=0).astype(BF16)
        vb = jnp.concatenate([vp[...], vc[...], vn[...]], axis=0).astype(BF16)
        valid = _attn_valid(n, t)
        for g in range(grp):
            qg = q_ref[:, g * HEAD:(g + 1) * HEAD].astype(BF16)
            pr, _ = _attn_probs(qg, kb, bias_ref[g], sink_ref[0, x * grp + g], valid)
            y_ref[:, g * HEAD:(g + 1) * HEAD] = _dot(pr.astype(BF16), vb).astype(BF16)

    return pl.pallas_call(
        body, name="attn_fwd", grid=(KV_HEADS, nb),
        in_specs=[q_spec] + band + [bias_spec, sink_spec],
        out_specs=pl.BlockSpec((WINDOW, grp * HEAD), lambda x, n: (n, x)),
        out_shape=jax.ShapeDtypeStruct((t, n_attn * HEAD), BF16),
        compiler_params=_cparams("parallel", "parallel"),
    )(p, p, p, p, p, p, p, bias, sink)


def _attn_bwd(p, dcat, bias, sink, n_hgrn, n_attn):
    t = p.shape[0]
    grp, nb, q_spec, band, bias_spec, sink_spec = _attn_specs(t, n_hgrn, n_attn)
    inv = 1.0 / math.sqrt(HEAD)

    def body(q_ref, kp, kc, kn, vp, vc, vn, bias_ref, sink_ref, do_ref,
             dq_ref, dk_ref, dv_ref, dbias_ref, dsink_ref, dk_s, dv_s):
        x, n = pl.program_id(0), pl.program_id(1)

        @pl.when(n == 0)
        def _():
            dk_s[...] = jnp.zeros_like(dk_s)
            dv_s[...] = jnp.zeros_like(dv_s)
            dbias_ref[...] = jnp.zeros_like(dbias_ref)
            dsink_ref[...] = jnp.zeros_like(dsink_ref)

        kb = jnp.concatenate([kp[...], kc[...], kn[...]], axis=0).astype(BF16)
        vb = jnp.concatenate([vp[...], vc[...], vn[...]], axis=0).astype(BF16)
        valid = _attn_valid(n, t)
        dkb = jnp.zeros((KEY_SPAN, HEAD), F32)
        dvb = jnp.zeros((KEY_SPAN, HEAD), F32)
        for g in range(grp):
            qg = q_ref[:, g * HEAD:(g + 1) * HEAD].astype(BF16)
            dog = do_ref[:, g * HEAD:(g + 1) * HEAD].astype(BF16)
            pr, p_sink = _attn_probs(qg, kb, bias_ref[g], sink_ref[0, x * grp + g], valid)
            dpr = _dot_nt(dog, vb)
            delta = jnp.sum(pr * dpr, axis=-1, keepdims=True)
            ds = pr * (dpr - delta)
            ds_b = ds.astype(BF16)
            dq_ref[:, g * HEAD:(g + 1) * HEAD] = (_dot(ds_b, kb) * inv).astype(BF16)
            dkb = dkb + _dot_tn(ds_b, qg) * inv
            dvb = dvb + _dot_tn(pr.astype(BF16), dog)
            dbias_ref[g] += ds
            dsink_ref[g:g + 1, :] += jnp.broadcast_to(
                jnp.sum(-p_sink * delta, axis=0, keepdims=True), (1, WINDOW))
        rows = pl.ds(pl.multiple_of(n * WINDOW, WINDOW), KEY_SPAN)
        dk_s[rows, :] += dkb
        dv_s[rows, :] += dvb

        @pl.when(n == nb - 1)
        def _():
            dk_ref[...] = dk_s[pl.ds(WINDOW, t), :].astype(BF16)
            dv_ref[...] = dv_s[pl.ds(WINDOW, t), :].astype(BF16)

    do_spec = pl.BlockSpec((WINDOW, grp * HEAD), lambda x, n: (n, n_hgrn // grp + x))
    kv_out = pl.BlockSpec((t, HEAD), lambda x, n: (0, x))
    return pl.pallas_call(
        body, name="attn_bwd", grid=(KV_HEADS, nb),
        in_specs=[q_spec] + band + [bias_spec, sink_spec, do_spec],
        out_specs=[pl.BlockSpec((WINDOW, grp * HEAD), lambda x, n: (n, x)), kv_out, kv_out,
                   bias_spec, pl.BlockSpec((None, grp, WINDOW), lambda x, n: (x, 0, 0))],
        out_shape=[jax.ShapeDtypeStruct((t, n_attn * HEAD), BF16),
                   jax.ShapeDtypeStruct((t, KV_HEADS * HEAD), BF16),
                   jax.ShapeDtypeStruct((t, KV_HEADS * HEAD), BF16),
                   jax.ShapeDtypeStruct((n_attn, WINDOW, KEY_SPAN), F32),
                   jax.ShapeDtypeStruct((KV_HEADS, grp, WINDOW), F32)],
        scratch_shapes=[pltpu.VMEM((t + 2 * WINDOW, HEAD), F32), pltpu.VMEM((t + 2 * WINDOW, HEAD), F32)],
        compiler_params=_cparams("parallel", "arbitrary"),
    )(p, p, p, p, p, p, p, bias, sink, dcat)


def _position():
    return lax.axis_index("x"), lax.axis_index("y"), lax.axis_index("c")


def _handshake(peers):
    barrier = pltpu.get_barrier_semaphore()
    for peer in peers:
        pl.semaphore_signal(barrier, inc=1, device_id=peer, device_id_type=MESH)
    pl.semaphore_wait(barrier, len(peers))


def _sequencer(name, collective_id, scratch_types):
    return functools.partial(
        pl.kernel, mesh=plsc.ScalarSubcoreMesh(axis_name="sc", num_cores=1), name=name,
        scratch_types=scratch_types, compiler_params=pltpu.CompilerParams(collective_id=collective_id))


def _all_gather(name, shard, collective_id):
    src = jax.new_ref(shard, memory_space=pltpu.MemorySpace.HBM)
    out = jax.empty_ref(jax.ShapeDtypeStruct((N_DEV,) + shard.shape, shard.dtype),
                        memory_space=pltpu.MemorySpace.HBM)

    @_sequencer(name, collective_id,
                (pltpu.SemaphoreType.DMA((7,)), pltpu.SemaphoreType.DMA((7,)), pltpu.SemaphoreType.DMA))
    def launch(send_sems, recv_sems, local_sem):
        x, y, c = _position()
        me, sibling = (x, y, c), (x, y, 1 - c)
        chips = [(1 - x, y), (x, 1 - y), (1 - x, 1 - y)]
        _handshake([sibling] + [(*chip, c) for chip in chips])

        def slot(px, py, pc):
            return out.at[4 * px + 2 * py + pc]

        def copy(k, block, to, from_src=False):
            return pltpu.make_async_remote_copy(
                src_ref=src if from_src else slot(*block), dst_ref=slot(*block),
                send_sem=send_sems.at[k], recv_sem=recv_sems.at[k], device_id=to, device_id_type=MESH)

        mine = pltpu.make_async_copy(src, slot(*me), local_sem)
        mine.start()
        first = [copy(0, me, sibling, from_src=True)]
        first += [copy(1 + j, me, (*chip, c), from_src=True) for j, chip in enumerate(chips)]
        for cp in first:
            cp.start()
        passed = [copy(4 + j, (*chip, c), sibling) for j, chip in enumerate(chips)]
        for j, chip in enumerate(chips):
            copy(1 + j, (*chip, c), me).wait_recv()
            passed[j].start()
        copy(0, sibling, me).wait_recv()
        for j, chip in enumerate(chips):
            copy(4 + j, (*chip, 1 - c), me).wait_recv()
        for cp in first + passed:
            cp.wait_send()
        mine.wait()

    launch()
    return out[...]


def _pair_exchange(name, stack):
    def body(s_ref, out_ref, send_sems, recv_sems):
        x, y, c = _position()
        sibling = (x, y, 1 - c)
        copies = [pltpu.make_async_remote_copy(
            src_ref=s_ref.at[2 * k + (1 - c)], dst_ref=out_ref.at[k], send_sem=send_sems.at[k],
            recv_sem=recv_sems.at[k], device_id=sibling, device_id_type=MESH) for k in range(4)]
        for cp in copies:
            cp.start()
        for cp in copies:
            cp.wait()

    return pl.pallas_call(
        body, name=name, in_specs=[ANY], out_specs=ANY,
        out_shape=jax.ShapeDtypeStruct((4,) + stack.shape[1:], stack.dtype),
        scratch_shapes=[pltpu.SemaphoreType.DMA((4,)), pltpu.SemaphoreType.DMA((4,))],
    )(stack)


def _pair_sum(name, stack, other, core):
    _, r, c = stack.shape
    tr = _tile(r, 256)

    def body(core_ref, a_ref, b_ref, o_ref):
        o_ref[...] = (a_ref[...].astype(F32) + b_ref[...].astype(F32)).astype(o_ref.dtype)

    grid_spec = pltpu.PrefetchScalarGridSpec(
        num_scalar_prefetch=1, grid=(4, r // tr),
        in_specs=[pl.BlockSpec((None, tr, c), lambda k, i, core_ref: (2 * k + core_ref[0], i, 0)),
                  pl.BlockSpec((None, tr, c), lambda k, i, core_ref: (k, i, 0))],
        out_specs=pl.BlockSpec((None, tr, c), lambda k, i, core_ref: (k, i, 0)))
    return pl.pallas_call(
        body, name=name, grid_spec=grid_spec, out_shape=jax.ShapeDtypeStruct((4, r, c), stack.dtype),
        compiler_params=_cparams("parallel", "parallel"),
    )(core, stack, other)


def _chip_exchange(name, sums, collective_id):
    src = jax.new_ref(sums, memory_space=pltpu.MemorySpace.HBM)
    out = jax.empty_ref(jax.ShapeDtypeStruct((3,) + sums.shape[1:], sums.dtype),
                        memory_space=pltpu.MemorySpace.HBM)

    @_sequencer(name, collective_id, (pltpu.SemaphoreType.DMA((3,)), pltpu.SemaphoreType.DMA((3,))))
    def launch(send_sems, recv_sems):
        x, y, c = _position()
        chips = [(1 - x, y), (x, 1 - y), (1 - x, 1 - y)]
        _handshake([(*chip, c) for chip in chips])
        copies = [pltpu.make_async_remote_copy(
            src_ref=src.at[2 * px + py], dst_ref=out.at[j], send_sem=send_sems.at[j],
            recv_sem=recv_sems.at[j], device_id=(px, py, c), device_id_type=MESH)
            for j, (px, py) in enumerate(chips)]
        for cp in copies:
            cp.start()
        for cp in copies:
            cp.wait()

    launch()
    return out[...]


def _small_all_reduce(name, part):
    r = part.shape[0]

    def body(x_ref, out_ref, gather, send_sems, recv_sems):
        x, y, c = _position()
        me = 4 * x + 2 * y + c
        gather[me] = x_ref[...]
        copies = []
        for k in range(1, N_DEV):
            peer = (x ^ (k >> 2), y ^ ((k >> 1) & 1), c ^ (k & 1))
            copies.append(pltpu.make_async_remote_copy(
                src_ref=x_ref, dst_ref=gather.at[me], send_sem=send_sems.at[k - 1],
                recv_sem=recv_sems.at[k - 1], device_id=peer, device_id_type=MESH))
        for cp in copies:
            cp.start()
        for k in range(1, N_DEV):
            peer_slot = 4 * (x ^ (k >> 2)) + 2 * (y ^ ((k >> 1) & 1)) + (c ^ (k & 1))
            pltpu.make_async_remote_copy(
                src_ref=x_ref, dst_ref=gather.at[peer_slot], send_sem=send_sems.at[k - 1],
                recv_sem=recv_sems.at[k - 1], device_id=(x, y, c), device_id_type=MESH).wait()
        acc = gather[0]
        for j in range(1, N_DEV):
            acc = acc + gather[j]
        out_ref[...] = acc

    vm = pl.BlockSpec(memory_space=pltpu.VMEM)
    return pl.pallas_call(
        body, name=name, in_specs=[vm], out_specs=vm, out_shape=jax.ShapeDtypeStruct((r, LANES), F32),
        scratch_shapes=[pltpu.VMEM((N_DEV, r, LANES), F32), pltpu.SemaphoreType.DMA((7,)),
                        pltpu.SemaphoreType.DMA((7,))],
    )(part)


def _adam_math(w, g, m, v):
    m = ADAM_B1 * m + (1.0 - ADAM_B1) * g
    v = ADAM_B2 * v + (1.0 - ADAM_B2) * jnp.square(g)
    m_hat = m / (1.0 - ADAM_B1 ** ADAM_STEP)
    v_hat = v / (1.0 - ADAM_B2 ** ADAM_STEP)
    delta = -ADAM_LR * (m_hat / (jnp.sqrt(v_hat) + ADAM_EPS) + ADAM_WD * w)
    return delta, m, v


def _adam_shard(name, w, m, v, sums, recv, chip, deps=()):
    r, c = w.shape
    tr = _tile(r, 128)

    def body(chip_ref, w_ref, m_ref, v_ref, own_ref, r0_ref, r1_ref, r2_ref, *rest):
        g_out, d_out, m_out, v_out = rest[-4:]
        g = ((own_ref[...].astype(F32) + r0_ref[...].astype(F32)) + r1_ref[...].astype(F32)) + r2_ref[...].astype(F32)
        delta, m_new, v_new = _adam_math(w_ref[...], g, m_ref[...], v_ref[...])
        g_out[...] = g
        d_out[...] = delta
        m_out[...] = m_new
        v_out[...] = v_new

    plain = pl.BlockSpec((tr, c), lambda i, chip_ref: (i, 0))
    piece = lambda j: pl.BlockSpec((None, tr, c), lambda i, chip_ref: (j, i, 0))
    grid_spec = pltpu.PrefetchScalarGridSpec(
        num_scalar_prefetch=1, grid=(r // tr,),
        in_specs=[plain, plain, plain,
                  pl.BlockSpec((None, tr, c), lambda i, chip_ref: (chip_ref[0], i, 0)),
                  piece(0), piece(1), piece(2)] + [ANY] * len(deps),
        out_specs=[plain] * 4)
    shape = jax.ShapeDtypeStruct((r, c), F32)
    return pl.pallas_call(
        body, name=name, grid_spec=grid_spec, out_shape=[shape] * 4, compiler_params=_cparams("parallel"),
    )(chip, w, m, v, sums, recv, recv, recv, *deps)


def _adam_small(name, w, g, m, v):
    r = w.shape[0]

    def body(w_ref, g_ref, m_ref, v_ref, d_out, m_out, v_out):
        delta, m_new, v_new = _adam_math(w_ref[...], g_ref[...], m_ref[...], v_ref[...])
        d_out[...] = delta
        m_out[...] = m_new
        v_out[...] = v_new

    vm = pl.BlockSpec(memory_space=pltpu.VMEM)
    shape = jax.ShapeDtypeStruct((r, LANES), F32)
    return pl.pallas_call(body, name=name, in_specs=[vm] * 4, out_specs=[vm] * 3, out_shape=[shape] * 3)(w, g, m, v)


def _reduce_scatter(tag, grad_stack, core, collective_id):
    other = _pair_exchange("rs_pair_" + tag, grad_stack)
    sums = _pair_sum("rs_sum_" + tag, grad_stack, other, core)
    return sums, _chip_exchange("rs_chip_" + tag, sums, collective_id)


def _pack(arrays):
    flat = jnp.concatenate([a.reshape(-1).astype(F32) for a in arrays])
    rows = -(-flat.shape[0] // LANES)
    rows = -(-rows // 8) * 8
    return jnp.pad(flat, (0, rows * LANES - flat.shape[0])).reshape(rows, LANES)


def _unpack(packed, like):
    flat = packed.reshape(-1)
    out, off = [], 0
    for a in like:
        out.append(flat[off:off + a.size].reshape(a.shape))
        off += a.size
    return out


SMALL = ("pre_norm_ffn1", "post_norm_ffn1", "pre_norm_mix", "post_norm_mix", "hgrn_lower_bounds_fwd",
         "hgrn_lower_bounds_bwd", "hgrn_out_norm", "attn_sink", "pre_norm_ffn2", "post_norm_ffn2", "rel_bias_table")
BIG = ("w_ffn1_gate_up", "w_ffn1_down", "w_mix_in", "w_mix_out", "w_ffn2_gate_up", "w_ffn2_down")
AG_ID = {n: 1 + i for i, n in enumerate(BIG)}
RS_ID = {n: 7 + i for i, n in enumerate(BIG)}
ORDER = ("pre_norm_ffn1", "post_norm_ffn1", "w_ffn1_gate_up", "w_ffn1_down", "pre_norm_mix", "post_norm_mix",
         "w_mix_in", "hgrn_lower_bounds_fwd", "hgrn_lower_bounds_bwd", "hgrn_out_norm", "attn_sink", "w_mix_out",
         "pre_norm_ffn2", "post_norm_ffn2", "w_ffn2_gate_up", "w_ffn2_down", "rel_bias_table")


def kernel(x, pre_norm_ffn1, post_norm_ffn1, w_ffn1_gate_up, w_ffn1_down, pre_norm_mix, post_norm_mix, w_mix_in, hgrn_lower_bounds_fwd, hgrn_lower_bounds_bwd, hgrn_out_norm, attn_sink, w_mix_out, pre_norm_ffn2, post_norm_ffn2, w_ffn2_gate_up, w_ffn2_down, rel_bias_table, loss_target, m_pre_norm_ffn1, m_post_norm_ffn1, m_w_ffn1_gate_up, m_w_ffn1_down, m_pre_norm_mix, m_post_norm_mix, m_w_mix_in, m_hgrn_lower_bounds_fwd, m_hgrn_lower_bounds_bwd, m_hgrn_out_norm, m_attn_sink, m_w_mix_out, m_pre_norm_ffn2, m_post_norm_ffn2, m_w_ffn2_gate_up, m_w_ffn2_down, m_rel_bias_table, v_pre_norm_ffn1, v_post_norm_ffn1, v_w_ffn1_gate_up, v_w_ffn1_down, v_pre_norm_mix, v_post_norm_mix, v_w_mix_in, v_hgrn_lower_bounds_fwd, v_hgrn_lower_bounds_bwd, v_hgrn_out_norm, v_attn_sink, v_w_mix_out, v_pre_norm_ffn2, v_post_norm_ffn2, v_w_ffn2_gate_up, v_w_ffn2_down, v_rel_bias_table):
    args = dict(locals())
    wts = {n: args[n] for n in ORDER}
    mom = {n: args["m_" + n] for n in ORDER}
    var = {n: args["v_" + n] for n in ORDER}

    x0 = x[0]
    target = loss_target[0]
    t, d = x0.shape
    n_hgrn = d // 2 // HEAD
    n_attn = (d - d // 2) // HEAD
    core = lax.axis_index("c").astype(jnp.int32).reshape(1)
    chip = (2 * lax.axis_index("x") + lax.axis_index("y")).astype(jnp.int32).reshape(1)

    full = {n: _all_gather("ag_" + n, wts[n][0].astype(BF16), AG_ID[n]) for n in BIG}
    w_gu1, w_gu2 = full["w_ffn1_gate_up"], full["w_ffn2_gate_up"]
    w_d1 = full["w_ffn1_down"].reshape(-1, d)
    w_d2 = full["w_ffn2_down"].reshape(-1, d)
    w_out = full["w_mix_out"].reshape(-1, d)
    w_in = full["w_mix_in"]

    g = {n: wts[n] for n in SMALL}
    lb_f = jax.nn.softmax(g["hgrn_lower_bounds_fwd"], axis=0)[0:1]
    lb_b = jax.nn.softmax(g["hgrn_lower_bounds_bwd"], axis=0)[0:1]
    bucket_idx = jnp.asarray(_t5_bucket_index())
    bias = _bias_build(g["rel_bias_table"], bucket_idx)

    n1 = _pre_norm("pre_norm1", x0, g["pre_norm_ffn1"])
    gu1 = _matmul("ffn1_gate_up", n1, w_gu1, mode="nn", stack=True, out_dtype=F32)
    a1 = _swiglu_fwd("ffn1_act", gu1)
    ff1 = _matmul("ffn1_down", a1, w_d1, mode="nn", out_dtype=F32)
    x1, h = _post_res_pre("res1", x0, ff1, g["post_norm_ffn1"], g["pre_norm_mix"], 0.5)
    p = _matmul("mix_in", h, w_in, mode="nn", stack=True, out_dtype=F32)
    y_h, o_raw = _hgrn_fwd(p, lb_f, lb_b, g["hgrn_out_norm"], n_hgrn)
    y_a = _attn_fwd(p, bias, g["attn_sink"], n_hgrn, n_attn)
    cat = jnp.concatenate([y_h, y_a], axis=1)
    mixed = _matmul("mix_out", cat, w_out, mode="nn", out_dtype=F32)
    x2, n2 = _post_res_pre("res2", x1, mixed, g["post_norm_mix"], g["pre_norm_ffn2"], 1.0)
    gu2 = _matmul("ffn2_gate_up", n2, w_gu2, mode="nn", stack=True, out_dtype=F32)
    a2 = _swiglu_fwd("ffn2_act", gu2)
    ff2 = _matmul("ffn2_down", a2, w_d2, mode="nn", out_dtype=F32)
    dy3, loss_part = _post_res_loss("res3_loss", x2, ff2, g["post_norm_ffn2"], target, 0.5)

    small_grad = {}
    scattered = {}

    def scatter(name, grad_stack):
        scattered[name] = _reduce_scatter(name, grad_stack, core, RS_ID[name])
        return [scattered[name][0]]

    def ffn_bwd(tag, dy, ff, a, gu, n_in, x_in, w_gu, w_d, post_name, pre_name, gu_name, d_name):
        dff, small_grad[post_name] = _post_bwd("post_bwd" + tag, dy, ff, g[post_name], 0.5)
        da = _matmul("d_act" + tag, dff, w_d, mode="nt", out_dtype=F32)
        dep = scatter(d_name, _matmul("dw_down" + tag, a, dff, mode="tn", out_dtype=BF16).reshape(N_DEV, -1, d))
        dgu = _swiglu_bwd("act_bwd" + tag, da, gu, deps=dep)
        dn = _matmul("d_norm" + tag, dgu, w_gu, mode="nt", stack=True, out_dtype=F32)
        dep = scatter(gu_name, _matmul("dw_gate_up" + tag, n_in, dgu, mode="tn", stack=True, out_dtype=BF16))
        dx, small_grad[pre_name] = _pre_bwd("pre_bwd" + tag, dn, x_in, g[pre_name], dy, deps=dep)
        return dx

    dx2 = ffn_bwd("2", dy3, ff2, a2, gu2, n2, x2, w_gu2, w_d2, "post_norm_ffn2", "pre_norm_ffn2",
                  "w_ffn2_gate_up", "w_ffn2_down")

    dmixed, small_grad["post_norm_mix"] = _post_bwd("post_bwd_mix", dx2, mixed, g["post_norm_mix"], 1.0)
    dcat = _matmul("d_cat", dmixed, w_out, mode="nt", out_dtype=F32)
    dep = scatter("w_mix_out", _matmul("dw_mix_out", cat, dmixed, mode="tn", out_dtype=BF16).reshape(N_DEV, -1, d))
    dq_h, di_h, dzf, dzb, dg_h, dlb_f, dlb_b, small_grad["hgrn_out_norm"] = _hgrn_bwd(
        p, o_raw, dcat, lb_f, lb_b, g["hgrn_out_norm"], n_hgrn)
    dq_a, dk_a, dv_a, dbias, dsink_rows = _attn_bwd(p, dcat, bias, g["attn_sink"], n_hgrn, n_attn)
    dp = jnp.concatenate([dq_h, di_h, dzf, dzb, dg_h, dq_a, dk_a, dv_a], axis=1)
    dh = _matmul("d_h", dp, w_in, mode="nt", stack=True, out_dtype=F32, deps=dep)
    dep = scatter("w_mix_in", _matmul("dw_mix_in", h, dp, mode="tn", stack=True, out_dtype=BF16))
    dx1, small_grad["pre_norm_mix"] = _pre_bwd("pre_bwd_mix", dh, x1, g["pre_norm_mix"], dx2, deps=dep)

    dx0 = ffn_bwd("1", dx1, ff1, a1, gu1, n1, x0, w_gu1, w_d1, "post_norm_ffn1", "pre_norm_ffn1",
                  "w_ffn1_gate_up", "w_ffn1_down")

    def lb_grad(dlb, lb):
        da0 = dlb * lb * (1.0 - lb)
        return jnp.concatenate([da0, -da0], axis=0)

    small_grad["hgrn_lower_bounds_fwd"] = lb_grad(dlb_f, lb_f)
    small_grad["hgrn_lower_bounds_bwd"] = lb_grad(dlb_b, lb_b)
    small_grad["attn_sink"] = dsink_rows[:, :, 0].reshape(1, n_attn)
    small_grad["rel_bias_table"] = jnp.transpose(_bias_reduce(dbias, bucket_idx)[:, :, 0])

    parts = [small_grad[n] for n in SMALL] + [loss_part[:, 0:1]]
    red = _small_all_reduce("small_all_reduce", _pack(parts))
    red_list = _unpack(red, parts)
    loss = red_list[-1].reshape(())
    sg = dict(zip(SMALL, red_list[:-1]))
    like = [wts[n] for n in SMALL]
    d_s, m_s, v_s = _adam_small("adam_small", _pack(like), _pack([sg[n] for n in SMALL]),
                                _pack([mom[n] for n in SMALL]), _pack([var[n] for n in SMALL]))
    grads = dict(sg)
    delta = dict(zip(SMALL, _unpack(d_s, like)))
    new_m = dict(zip(SMALL, _unpack(m_s, like)))
    new_v = dict(zip(SMALL, _unpack(v_s, like)))

    dep = []
    for n in ("w_ffn2_down", "w_ffn2_gate_up", "w_mix_out", "w_mix_in", "w_ffn1_down", "w_ffn1_gate_up"):
        sums, recv = scattered[n]
        gr, de, nm, nv = _adam_shard("adam_" + n, wts[n][0], mom[n][0], var[n][0], sums, recv, chip, deps=dep)
        grads[n], delta[n], new_m[n], new_v[n] = gr[None], de[None], nm[None], nv[None]
        dep = [gr]

    return (loss, dx0[None], *[grads[n] for n in ORDER], *[delta[n] for n in ORDER],
            *[new_m[n] for n in ORDER], *[new_v[n] for n in ORDER])
```

```python
import functools
import math

import numpy as np
import jax
import jax.numpy as jnp
from jax import lax
from jax.experimental import pallas as pl
from jax.experimental.pallas import tpu as pltpu
from jax.experimental.pallas import tpu_sc as plsc

F32 = jnp.float32
BF16 = jnp.bfloat16
HIGHEST = lax.Precision.HIGHEST
MESH = pl.DeviceIdType.MESH

N_DEV = 8
EPS = 1e-6
NEG_INF = -1e30
HEAD = 128
CHUNK = 64
WINDOW = 128
KEY_SPAN = 3 * WINDOW
KV_HEADS = 2
REL_BUCKETS = 32
REL_MAX_DIST = 128
ADAM_LR, ADAM_B1, ADAM_B2, ADAM_EPS, ADAM_WD, ADAM_STEP = 0.001, 0.9, 0.999, 1e-08, 0.01, 10
LANES = 128
VMEM_LIMIT = 56 * 1024 * 1024
ANY = pl.BlockSpec(memory_space=pl.ANY)


def _cparams(*sem):
    return pltpu.CompilerParams(dimension_semantics=sem if sem else None, vmem_limit_bytes=VMEM_LIMIT)


def _dot(a, b):
    return jnp.dot(a, b, preferred_element_type=F32)


def _dot_nt(a, b):
    return lax.dot_general(a, b, (((1,), (1,)), ((), ())), preferred_element_type=F32)


def _dot_tn(a, b):
    return lax.dot_general(a, b, (((0,), (0,)), ((), ())), preferred_element_type=F32)


def _tile(dim, target):
    for c in (target, 1024, 512, 256, 128):
        if c <= target and dim % c == 0:
            return c
    return dim


K_WHOLE = 2048
K_STEP = 2816


def _k_tile(kd):
    if kd <= K_WHOLE:
        return kd
    return max(c for c in range(LANES, K_STEP + 1, LANES) if kd % c == 0)


def _matmul(name, a, b, *, mode, out_dtype, stack=False, halves=False, tm=1024, tn=1024, deps=()):
    grp = 1
    if mode == "nn":
        m, kd = a.shape
        n = b.shape[0] * b.shape[2] if stack else b.shape[1]
    elif mode == "nt":
        m = a.shape[-2]
        n, kd = (b.shape[1], b.shape[0] * b.shape[2]) if stack else b.shape
    else:
        kd, m = a.shape
        n = b.shape[-1] * (2 if halves else 1)
    if stack:
        n1 = b.shape[2] if mode != "tn" else n // N_DEV
        if mode == "nt":
            grp = 2 if 2 * n1 <= K_STEP else 1
            tk = grp * n1
        else:
            grp = 1 if n1 % LANES == 0 else 2
            tn = grp * n1
        assert (grp * n1) % LANES == 0
    per_half = N_DEV // 2 // grp
    tm = _tile(m, tm)
    if not (stack and mode in ("nn", "tn")):
        tn = _tile(n, tn)
    if not (stack and mode == "nt"):
        tk = _k_tile(kd)
    nk = kd // tk
    lead = None if grp == 1 else grp
    b_outer = nk == 1 and b.size > a.size
    grid = (n // tn, m // tm, nk) if b_outer else (m // tm, n // tn, nk)

    def spec(shape, index):
        return pl.BlockSpec(shape, (lambda g0, g1, k: index(g1, g0, k)) if b_outer else index)

    if mode == "nn":
        a_spec = spec((tm, tk), lambda i, j, k: (i, k))
        if stack:
            b_spec = spec((lead, tk, n1), lambda i, j, k: (j, k, 0))
        else:
            b_spec = spec((tk, tn), lambda i, j, k: (k, j))
        dot = _dot
    elif mode == "nt":
        if halves:
            a_spec = spec((None, tm, tk), lambda i, j, k: (k // per_half, i, k % per_half))
        else:
            a_spec = spec((tm, tk), lambda i, j, k: (i, k))
        if stack:
            b_spec = spec((lead, tn, n1), lambda i, j, k: (k, j, 0))
        else:
            b_spec = spec((tn, tk), lambda i, j, k: (j, k))
        dot = _dot_nt
    else:
        a_spec = spec((tk, tm), lambda i, j, k: (k, i))
        if halves:
            b_spec = spec((None, tk, tn), lambda i, j, k: (j // per_half, k, j % per_half))
        else:
            b_spec = spec((tk, tn), lambda i, j, k: (k, j))
        dot = _dot_tn
    if stack and mode == "tn":
        out_shape = jax.ShapeDtypeStruct((N_DEV, m, n1), out_dtype)
        o_spec = spec((lead, tm, n1), lambda i, j, k: (j, i, 0))
    else:
        out_shape = jax.ShapeDtypeStruct((m, n), out_dtype)
        o_spec = spec((tm, tn), lambda i, j, k: (i, j))
    b_grouped = stack and grp > 1 and mode != "tn"
    o_grouped = stack and grp > 1 and mode == "tn"

    def product(a_ref, b_ref):
        bmat = jnp.concatenate([b_ref[s] for s in range(grp)], axis=1) if b_grouped else b_ref[...]
        return dot(a_ref[...], bmat)

    def store(o_ref, val):
        if o_grouped:
            for s in range(grp):
                o_ref[s] = val[:, s * n1:(s + 1) * n1].astype(o_ref.dtype)
        else:
            o_ref[...] = val.astype(o_ref.dtype)

    def body_whole(a_ref, b_ref, *rest):
        store(rest[-1], product(a_ref, b_ref))

    def body_steps(a_ref, b_ref, *rest):
        o_ref, acc_ref = rest[-2:]
        k = pl.program_id(2)

        @pl.when(k == 0)
        def _():
            acc_ref[...] = product(a_ref, b_ref)

        @pl.when(k > 0)
        def _():
            acc_ref[...] += product(a_ref, b_ref)

        @pl.when(k == nk - 1)
        def _():
            store(o_ref, acc_ref[...])

    return pl.pallas_call(
        body_whole if nk == 1 else body_steps, name=name, grid=grid,
        in_specs=[a_spec, b_spec] + [ANY] * len(deps), out_specs=o_spec, out_shape=out_shape,
        scratch_shapes=[] if nk == 1 else [pltpu.VMEM((tm, tn), F32)],
        compiler_params=_cparams("parallel", "parallel", "arbitrary"),
    )(a, b, *deps)


def _ffn_up(name, n, w_stack):
    t, d = n.shape
    s, _, n1 = w_stack.shape
    half = s // 2
    tm = _tile(t, 512)

    def body(n_ref, wg_ref, wu_ref, act_ref, gu_ref):
        nv = n_ref[...]
        gate = _dot(nv, wg_ref[...])
        up = _dot(nv, wu_ref[...])
        act_ref[...] = (gate * jax.nn.sigmoid(gate) * up).astype(BF16)
        gu_ref[0] = gate.astype(BF16)
        gu_ref[1] = up.astype(BF16)

    return pl.pallas_call(
        body, name=name, grid=(half, t // tm),
        in_specs=[pl.BlockSpec((tm, d), lambda j, i: (i, 0)),
                  pl.BlockSpec((None, d, n1), lambda j, i: (j, 0, 0)),
                  pl.BlockSpec((None, d, n1), lambda j, i: (half + j, 0, 0))],
        out_specs=[pl.BlockSpec((tm, n1), lambda j, i: (i, j)), pl.BlockSpec((2, tm, n1), lambda j, i: (0, i, j))],
        out_shape=[jax.ShapeDtypeStruct((t, half * n1), BF16), jax.ShapeDtypeStruct((2, t, half * n1), BF16)],
        compiler_params=_cparams("parallel", "parallel"),
    )(n, w_stack, w_stack)


def _ffn_dact(name, dff, w_d, gu, deps=()):
    t, d = dff.shape
    f = w_d.shape[0]
    tm = _tile(t, 512)
    tn = _tile(f, 1408)

    def body(dff_ref, w_ref, gu_ref, *rest):
        dgu_ref = rest[-1]
        da = _dot_nt(dff_ref[...], w_ref[...])
        gate = gu_ref[0].astype(F32)
        sg = jax.nn.sigmoid(gate)
        dgu_ref[0] = (da * gu_ref[1].astype(F32) * (sg * (1.0 + gate * (1.0 - sg)))).astype(BF16)
        dgu_ref[1] = (da * (gate * sg)).astype(BF16)

    pair = pl.BlockSpec((2, tm, tn), lambda j, i: (0, i, j))
    return pl.pallas_call(
        body, name=name, grid=(f // tn, t // tm),
        in_specs=[pl.BlockSpec((tm, d), lambda j, i: (i, 0)), pl.BlockSpec((tn, d), lambda j, i: (j, 0)), pair]
        + [ANY] * len(deps),
        out_specs=pair, out_shape=jax.ShapeDtypeStruct((2, t, f), BF16),
        compiler_params=_cparams("parallel", "parallel"),
    )(dff, w_d, gu, *deps)


ROWS = 256


def _rstd(xf):
    return lax.rsqrt(jnp.mean(xf * xf, axis=-1, keepdims=True) + EPS)


def _row_spec(t, d):
    return pl.BlockSpec((min(ROWS, t), d), lambda i: (i, 0))


def _vec_spec(d):
    return pl.BlockSpec((1, d), lambda i: (0, 0))


def _pre_norm(name, x, gain):
    t, d = x.shape

    def body(x_ref, g_ref, n_ref):
        xf = x_ref[...]
        n_ref[...] = (xf * _rstd(xf) * g_ref[...]).astype(BF16)

    return pl.pallas_call(
        body, name=name, grid=(t // min(ROWS, t),), in_specs=[_row_spec(t, d), _vec_spec(d)],
        out_specs=_row_spec(t, d), out_shape=jax.ShapeDtypeStruct((t, d), BF16),
        compiler_params=_cparams("parallel"),
    )(x, gain)


def _post_res_pre(name, x, ff, g_post, g_next, scale):
    t, d = x.shape

    def body(x_ref, ff_ref, gp_ref, gn_ref, xo_ref, n_ref):
        ff_ = ff_ref[...]
        xn = x_ref[...] + scale * (ff_ * _rstd(ff_) * gp_ref[...])
        xo_ref[...] = xn
        n_ref[...] = (xn * _rstd(xn) * gn_ref[...]).astype(BF16)

    return pl.pallas_call(
        body, name=name, grid=(t // min(ROWS, t),),
        in_specs=[_row_spec(t, d), _row_spec(t, d), _vec_spec(d), _vec_spec(d)],
        out_specs=[_row_spec(t, d), _row_spec(t, d)],
        out_shape=[jax.ShapeDtypeStruct((t, d), F32), jax.ShapeDtypeStruct((t, d), BF16)],
        compiler_params=_cparams("parallel"),
    )(x, ff, g_post, g_next)


def _post_res_loss(name, x, ff, g_post, target, scale):
    t, d = x.shape

    def body(x_ref, ff_ref, gp_ref, tg_ref, dy_ref, loss_ref):
        ff_ = ff_ref[...]
        err = x_ref[...] + scale * (ff_ * _rstd(ff_) * gp_ref[...]) - tg_ref[...]
        dy_ref[...] = err / d
        part = 0.5 * jnp.sum(jnp.mean(err * err, axis=-1, keepdims=True), axis=0, keepdims=True)

        @pl.when(pl.program_id(0) == 0)
        def _():
            loss_ref[...] = jnp.zeros_like(loss_ref)

        loss_ref[...] += jnp.broadcast_to(part, loss_ref.shape)

    return pl.pallas_call(
        body, name=name, grid=(t // min(ROWS, t),),
        in_specs=[_row_spec(t, d), _row_spec(t, d), _vec_spec(d), _row_spec(t, d)],
        out_specs=[_row_spec(t, d), _vec_spec(LANES)],
        out_shape=[jax.ShapeDtypeStruct((t, d), F32), jax.ShapeDtypeStruct((1, LANES), F32)],
        compiler_params=_cparams("arbitrary"),
    )(x, ff, g_post, target)


def _post_bwd(name, dy, ff, g_post, scale):
    t, d = dy.shape

    def body(dy_ref, ff_ref, gp_ref, dff_ref, dg_ref):
        ff_ = ff_ref[...]
        r = _rstd(ff_)
        xh = ff_ * r
        dyn = scale * dy_ref[...]
        dxh = dyn * gp_ref[...]
        dff_ref[...] = (r * (dxh - xh * jnp.mean(dxh * xh, axis=-1, keepdims=True))).astype(BF16)

        @pl.when(pl.program_id(0) == 0)
        def _():
            dg_ref[...] = jnp.zeros_like(dg_ref)

        dg_ref[...] += jnp.sum(dyn * xh, axis=0, keepdims=True)

    return pl.pallas_call(
        body, name=name, grid=(t // min(ROWS, t),),
        in_specs=[_row_spec(t, d), _row_spec(t, d), _vec_spec(d)],
        out_specs=[_row_spec(t, d), _vec_spec(d)],
        out_shape=[jax.ShapeDtypeStruct((t, d), BF16), jax.ShapeDtypeStruct((1, d), F32)],
        compiler_params=_cparams("arbitrary"),
    )(dy, ff, g_post)


def _pre_bwd(name, dn, x, g_pre, dy, deps=()):
    t, d = x.shape

    def body(dn_ref, x_ref, g_ref, dy_ref, *rest):
        dx_ref, dg_ref = rest[-2:]
        xf = x_ref[...]
        r = _rstd(xf)
        xh = xf * r
        dnf = dn_ref[...].astype(F32)
        dxh = dnf * g_ref[...]
        dx_ref[...] = dy_ref[...] + r * (dxh - xh * jnp.mean(dxh * xh, axis=-1, keepdims=True))

        @pl.when(pl.program_id(0) == 0)
        def _():
            dg_ref[...] = jnp.zeros_like(dg_ref)

        dg_ref[...] += jnp.sum(dnf * xh, axis=0, keepdims=True)

    return pl.pallas_call(
        body, name=name, grid=(t // min(ROWS, t),),
        in_specs=[_row_spec(t, d), _row_spec(t, d), _vec_spec(d), _row_spec(t, d)] + [ANY] * len(deps),
        out_specs=[_row_spec(t, d), _vec_spec(d)],
        out_shape=[jax.ShapeDtypeStruct((t, d), F32), jax.ShapeDtypeStruct((1, d), F32)],
        compiler_params=_cparams("arbitrary"),
    )(dn, x, g_pre, dy, *deps)


def _bdot(a, b, ca, cb, precision=None):
    return lax.dot_general(a, b, (((ca,), (cb,)), ((0,), (0,))), preferred_element_type=F32, precision=precision)


def _tri_masks(g):
    row = lax.broadcasted_iota(jnp.int32, (g, CHUNK, CHUNK), 1)
    col = lax.broadcasted_iota(jnp.int32, (g, CHUNK, CHUNK), 2)
    return col <= row, col >= row


def _hgrn_block(z, lb, q, v, cum_mat):
    sg = jax.nn.sigmoid(z)
    f = lb + (1.0 - lb) * sg
    lf = jnp.log(f)
    k = 1.0 - f
    a = _bdot(cum_mat, lf, 2, 1, HIGHEST)
    last = jnp.sum(lf, axis=1, keepdims=True)
    e_a = jnp.exp(a)
    e_na = jnp.exp(-a)
    e_t = jnp.exp(last - a)
    return dict(sg=sg, f=f, k=k, decay=jnp.exp(last), e_a=e_a, e_na=e_na, e_t=e_t,
                qd=q * e_a, kd=k * e_na, kt=k * e_t)


def _hgrn_states(state, kv, decay, order):
    entering = [None] * len(order)
    for g in order:
        entering[g] = state
        state = decay[g] * state + kv[g]
    return jnp.stack(entering, axis=0), state


def _hgrn_fwd(p, lb_f, lb_b, gain, n_heads):
    t = p.shape[0]
    w = n_heads * HEAD
    blk = min(8, t // CHUNK)
    rows_blk = blk * CHUNK
    n_blocks = t // rows_blk
    fin_rows = min(256, t)

    def body(q_ref, i_ref, zf_ref, zb_ref, g_ref, lbf_ref, lbb_ref, gain_ref, y_ref, o_ref, st_ref):
        low, up = _tri_masks(blk)
        m_low, m_up = low.astype(F32), up.astype(F32)
        o_ref[...] = jnp.zeros_like(o_ref)
        st_ref[...] = jnp.zeros_like(st_ref)

        def one(r0, z_ref, lb, slot, rev):
            rows = pl.ds(r0, rows_blk)
            split = lambda ref: ref[rows, :].reshape(blk, CHUNK, HEAD)
            q, v = split(q_ref), split(i_ref)
            c = _hgrn_block(split(z_ref), lb, q, v, m_up if rev else m_low)
            qd, kd, kt, vb = c["qd"].astype(BF16), c["kd"].astype(BF16), c["kt"].astype(BF16), v.astype(BF16)
            pm = jnp.where(up if rev else low, _bdot(qd, kd, 2, 2), 0.0).astype(BF16)
            kv = _bdot(vb, kt, 1, 1)
            order = range(blk - 1, -1, -1) if rev else range(blk)
            entering, st_ref[slot] = _hgrn_states(st_ref[slot], kv, c["decay"], order)
            o = _bdot(pm, vb, 2, 1) + _bdot(qd, entering.astype(BF16), 2, 2)
            o_ref[rows, :] += o.reshape(rows_blk, HEAD)

        def step(n, carry):
            one(pl.multiple_of(n * rows_blk, rows_blk), zf_ref, lbf_ref[...], 0, False)
            one(pl.multiple_of((n_blocks - 1 - n) * rows_blk, rows_blk), zb_ref, lbb_ref[...], 1, True)
            return carry

        lax.fori_loop(0, n_blocks, step, 0)

        def fin(n, carry):
            rows = pl.ds(pl.multiple_of(n * fin_rows, fin_rows), fin_rows)
            o = o_ref[rows, :]
            g = g_ref[rows, :]
            y_ref[rows, :] = (o * _rstd(o) * gain_ref[...] * (g * jax.nn.sigmoid(g))).astype(BF16)
            return carry

        lax.fori_loop(0, t // fin_rows, fin, 0)

    col = lambda grp: pl.BlockSpec((t, HEAD), lambda h: (0, grp * n_heads + h))
    vec = pl.BlockSpec((1, HEAD), lambda h: (0, h))
    out = pl.BlockSpec((t, HEAD), lambda h: (0, h))
    return pl.pallas_call(
        body, name="hgrn_fwd", grid=(n_heads,),
        in_specs=[col(0), col(1), col(2), col(3), col(4), vec, vec, vec],
        out_specs=[out, out],
        out_shape=[jax.ShapeDtypeStruct((t, w), BF16), jax.ShapeDtypeStruct((t, w), F32)],
        scratch_shapes=[pltpu.VMEM((2, HEAD, HEAD), F32)],
        compiler_params=_cparams("parallel"),
    )(p, p, p, p, p, lb_f, lb_b, gain)


def _hgrn_bwd(p, o_raw, dcat, lb_f, lb_b, gain, n_heads):
    t = p.shape[0]
    w = n_heads * HEAD
    n_chunks = t // CHUNK
    blk = min(8, n_chunks)
    rows_blk = blk * CHUNK
    n_blocks = t // rows_blk
    rb = min(256, t)

    def body(q_ref, i_ref, zf_ref, zb_ref, g_ref, o_ref, dy_ref, lbf_ref, lbb_ref, gain_ref,
             dq_ref, di_ref, dzf_ref, dzb_ref, dg_ref, dlbf_ref, dlbb_ref, dgain_ref,
             do_s, dq_s, dv_s, st_s, cur_s):
        low, up = _tri_masks(blk)
        m_low, m_up = low.astype(F32), up.astype(F32)
        rowid = lax.broadcasted_iota(jnp.int32, (blk, CHUNK, HEAD), 1)
        gain_v = gain_ref[...]

        def norm_bwd(n, dgain):
            rows = pl.ds(pl.multiple_of(n * rb, rb), rb)
            o = o_ref[rows, :]
            g = g_ref[rows, :]
            dy = dy_ref[rows, :]
            r = _rstd(o)
            oh = o * r
            sg = jax.nn.sigmoid(g)
            dg_ref[rows, :] = (dy * oh * gain_v * (sg * (1.0 + g * (1.0 - sg)))).astype(BF16)
            dno = dy * (g * sg)
            dxh = dno * gain_v
            do_s[rows, :] = r * (dxh - oh * jnp.mean(dxh * oh, axis=-1, keepdims=True))
            return dgain + jnp.sum(dno * oh, axis=0, keepdims=True)

        dgain_ref[...] = lax.fori_loop(0, t // rb, norm_bwd, jnp.zeros((1, HEAD), F32))
        dq_s[...] = jnp.zeros_like(dq_s)
        dv_s[...] = jnp.zeros_like(dv_s)

        def direction(z_ref, lb_ref, dz_ref, dlb_ref, rev):
            lb = lb_ref[...]
            cum_mat = m_up if rev else m_low
            cum_mat_t = m_low if rev else m_up
            mask = up if rev else low
            last_row = 0 if rev else CHUNK - 1

            order = range(blk - 1, -1, -1) if rev else range(blk)

            def rows_of(j):
                bidx = (n_blocks - 1 - j) if rev else j
                return bidx, pl.ds(pl.multiple_of(bidx * rows_blk, rows_blk), rows_blk)

            def load(rows):
                split = lambda ref: ref[rows, :].reshape(blk, CHUNK, HEAD)
                q, v = split(q_ref), split(i_ref)
                return q, v, _hgrn_block(split(z_ref), lb, q, v, cum_mat)

            def sweep_fwd(j, carry):
                bidx, rows = rows_of(j)
                _, v, c = load(rows)
                kv = _bdot(v.astype(BF16), c["kt"].astype(BF16), 1, 1)
                st_s[pl.ds(bidx * blk, blk)], cur_s[0] = _hgrn_states(cur_s[0], kv, c["decay"], order)
                return carry

            cur_s[...] = jnp.zeros_like(cur_s)
            dlb_ref[...] = jnp.zeros_like(dlb_ref)
            lax.fori_loop(0, n_blocks, sweep_fwd, 0)

            def sweep_bwd(jj, carry):
                bidx, rows = rows_of(n_blocks - 1 - jj)
                _, v, c = load(rows)
                st = st_s[pl.ds(bidx * blk, blk)]
                do = do_s[rows, :].reshape(blk, CHUNK, HEAD)
                qd, kd, kt, decay = c["qd"], c["kd"], c["kt"], c["decay"]
                qd_b, kd_b, kt_b = qd.astype(BF16), kd.astype(BF16), kt.astype(BF16)
                v_b, do_b, st_b = v.astype(BF16), do.astype(BF16), st.astype(BF16)
                pm = jnp.where(mask, _bdot(qd_b, kd_b, 2, 2), 0.0).astype(BF16)
                dpm = jnp.where(mask, _bdot(do_b, v_b, 2, 2), 0.0).astype(BF16)
                gq = _bdot(do_b, qd_b, 1, 1)
                dstate = cur_s[1]
                after = [None] * blk
                for g in reversed(order):
                    after[g] = dstate
                    dstate = gq[g] + decay[g] * dstate
                cur_s[1] = dstate
                dst = jnp.stack(after, axis=0)
                dst_b = dst.astype(BF16)
                dv = _bdot(pm, do_b, 1, 1) + _bdot(kt_b, dst_b, 2, 2)
                dqd = _bdot(dpm, kd_b, 2, 1) + _bdot(do_b, st_b, 2, 1)
                dkd = _bdot(dpm, qd_b, 1, 1)
                dkt = _bdot(v_b, dst_b, 2, 1)
                dlast = (jnp.sum(dkt * kt, axis=1, keepdims=True)
                         + decay * jnp.sum(dst * st, axis=1, keepdims=True))
                dq_s[rows, :] += (dqd * c["e_a"]).reshape(rows_blk, HEAD)
                dv_s[rows, :] += dv.reshape(rows_blk, HEAD)
                dk = dkd * c["e_na"] + dkt * c["e_t"]
                da = dqd * qd - dkd * kd - dkt * kt
                da = da + jnp.where(rowid == last_row, dlast, 0.0)
                dlf = _bdot(cum_mat_t, da, 2, 1, HIGHEST)
                df = dlf / c["f"] - dk
                sg = c["sg"]
                dz_ref[rows, :] = (df * (1.0 - lb) * (sg * (1.0 - sg))).reshape(rows_blk, HEAD).astype(BF16)
                dlb_ref[...] += jnp.sum((df * (1.0 - sg)).reshape(rows_blk, HEAD), axis=0, keepdims=True)
                return carry

            lax.fori_loop(0, n_blocks, sweep_bwd, 0)

        direction(zf_ref, lbf_ref, dzf_ref, dlbf_ref, False)
        direction(zb_ref, lbb_ref, dzb_ref, dlbb_ref, True)
        dq_ref[...] = dq_s[...].astype(BF16)
        di_ref[...] = dv_s[...].astype(BF16)

    col = lambda grp: pl.BlockSpec((t, HEAD), lambda h: (0, grp * n_heads + h))
    one = pl.BlockSpec((t, HEAD), lambda h: (0, h))
    vec = pl.BlockSpec((1, HEAD), lambda h: (0, h))
    big = jax.ShapeDtypeStruct((t, w), BF16)
    small = jax.ShapeDtypeStruct((1, w), F32)
    return pl.pallas_call(
        body, name="hgrn_bwd", grid=(n_heads,),
        in_specs=[col(0), col(1), col(2), col(3), col(4), one, one, vec, vec, vec],
        out_specs=[one] * 5 + [vec] * 3,
        out_shape=[big] * 5 + [small] * 3,
        scratch_shapes=[pltpu.VMEM((t, HEAD), F32), pltpu.VMEM((t, HEAD), F32), pltpu.VMEM((t, HEAD), F32),
                        pltpu.VMEM((n_chunks, HEAD, HEAD), F32), pltpu.VMEM((2, HEAD, HEAD), F32)],
        compiler_params=_cparams("parallel"),
    )(p, p, p, p, p, o_raw, dcat, lb_f, lb_b, gain)


def _t5_bucket_index():
    c = np.arange(WINDOW)[:, None]
    s = np.arange(KEY_SPAN)[None, :]
    rel = s - WINDOW - c
    nb = REL_BUCKETS // 2
    max_exact = nb // 2
    bucket = (rel > 0).astype(np.int32) * nb
    n = np.abs(rel)
    large = max_exact + (np.log(np.maximum(n, 1) / max_exact) / np.log(REL_MAX_DIST / max_exact)
                         * (nb - max_exact)).astype(np.int32)
    large = np.minimum(large, nb - 1)
    return bucket + np.where(n < max_exact, n, large).astype(np.int32)


def _bias_build(table, idx):
    n_attn = table.shape[1]

    def body(tab_ref, idx_ref, o_ref):
        h = pl.program_id(0)
        idx_v = idx_ref[...]
        acc = jnp.zeros((WINDOW, KEY_SPAN), F32)
        for b in range(REL_BUCKETS):
            acc = jnp.where(idx_v == b, tab_ref[b, h], acc)
        o_ref[...] = acc

    return pl.pallas_call(
        body, name="bias_build", grid=(n_attn,),
        in_specs=[pl.BlockSpec(memory_space=pltpu.SMEM), pl.BlockSpec((WINDOW, KEY_SPAN), lambda h: (0, 0))],
        out_specs=pl.BlockSpec((None, WINDOW, KEY_SPAN), lambda h: (h, 0, 0)),
        out_shape=jax.ShapeDtypeStruct((n_attn, WINDOW, KEY_SPAN), F32), compiler_params=_cparams("parallel"),
    )(table, idx)


def _bias_reduce(dbias, idx):
    n_attn = dbias.shape[0]

    def body(idx_ref, d_ref, o_ref):
        idx_v = idx_ref[...]
        dv = d_ref[...]
        rows = lax.broadcasted_iota(jnp.int32, (REL_BUCKETS, LANES), 0)
        acc = jnp.zeros((REL_BUCKETS, LANES), F32)
        for b in range(REL_BUCKETS):
            part = jnp.sum(jnp.where(idx_v == b, dv, 0.0), axis=1, keepdims=True)
            acc = jnp.where(rows == b, jnp.sum(part, axis=0, keepdims=True), acc)
        o_ref[...] = acc

    return pl.pallas_call(
        body, name="bias_reduce", grid=(n_attn,),
        in_specs=[pl.BlockSpec((WINDOW, KEY_SPAN), lambda h: (0, 0)),
                  pl.BlockSpec((None, WINDOW, KEY_SPAN), lambda h: (h, 0, 0))],
        out_specs=pl.BlockSpec((None, REL_BUCKETS, LANES), lambda h: (h, 0, 0)),
        out_shape=jax.ShapeDtypeStruct((n_attn, REL_BUCKETS, LANES), F32), compiler_params=_cparams("parallel"),
    )(idx, dbias)


def _attn_probs(qg, kb, bias, sink, valid):
    s = _dot_nt(qg, kb) / math.sqrt(HEAD) + bias
    s = jnp.where(valid, s, NEG_INF)
    m = jnp.maximum(jnp.max(s, axis=-1, keepdims=True), sink)
    e = jnp.exp(s - m)
    e_sink = jnp.exp(sink - m)
    den = jnp.sum(e, axis=-1, keepdims=True) + e_sink
    return e / den, e_sink / den


def _attn_valid(n, t):
    c = lax.broadcasted_iota(jnp.int32, (WINDOW, KEY_SPAN), 0)
    s = lax.broadcasted_iota(jnp.int32, (WINDOW, KEY_SPAN), 1)
    rel = s - WINDOW - c
    key_pos = n * WINDOW - WINDOW + s
    return (jnp.abs(rel) <= WINDOW) & (key_pos >= 0) & (key_pos < t)


def _attn_specs(t, n_hgrn, n_attn):
    grp = n_attn // KV_HEADS
    nb = t // WINDOW
    cq = 5 * n_hgrn
    ck = cq + n_attn
    cv = ck + KV_HEADS
    q_spec = pl.BlockSpec((WINDOW, grp * HEAD), lambda x, n: (n, cq // grp + x))
    kv = lambda base, off: pl.BlockSpec(
        (WINDOW, HEAD), lambda x, n: (jnp.clip(n + off, 0, nb - 1), base + x))
    band = [kv(ck, -1), kv(ck, 0), kv(ck, 1), kv(cv, -1), kv(cv, 0), kv(cv, 1)]
    bias_spec = pl.BlockSpec((grp, WINDOW, KEY_SPAN), lambda x, n: (x, 0, 0))
    sink_spec = pl.BlockSpec(memory_space=pltpu.SMEM)
    return grp, nb, q_spec, band, bias_spec, sink_spec


def _attn_fwd(p, bias, sink, n_hgrn, n_attn):
    t = p.shape[0]
    grp, nb, q_spec, band, bias_spec, sink_spec = _attn_specs(t, n_hgrn, n_attn)

    def body(q_ref, kp, kc, kn, vp, vc, vn, bias_ref, sink_ref, y_ref):
        x, n = pl.program_id(0), pl.program_id(1)
        kb = jnp.concatenate([kp[...], kc[...], kn[...]], axis=0).astype(BF16)
        vb = jnp.concatenate([vp[...], vc[...], vn[...]], axis=0).astype(BF16)
        valid = _attn_valid(n, t)
        for g in range(grp):
            qg = q_ref[:, g * HEAD:(g + 1) * HEAD].astype(BF16)
            pr, _ = _attn_probs(qg, kb, bias_ref[g], sink_ref[0, x * grp + g], valid)
            y_ref[:, g * HEAD:(g + 1) * HEAD] = _dot(pr.astype(BF16), vb).astype(BF16)

    return pl.pallas_call(
        body, name="attn_fwd", grid=(KV_HEADS, nb),
        in_specs=[q_spec] + band + [bias_spec, sink_spec],
        out_specs=pl.BlockSpec((WINDOW, grp * HEAD), lambda x, n: (n, x)),
        out_shape=jax.ShapeDtypeStruct((t, n_attn * HEAD), BF16),
        compiler_params=_cparams("parallel", "parallel"),
    )(p, p, p, p, p, p, p, bias, sink)


def _attn_bwd(p, dcat, bias, sink, n_hgrn, n_attn):
    t = p.shape[0]
    grp, nb, q_spec, band, bias_spec, sink_spec = _attn_specs(t, n_hgrn, n_attn)
    inv = 1.0 / math.sqrt(HEAD)

    def body(q_ref, kp, kc, kn, vp, vc, vn, bias_ref, sink_ref, do_ref,
             dq_ref, dk_ref, dv_ref, dbias_ref, dsink_ref, dk_s, dv_s):
        x, n = pl.program_id(0), pl.program_id(1)

        @pl.when(n == 0)
        def _():
            dk_s[...] = jnp.zeros_like(dk_s)
            dv_s[...] = jnp.zeros_like(dv_s)
            dbias_ref[...] = jnp.zeros_like(dbias_ref)
            dsink_ref[...] = jnp.zeros_like(dsink_ref)

        kb = jnp.concatenate([kp[...], kc[...], kn[...]], axis=0).astype(BF16)
        vb = jnp.concatenate([vp[...], vc[...], vn[...]], axis=0).astype(BF16)
        valid = _attn_valid(n, t)
        dkb = jnp.zeros((KEY_SPAN, HEAD), F32)
        dvb = jnp.zeros((KEY_SPAN, HEAD), F32)
        for g in range(grp):
            qg = q_ref[:, g * HEAD:(g + 1) * HEAD].astype(BF16)
            dog = do_ref[:, g * HEAD:(g + 1) * HEAD].astype(BF16)
            pr, p_sink = _attn_probs(qg, kb, bias_ref[g], sink_ref[0, x * grp + g], valid)
            dpr = _dot_nt(dog, vb)
            delta = jnp.sum(pr * dpr, axis=-1, keepdims=True)
            ds = pr * (dpr - delta)
            ds_b = ds.astype(BF16)
            dq_ref[:, g * HEAD:(g + 1) * HEAD] = (_dot(ds_b, kb) * inv).astype(BF16)
            dkb = dkb + _dot_tn(ds_b, qg) * inv
            dvb = dvb + _dot_tn(pr.astype(BF16), dog)
            dbias_ref[g] += ds
            dsink_ref[g:g + 1, :] += jnp.broadcast_to(
                jnp.sum(-p_sink * delta, axis=0, keepdims=True), (1, WINDOW))
        rows = pl.ds(pl.multiple_of(n * WINDOW, WINDOW), KEY_SPAN)
        dk_s[rows, :] += dkb
        dv_s[rows, :] += dvb

        @pl.when(n == nb - 1)
        def _():
            dk_ref[...] = dk_s[pl.ds(WINDOW, t), :].astype(BF16)
            dv_ref[...] = dv_s[pl.ds(WINDOW, t), :].astype(BF16)

    do_spec = pl.BlockSpec((WINDOW, grp * HEAD), lambda x, n: (n, n_hgrn // grp + x))
    kv_out = pl.BlockSpec((t, HEAD), lambda x, n: (0, x))
    return pl.pallas_call(
        body, name="attn_bwd", grid=(KV_HEADS, nb),
        in_specs=[q_spec] + band + [bias_spec, sink_spec, do_spec],
        out_specs=[pl.BlockSpec((WINDOW, grp * HEAD), lambda x, n: (n, x)), kv_out, kv_out,
                   bias_spec, pl.BlockSpec((None, grp, WINDOW), lambda x, n: (x, 0, 0))],
        out_shape=[jax.ShapeDtypeStruct((t, n_attn * HEAD), BF16),
                   jax.ShapeDtypeStruct((t, KV_HEADS * HEAD), BF16),
                   jax.ShapeDtypeStruct((t, KV_HEADS * HEAD), BF16),
                   jax.ShapeDtypeStruct((n_attn, WINDOW, KEY_SPAN), F32),
                   jax.ShapeDtypeStruct((KV_HEADS, grp, WINDOW), F32)],
        scratch_shapes=[pltpu.VMEM((t + 2 * WINDOW, HEAD), F32), pltpu.VMEM((t + 2 * WINDOW, HEAD), F32)],
        compiler_params=_cparams("parallel", "arbitrary"),
    )(p, p, p, p, p, p, p, bias, sink, dcat)


def _position():
    return lax.axis_index("x"), lax.axis_index("y"), lax.axis_index("c")


def _handshake(peers):
    barrier = pltpu.get_barrier_semaphore()
    for peer in peers:
        pl.semaphore_signal(barrier, inc=1, device_id=peer, device_id_type=MESH)
    pl.semaphore_wait(barrier, len(peers))


def _sequencer(name, collective_id, scratch_types):
    return functools.partial(
        pl.kernel, mesh=plsc.ScalarSubcoreMesh(axis_name="sc", num_cores=1), name=name,
        scratch_types=scratch_types, compiler_params=pltpu.CompilerParams(collective_id=collective_id))


def _all_gather(name, shard, collective_id):
    src = jax.new_ref(shard, memory_space=pltpu.MemorySpace.HBM)
    out = jax.empty_ref(jax.ShapeDtypeStruct((N_DEV,) + shard.shape, shard.dtype),
                        memory_space=pltpu.MemorySpace.HBM)

    @_sequencer(name, collective_id,
                (pltpu.SemaphoreType.DMA((7,)), pltpu.SemaphoreType.DMA((7,)), pltpu.SemaphoreType.DMA))
    def launch(send_sems, recv_sems, local_sem):
        x, y, c = _position()
        me, sibling = (x, y, c), (x, y, 1 - c)
        chips = [(1 - x, y), (x, 1 - y), (1 - x, 1 - y)]
        _handshake([sibling] + [(*chip, c) for chip in chips])

        def slot(px, py, pc):
            return out.at[4 * px + 2 * py + pc]

        def copy(k, block, to, from_src=False):
            return pltpu.make_async_remote_copy(
                src_ref=src if from_src else slot(*block), dst_ref=slot(*block),
                send_sem=send_sems.at[k], recv_sem=recv_sems.at[k], device_id=to, device_id_type=MESH)

        mine = pltpu.make_async_copy(src, slot(*me), local_sem)
        mine.start()
        first = [copy(0, me, sibling, from_src=True)]
        first += [copy(1 + j, me, (*chip, c), from_src=True) for j, chip in enumerate(chips)]
        for cp in first:
            cp.start()
        passed = [copy(4 + j, (*chip, c), sibling) for j, chip in enumerate(chips)]
        for j, chip in enumerate(chips):
            copy(1 + j, (*chip, c), me).wait_recv()
            passed[j].start()
        copy(0, sibling, me).wait_recv()
        for j, chip in enumerate(chips):
            copy(4 + j, (*chip, 1 - c), me).wait_recv()
        for cp in first + passed:
            cp.wait_send()
        mine.wait()

    launch()
    return out[...]


def _pair_exchange(name, stack):
    def body(s_ref, out_ref, send_sems, recv_sems):
        x, y, c = _position()
        sibling = (x, y, 1 - c)
        copies = [pltpu.make_async_remote_copy(
            src_ref=s_ref.at[2 * k + (1 - c)], dst_ref=out_ref.at[k], send_sem=send_sems.at[k],
            recv_sem=recv_sems.at[k], device_id=sibling, device_id_type=MESH) for k in range(4)]
        for cp in copies:
            cp.start()
        for cp in copies:
            cp.wait()

    return pl.pallas_call(
        body, name=name, in_specs=[ANY], out_specs=ANY,
        out_shape=jax.ShapeDtypeStruct((4,) + stack.shape[1:], stack.dtype),
        scratch_shapes=[pltpu.SemaphoreType.DMA((4,)), pltpu.SemaphoreType.DMA((4,))],
    )(stack)


def _pair_sum(name, stack, other, core):
    _, r, c = stack.shape
    tr = _tile(r, 256)

    def body(core_ref, a_ref, b_ref, o_ref):
        o_ref[...] = (a_ref[...].astype(F32) + b_ref[...].astype(F32)).astype(o_ref.dtype)

    grid_spec = pltpu.PrefetchScalarGridSpec(
        num_scalar_prefetch=1, grid=(4, r // tr),
        in_specs=[pl.BlockSpec((None, tr, c), lambda k, i, core_ref: (2 * k + core_ref[0], i, 0)),
                  pl.BlockSpec((None, tr, c), lambda k, i, core_ref: (k, i, 0))],
        out_specs=pl.BlockSpec((None, tr, c), lambda k, i, core_ref: (k, i, 0)))
    return pl.pallas_call(
        body, name=name, grid_spec=grid_spec, out_shape=jax.ShapeDtypeStruct((4, r, c), stack.dtype),
        compiler_params=_cparams("parallel", "parallel"),
    )(core, stack, other)


def _chip_exchange(name, sums, collective_id):
    src = jax.new_ref(sums, memory_space=pltpu.MemorySpace.HBM)
    out = jax.empty_ref(jax.ShapeDtypeStruct((3,) + sums.shape[1:], sums.dtype),
                        memory_space=pltpu.MemorySpace.HBM)

    @_sequencer(name, collective_id, (pltpu.SemaphoreType.DMA((3,)), pltpu.SemaphoreType.DMA((3,))))
    def launch(send_sems, recv_sems):
        x, y, c = _position()
        chips = [(1 - x, y), (x, 1 - y), (1 - x, 1 - y)]
        _handshake([(*chip, c) for chip in chips])
        copies = [pltpu.make_async_remote_copy(
            src_ref=src.at[2 * px + py], dst_ref=out.at[j], send_sem=send_sems.at[j],
            recv_sem=recv_sems.at[j], device_id=(px, py, c), device_id_type=MESH)
            for j, (px, py) in enumerate(chips)]
        for cp in copies:
            cp.start()
        for cp in copies:
            cp.wait()

    launch()
    return out[...]


def _small_all_reduce(name, part):
    r = part.shape[0]

    def body(x_ref, out_ref, gather, send_sems, recv_sems):
        x, y, c = _position()
        me = 4 * x + 2 * y + c
        gather[me] = x_ref[...]
        copies = []
        for k in range(1, N_DEV):
            peer = (x ^ (k >> 2), y ^ ((k >> 1) & 1), c ^ (k & 1))
            copies.append(pltpu.make_async_remote_copy(
                src_ref=x_ref, dst_ref=gather.at[me], send_sem=send_sems.at[k - 1],
                recv_sem=recv_sems.at[k - 1], device_id=peer, device_id_type=MESH))
        for cp in copies:
            cp.start()
        for k in range(1, N_DEV):
            peer_slot = 4 * (x ^ (k >> 2)) + 2 * (y ^ ((k >> 1) & 1)) + (c ^ (k & 1))
            pltpu.make_async_remote_copy(
                src_ref=x_ref, dst_ref=gather.at[peer_slot], send_sem=send_sems.at[k - 1],
                recv_sem=recv_sems.at[k - 1], device_id=(x, y, c), device_id_type=MESH).wait()
        acc = gather[0]
        for j in range(1, N_DEV):
            acc = acc + gather[j]
        out_ref[...] = acc

    vm = pl.BlockSpec(memory_space=pltpu.VMEM)
    return pl.pallas_call(
        body, name=name, in_specs=[vm], out_specs=vm, out_shape=jax.ShapeDtypeStruct((r, LANES), F32),
        scratch_shapes=[pltpu.VMEM((N_DEV, r, LANES), F32), pltpu.SemaphoreType.DMA((7,)),
                        pltpu.SemaphoreType.DMA((7,))],
    )(part)


def _adam_math(w, g, m, v):
    m = ADAM_B1 * m + (1.0 - ADAM_B1) * g
    v = ADAM_B2 * v + (1.0 - ADAM_B2) * jnp.square(g)
    m_hat = m / (1.0 - ADAM_B1 ** ADAM_STEP)
    v_hat = v / (1.0 - ADAM_B2 ** ADAM_STEP)
    delta = -ADAM_LR * (m_hat / (jnp.sqrt(v_hat) + ADAM_EPS) + ADAM_WD * w)
    return delta, m, v


def _adam_shard(name, w, m, v, sums, recv, chip, deps=()):
    r, c = w.shape
    tr = _tile(r, 128)

    def body(chip_ref, w_ref, m_ref, v_ref, own_ref, r0_ref, r1_ref, r2_ref, *rest):
        g_out, d_out, m_out, v_out = rest[-4:]
        g = ((own_ref[...].astype(F32) + r0_ref[...].astype(F32)) + r1_ref[...].astype(F32)) + r2_ref[...].astype(F32)
        delta, m_new, v_new = _adam_math(w_ref[...], g, m_ref[...], v_ref[...])
        g_out[...] = g
        d_out[...] = delta
        m_out[...] = m_new
        v_out[...] = v_new

    plain = pl.BlockSpec((tr, c), lambda i, chip_ref: (i, 0))
    piece = lambda j: pl.BlockSpec((None, tr, c), lambda i, chip_ref: (j, i, 0))
    grid_spec = pltpu.PrefetchScalarGridSpec(
        num_scalar_prefetch=1, grid=(r // tr,),
        in_specs=[plain, plain, plain,
                  pl.BlockSpec((None, tr, c), lambda i, chip_ref: (chip_ref[0], i, 0)),
                  piece(0), piece(1), piece(2)] + [ANY] * len(deps),
        out_specs=[plain] * 4)
    shape = jax.ShapeDtypeStruct((r, c), F32)
    return pl.pallas_call(
        body, name=name, grid_spec=grid_spec, out_shape=[shape] * 4, compiler_params=_cparams("parallel"),
    )(chip, w, m, v, sums, recv, recv, recv, *deps)


def _adam_small(name, w, g, m, v):
    r = w.shape[0]

    def body(w_ref, g_ref, m_ref, v_ref, d_out, m_out, v_out):
        delta, m_new, v_new = _adam_math(w_ref[...], g_ref[...], m_ref[...], v_ref[...])
        d_out[...] = delta
        m_out[...] = m_new
        v_out[...] = v_new

    vm = pl.BlockSpec(memory_space=pltpu.VMEM)
    shape = jax.ShapeDtypeStruct((r, LANES), F32)
    return pl.pallas_call(body, name=name, in_specs=[vm] * 4, out_specs=[vm] * 3, out_shape=[shape] * 3)(w, g, m, v)


def _reduce_scatter(tag, grad_stack, core, collective_id):
    other = _pair_exchange("rs_pair_" + tag, grad_stack)
    sums = _pair_sum("rs_sum_" + tag, grad_stack, other, core)
    return sums, _chip_exchange("rs_chip_" + tag, sums, collective_id)


def _pack(arrays):
    flat = jnp.concatenate([a.reshape(-1).astype(F32) for a in arrays])
    rows = -(-flat.shape[0] // LANES)
    rows = -(-rows // 8) * 8
    return jnp.pad(flat, (0, rows * LANES - flat.shape[0])).reshape(rows, LANES)


def _unpack(packed, like):
    flat = packed.reshape(-1)
    out, off = [], 0
    for a in like:
        out.append(flat[off:off + a.size].reshape(a.shape))
        off += a.size
    return out


SMALL = ("pre_norm_ffn1", "post_norm_ffn1", "pre_norm_mix", "post_norm_mix", "hgrn_lower_bounds_fwd",
         "hgrn_lower_bounds_bwd", "hgrn_out_norm", "attn_sink", "pre_norm_ffn2", "post_norm_ffn2", "rel_bias_table")
BIG = ("w_ffn1_gate_up", "w_ffn1_down", "w_mix_in", "w_mix_out", "w_ffn2_gate_up", "w_ffn2_down")
AG_ID = {n: 1 + i for i, n in enumerate(BIG)}
RS_ID = {n: 7 + i for i, n in enumerate(BIG)}
ORDER = ("pre_norm_ffn1", "post_norm_ffn1", "w_ffn1_gate_up", "w_ffn1_down", "pre_norm_mix", "post_norm_mix",
         "w_mix_in", "hgrn_lower_bounds_fwd", "hgrn_lower_bounds_bwd", "hgrn_out_norm", "attn_sink", "w_mix_out",
         "pre_norm_ffn2", "post_norm_ffn2", "w_ffn2_gate_up", "w_ffn2_down", "rel_bias_table")


def kernel(x, pre_norm_ffn1, post_norm_ffn1, w_ffn1_gate_up, w_ffn1_down, pre_norm_mix, post_norm_mix, w_mix_in, hgrn_lower_bounds_fwd, hgrn_lower_bounds_bwd, hgrn_out_norm, attn_sink, w_mix_out, pre_norm_ffn2, post_norm_ffn2, w_ffn2_gate_up, w_ffn2_down, rel_bias_table, loss_target, m_pre_norm_ffn1, m_post_norm_ffn1, m_w_ffn1_gate_up, m_w_ffn1_down, m_pre_norm_mix, m_post_norm_mix, m_w_mix_in, m_hgrn_lower_bounds_fwd, m_hgrn_lower_bounds_bwd, m_hgrn_out_norm, m_attn_sink, m_w_mix_out, m_pre_norm_ffn2, m_post_norm_ffn2, m_w_ffn2_gate_up, m_w_ffn2_down, m_rel_bias_table, v_pre_norm_ffn1, v_post_norm_ffn1, v_w_ffn1_gate_up, v_w_ffn1_down, v_pre_norm_mix, v_post_norm_mix, v_w_mix_in, v_hgrn_lower_bounds_fwd, v_hgrn_lower_bounds_bwd, v_hgrn_out_norm, v_attn_sink, v_w_mix_out, v_pre_norm_ffn2, v_post_norm_ffn2, v_w_ffn2_gate_up, v_w_ffn2_down, v_rel_bias_table):
    args = dict(locals())
    wts = {n: args[n] for n in ORDER}
    mom = {n: args["m_" + n] for n in ORDER}
    var = {n: args["v_" + n] for n in ORDER}

    x0 = x[0]
    target = loss_target[0]
    t, d = x0.shape
    n_hgrn = d // 2 // HEAD
    n_attn = (d - d // 2) // HEAD
    core = lax.axis_index("c").astype(jnp.int32).reshape(1)
    chip = (2 * lax.axis_index("x") + lax.axis_index("y")).astype(jnp.int32).reshape(1)

    full = {n: _all_gather("ag_" + n, wts[n][0].astype(BF16), AG_ID[n]) for n in BIG}
    w_gu1, w_gu2 = full["w_ffn1_gate_up"], full["w_ffn2_gate_up"]
    w_d1 = full["w_ffn1_down"].reshape(-1, d)
    w_d2 = full["w_ffn2_down"].reshape(-1, d)
    w_out = full["w_mix_out"].reshape(-1, d)
    w_in = full["w_mix_in"]

    g = {n: wts[n] for n in SMALL}
    lb_f = jax.nn.softmax(g["hgrn_lower_bounds_fwd"], axis=0)[0:1]
    lb_b = jax.nn.softmax(g["hgrn_lower_bounds_bwd"], axis=0)[0:1]
    bucket_idx = jnp.asarray(_t5_bucket_index())
    bias = _bias_build(g["rel_bias_table"], bucket_idx)

    n1 = _pre_norm("pre_norm1", x0, g["pre_norm_ffn1"])
    a1, gu1 = _ffn_up("ffn1_gate_up", n1, w_gu1)
    ff1 = _matmul("ffn1_down", a1, w_d1, mode="nn", out_dtype=F32)
    x1, h = _post_res_pre("res1", x0, ff1, g["post_norm_ffn1"], g["pre_norm_mix"], 0.5)
    p = _matmul("mix_in", h, w_in, mode="nn", stack=True, out_dtype=F32)
    y_h, o_raw = _hgrn_fwd(p, lb_f, lb_b, g["hgrn_out_norm"], n_hgrn)
    y_a = _attn_fwd(p, bias, g["attn_sink"], n_hgrn, n_attn)
    cat = jnp.concatenate([y_h, y_a], axis=1)
    mixed = _matmul("mix_out", cat, w_out, mode="nn", out_dtype=F32)
    x2, n2 = _post_res_pre("res2", x1, mixed, g["post_norm_mix"], g["pre_norm_ffn2"], 1.0)
    a2, gu2 = _ffn_up("ffn2_gate_up", n2, w_gu2)
    ff2 = _matmul("ffn2_down", a2, w_d2, mode="nn", out_dtype=F32)
    dy3, loss_part = _post_res_loss("res3_loss", x2, ff2, g["post_norm_ffn2"], target, 0.5)

    small_grad = {}
    scattered = {}

    def scatter(name, grad_stack):
        scattered[name] = _reduce_scatter(name, grad_stack, core, RS_ID[name])
        return [scattered[name][0]]

    def ffn_bwd(tag, dy, ff, a, gu, n_in, x_in, w_gu, w_d, post_name, pre_name, gu_name, d_name):
        dff, small_grad[post_name] = _post_bwd("post_bwd" + tag, dy, ff, g[post_name], 0.5)
        dep = scatter(d_name, _matmul("dw_down" + tag, a, dff, mode="tn", out_dtype=BF16).reshape(N_DEV, -1, d))
        dgu = _ffn_dact("d_act" + tag, dff, w_d, gu, deps=dep)
        dn = _matmul("d_norm" + tag, dgu, w_gu, mode="nt", stack=True, halves=True, out_dtype=F32)
        dep = scatter(gu_name, _matmul("dw_gate_up" + tag, n_in, dgu, mode="tn", stack=True, halves=True,
                                       out_dtype=BF16))
        dx, small_grad[pre_name] = _pre_bwd("pre_bwd" + tag, dn, x_in, g[pre_name], dy, deps=dep)
        return dx

    dx2 = ffn_bwd("2", dy3, ff2, a2, gu2, n2, x2, w_gu2, w_d2, "post_norm_ffn2", "pre_norm_ffn2",
                  "w_ffn2_gate_up", "w_ffn2_down")

    dmixed, small_grad["post_norm_mix"] = _post_bwd("post_bwd_mix", dx2, mixed, g["post_norm_mix"], 1.0)
    dcat = _matmul("d_cat", dmixed, w_out, mode="nt", out_dtype=F32)
    dep = scatter("w_mix_out", _matmul("dw_mix_out", cat, dmixed, mode="tn", out_dtype=BF16).reshape(N_DEV, -1, d))
    dq_h, di_h, dzf, dzb, dg_h, dlb_f, dlb_b, small_grad["hgrn_out_norm"] = _hgrn_bwd(
        p, o_raw, dcat, lb_f, lb_b, g["hgrn_out_norm"], n_hgrn)
    dq_a, dk_a, dv_a, dbias, dsink_rows = _attn_bwd(p, dcat, bias, g["attn_sink"], n_hgrn, n_attn)
    dp = jnp.concatenate([dq_h, di_h, dzf, dzb, dg_h, dq_a, dk_a, dv_a], axis=1)
    dh = _matmul("d_h", dp, w_in, mode="nt", stack=True, out_dtype=F32, deps=dep)
    dep = scatter("w_mix_in", _matmul("dw_mix_in", h, dp, mode="tn", stack=True, out_dtype=BF16))
    dx1, small_grad["pre_norm_mix"] = _pre_bwd("pre_bwd_mix", dh, x1, g["pre_norm_mix"], dx2, deps=dep)

    dx0 = ffn_bwd("1", dx1, ff1, a1, gu1, n1, x0, w_gu1, w_d1, "post_norm_ffn1", "pre_norm_ffn1",
                  "w_ffn1_gate_up", "w_ffn1_down")

    def lb_grad(dlb, lb):
        da0 = dlb * lb * (1.0 - lb)
        return jnp.concatenate([da0, -da0], axis=0)

    small_grad["hgrn_lower_bounds_fwd"] = lb_grad(dlb_f, lb_f)
    small_grad["hgrn_lower_bounds_bwd"] = lb_grad(dlb_b, lb_b)
    small_grad["attn_sink"] = dsink_rows[:, :, 0].reshape(1, n_attn)
    small_grad["rel_bias_table"] = jnp.transpose(_bias_reduce(dbias, bucket_idx)[:, :, 0])

    parts = [small_grad[n] for n in SMALL] + [loss_part[:, 0:1]]
    red = _small_all_reduce("small_all_reduce", _pack(parts))
    red_list = _unpack(red, parts)
    loss = red_list[-1].reshape(())
    sg = dict(zip(SMALL, red_list[:-1]))
    like = [wts[n] for n in SMALL]
    d_s, m_s, v_s = _adam_small("adam_small", _pack(like), _pack([sg[n] for n in SMALL]),
                                _pack([mom[n] for n in SMALL]), _pack([var[n] for n in SMALL]))
    grads = dict(sg)
    delta = dict(zip(SMALL, _unpack(d_s, like)))
    new_m = dict(zip(SMALL, _unpack(m_s, like)))
    new_v = dict(zip(SMALL, _unpack(v_s, like)))

    dep = []
    for n in ("w_ffn2_down", "w_ffn2_gate_up", "w_mix_out", "w_mix_in", "w_ffn1_down", "w_ffn1_gate_up"):
        sums, recv = scattered[n]
        gr, de, nm, nv = _adam_shard("adam_" + n, wts[n][0], mom[n][0], var[n][0], sums, recv, chip, deps=dep)
        grads[n], delta[n], new_m[n], new_v[n] = gr[None], de[None], nm[None], nv[None]
        dep = [gr]

    return (loss, dx0[None], *[grads[n] for n in ORDER], *[delta[n] for n in ORDER],
            *[new_m[n] for n in ORDER], *[new_v[n] for n in ORDER])
```

```python
import functools
import math

import numpy as np
import jax
import jax.numpy as jnp
from jax import lax
from jax.experimental import pallas as pl
from jax.experimental.pallas import tpu as pltpu
from jax.experimental.pallas import tpu_sc as plsc

F32 = jnp.float32
BF16 = jnp.bfloat16
HIGHEST = lax.Precision.HIGHEST
MESH = pl.DeviceIdType.MESH

N_DEV = 8
EPS = 1e-6
NEG_INF = -1e30
HEAD = 128
CHUNK = 64
WINDOW = 128
KEY_SPAN = 3 * WINDOW
KV_HEADS = 2
REL_BUCKETS = 32
REL_MAX_DIST = 128
ADAM_LR, ADAM_B1, ADAM_B2, ADAM_EPS, ADAM_WD, ADAM_STEP = 0.001, 0.9, 0.999, 1e-08, 0.01, 10
LANES = 128
VMEM_LIMIT = 56 * 1024 * 1024
ANY = pl.BlockSpec(memory_space=pl.ANY)


def _cparams(*sem):
    return pltpu.CompilerParams(dimension_semantics=sem if sem else None, vmem_limit_bytes=VMEM_LIMIT)


def _dot(a, b):
    return jnp.dot(a, b, preferred_element_type=F32)


def _dot_nt(a, b):
    return lax.dot_general(a, b, (((1,), (1,)), ((), ())), preferred_element_type=F32)


def _dot_tn(a, b):
    return lax.dot_general(a, b, (((0,), (0,)), ((), ())), preferred_element_type=F32)


def _tile(dim, target):
    for c in (target, 1024, 512, 256, 128):
        if c <= target and dim % c == 0:
            return c
    return dim


K_WHOLE = 2048
K_STEP = 2816


def _k_tile(kd):
    if kd <= K_WHOLE:
        return kd
    return max(c for c in range(LANES, K_STEP + 1, LANES) if kd % c == 0)


def _matmul(name, a, b, *, mode, out_dtype, stack=False, halves=False, tm=1024, tn=1024, deps=()):
    grp = 1
    if mode == "nn":
        m, kd = a.shape
        n = b.shape[0] * b.shape[2] if stack else b.shape[1]
    elif mode == "nt":
        m = a.shape[-2]
        n, kd = (b.shape[1], b.shape[0] * b.shape[2]) if stack else b.shape
    else:
        kd, m = a.shape
        n = b.shape[-1] * (2 if halves else 1)
    if stack:
        n1 = b.shape[2] if mode != "tn" else n // N_DEV
        if mode == "nt":
            grp = 2 if 2 * n1 <= K_STEP else 1
            tk = grp * n1
        else:
            grp = 1 if n1 % LANES == 0 else 2
            tn = grp * n1
        assert (grp * n1) % LANES == 0
    per_half = N_DEV // 2 // grp
    tm = _tile(m, tm)
    if not (stack and mode in ("nn", "tn")):
        tn = _tile(n, tn)
    if not (stack and mode == "nt"):
        tk = _k_tile(kd)
    nk = kd // tk
    lead = None if grp == 1 else grp
    b_outer = nk == 1 and b.size > a.size
    grid = (n // tn, m // tm, nk) if b_outer else (m // tm, n // tn, nk)

    def spec(shape, index):
        return pl.BlockSpec(shape, (lambda g0, g1, k: index(g1, g0, k)) if b_outer else index)

    if mode == "nn":
        a_spec = spec((tm, tk), lambda i, j, k: (i, k))
        if stack:
            b_spec = spec((lead, tk, n1), lambda i, j, k: (j, k, 0))
        else:
            b_spec = spec((tk, tn), lambda i, j, k: (k, j))
        dot = _dot
    elif mode == "nt":
        if halves:
            a_spec = spec((None, tm, tk), lambda i, j, k: (k // per_half, i, k % per_half))
        else:
            a_spec = spec((tm, tk), lambda i, j, k: (i, k))
        if stack:
            b_spec = spec((lead, tn, n1), lambda i, j, k: (k, j, 0))
        else:
            b_spec = spec((tn, tk), lambda i, j, k: (j, k))
        dot = _dot_nt
    else:
        a_spec = spec((tk, tm), lambda i, j, k: (k, i))
        if halves:
            b_spec = spec((None, tk, tn), lambda i, j, k: (j // per_half, k, j % per_half))
        else:
            b_spec = spec((tk, tn), lambda i, j, k: (k, j))
        dot = _dot_tn
    if stack and mode == "tn":
        out_shape = jax.ShapeDtypeStruct((N_DEV, m, n1), out_dtype)
        o_spec = spec((lead, tm, n1), lambda i, j, k: (j, i, 0))
    else:
        out_shape = jax.ShapeDtypeStruct((m, n), out_dtype)
        o_spec = spec((tm, tn), lambda i, j, k: (i, j))
    b_grouped = stack and grp > 1 and mode != "tn"
    o_grouped = stack and grp > 1 and mode == "tn"

    def product(a_ref, b_ref):
        bmat = jnp.concatenate([b_ref[s] for s in range(grp)], axis=1) if b_grouped else b_ref[...]
        return dot(a_ref[...], bmat)

    def store(o_ref, val):
        if o_grouped:
            for s in range(grp):
                o_ref[s] = val[:, s * n1:(s + 1) * n1].astype(o_ref.dtype)
        else:
            o_ref[...] = val.astype(o_ref.dtype)

    def body_whole(a_ref, b_ref, *rest):
        store(rest[-1], product(a_ref, b_ref))

    def body_steps(a_ref, b_ref, *rest):
        o_ref, acc_ref = rest[-2:]
        k = pl.program_id(2)

        @pl.when(k == 0)
        def _():
            acc_ref[...] = product(a_ref, b_ref)

        @pl.when(k > 0)
        def _():
            acc_ref[...] += product(a_ref, b_ref)

        @pl.when(k == nk - 1)
        def _():
            store(o_ref, acc_ref[...])

    return pl.pallas_call(
        body_whole if nk == 1 else body_steps, name=name, grid=grid,
        in_specs=[a_spec, b_spec] + [ANY] * len(deps), out_specs=o_spec, out_shape=out_shape,
        scratch_shapes=[] if nk == 1 else [pltpu.VMEM((tm, tn), F32)],
        compiler_params=_cparams("parallel", "parallel", "arbitrary"),
    )(a, b, *deps)


def _ffn_up(name, n, w_stack):
    t, d = n.shape
    s, _, n1 = w_stack.shape
    half = s // 2
    tm = _tile(t, 512)

    def body(n_ref, wg_ref, wu_ref, act_ref, gu_ref):
        nv = n_ref[...]
        gate = _dot(nv, wg_ref[...])
        up = _dot(nv, wu_ref[...])
        act_ref[...] = (gate * jax.nn.sigmoid(gate) * up).astype(BF16)
        gu_ref[0] = gate.astype(BF16)
        gu_ref[1] = up.astype(BF16)

    return pl.pallas_call(
        body, name=name, grid=(half, t // tm),
        in_specs=[pl.BlockSpec((tm, d), lambda j, i: (i, 0)),
                  pl.BlockSpec((None, d, n1), lambda j, i: (j, 0, 0)),
                  pl.BlockSpec((None, d, n1), lambda j, i: (half + j, 0, 0))],
        out_specs=[pl.BlockSpec((tm, n1), lambda j, i: (i, j)), pl.BlockSpec((2, tm, n1), lambda j, i: (0, i, j))],
        out_shape=[jax.ShapeDtypeStruct((t, half * n1), BF16), jax.ShapeDtypeStruct((2, t, half * n1), BF16)],
        compiler_params=_cparams("parallel", "parallel"),
    )(n, w_stack, w_stack)


def _ffn_dact(name, dff, w_d, gu, deps=()):
    t, d = dff.shape
    f = w_d.shape[0]
    tm = _tile(t, 512)
    tn = _tile(f, 1408)

    def body(dff_ref, w_ref, gu_ref, *rest):
        dgu_ref = rest[-1]
        da = _dot_nt(dff_ref[...], w_ref[...])
        gate = gu_ref[0].astype(F32)
        sg = jax.nn.sigmoid(gate)
        dgu_ref[0] = (da * gu_ref[1].astype(F32) * (sg * (1.0 + gate * (1.0 - sg)))).astype(BF16)
        dgu_ref[1] = (da * (gate * sg)).astype(BF16)

    pair = pl.BlockSpec((2, tm, tn), lambda j, i: (0, i, j))
    return pl.pallas_call(
        body, name=name, grid=(f // tn, t // tm),
        in_specs=[pl.BlockSpec((tm, d), lambda j, i: (i, 0)), pl.BlockSpec((tn, d), lambda j, i: (j, 0)), pair]
        + [ANY] * len(deps),
        out_specs=pair, out_shape=jax.ShapeDtypeStruct((2, t, f), BF16),
        compiler_params=_cparams("parallel", "parallel"),
    )(dff, w_d, gu, *deps)


ROWS = 256


def _rstd(xf):
    return lax.rsqrt(jnp.mean(xf * xf, axis=-1, keepdims=True) + EPS)


def _row_spec(t, d):
    return pl.BlockSpec((min(ROWS, t), d), lambda i: (i, 0))


def _vec_spec(d):
    return pl.BlockSpec((1, d), lambda i: (0, 0))


def _pre_norm(name, x, gain):
    t, d = x.shape

    def body(x_ref, g_ref, n_ref):
        xf = x_ref[...]
        n_ref[...] = (xf * _rstd(xf) * g_ref[...]).astype(BF16)

    return pl.pallas_call(
        body, name=name, grid=(t // min(ROWS, t),), in_specs=[_row_spec(t, d), _vec_spec(d)],
        out_specs=_row_spec(t, d), out_shape=jax.ShapeDtypeStruct((t, d), BF16),
        compiler_params=_cparams("parallel"),
    )(x, gain)


def _post_res_pre(name, x, ff, g_post, g_next, scale):
    t, d = x.shape

    def body(x_ref, ff_ref, gp_ref, gn_ref, xo_ref, n_ref):
        ff_ = ff_ref[...]
        xn = x_ref[...] + scale * (ff_ * _rstd(ff_) * gp_ref[...])
        xo_ref[...] = xn
        n_ref[...] = (xn * _rstd(xn) * gn_ref[...]).astype(BF16)

    return pl.pallas_call(
        body, name=name, grid=(t // min(ROWS, t),),
        in_specs=[_row_spec(t, d), _row_spec(t, d), _vec_spec(d), _vec_spec(d)],
        out_specs=[_row_spec(t, d), _row_spec(t, d)],
        out_shape=[jax.ShapeDtypeStruct((t, d), F32), jax.ShapeDtypeStruct((t, d), BF16)],
        compiler_params=_cparams("parallel"),
    )(x, ff, g_post, g_next)


def _post_res_loss(name, x, ff, g_post, target, scale):
    t, d = x.shape

    def body(x_ref, ff_ref, gp_ref, tg_ref, dy_ref, loss_ref):
        ff_ = ff_ref[...]
        err = x_ref[...] + scale * (ff_ * _rstd(ff_) * gp_ref[...]) - tg_ref[...]
        dy_ref[...] = err / d
        part = 0.5 * jnp.sum(jnp.mean(err * err, axis=-1, keepdims=True), axis=0, keepdims=True)

        @pl.when(pl.program_id(0) == 0)
        def _():
            loss_ref[...] = jnp.zeros_like(loss_ref)

        loss_ref[...] += jnp.broadcast_to(part, loss_ref.shape)

    return pl.pallas_call(
        body, name=name, grid=(t // min(ROWS, t),),
        in_specs=[_row_spec(t, d), _row_spec(t, d), _vec_spec(d), _row_spec(t, d)],
        out_specs=[_row_spec(t, d), _vec_spec(LANES)],
        out_shape=[jax.ShapeDtypeStruct((t, d), F32), jax.ShapeDtypeStruct((1, LANES), F32)],
        compiler_params=_cparams("arbitrary"),
    )(x, ff, g_post, target)


def _post_bwd(name, dy, ff, g_post, scale):
    t, d = dy.shape

    def body(dy_ref, ff_ref, gp_ref, dff_ref, dg_ref):
        ff_ = ff_ref[...]
        r = _rstd(ff_)
        xh = ff_ * r
        dyn = scale * dy_ref[...]
        dxh = dyn * gp_ref[...]
        dff_ref[...] = (r * (dxh - xh * jnp.mean(dxh * xh, axis=-1, keepdims=True))).astype(BF16)

        @pl.when(pl.program_id(0) == 0)
        def _():
            dg_ref[...] = jnp.zeros_like(dg_ref)

        dg_ref[...] += jnp.sum(dyn * xh, axis=0, keepdims=True)

    return pl.pallas_call(
        body, name=name, grid=(t // min(ROWS, t),),
        in_specs=[_row_spec(t, d), _row_spec(t, d), _vec_spec(d)],
        out_specs=[_row_spec(t, d), _vec_spec(d)],
        out_shape=[jax.ShapeDtypeStruct((t, d), BF16), jax.ShapeDtypeStruct((1, d), F32)],
        compiler_params=_cparams("arbitrary"),
    )(dy, ff, g_post)


def _pre_bwd(name, dn, x, g_pre, dy, deps=()):
    t, d = x.shape

    def body(dn_ref, x_ref, g_ref, dy_ref, *rest):
        dx_ref, dg_ref = rest[-2:]
        xf = x_ref[...]
        r = _rstd(xf)
        xh = xf * r
        dnf = dn_ref[...].astype(F32)
        dxh = dnf * g_ref[...]
        dx_ref[...] = dy_ref[...] + r * (dxh - xh * jnp.mean(dxh * xh, axis=-1, keepdims=True))

        @pl.when(pl.program_id(0) == 0)
        def _():
            dg_ref[...] = jnp.zeros_like(dg_ref)

        dg_ref[...] += jnp.sum(dnf * xh, axis=0, keepdims=True)

    return pl.pallas_call(
        body, name=name, grid=(t // min(ROWS, t),),
        in_specs=[_row_spec(t, d), _row_spec(t, d), _vec_spec(d), _row_spec(t, d)] + [ANY] * len(deps),
        out_specs=[_row_spec(t, d), _vec_spec(d)],
        out_shape=[jax.ShapeDtypeStruct((t, d), F32), jax.ShapeDtypeStruct((1, d), F32)],
        compiler_params=_cparams("arbitrary"),
    )(dn, x, g_pre, dy, *deps)


def _bdot(a, b, ca, cb, precision=None):
    return lax.dot_general(a, b, (((ca,), (cb,)), ((0,), (0,))), preferred_element_type=F32, precision=precision)


def _tri_masks(g):
    row = lax.broadcasted_iota(jnp.int32, (g, CHUNK, CHUNK), 1)
    col = lax.broadcasted_iota(jnp.int32, (g, CHUNK, CHUNK), 2)
    return col <= row, col >= row


def _hgrn_block(z, lb, q, v, cum_mat):
    sg = jax.nn.sigmoid(z)
    f = lb + (1.0 - lb) * sg
    lf = jnp.log(f)
    k = 1.0 - f
    a = _bdot(cum_mat, lf, 2, 1, HIGHEST)
    last = jnp.sum(lf, axis=1, keepdims=True)
    e_a = jnp.exp(a)
    e_na = jnp.exp(-a)
    e_t = jnp.exp(last - a)
    return dict(sg=sg, f=f, k=k, decay=jnp.exp(last), e_a=e_a, e_na=e_na, e_t=e_t,
                qd=q * e_a, kd=k * e_na, kt=k * e_t)


def _hgrn_states(state, kv, decay, order):
    entering = [None] * len(order)
    for g in order:
        entering[g] = state
        state = decay[g] * state + kv[g]
    return jnp.stack(entering, axis=0), state


def _hgrn_fwd(p, lb_f, lb_b, gain, n_heads):
    t = p.shape[0]
    w = n_heads * HEAD
    blk = min(8, t // CHUNK)
    rows_blk = blk * CHUNK
    n_blocks = t // rows_blk
    fin_rows = min(256, t)

    def body(q_ref, i_ref, zf_ref, zb_ref, g_ref, lbf_ref, lbb_ref, gain_ref, y_ref, o_ref, st_ref):
        low, up = _tri_masks(blk)
        m_low, m_up = low.astype(F32), up.astype(F32)
        o_ref[...] = jnp.zeros_like(o_ref)
        st_ref[...] = jnp.zeros_like(st_ref)

        def one(r0, z_ref, lb, slot, rev):
            rows = pl.ds(r0, rows_blk)
            split = lambda ref: ref[rows, :].reshape(blk, CHUNK, HEAD)
            q, v = split(q_ref), split(i_ref)
            c = _hgrn_block(split(z_ref), lb, q, v, m_up if rev else m_low)
            qd, kd, kt, vb = c["qd"].astype(BF16), c["kd"].astype(BF16), c["kt"].astype(BF16), v.astype(BF16)
            pm = jnp.where(up if rev else low, _bdot(qd, kd, 2, 2), 0.0).astype(BF16)
            kv = _bdot(vb, kt, 1, 1)
            order = range(blk - 1, -1, -1) if rev else range(blk)
            entering, st_ref[slot] = _hgrn_states(st_ref[slot], kv, c["decay"], order)
            o = _bdot(pm, vb, 2, 1) + _bdot(qd, entering.astype(BF16), 2, 2)
            o_ref[rows, :] += o.reshape(rows_blk, HEAD)

        def step(n, carry):
            one(pl.multiple_of(n * rows_blk, rows_blk), zf_ref, lbf_ref[...], 0, False)
            one(pl.multiple_of((n_blocks - 1 - n) * rows_blk, rows_blk), zb_ref, lbb_ref[...], 1, True)
            return carry

        lax.fori_loop(0, n_blocks, step, 0)

        def fin(n, carry):
            rows = pl.ds(pl.multiple_of(n * fin_rows, fin_rows), fin_rows)
            o = o_ref[rows, :]
            g = g_ref[rows, :]
            y_ref[rows, :] = (o * _rstd(o) * gain_ref[...] * (g * jax.nn.sigmoid(g))).astype(BF16)
            return carry

        lax.fori_loop(0, t // fin_rows, fin, 0)

    col = lambda grp: pl.BlockSpec((t, HEAD), lambda h: (0, grp * n_heads + h))
    vec = pl.BlockSpec((1, HEAD), lambda h: (0, h))
    out = pl.BlockSpec((t, HEAD), lambda h: (0, h))
    return pl.pallas_call(
        body, name="hgrn_fwd", grid=(n_heads,),
        in_specs=[col(0), col(1), col(2), col(3), col(4), vec, vec, vec],
        out_specs=[out, out],
        out_shape=[jax.ShapeDtypeStruct((t, w), BF16), jax.ShapeDtypeStruct((t, w), F32)],
        scratch_shapes=[pltpu.VMEM((2, HEAD, HEAD), F32)],
        compiler_params=_cparams("parallel"),
    )(p, p, p, p, p, lb_f, lb_b, gain)


def _hgrn_bwd(p, o_raw, dcat, lb_f, lb_b, gain, n_heads):
    t = p.shape[0]
    w = n_heads * HEAD
    n_chunks = t // CHUNK
    blk = min(8, n_chunks)
    rows_blk = blk * CHUNK
    n_blocks = t // rows_blk
    rb = min(256, t)

    def body(q_ref, i_ref, zf_ref, zb_ref, g_ref, o_ref, dy_ref, lbf_ref, lbb_ref, gain_ref,
             dq_ref, di_ref, dzf_ref, dzb_ref, dg_ref, dlbf_ref, dlbb_ref, dgain_ref,
             do_s, dq_s, dv_s, st_s, cur_s):
        low, up = _tri_masks(blk)
        m_low, m_up = low.astype(F32), up.astype(F32)
        rowid = lax.broadcasted_iota(jnp.int32, (blk, CHUNK, HEAD), 1)
        gain_v = gain_ref[...]

        def norm_bwd(n, dgain):
            rows = pl.ds(pl.multiple_of(n * rb, rb), rb)
            o = o_ref[rows, :]
            g = g_ref[rows, :]
            dy = dy_ref[rows, :]
            r = _rstd(o)
            oh = o * r
            sg = jax.nn.sigmoid(g)
            dg_ref[rows, :] = (dy * oh * gain_v * (sg * (1.0 + g * (1.0 - sg)))).astype(BF16)
            dno = dy * (g * sg)
            dxh = dno * gain_v
            do_s[rows, :] = r * (dxh - oh * jnp.mean(dxh * oh, axis=-1, keepdims=True))
            return dgain + jnp.sum(dno * oh, axis=0, keepdims=True)

        dgain_ref[...] = lax.fori_loop(0, t // rb, norm_bwd, jnp.zeros((1, HEAD), F32))
        dq_s[...] = jnp.zeros_like(dq_s)
        dv_s[...] = jnp.zeros_like(dv_s)

        def direction(z_ref, lb_ref, dz_ref, dlb_ref, rev):
            lb = lb_ref[...]
            cum_mat = m_up if rev else m_low
            cum_mat_t = m_low if rev else m_up
            mask = up if rev else low
            last_row = 0 if rev else CHUNK - 1

            order = range(blk - 1, -1, -1) if rev else range(blk)

            def rows_of(j):
                bidx = (n_blocks - 1 - j) if rev else j
                return bidx, pl.ds(pl.multiple_of(bidx * rows_blk, rows_blk), rows_blk)

            def load(rows):
                split = lambda ref: ref[rows, :].reshape(blk, CHUNK, HEAD)
                q, v = split(q_ref), split(i_ref)
                return q, v, _hgrn_block(split(z_ref), lb, q, v, cum_mat)

            def sweep_fwd(j, carry):
                bidx, rows = rows_of(j)
                _, v, c = load(rows)
                kv = _bdot(v.astype(BF16), c["kt"].astype(BF16), 1, 1)
                st_s[pl.ds(bidx * blk, blk)], cur_s[0] = _hgrn_states(cur_s[0], kv, c["decay"], order)
                return carry

            cur_s[...] = jnp.zeros_like(cur_s)
            dlb_ref[...] = jnp.zeros_like(dlb_ref)
            lax.fori_loop(0, n_blocks, sweep_fwd, 0)

            def sweep_bwd(jj, carry):
                bidx, rows = rows_of(n_blocks - 1 - jj)
                _, v, c = load(rows)
                st = st_s[pl.ds(bidx * blk, blk)]
                do = do_s[rows, :].reshape(blk, CHUNK, HEAD)
                qd, kd, kt, decay = c["qd"], c["kd"], c["kt"], c["decay"]
                qd_b, kd_b, kt_b = qd.astype(BF16), kd.astype(BF16), kt.astype(BF16)
                v_b, do_b, st_b = v.astype(BF16), do.astype(BF16), st.astype(BF16)
                pm = jnp.where(mask, _bdot(qd_b, kd_b, 2, 2), 0.0).astype(BF16)
                dpm = jnp.where(mask, _bdot(do_b, v_b, 2, 2), 0.0).astype(BF16)
                gq = _bdot(do_b, qd_b, 1, 1)
                dstate = cur_s[1]
                after = [None] * blk
                for g in reversed(order):
                    after[g] = dstate
                    dstate = gq[g] + decay[g] * dstate
                cur_s[1] = dstate
                dst = jnp.stack(after, axis=0)
                dst_b = dst.astype(BF16)
                dv = _bdot(pm, do_b, 1, 1) + _bdot(kt_b, dst_b, 2, 2)
                dqd = _bdot(dpm, kd_b, 2, 1) + _bdot(do_b, st_b, 2, 1)
                dkd = _bdot(dpm, qd_b, 1, 1)
                dkt = _bdot(v_b, dst_b, 2, 1)
                dlast = (jnp.sum(dkt * kt, axis=1, keepdims=True)
                         + decay * jnp.sum(dst * st, axis=1, keepdims=True))
                dq_s[rows, :] += (dqd * c["e_a"]).reshape(rows_blk, HEAD)
                dv_s[rows, :] += dv.reshape(rows_blk, HEAD)
                dk = dkd * c["e_na"] + dkt * c["e_t"]
                da = dqd * qd - dkd * kd - dkt * kt
                da = da + jnp.where(rowid == last_row, dlast, 0.0)
                dlf = _bdot(cum_mat_t, da, 2, 1, HIGHEST)
                df = dlf / c["f"] - dk
                sg = c["sg"]
                dz_ref[rows, :] = (df * (1.0 - lb) * (sg * (1.0 - sg))).reshape(rows_blk, HEAD).astype(BF16)
                dlb_ref[...] += jnp.sum((df * (1.0 - sg)).reshape(rows_blk, HEAD), axis=0, keepdims=True)
                return carry

            lax.fori_loop(0, n_blocks, sweep_bwd, 0)

        direction(zf_ref, lbf_ref, dzf_ref, dlbf_ref, False)
        direction(zb_ref, lbb_ref, dzb_ref, dlbb_ref, True)
        dq_ref[...] = dq_s[...].astype(BF16)
        di_ref[...] = dv_s[...].astype(BF16)

    col = lambda grp: pl.BlockSpec((t, HEAD), lambda h: (0, grp * n_heads + h))
    one = pl.BlockSpec((t, HEAD), lambda h: (0, h))
    vec = pl.BlockSpec((1, HEAD), lambda h: (0, h))
    big = jax.ShapeDtypeStruct((t, w), BF16)
    small = jax.ShapeDtypeStruct((1, w), F32)
    return pl.pallas_call(
        body, name="hgrn_bwd", grid=(n_heads,),
        in_specs=[col(0), col(1), col(2), col(3), col(4), one, one, vec, vec, vec],
        out_specs=[one] * 5 + [vec] * 3,
        out_shape=[big] * 5 + [small] * 3,
        scratch_shapes=[pltpu.VMEM((t, HEAD), F32), pltpu.VMEM((t, HEAD), F32), pltpu.VMEM((t, HEAD), F32),
                        pltpu.VMEM((n_chunks, HEAD, HEAD), F32), pltpu.VMEM((2, HEAD, HEAD), F32)],
        compiler_params=_cparams("parallel"),
    )(p, p, p, p, p, o_raw, dcat, lb_f, lb_b, gain)


def _t5_bucket_index():
    c = np.arange(WINDOW)[:, None]
    s = np.arange(KEY_SPAN)[None, :]
    rel = s - WINDOW - c
    nb = REL_BUCKETS // 2
    max_exact = nb // 2
    bucket = (rel > 0).astype(np.int32) * nb
    n = np.abs(rel)
    large = max_exact + (np.log(np.maximum(n, 1) / max_exact) / np.log(REL_MAX_DIST / max_exact)
                         * (nb - max_exact)).astype(np.int32)
    large = np.minimum(large, nb - 1)
    return bucket + np.where(n < max_exact, n, large).astype(np.int32)


def _bias_build(table, idx):
    n_attn = table.shape[1]

    def body(tab_ref, idx_ref, o_ref):
        h = pl.program_id(0)
        idx_v = idx_ref[...]
        acc = jnp.zeros((WINDOW, KEY_SPAN), F32)
        for b in range(REL_BUCKETS):
            acc = jnp.where(idx_v == b, tab_ref[b, h], acc)
        o_ref[...] = acc

    return pl.pallas_call(
        body, name="bias_build", grid=(n_attn,),
        in_specs=[pl.BlockSpec(memory_space=pltpu.SMEM), pl.BlockSpec((WINDOW, KEY_SPAN), lambda h: (0, 0))],
        out_specs=pl.BlockSpec((None, WINDOW, KEY_SPAN), lambda h: (h, 0, 0)),
        out_shape=jax.ShapeDtypeStruct((n_attn, WINDOW, KEY_SPAN), F32), compiler_params=_cparams("parallel"),
    )(table, idx)


def _bias_reduce(dbias, idx):
    n_attn = dbias.shape[0]

    def body(idx_ref, d_ref, o_ref):
        idx_v = idx_ref[...]
        dv = d_ref[...]
        rows = lax.broadcasted_iota(jnp.int32, (REL_BUCKETS, LANES), 0)
        acc = jnp.zeros((REL_BUCKETS, LANES), F32)
        for b in range(REL_BUCKETS):
            part = jnp.sum(jnp.where(idx_v == b, dv, 0.0), axis=1, keepdims=True)
            acc = jnp.where(rows == b, jnp.sum(part, axis=0, keepdims=True), acc)
        o_ref[...] = acc

    return pl.pallas_call(
        body, name="bias_reduce", grid=(n_attn,),
        in_specs=[pl.BlockSpec((WINDOW, KEY_SPAN), lambda h: (0, 0)),
                  pl.BlockSpec((None, WINDOW, KEY_SPAN), lambda h: (h, 0, 0))],
        out_specs=pl.BlockSpec((None, REL_BUCKETS, LANES), lambda h: (h, 0, 0)),
        out_shape=jax.ShapeDtypeStruct((n_attn, REL_BUCKETS, LANES), F32), compiler_params=_cparams("parallel"),
    )(idx, dbias)


def _attn_probs(qg, kb, bias, sink, valid):
    s = _dot_nt(qg, kb) / math.sqrt(HEAD) + bias
    s = jnp.where(valid, s, NEG_INF)
    m = jnp.maximum(jnp.max(s, axis=-1, keepdims=True), sink)
    e = jnp.exp(s - m)
    e_sink = jnp.exp(sink - m)
    den = jnp.sum(e, axis=-1, keepdims=True) + e_sink
    return e / den, e_sink / den


def _attn_valid(n, t):
    c = lax.broadcasted_iota(jnp.int32, (WINDOW, KEY_SPAN), 0)
    s = lax.broadcasted_iota(jnp.int32, (WINDOW, KEY_SPAN), 1)
    rel = s - WINDOW - c
    key_pos = n * WINDOW - WINDOW + s
    return (jnp.abs(rel) <= WINDOW) & (key_pos >= 0) & (key_pos < t)


def _attn_specs(t, n_hgrn, n_attn):
    grp = n_attn // KV_HEADS
    nb = t // WINDOW
    cq = 5 * n_hgrn
    ck = cq + n_attn
    cv = ck + KV_HEADS
    q_spec = pl.BlockSpec((WINDOW, grp * HEAD), lambda x, n: (n, cq // grp + x))
    kv = lambda base, off: pl.BlockSpec(
        (WINDOW, HEAD), lambda x, n: (jnp.clip(n + off, 0, nb - 1), base + x))
    band = [kv(ck, -1), kv(ck, 0), kv(ck, 1), kv(cv, -1), kv(cv, 0), kv(cv, 1)]
    bias_spec = pl.BlockSpec((grp, WINDOW, KEY_SPAN), lambda x, n: (x, 0, 0))
    sink_spec = pl.BlockSpec(memory_space=pltpu.SMEM)
    return grp, nb, q_spec, band, bias_spec, sink_spec


def _attn_fwd(p, bias, sink, n_hgrn, n_attn):
    t = p.shape[0]
    grp, nb, q_spec, band, bias_spec, sink_spec = _attn_specs(t, n_hgrn, n_attn)

    def body(q_ref, kp, kc, kn, vp, vc, vn, bias_ref, sink_ref, y_ref):
        x, n = pl.program_id(0), pl.program_id(1)
        kb = jnp.concatenate([kp[...], kc[...], kn[...]], axis=0).astype(BF16)
        vb = jnp.concatenate([vp[...], vc[...], vn[...]], axis=0).astype(BF16)
        valid = _attn_valid(n, t)
        for g in range(grp):
            qg = q_ref[:, g * HEAD:(g + 1) * HEAD].astype(BF16)
            pr, _ = _attn_probs(qg, kb, bias_ref[g], sink_ref[0, x * grp + g], valid)
            y_ref[:, g * HEAD:(g + 1) * HEAD] = _dot(pr.astype(BF16), vb).astype(BF16)

    return pl.pallas_call(
        body, name="attn_fwd", grid=(KV_HEADS, nb),
        in_specs=[q_spec] + band + [bias_spec, sink_spec],
        out_specs=pl.BlockSpec((WINDOW, grp * HEAD), lambda x, n: (n, x)),
        out_shape=jax.ShapeDtypeStruct((t, n_attn * HEAD), BF16),
        compiler_params=_cparams("parallel", "parallel"),
    )(p, p, p, p, p, p, p, bias, sink)


def _attn_bwd(p, dcat, bias, sink, n_hgrn, n_attn):
    t = p.shape[0]
    grp, nb, q_spec, band, bias_spec, sink_spec = _attn_specs(t, n_hgrn, n_attn)
    inv = 1.0 / math.sqrt(HEAD)

    def body(q_ref, kp, kc, kn, vp, vc, vn, bias_ref, sink_ref, do_ref,
             dq_ref, dk_ref, dv_ref, dbias_ref, dsink_ref, dk_s, dv_s):
        x, n = pl.program_id(0), pl.program_id(1)

        @pl.when(n == 0)
        def _():
            dk_s[...] = jnp.zeros_like(dk_s)
            dv_s[...] = jnp.zeros_like(dv_s)
            dbias_ref[...] = jnp.zeros_like(dbias_ref)
            dsink_ref[...] = jnp.zeros_like(dsink_ref)

        kb = jnp.concatenate([kp[...], kc[...], kn[...]], axis=0).astype(BF16)
        vb = jnp.concatenate([vp[...], vc[...], vn[...]], axis=0).astype(BF16)
        valid = _attn_valid(n, t)
        dkb = jnp.zeros((KEY_SPAN, HEAD), F32)
        dvb = jnp.zeros((KEY_SPAN, HEAD), F32)
        for g in range(grp):
            qg = q_ref[:, g * HEAD:(g + 1) * HEAD].astype(BF16)
            dog = do_ref[:, g * HEAD:(g + 1) * HEAD].astype(BF16)
            pr, p_sink = _attn_probs(qg, kb, bias_ref[g], sink_ref[0, x * grp + g], valid)
            dpr = _dot_nt(dog, vb)
            delta = jnp.sum(pr * dpr, axis=-1, keepdims=True)
            ds = pr * (dpr - delta)
            ds_b = ds.astype(BF16)
            dq_ref[:, g * HEAD:(g + 1) * HEAD] = (_dot(ds_b, kb) * inv).astype(BF16)
            dkb = dkb + _dot_tn(ds_b, qg) * inv
            dvb = dvb + _dot_tn(pr.astype(BF16), dog)
            dbias_ref[g] += ds
            dsink_ref[g:g + 1, :] += jnp.broadcast_to(
                jnp.sum(-p_sink * delta, axis=0, keepdims=True), (1, WINDOW))
        rows = pl.ds(pl.multiple_of(n * WINDOW, WINDOW), KEY_SPAN)
        dk_s[rows, :] += dkb
        dv_s[rows, :] += dvb

        @pl.when(n == nb - 1)
        def _():
            dk_ref[...] = dk_s[pl.ds(WINDOW, t), :].astype(BF16)
            dv_ref[...] = dv_s[pl.ds(WINDOW, t), :].astype(BF16)

    do_spec = pl.BlockSpec((WINDOW, grp * HEAD), lambda x, n: (n, n_hgrn // grp + x))
    kv_out = pl.BlockSpec((t, HEAD), lambda x, n: (0, x))
    return pl.pallas_call(
        body, name="attn_bwd", grid=(KV_HEADS, nb),
        in_specs=[q_spec] + band + [bias_spec, sink_spec, do_spec],
        out_specs=[pl.BlockSpec((WINDOW, grp * HEAD), lambda x, n: (n, x)), kv_out, kv_out,
                   bias_spec, pl.BlockSpec((None, grp, WINDOW), lambda x, n: (x, 0, 0))],
        out_shape=[jax.ShapeDtypeStruct((t, n_attn * HEAD), BF16),
                   jax.ShapeDtypeStruct((t, KV_HEADS * HEAD), BF16),
                   jax.ShapeDtypeStruct((t, KV_HEADS * HEAD), BF16),
                   jax.ShapeDtypeStruct((n_attn, WINDOW, KEY_SPAN), F32),
                   jax.ShapeDtypeStruct((KV_HEADS, grp, WINDOW), F32)],
        scratch_shapes=[pltpu.VMEM((t + 2 * WINDOW, HEAD), F32), pltpu.VMEM((t + 2 * WINDOW, HEAD), F32)],
        compiler_params=_cparams("parallel", "arbitrary"),
    )(p, p, p, p, p, p, p, bias, sink, dcat)


def _position():
    return lax.axis_index("x"), lax.axis_index("y"), lax.axis_index("c")


def _handshake(peers):
    barrier = pltpu.get_barrier_semaphore()
    for peer in peers:
        pl.semaphore_signal(barrier, inc=1, device_id=peer, device_id_type=MESH)
    pl.semaphore_wait(barrier, len(peers))


def _sequencer(name, collective_id, scratch_types):
    return functools.partial(
        pl.kernel, mesh=plsc.ScalarSubcoreMesh(axis_name="sc", num_cores=1), name=name,
        scratch_types=scratch_types, compiler_params=pltpu.CompilerParams(collective_id=collective_id))


def _all_gather(name, shard, collective_id):
    rows = shard.shape[0]
    assert rows % 2 == 0
    rh = rows // 2
    src = jax.new_ref(shard, memory_space=pltpu.MemorySpace.HBM)
    out = jax.empty_ref(jax.ShapeDtypeStruct((N_DEV,) + shard.shape, shard.dtype),
                        memory_space=pltpu.MemorySpace.HBM)
    n_copies = 11

    @_sequencer(name, collective_id, (pltpu.SemaphoreType.DMA((n_copies,)), pltpu.SemaphoreType.DMA((n_copies,)),
                                      pltpu.SemaphoreType.DMA))
    def launch(send_sems, recv_sems, local_sem):
        x, y, c = _position()
        sibling = (x, y, 1 - c)
        xn, yn, dg = (1 - x, y), (x, 1 - y), (1 - x, 1 - y)
        _handshake([sibling, (*xn, c), (*yn, c)])

        def part(ref, half):
            return ref if half is None else ref.at[pl.ds(half * rh, rh)]

        def slot(chip, core, half=None):
            return part(out.at[4 * chip[0] + 2 * chip[1] + core], half)

        def copy(k, chip, core, half, to, own=False):
            return pltpu.make_async_remote_copy(
                src_ref=part(src, half) if own else slot(chip, core, half), dst_ref=slot(chip, core, half),
                send_sem=send_sems.at[k], recv_sem=recv_sems.at[k], device_id=to, device_id_type=MESH)

        def landed(k, chip, core, half):
            copy(k, chip, core, half, (x, y, c)).wait_recv()

        mine = pltpu.make_async_copy(src, slot((x, y), c), local_sem)
        mine.start()
        sent = [copy(0, (x, y), c, None, sibling, own=True),
                copy(1, (x, y), c, 0, (*xn, c), own=True), copy(3, (x, y), c, 1, (*yn, c), own=True),
                copy(2, (x, y), c, 1, (*xn, c), own=True), copy(4, (x, y), c, 0, (*yn, c), own=True)]
        for cp in sent:
            cp.start()

        def then(cp):
            cp.start()
            sent.append(cp)

        landed(1, xn, c, 0)
        then(copy(5, xn, c, 0, (*yn, c)))
        landed(3, yn, c, 1)
        then(copy(6, yn, c, 1, (*xn, c)))
        landed(2, xn, c, 1)
        then(copy(7, xn, c, None, sibling))
        landed(4, yn, c, 0)
        then(copy(8, yn, c, None, sibling))
        landed(5, dg, c, 0)
        then(copy(9, dg, c, 0, sibling))
        landed(6, dg, c, 1)
        then(copy(10, dg, c, 1, sibling))
        landed(0, (x, y), 1 - c, None)
        landed(7, xn, 1 - c, None)
        landed(8, yn, 1 - c, None)
        landed(9, dg, 1 - c, 0)
        landed(10, dg, 1 - c, 1)
        for cp in sent:
            cp.wait_send()
        mine.wait()

    launch()
    return out[...]


def _pair_exchange(name, stack):
    def body(s_ref, out_ref, send_sems, recv_sems):
        x, y, c = _position()
        sibling = (x, y, 1 - c)
        copies = [pltpu.make_async_remote_copy(
            src_ref=s_ref.at[2 * k + (1 - c)], dst_ref=out_ref.at[k], send_sem=send_sems.at[k],
            recv_sem=recv_sems.at[k], device_id=sibling, device_id_type=MESH) for k in range(4)]
        for cp in copies:
            cp.start()
        for cp in copies:
            cp.wait()

    return pl.pallas_call(
        body, name=name, in_specs=[ANY], out_specs=ANY,
        out_shape=jax.ShapeDtypeStruct((4,) + stack.shape[1:], stack.dtype),
        scratch_shapes=[pltpu.SemaphoreType.DMA((4,)), pltpu.SemaphoreType.DMA((4,))],
    )(stack)


def _pair_sum(name, stack, other, core):
    _, r, c = stack.shape
    tr = _tile(r, 256)

    def body(core_ref, a_ref, b_ref, o_ref):
        o_ref[...] = (a_ref[...].astype(F32) + b_ref[...].astype(F32)).astype(o_ref.dtype)

    grid_spec = pltpu.PrefetchScalarGridSpec(
        num_scalar_prefetch=1, grid=(4, r // tr),
        in_specs=[pl.BlockSpec((None, tr, c), lambda k, i, core_ref: (2 * k + core_ref[0], i, 0)),
                  pl.BlockSpec((None, tr, c), lambda k, i, core_ref: (k, i, 0))],
        out_specs=pl.BlockSpec((None, tr, c), lambda k, i, core_ref: (k, i, 0)))
    return pl.pallas_call(
        body, name=name, grid_spec=grid_spec, out_shape=jax.ShapeDtypeStruct((4, r, c), stack.dtype),
        compiler_params=_cparams("parallel", "parallel"),
    )(core, stack, other)


def _chip_exchange(name, sums, collective_id):
    src = jax.new_ref(sums, memory_space=pltpu.MemorySpace.HBM)
    out = jax.empty_ref(jax.ShapeDtypeStruct((3,) + sums.shape[1:], sums.dtype),
                        memory_space=pltpu.MemorySpace.HBM)

    @_sequencer(name, collective_id, (pltpu.SemaphoreType.DMA((3,)), pltpu.SemaphoreType.DMA((3,))))
    def launch(send_sems, recv_sems):
        x, y, c = _position()
        chips = [(1 - x, y), (x, 1 - y), (1 - x, 1 - y)]
        _handshake([(*chip, c) for chip in chips])
        copies = [pltpu.make_async_remote_copy(
            src_ref=src.at[2 * px + py], dst_ref=out.at[j], send_sem=send_sems.at[j],
            recv_sem=recv_sems.at[j], device_id=(px, py, c), device_id_type=MESH)
            for j, (px, py) in enumerate(chips)]
        for cp in copies:
            cp.start()
        for cp in copies:
            cp.wait()

    launch()
    return out[...]


def _small_all_reduce(name, part):
    r = part.shape[0]

    def body(x_ref, out_ref, gather, send_sems, recv_sems):
        x, y, c = _position()
        me = 4 * x + 2 * y + c
        gather[me] = x_ref[...]
        copies = []
        for k in range(1, N_DEV):
            peer = (x ^ (k >> 2), y ^ ((k >> 1) & 1), c ^ (k & 1))
            copies.append(pltpu.make_async_remote_copy(
                src_ref=x_ref, dst_ref=gather.at[me], send_sem=send_sems.at[k - 1],
                recv_sem=recv_sems.at[k - 1], device_id=peer, device_id_type=MESH))
        for cp in copies:
            cp.start()
        for k in range(1, N_DEV):
            peer_slot = 4 * (x ^ (k >> 2)) + 2 * (y ^ ((k >> 1) & 1)) + (c ^ (k & 1))
            pltpu.make_async_remote_copy(
                src_ref=x_ref, dst_ref=gather.at[peer_slot], send_sem=send_sems.at[k - 1],
                recv_sem=recv_sems.at[k - 1], device_id=(x, y, c), device_id_type=MESH).wait()
        acc = gather[0]
        for j in range(1, N_DEV):
            acc = acc + gather[j]
        out_ref[...] = acc

    vm = pl.BlockSpec(memory_space=pltpu.VMEM)
    return pl.pallas_call(
        body, name=name, in_specs=[vm], out_specs=vm, out_shape=jax.ShapeDtypeStruct((r, LANES), F32),
        scratch_shapes=[pltpu.VMEM((N_DEV, r, LANES), F32), pltpu.SemaphoreType.DMA((7,)),
                        pltpu.SemaphoreType.DMA((7,))],
    )(part)


def _adam_math(w, g, m, v):
    m = ADAM_B1 * m + (1.0 - ADAM_B1) * g
    v = ADAM_B2 * v + (1.0 - ADAM_B2) * jnp.square(g)
    m_hat = m / (1.0 - ADAM_B1 ** ADAM_STEP)
    v_hat = v / (1.0 - ADAM_B2 ** ADAM_STEP)
    delta = -ADAM_LR * (m_hat / (jnp.sqrt(v_hat) + ADAM_EPS) + ADAM_WD * w)
    return delta, m, v


def _adam_shard(name, w, m, v, sums, recv, chip, deps=()):
    r, c = w.shape
    tr = _tile(r, 128)

    def body(chip_ref, w_ref, m_ref, v_ref, own_ref, r0_ref, r1_ref, r2_ref, *rest):
        g_out, d_out, m_out, v_out = rest[-4:]
        g = ((own_ref[...].astype(F32) + r0_ref[...].astype(F32)) + r1_ref[...].astype(F32)) + r2_ref[...].astype(F32)
        delta, m_new, v_new = _adam_math(w_ref[...], g, m_ref[...], v_ref[...])
        g_out[...] = g
        d_out[...] = delta
        m_out[...] = m_new
        v_out[...] = v_new

    plain = pl.BlockSpec((tr, c), lambda i, chip_ref: (i, 0))
    piece = lambda j: pl.BlockSpec((None, tr, c), lambda i, chip_ref: (j, i, 0))
    grid_spec = pltpu.PrefetchScalarGridSpec(
        num_scalar_prefetch=1, grid=(r // tr,),
        in_specs=[plain, plain, plain,
                  pl.BlockSpec((None, tr, c), lambda i, chip_ref: (chip_ref[0], i, 0)),
                  piece(0), piece(1), piece(2)] + [ANY] * len(deps),
        out_specs=[plain] * 4)
    shape = jax.ShapeDtypeStruct((r, c), F32)
    return pl.pallas_call(
        body, name=name, grid_spec=grid_spec, out_shape=[shape] * 4, compiler_params=_cparams("parallel"),
    )(chip, w, m, v, sums, recv, recv, recv, *deps)


def _adam_small(name, w, g, m, v):
    r = w.shape[0]

    def body(w_ref, g_ref, m_ref, v_ref, d_out, m_out, v_out):
        delta, m_new, v_new = _adam_math(w_ref[...], g_ref[...], m_ref[...], v_ref[...])
        d_out[...] = delta
        m_out[...] = m_new
        v_out[...] = v_new

    vm = pl.BlockSpec(memory_space=pltpu.VMEM)
    shape = jax.ShapeDtypeStruct((r, LANES), F32)
    return pl.pallas_call(body, name=name, in_specs=[vm] * 4, out_specs=[vm] * 3, out_shape=[shape] * 3)(w, g, m, v)


def _reduce_scatter(tag, grad_stack, core, collective_id):
    other = _pair_exchange("rs_pair_" + tag, grad_stack)
    sums = _pair_sum("rs_sum_" + tag, grad_stack, other, core)
    return sums, _chip_exchange("rs_chip_" + tag, sums, collective_id)


def _pack(arrays):
    flat = jnp.concatenate([a.reshape(-1).astype(F32) for a in arrays])
    rows = -(-flat.shape[0] // LANES)
    rows = -(-rows // 8) * 8
    return jnp.pad(flat, (0, rows * LANES - flat.shape[0])).reshape(rows, LANES)


def _unpack(packed, like):
    flat = packed.reshape(-1)
    out, off = [], 0
    for a in like:
        out.append(flat[off:off + a.size].reshape(a.shape))
        off += a.size
    return out


SMALL = ("pre_norm_ffn1", "post_norm_ffn1", "pre_norm_mix", "post_norm_mix", "hgrn_lower_bounds_fwd",
         "hgrn_lower_bounds_bwd", "hgrn_out_norm", "attn_sink", "pre_norm_ffn2", "post_norm_ffn2", "rel_bias_table")
BIG = ("w_ffn1_gate_up", "w_ffn1_down", "w_mix_in", "w_mix_out", "w_ffn2_gate_up", "w_ffn2_down")
AG_ID = {n: 1 + i for i, n in enumerate(BIG)}
RS_ID = {n: 7 + i for i, n in enumerate(BIG)}
ORDER = ("pre_norm_ffn1", "post_norm_ffn1", "w_ffn1_gate_up", "w_ffn1_down", "pre_norm_mix", "post_norm_mix",
         "w_mix_in", "hgrn_lower_bounds_fwd", "hgrn_lower_bounds_bwd", "hgrn_out_norm", "attn_sink", "w_mix_out",
         "pre_norm_ffn2", "post_norm_ffn2", "w_ffn2_gate_up", "w_ffn2_down", "rel_bias_table")


def kernel(x, pre_norm_ffn1, post_norm_ffn1, w_ffn1_gate_up, w_ffn1_down, pre_norm_mix, post_norm_mix, w_mix_in, hgrn_lower_bounds_fwd, hgrn_lower_bounds_bwd, hgrn_out_norm, attn_sink, w_mix_out, pre_norm_ffn2, post_norm_ffn2, w_ffn2_gate_up, w_ffn2_down, rel_bias_table, loss_target, m_pre_norm_ffn1, m_post_norm_ffn1, m_w_ffn1_gate_up, m_w_ffn1_down, m_pre_norm_mix, m_post_norm_mix, m_w_mix_in, m_hgrn_lower_bounds_fwd, m_hgrn_lower_bounds_bwd, m_hgrn_out_norm, m_attn_sink, m_w_mix_out, m_pre_norm_ffn2, m_post_norm_ffn2, m_w_ffn2_gate_up, m_w_ffn2_down, m_rel_bias_table, v_pre_norm_ffn1, v_post_norm_ffn1, v_w_ffn1_gate_up, v_w_ffn1_down, v_pre_norm_mix, v_post_norm_mix, v_w_mix_in, v_hgrn_lower_bounds_fwd, v_hgrn_lower_bounds_bwd, v_hgrn_out_norm, v_attn_sink, v_w_mix_out, v_pre_norm_ffn2, v_post_norm_ffn2, v_w_ffn2_gate_up, v_w_ffn2_down, v_rel_bias_table):
    args = dict(locals())
    wts = {n: args[n] for n in ORDER}
    mom = {n: args["m_" + n] for n in ORDER}
    var = {n: args["v_" + n] for n in ORDER}

    x0 = x[0]
    target = loss_target[0]
    t, d = x0.shape
    n_hgrn = d // 2 // HEAD
    n_attn = (d - d // 2) // HEAD
    core = lax.axis_index("c").astype(jnp.int32).reshape(1)
    chip = (2 * lax.axis_index("x") + lax.axis_index("y")).astype(jnp.int32).reshape(1)

    full = {n: _all_gather("ag_" + n, wts[n][0].astype(BF16), AG_ID[n]) for n in BIG}
    w_gu1, w_gu2 = full["w_ffn1_gate_up"], full["w_ffn2_gate_up"]
    w_d1 = full["w_ffn1_down"].reshape(-1, d)
    w_d2 = full["w_ffn2_down"].reshape(-1, d)
    w_out = full["w_mix_out"].reshape(-1, d)
    w_in = full["w_mix_in"]

    g = {n: wts[n] for n in SMALL}
    lb_f = jax.nn.softmax(g["hgrn_lower_bounds_fwd"], axis=0)[0:1]
    lb_b = jax.nn.softmax(g["hgrn_lower_bounds_bwd"], axis=0)[0:1]
    bucket_idx = jnp.asarray(_t5_bucket_index())
    bias = _bias_build(g["rel_bias_table"], bucket_idx)

    n1 = _pre_norm("pre_norm1", x0, g["pre_norm_ffn1"])
    a1, gu1 = _ffn_up("ffn1_gate_up", n1, w_gu1)
    ff1 = _matmul("ffn1_down", a1, w_d1, mode="nn", out_dtype=F32)
    x1, h = _post_res_pre("res1", x0, ff1, g["post_norm_ffn1"], g["pre_norm_mix"], 0.5)
    p = _matmul("mix_in", h, w_in, mode="nn", stack=True, out_dtype=F32)
    y_h, o_raw = _hgrn_fwd(p, lb_f, lb_b, g["hgrn_out_norm"], n_hgrn)
    y_a = _attn_fwd(p, bias, g["attn_sink"], n_hgrn, n_attn)
    cat = jnp.concatenate([y_h, y_a], axis=1)
    mixed = _matmul("mix_out", cat, w_out, mode="nn", out_dtype=F32)
    x2, n2 = _post_res_pre("res2", x1, mixed, g["post_norm_mix"], g["pre_norm_ffn2"], 1.0)
    a2, gu2 = _ffn_up("ffn2_gate_up", n2, w_gu2)
    ff2 = _matmul("ffn2_down", a2, w_d2, mode="nn", out_dtype=F32)
    dy3, loss_part = _post_res_loss("res3_loss", x2, ff2, g["post_norm_ffn2"], target, 0.5)

    small_grad = {}
    scattered = {}

    def scatter(name, grad_stack):
        scattered[name] = _reduce_scatter(name, grad_stack, core, RS_ID[name])
        return [scattered[name][0]]

    def ffn_bwd(tag, dy, ff, a, gu, n_in, x_in, w_gu, w_d, post_name, pre_name, gu_name, d_name):
        dff, small_grad[post_name] = _post_bwd("post_bwd" + tag, dy, ff, g[post_name], 0.5)
        dep = scatter(d_name, _matmul("dw_down" + tag, a, dff, mode="tn", out_dtype=BF16).reshape(N_DEV, -1, d))
        dgu = _ffn_dact("d_act" + tag, dff, w_d, gu, deps=dep)
        dn = _matmul("d_norm" + tag, dgu, w_gu, mode="nt", stack=True, halves=True, out_dtype=F32)
        dep = scatter(gu_name, _matmul("dw_gate_up" + tag, n_in, dgu, mode="tn", stack=True, halves=True,
                                       out_dtype=BF16))
        dx, small_grad[pre_name] = _pre_bwd("pre_bwd" + tag, dn, x_in, g[pre_name], dy, deps=dep)
        return dx

    dx2 = ffn_bwd("2", dy3, ff2, a2, gu2, n2, x2, w_gu2, w_d2, "post_norm_ffn2", "pre_norm_ffn2",
                  "w_ffn2_gate_up", "w_ffn2_down")

    dmixed, small_grad["post_norm_mix"] = _post_bwd("post_bwd_mix", dx2, mixed, g["post_norm_mix"], 1.0)
    dcat = _matmul("d_cat", dmixed, w_out, mode="nt", out_dtype=F32)
    dep = scatter("w_mix_out", _matmul("dw_mix_out", cat, dmixed, mode="tn", out_dtype=BF16).reshape(N_DEV, -1, d))
    dq_h, di_h, dzf, dzb, dg_h, dlb_f, dlb_b, small_grad["hgrn_out_norm"] = _hgrn_bwd(
        p, o_raw, dcat, lb_f, lb_b, g["hgrn_out_norm"], n_hgrn)
    dq_a, dk_a, dv_a, dbias, dsink_rows = _attn_bwd(p, dcat, bias, g["attn_sink"], n_hgrn, n_attn)
    dp = jnp.concatenate([dq_h, di_h, dzf, dzb, dg_h, dq_a, dk_a, dv_a], axis=1)
    dh = _matmul("d_h", dp, w_in, mode="nt", stack=True, out_dtype=F32, deps=dep)
    dep = scatter("w_mix_in", _matmul("dw_mix_in", h, dp, mode="tn", stack=True, out_dtype=BF16))
    dx1, small_grad["pre_norm_mix"] = _pre_bwd("pre_bwd_mix", dh, x1, g["pre_norm_mix"], dx2, deps=dep)

    dx0 = ffn_bwd("1", dx1, ff1, a1, gu1, n1, x0, w_gu1, w_d1, "post_norm_ffn1", "pre_norm_ffn1",
                  "w_ffn1_gate_up", "w_ffn1_down")

    def lb_grad(dlb, lb):
        da0 = dlb * lb * (1.0 - lb)
        return jnp.concatenate([da0, -da0], axis=0)

    small_grad["hgrn_lower_bounds_fwd"] = lb_grad(dlb_f, lb_f)
    small_grad["hgrn_lower_bounds_bwd"] = lb_grad(dlb_b, lb_b)
    small_grad["attn_sink"] = dsink_rows[:, :, 0].reshape(1, n_attn)
    small_grad["rel_bias_table"] = jnp.transpose(_bias_reduce(dbias, bucket_idx)[:, :, 0])

    parts = [small_grad[n] for n in SMALL] + [loss_part[:, 0:1]]
    red = _small_all_reduce("small_all_reduce", _pack(parts))
    red_list = _unpack(red, parts)
    loss = red_list[-1].reshape(())
    sg = dict(zip(SMALL, red_list[:-1]))
    like = [wts[n] for n in SMALL]
    d_s, m_s, v_s = _adam_small("adam_small", _pack(like), _pack([sg[n] for n in SMALL]),
                                _pack([mom[n] for n in SMALL]), _pack([var[n] for n in SMALL]))
    grads = dict(sg)
    delta = dict(zip(SMALL, _unpack(d_s, like)))
    new_m = dict(zip(SMALL, _unpack(m_s, like)))
    new_v = dict(zip(SMALL, _unpack(v_s, like)))

    dep = []
    for n in ("w_ffn2_down", "w_ffn2_gate_up", "w_mix_out", "w_mix_in", "w_ffn1_down", "w_ffn1_gate_up"):
        sums, recv = scattered[n]
        gr, de, nm, nv = _adam_shard("adam_" + n, wts[n][0], mom[n][0], var[n][0], sums, recv, chip, deps=dep)
        grads[n], delta[n], new_m[n], new_v[n] = gr[None], de[None], nm[None], nv[None]
        dep = [gr]

    return (loss, dx0[None], *[grads[n] for n in ORDER], *[delta[n] for n in ORDER],
            *[new_m[n] for n in ORDER], *[new_v[n] for n in ORDER])
```

```python
import functools
import math

import numpy as np
import jax
import jax.numpy as jnp
from jax import lax
from jax.experimental import pallas as pl
from jax.experimental.pallas import tpu as pltpu
from jax.experimental.pallas import tpu_sc as plsc

F32 = jnp.float32
BF16 = jnp.bfloat16
HIGHEST = lax.Precision.HIGHEST
MESH = pl.DeviceIdType.MESH

N_DEV = 8
EPS = 1e-6
NEG_INF = -1e30
HEAD = 128
CHUNK = 64
WINDOW = 128
KEY_SPAN = 3 * WINDOW
KV_HEADS = 2
REL_BUCKETS = 32
REL_MAX_DIST = 128
ADAM_LR, ADAM_B1, ADAM_B2, ADAM_EPS, ADAM_WD, ADAM_STEP = 0.001, 0.9, 0.999, 1e-08, 0.01, 10
LANES = 128
VMEM_LIMIT = 56 * 1024 * 1024
ANY = pl.BlockSpec(memory_space=pl.ANY)


def _cparams(*sem):
    return pltpu.CompilerParams(dimension_semantics=sem if sem else None, vmem_limit_bytes=VMEM_LIMIT)


def _dot(a, b):
    return jnp.dot(a, b, preferred_element_type=F32)


def _dot_nt(a, b):
    return lax.dot_general(a, b, (((1,), (1,)), ((), ())), preferred_element_type=F32)


def _dot_tn(a, b):
    return lax.dot_general(a, b, (((0,), (0,)), ((), ())), preferred_element_type=F32)


def _tile(dim, target):
    for c in (target, 1024, 512, 256, 128):
        if c <= target and dim % c == 0:
            return c
    return dim


K_WHOLE = 2048
K_STEP = 2816


def _k_tile(kd):
    if kd <= K_WHOLE:
        return kd
    return max(c for c in range(LANES, K_STEP + 1, LANES) if kd % c == 0)


def _matmul(name, a, b, *, mode, out_dtype, stack=False, halves=False, tm=1024, tn=1024, deps=()):
    grp = 1
    if mode == "nn":
        m, kd = a.shape
        n = b.shape[0] * b.shape[2] if stack else b.shape[1]
    elif mode == "nt":
        m = a.shape[-2]
        n, kd = (b.shape[1], b.shape[0] * b.shape[2]) if stack else b.shape
    else:
        kd, m = a.shape
        n = b.shape[-1] * (2 if halves else 1)
    if stack:
        n1 = b.shape[2] if mode != "tn" else n // N_DEV
        if mode == "nt":
            grp = 2 if 2 * n1 <= K_STEP else 1
            tk = grp * n1
        else:
            grp = 1 if n1 % LANES == 0 else 2
            tn = grp * n1
        assert (grp * n1) % LANES == 0
    per_half = N_DEV // 2 // grp
    tm = _tile(m, tm)
    if not (stack and mode in ("nn", "tn")):
        tn = _tile(n, tn)
    if not (stack and mode == "nt"):
        tk = _k_tile(kd)
    nk = kd // tk
    lead = None if grp == 1 else grp
    b_outer = nk == 1 and b.size > a.size
    grid = (n // tn, m // tm, nk) if b_outer else (m // tm, n // tn, nk)

    def spec(shape, index):
        return pl.BlockSpec(shape, (lambda g0, g1, k: index(g1, g0, k)) if b_outer else index)

    if mode == "nn":
        a_spec = spec((tm, tk), lambda i, j, k: (i, k))
        if stack:
            b_spec = spec((lead, tk, n1), lambda i, j, k: (j, k, 0))
        else:
            b_spec = spec((tk, tn), lambda i, j, k: (k, j))
        dot = _dot
    elif mode == "nt":
        if halves:
            a_spec = spec((None, tm, tk), lambda i, j, k: (k // per_half, i, k % per_half))
        else:
            a_spec = spec((tm, tk), lambda i, j, k: (i, k))
        if stack:
            b_spec = spec((lead, tn, n1), lambda i, j, k: (k, j, 0))
        else:
            b_spec = spec((tn, tk), lambda i, j, k: (j, k))
        dot = _dot_nt
    else:
        a_spec = spec((tk, tm), lambda i, j, k: (k, i))
        if halves:
            b_spec = spec((None, tk, tn), lambda i, j, k: (j // per_half, k, j % per_half))
        else:
            b_spec = spec((tk, tn), lambda i, j, k: (k, j))
        dot = _dot_tn
    if stack and mode == "tn":
        out_shape = jax.ShapeDtypeStruct((N_DEV, m, n1), out_dtype)
        o_spec = spec((lead, tm, n1), lambda i, j, k: (j, i, 0))
    else:
        out_shape = jax.ShapeDtypeStruct((m, n), out_dtype)
        o_spec = spec((tm, tn), lambda i, j, k: (i, j))
    b_grouped = stack and grp > 1 and mode != "tn"
    o_grouped = stack and grp > 1 and mode == "tn"

    def product(a_ref, b_ref):
        bmat = jnp.concatenate([b_ref[s] for s in range(grp)], axis=1) if b_grouped else b_ref[...]
        return dot(a_ref[...], bmat)

    def store(o_ref, val):
        if o_grouped:
            for s in range(grp):
                o_ref[s] = val[:, s * n1:(s + 1) * n1].astype(o_ref.dtype)
        else:
            o_ref[...] = val.astype(o_ref.dtype)

    def body_whole(a_ref, b_ref, *rest):
        store(rest[-1], product(a_ref, b_ref))

    def body_steps(a_ref, b_ref, *rest):
        o_ref, acc_ref = rest[-2:]
        k = pl.program_id(2)

        @pl.when(k == 0)
        def _():
            acc_ref[...] = product(a_ref, b_ref)

        @pl.when(k > 0)
        def _():
            acc_ref[...] += product(a_ref, b_ref)

        @pl.when(k == nk - 1)
        def _():
            store(o_ref, acc_ref[...])

    return pl.pallas_call(
        body_whole if nk == 1 else body_steps, name=name, grid=grid,
        in_specs=[a_spec, b_spec] + [ANY] * len(deps), out_specs=o_spec, out_shape=out_shape,
        scratch_shapes=[] if nk == 1 else [pltpu.VMEM((tm, tn), F32)],
        compiler_params=_cparams("parallel", "parallel", "arbitrary"),
    )(a, b, *deps)


def _ffn_up(name, n, w_stack):
    t, d = n.shape
    s, _, n1 = w_stack.shape
    half = s // 2
    tm = _tile(t, 512)

    def body(n_ref, wg_ref, wu_ref, act_ref, gu_ref):
        nv = n_ref[...]
        gate = _dot(nv, wg_ref[...])
        up = _dot(nv, wu_ref[...])
        act_ref[...] = (gate * jax.nn.sigmoid(gate) * up).astype(BF16)
        gu_ref[0] = gate.astype(BF16)
        gu_ref[1] = up.astype(BF16)

    return pl.pallas_call(
        body, name=name, grid=(half, t // tm),
        in_specs=[pl.BlockSpec((tm, d), lambda j, i: (i, 0)),
                  pl.BlockSpec((None, d, n1), lambda j, i: (j, 0, 0)),
                  pl.BlockSpec((None, d, n1), lambda j, i: (half + j, 0, 0))],
        out_specs=[pl.BlockSpec((tm, n1), lambda j, i: (i, j)), pl.BlockSpec((2, tm, n1), lambda j, i: (0, i, j))],
        out_shape=[jax.ShapeDtypeStruct((t, half * n1), BF16), jax.ShapeDtypeStruct((2, t, half * n1), BF16)],
        compiler_params=_cparams("parallel", "parallel"),
    )(n, w_stack, w_stack)


def _ffn_dact(name, dff, w_d, gu, deps=()):
    t, d = dff.shape
    f = w_d.shape[0]
    tm = _tile(t, 512)
    tn = _tile(f, 1408)

    def body(dff_ref, w_ref, gu_ref, *rest):
        dgu_ref = rest[-1]
        da = _dot_nt(dff_ref[...], w_ref[...])
        gate = gu_ref[0].astype(F32)
        sg = jax.nn.sigmoid(gate)
        dgu_ref[0] = (da * gu_ref[1].astype(F32) * (sg * (1.0 + gate * (1.0 - sg)))).astype(BF16)
        dgu_ref[1] = (da * (gate * sg)).astype(BF16)

    pair = pl.BlockSpec((2, tm, tn), lambda j, i: (0, i, j))
    return pl.pallas_call(
        body, name=name, grid=(f // tn, t // tm),
        in_specs=[pl.BlockSpec((tm, d), lambda j, i: (i, 0)), pl.BlockSpec((tn, d), lambda j, i: (j, 0)), pair]
        + [ANY] * len(deps),
        out_specs=pair, out_shape=jax.ShapeDtypeStruct((2, t, f), BF16),
        compiler_params=_cparams("parallel", "parallel"),
    )(dff, w_d, gu, *deps)


ROWS = 256


def _rstd(xf):
    return lax.rsqrt(jnp.mean(xf * xf, axis=-1, keepdims=True) + EPS)


def _row_spec(t, d):
    return pl.BlockSpec((min(ROWS, t), d), lambda i: (i, 0))


def _vec_spec(d):
    return pl.BlockSpec((1, d), lambda i: (0, 0))


def _pre_norm(name, x, gain):
    t, d = x.shape

    def body(x_ref, g_ref, n_ref):
        xf = x_ref[...]
        n_ref[...] = (xf * _rstd(xf) * g_ref[...]).astype(BF16)

    return pl.pallas_call(
        body, name=name, grid=(t // min(ROWS, t),), in_specs=[_row_spec(t, d), _vec_spec(d)],
        out_specs=_row_spec(t, d), out_shape=jax.ShapeDtypeStruct((t, d), BF16),
        compiler_params=_cparams("parallel"),
    )(x, gain)


def _post_res_pre(name, x, ff, g_post, g_next, scale):
    t, d = x.shape

    def body(x_ref, ff_ref, gp_ref, gn_ref, xo_ref, n_ref):
        ff_ = ff_ref[...]
        xn = x_ref[...] + scale * (ff_ * _rstd(ff_) * gp_ref[...])
        xo_ref[...] = xn
        n_ref[...] = (xn * _rstd(xn) * gn_ref[...]).astype(BF16)

    return pl.pallas_call(
        body, name=name, grid=(t // min(ROWS, t),),
        in_specs=[_row_spec(t, d), _row_spec(t, d), _vec_spec(d), _vec_spec(d)],
        out_specs=[_row_spec(t, d), _row_spec(t, d)],
        out_shape=[jax.ShapeDtypeStruct((t, d), F32), jax.ShapeDtypeStruct((t, d), BF16)],
        compiler_params=_cparams("parallel"),
    )(x, ff, g_post, g_next)


def _post_res_loss(name, x, ff, g_post, target, scale):
    t, d = x.shape

    def body(x_ref, ff_ref, gp_ref, tg_ref, dy_ref, loss_ref):
        ff_ = ff_ref[...]
        err = x_ref[...] + scale * (ff_ * _rstd(ff_) * gp_ref[...]) - tg_ref[...]
        dy_ref[...] = err / d
        part = 0.5 * jnp.sum(jnp.mean(err * err, axis=-1, keepdims=True), axis=0, keepdims=True)

        @pl.when(pl.program_id(0) == 0)
        def _():
            loss_ref[...] = jnp.zeros_like(loss_ref)

        loss_ref[...] += jnp.broadcast_to(part, loss_ref.shape)

    return pl.pallas_call(
        body, name=name, grid=(t // min(ROWS, t),),
        in_specs=[_row_spec(t, d), _row_spec(t, d), _vec_spec(d), _row_spec(t, d)],
        out_specs=[_row_spec(t, d), _vec_spec(LANES)],
        out_shape=[jax.ShapeDtypeStruct((t, d), F32), jax.ShapeDtypeStruct((1, LANES), F32)],
        compiler_params=_cparams("arbitrary"),
    )(x, ff, g_post, target)


def _post_bwd(name, dy, ff, g_post, scale):
    t, d = dy.shape

    def body(dy_ref, ff_ref, gp_ref, dff_ref, dg_ref):
        ff_ = ff_ref[...]
        r = _rstd(ff_)
        xh = ff_ * r
        dyn = scale * dy_ref[...]
        dxh = dyn * gp_ref[...]
        dff_ref[...] = (r * (dxh - xh * jnp.mean(dxh * xh, axis=-1, keepdims=True))).astype(BF16)

        @pl.when(pl.program_id(0) == 0)
        def _():
            dg_ref[...] = jnp.zeros_like(dg_ref)

        dg_ref[...] += jnp.sum(dyn * xh, axis=0, keepdims=True)

    return pl.pallas_call(
        body, name=name, grid=(t // min(ROWS, t),),
        in_specs=[_row_spec(t, d), _row_spec(t, d), _vec_spec(d)],
        out_specs=[_row_spec(t, d), _vec_spec(d)],
        out_shape=[jax.ShapeDtypeStruct((t, d), BF16), jax.ShapeDtypeStruct((1, d), F32)],
        compiler_params=_cparams("arbitrary"),
    )(dy, ff, g_post)


def _pre_bwd(name, dn, x, g_pre, dy, deps=()):
    t, d = x.shape

    def body(dn_ref, x_ref, g_ref, dy_ref, *rest):
        dx_ref, dg_ref = rest[-2:]
        xf = x_ref[...]
        r = _rstd(xf)
        xh = xf * r
        dnf = dn_ref[...].astype(F32)
        dxh = dnf * g_ref[...]
        dx_ref[...] = dy_ref[...] + r * (dxh - xh * jnp.mean(dxh * xh, axis=-1, keepdims=True))

        @pl.when(pl.program_id(0) == 0)
        def _():
            dg_ref[...] = jnp.zeros_like(dg_ref)

        dg_ref[...] += jnp.sum(dnf * xh, axis=0, keepdims=True)

    return pl.pallas_call(
        body, name=name, grid=(t // min(ROWS, t),),
        in_specs=[_row_spec(t, d), _row_spec(t, d), _vec_spec(d), _row_spec(t, d)] + [ANY] * len(deps),
        out_specs=[_row_spec(t, d), _vec_spec(d)],
        out_shape=[jax.ShapeDtypeStruct((t, d), F32), jax.ShapeDtypeStruct((1, d), F32)],
        compiler_params=_cparams("arbitrary"),
    )(dn, x, g_pre, dy, *deps)


def _bdot(a, b, ca, cb, precision=None):
    return lax.dot_general(a, b, (((ca,), (cb,)), ((0,), (0,))), preferred_element_type=F32, precision=precision)


def _tri_masks(g):
    row = lax.broadcasted_iota(jnp.int32, (g, CHUNK, CHUNK), 1)
    col = lax.broadcasted_iota(jnp.int32, (g, CHUNK, CHUNK), 2)
    return col <= row, col >= row


def _hgrn_block(z, lb, q, v, cum_mat):
    sg = jax.nn.sigmoid(z)
    f = lb + (1.0 - lb) * sg
    lf = jnp.log(f)
    k = 1.0 - f
    a = _bdot(cum_mat, lf, 2, 1, HIGHEST)
    last = jnp.sum(lf, axis=1, keepdims=True)
    e_a = jnp.exp(a)
    e_na = jnp.exp(-a)
    e_t = jnp.exp(last - a)
    return dict(sg=sg, f=f, k=k, decay=jnp.exp(last), e_a=e_a, e_na=e_na, e_t=e_t,
                qd=q * e_a, kd=k * e_na, kt=k * e_t)


def _hgrn_states(state, kv, decay, order):
    entering = [None] * len(order)
    for g in order:
        entering[g] = state
        state = decay[g] * state + kv[g]
    return jnp.stack(entering, axis=0), state


def _hgrn_fwd(p, lb_f, lb_b, gain, n_heads):
    t = p.shape[0]
    w = n_heads * HEAD
    blk = min(8, t // CHUNK)
    rows_blk = blk * CHUNK
    n_blocks = t // rows_blk
    fin_rows = min(256, t)

    def body(q_ref, i_ref, zf_ref, zb_ref, g_ref, lbf_ref, lbb_ref, gain_ref, y_ref, o_ref, st_ref):
        low, up = _tri_masks(blk)
        m_low, m_up = low.astype(F32), up.astype(F32)
        o_ref[...] = jnp.zeros_like(o_ref)
        st_ref[...] = jnp.zeros_like(st_ref)

        def one(r0, z_ref, lb, slot, rev):
            rows = pl.ds(r0, rows_blk)
            split = lambda ref: ref[rows, :].reshape(blk, CHUNK, HEAD)
            q, v = split(q_ref), split(i_ref)
            c = _hgrn_block(split(z_ref), lb, q, v, m_up if rev else m_low)
            qd, kd, kt, vb = c["qd"].astype(BF16), c["kd"].astype(BF16), c["kt"].astype(BF16), v.astype(BF16)
            pm = jnp.where(up if rev else low, _bdot(qd, kd, 2, 2), 0.0).astype(BF16)
            kv = _bdot(vb, kt, 1, 1)
            order = range(blk - 1, -1, -1) if rev else range(blk)
            entering, st_ref[slot] = _hgrn_states(st_ref[slot], kv, c["decay"], order)
            o = _bdot(pm, vb, 2, 1) + _bdot(qd, entering.astype(BF16), 2, 2)
            o_ref[rows, :] += o.reshape(rows_blk, HEAD)

        def step(n, carry):
            one(pl.multiple_of(n * rows_blk, rows_blk), zf_ref, lbf_ref[...], 0, False)
            one(pl.multiple_of((n_blocks - 1 - n) * rows_blk, rows_blk), zb_ref, lbb_ref[...], 1, True)
            return carry

        lax.fori_loop(0, n_blocks, step, 0)

        def fin(n, carry):
            rows = pl.ds(pl.multiple_of(n * fin_rows, fin_rows), fin_rows)
            o = o_ref[rows, :]
            g = g_ref[rows, :]
            y_ref[rows, :] = (o * _rstd(o) * gain_ref[...] * (g * jax.nn.sigmoid(g))).astype(BF16)
            return carry

        lax.fori_loop(0, t // fin_rows, fin, 0)

    col = lambda grp: pl.BlockSpec((t, HEAD), lambda h: (0, grp * n_heads + h))
    vec = pl.BlockSpec((1, HEAD), lambda h: (0, h))
    out = pl.BlockSpec((t, HEAD), lambda h: (0, h))
    return pl.pallas_call(
        body, name="hgrn_fwd", grid=(n_heads,),
        in_specs=[col(0), col(1), col(2), col(3), col(4), vec, vec, vec],
        out_specs=[out, out],
        out_shape=[jax.ShapeDtypeStruct((t, w), BF16), jax.ShapeDtypeStruct((t, w), F32)],
        scratch_shapes=[pltpu.VMEM((2, HEAD, HEAD), F32)],
        compiler_params=_cparams("parallel"),
    )(p, p, p, p, p, lb_f, lb_b, gain)


def _hgrn_bwd(p, o_raw, dcat, lb_f, lb_b, gain, n_heads):
    t = p.shape[0]
    w = n_heads * HEAD
    n_chunks = t // CHUNK
    blk = min(8, n_chunks)
    rows_blk = blk * CHUNK
    n_blocks = t // rows_blk
    rb = min(256, t)

    def body(q_ref, i_ref, zf_ref, zb_ref, g_ref, o_ref, dy_ref, lbf_ref, lbb_ref, gain_ref,
             dq_ref, di_ref, dzf_ref, dzb_ref, dg_ref, dlbf_ref, dlbb_ref, dgain_ref,
             do_s, dq_s, dv_s, st_s, cur_s):
        low, up = _tri_masks(blk)
        m_low, m_up = low.astype(F32), up.astype(F32)
        rowid = lax.broadcasted_iota(jnp.int32, (blk, CHUNK, HEAD), 1)
        gain_v = gain_ref[...]

        def norm_bwd(n, dgain):
            rows = pl.ds(pl.multiple_of(n * rb, rb), rb)
            o = o_ref[rows, :]
            g = g_ref[rows, :]
            dy = dy_ref[rows, :]
            r = _rstd(o)
            oh = o * r
            sg = jax.nn.sigmoid(g)
            dg_ref[rows, :] = (dy * oh * gain_v * (sg * (1.0 + g * (1.0 - sg)))).astype(BF16)
            dno = dy * (g * sg)
            dxh = dno * gain_v
            do_s[rows, :] = r * (dxh - oh * jnp.mean(dxh * oh, axis=-1, keepdims=True))
            return dgain + jnp.sum(dno * oh, axis=0, keepdims=True)

        dgain_ref[...] = lax.fori_loop(0, t // rb, norm_bwd, jnp.zeros((1, HEAD), F32))
        dq_s[...] = jnp.zeros_like(dq_s)
        dv_s[...] = jnp.zeros_like(dv_s)

        def direction(z_ref, lb_ref, dz_ref, dlb_ref, rev):
            lb = lb_ref[...]
            cum_mat = m_up if rev else m_low
            cum_mat_t = m_low if rev else m_up
            mask = up if rev else low
            last_row = 0 if rev else CHUNK - 1

            order = range(blk - 1, -1, -1) if rev else range(blk)

            def rows_of(j):
                bidx = (n_blocks - 1 - j) if rev else j
                return bidx, pl.ds(pl.multiple_of(bidx * rows_blk, rows_blk), rows_blk)

            def load(rows):
                split = lambda ref: ref[rows, :].reshape(blk, CHUNK, HEAD)
                q, v = split(q_ref), split(i_ref)
                return q, v, _hgrn_block(split(z_ref), lb, q, v, cum_mat)

            def sweep_fwd(j, carry):
                bidx, rows = rows_of(j)
                _, v, c = load(rows)
                kv = _bdot(v.astype(BF16), c["kt"].astype(BF16), 1, 1)
                st_s[pl.ds(bidx * blk, blk)], cur_s[0] = _hgrn_states(cur_s[0], kv, c["decay"], order)
                return carry

            cur_s[...] = jnp.zeros_like(cur_s)
            dlb_ref[...] = jnp.zeros_like(dlb_ref)
            lax.fori_loop(0, n_blocks, sweep_fwd, 0)

            def sweep_bwd(jj, carry):
                bidx, rows = rows_of(n_blocks - 1 - jj)
                _, v, c = load(rows)
                st = st_s[pl.ds(bidx * blk, blk)]
                do = do_s[rows, :].reshape(blk, CHUNK, HEAD)
                qd, kd, kt, decay = c["qd"], c["kd"], c["kt"], c["decay"]
                qd_b, kd_b, kt_b = qd.astype(BF16), kd.astype(BF16), kt.astype(BF16)
                v_b, do_b, st_b = v.astype(BF16), do.astype(BF16), st.astype(BF16)
                pm = jnp.where(mask, _bdot(qd_b, kd_b, 2, 2), 0.0).astype(BF16)
                dpm = jnp.where(mask, _bdot(do_b, v_b, 2, 2), 0.0).astype(BF16)
                gq = _bdot(do_b, qd_b, 1, 1)
                dstate = cur_s[1]
                after = [None] * blk
                for g in reversed(order):
                    after[g] = dstate
                    dstate = gq[g] + decay[g] * dstate
                cur_s[1] = dstate
                dst = jnp.stack(after, axis=0)
                dst_b = dst.astype(BF16)
                dv = _bdot(pm, do_b, 1, 1) + _bdot(kt_b, dst_b, 2, 2)
                dqd = _bdot(dpm, kd_b, 2, 1) + _bdot(do_b, st_b, 2, 1)
                dkd = _bdot(dpm, qd_b, 1, 1)
                dkt = _bdot(v_b, dst_b, 2, 1)
                dlast = (jnp.sum(dkt * kt, axis=1, keepdims=True)
                         + decay * jnp.sum(dst * st, axis=1, keepdims=True))
                dq_s[rows, :] += (dqd * c["e_a"]).reshape(rows_blk, HEAD)
                dv_s[rows, :] += dv.reshape(rows_blk, HEAD)
                dk = dkd * c["e_na"] + dkt * c["e_t"]
                da = dqd * qd - dkd * kd - dkt * kt
                da = da + jnp.where(rowid == last_row, dlast, 0.0)
                dlf = _bdot(cum_mat_t, da, 2, 1, HIGHEST)
                df = dlf / c["f"] - dk
                sg = c["sg"]
                dz_ref[rows, :] = (df * (1.0 - lb) * (sg * (1.0 - sg))).reshape(rows_blk, HEAD).astype(BF16)
                dlb_ref[...] += jnp.sum((df * (1.0 - sg)).reshape(rows_blk, HEAD), axis=0, keepdims=True)
                return carry

            lax.fori_loop(0, n_blocks, sweep_bwd, 0)

        direction(zf_ref, lbf_ref, dzf_ref, dlbf_ref, False)
        direction(zb_ref, lbb_ref, dzb_ref, dlbb_ref, True)
        dq_ref[...] = dq_s[...].astype(BF16)
        di_ref[...] = dv_s[...].astype(BF16)

    col = lambda grp: pl.BlockSpec((t, HEAD), lambda h: (0, grp * n_heads + h))
    one = pl.BlockSpec((t, HEAD), lambda h: (0, h))
    vec = pl.BlockSpec((1, HEAD), lambda h: (0, h))
    big = jax.ShapeDtypeStruct((t, w), BF16)
    small = jax.ShapeDtypeStruct((1, w), F32)
    return pl.pallas_call(
        body, name="hgrn_bwd", grid=(n_heads,),
        in_specs=[col(0), col(1), col(2), col(3), col(4), one, one, vec, vec, vec],
        out_specs=[one] * 5 + [vec] * 3,
        out_shape=[big] * 5 + [small] * 3,
        scratch_shapes=[pltpu.VMEM((t, HEAD), F32), pltpu.VMEM((t, HEAD), F32), pltpu.VMEM((t, HEAD), F32),
                        pltpu.VMEM((n_chunks, HEAD, HEAD), F32), pltpu.VMEM((2, HEAD, HEAD), F32)],
        compiler_params=_cparams("parallel"),
    )(p, p, p, p, p, o_raw, dcat, lb_f, lb_b, gain)


def _t5_bucket_index():
    c = np.arange(WINDOW)[:, None]
    s = np.arange(KEY_SPAN)[None, :]
    rel = s - WINDOW - c
    nb = REL_BUCKETS // 2
    max_exact = nb // 2
    bucket = (rel > 0).astype(np.int32) * nb
    n = np.abs(rel)
    large = max_exact + (np.log(np.maximum(n, 1) / max_exact) / np.log(REL_MAX_DIST / max_exact)
                         * (nb - max_exact)).astype(np.int32)
    large = np.minimum(large, nb - 1)
    return bucket + np.where(n < max_exact, n, large).astype(np.int32)


def _bias_build(table, idx):
    n_attn = table.shape[1]

    def body(tab_ref, idx_ref, o_ref):
        h = pl.program_id(0)
        idx_v = idx_ref[...]
        acc = jnp.zeros((WINDOW, KEY_SPAN), F32)
        for b in range(REL_BUCKETS):
            acc = jnp.where(idx_v == b, tab_ref[b, h], acc)
        o_ref[...] = acc

    return pl.pallas_call(
        body, name="bias_build", grid=(n_attn,),
        in_specs=[pl.BlockSpec(memory_space=pltpu.SMEM), pl.BlockSpec((WINDOW, KEY_SPAN), lambda h: (0, 0))],
        out_specs=pl.BlockSpec((None, WINDOW, KEY_SPAN), lambda h: (h, 0, 0)),
        out_shape=jax.ShapeDtypeStruct((n_attn, WINDOW, KEY_SPAN), F32), compiler_params=_cparams("parallel"),
    )(table, idx)


def _bias_reduce(dbias, idx):
    n_attn = dbias.shape[0]

    def body(idx_ref, d_ref, o_ref):
        idx_v = idx_ref[...]
        dv = d_ref[...]
        rows = lax.broadcasted_iota(jnp.int32, (REL_BUCKETS, LANES), 0)
        acc = jnp.zeros((REL_BUCKETS, LANES), F32)
        for b in range(REL_BUCKETS):
            part = jnp.sum(jnp.where(idx_v == b, dv, 0.0), axis=1, keepdims=True)
            acc = jnp.where(rows == b, jnp.sum(part, axis=0, keepdims=True), acc)
        o_ref[...] = acc

    return pl.pallas_call(
        body, name="bias_reduce", grid=(n_attn,),
        in_specs=[pl.BlockSpec((WINDOW, KEY_SPAN), lambda h: (0, 0)),
                  pl.BlockSpec((None, WINDOW, KEY_SPAN), lambda h: (h, 0, 0))],
        out_specs=pl.BlockSpec((None, REL_BUCKETS, LANES), lambda h: (h, 0, 0)),
        out_shape=jax.ShapeDtypeStruct((n_attn, REL_BUCKETS, LANES), F32), compiler_params=_cparams("parallel"),
    )(idx, dbias)


def _attn_probs(qg, kb, bias, sink, valid):
    s = _dot_nt(qg, kb) / math.sqrt(HEAD) + bias
    s = jnp.where(valid, s, NEG_INF)
    m = jnp.maximum(jnp.max(s, axis=-1, keepdims=True), sink)
    e = jnp.exp(s - m)
    e_sink = jnp.exp(sink - m)
    den = jnp.sum(e, axis=-1, keepdims=True) + e_sink
    return e / den, e_sink / den


def _attn_valid(n, t):
    c = lax.broadcasted_iota(jnp.int32, (WINDOW, KEY_SPAN), 0)
    s = lax.broadcasted_iota(jnp.int32, (WINDOW, KEY_SPAN), 1)
    rel = s - WINDOW - c
    key_pos = n * WINDOW - WINDOW + s
    return (jnp.abs(rel) <= WINDOW) & (key_pos >= 0) & (key_pos < t)


def _attn_specs(t, n_hgrn, n_attn):
    grp = n_attn // KV_HEADS
    nb = t // WINDOW
    cq = 5 * n_hgrn
    ck = cq + n_attn
    cv = ck + KV_HEADS
    q_spec = pl.BlockSpec((WINDOW, grp * HEAD), lambda x, n: (n, cq // grp + x))
    kv = lambda base, off: pl.BlockSpec(
        (WINDOW, HEAD), lambda x, n: (jnp.clip(n + off, 0, nb - 1), base + x))
    band = [kv(ck, -1), kv(ck, 0), kv(ck, 1), kv(cv, -1), kv(cv, 0), kv(cv, 1)]
    bias_spec = pl.BlockSpec((grp, WINDOW, KEY_SPAN), lambda x, n: (x, 0, 0))
    sink_spec = pl.BlockSpec(memory_space=pltpu.SMEM)
    return grp, nb, q_spec, band, bias_spec, sink_spec


def _attn_fwd(p, bias, sink, n_hgrn, n_attn):
    t = p.shape[0]
    grp, nb, q_spec, band, bias_spec, sink_spec = _attn_specs(t, n_hgrn, n_attn)

    def body(q_ref, kp, kc, kn, vp, vc, vn, bias_ref, sink_ref, y_ref):
        x, n = pl.program_id(0), pl.program_id(1)
        kb = jnp.concatenate([kp[...], kc[...], kn[...]], axis=0).astype(BF16)
        vb = jnp.concatenate([vp[...], vc[...], vn[...]], axis=0).astype(BF16)
        valid = _attn_valid(n, t)
        for g in range(grp):
            qg = q_ref[:, g * HEAD:(g + 1) * HEAD].astype(BF16)
            pr, _ = _attn_probs(qg, kb, bias_ref[g], sink_ref[0, x * grp + g], valid)
            y_ref[:, g * HEAD:(g + 1) * HEAD] = _dot(pr.astype(BF16), vb).astype(BF16)

    return pl.pallas_call(
        body, name="attn_fwd", grid=(KV_HEADS, nb),
        in_specs=[q_spec] + band + [bias_spec, sink_spec],
        out_specs=pl.BlockSpec((WINDOW, grp * HEAD), lambda x, n: (n, x)),
        out_shape=jax.ShapeDtypeStruct((t, n_attn * HEAD), BF16),
        compiler_params=_cparams("parallel", "parallel"),
    )(p, p, p, p, p, p, p, bias, sink)


def _attn_bwd(p, dcat, bias, sink, n_hgrn, n_attn, deps=()):
    t = p.shape[0]
    grp, nb, q_spec, band, bias_spec, sink_spec = _attn_specs(t, n_hgrn, n_attn)
    inv = 1.0 / math.sqrt(HEAD)

    def body(q_ref, kp, kc, kn, vp, vc, vn, bias_ref, sink_ref, do_ref, *rest):
        dq_ref, dk_ref, dv_ref, dbias_ref, dsink_ref, dk_s, dv_s = rest[-7:]
        x, n = pl.program_id(0), pl.program_id(1)

        @pl.when(n == 0)
        def _():
            dk_s[...] = jnp.zeros_like(dk_s)
            dv_s[...] = jnp.zeros_like(dv_s)
            dbias_ref[...] = jnp.zeros_like(dbias_ref)
            dsink_ref[...] = jnp.zeros_like(dsink_ref)

        kb = jnp.concatenate([kp[...], kc[...], kn[...]], axis=0).astype(BF16)
        vb = jnp.concatenate([vp[...], vc[...], vn[...]], axis=0).astype(BF16)
        valid = _attn_valid(n, t)
        dkb = jnp.zeros((KEY_SPAN, HEAD), F32)
        dvb = jnp.zeros((KEY_SPAN, HEAD), F32)
        for g in range(grp):
            qg = q_ref[:, g * HEAD:(g + 1) * HEAD].astype(BF16)
            dog = do_ref[:, g * HEAD:(g + 1) * HEAD].astype(BF16)
            pr, p_sink = _attn_probs(qg, kb, bias_ref[g], sink_ref[0, x * grp + g], valid)
            dpr = _dot_nt(dog, vb)
            delta = jnp.sum(pr * dpr, axis=-1, keepdims=True)
            ds = pr * (dpr - delta)
            ds_b = ds.astype(BF16)
            dq_ref[:, g * HEAD:(g + 1) * HEAD] = (_dot(ds_b, kb) * inv).astype(BF16)
            dkb = dkb + _dot_tn(ds_b, qg) * inv
            dvb = dvb + _dot_tn(pr.astype(BF16), dog)
            dbias_ref[g] += ds
            dsink_ref[g:g + 1, :] += jnp.broadcast_to(
                jnp.sum(-p_sink * delta, axis=0, keepdims=True), (1, WINDOW))
        rows = pl.ds(pl.multiple_of(n * WINDOW, WINDOW), KEY_SPAN)
        dk_s[rows, :] += dkb
        dv_s[rows, :] += dvb

        @pl.when(n == nb - 1)
        def _():
            dk_ref[...] = dk_s[pl.ds(WINDOW, t), :].astype(BF16)
            dv_ref[...] = dv_s[pl.ds(WINDOW, t), :].astype(BF16)

    do_spec = pl.BlockSpec((WINDOW, grp * HEAD), lambda x, n: (n, n_hgrn // grp + x))
    kv_out = pl.BlockSpec((t, HEAD), lambda x, n: (0, x))
    return pl.pallas_call(
        body, name="attn_bwd", grid=(KV_HEADS, nb),
        in_specs=[q_spec] + band + [bias_spec, sink_spec, do_spec] + [ANY] * len(deps),
        out_specs=[pl.BlockSpec((WINDOW, grp * HEAD), lambda x, n: (n, x)), kv_out, kv_out,
                   bias_spec, pl.BlockSpec((None, grp, WINDOW), lambda x, n: (x, 0, 0))],
        out_shape=[jax.ShapeDtypeStruct((t, n_attn * HEAD), BF16),
                   jax.ShapeDtypeStruct((t, KV_HEADS * HEAD), BF16),
                   jax.ShapeDtypeStruct((t, KV_HEADS * HEAD), BF16),
                   jax.ShapeDtypeStruct((n_attn, WINDOW, KEY_SPAN), F32),
                   jax.ShapeDtypeStruct((KV_HEADS, grp, WINDOW), F32)],
        scratch_shapes=[pltpu.VMEM((t + 2 * WINDOW, HEAD), F32), pltpu.VMEM((t + 2 * WINDOW, HEAD), F32)],
        compiler_params=_cparams("parallel", "arbitrary"),
    )(p, p, p, p, p, p, p, bias, sink, dcat, *deps)


def _position():
    return lax.axis_index("x"), lax.axis_index("y"), lax.axis_index("c")


def _handshake(peers):
    barrier = pltpu.get_barrier_semaphore()
    for peer in peers:
        pl.semaphore_signal(barrier, inc=1, device_id=peer, device_id_type=MESH)
    pl.semaphore_wait(barrier, len(peers))


def _sequencer(name, collective_id, scratch_types):
    return functools.partial(
        pl.kernel, mesh=plsc.ScalarSubcoreMesh(axis_name="sc", num_cores=1), name=name,
        scratch_types=scratch_types, compiler_params=pltpu.CompilerParams(collective_id=collective_id))


def _all_gather(name, shard, collective_id):
    rows = shard.shape[0]
    assert rows % 2 == 0
    rh = rows // 2
    src = jax.new_ref(shard, memory_space=pltpu.MemorySpace.HBM)
    out = jax.empty_ref(jax.ShapeDtypeStruct((N_DEV,) + shard.shape, shard.dtype),
                        memory_space=pltpu.MemorySpace.HBM)
    n_copies = 11

    @_sequencer(name, collective_id, (pltpu.SemaphoreType.DMA((n_copies,)), pltpu.SemaphoreType.DMA((n_copies,)),
                                      pltpu.SemaphoreType.DMA))
    def launch(send_sems, recv_sems, local_sem):
        x, y, c = _position()
        sibling = (x, y, 1 - c)
        xn, yn, dg = (1 - x, y), (x, 1 - y), (1 - x, 1 - y)
        _handshake([sibling, (*xn, c), (*yn, c)])

        def part(ref, half):
            return ref if half is None else ref.at[pl.ds(half * rh, rh)]

        def slot(chip, core, half=None):
            return part(out.at[4 * chip[0] + 2 * chip[1] + core], half)

        def copy(k, chip, core, half, to, own=False):
            return pltpu.make_async_remote_copy(
                src_ref=part(src, half) if own else slot(chip, core, half), dst_ref=slot(chip, core, half),
                send_sem=send_sems.at[k], recv_sem=recv_sems.at[k], device_id=to, device_id_type=MESH)

        def landed(k, chip, core, half):
            copy(k, chip, core, half, (x, y, c)).wait_recv()

        mine = pltpu.make_async_copy(src, slot((x, y), c), local_sem)
        mine.start()
        sent = [copy(0, (x, y), c, None, sibling, own=True),
                copy(1, (x, y), c, 0, (*xn, c), own=True), copy(3, (x, y), c, 1, (*yn, c), own=True),
                copy(2, (x, y), c, 1, (*xn, c), own=True), copy(4, (x, y), c, 0, (*yn, c), own=True)]
        for cp in sent:
            cp.start()

        def then(cp):
            cp.start()
            sent.append(cp)

        landed(1, xn, c, 0)
        then(copy(5, xn, c, 0, (*yn, c)))
        landed(3, yn, c, 1)
        then(copy(6, yn, c, 1, (*xn, c)))
        landed(2, xn, c, 1)
        then(copy(7, xn, c, None, sibling))
        landed(4, yn, c, 0)
        then(copy(8, yn, c, None, sibling))
        landed(5, dg, c, 0)
        then(copy(9, dg, c, 0, sibling))
        landed(6, dg, c, 1)
        then(copy(10, dg, c, 1, sibling))
        landed(0, (x, y), 1 - c, None)
        landed(7, xn, 1 - c, None)
        landed(8, yn, 1 - c, None)
        landed(9, dg, 1 - c, 0)
        landed(10, dg, 1 - c, 1)
        for cp in sent:
            cp.wait_send()
        mine.wait()

    launch()
    return out[...]


HBM = pl.BlockSpec(memory_space=pltpu.HBM)
SEM = pl.BlockSpec(memory_space=pltpu.SEMAPHORE)
EFFECT = pltpu.SideEffectType.DATAFLOW_SIDE_EFFECTING


def _pair_copies(s_ref, land_ref, send_sems, recv_sems):
    x, y, c = _position()
    return [pltpu.make_async_remote_copy(
        src_ref=s_ref.at[2 * k + (1 - c)], dst_ref=land_ref.at[k], send_sem=send_sems.at[k],
        recv_sem=recv_sems.at[k], device_id=(x, y, 1 - c), device_id_type=MESH) for k in range(4)]


def _pair_start(name, stack):
    land_shape = (4,) + stack.shape[1:]

    def body(s_ref, land_ref, send_sems, recv_sems, s_thru, land_thru, token):
        for cp in _pair_copies(s_ref, land_ref, send_sems, recv_sems):
            cp.start()
        token[...] = jnp.zeros_like(token)

    return pl.pallas_call(
        body, name=name,
        out_shape=(pltpu.SemaphoreType.DMA((4,)), pltpu.SemaphoreType.DMA((4,)),
                   pltpu.HBM(stack.shape, stack.dtype), pltpu.HBM(land_shape, stack.dtype),
                   jax.ShapeDtypeStruct((8, LANES), F32)),
        in_specs=(HBM, HBM), out_specs=(SEM, SEM, HBM, HBM, pl.BlockSpec(memory_space=pltpu.VMEM)),
        input_output_aliases={0: 2, 1: 3}, compiler_params=pltpu.CompilerParams(has_side_effects=EFFECT),
    )(pltpu.with_memory_space_constraint(stack, pltpu.HBM),
      pltpu.with_memory_space_constraint(lax.empty(land_shape, stack.dtype), pltpu.HBM))


def _pair_wait(name, started, after):
    send_sems, recv_sems, s_thru, land_thru, _ = started

    def body(s_ref, land_ref, send_sems, recv_sems, after_ref, s_out, land_out):
        for cp in _pair_copies(s_ref, land_ref, send_sems, recv_sems):
            cp.wait_send()
            cp.wait_recv()

    return pl.pallas_call(
        body, name=name,
        out_shape=(pltpu.HBM(s_thru.shape, s_thru.dtype), pltpu.HBM(land_thru.shape, land_thru.dtype)),
        in_specs=(HBM, HBM, SEM, SEM, ANY), out_specs=(HBM, HBM), input_output_aliases={0: 0, 1: 1},
        compiler_params=pltpu.CompilerParams(has_side_effects=EFFECT),
    )(s_thru, land_thru, send_sems, recv_sems, after)


def _pair_sum(name, stack, other, core):
    _, r, c = stack.shape
    tr = _tile(r, 256)

    def body(core_ref, a_ref, b_ref, o_ref):
        o_ref[...] = (a_ref[...].astype(F32) + b_ref[...].astype(F32)).astype(o_ref.dtype)

    grid_spec = pltpu.PrefetchScalarGridSpec(
        num_scalar_prefetch=1, grid=(4, r // tr),
        in_specs=[pl.BlockSpec((None, tr, c), lambda k, i, core_ref: (2 * k + core_ref[0], i, 0)),
                  pl.BlockSpec((None, tr, c), lambda k, i, core_ref: (k, i, 0))],
        out_specs=pl.BlockSpec((None, tr, c), lambda k, i, core_ref: (k, i, 0)))
    return pl.pallas_call(
        body, name=name, grid_spec=grid_spec, out_shape=jax.ShapeDtypeStruct((4, r, c), stack.dtype),
        compiler_params=_cparams("parallel", "parallel"),
    )(core, stack, other)


def _chip_exchange(name, sums, collective_id):
    src = jax.new_ref(sums, memory_space=pltpu.MemorySpace.HBM)
    out = jax.empty_ref(jax.ShapeDtypeStruct((3,) + sums.shape[1:], sums.dtype),
                        memory_space=pltpu.MemorySpace.HBM)

    @_sequencer(name, collective_id, (pltpu.SemaphoreType.DMA((3,)), pltpu.SemaphoreType.DMA((3,))))
    def launch(send_sems, recv_sems):
        x, y, c = _position()
        chips = [(1 - x, y), (x, 1 - y), (1 - x, 1 - y)]
        _handshake([(*chip, c) for chip in chips])
        copies = [pltpu.make_async_remote_copy(
            src_ref=src.at[2 * px + py], dst_ref=out.at[j], send_sem=send_sems.at[j],
            recv_sem=recv_sems.at[j], device_id=(px, py, c), device_id_type=MESH)
            for j, (px, py) in enumerate(chips)]
        for cp in copies:
            cp.start()
        for cp in copies:
            cp.wait()

    launch()
    return out[...]


def _small_all_reduce(name, part):
    r = part.shape[0]

    def body(x_ref, out_ref, gather, send_sems, recv_sems):
        x, y, c = _position()
        me = 4 * x + 2 * y + c
        gather[me] = x_ref[...]
        copies = []
        for k in range(1, N_DEV):
            peer = (x ^ (k >> 2), y ^ ((k >> 1) & 1), c ^ (k & 1))
            copies.append(pltpu.make_async_remote_copy(
                src_ref=x_ref, dst_ref=gather.at[me], send_sem=send_sems.at[k - 1],
                recv_sem=recv_sems.at[k - 1], device_id=peer, device_id_type=MESH))
        for cp in copies:
            cp.start()
        for k in range(1, N_DEV):
            peer_slot = 4 * (x ^ (k >> 2)) + 2 * (y ^ ((k >> 1) & 1)) + (c ^ (k & 1))
            pltpu.make_async_remote_copy(
                src_ref=x_ref, dst_ref=gather.at[peer_slot], send_sem=send_sems.at[k - 1],
                recv_sem=recv_sems.at[k - 1], device_id=(x, y, c), device_id_type=MESH).wait()
        acc = gather[0]
        for j in range(1, N_DEV):
            acc = acc + gather[j]
        out_ref[...] = acc

    vm = pl.BlockSpec(memory_space=pltpu.VMEM)
    return pl.pallas_call(
        body, name=name, in_specs=[vm], out_specs=vm, out_shape=jax.ShapeDtypeStruct((r, LANES), F32),
        scratch_shapes=[pltpu.VMEM((N_DEV, r, LANES), F32), pltpu.SemaphoreType.DMA((7,)),
                        pltpu.SemaphoreType.DMA((7,))],
    )(part)


def _adam_math(w, g, m, v):
    m = ADAM_B1 * m + (1.0 - ADAM_B1) * g
    v = ADAM_B2 * v + (1.0 - ADAM_B2) * jnp.square(g)
    m_hat = m / (1.0 - ADAM_B1 ** ADAM_STEP)
    v_hat = v / (1.0 - ADAM_B2 ** ADAM_STEP)
    delta = -ADAM_LR * (m_hat / (jnp.sqrt(v_hat) + ADAM_EPS) + ADAM_WD * w)
    return delta, m, v


def _adam_shard(name, w, m, v, sums, recv, chip, deps=()):
    r, c = w.shape
    tr = _tile(r, 128)

    def body(chip_ref, w_ref, m_ref, v_ref, own_ref, r0_ref, r1_ref, r2_ref, *rest):
        g_out, d_out, m_out, v_out = rest[-4:]
        g = ((own_ref[...].astype(F32) + r0_ref[...].astype(F32)) + r1_ref[...].astype(F32)) + r2_ref[...].astype(F32)
        delta, m_new, v_new = _adam_math(w_ref[...], g, m_ref[...], v_ref[...])
        g_out[...] = g
        d_out[...] = delta
        m_out[...] = m_new
        v_out[...] = v_new

    plain = pl.BlockSpec((tr, c), lambda i, chip_ref: (i, 0))
    piece = lambda j: pl.BlockSpec((None, tr, c), lambda i, chip_ref: (j, i, 0))
    grid_spec = pltpu.PrefetchScalarGridSpec(
        num_scalar_prefetch=1, grid=(r // tr,),
        in_specs=[plain, plain, plain,
                  pl.BlockSpec((None, tr, c), lambda i, chip_ref: (chip_ref[0], i, 0)),
                  piece(0), piece(1), piece(2)] + [ANY] * len(deps),
        out_specs=[plain] * 4)
    shape = jax.ShapeDtypeStruct((r, c), F32)
    return pl.pallas_call(
        body, name=name, grid_spec=grid_spec, out_shape=[shape] * 4, compiler_params=_cparams("parallel"),
    )(chip, w, m, v, sums, recv, recv, recv, *deps)


def _adam_small(name, w, g, m, v):
    r = w.shape[0]

    def body(w_ref, g_ref, m_ref, v_ref, d_out, m_out, v_out):
        delta, m_new, v_new = _adam_math(w_ref[...], g_ref[...], m_ref[...], v_ref[...])
        d_out[...] = delta
        m_out[...] = m_new
        v_out[...] = v_new

    vm = pl.BlockSpec(memory_space=pltpu.VMEM)
    shape = jax.ShapeDtypeStruct((r, LANES), F32)
    return pl.pallas_call(body, name=name, in_specs=[vm] * 4, out_specs=[vm] * 3, out_shape=[shape] * 3)(w, g, m, v)


def _reduce_scatter(tag, started, after, core, collective_id):
    grad_stack, other = _pair_wait("rs_pair_wait_" + tag, started, after)
    sums = _pair_sum("rs_sum_" + tag, grad_stack, other, core)
    return sums, _chip_exchange("rs_chip_" + tag, sums, collective_id)


def _pack(arrays):
    flat = jnp.concatenate([a.reshape(-1).astype(F32) for a in arrays])
    rows = -(-flat.shape[0] // LANES)
    rows = -(-rows // 8) * 8
    return jnp.pad(flat, (0, rows * LANES - flat.shape[0])).reshape(rows, LANES)


def _unpack(packed, like):
    flat = packed.reshape(-1)
    out, off = [], 0
    for a in like:
        out.append(flat[off:off + a.size].reshape(a.shape))
        off += a.size
    return out


SMALL = ("pre_norm_ffn1", "post_norm_ffn1", "pre_norm_mix", "post_norm_mix", "hgrn_lower_bounds_fwd",
         "hgrn_lower_bounds_bwd", "hgrn_out_norm", "attn_sink", "pre_norm_ffn2", "post_norm_ffn2", "rel_bias_table")
BIG = ("w_ffn1_gate_up", "w_ffn1_down", "w_mix_in", "w_mix_out", "w_ffn2_gate_up", "w_ffn2_down")
AG_ID = {n: 1 + i for i, n in enumerate(BIG)}
RS_ID = {n: 7 + i for i, n in enumerate(BIG)}
ORDER = ("pre_norm_ffn1", "post_norm_ffn1", "w_ffn1_gate_up", "w_ffn1_down", "pre_norm_mix", "post_norm_mix",
         "w_mix_in", "hgrn_lower_bounds_fwd", "hgrn_lower_bounds_bwd", "hgrn_out_norm", "attn_sink", "w_mix_out",
         "pre_norm_ffn2", "post_norm_ffn2", "w_ffn2_gate_up", "w_ffn2_down", "rel_bias_table")


def kernel(x, pre_norm_ffn1, post_norm_ffn1, w_ffn1_gate_up, w_ffn1_down, pre_norm_mix, post_norm_mix, w_mix_in, hgrn_lower_bounds_fwd, hgrn_lower_bounds_bwd, hgrn_out_norm, attn_sink, w_mix_out, pre_norm_ffn2, post_norm_ffn2, w_ffn2_gate_up, w_ffn2_down, rel_bias_table, loss_target, m_pre_norm_ffn1, m_post_norm_ffn1, m_w_ffn1_gate_up, m_w_ffn1_down, m_pre_norm_mix, m_post_norm_mix, m_w_mix_in, m_hgrn_lower_bounds_fwd, m_hgrn_lower_bounds_bwd, m_hgrn_out_norm, m_attn_sink, m_w_mix_out, m_pre_norm_ffn2, m_post_norm_ffn2, m_w_ffn2_gate_up, m_w_ffn2_down, m_rel_bias_table, v_pre_norm_ffn1, v_post_norm_ffn1, v_w_ffn1_gate_up, v_w_ffn1_down, v_pre_norm_mix, v_post_norm_mix, v_w_mix_in, v_hgrn_lower_bounds_fwd, v_hgrn_lower_bounds_bwd, v_hgrn_out_norm, v_attn_sink, v_w_mix_out, v_pre_norm_ffn2, v_post_norm_ffn2, v_w_ffn2_gate_up, v_w_ffn2_down, v_rel_bias_table):
    args = dict(locals())
    wts = {n: args[n] for n in ORDER}
    mom = {n: args["m_" + n] for n in ORDER}
    var = {n: args["v_" + n] for n in ORDER}

    x0 = x[0]
    target = loss_target[0]
    t, d = x0.shape
    n_hgrn = d // 2 // HEAD
    n_attn = (d - d // 2) // HEAD
    core = lax.axis_index("c").astype(jnp.int32).reshape(1)
    chip = (2 * lax.axis_index("x") + lax.axis_index("y")).astype(jnp.int32).reshape(1)

    full = {n: _all_gather("ag_" + n, wts[n][0].astype(BF16), AG_ID[n]) for n in BIG}
    w_gu1, w_gu2 = full["w_ffn1_gate_up"], full["w_ffn2_gate_up"]
    w_d1 = full["w_ffn1_down"].reshape(-1, d)
    w_d2 = full["w_ffn2_down"].reshape(-1, d)
    w_out = full["w_mix_out"].reshape(-1, d)
    w_in = full["w_mix_in"]

    g = {n: wts[n] for n in SMALL}
    lb_f = jax.nn.softmax(g["hgrn_lower_bounds_fwd"], axis=0)[0:1]
    lb_b = jax.nn.softmax(g["hgrn_lower_bounds_bwd"], axis=0)[0:1]
    bucket_idx = jnp.asarray(_t5_bucket_index())
    bias = _bias_build(g["rel_bias_table"], bucket_idx)

    n1 = _pre_norm("pre_norm1", x0, g["pre_norm_ffn1"])
    a1, gu1 = _ffn_up("ffn1_gate_up", n1, w_gu1)
    ff1 = _matmul("ffn1_down", a1, w_d1, mode="nn", out_dtype=F32)
    x1, h = _post_res_pre("res1", x0, ff1, g["post_norm_ffn1"], g["pre_norm_mix"], 0.5)
    p = _matmul("mix_in", h, w_in, mode="nn", stack=True, out_dtype=F32)
    y_h, o_raw = _hgrn_fwd(p, lb_f, lb_b, g["hgrn_out_norm"], n_hgrn)
    y_a = _attn_fwd(p, bias, g["attn_sink"], n_hgrn, n_attn)
    cat = jnp.concatenate([y_h, y_a], axis=1)
    mixed = _matmul("mix_out", cat, w_out, mode="nn", out_dtype=F32)
    x2, n2 = _post_res_pre("res2", x1, mixed, g["post_norm_mix"], g["pre_norm_ffn2"], 1.0)
    a2, gu2 = _ffn_up("ffn2_gate_up", n2, w_gu2)
    ff2 = _matmul("ffn2_down", a2, w_d2, mode="nn", out_dtype=F32)
    dy3, loss_part = _post_res_loss("res3_loss", x2, ff2, g["post_norm_ffn2"], target, 0.5)

    small_grad = {}
    scattered = {}

    pending = []

    def scatter(name, grad_stack):
        started = _pair_start("rs_pair_" + name, grad_stack)
        pending.append((name, started))
        return [started[4]]

    def settle(after):
        deps = []
        while pending:
            name, started = pending.pop(0)
            scattered[name] = _reduce_scatter(name, started, after, core, RS_ID[name])
            deps.append(scattered[name][0])
        return deps

    def ffn_bwd(tag, dy, ff, a, gu, n_in, x_in, w_gu, w_d, post_name, pre_name, gu_name, d_name):
        dff, small_grad[post_name] = _post_bwd("post_bwd" + tag, dy, ff, g[post_name], 0.5)
        dep = settle(dff)
        dep = scatter(d_name, _matmul("dw_down" + tag, a, dff, mode="tn", out_dtype=BF16,
                                      deps=dep).reshape(N_DEV, -1, d))
        dgu = _ffn_dact("d_act" + tag, dff, w_d, gu, deps=dep)
        dep = settle(dgu)
        dn = _matmul("d_norm" + tag, dgu, w_gu, mode="nt", stack=True, halves=True, out_dtype=F32, deps=dep)
        dep = scatter(gu_name, _matmul("dw_gate_up" + tag, n_in, dgu, mode="tn", stack=True, halves=True,
                                       out_dtype=BF16))
        dx, small_grad[pre_name] = _pre_bwd("pre_bwd" + tag, dn, x_in, g[pre_name], dy, deps=dep)
        return dx

    dx2 = ffn_bwd("2", dy3, ff2, a2, gu2, n2, x2, w_gu2, w_d2, "post_norm_ffn2", "pre_norm_ffn2",
                  "w_ffn2_gate_up", "w_ffn2_down")

    dmixed, small_grad["post_norm_mix"] = _post_bwd("post_bwd_mix", dx2, mixed, g["post_norm_mix"], 1.0)
    dep = settle(dmixed)
    dcat = _matmul("d_cat", dmixed, w_out, mode="nt", out_dtype=F32, deps=dep)
    dep = scatter("w_mix_out", _matmul("dw_mix_out", cat, dmixed, mode="tn", out_dtype=BF16).reshape(N_DEV, -1, d))
    dq_a, dk_a, dv_a, dbias, dsink_rows = _attn_bwd(p, dcat, bias, g["attn_sink"], n_hgrn, n_attn, deps=dep)
    dq_h, di_h, dzf, dzb, dg_h, dlb_f, dlb_b, small_grad["hgrn_out_norm"] = _hgrn_bwd(
        p, o_raw, dcat, lb_f, lb_b, g["hgrn_out_norm"], n_hgrn)
    dp = jnp.concatenate([dq_h, di_h, dzf, dzb, dg_h, dq_a, dk_a, dv_a], axis=1)
    dep = settle(dp)
    dh = _matmul("d_h", dp, w_in, mode="nt", stack=True, out_dtype=F32, deps=dep)
    dep = scatter("w_mix_in", _matmul("dw_mix_in", h, dp, mode="tn", stack=True, out_dtype=BF16))
    dx1, small_grad["pre_norm_mix"] = _pre_bwd("pre_bwd_mix", dh, x1, g["pre_norm_mix"], dx2, deps=dep)

    dx0 = ffn_bwd("1", dx1, ff1, a1, gu1, n1, x0, w_gu1, w_d1, "post_norm_ffn1", "pre_norm_ffn1",
                  "w_ffn1_gate_up", "w_ffn1_down")
    settle(dx0)

    def lb_grad(dlb, lb):
        da0 = dlb * lb * (1.0 - lb)
        return jnp.concatenate([da0, -da0], axis=0)

    small_grad["hgrn_lower_bounds_fwd"] = lb_grad(dlb_f, lb_f)
    small_grad["hgrn_lower_bounds_bwd"] = lb_grad(dlb_b, lb_b)
    small_grad["attn_sink"] = dsink_rows[:, :, 0].reshape(1, n_attn)
    small_grad["rel_bias_table"] = jnp.transpose(_bias_reduce(dbias, bucket_idx)[:, :, 0])

    parts = [small_grad[n] for n in SMALL] + [loss_part[:, 0:1]]
    red = _small_all_reduce("small_all_reduce", _pack(parts))
    red_list = _unpack(red, parts)
    loss = red_list[-1].reshape(())
    sg = dict(zip(SMALL, red_list[:-1]))
    like = [wts[n] for n in SMALL]
    d_s, m_s, v_s = _adam_small("adam_small", _pack(like), _pack([sg[n] for n in SMALL]),
                                _pack([mom[n] for n in SMALL]), _pack([var[n] for n in SMALL]))
    grads = dict(sg)
    delta = dict(zip(SMALL, _unpack(d_s, like)))
    new_m = dict(zip(SMALL, _unpack(m_s, like)))
    new_v = dict(zip(SMALL, _unpack(v_s, like)))

    dep = []
    for n in ("w_ffn2_down", "w_ffn2_gate_up", "w_mix_out", "w_mix_in", "w_ffn1_down", "w_ffn1_gate_up"):
        sums, recv = scattered[n]
        gr, de, nm, nv = _adam_shard("adam_" + n, wts[n][0], mom[n][0], var[n][0], sums, recv, chip, deps=dep)
        grads[n], delta[n], new_m[n], new_v[n] = gr[None], de[None], nm[None], nv[None]
        dep = [gr]

    return (loss, dx0[None], *[grads[n] for n in ORDER], *[delta[n] for n in ORDER],
            *[new_m[n] for n in ORDER], *[new_v[n] for n in ORDER])
```

```python
import functools
import math

import numpy as np
import jax
import jax.numpy as jnp
from jax import lax
from jax.experimental import pallas as pl
from jax.experimental.pallas import tpu as pltpu
from jax.experimental.pallas import tpu_sc as plsc

F32 = jnp.float32
BF16 = jnp.bfloat16
MESH = pl.DeviceIdType.MESH

N_DEV = 8
EPS = 1e-6
NEG_INF = -1e30
HEAD = 128
CHUNK = 64
WINDOW = 128
KEY_SPAN = 3 * WINDOW
KV_HEADS = 2
REL_BUCKETS = 32
REL_MAX_DIST = 128
ADAM_LR, ADAM_B1, ADAM_B2, ADAM_EPS, ADAM_WD, ADAM_STEP = 0.001, 0.9, 0.999, 1e-08, 0.01, 10
LANES = 128
VMEM_LIMIT = 56 * 1024 * 1024
ANY = pl.BlockSpec(memory_space=pl.ANY)


def _cparams(*sem):
    return pltpu.CompilerParams(dimension_semantics=sem if sem else None, vmem_limit_bytes=VMEM_LIMIT)


def _dot(a, b):
    return jnp.dot(a, b, preferred_element_type=F32)


def _dot_nt(a, b):
    return lax.dot_general(a, b, (((1,), (1,)), ((), ())), preferred_element_type=F32)


def _dot_tn(a, b):
    return lax.dot_general(a, b, (((0,), (0,)), ((), ())), preferred_element_type=F32)


def _tile(dim, target):
    for c in (target, 1024, 512, 256, 128):
        if c <= target and dim % c == 0:
            return c
    return dim


def _row_tile(rows, target):
    fits = [c for c in range(16, min(rows, target) + 1, 16) if rows % c == 0]
    return max(fits) if fits else rows


K_WHOLE = 2048
K_STEP = 2816


def _k_tile(kd):
    if kd <= K_WHOLE:
        return kd
    return max(c for c in range(LANES, K_STEP + 1, LANES) if kd % c == 0)


def _matmul(name, a, b, *, mode, out_dtype, stack=False, halves=False, tm=1024, tn=1024, deps=()):
    grp = 1
    if mode == "nn":
        m, kd = a.shape
        n = b.shape[0] * b.shape[2] if stack else b.shape[1]
    elif mode == "nt":
        m = a.shape[-2]
        n, kd = (b.shape[1], b.shape[0] * b.shape[2]) if stack else b.shape
    else:
        kd, m = a.shape
        n = b.shape[-1] * (2 if halves else 1)
    if stack:
        n1 = b.shape[2] if mode != "tn" else n // N_DEV
        if mode == "nt":
            grp = 2 if 2 * n1 <= K_STEP else 1
            tk = grp * n1
        else:
            grp = 1 if n1 % LANES == 0 else 2
            tn = grp * n1
        assert (grp * n1) % LANES == 0
    per_half = N_DEV // 2 // grp
    tm = _tile(m, tm)
    if not (stack and mode in ("nn", "tn")):
        tn = _tile(n, tn)
    if not (stack and mode == "nt"):
        tk = _k_tile(kd)
    nk = kd // tk
    lead = None if grp == 1 else grp
    b_outer = nk == 1 and b.size > a.size
    grid = (n // tn, m // tm, nk) if b_outer else (m // tm, n // tn, nk)

    def spec(shape, index):
        return pl.BlockSpec(shape, (lambda g0, g1, k: index(g1, g0, k)) if b_outer else index)

    if mode == "nn":
        a_spec = spec((tm, tk), lambda i, j, k: (i, k))
        if stack:
            b_spec = spec((lead, tk, n1), lambda i, j, k: (j, k, 0))
        else:
            b_spec = spec((tk, tn), lambda i, j, k: (k, j))
        dot = _dot
    elif mode == "nt":
        if halves:
            a_spec = spec((None, tm, tk), lambda i, j, k: (k // per_half, i, k % per_half))
        else:
            a_spec = spec((tm, tk), lambda i, j, k: (i, k))
        if stack:
            b_spec = spec((lead, tn, n1), lambda i, j, k: (k, j, 0))
        else:
            b_spec = spec((tn, tk), lambda i, j, k: (j, k))
        dot = _dot_nt
    else:
        a_spec = spec((tk, tm), lambda i, j, k: (k, i))
        if halves:
            b_spec = spec((None, tk, tn), lambda i, j, k: (j // per_half, k, j % per_half))
        else:
            b_spec = spec((tk, tn), lambda i, j, k: (k, j))
        dot = _dot_tn
    if stack and mode == "tn":
        out_shape = jax.ShapeDtypeStruct((N_DEV, m, n1), out_dtype)
        o_spec = spec((lead, tm, n1), lambda i, j, k: (j, i, 0))
    else:
        out_shape = jax.ShapeDtypeStruct((m, n), out_dtype)
        o_spec = spec((tm, tn), lambda i, j, k: (i, j))
    b_grouped = stack and grp > 1 and mode != "tn"
    o_grouped = stack and grp > 1 and mode == "tn"

    def product(a_ref, b_ref):
        bmat = jnp.concatenate([b_ref[s] for s in range(grp)], axis=1) if b_grouped else b_ref[...]
        return dot(a_ref[...], bmat)

    def store(o_ref, val):
        if o_grouped:
            for s in range(grp):
                o_ref[s] = val[:, s * n1:(s + 1) * n1].astype(o_ref.dtype)
        else:
            o_ref[...] = val.astype(o_ref.dtype)

    def body_whole(a_ref, b_ref, *rest):
        store(rest[-1], product(a_ref, b_ref))

    def body_steps(a_ref, b_ref, *rest):
        o_ref, acc_ref = rest[-2:]
        k = pl.program_id(2)

        @pl.when(k == 0)
        def _():
            acc_ref[...] = product(a_ref, b_ref)

        @pl.when(k > 0)
        def _():
            acc_ref[...] += product(a_ref, b_ref)

        @pl.when(k == nk - 1)
        def _():
            store(o_ref, acc_ref[...])

    return pl.pallas_call(
        body_whole if nk == 1 else body_steps, name=name, grid=grid,
        in_specs=[a_spec, b_spec] + [ANY] * len(deps), out_specs=o_spec, out_shape=out_shape,
        scratch_shapes=[] if nk == 1 else [pltpu.VMEM((tm, tn), F32)],
        compiler_params=_cparams("parallel", "parallel", "arbitrary"),
    )(a, b, *deps)


def _ffn_up(name, n, w_stack):
    t, d = n.shape
    s, _, n1 = w_stack.shape
    half = s // 2
    tm = _tile(t, 512)

    def body(n_ref, wg_ref, wu_ref, act_ref, gu_ref):
        nv = n_ref[...]
        gate = _dot(nv, wg_ref[...])
        up = _dot(nv, wu_ref[...])
        act_ref[...] = (gate * jax.nn.sigmoid(gate) * up).astype(BF16)
        gu_ref[0] = gate.astype(BF16)
        gu_ref[1] = up.astype(BF16)

    return pl.pallas_call(
        body, name=name, grid=(half, t // tm),
        in_specs=[pl.BlockSpec((tm, d), lambda j, i: (i, 0)),
                  pl.BlockSpec((None, d, n1), lambda j, i: (j, 0, 0)),
                  pl.BlockSpec((None, d, n1), lambda j, i: (half + j, 0, 0))],
        out_specs=[pl.BlockSpec((tm, n1), lambda j, i: (i, j)), pl.BlockSpec((2, tm, n1), lambda j, i: (0, i, j))],
        out_shape=[jax.ShapeDtypeStruct((t, half * n1), BF16), jax.ShapeDtypeStruct((2, t, half * n1), BF16)],
        compiler_params=_cparams("parallel", "parallel"),
    )(n, w_stack, w_stack)


def _ffn_dact(name, dff, w_d, gu, deps=()):
    t, d = dff.shape
    f = w_d.shape[0]
    tm = _tile(t, 512)
    tn = _tile(f, 1408)

    def body(dff_ref, w_ref, gu_ref, *rest):
        dgu_ref = rest[-1]
        da = _dot_nt(dff_ref[...], w_ref[...]).astype(BF16)
        gate = gu_ref[0].astype(F32)
        sg = jax.nn.sigmoid(gate)
        dgu_ref[0] = da * gu_ref[1] * (sg * (1.0 + gate * (1.0 - sg))).astype(BF16)
        dgu_ref[1] = da * (gate * sg).astype(BF16)

    pair = pl.BlockSpec((2, tm, tn), lambda j, i: (0, i, j))
    return pl.pallas_call(
        body, name=name, grid=(f // tn, t // tm),
        in_specs=[pl.BlockSpec((tm, d), lambda j, i: (i, 0)), pl.BlockSpec((tn, d), lambda j, i: (j, 0)), pair]
        + [ANY] * len(deps),
        out_specs=pair, out_shape=jax.ShapeDtypeStruct((2, t, f), BF16),
        compiler_params=_cparams("parallel", "parallel"),
    )(dff, w_d, gu, *deps)


ROWS = 256


def _rstd(xf):
    return lax.rsqrt(jnp.mean(xf * xf, axis=-1, keepdims=True) + EPS)


def _row_spec(t, d):
    return pl.BlockSpec((min(ROWS, t), d), lambda i: (i, 0))


def _vec_spec(d):
    return pl.BlockSpec((1, d), lambda i: (0, 0))


def _pre_norm(name, x, gain):
    t, d = x.shape

    def body(x_ref, g_ref, n_ref):
        xf = x_ref[...]
        n_ref[...] = (xf * _rstd(xf) * g_ref[...]).astype(BF16)

    return pl.pallas_call(
        body, name=name, grid=(t // min(ROWS, t),), in_specs=[_row_spec(t, d), _vec_spec(d)],
        out_specs=_row_spec(t, d), out_shape=jax.ShapeDtypeStruct((t, d), BF16),
        compiler_params=_cparams("parallel"),
    )(x, gain)


def _post_res_pre(name, x, ff, g_post, g_next, scale):
    t, d = x.shape

    def body(x_ref, ff_ref, gp_ref, gn_ref, xo_ref, n_ref):
        ff_ = ff_ref[...]
        xn = x_ref[...] + scale * (ff_ * _rstd(ff_) * gp_ref[...])
        xo_ref[...] = xn
        n_ref[...] = (xn * _rstd(xn) * gn_ref[...]).astype(BF16)

    return pl.pallas_call(
        body, name=name, grid=(t // min(ROWS, t),),
        in_specs=[_row_spec(t, d), _row_spec(t, d), _vec_spec(d), _vec_spec(d)],
        out_specs=[_row_spec(t, d), _row_spec(t, d)],
        out_shape=[jax.ShapeDtypeStruct((t, d), F32), jax.ShapeDtypeStruct((t, d), BF16)],
        compiler_params=_cparams("parallel"),
    )(x, ff, g_post, g_next)


def _post_res_loss(name, x, ff, g_post, target, scale):
    t, d = x.shape

    def body(x_ref, ff_ref, gp_ref, tg_ref, dy_ref, loss_ref):
        ff_ = ff_ref[...]
        err = x_ref[...] + scale * (ff_ * _rstd(ff_) * gp_ref[...]) - tg_ref[...]
        dy_ref[...] = err / d
        part = 0.5 * jnp.sum(jnp.mean(err * err, axis=-1, keepdims=True), axis=0, keepdims=True)

        @pl.when(pl.program_id(0) == 0)
        def _():
            loss_ref[...] = jnp.zeros_like(loss_ref)

        loss_ref[...] += jnp.broadcast_to(part, loss_ref.shape)

    return pl.pallas_call(
        body, name=name, grid=(t // min(ROWS, t),),
        in_specs=[_row_spec(t, d), _row_spec(t, d), _vec_spec(d), _row_spec(t, d)],
        out_specs=[_row_spec(t, d), _vec_spec(LANES)],
        out_shape=[jax.ShapeDtypeStruct((t, d), F32), jax.ShapeDtypeStruct((1, LANES), F32)],
        compiler_params=_cparams("arbitrary"),
    )(x, ff, g_post, target)


def _post_bwd(name, dy, ff, g_post, scale):
    t, d = dy.shape

    def body(dy_ref, ff_ref, gp_ref, dff_ref, dg_ref):
        ff_ = ff_ref[...]
        r = _rstd(ff_)
        xh = ff_ * r
        dyn = scale * dy_ref[...]
        dxh = dyn * gp_ref[...]
        dff_ref[...] = (r * (dxh - xh * jnp.mean(dxh * xh, axis=-1, keepdims=True))).astype(BF16)

        @pl.when(pl.program_id(0) == 0)
        def _():
            dg_ref[...] = jnp.zeros_like(dg_ref)

        dg_ref[...] += jnp.sum(dyn * xh, axis=0, keepdims=True)

    return pl.pallas_call(
        body, name=name, grid=(t // min(ROWS, t),),
        in_specs=[_row_spec(t, d), _row_spec(t, d), _vec_spec(d)],
        out_specs=[_row_spec(t, d), _vec_spec(d)],
        out_shape=[jax.ShapeDtypeStruct((t, d), BF16), jax.ShapeDtypeStruct((1, d), F32)],
        compiler_params=_cparams("arbitrary"),
    )(dy, ff, g_post)


def _pre_bwd(name, dn, x, g_pre, dy, deps=()):
    t, d = x.shape

    def body(dn_ref, x_ref, g_ref, dy_ref, *rest):
        dx_ref, dg_ref = rest[-2:]
        xf = x_ref[...]
        r = _rstd(xf)
        xh = xf * r
        dnf = dn_ref[...].astype(F32)
        dxh = dnf * g_ref[...]
        dx_ref[...] = dy_ref[...] + r * (dxh - xh * jnp.mean(dxh * xh, axis=-1, keepdims=True))

        @pl.when(pl.program_id(0) == 0)
        def _():
            dg_ref[...] = jnp.zeros_like(dg_ref)

        dg_ref[...] += jnp.sum(dnf * xh, axis=0, keepdims=True)

    return pl.pallas_call(
        body, name=name, grid=(t // min(ROWS, t),),
        in_specs=[_row_spec(t, d), _row_spec(t, d), _vec_spec(d), _row_spec(t, d)] + [ANY] * len(deps),
        out_specs=[_row_spec(t, d), _vec_spec(d)],
        out_shape=[jax.ShapeDtypeStruct((t, d), F32), jax.ShapeDtypeStruct((1, d), F32)],
        compiler_params=_cparams("arbitrary"),
    )(dn, x, g_pre, dy, *deps)


def _bdot(a, b, ca, cb, precision=None):
    return lax.dot_general(a, b, (((ca,), (cb,)), ((0,), (0,))), preferred_element_type=F32, precision=precision)


def _tri_masks(g):
    row = lax.broadcasted_iota(jnp.int32, (g, CHUNK, CHUNK), 1)
    col = lax.broadcasted_iota(jnp.int32, (g, CHUNK, CHUNK), 2)
    return col <= row, col >= row


def _ones_matmul(ones_mat, val):
    hi = val.astype(BF16)
    lo = (val - hi.astype(F32)).astype(BF16)
    return _bdot(ones_mat, hi, 2, 1) + _bdot(ones_mat, lo, 2, 1)


def _hgrn_block(z, lb, q, v, cum_mat):
    sg = jax.nn.sigmoid(z)
    f = lb + (1.0 - lb) * sg
    lf = jnp.log(f)
    k = 1.0 - f
    a = _ones_matmul(cum_mat, lf)
    last = jnp.sum(lf, axis=1, keepdims=True)
    e_a = jnp.exp(a)
    e_na = jnp.exp(-a)
    e_t = jnp.exp(last - a)
    return dict(sg=sg, f=f, k=k, decay=jnp.exp(last), e_a=e_a, e_na=e_na, e_t=e_t,
                qd=q * e_a, kd=k * e_na, kt=k * e_t)


def _hgrn_states(state, kv, decay, order):
    entering = [None] * len(order)
    for g in order:
        entering[g] = state
        state = decay[g] * state + kv[g]
    return jnp.stack(entering, axis=0), state


def _hgrn_fwd(p, lb_f, lb_b, gain, n_heads):
    t = p.shape[0]
    w = n_heads * HEAD
    blk = min(8, t // CHUNK)
    rows_blk = blk * CHUNK
    n_blocks = t // rows_blk
    fin_rows = min(256, t)

    def body(q_ref, i_ref, zf_ref, zb_ref, g_ref, lbf_ref, lbb_ref, gain_ref, y_ref, o_ref, st_ref):
        low, up = _tri_masks(blk)
        m_low, m_up = low.astype(BF16), up.astype(BF16)
        o_ref[...] = jnp.zeros_like(o_ref)
        st_ref[...] = jnp.zeros_like(st_ref)

        def one(r0, z_ref, lb, slot, rev):
            rows = pl.ds(r0, rows_blk)
            split = lambda ref: ref[rows, :].reshape(blk, CHUNK, HEAD)
            q, v = split(q_ref), split(i_ref)
            c = _hgrn_block(split(z_ref), lb, q, v, m_up if rev else m_low)
            qd, kd, kt, vb = c["qd"].astype(BF16), c["kd"].astype(BF16), c["kt"].astype(BF16), v.astype(BF16)
            pm = jnp.where(up if rev else low, _bdot(qd, kd, 2, 2), 0.0).astype(BF16)
            kv = _bdot(vb, kt, 1, 1)
            order = range(blk - 1, -1, -1) if rev else range(blk)
            entering, st_ref[slot] = _hgrn_states(st_ref[slot], kv, c["decay"], order)
            o = _bdot(pm, vb, 2, 1) + _bdot(qd, entering.astype(BF16), 2, 2)
            o_ref[rows, :] += o.reshape(rows_blk, HEAD)

        def step(n, carry):
            one(pl.multiple_of(n * rows_blk, rows_blk), zf_ref, lbf_ref[...], 0, False)
            one(pl.multiple_of((n_blocks - 1 - n) * rows_blk, rows_blk), zb_ref, lbb_ref[...], 1, True)
            return carry

        lax.fori_loop(0, n_blocks, step, 0)

        def fin(n, carry):
            rows = pl.ds(pl.multiple_of(n * fin_rows, fin_rows), fin_rows)
            o = o_ref[rows, :]
            g = g_ref[rows, :]
            y_ref[rows, :] = (o * _rstd(o) * gain_ref[...] * (g * jax.nn.sigmoid(g))).astype(BF16)
            return carry

        lax.fori_loop(0, t // fin_rows, fin, 0)

    col = lambda grp: pl.BlockSpec((t, HEAD), lambda h: (0, grp * n_heads + h))
    vec = pl.BlockSpec((1, HEAD), lambda h: (0, h))
    out = pl.BlockSpec((t, HEAD), lambda h: (0, h))
    return pl.pallas_call(
        body, name="hgrn_fwd", grid=(n_heads,),
        in_specs=[col(0), col(1), col(2), col(3), col(4), vec, vec, vec],
        out_specs=[out, out],
        out_shape=[jax.ShapeDtypeStruct((t, w), BF16), jax.ShapeDtypeStruct((t, w), F32)],
        scratch_shapes=[pltpu.VMEM((2, HEAD, HEAD), F32)],
        compiler_params=_cparams("parallel"),
    )(p, p, p, p, p, lb_f, lb_b, gain)


def _hgrn_bwd(p, o_raw, dcat, lb_f, lb_b, gain, n_heads):
    t = p.shape[0]
    w = n_heads * HEAD
    n_chunks = t // CHUNK
    blk = min(8, n_chunks)
    rows_blk = blk * CHUNK
    n_blocks = t // rows_blk
    rb = min(256, t)

    def body(q_ref, i_ref, zf_ref, zb_ref, g_ref, o_ref, dy_ref, lbf_ref, lbb_ref, gain_ref,
             dq_ref, di_ref, dzf_ref, dzb_ref, dg_ref, dlbf_ref, dlbb_ref, dgain_ref,
             do_s, dq_s, dv_s, st_s, cur_s):
        low, up = _tri_masks(blk)
        m_low, m_up = low.astype(BF16), up.astype(BF16)
        rowid = lax.broadcasted_iota(jnp.int32, (blk, CHUNK, HEAD), 1)
        gain_v = gain_ref[...]

        def norm_bwd(n, dgain):
            rows = pl.ds(pl.multiple_of(n * rb, rb), rb)
            o = o_ref[rows, :]
            g = g_ref[rows, :]
            dy = dy_ref[rows, :]
            r = _rstd(o)
            oh = o * r
            sg = jax.nn.sigmoid(g)
            dg_ref[rows, :] = (dy * oh * gain_v * (sg * (1.0 + g * (1.0 - sg)))).astype(BF16)
            dno = dy * (g * sg)
            dxh = dno * gain_v
            do_s[rows, :] = r * (dxh - oh * jnp.mean(dxh * oh, axis=-1, keepdims=True))
            return dgain + jnp.sum(dno * oh, axis=0, keepdims=True)

        dgain_ref[...] = lax.fori_loop(0, t // rb, norm_bwd, jnp.zeros((1, HEAD), F32))
        dq_s[...] = jnp.zeros_like(dq_s)
        dv_s[...] = jnp.zeros_like(dv_s)

        def direction(z_ref, lb_ref, dz_ref, dlb_ref, rev):
            lb = lb_ref[...]
            cum_mat = m_up if rev else m_low
            cum_mat_t = m_low if rev else m_up
            mask = up if rev else low
            last_row = 0 if rev else CHUNK - 1

            order = range(blk - 1, -1, -1) if rev else range(blk)

            def rows_of(j):
                bidx = (n_blocks - 1 - j) if rev else j
                return bidx, pl.ds(pl.multiple_of(bidx * rows_blk, rows_blk), rows_blk)

            def load(rows):
                split = lambda ref: ref[rows, :].reshape(blk, CHUNK, HEAD)
                q, v = split(q_ref), split(i_ref)
                return q, v, _hgrn_block(split(z_ref), lb, q, v, cum_mat)

            def sweep_fwd(j, carry):
                bidx, rows = rows_of(j)
                _, v, c = load(rows)
                kv = _bdot(v.astype(BF16), c["kt"].astype(BF16), 1, 1)
                st_s[pl.ds(bidx * blk, blk)], cur_s[0] = _hgrn_states(cur_s[0], kv, c["decay"], order)
                return carry

            cur_s[...] = jnp.zeros_like(cur_s)
            dlb_ref[...] = jnp.zeros_like(dlb_ref)
            lax.fori_loop(0, n_blocks, sweep_fwd, 0)

            def sweep_bwd(jj, carry):
                bidx, rows = rows_of(n_blocks - 1 - jj)
                _, v, c = load(rows)
                st = st_s[pl.ds(bidx * blk, blk)]
                do = do_s[rows, :].reshape(blk, CHUNK, HEAD)
                qd, kd, kt, decay = c["qd"], c["kd"], c["kt"], c["decay"]
                qd_b, kd_b, kt_b = qd.astype(BF16), kd.astype(BF16), kt.astype(BF16)
                v_b, do_b, st_b = v.astype(BF16), do.astype(BF16), st.astype(BF16)
                pm = jnp.where(mask, _bdot(qd_b, kd_b, 2, 2), 0.0).astype(BF16)
                dpm = jnp.where(mask, _bdot(do_b, v_b, 2, 2), 0.0).astype(BF16)
                gq = _bdot(do_b, qd_b, 1, 1)
                dstate = cur_s[1]
                after = [None] * blk
                for g in reversed(order):
                    after[g] = dstate
                    dstate = gq[g] + decay[g] * dstate
                cur_s[1] = dstate
                dst = jnp.stack(after, axis=0)
                dst_b = dst.astype(BF16)
                dv = _bdot(pm, do_b, 1, 1) + _bdot(kt_b, dst_b, 2, 2)
                dqd = _bdot(dpm, kd_b, 2, 1) + _bdot(do_b, st_b, 2, 1)
                dkd = _bdot(dpm, qd_b, 1, 1)
                dkt = _bdot(v_b, dst_b, 2, 1)
                dlast = (jnp.sum(dkt * kt, axis=1, keepdims=True)
                         + decay * jnp.sum(dst * st, axis=1, keepdims=True))
                dq_s[rows, :] += (dqd * c["e_a"]).reshape(rows_blk, HEAD)
                dv_s[rows, :] += dv.reshape(rows_blk, HEAD)
                dk = dkd * c["e_na"] + dkt * c["e_t"]
                da = dqd * qd - dkd * kd - dkt * kt
                da = da + jnp.where(rowid == last_row, dlast, 0.0)
                dlf = _ones_matmul(cum_mat_t, da)
                df = dlf / c["f"] - dk
                sg = c["sg"]
                dz_ref[rows, :] = (df * (1.0 - lb) * (sg * (1.0 - sg))).reshape(rows_blk, HEAD).astype(BF16)
                dlb_ref[...] += jnp.sum((df * (1.0 - sg)).reshape(rows_blk, HEAD), axis=0, keepdims=True)
                return carry

            lax.fori_loop(0, n_blocks, sweep_bwd, 0)

        direction(zf_ref, lbf_ref, dzf_ref, dlbf_ref, False)
        direction(zb_ref, lbb_ref, dzb_ref, dlbb_ref, True)
        dq_ref[...] = dq_s[...].astype(BF16)
        di_ref[...] = dv_s[...].astype(BF16)

    col = lambda grp: pl.BlockSpec((t, HEAD), lambda h: (0, grp * n_heads + h))
    one = pl.BlockSpec((t, HEAD), lambda h: (0, h))
    vec = pl.BlockSpec((1, HEAD), lambda h: (0, h))
    big = jax.ShapeDtypeStruct((t, w), BF16)
    small = jax.ShapeDtypeStruct((1, w), F32)
    return pl.pallas_call(
        body, name="hgrn_bwd", grid=(n_heads,),
        in_specs=[col(0), col(1), col(2), col(3), col(4), one, one, vec, vec, vec],
        out_specs=[one] * 5 + [vec] * 3,
        out_shape=[big] * 5 + [small] * 3,
        scratch_shapes=[pltpu.VMEM((t, HEAD), F32), pltpu.VMEM((t, HEAD), F32), pltpu.VMEM((t, HEAD), F32),
                        pltpu.VMEM((n_chunks, HEAD, HEAD), F32), pltpu.VMEM((2, HEAD, HEAD), F32)],
        compiler_params=_cparams("parallel"),
    )(p, p, p, p, p, o_raw, dcat, lb_f, lb_b, gain)


def _t5_bucket_index():
    c = np.arange(WINDOW)[:, None]
    s = np.arange(KEY_SPAN)[None, :]
    rel = s - WINDOW - c
    nb = REL_BUCKETS // 2
    max_exact = nb // 2
    bucket = (rel > 0).astype(np.int32) * nb
    n = np.abs(rel)
    large = max_exact + (np.log(np.maximum(n, 1) / max_exact) / np.log(REL_MAX_DIST / max_exact)
                         * (nb - max_exact)).astype(np.int32)
    large = np.minimum(large, nb - 1)
    return bucket + np.where(n < max_exact, n, large).astype(np.int32)


def _bias_build(table, idx):
    n_attn = table.shape[1]

    def body(tab_ref, idx_ref, o_ref):
        h = pl.program_id(0)
        idx_v = idx_ref[...]
        acc = jnp.zeros((WINDOW, KEY_SPAN), F32)
        for b in range(REL_BUCKETS):
            acc = jnp.where(idx_v == b, tab_ref[b, h], acc)
        o_ref[...] = acc

    return pl.pallas_call(
        body, name="bias_build", grid=(n_attn,),
        in_specs=[pl.BlockSpec(memory_space=pltpu.SMEM), pl.BlockSpec((WINDOW, KEY_SPAN), lambda h: (0, 0))],
        out_specs=pl.BlockSpec((None, WINDOW, KEY_SPAN), lambda h: (h, 0, 0)),
        out_shape=jax.ShapeDtypeStruct((n_attn, WINDOW, KEY_SPAN), F32), compiler_params=_cparams("parallel"),
    )(table, idx)


def _bias_reduce(dbias, idx):
    n_attn = dbias.shape[0]

    def body(idx_ref, d_ref, o_ref):
        idx_v = idx_ref[...]
        dv = d_ref[...]
        rows = lax.broadcasted_iota(jnp.int32, (REL_BUCKETS, LANES), 0)
        acc = jnp.zeros((REL_BUCKETS, LANES), F32)
        for b in range(REL_BUCKETS):
            part = jnp.sum(jnp.where(idx_v == b, dv, 0.0), axis=1, keepdims=True)
            acc = jnp.where(rows == b, jnp.sum(part, axis=0, keepdims=True), acc)
        o_ref[...] = acc

    return pl.pallas_call(
        body, name="bias_reduce", grid=(n_attn,),
        in_specs=[pl.BlockSpec((WINDOW, KEY_SPAN), lambda h: (0, 0)),
                  pl.BlockSpec((None, WINDOW, KEY_SPAN), lambda h: (h, 0, 0))],
        out_specs=pl.BlockSpec((None, REL_BUCKETS, LANES), lambda h: (h, 0, 0)),
        out_shape=jax.ShapeDtypeStruct((n_attn, REL_BUCKETS, LANES), F32), compiler_params=_cparams("parallel"),
    )(idx, dbias)


def _attn_probs(q, kb, bias, sink, valid):
    s = _dot_nt(q, kb) / math.sqrt(HEAD) + bias
    s = jnp.where(valid, s, NEG_INF)
    m = jnp.maximum(jnp.max(s, axis=-1, keepdims=True), sink)
    e = jnp.exp(s - m)
    e_sink = jnp.exp(sink - m)
    den = jnp.sum(e, axis=-1, keepdims=True) + e_sink
    return e / den, e_sink / den


def _attn_valid(n, t, grp):
    c = lax.broadcasted_iota(jnp.int32, (grp * WINDOW, KEY_SPAN), 0) & (WINDOW - 1)
    s = lax.broadcasted_iota(jnp.int32, (grp * WINDOW, KEY_SPAN), 1)
    rel = s - WINDOW - c
    key_pos = n * WINDOW - WINDOW + s
    return (jnp.abs(rel) <= WINDOW) & (key_pos >= 0) & (key_pos < t)


def _stack_heads(ref, grp):
    return jnp.concatenate([ref[:, g * HEAD:(g + 1) * HEAD] for g in range(grp)], axis=0).astype(BF16)


def _sink_column(sink_ref, x, grp):
    return jnp.concatenate([jnp.full((WINDOW, 1), sink_ref[0, x * grp + g], F32) for g in range(grp)], axis=0)


def _attn_specs(t, n_hgrn, n_attn):
    grp = n_attn // KV_HEADS
    nb = t // WINDOW
    cq = 5 * n_hgrn
    ck = cq + n_attn
    cv = ck + KV_HEADS
    q_spec = pl.BlockSpec((WINDOW, grp * HEAD), lambda x, n: (n, cq // grp + x))
    kv = lambda base, off: pl.BlockSpec(
        (WINDOW, HEAD), lambda x, n: (jnp.clip(n + off, 0, nb - 1), base + x))
    band = [kv(ck, -1), kv(ck, 0), kv(ck, 1), kv(cv, -1), kv(cv, 0), kv(cv, 1)]
    bias_spec = pl.BlockSpec((grp, WINDOW, KEY_SPAN), lambda x, n: (x, 0, 0))
    sink_spec = pl.BlockSpec(memory_space=pltpu.SMEM)
    return grp, nb, q_spec, band, bias_spec, sink_spec


def _attn_fwd(p, bias, sink, n_hgrn, n_attn):
    t = p.shape[0]
    grp, nb, q_spec, band, bias_spec, sink_spec = _attn_specs(t, n_hgrn, n_attn)

    def body(q_ref, kp, kc, kn, vp, vc, vn, bias_ref, sink_ref, y_ref):
        x, n = pl.program_id(0), pl.program_id(1)
        kb = jnp.concatenate([kp[...], kc[...], kn[...]], axis=0).astype(BF16)
        vb = jnp.concatenate([vp[...], vc[...], vn[...]], axis=0).astype(BF16)
        pr, _ = _attn_probs(_stack_heads(q_ref, grp), kb, bias_ref[...].reshape(grp * WINDOW, KEY_SPAN),
                            _sink_column(sink_ref, x, grp), _attn_valid(n, t, grp))
        y = _dot(pr.astype(BF16), vb).astype(BF16)
        for g in range(grp):
            y_ref[:, g * HEAD:(g + 1) * HEAD] = y[g * WINDOW:(g + 1) * WINDOW]

    return pl.pallas_call(
        body, name="attn_fwd", grid=(KV_HEADS, nb),
        in_specs=[q_spec] + band + [bias_spec, sink_spec],
        out_specs=pl.BlockSpec((WINDOW, grp * HEAD), lambda x, n: (n, x)),
        out_shape=jax.ShapeDtypeStruct((t, n_attn * HEAD), BF16),
        compiler_params=_cparams("parallel", "parallel"),
    )(p, p, p, p, p, p, p, bias, sink)


def _attn_bwd(p, dcat, bias, sink, n_hgrn, n_attn, deps=()):
    t = p.shape[0]
    grp, nb, q_spec, band, bias_spec, sink_spec = _attn_specs(t, n_hgrn, n_attn)
    inv = 1.0 / math.sqrt(HEAD)

    def body(q_ref, kp, kc, kn, vp, vc, vn, bias_ref, sink_ref, do_ref, *rest):
        dq_ref, dk_ref, dv_ref, dbias_ref, dsink_ref, dk_s, dv_s = rest[-7:]
        x, n = pl.program_id(0), pl.program_id(1)

        @pl.when(n == 0)
        def _():
            dk_s[...] = jnp.zeros_like(dk_s)
            dv_s[...] = jnp.zeros_like(dv_s)
            dbias_ref[...] = jnp.zeros_like(dbias_ref)
            dsink_ref[...] = jnp.zeros_like(dsink_ref)

        kb = jnp.concatenate([kp[...], kc[...], kn[...]], axis=0).astype(BF16)
        vb = jnp.concatenate([vp[...], vc[...], vn[...]], axis=0).astype(BF16)
        q = _stack_heads(q_ref, grp)
        do = _stack_heads(do_ref, grp)
        pr, p_sink = _attn_probs(q, kb, bias_ref[...].reshape(grp * WINDOW, KEY_SPAN),
                                 _sink_column(sink_ref, x, grp), _attn_valid(n, t, grp))
        dpr = _dot_nt(do, vb)
        delta = jnp.sum(pr * dpr, axis=-1, keepdims=True)
        ds = pr * (dpr - delta)
        ds_b = ds.astype(BF16)
        dq = (_dot(ds_b, kb) * inv).astype(BF16)
        dsink = -p_sink * delta
        for g in range(grp):
            head = slice(g * WINDOW, (g + 1) * WINDOW)
            dq_ref[:, g * HEAD:(g + 1) * HEAD] = dq[head]
            dbias_ref[g] += ds[head]
            dsink_ref[g:g + 1, :] += jnp.broadcast_to(jnp.sum(dsink[head], axis=0, keepdims=True), (1, WINDOW))
        rows = pl.ds(pl.multiple_of(n * WINDOW, WINDOW), KEY_SPAN)
        dk_s[rows, :] += _dot_tn(ds_b, q) * inv
        dv_s[rows, :] += _dot_tn(pr.astype(BF16), do)

        @pl.when(n == nb - 1)
        def _():
            dk_ref[...] = dk_s[pl.ds(WINDOW, t), :].astype(BF16)
            dv_ref[...] = dv_s[pl.ds(WINDOW, t), :].astype(BF16)

    do_spec = pl.BlockSpec((WINDOW, grp * HEAD), lambda x, n: (n, n_hgrn // grp + x))
    kv_out = pl.BlockSpec((t, HEAD), lambda x, n: (0, x))
    return pl.pallas_call(
        body, name="attn_bwd", grid=(KV_HEADS, nb),
        in_specs=[q_spec] + band + [bias_spec, sink_spec, do_spec] + [ANY] * len(deps),
        out_specs=[pl.BlockSpec((WINDOW, grp * HEAD), lambda x, n: (n, x)), kv_out, kv_out,
                   bias_spec, pl.BlockSpec((None, grp, WINDOW), lambda x, n: (x, 0, 0))],
        out_shape=[jax.ShapeDtypeStruct((t, n_attn * HEAD), BF16),
                   jax.ShapeDtypeStruct((t, KV_HEADS * HEAD), BF16),
                   jax.ShapeDtypeStruct((t, KV_HEADS * HEAD), BF16),
                   jax.ShapeDtypeStruct((n_attn, WINDOW, KEY_SPAN), F32),
                   jax.ShapeDtypeStruct((KV_HEADS, grp, WINDOW), F32)],
        scratch_shapes=[pltpu.VMEM((t + 2 * WINDOW, HEAD), F32), pltpu.VMEM((t + 2 * WINDOW, HEAD), F32)],
        compiler_params=_cparams("parallel", "arbitrary"),
    )(p, p, p, p, p, p, p, bias, sink, dcat, *deps)


def _position():
    return lax.axis_index("x"), lax.axis_index("y"), lax.axis_index("c")


def _handshake(peers):
    barrier = pltpu.get_barrier_semaphore()
    for peer in peers:
        pl.semaphore_signal(barrier, inc=1, device_id=peer, device_id_type=MESH)
    pl.semaphore_wait(barrier, len(peers))


def _sequencer(name, collective_id, scratch_types):
    return functools.partial(
        pl.kernel, mesh=plsc.ScalarSubcoreMesh(axis_name="sc", num_cores=1), name=name,
        scratch_types=scratch_types, compiler_params=pltpu.CompilerParams(collective_id=collective_id))


def _all_gather(name, shard, collective_id):
    rows = shard.shape[0]
    assert rows % 2 == 0
    rh = rows // 2
    src = jax.new_ref(shard, memory_space=pltpu.MemorySpace.HBM)
    out = jax.empty_ref(jax.ShapeDtypeStruct((N_DEV,) + shard.shape, shard.dtype),
                        memory_space=pltpu.MemorySpace.HBM)
    n_copies = 11

    @_sequencer(name, collective_id, (pltpu.SemaphoreType.DMA((n_copies,)), pltpu.SemaphoreType.DMA((n_copies,)),
                                      pltpu.SemaphoreType.DMA))
    def launch(send_sems, recv_sems, local_sem):
        x, y, c = _position()
        sibling = (x, y, 1 - c)
        xn, yn, dg = (1 - x, y), (x, 1 - y), (1 - x, 1 - y)
        _handshake([sibling, (*xn, c), (*yn, c)])

        def part(ref, half):
            return ref if half is None else ref.at[pl.ds(half * rh, rh)]

        def slot(chip, core, half=None):
            return part(out.at[4 * chip[0] + 2 * chip[1] + core], half)

        def copy(k, chip, core, half, to, own=False):
            return pltpu.make_async_remote_copy(
                src_ref=part(src, half) if own else slot(chip, core, half), dst_ref=slot(chip, core, half),
                send_sem=send_sems.at[k], recv_sem=recv_sems.at[k], device_id=to, device_id_type=MESH)

        def landed(k, chip, core, half):
            copy(k, chip, core, half, (x, y, c)).wait_recv()

        mine = pltpu.make_async_copy(src, slot((x, y), c), local_sem)
        mine.start()
        sent = [copy(0, (x, y), c, None, sibling, own=True),
                copy(1, (x, y), c, 0, (*xn, c), own=True), copy(3, (x, y), c, 1, (*yn, c), own=True),
                copy(2, (x, y), c, 1, (*xn, c), own=True), copy(4, (x, y), c, 0, (*yn, c), own=True)]
        for cp in sent:
            cp.start()

        def then(cp):
            cp.start()
            sent.append(cp)

        landed(1, xn, c, 0)
        then(copy(5, xn, c, 0, (*yn, c)))
        landed(3, yn, c, 1)
        then(copy(6, yn, c, 1, (*xn, c)))
        landed(2, xn, c, 1)
        then(copy(7, xn, c, None, sibling))
        landed(4, yn, c, 0)
        then(copy(8, yn, c, None, sibling))
        landed(5, dg, c, 0)
        then(copy(9, dg, c, 0, sibling))
        landed(6, dg, c, 1)
        then(copy(10, dg, c, 1, sibling))
        landed(0, (x, y), 1 - c, None)
        landed(7, xn, 1 - c, None)
        landed(8, yn, 1 - c, None)
        landed(9, dg, 1 - c, 0)
        landed(10, dg, 1 - c, 1)
        for cp in sent:
            cp.wait_send()
        mine.wait()

    launch()
    return out[...]


HBM = pl.BlockSpec(memory_space=pltpu.HBM)
SEM = pl.BlockSpec(memory_space=pltpu.SEMAPHORE)
EFFECT = pltpu.SideEffectType.DATAFLOW_SIDE_EFFECTING


def _pair_copies(s_ref, land_ref, send_sems, recv_sems):
    x, y, c = _position()
    return [pltpu.make_async_remote_copy(
        src_ref=s_ref.at[2 * k + (1 - c)], dst_ref=land_ref.at[k], send_sem=send_sems.at[k],
        recv_sem=recv_sems.at[k], device_id=(x, y, 1 - c), device_id_type=MESH) for k in range(4)]


def _pair_start(name, stack):
    land_shape = (4,) + stack.shape[1:]

    def body(s_ref, land_ref, send_sems, recv_sems, s_thru, land_thru, token):
        for cp in _pair_copies(s_ref, land_ref, send_sems, recv_sems):
            cp.start()
        token[...] = jnp.zeros_like(token)

    return pl.pallas_call(
        body, name=name,
        out_shape=(pltpu.SemaphoreType.DMA((4,)), pltpu.SemaphoreType.DMA((4,)),
                   pltpu.HBM(stack.shape, stack.dtype), pltpu.HBM(land_shape, stack.dtype),
                   jax.ShapeDtypeStruct((8, LANES), F32)),
        in_specs=(HBM, HBM), out_specs=(SEM, SEM, HBM, HBM, pl.BlockSpec(memory_space=pltpu.VMEM)),
        input_output_aliases={0: 2, 1: 3}, compiler_params=pltpu.CompilerParams(has_side_effects=EFFECT),
    )(pltpu.with_memory_space_constraint(stack, pltpu.HBM),
      pltpu.with_memory_space_constraint(lax.empty(land_shape, stack.dtype), pltpu.HBM))


def _pair_wait(name, started, after):
    send_sems, recv_sems, s_thru, land_thru, _ = started

    def body(s_ref, land_ref, send_sems, recv_sems, after_ref, s_out, land_out):
        for cp in _pair_copies(s_ref, land_ref, send_sems, recv_sems):
            cp.wait_send()
            cp.wait_recv()

    return pl.pallas_call(
        body, name=name,
        out_shape=(pltpu.HBM(s_thru.shape, s_thru.dtype), pltpu.HBM(land_thru.shape, land_thru.dtype)),
        in_specs=(HBM, HBM, SEM, SEM, ANY), out_specs=(HBM, HBM), input_output_aliases={0: 0, 1: 1},
        compiler_params=pltpu.CompilerParams(has_side_effects=EFFECT),
    )(s_thru, land_thru, send_sems, recv_sems, after)


def _pair_sum(name, stack, other, core):
    _, r, c = stack.shape
    tr = _row_tile(r, 1024)

    def body(core_ref, a_ref, b_ref, o_ref):
        o_ref[...] = (a_ref[...].astype(F32) + b_ref[...].astype(F32)).astype(o_ref.dtype)

    grid_spec = pltpu.PrefetchScalarGridSpec(
        num_scalar_prefetch=1, grid=(4, r // tr),
        in_specs=[pl.BlockSpec((None, tr, c), lambda k, i, core_ref: (2 * k + core_ref[0], i, 0)),
                  pl.BlockSpec((None, tr, c), lambda k, i, core_ref: (k, i, 0))],
        out_specs=pl.BlockSpec((None, tr, c), lambda k, i, core_ref: (k, i, 0)))
    return pl.pallas_call(
        body, name=name, grid_spec=grid_spec, out_shape=jax.ShapeDtypeStruct((4, r, c), stack.dtype),
        compiler_params=_cparams("parallel", "parallel"),
    )(core, stack, other)


def _chip_exchange(name, sums, collective_id):
    src = jax.new_ref(sums, memory_space=pltpu.MemorySpace.HBM)
    out = jax.empty_ref(jax.ShapeDtypeStruct((3,) + sums.shape[1:], sums.dtype),
                        memory_space=pltpu.MemorySpace.HBM)

    @_sequencer(name, collective_id, (pltpu.SemaphoreType.DMA((3,)), pltpu.SemaphoreType.DMA((3,))))
    def launch(send_sems, recv_sems):
        x, y, c = _position()
        chips = [(1 - x, y), (x, 1 - y), (1 - x, 1 - y)]
        _handshake([(*chip, c) for chip in chips])
        copies = [pltpu.make_async_remote_copy(
            src_ref=src.at[2 * px + py], dst_ref=out.at[j], send_sem=send_sems.at[j],
            recv_sem=recv_sems.at[j], device_id=(px, py, c), device_id_type=MESH)
            for j, (px, py) in enumerate(chips)]
        for cp in copies:
            cp.start()
        for cp in copies:
            cp.wait()

    launch()
    return out[...]


def _small_all_reduce(name, part):
    r = part.shape[0]

    def body(x_ref, out_ref, gather, send_sems, recv_sems):
        x, y, c = _position()
        me = 4 * x + 2 * y + c
        gather[me] = x_ref[...]
        copies = []
        for k in range(1, N_DEV):
            peer = (x ^ (k >> 2), y ^ ((k >> 1) & 1), c ^ (k & 1))
            copies.append(pltpu.make_async_remote_copy(
                src_ref=x_ref, dst_ref=gather.at[me], send_sem=send_sems.at[k - 1],
                recv_sem=recv_sems.at[k - 1], device_id=peer, device_id_type=MESH))
        for cp in copies:
            cp.start()
        for k in range(1, N_DEV):
            peer_slot = 4 * (x ^ (k >> 2)) + 2 * (y ^ ((k >> 1) & 1)) + (c ^ (k & 1))
            pltpu.make_async_remote_copy(
                src_ref=x_ref, dst_ref=gather.at[peer_slot], send_sem=send_sems.at[k - 1],
                recv_sem=recv_sems.at[k - 1], device_id=(x, y, c), device_id_type=MESH).wait()
        acc = gather[0]
        for j in range(1, N_DEV):
            acc = acc + gather[j]
        out_ref[...] = acc

    vm = pl.BlockSpec(memory_space=pltpu.VMEM)
    return pl.pallas_call(
        body, name=name, in_specs=[vm], out_specs=vm, out_shape=jax.ShapeDtypeStruct((r, LANES), F32),
        scratch_shapes=[pltpu.VMEM((N_DEV, r, LANES), F32), pltpu.SemaphoreType.DMA((7,)),
                        pltpu.SemaphoreType.DMA((7,))],
    )(part)


def _adam_math(w, g, m, v):
    m = ADAM_B1 * m + (1.0 - ADAM_B1) * g
    v = ADAM_B2 * v + (1.0 - ADAM_B2) * jnp.square(g)
    m_hat = m / (1.0 - ADAM_B1 ** ADAM_STEP)
    v_hat = v / (1.0 - ADAM_B2 ** ADAM_STEP)
    delta = -ADAM_LR * (m_hat / (jnp.sqrt(v_hat) + ADAM_EPS) + ADAM_WD * w)
    return delta, m, v


def _adam_shard(name, w, m, v, sums, recv, chip, deps=()):
    r, c = w.shape
    tr = _row_tile(r, 256)

    def body(chip_ref, w_ref, m_ref, v_ref, own_ref, r0_ref, r1_ref, r2_ref, *rest):
        g_out, d_out, m_out, v_out = rest[-4:]
        g = ((own_ref[...].astype(F32) + r0_ref[...].astype(F32)) + r1_ref[...].astype(F32)) + r2_ref[...].astype(F32)
        delta, m_new, v_new = _adam_math(w_ref[...], g, m_ref[...], v_ref[...])
        g_out[...] = g
        d_out[...] = delta
        m_out[...] = m_new
        v_out[...] = v_new

    plain = pl.BlockSpec((tr, c), lambda i, chip_ref: (i, 0))
    piece = lambda j: pl.BlockSpec((None, tr, c), lambda i, chip_ref: (j, i, 0))
    grid_spec = pltpu.PrefetchScalarGridSpec(
        num_scalar_prefetch=1, grid=(r // tr,),
        in_specs=[plain, plain, plain,
                  pl.BlockSpec((None, tr, c), lambda i, chip_ref: (chip_ref[0], i, 0)),
                  piece(0), piece(1), piece(2)] + [ANY] * len(deps),
        out_specs=[plain] * 4)
    shape = jax.ShapeDtypeStruct((r, c), F32)
    return pl.pallas_call(
        body, name=name, grid_spec=grid_spec, out_shape=[shape] * 4, compiler_params=_cparams("parallel"),
    )(chip, w, m, v, sums, recv, recv, recv, *deps)


def _adam_small(name, w, g, m, v):
    r = w.shape[0]

    def body(w_ref, g_ref, m_ref, v_ref, d_out, m_out, v_out):
        delta, m_new, v_new = _adam_math(w_ref[...], g_ref[...], m_ref[...], v_ref[...])
        d_out[...] = delta
        m_out[...] = m_new
        v_out[...] = v_new

    vm = pl.BlockSpec(memory_space=pltpu.VMEM)
    shape = jax.ShapeDtypeStruct((r, LANES), F32)
    return pl.pallas_call(body, name=name, in_specs=[vm] * 4, out_specs=[vm] * 3, out_shape=[shape] * 3)(w, g, m, v)


def _reduce_scatter(tag, started, after, core, collective_id):
    grad_stack, other = _pair_wait("rs_pair_wait_" + tag, started, after)
    sums = _pair_sum("rs_sum_" + tag, grad_stack, other, core)
    return sums, _chip_exchange("rs_chip_" + tag, sums, collective_id)


def _pack(arrays):
    flat = jnp.concatenate([a.reshape(-1).astype(F32) for a in arrays])
    rows = -(-flat.shape[0] // LANES)
    rows = -(-rows // 8) * 8
    return jnp.pad(flat, (0, rows * LANES - flat.shape[0])).reshape(rows, LANES)


def _unpack(packed, like):
    flat = packed.reshape(-1)
    out, off = [], 0
    for a in like:
        out.append(flat[off:off + a.size].reshape(a.shape))
        off += a.size
    return out


SMALL = ("pre_norm_ffn1", "post_norm_ffn1", "pre_norm_mix", "post_norm_mix", "hgrn_lower_bounds_fwd",
         "hgrn_lower_bounds_bwd", "hgrn_out_norm", "attn_sink", "pre_norm_ffn2", "post_norm_ffn2", "rel_bias_table")
BIG = ("w_ffn1_gate_up", "w_ffn1_down", "w_mix_in", "w_mix_out", "w_ffn2_gate_up", "w_ffn2_down")
AG_ID = {n: 1 + i for i, n in enumerate(BIG)}
RS_ID = {n: 7 + i for i, n in enumerate(BIG)}
ORDER = ("pre_norm_ffn1", "post_norm_ffn1", "w_ffn1_gate_up", "w_ffn1_down", "pre_norm_mix", "post_norm_mix",
         "w_mix_in", "hgrn_lower_bounds_fwd", "hgrn_lower_bounds_bwd", "hgrn_out_norm", "attn_sink", "w_mix_out",
         "pre_norm_ffn2", "post_norm_ffn2", "w_ffn2_gate_up", "w_ffn2_down", "rel_bias_table")


def kernel(x, pre_norm_ffn1, post_norm_ffn1, w_ffn1_gate_up, w_ffn1_down, pre_norm_mix, post_norm_mix, w_mix_in, hgrn_lower_bounds_fwd, hgrn_lower_bounds_bwd, hgrn_out_norm, attn_sink, w_mix_out, pre_norm_ffn2, post_norm_ffn2, w_ffn2_gate_up, w_ffn2_down, rel_bias_table, loss_target, m_pre_norm_ffn1, m_post_norm_ffn1, m_w_ffn1_gate_up, m_w_ffn1_down, m_pre_norm_mix, m_post_norm_mix, m_w_mix_in, m_hgrn_lower_bounds_fwd, m_hgrn_lower_bounds_bwd, m_hgrn_out_norm, m_attn_sink, m_w_mix_out, m_pre_norm_ffn2, m_post_norm_ffn2, m_w_ffn2_gate_up, m_w_ffn2_down, m_rel_bias_table, v_pre_norm_ffn1, v_post_norm_ffn1, v_w_ffn1_gate_up, v_w_ffn1_down, v_pre_norm_mix, v_post_norm_mix, v_w_mix_in, v_hgrn_lower_bounds_fwd, v_hgrn_lower_bounds_bwd, v_hgrn_out_norm, v_attn_sink, v_w_mix_out, v_pre_norm_ffn2, v_post_norm_ffn2, v_w_ffn2_gate_up, v_w_ffn2_down, v_rel_bias_table):
    args = dict(locals())
    wts = {n: args[n] for n in ORDER}
    mom = {n: args["m_" + n] for n in ORDER}
    var = {n: args["v_" + n] for n in ORDER}

    x0 = x[0]
    target = loss_target[0]
    t, d = x0.shape
    n_hgrn = d // 2 // HEAD
    n_attn = (d - d // 2) // HEAD
    core = lax.axis_index("c").astype(jnp.int32).reshape(1)
    chip = (2 * lax.axis_index("x") + lax.axis_index("y")).astype(jnp.int32).reshape(1)

    full = {n: _all_gather("ag_" + n, wts[n][0].astype(BF16), AG_ID[n]) for n in BIG}
    w_gu1, w_gu2 = full["w_ffn1_gate_up"], full["w_ffn2_gate_up"]
    w_d1 = full["w_ffn1_down"].reshape(-1, d)
    w_d2 = full["w_ffn2_down"].reshape(-1, d)
    w_out = full["w_mix_out"].reshape(-1, d)
    w_in = full["w_mix_in"]

    g = {n: wts[n] for n in SMALL}
    lb_f = jax.nn.softmax(g["hgrn_lower_bounds_fwd"], axis=0)[0:1]
    lb_b = jax.nn.softmax(g["hgrn_lower_bounds_bwd"], axis=0)[0:1]
    bucket_idx = jnp.asarray(_t5_bucket_index())
    bias = _bias_build(g["rel_bias_table"], bucket_idx)

    n1 = _pre_norm("pre_norm1", x0, g["pre_norm_ffn1"])
    a1, gu1 = _ffn_up("ffn1_gate_up", n1, w_gu1)
    ff1 = _matmul("ffn1_down", a1, w_d1, mode="nn", out_dtype=F32)
    x1, h = _post_res_pre("res1", x0, ff1, g["post_norm_ffn1"], g["pre_norm_mix"], 0.5)
    p = _matmul("mix_in", h, w_in, mode="nn", stack=True, out_dtype=F32)
    y_h, o_raw = _hgrn_fwd(p, lb_f, lb_b, g["hgrn_out_norm"], n_hgrn)
    y_a = _attn_fwd(p, bias, g["attn_sink"], n_hgrn, n_attn)
    cat = jnp.concatenate([y_h, y_a], axis=1)
    mixed = _matmul("mix_out", cat, w_out, mode="nn", out_dtype=F32)
    x2, n2 = _post_res_pre("res2", x1, mixed, g["post_norm_mix"], g["pre_norm_ffn2"], 1.0)
    a2, gu2 = _ffn_up("ffn2_gate_up", n2, w_gu2)
    ff2 = _matmul("ffn2_down", a2, w_d2, mode="nn", out_dtype=F32)
    dy3, loss_part = _post_res_loss("res3_loss", x2, ff2, g["post_norm_ffn2"], target, 0.5)

    small_grad = {}
    scattered = {}

    pending = []

    def scatter(name, grad_stack):
        started = _pair_start("rs_pair_" + name, grad_stack)
        pending.append((name, started))
        return [started[4]]

    def settle(after):
        deps = []
        while pending:
            name, started = pending.pop(0)
            scattered[name] = _reduce_scatter(name, started, after, core, RS_ID[name])
            deps.append(scattered[name][0])
        return deps

    def ffn_bwd(tag, dy, ff, a, gu, n_in, x_in, w_gu, w_d, post_name, pre_name, gu_name, d_name):
        dff, small_grad[post_name] = _post_bwd("post_bwd" + tag, dy, ff, g[post_name], 0.5)
        dep = settle(dff)
        dep = scatter(d_name, _matmul("dw_down" + tag, a, dff, mode="tn", out_dtype=BF16,
                                      deps=dep).reshape(N_DEV, -1, d))
        dgu = _ffn_dact("d_act" + tag, dff, w_d, gu, deps=dep)
        dep = settle(dgu)
        dep = scatter(gu_name, _matmul("dw_gate_up" + tag, n_in, dgu, mode="tn", stack=True, halves=True,
                                       out_dtype=BF16, deps=dep))
        dn = _matmul("d_norm" + tag, dgu, w_gu, mode="nt", stack=True, halves=True, out_dtype=F32, deps=dep)
        dep = settle(dn)
        dx, small_grad[pre_name] = _pre_bwd("pre_bwd" + tag, dn, x_in, g[pre_name], dy, deps=dep)
        return dx

    dx2 = ffn_bwd("2", dy3, ff2, a2, gu2, n2, x2, w_gu2, w_d2, "post_norm_ffn2", "pre_norm_ffn2",
                  "w_ffn2_gate_up", "w_ffn2_down")

    dmixed, small_grad["post_norm_mix"] = _post_bwd("post_bwd_mix", dx2, mixed, g["post_norm_mix"], 1.0)
    dep = settle(dmixed)
    dcat = _matmul("d_cat", dmixed, w_out, mode="nt", out_dtype=F32, deps=dep)
    dep = scatter("w_mix_out", _matmul("dw_mix_out", cat, dmixed, mode="tn", out_dtype=BF16).reshape(N_DEV, -1, d))
    dq_a, dk_a, dv_a, dbias, dsink_rows = _attn_bwd(p, dcat, bias, g["attn_sink"], n_hgrn, n_attn, deps=dep)
    dq_h, di_h, dzf, dzb, dg_h, dlb_f, dlb_b, small_grad["hgrn_out_norm"] = _hgrn_bwd(
        p, o_raw, dcat, lb_f, lb_b, g["hgrn_out_norm"], n_hgrn)
    dp = jnp.concatenate([dq_h, di_h, dzf, dzb, dg_h, dq_a, dk_a, dv_a], axis=1)
    dep = settle(dp)
    dh = _matmul("d_h", dp, w_in, mode="nt", stack=True, out_dtype=F32, deps=dep)
    dep = scatter("w_mix_in", _matmul("dw_mix_in", h, dp, mode="tn", stack=True, out_dtype=BF16))
    dx1, small_grad["pre_norm_mix"] = _pre_bwd("pre_bwd_mix", dh, x1, g["pre_norm_mix"], dx2, deps=dep)

    dx0 = ffn_bwd("1", dx1, ff1, a1, gu1, n1, x0, w_gu1, w_d1, "post_norm_ffn1", "pre_norm_ffn1",
                  "w_ffn1_gate_up", "w_ffn1_down")
    settle(dx0)

    def lb_grad(dlb, lb):
        da0 = dlb * lb * (1.0 - lb)
        return jnp.concatenate([da0, -da0], axis=0)

    small_grad["hgrn_lower_bounds_fwd"] = lb_grad(dlb_f, lb_f)
    small_grad["hgrn_lower_bounds_bwd"] = lb_grad(dlb_b, lb_b)
    small_grad["attn_sink"] = dsink_rows[:, :, 0].reshape(1, n_attn)
    small_grad["rel_bias_table"] = jnp.transpose(_bias_reduce(dbias, bucket_idx)[:, :, 0])

    parts = [small_grad[n] for n in SMALL] + [loss_part[:, 0:1]]
    red = _small_all_reduce("small_all_reduce", _pack(parts))
    red_list = _unpack(red, parts)
    loss = red_list[-1].reshape(())
    sg = dict(zip(SMALL, red_list[:-1]))
    like = [wts[n] for n in SMALL]
    d_s, m_s, v_s = _adam_small("adam_small", _pack(like), _pack([sg[n] for n in SMALL]),
                                _pack([mom[n] for n in SMALL]), _pack([var[n] for n in SMALL]))
    grads = dict(sg)
    delta = dict(zip(SMALL, _unpack(d_s, like)))
    new_m = dict(zip(SMALL, _unpack(m_s, like)))
    new_v = dict(zip(SMALL, _unpack(v_s, like)))

    dep = []
    for n in ("w_ffn2_down", "w_ffn2_gate_up", "w_mix_out", "w_mix_in", "w_ffn1_down", "w_ffn1_gate_up"):
        sums, recv = scattered[n]
        gr, de, nm, nv = _adam_shard("adam_" + n, wts[n][0], mom[n][0], var[n][0], sums, recv, chip, deps=dep)
        grads[n], delta[n], new_m[n], new_v[n] = gr[None], de[None], nm[None], nv[None]
        dep = [gr]

    return (loss, dx0[None], *[grads[n] for n in ORDER], *[delta[n] for n in ORDER],
            *[new_m[n] for n in ORDER], *[new_v[n] for n in ORDER])
```

```python
import functools
import math

import numpy as np
import jax
import jax.numpy as jnp
from jax import lax
from jax.experimental import pallas as pl
from jax.experimental.pallas import tpu as pltpu
from jax.experimental.pallas import tpu_sc as plsc

F32 = jnp.float32
BF16 = jnp.bfloat16
MESH = pl.DeviceIdType.MESH

N_DEV = 8
EPS = 1e-6
NEG_INF = -1e30
HEAD = 128
CHUNK = 64
WINDOW = 128
KEY_SPAN = 3 * WINDOW
KV_HEADS = 2
REL_BUCKETS = 32
REL_MAX_DIST = 128
ADAM_LR, ADAM_B1, ADAM_B2, ADAM_EPS, ADAM_WD, ADAM_STEP = 0.001, 0.9, 0.999, 1e-08, 0.01, 10
LANES = 128
VMEM_LIMIT = 56 * 1024 * 1024
ANY = pl.BlockSpec(memory_space=pl.ANY)


def _cparams(*sem):
    return pltpu.CompilerParams(dimension_semantics=sem if sem else None, vmem_limit_bytes=VMEM_LIMIT)


def _dot(a, b):
    return jnp.dot(a, b, preferred_element_type=F32)


def _dot_nt(a, b):
    return lax.dot_general(a, b, (((1,), (1,)), ((), ())), preferred_element_type=F32)


def _dot_tn(a, b):
    return lax.dot_general(a, b, (((0,), (0,)), ((), ())), preferred_element_type=F32)


def _tile(dim, target):
    for c in (target, 1024, 512, 256, 128):
        if c <= target and dim % c == 0:
            return c
    return dim


def _row_tile(rows, target):
    fits = [c for c in range(16, min(rows, target) + 1, 16) if rows % c == 0]
    return max(fits) if fits else rows


K_WHOLE = 2048
K_STEP = 2816


def _k_tile(kd):
    if kd <= K_WHOLE:
        return kd
    return max(c for c in range(LANES, K_STEP + 1, LANES) if kd % c == 0)


def _matmul(name, a, b, *, mode, out_dtype, stack=False, halves=False, tm=1024, tn=1024, deps=()):
    grp = 1
    if mode == "nn":
        m, kd = a.shape
        n = b.shape[0] * b.shape[2] if stack else b.shape[1]
    elif mode == "nt":
        m = a.shape[-2]
        n, kd = (b.shape[1], b.shape[0] * b.shape[2]) if stack else b.shape
    else:
        kd, m = a.shape
        n = b.shape[-1] * (2 if halves else 1)
    if stack:
        n1 = b.shape[2] if mode != "tn" else n // N_DEV
        if mode == "nt":
            grp = 2 if 2 * n1 <= K_STEP else 1
            tk = grp * n1
        else:
            grp = 1 if n1 % LANES == 0 else 2
            tn = grp * n1
        assert (grp * n1) % LANES == 0
    per_half = N_DEV // 2 // grp
    tm = _tile(m, tm)
    if not (stack and mode in ("nn", "tn")):
        tn = _tile(n, tn)
    if not (stack and mode == "nt"):
        tk = _k_tile(kd)
    nk = kd // tk
    lead = None if grp == 1 else grp
    b_outer = nk == 1 and b.size > a.size
    grid = (n // tn, m // tm, nk) if b_outer else (m // tm, n // tn, nk)

    def spec(shape, index):
        return pl.BlockSpec(shape, (lambda g0, g1, k: index(g1, g0, k)) if b_outer else index)

    if mode == "nn":
        a_spec = spec((tm, tk), lambda i, j, k: (i, k))
        if stack:
            b_spec = spec((lead, tk, n1), lambda i, j, k: (j, k, 0))
        else:
            b_spec = spec((tk, tn), lambda i, j, k: (k, j))
        dot = _dot
    elif mode == "nt":
        if halves:
            a_spec = spec((None, tm, tk), lambda i, j, k: (k // per_half, i, k % per_half))
        else:
            a_spec = spec((tm, tk), lambda i, j, k: (i, k))
        if stack:
            b_spec = spec((lead, tn, n1), lambda i, j, k: (k, j, 0))
        else:
            b_spec = spec((tn, tk), lambda i, j, k: (j, k))
        dot = _dot_nt
    else:
        a_spec = spec((tk, tm), lambda i, j, k: (k, i))
        if halves:
            b_spec = spec((None, tk, tn), lambda i, j, k: (j // per_half, k, j % per_half))
        else:
            b_spec = spec((tk, tn), lambda i, j, k: (k, j))
        dot = _dot_tn
    if stack and mode == "tn":
        out_shape = jax.ShapeDtypeStruct((N_DEV, m, n1), out_dtype)
        o_spec = spec((lead, tm, n1), lambda i, j, k: (j, i, 0))
    else:
        out_shape = jax.ShapeDtypeStruct((m, n), out_dtype)
        o_spec = spec((tm, tn), lambda i, j, k: (i, j))
    b_grouped = stack and grp > 1 and mode != "tn"
    o_grouped = stack and grp > 1 and mode == "tn"

    def product(a_ref, b_ref):
        bmat = jnp.concatenate([b_ref[s] for s in range(grp)], axis=1) if b_grouped else b_ref[...]
        return dot(a_ref[...], bmat)

    def store(o_ref, val):
        if o_grouped:
            for s in range(grp):
                o_ref[s] = val[:, s * n1:(s + 1) * n1].astype(o_ref.dtype)
        else:
            o_ref[...] = val.astype(o_ref.dtype)

    def body_whole(a_ref, b_ref, *rest):
        store(rest[-1], product(a_ref, b_ref))

    def body_steps(a_ref, b_ref, *rest):
        o_ref, acc_ref = rest[-2:]
        k = pl.program_id(2)

        @pl.when(k == 0)
        def _():
            acc_ref[...] = product(a_ref, b_ref)

        @pl.when(k > 0)
        def _():
            acc_ref[...] += product(a_ref, b_ref)

        @pl.when(k == nk - 1)
        def _():
            store(o_ref, acc_ref[...])

    return pl.pallas_call(
        body_whole if nk == 1 else body_steps, name=name, grid=grid,
        in_specs=[a_spec, b_spec] + [ANY] * len(deps), out_specs=o_spec, out_shape=out_shape,
        scratch_shapes=[] if nk == 1 else [pltpu.VMEM((tm, tn), F32)],
        compiler_params=_cparams("parallel", "parallel", "arbitrary"),
    )(a, b, *deps)


def _col_parts(width, parts=2):
    groups = width // LANES
    parts = max(1, min(parts, groups // 2))
    bounds = [LANES * (groups * p // parts) for p in range(parts)] + [width]
    return [slice(bounds[p], bounds[p + 1]) for p in range(parts)]


def _ffn_up(name, n, w_stack):
    t, d = n.shape
    s, _, n1 = w_stack.shape
    half = s // 2
    tm = _tile(t, 512)

    def body(n_ref, wg_ref, wu_ref, act_ref, gu_ref):
        nv = n_ref[...]
        for cols in _col_parts(n1):
            gate = _dot(nv, wg_ref[:, cols])
            up = _dot(nv, wu_ref[:, cols])
            sg = jax.nn.sigmoid(gate)
            silu = gate * sg
            act_ref[:, cols] = (silu * up).astype(BF16)
            gu_ref[0, :, cols] = (up * (sg * (1.0 + gate * (1.0 - sg)))).astype(BF16)
            gu_ref[1, :, cols] = silu.astype(BF16)

    return pl.pallas_call(
        body, name=name, grid=(half, t // tm),
        in_specs=[pl.BlockSpec((tm, d), lambda j, i: (i, 0)),
                  pl.BlockSpec((None, d, n1), lambda j, i: (j, 0, 0)),
                  pl.BlockSpec((None, d, n1), lambda j, i: (half + j, 0, 0))],
        out_specs=[pl.BlockSpec((tm, n1), lambda j, i: (i, j)), pl.BlockSpec((2, tm, n1), lambda j, i: (0, i, j))],
        out_shape=[jax.ShapeDtypeStruct((t, half * n1), BF16), jax.ShapeDtypeStruct((2, t, half * n1), BF16)],
        compiler_params=_cparams("parallel", "parallel"),
    )(n, w_stack, w_stack)


def _ffn_dact(name, dff, w_d, gu, deps=()):
    t, d = dff.shape
    f = w_d.shape[0]
    tm = _tile(t, 512)
    tn = _tile(f, 1408)

    def body(dff_ref, w_ref, gu_ref, *rest):
        dgu_ref = rest[-1]
        da = _dot_nt(dff_ref[...], w_ref[...]).astype(BF16)
        dgu_ref[0] = da * gu_ref[0]
        dgu_ref[1] = da * gu_ref[1]

    pair = pl.BlockSpec((2, tm, tn), lambda j, i: (0, i, j))
    return pl.pallas_call(
        body, name=name, grid=(f // tn, t // tm),
        in_specs=[pl.BlockSpec((tm, d), lambda j, i: (i, 0)), pl.BlockSpec((tn, d), lambda j, i: (j, 0)), pair]
        + [ANY] * len(deps),
        out_specs=pair, out_shape=jax.ShapeDtypeStruct((2, t, f), BF16),
        compiler_params=_cparams("parallel", "parallel"),
    )(dff, w_d, gu, *deps)


ROWS = 256


def _rstd(xf):
    return lax.rsqrt(jnp.mean(xf * xf, axis=-1, keepdims=True) + EPS)


def _row_spec(t, d):
    return pl.BlockSpec((min(ROWS, t), d), lambda i: (i, 0))


def _vec_spec(d):
    return pl.BlockSpec((1, d), lambda i: (0, 0))


def _pre_norm(name, x, gain):
    t, d = x.shape

    def body(x_ref, g_ref, n_ref):
        xf = x_ref[...]
        n_ref[...] = (xf * _rstd(xf) * g_ref[...]).astype(BF16)

    return pl.pallas_call(
        body, name=name, grid=(t // min(ROWS, t),), in_specs=[_row_spec(t, d), _vec_spec(d)],
        out_specs=_row_spec(t, d), out_shape=jax.ShapeDtypeStruct((t, d), BF16),
        compiler_params=_cparams("parallel"),
    )(x, gain)


def _post_res_pre(name, x, ff, g_post, g_next, scale):
    t, d = x.shape

    def body(x_ref, ff_ref, gp_ref, gn_ref, xo_ref, n_ref):
        ff_ = ff_ref[...]
        xn = x_ref[...] + scale * (ff_ * _rstd(ff_) * gp_ref[...])
        xo_ref[...] = xn
        n_ref[...] = (xn * _rstd(xn) * gn_ref[...]).astype(BF16)

    return pl.pallas_call(
        body, name=name, grid=(t // min(ROWS, t),),
        in_specs=[_row_spec(t, d), _row_spec(t, d), _vec_spec(d), _vec_spec(d)],
        out_specs=[_row_spec(t, d), _row_spec(t, d)],
        out_shape=[jax.ShapeDtypeStruct((t, d), F32), jax.ShapeDtypeStruct((t, d), BF16)],
        compiler_params=_cparams("parallel"),
    )(x, ff, g_post, g_next)


def _post_res_loss(name, x, ff, g_post, target, scale):
    t, d = x.shape

    def body(x_ref, ff_ref, gp_ref, tg_ref, dy_ref, loss_ref):
        ff_ = ff_ref[...]
        err = x_ref[...] + scale * (ff_ * _rstd(ff_) * gp_ref[...]) - tg_ref[...]
        dy_ref[...] = err / d
        part = 0.5 * jnp.sum(jnp.mean(err * err, axis=-1, keepdims=True), axis=0, keepdims=True)

        @pl.when(pl.program_id(0) == 0)
        def _():
            loss_ref[...] = jnp.zeros_like(loss_ref)

        loss_ref[...] += jnp.broadcast_to(part, loss_ref.shape)

    return pl.pallas_call(
        body, name=name, grid=(t // min(ROWS, t),),
        in_specs=[_row_spec(t, d), _row_spec(t, d), _vec_spec(d), _row_spec(t, d)],
        out_specs=[_row_spec(t, d), _vec_spec(LANES)],
        out_shape=[jax.ShapeDtypeStruct((t, d), F32), jax.ShapeDtypeStruct((1, LANES), F32)],
        compiler_params=_cparams("arbitrary"),
    )(x, ff, g_post, target)


def _post_bwd(name, dy, ff, g_post, scale):
    t, d = dy.shape

    def body(dy_ref, ff_ref, gp_ref, dff_ref, dg_ref):
        ff_ = ff_ref[...]
        r = _rstd(ff_)
        xh = ff_ * r
        dyn = scale * dy_ref[...]
        dxh = dyn * gp_ref[...]
        dff_ref[...] = (r * (dxh - xh * jnp.mean(dxh * xh, axis=-1, keepdims=True))).astype(BF16)

        @pl.when(pl.program_id(0) == 0)
        def _():
            dg_ref[...] = jnp.zeros_like(dg_ref)

        dg_ref[...] += jnp.sum(dyn * xh, axis=0, keepdims=True)

    return pl.pallas_call(
        body, name=name, grid=(t // min(ROWS, t),),
        in_specs=[_row_spec(t, d), _row_spec(t, d), _vec_spec(d)],
        out_specs=[_row_spec(t, d), _vec_spec(d)],
        out_shape=[jax.ShapeDtypeStruct((t, d), BF16), jax.ShapeDtypeStruct((1, d), F32)],
        compiler_params=_cparams("arbitrary"),
    )(dy, ff, g_post)


def _pre_bwd(name, dn, x, g_pre, dy, deps=()):
    t, d = x.shape

    def body(dn_ref, x_ref, g_ref, dy_ref, *rest):
        dx_ref, dg_ref = rest[-2:]
        xf = x_ref[...]
        r = _rstd(xf)
        xh = xf * r
        dnf = dn_ref[...].astype(F32)
        dxh = dnf * g_ref[...]
        dx_ref[...] = dy_ref[...] + r * (dxh - xh * jnp.mean(dxh * xh, axis=-1, keepdims=True))

        @pl.when(pl.program_id(0) == 0)
        def _():
            dg_ref[...] = jnp.zeros_like(dg_ref)

        dg_ref[...] += jnp.sum(dnf * xh, axis=0, keepdims=True)

    return pl.pallas_call(
        body, name=name, grid=(t // min(ROWS, t),),
        in_specs=[_row_spec(t, d), _row_spec(t, d), _vec_spec(d), _row_spec(t, d)] + [ANY] * len(deps),
        out_specs=[_row_spec(t, d), _vec_spec(d)],
        out_shape=[jax.ShapeDtypeStruct((t, d), F32), jax.ShapeDtypeStruct((1, d), F32)],
        compiler_params=_cparams("arbitrary"),
    )(dn, x, g_pre, dy, *deps)


def _bdot(a, b, ca, cb, precision=None):
    return lax.dot_general(a, b, (((ca,), (cb,)), ((0,), (0,))), preferred_element_type=F32, precision=precision)


def _tri_masks(g):
    row = lax.broadcasted_iota(jnp.int32, (g, CHUNK, CHUNK), 1)
    col = lax.broadcasted_iota(jnp.int32, (g, CHUNK, CHUNK), 2)
    return col <= row, col >= row


def _ones_matmul(ones_mat, val):
    hi = val.astype(BF16)
    lo = (val - hi.astype(F32)).astype(BF16)
    return _bdot(ones_mat, hi, 2, 1) + _bdot(ones_mat, lo, 2, 1)


def _hgrn_block(z, lb, q, v, cum_mat):
    sg = jax.nn.sigmoid(z)
    f = lb + (1.0 - lb) * sg
    lf = jnp.log(f)
    k = 1.0 - f
    a = _ones_matmul(cum_mat, lf)
    last = jnp.sum(lf, axis=1, keepdims=True)
    e_a = jnp.exp(a)
    e_na = jnp.exp(-a)
    e_t = jnp.exp(last - a)
    return dict(sg=sg, f=f, k=k, decay=jnp.exp(last), e_a=e_a, e_na=e_na, e_t=e_t,
                qd=q * e_a, kd=k * e_na, kt=k * e_t)


def _hgrn_states(state, kv, decay, order):
    entering = [None] * len(order)
    for g in order:
        entering[g] = state
        state = decay[g] * state + kv[g]
    return jnp.stack(entering, axis=0), state


def _hgrn_fwd(p, lb_f, lb_b, gain, n_heads):
    t = p.shape[0]
    w = n_heads * HEAD
    blk = min(8, t // CHUNK)
    rows_blk = blk * CHUNK
    n_blocks = t // rows_blk
    fin_rows = min(256, t)

    def body(q_ref, i_ref, zf_ref, zb_ref, g_ref, lbf_ref, lbb_ref, gain_ref, y_ref, o_ref, st_ref):
        low, up = _tri_masks(blk)
        m_low, m_up = low.astype(BF16), up.astype(BF16)
        o_ref[...] = jnp.zeros_like(o_ref)
        st_ref[...] = jnp.zeros_like(st_ref)

        def one(r0, z_ref, lb, slot, rev):
            rows = pl.ds(r0, rows_blk)
            split = lambda ref: ref[rows, :].reshape(blk, CHUNK, HEAD)
            q, v = split(q_ref), split(i_ref)
            c = _hgrn_block(split(z_ref), lb, q, v, m_up if rev else m_low)
            qd, kd, kt, vb = c["qd"].astype(BF16), c["kd"].astype(BF16), c["kt"].astype(BF16), v.astype(BF16)
            pm = jnp.where(up if rev else low, _bdot(qd, kd, 2, 2), 0.0).astype(BF16)
            kv = _bdot(vb, kt, 1, 1)
            order = range(blk - 1, -1, -1) if rev else range(blk)
            entering, st_ref[slot] = _hgrn_states(st_ref[slot], kv, c["decay"], order)
            o = _bdot(pm, vb, 2, 1) + _bdot(qd, entering.astype(BF16), 2, 2)
            o_ref[rows, :] += o.reshape(rows_blk, HEAD)

        def step(n, carry):
            one(pl.multiple_of(n * rows_blk, rows_blk), zf_ref, lbf_ref[...], 0, False)
            one(pl.multiple_of((n_blocks - 1 - n) * rows_blk, rows_blk), zb_ref, lbb_ref[...], 1, True)
            return carry

        lax.fori_loop(0, n_blocks, step, 0)

        def fin(n, carry):
            rows = pl.ds(pl.multiple_of(n * fin_rows, fin_rows), fin_rows)
            o = o_ref[rows, :]
            g = g_ref[rows, :]
            y_ref[rows, :] = (o * _rstd(o) * gain_ref[...] * (g * jax.nn.sigmoid(g))).astype(BF16)
            return carry

        lax.fori_loop(0, t // fin_rows, fin, 0)

    col = lambda grp: pl.BlockSpec((t, HEAD), lambda h: (0, grp * n_heads + h))
    vec = pl.BlockSpec((1, HEAD), lambda h: (0, h))
    out = pl.BlockSpec((t, HEAD), lambda h: (0, h))
    return pl.pallas_call(
        body, name="hgrn_fwd", grid=(n_heads,),
        in_specs=[col(0), col(1), col(2), col(3), col(4), vec, vec, vec],
        out_specs=[out, out],
        out_shape=[jax.ShapeDtypeStruct((t, w), BF16), jax.ShapeDtypeStruct((t, w), F32)],
        scratch_shapes=[pltpu.VMEM((2, HEAD, HEAD), F32)],
        compiler_params=_cparams("parallel"),
    )(p, p, p, p, p, lb_f, lb_b, gain)


def _hgrn_bwd(p, o_raw, dcat, lb_f, lb_b, gain, n_heads):
    t = p.shape[0]
    w = n_heads * HEAD
    n_chunks = t // CHUNK
    blk = min(8, n_chunks)
    rows_blk = blk * CHUNK
    n_blocks = t // rows_blk
    rb = min(256, t)

    def body(q_ref, i_ref, zf_ref, zb_ref, g_ref, o_ref, dy_ref, lbf_ref, lbb_ref, gain_ref,
             dq_ref, di_ref, dzf_ref, dzb_ref, dg_ref, dlbf_ref, dlbb_ref, dgain_ref,
             do_s, dq_s, dv_s, st_s, cur_s):
        low, up = _tri_masks(blk)
        m_low, m_up = low.astype(BF16), up.astype(BF16)
        rowid = lax.broadcasted_iota(jnp.int32, (blk, CHUNK, HEAD), 1)
        gain_v = gain_ref[...]

        def norm_bwd(n, dgain):
            rows = pl.ds(pl.multiple_of(n * rb, rb), rb)
            o = o_ref[rows, :]
            g = g_ref[rows, :]
            dy = dy_ref[rows, :]
            r = _rstd(o)
            oh = o * r
            sg = jax.nn.sigmoid(g)
            dg_ref[rows, :] = (dy * oh * gain_v * (sg * (1.0 + g * (1.0 - sg)))).astype(BF16)
            dno = dy * (g * sg)
            dxh = dno * gain_v
            do_s[rows, :] = r * (dxh - oh * jnp.mean(dxh * oh, axis=-1, keepdims=True))
            return dgain + jnp.sum(dno * oh, axis=0, keepdims=True)

        dgain_ref[...] = lax.fori_loop(0, t // rb, norm_bwd, jnp.zeros((1, HEAD), F32))
        dq_s[...] = jnp.zeros_like(dq_s)
        dv_s[...] = jnp.zeros_like(dv_s)

        def direction(z_ref, lb_ref, dz_ref, dlb_ref, rev):
            lb = lb_ref[...]
            cum_mat = m_up if rev else m_low
            cum_mat_t = m_low if rev else m_up
            mask = up if rev else low
            last_row = 0 if rev else CHUNK - 1

            order = range(blk - 1, -1, -1) if rev else range(blk)

            def rows_of(j):
                bidx = (n_blocks - 1 - j) if rev else j
                return bidx, pl.ds(pl.multiple_of(bidx * rows_blk, rows_blk), rows_blk)

            def load(rows):
                split = lambda ref: ref[rows, :].reshape(blk, CHUNK, HEAD)
                q, v = split(q_ref), split(i_ref)
                return q, v, _hgrn_block(split(z_ref), lb, q, v, cum_mat)

            def sweep_fwd(j, carry):
                bidx, rows = rows_of(j)
                _, v, c = load(rows)
                kv = _bdot(v.astype(BF16), c["kt"].astype(BF16), 1, 1)
                st_s[pl.ds(bidx * blk, blk)], cur_s[0] = _hgrn_states(cur_s[0], kv, c["decay"], order)
                return carry

            cur_s[...] = jnp.zeros_like(cur_s)
            dlb_ref[...] = jnp.zeros_like(dlb_ref)
            lax.fori_loop(0, n_blocks, sweep_fwd, 0)

            def sweep_bwd(jj, carry):
                bidx, rows = rows_of(n_blocks - 1 - jj)
                _, v, c = load(rows)
                st = st_s[pl.ds(bidx * blk, blk)]
                do = do_s[rows, :].reshape(blk, CHUNK, HEAD)
                qd, kd, kt, decay = c["qd"], c["kd"], c["kt"], c["decay"]
                qd_b, kd_b, kt_b = qd.astype(BF16), kd.astype(BF16), kt.astype(BF16)
                v_b, do_b, st_b = v.astype(BF16), do.astype(BF16), st.astype(BF16)
                pm = jnp.where(mask, _bdot(qd_b, kd_b, 2, 2), 0.0).astype(BF16)
                dpm = jnp.where(mask, _bdot(do_b, v_b, 2, 2), 0.0).astype(BF16)
                gq = _bdot(do_b, qd_b, 1, 1)
                dstate = cur_s[1]
                after = [None] * blk
                for g in reversed(order):
                    after[g] = dstate
                    dstate = gq[g] + decay[g] * dstate
                cur_s[1] = dstate
                dst = jnp.stack(after, axis=0)
                dst_b = dst.astype(BF16)
                dv = _bdot(pm, do_b, 1, 1) + _bdot(kt_b, dst_b, 2, 2)
                dqd = _bdot(dpm, kd_b, 2, 1) + _bdot(do_b, st_b, 2, 1)
                dkd = _bdot(dpm, qd_b, 1, 1)
                dkt = _bdot(v_b, dst_b, 2, 1)
                dlast = (jnp.sum(dkt * kt, axis=1, keepdims=True)
                         + decay * jnp.sum(dst * st, axis=1, keepdims=True))
                dq_s[rows, :] += (dqd * c["e_a"]).reshape(rows_blk, HEAD)
                dv_s[rows, :] += dv.reshape(rows_blk, HEAD)
                dk = dkd * c["e_na"] + dkt * c["e_t"]
                da = dqd * qd - dkd * kd - dkt * kt
                da = da + jnp.where(rowid == last_row, dlast, 0.0)
                dlf = _ones_matmul(cum_mat_t, da)
                df = dlf / c["f"] - dk
                sg = c["sg"]
                dz_ref[rows, :] = (df * (1.0 - lb) * (sg * (1.0 - sg))).reshape(rows_blk, HEAD).astype(BF16)
                dlb_ref[...] += jnp.sum((df * (1.0 - sg)).reshape(rows_blk, HEAD), axis=0, keepdims=True)
                return carry

            lax.fori_loop(0, n_blocks, sweep_bwd, 0)

        direction(zf_ref, lbf_ref, dzf_ref, dlbf_ref, False)
        direction(zb_ref, lbb_ref, dzb_ref, dlbb_ref, True)
        dq_ref[...] = dq_s[...].astype(BF16)
        di_ref[...] = dv_s[...].astype(BF16)

    col = lambda grp: pl.BlockSpec((t, HEAD), lambda h: (0, grp * n_heads + h))
    one = pl.BlockSpec((t, HEAD), lambda h: (0, h))
    vec = pl.BlockSpec((1, HEAD), lambda h: (0, h))
    big = jax.ShapeDtypeStruct((t, w), BF16)
    small = jax.ShapeDtypeStruct((1, w), F32)
    return pl.pallas_call(
        body, name="hgrn_bwd", grid=(n_heads,),
        in_specs=[col(0), col(1), col(2), col(3), col(4), one, one, vec, vec, vec],
        out_specs=[one] * 5 + [vec] * 3,
        out_shape=[big] * 5 + [small] * 3,
        scratch_shapes=[pltpu.VMEM((t, HEAD), F32), pltpu.VMEM((t, HEAD), F32), pltpu.VMEM((t, HEAD), F32),
                        pltpu.VMEM((n_chunks, HEAD, HEAD), F32), pltpu.VMEM((2, HEAD, HEAD), F32)],
        compiler_params=_cparams("parallel"),
    )(p, p, p, p, p, o_raw, dcat, lb_f, lb_b, gain)


def _t5_bucket_index():
    c = np.arange(WINDOW)[:, None]
    s = np.arange(KEY_SPAN)[None, :]
    rel = s - WINDOW - c
    nb = REL_BUCKETS // 2
    max_exact = nb // 2
    bucket = (rel > 0).astype(np.int32) * nb
    n = np.abs(rel)
    large = max_exact + (np.log(np.maximum(n, 1) / max_exact) / np.log(REL_MAX_DIST / max_exact)
                         * (nb - max_exact)).astype(np.int32)
    large = np.minimum(large, nb - 1)
    return bucket + np.where(n < max_exact, n, large).astype(np.int32)


def _bias_build(table, idx):
    n_attn = table.shape[1]

    def body(tab_ref, idx_ref, o_ref):
        h = pl.program_id(0)
        idx_v = idx_ref[...]
        acc = jnp.zeros((WINDOW, KEY_SPAN), F32)
        for b in range(REL_BUCKETS):
            acc = jnp.where(idx_v == b, tab_ref[b, h], acc)
        o_ref[...] = acc

    return pl.pallas_call(
        body, name="bias_build", grid=(n_attn,),
        in_specs=[pl.BlockSpec(memory_space=pltpu.SMEM), pl.BlockSpec((WINDOW, KEY_SPAN), lambda h: (0, 0))],
        out_specs=pl.BlockSpec((None, WINDOW, KEY_SPAN), lambda h: (h, 0, 0)),
        out_shape=jax.ShapeDtypeStruct((n_attn, WINDOW, KEY_SPAN), F32), compiler_params=_cparams("parallel"),
    )(table, idx)


def _bias_reduce(dbias, idx):
    n_attn = dbias.shape[0]

    def body(idx_ref, d_ref, o_ref):
        idx_v = idx_ref[...]
        dv = d_ref[...]
        rows = lax.broadcasted_iota(jnp.int32, (REL_BUCKETS, LANES), 0)
        acc = jnp.zeros((REL_BUCKETS, LANES), F32)
        for b in range(REL_BUCKETS):
            part = jnp.sum(jnp.where(idx_v == b, dv, 0.0), axis=1, keepdims=True)
            acc = jnp.where(rows == b, jnp.sum(part, axis=0, keepdims=True), acc)
        o_ref[...] = acc

    return pl.pallas_call(
        body, name="bias_reduce", grid=(n_attn,),
        in_specs=[pl.BlockSpec((WINDOW, KEY_SPAN), lambda h: (0, 0)),
                  pl.BlockSpec((None, WINDOW, KEY_SPAN), lambda h: (h, 0, 0))],
        out_specs=pl.BlockSpec((None, REL_BUCKETS, LANES), lambda h: (h, 0, 0)),
        out_shape=jax.ShapeDtypeStruct((n_attn, REL_BUCKETS, LANES), F32), compiler_params=_cparams("parallel"),
    )(idx, dbias)


def _attn_probs(q, kb, bias, sink, valid):
    s = _dot_nt(q, kb) / math.sqrt(HEAD) + bias
    s = jnp.where(valid, s, NEG_INF)
    m = jnp.maximum(jnp.max(s, axis=-1, keepdims=True), sink)
    e = jnp.exp(s - m)
    e_sink = jnp.exp(sink - m)
    den = jnp.sum(e, axis=-1, keepdims=True) + e_sink
    return e / den, e_sink / den


def _attn_valid(n, t, grp):
    c = lax.broadcasted_iota(jnp.int32, (grp * WINDOW, KEY_SPAN), 0) & (WINDOW - 1)
    s = lax.broadcasted_iota(jnp.int32, (grp * WINDOW, KEY_SPAN), 1)
    rel = s - WINDOW - c
    key_pos = n * WINDOW - WINDOW + s
    return (jnp.abs(rel) <= WINDOW) & (key_pos >= 0) & (key_pos < t)


def _stack_heads(ref, grp):
    return jnp.concatenate([ref[:, g * HEAD:(g + 1) * HEAD] for g in range(grp)], axis=0).astype(BF16)


def _sink_column(sink_ref, x, grp):
    return jnp.concatenate([jnp.full((WINDOW, 1), sink_ref[0, x * grp + g], F32) for g in range(grp)], axis=0)


def _attn_specs(t, n_hgrn, n_attn):
    grp = n_attn // KV_HEADS
    nb = t // WINDOW
    cq = 5 * n_hgrn
    ck = cq + n_attn
    cv = ck + KV_HEADS
    q_spec = pl.BlockSpec((WINDOW, grp * HEAD), lambda x, n: (n, cq // grp + x))
    kv = lambda base, off: pl.BlockSpec(
        (WINDOW, HEAD), lambda x, n: (jnp.clip(n + off, 0, nb - 1), base + x))
    band = [kv(ck, -1), kv(ck, 0), kv(ck, 1), kv(cv, -1), kv(cv, 0), kv(cv, 1)]
    bias_spec = pl.BlockSpec((grp, WINDOW, KEY_SPAN), lambda x, n: (x, 0, 0))
    sink_spec = pl.BlockSpec(memory_space=pltpu.SMEM)
    return grp, nb, q_spec, band, bias_spec, sink_spec


def _attn_fwd(p, bias, sink, n_hgrn, n_attn):
    t = p.shape[0]
    grp, nb, q_spec, band, bias_spec, sink_spec = _attn_specs(t, n_hgrn, n_attn)

    def body(q_ref, kp, kc, kn, vp, vc, vn, bias_ref, sink_ref, y_ref):
        x, n = pl.program_id(0), pl.program_id(1)
        kb = jnp.concatenate([kp[...], kc[...], kn[...]], axis=0).astype(BF16)
        vb = jnp.concatenate([vp[...], vc[...], vn[...]], axis=0).astype(BF16)
        pr, _ = _attn_probs(_stack_heads(q_ref, grp), kb, bias_ref[...].reshape(grp * WINDOW, KEY_SPAN),
                            _sink_column(sink_ref, x, grp), _attn_valid(n, t, grp))
        y = _dot(pr.astype(BF16), vb).astype(BF16)
        for g in range(grp):
            y_ref[:, g * HEAD:(g + 1) * HEAD] = y[g * WINDOW:(g + 1) * WINDOW]

    return pl.pallas_call(
        body, name="attn_fwd", grid=(KV_HEADS, nb),
        in_specs=[q_spec] + band + [bias_spec, sink_spec],
        out_specs=pl.BlockSpec((WINDOW, grp * HEAD), lambda x, n: (n, x)),
        out_shape=jax.ShapeDtypeStruct((t, n_attn * HEAD), BF16),
        compiler_params=_cparams("parallel", "parallel"),
    )(p, p, p, p, p, p, p, bias, sink)


def _attn_bwd(p, dcat, bias, sink, n_hgrn, n_attn, deps=()):
    t = p.shape[0]
    grp, nb, q_spec, band, bias_spec, sink_spec = _attn_specs(t, n_hgrn, n_attn)
    inv = 1.0 / math.sqrt(HEAD)

    def body(q_ref, kp, kc, kn, vp, vc, vn, bias_ref, sink_ref, do_ref, *rest):
        dq_ref, dk_ref, dv_ref, dbias_ref, dsink_ref, dk_s, dv_s = rest[-7:]
        x, n = pl.program_id(0), pl.program_id(1)

        @pl.when(n == 0)
        def _():
            dk_s[...] = jnp.zeros_like(dk_s)
            dv_s[...] = jnp.zeros_like(dv_s)
            dbias_ref[...] = jnp.zeros_like(dbias_ref)
            dsink_ref[...] = jnp.zeros_like(dsink_ref)

        kb = jnp.concatenate([kp[...], kc[...], kn[...]], axis=0).astype(BF16)
        vb = jnp.concatenate([vp[...], vc[...], vn[...]], axis=0).astype(BF16)
        q = _stack_heads(q_ref, grp)
        do = _stack_heads(do_ref, grp)
        pr, p_sink = _attn_probs(q, kb, bias_ref[...].reshape(grp * WINDOW, KEY_SPAN),
                                 _sink_column(sink_ref, x, grp), _attn_valid(n, t, grp))
        dpr = _dot_nt(do, vb)
        delta = jnp.sum(pr * dpr, axis=-1, keepdims=True)
        ds = pr * (dpr - delta)
        ds_b = ds.astype(BF16)
        dq = (_dot(ds_b, kb) * inv).astype(BF16)
        dsink = -p_sink * delta
        for g in range(grp):
            head = slice(g * WINDOW, (g + 1) * WINDOW)
            dq_ref[:, g * HEAD:(g + 1) * HEAD] = dq[head]
            dbias_ref[g] += ds[head]
            dsink_ref[g:g + 1, :] += jnp.broadcast_to(jnp.sum(dsink[head], axis=0, keepdims=True), (1, WINDOW))
        rows = pl.ds(pl.multiple_of(n * WINDOW, WINDOW), KEY_SPAN)
        dk_s[rows, :] += _dot_tn(ds_b, q) * inv
        dv_s[rows, :] += _dot_tn(pr.astype(BF16), do)

        @pl.when(n == nb - 1)
        def _():
            dk_ref[...] = dk_s[pl.ds(WINDOW, t), :].astype(BF16)
            dv_ref[...] = dv_s[pl.ds(WINDOW, t), :].astype(BF16)

    do_spec = pl.BlockSpec((WINDOW, grp * HEAD), lambda x, n: (n, n_hgrn // grp + x))
    kv_out = pl.BlockSpec((t, HEAD), lambda x, n: (0, x))
    return pl.pallas_call(
        body, name="attn_bwd", grid=(KV_HEADS, nb),
        in_specs=[q_spec] + band + [bias_spec, sink_spec, do_spec] + [ANY] * len(deps),
        out_specs=[pl.BlockSpec((WINDOW, grp * HEAD), lambda x, n: (n, x)), kv_out, kv_out,
                   bias_spec, pl.BlockSpec((None, grp, WINDOW), lambda x, n: (x, 0, 0))],
        out_shape=[jax.ShapeDtypeStruct((t, n_attn * HEAD), BF16),
                   jax.ShapeDtypeStruct((t, KV_HEADS * HEAD), BF16),
                   jax.ShapeDtypeStruct((t, KV_HEADS * HEAD), BF16),
                   jax.ShapeDtypeStruct((n_attn, WINDOW, KEY_SPAN), F32),
                   jax.ShapeDtypeStruct((KV_HEADS, grp, WINDOW), F32)],
        scratch_shapes=[pltpu.VMEM((t + 2 * WINDOW, HEAD), F32), pltpu.VMEM((t + 2 * WINDOW, HEAD), F32)],
        compiler_params=_cparams("parallel", "arbitrary"),
    )(p, p, p, p, p, p, p, bias, sink, dcat, *deps)


def _position():
    return lax.axis_index("x"), lax.axis_index("y"), lax.axis_index("c")


def _handshake(peers):
    barrier = pltpu.get_barrier_semaphore()
    for peer in peers:
        pl.semaphore_signal(barrier, inc=1, device_id=peer, device_id_type=MESH)
    pl.semaphore_wait(barrier, len(peers))


def _sequencer(name, collective_id, scratch_types):
    return functools.partial(
        pl.kernel, mesh=plsc.ScalarSubcoreMesh(axis_name="sc", num_cores=1), name=name,
        scratch_types=scratch_types, compiler_params=pltpu.CompilerParams(collective_id=collective_id))


def _all_gather(name, shard, collective_id):
    rows = shard.shape[0]
    assert rows % 2 == 0
    rh = rows // 2
    src = jax.new_ref(shard, memory_space=pltpu.MemorySpace.HBM)
    out = jax.empty_ref(jax.ShapeDtypeStruct((N_DEV,) + shard.shape, shard.dtype),
                        memory_space=pltpu.MemorySpace.HBM)
    n_copies = 11

    @_sequencer(name, collective_id, (pltpu.SemaphoreType.DMA((n_copies,)), pltpu.SemaphoreType.DMA((n_copies,)),
                                      pltpu.SemaphoreType.DMA))
    def launch(send_sems, recv_sems, local_sem):
        x, y, c = _position()
        sibling = (x, y, 1 - c)
        xn, yn, dg = (1 - x, y), (x, 1 - y), (1 - x, 1 - y)
        _handshake([sibling, (*xn, c), (*yn, c)])

        def part(ref, half):
            return ref if half is None else ref.at[pl.ds(half * rh, rh)]

        def slot(chip, core, half=None):
            return part(out.at[4 * chip[0] + 2 * chip[1] + core], half)

        def copy(k, chip, core, half, to, own=False):
            return pltpu.make_async_remote_copy(
                src_ref=part(src, half) if own else slot(chip, core, half), dst_ref=slot(chip, core, half),
                send_sem=send_sems.at[k], recv_sem=recv_sems.at[k], device_id=to, device_id_type=MESH)

        def landed(k, chip, core, half):
            copy(k, chip, core, half, (x, y, c)).wait_recv()

        mine = pltpu.make_async_copy(src, slot((x, y), c), local_sem)
        mine.start()
        sent = [copy(0, (x, y), c, None, sibling, own=True),
                copy(1, (x, y), c, 0, (*xn, c), own=True), copy(3, (x, y), c, 1, (*yn, c), own=True),
                copy(2, (x, y), c, 1, (*xn, c), own=True), copy(4, (x, y), c, 0, (*yn, c), own=True)]
        for cp in sent:
            cp.start()

        def then(cp):
            cp.start()
            sent.append(cp)

        landed(1, xn, c, 0)
        then(copy(5, xn, c, 0, (*yn, c)))
        landed(3, yn, c, 1)
        then(copy(6, yn, c, 1, (*xn, c)))
        landed(2, xn, c, 1)
        then(copy(7, xn, c, None, sibling))
        landed(4, yn, c, 0)
        then(copy(8, yn, c, None, sibling))
        landed(5, dg, c, 0)
        then(copy(9, dg, c, 0, sibling))
        landed(6, dg, c, 1)
        then(copy(10, dg, c, 1, sibling))
        landed(0, (x, y), 1 - c, None)
        landed(7, xn, 1 - c, None)
        landed(8, yn, 1 - c, None)
        landed(9, dg, 1 - c, 0)
        landed(10, dg, 1 - c, 1)
        for cp in sent:
            cp.wait_send()
        mine.wait()

    launch()
    return out[...]


HBM = pl.BlockSpec(memory_space=pltpu.HBM)
SEM = pl.BlockSpec(memory_space=pltpu.SEMAPHORE)
EFFECT = pltpu.SideEffectType.DATAFLOW_SIDE_EFFECTING


def _pair_copies(s_ref, land_ref, send_sems, recv_sems):
    x, y, c = _position()
    return [pltpu.make_async_remote_copy(
        src_ref=s_ref.at[2 * k + (1 - c)], dst_ref=land_ref.at[k], send_sem=send_sems.at[k],
        recv_sem=recv_sems.at[k], device_id=(x, y, 1 - c), device_id_type=MESH) for k in range(4)]


def _pair_start(name, stack):
    land_shape = (4,) + stack.shape[1:]

    def body(s_ref, land_ref, send_sems, recv_sems, s_thru, land_thru, token):
        for cp in _pair_copies(s_ref, land_ref, send_sems, recv_sems):
            cp.start()
        token[...] = jnp.zeros_like(token)

    return pl.pallas_call(
        body, name=name,
        out_shape=(pltpu.SemaphoreType.DMA((4,)), pltpu.SemaphoreType.DMA((4,)),
                   pltpu.HBM(stack.shape, stack.dtype), pltpu.HBM(land_shape, stack.dtype),
                   jax.ShapeDtypeStruct((8, LANES), F32)),
        in_specs=(HBM, HBM), out_specs=(SEM, SEM, HBM, HBM, pl.BlockSpec(memory_space=pltpu.VMEM)),
        input_output_aliases={0: 2, 1: 3}, compiler_params=pltpu.CompilerParams(has_side_effects=EFFECT),
    )(pltpu.with_memory_space_constraint(stack, pltpu.HBM),
      pltpu.with_memory_space_constraint(lax.empty(land_shape, stack.dtype), pltpu.HBM))


def _pair_wait(name, started, after):
    send_sems, recv_sems, s_thru, land_thru, _ = started

    def body(s_ref, land_ref, send_sems, recv_sems, after_ref, s_out, land_out):
        for cp in _pair_copies(s_ref, land_ref, send_sems, recv_sems):
            cp.wait_send()
            cp.wait_recv()

    return pl.pallas_call(
        body, name=name,
        out_shape=(pltpu.HBM(s_thru.shape, s_thru.dtype), pltpu.HBM(land_thru.shape, land_thru.dtype)),
        in_specs=(HBM, HBM, SEM, SEM, ANY), out_specs=(HBM, HBM), input_output_aliases={0: 0, 1: 1},
        compiler_params=pltpu.CompilerParams(has_side_effects=EFFECT),
    )(s_thru, land_thru, send_sems, recv_sems, after)


def _pair_sum(name, stack, other, core):
    _, r, c = stack.shape
    tr = _row_tile(r, 1024)

    def body(core_ref, a_ref, b_ref, o_ref):
        o_ref[...] = (a_ref[...].astype(F32) + b_ref[...].astype(F32)).astype(o_ref.dtype)

    grid_spec = pltpu.PrefetchScalarGridSpec(
        num_scalar_prefetch=1, grid=(4, r // tr),
        in_specs=[pl.BlockSpec((None, tr, c), lambda k, i, core_ref: (2 * k + core_ref[0], i, 0)),
                  pl.BlockSpec((None, tr, c), lambda k, i, core_ref: (k, i, 0))],
        out_specs=pl.BlockSpec((None, tr, c), lambda k, i, core_ref: (k, i, 0)))
    return pl.pallas_call(
        body, name=name, grid_spec=grid_spec, out_shape=jax.ShapeDtypeStruct((4, r, c), stack.dtype),
        compiler_params=_cparams("parallel", "parallel"),
    )(core, stack, other)


def _chip_exchange(name, sums, collective_id):
    src = jax.new_ref(sums, memory_space=pltpu.MemorySpace.HBM)
    out = jax.empty_ref(jax.ShapeDtypeStruct((3,) + sums.shape[1:], sums.dtype),
                        memory_space=pltpu.MemorySpace.HBM)

    @_sequencer(name, collective_id, (pltpu.SemaphoreType.DMA((3,)), pltpu.SemaphoreType.DMA((3,))))
    def launch(send_sems, recv_sems):
        x, y, c = _position()
        chips = [(1 - x, y), (x, 1 - y), (1 - x, 1 - y)]
        _handshake([(*chip, c) for chip in chips])
        copies = [pltpu.make_async_remote_copy(
            src_ref=src.at[2 * px + py], dst_ref=out.at[j], send_sem=send_sems.at[j],
            recv_sem=recv_sems.at[j], device_id=(px, py, c), device_id_type=MESH)
            for j, (px, py) in enumerate(chips)]
        for cp in copies:
            cp.start()
        for cp in copies:
            cp.wait()

    launch()
    return out[...]


def _small_all_reduce(name, part):
    r = part.shape[0]

    def body(x_ref, out_ref, gather, send_sems, recv_sems):
        x, y, c = _position()
        me = 4 * x + 2 * y + c
        gather[me] = x_ref[...]
        copies = []
        for k in range(1, N_DEV):
            peer = (x ^ (k >> 2), y ^ ((k >> 1) & 1), c ^ (k & 1))
            copies.append(pltpu.make_async_remote_copy(
                src_ref=x_ref, dst_ref=gather.at[me], send_sem=send_sems.at[k - 1],
                recv_sem=recv_sems.at[k - 1], device_id=peer, device_id_type=MESH))
        for cp in copies:
            cp.start()
        for k in range(1, N_DEV):
            peer_slot = 4 * (x ^ (k >> 2)) + 2 * (y ^ ((k >> 1) & 1)) + (c ^ (k & 1))
            pltpu.make_async_remote_copy(
                src_ref=x_ref, dst_ref=gather.at[peer_slot], send_sem=send_sems.at[k - 1],
                recv_sem=recv_sems.at[k - 1], device_id=(x, y, c), device_id_type=MESH).wait()
        acc = gather[0]
        for j in range(1, N_DEV):
            acc = acc + gather[j]
        out_ref[...] = acc

    vm = pl.BlockSpec(memory_space=pltpu.VMEM)
    return pl.pallas_call(
        body, name=name, in_specs=[vm], out_specs=vm, out_shape=jax.ShapeDtypeStruct((r, LANES), F32),
        scratch_shapes=[pltpu.VMEM((N_DEV, r, LANES), F32), pltpu.SemaphoreType.DMA((7,)),
                        pltpu.SemaphoreType.DMA((7,))],
    )(part)


def _adam_math(w, g, m, v):
    m = ADAM_B1 * m + (1.0 - ADAM_B1) * g
    v = ADAM_B2 * v + (1.0 - ADAM_B2) * jnp.square(g)
    m_hat = m / (1.0 - ADAM_B1 ** ADAM_STEP)
    v_hat = v / (1.0 - ADAM_B2 ** ADAM_STEP)
    delta = -ADAM_LR * (m_hat / (jnp.sqrt(v_hat) + ADAM_EPS) + ADAM_WD * w)
    return delta, m, v


def _adam_shard(name, w, m, v, sums, recv, chip, deps=()):
    r, c = w.shape
    tr = _row_tile(r, 256)

    def body(chip_ref, w_ref, m_ref, v_ref, own_ref, r0_ref, r1_ref, r2_ref, *rest):
        g_out, d_out, m_out, v_out = rest[-4:]
        g = ((own_ref[...].astype(F32) + r0_ref[...].astype(F32)) + r1_ref[...].astype(F32)) + r2_ref[...].astype(F32)
        delta, m_new, v_new = _adam_math(w_ref[...], g, m_ref[...], v_ref[...])
        g_out[...] = g
        d_out[...] = delta
        m_out[...] = m_new
        v_out[...] = v_new

    plain = pl.BlockSpec((tr, c), lambda i, chip_ref: (i, 0))
    piece = lambda j: pl.BlockSpec((None, tr, c), lambda i, chip_ref: (j, i, 0))
    grid_spec = pltpu.PrefetchScalarGridSpec(
        num_scalar_prefetch=1, grid=(r // tr,),
        in_specs=[plain, plain, plain,
                  pl.BlockSpec((None, tr, c), lambda i, chip_ref: (chip_ref[0], i, 0)),
                  piece(0), piece(1), piece(2)] + [ANY] * len(deps),
        out_specs=[plain] * 4)
    shape = jax.ShapeDtypeStruct((r, c), F32)
    return pl.pallas_call(
        body, name=name, grid_spec=grid_spec, out_shape=[shape] * 4, compiler_params=_cparams("parallel"),
    )(chip, w, m, v, sums, recv, recv, recv, *deps)


def _adam_small(name, w, g, m, v):
    r = w.shape[0]

    def body(w_ref, g_ref, m_ref, v_ref, d_out, m_out, v_out):
        delta, m_new, v_new = _adam_math(w_ref[...], g_ref[...], m_ref[...], v_ref[...])
        d_out[...] = delta
        m_out[...] = m_new
        v_out[...] = v_new

    vm = pl.BlockSpec(memory_space=pltpu.VMEM)
    shape = jax.ShapeDtypeStruct((r, LANES), F32)
    return pl.pallas_call(body, name=name, in_specs=[vm] * 4, out_specs=[vm] * 3, out_shape=[shape] * 3)(w, g, m, v)


def _reduce_scatter(tag, started, after, core, collective_id):
    grad_stack, other = _pair_wait("rs_pair_wait_" + tag, started, after)
    sums = _pair_sum("rs_sum_" + tag, grad_stack, other, core)
    return sums, _chip_exchange("rs_chip_" + tag, sums, collective_id)


def _pack(arrays):
    flat = jnp.concatenate([a.reshape(-1).astype(F32) for a in arrays])
    rows = -(-flat.shape[0] // LANES)
    rows = -(-rows // 8) * 8
    return jnp.pad(flat, (0, rows * LANES - flat.shape[0])).reshape(rows, LANES)


def _unpack(packed, like):
    flat = packed.reshape(-1)
    out, off = [], 0
    for a in like:
        out.append(flat[off:off + a.size].reshape(a.shape))
        off += a.size
    return out


SMALL = ("pre_norm_ffn1", "post_norm_ffn1", "pre_norm_mix", "post_norm_mix", "hgrn_lower_bounds_fwd",
         "hgrn_lower_bounds_bwd", "hgrn_out_norm", "attn_sink", "pre_norm_ffn2", "post_norm_ffn2", "rel_bias_table")
BIG = ("w_ffn1_gate_up", "w_ffn1_down", "w_mix_in", "w_mix_out", "w_ffn2_gate_up", "w_ffn2_down")
AG_ID = {n: 1 + i for i, n in enumerate(BIG)}
RS_ID = {n: 7 + i for i, n in enumerate(BIG)}
ORDER = ("pre_norm_ffn1", "post_norm_ffn1", "w_ffn1_gate_up", "w_ffn1_down", "pre_norm_mix", "post_norm_mix",
         "w_mix_in", "hgrn_lower_bounds_fwd", "hgrn_lower_bounds_bwd", "hgrn_out_norm", "attn_sink", "w_mix_out",
         "pre_norm_ffn2", "post_norm_ffn2", "w_ffn2_gate_up", "w_ffn2_down", "rel_bias_table")


def kernel(x, pre_norm_ffn1, post_norm_ffn1, w_ffn1_gate_up, w_ffn1_down, pre_norm_mix, post_norm_mix, w_mix_in, hgrn_lower_bounds_fwd, hgrn_lower_bounds_bwd, hgrn_out_norm, attn_sink, w_mix_out, pre_norm_ffn2, post_norm_ffn2, w_ffn2_gate_up, w_ffn2_down, rel_bias_table, loss_target, m_pre_norm_ffn1, m_post_norm_ffn1, m_w_ffn1_gate_up, m_w_ffn1_down, m_pre_norm_mix, m_post_norm_mix, m_w_mix_in, m_hgrn_lower_bounds_fwd, m_hgrn_lower_bounds_bwd, m_hgrn_out_norm, m_attn_sink, m_w_mix_out, m_pre_norm_ffn2, m_post_norm_ffn2, m_w_ffn2_gate_up, m_w_ffn2_down, m_rel_bias_table, v_pre_norm_ffn1, v_post_norm_ffn1, v_w_ffn1_gate_up, v_w_ffn1_down, v_pre_norm_mix, v_post_norm_mix, v_w_mix_in, v_hgrn_lower_bounds_fwd, v_hgrn_lower_bounds_bwd, v_hgrn_out_norm, v_attn_sink, v_w_mix_out, v_pre_norm_ffn2, v_post_norm_ffn2, v_w_ffn2_gate_up, v_w_ffn2_down, v_rel_bias_table):
    args = dict(locals())
    wts = {n: args[n] for n in ORDER}
    mom = {n: args["m_" + n] for n in ORDER}
    var = {n: args["v_" + n] for n in ORDER}

    x0 = x[0]
    target = loss_target[0]
    t, d = x0.shape
    n_hgrn = d // 2 // HEAD
    n_attn = (d - d // 2) // HEAD
    core = lax.axis_index("c").astype(jnp.int32).reshape(1)
    chip = (2 * lax.axis_index("x") + lax.axis_index("y")).astype(jnp.int32).reshape(1)

    full = {n: _all_gather("ag_" + n, wts[n][0].astype(BF16), AG_ID[n]) for n in BIG}
    w_gu1, w_gu2 = full["w_ffn1_gate_up"], full["w_ffn2_gate_up"]
    w_d1 = full["w_ffn1_down"].reshape(-1, d)
    w_d2 = full["w_ffn2_down"].reshape(-1, d)
    w_out = full["w_mix_out"].reshape(-1, d)
    w_in = full["w_mix_in"]

    g = {n: wts[n] for n in SMALL}
    lb_f = jax.nn.softmax(g["hgrn_lower_bounds_fwd"], axis=0)[0:1]
    lb_b = jax.nn.softmax(g["hgrn_lower_bounds_bwd"], axis=0)[0:1]
    bucket_idx = jnp.asarray(_t5_bucket_index())
    bias = _bias_build(g["rel_bias_table"], bucket_idx)

    n1 = _pre_norm("pre_norm1", x0, g["pre_norm_ffn1"])
    a1, gu1 = _ffn_up("ffn1_gate_up", n1, w_gu1)
    ff1 = _matmul("ffn1_down", a1, w_d1, mode="nn", out_dtype=F32)
    x1, h = _post_res_pre("res1", x0, ff1, g["post_norm_ffn1"], g["pre_norm_mix"], 0.5)
    p = _matmul("mix_in", h, w_in, mode="nn", stack=True, out_dtype=F32)
    y_h, o_raw = _hgrn_fwd(p, lb_f, lb_b, g["hgrn_out_norm"], n_hgrn)
    y_a = _attn_fwd(p, bias, g["attn_sink"], n_hgrn, n_attn)
    cat = jnp.concatenate([y_h, y_a], axis=1)
    mixed = _matmul("mix_out", cat, w_out, mode="nn", out_dtype=F32)
    x2, n2 = _post_res_pre("res2", x1, mixed, g["post_norm_mix"], g["pre_norm_ffn2"], 1.0)
    a2, gu2 = _ffn_up("ffn2_gate_up", n2, w_gu2)
    ff2 = _matmul("ffn2_down", a2, w_d2, mode="nn", out_dtype=F32)
    dy3, loss_part = _post_res_loss("res3_loss", x2, ff2, g["post_norm_ffn2"], target, 0.5)

    small_grad = {}
    scattered = {}

    pending = []

    def scatter(name, grad_stack):
        started = _pair_start("rs_pair_" + name, grad_stack)
        pending.append((name, started))
        return [started[4]]

    def settle(after):
        deps = []
        while pending:
            name, started = pending.pop(0)
            scattered[name] = _reduce_scatter(name, started, after, core, RS_ID[name])
            deps.append(scattered[name][0])
        return deps

    def ffn_bwd(tag, dy, ff, a, gu, n_in, x_in, w_gu, w_d, post_name, pre_name, gu_name, d_name):
        dff, small_grad[post_name] = _post_bwd("post_bwd" + tag, dy, ff, g[post_name], 0.5)
        dep = settle(dff)
        dep = scatter(d_name, _matmul("dw_down" + tag, a, dff, mode="tn", out_dtype=BF16,
                                      deps=dep).reshape(N_DEV, -1, d))
        dgu = _ffn_dact("d_act" + tag, dff, w_d, gu, deps=dep)
        dep = settle(dgu)
        dep = scatter(gu_name, _matmul("dw_gate_up" + tag, n_in, dgu, mode="tn", stack=True, halves=True,
                                       out_dtype=BF16, deps=dep))
        dn = _matmul("d_norm" + tag, dgu, w_gu, mode="nt", stack=True, halves=True, out_dtype=F32, deps=dep)
        dep = settle(dn)
        dx, small_grad[pre_name] = _pre_bwd("pre_bwd" + tag, dn, x_in, g[pre_name], dy, deps=dep)
        return dx

    dx2 = ffn_bwd("2", dy3, ff2, a2, gu2, n2, x2, w_gu2, w_d2, "post_norm_ffn2", "pre_norm_ffn2",
                  "w_ffn2_gate_up", "w_ffn2_down")

    dmixed, small_grad["post_norm_mix"] = _post_bwd("post_bwd_mix", dx2, mixed, g["post_norm_mix"], 1.0)
    dep = settle(dmixed)
    dcat = _matmul("d_cat", dmixed, w_out, mode="nt", out_dtype=F32, deps=dep)
    dep = scatter("w_mix_out", _matmul("dw_mix_out", cat, dmixed, mode="tn", out_dtype=BF16).reshape(N_DEV, -1, d))
    dq_a, dk_a, dv_a, dbias, dsink_rows = _attn_bwd(p, dcat, bias, g["attn_sink"], n_hgrn, n_attn, deps=dep)
    dq_h, di_h, dzf, dzb, dg_h, dlb_f, dlb_b, small_grad["hgrn_out_norm"] = _hgrn_bwd(
        p, o_raw, dcat, lb_f, lb_b, g["hgrn_out_norm"], n_hgrn)
    dp = jnp.concatenate([dq_h, di_h, dzf, dzb, dg_h, dq_a, dk_a, dv_a], axis=1)
    dep = settle(dp)
    dh = _matmul("d_h", dp, w_in, mode="nt", stack=True, out_dtype=F32, deps=dep)
    dep = scatter("w_mix_in", _matmul("dw_mix_in", h, dp, mode="tn", stack=True, out_dtype=BF16))
    dx1, small_grad["pre_norm_mix"] = _pre_bwd("pre_bwd_mix", dh, x1, g["pre_norm_mix"], dx2, deps=dep)

    dx0 = ffn_bwd("1", dx1, ff1, a1, gu1, n1, x0, w_gu1, w_d1, "post_norm_ffn1", "pre_norm_ffn1",
                  "w_ffn1_gate_up", "w_ffn1_down")
    settle(dx0)

    def lb_grad(dlb, lb):
        da0 = dlb * lb * (1.0 - lb)
        return jnp.concatenate([da0, -da0], axis=0)

    small_grad["hgrn_lower_bounds_fwd"] = lb_grad(dlb_f, lb_f)
    small_grad["hgrn_lower_bounds_bwd"] = lb_grad(dlb_b, lb_b)
    small_grad["attn_sink"] = dsink_rows[:, :, 0].reshape(1, n_attn)
    small_grad["rel_bias_table"] = jnp.transpose(_bias_reduce(dbias, bucket_idx)[:, :, 0])

    parts = [small_grad[n] for n in SMALL] + [loss_part[:, 0:1]]
    red = _small_all_reduce("small_all_reduce", _pack(parts))
    red_list = _unpack(red, parts)
    loss = red_list[-1].reshape(())
    sg = dict(zip(SMALL, red_list[:-1]))
    like = [wts[n] for n in SMALL]
    d_s, m_s, v_s = _adam_small("adam_small", _pack(like), _pack([sg[n] for n in SMALL]),
                                _pack([mom[n] for n in SMALL]), _pack([var[n] for n in SMALL]))
    grads = dict(sg)
    delta = dict(zip(SMALL, _unpack(d_s, like)))
    new_m = dict(zip(SMALL, _unpack(m_s, like)))
    new_v = dict(zip(SMALL, _unpack(v_s, like)))

    dep = []
    for n in ("w_ffn2_down", "w_ffn2_gate_up", "w_mix_out", "w_mix_in", "w_ffn1_down", "w_ffn1_gate_up"):
        sums, recv = scattered[n]
        gr, de, nm, nv = _adam_shard("adam_" + n, wts[n][0], mom[n][0], var[n][0], sums, recv, chip, deps=dep)
        grads[n], delta[n], new_m[n], new_v[n] = gr[None], de[None], nm[None], nv[None]
        dep = [gr]

    return (loss, dx0[None], *[grads[n] for n in ORDER], *[delta[n] for n in ORDER],
            *[new_m[n] for n in ORDER], *[new_v[n] for n in ORDER])
```

```python
import functools
import math

import numpy as np
import jax
import jax.numpy as jnp
from jax import lax
from jax.experimental import pallas as pl
from jax.experimental.pallas import tpu as pltpu
from jax.experimental.pallas import tpu_sc as plsc

F32 = jnp.float32
BF16 = jnp.bfloat16
MESH = pl.DeviceIdType.MESH

N_DEV = 8
EPS = 1e-6
NEG_INF = -1e30
HEAD = 128
CHUNK = 64
WINDOW = 128
KEY_SPAN = 3 * WINDOW
KV_HEADS = 2
REL_BUCKETS = 32
REL_MAX_DIST = 128
ADAM_LR, ADAM_B1, ADAM_B2, ADAM_EPS, ADAM_WD, ADAM_STEP = 0.001, 0.9, 0.999, 1e-08, 0.01, 10
LANES = 128
VMEM_LIMIT = 56 * 1024 * 1024
ANY = pl.BlockSpec(memory_space=pl.ANY)


def _cparams(*sem):
    return pltpu.CompilerParams(dimension_semantics=sem if sem else None, vmem_limit_bytes=VMEM_LIMIT)


def _dot(a, b):
    return jnp.dot(a, b, preferred_element_type=F32)


def _dot_nt(a, b):
    return lax.dot_general(a, b, (((1,), (1,)), ((), ())), preferred_element_type=F32)


def _dot_tn(a, b):
    return lax.dot_general(a, b, (((0,), (0,)), ((), ())), preferred_element_type=F32)


def _tile(dim, target):
    for c in (target, 1024, 512, 256, 128):
        if c <= target and dim % c == 0:
            return c
    return dim


def _row_tile(rows, target):
    fits = [c for c in range(16, min(rows, target) + 1, 16) if rows % c == 0]
    return max(fits) if fits else rows


K_WHOLE = 2048
K_STEP = 2816


def _k_tile(kd):
    if kd <= K_WHOLE:
        return kd
    return max(c for c in range(LANES, K_STEP + 1, LANES) if kd % c == 0)


def _matmul(name, a, b, *, mode, out_dtype, stack=False, halves=False, tm=1024, tn=1024, deps=()):
    grp = 1
    if mode == "nn":
        m, kd = a.shape
        n = b.shape[0] * b.shape[2] if stack else b.shape[1]
    elif mode == "nt":
        m = a.shape[-2]
        n, kd = (b.shape[1], b.shape[0] * b.shape[2]) if stack else b.shape
    else:
        kd, m = a.shape
        n = b.shape[-1] * (2 if halves else 1)
    if stack:
        n1 = b.shape[2] if mode != "tn" else n // N_DEV
        if mode == "nt":
            grp = 2 if 2 * n1 <= K_STEP else 1
            tk = grp * n1
        else:
            grp = 1 if n1 % LANES == 0 else 2
            tn = grp * n1
        assert (grp * n1) % LANES == 0
    per_half = N_DEV // 2 // grp
    tm = _tile(m, tm)
    if not (stack and mode in ("nn", "tn")):
        tn = _tile(n, tn)
    if not (stack and mode == "nt"):
        tk = _k_tile(kd)
    nk = kd // tk
    lead = None if grp == 1 else grp
    b_outer = nk == 1 and b.size > a.size
    grid = (n // tn, m // tm, nk) if b_outer else (m // tm, n // tn, nk)

    def spec(shape, index):
        return pl.BlockSpec(shape, (lambda g0, g1, k: index(g1, g0, k)) if b_outer else index)

    if mode == "nn":
        a_spec = spec((tm, tk), lambda i, j, k: (i, k))
        if stack:
            b_spec = spec((lead, tk, n1), lambda i, j, k: (j, k, 0))
        else:
            b_spec = spec((tk, tn), lambda i, j, k: (k, j))
        dot = _dot
    elif mode == "nt":
        if halves:
            a_spec = spec((None, tm, tk), lambda i, j, k: (k // per_half, i, k % per_half))
        else:
            a_spec = spec((tm, tk), lambda i, j, k: (i, k))
        if stack:
            b_spec = spec((lead, tn, n1), lambda i, j, k: (k, j, 0))
        else:
            b_spec = spec((tn, tk), lambda i, j, k: (j, k))
        dot = _dot_nt
    else:
        a_spec = spec((tk, tm), lambda i, j, k: (k, i))
        if halves:
            b_spec = spec((None, tk, tn), lambda i, j, k: (j // per_half, k, j % per_half))
        else:
            b_spec = spec((tk, tn), lambda i, j, k: (k, j))
        dot = _dot_tn
    if stack and mode == "tn":
        out_shape = jax.ShapeDtypeStruct((N_DEV, m, n1), out_dtype)
        o_spec = spec((lead, tm, n1), lambda i, j, k: (j, i, 0))
    else:
        out_shape = jax.ShapeDtypeStruct((m, n), out_dtype)
        o_spec = spec((tm, tn), lambda i, j, k: (i, j))
    b_grouped = stack and grp > 1 and mode != "tn"
    o_grouped = stack and grp > 1 and mode == "tn"

    def product(a_ref, b_ref):
        bmat = jnp.concatenate([b_ref[s] for s in range(grp)], axis=1) if b_grouped else b_ref[...]
        return dot(a_ref[...], bmat)

    def store(o_ref, val):
        if o_grouped:
            for s in range(grp):
                o_ref[s] = val[:, s * n1:(s + 1) * n1].astype(o_ref.dtype)
        else:
            o_ref[...] = val.astype(o_ref.dtype)

    def body_whole(a_ref, b_ref, *rest):
        store(rest[-1], product(a_ref, b_ref))

    def body_steps(a_ref, b_ref, *rest):
        o_ref, acc_ref = rest[-2:]
        k = pl.program_id(2)

        @pl.when(k == 0)
        def _():
            acc_ref[...] = product(a_ref, b_ref)

        @pl.when(k > 0)
        def _():
            acc_ref[...] += product(a_ref, b_ref)

        @pl.when(k == nk - 1)
        def _():
            store(o_ref, acc_ref[...])

    return pl.pallas_call(
        body_whole if nk == 1 else body_steps, name=name, grid=grid,
        in_specs=[a_spec, b_spec] + [ANY] * len(deps), out_specs=o_spec, out_shape=out_shape,
        scratch_shapes=[] if nk == 1 else [pltpu.VMEM((tm, tn), F32)],
        compiler_params=_cparams("parallel", "parallel", "arbitrary"),
    )(a, b, *deps)


def _col_parts(width, parts=2):
    groups = width // LANES
    parts = max(1, min(parts, groups // 2))
    bounds = [LANES * (groups * p // parts) for p in range(parts)] + [width]
    return [slice(bounds[p], bounds[p + 1]) for p in range(parts)]


def _ffn_up(name, n, w_stack):
    t, d = n.shape
    s, _, n1 = w_stack.shape
    half = s // 2
    tm = _tile(t, 512)

    def body(n_ref, wg_ref, wu_ref, act_ref, gu_ref):
        nv = n_ref[...]
        for cols in _col_parts(n1):
            gate = _dot(nv, wg_ref[:, cols])
            up = _dot(nv, wu_ref[:, cols])
            sg = jax.nn.sigmoid(gate)
            silu = gate * sg
            act_ref[:, cols] = (silu * up).astype(BF16)
            gu_ref[0, :, cols] = (up * (sg * (1.0 + gate * (1.0 - sg)))).astype(BF16)
            gu_ref[1, :, cols] = silu.astype(BF16)

    return pl.pallas_call(
        body, name=name, grid=(half, t // tm),
        in_specs=[pl.BlockSpec((tm, d), lambda j, i: (i, 0)),
                  pl.BlockSpec((None, d, n1), lambda j, i: (j, 0, 0)),
                  pl.BlockSpec((None, d, n1), lambda j, i: (half + j, 0, 0))],
        out_specs=[pl.BlockSpec((tm, n1), lambda j, i: (i, j)), pl.BlockSpec((2, tm, n1), lambda j, i: (0, i, j))],
        out_shape=[jax.ShapeDtypeStruct((t, half * n1), BF16), jax.ShapeDtypeStruct((2, t, half * n1), BF16)],
        compiler_params=_cparams("parallel", "parallel"),
    )(n, w_stack, w_stack)


def _ffn_dact(name, dff, w_d, gu, deps=()):
    t, d = dff.shape
    f = w_d.shape[0]
    tm = _tile(t, 512)
    tn = _tile(f, 1408)

    def body(dff_ref, w_ref, gu_ref, *rest):
        dgu_ref = rest[-1]
        da = _dot_nt(dff_ref[...], w_ref[...]).astype(BF16)
        dgu_ref[0] = da * gu_ref[0]
        dgu_ref[1] = da * gu_ref[1]

    pair = pl.BlockSpec((2, tm, tn), lambda j, i: (0, i, j))
    return pl.pallas_call(
        body, name=name, grid=(f // tn, t // tm),
        in_specs=[pl.BlockSpec((tm, d), lambda j, i: (i, 0)), pl.BlockSpec((tn, d), lambda j, i: (j, 0)), pair]
        + [ANY] * len(deps),
        out_specs=pair, out_shape=jax.ShapeDtypeStruct((2, t, f), BF16),
        compiler_params=_cparams("parallel", "parallel"),
    )(dff, w_d, gu, *deps)


ROWS = 256


def _rstd(xf):
    return lax.rsqrt(jnp.mean(xf * xf, axis=-1, keepdims=True) + EPS)


def _row_spec(t, d):
    return pl.BlockSpec((min(ROWS, t), d), lambda i: (i, 0))


def _vec_spec(d):
    return pl.BlockSpec((1, d), lambda i: (0, 0))


def _pre_norm(name, x, gain):
    t, d = x.shape

    def body(x_ref, g_ref, n_ref):
        xf = x_ref[...]
        n_ref[...] = (xf * _rstd(xf) * g_ref[...]).astype(BF16)

    return pl.pallas_call(
        body, name=name, grid=(t // min(ROWS, t),), in_specs=[_row_spec(t, d), _vec_spec(d)],
        out_specs=_row_spec(t, d), out_shape=jax.ShapeDtypeStruct((t, d), BF16),
        compiler_params=_cparams("parallel"),
    )(x, gain)


def _post_res_pre(name, x, ff, g_post, g_next, scale):
    t, d = x.shape

    def body(x_ref, ff_ref, gp_ref, gn_ref, xo_ref, n_ref):
        ff_ = ff_ref[...]
        xn = x_ref[...] + scale * (ff_ * _rstd(ff_) * gp_ref[...])
        xo_ref[...] = xn
        n_ref[...] = (xn * _rstd(xn) * gn_ref[...]).astype(BF16)

    return pl.pallas_call(
        body, name=name, grid=(t // min(ROWS, t),),
        in_specs=[_row_spec(t, d), _row_spec(t, d), _vec_spec(d), _vec_spec(d)],
        out_specs=[_row_spec(t, d), _row_spec(t, d)],
        out_shape=[jax.ShapeDtypeStruct((t, d), F32), jax.ShapeDtypeStruct((t, d), BF16)],
        compiler_params=_cparams("parallel"),
    )(x, ff, g_post, g_next)


def _post_res_loss(name, x, ff, g_post, target, scale):
    t, d = x.shape

    def body(x_ref, ff_ref, gp_ref, tg_ref, dy_ref, loss_ref):
        ff_ = ff_ref[...]
        err = x_ref[...] + scale * (ff_ * _rstd(ff_) * gp_ref[...]) - tg_ref[...]
        dy_ref[...] = err / d
        part = 0.5 * jnp.sum(jnp.mean(err * err, axis=-1, keepdims=True), axis=0, keepdims=True)

        @pl.when(pl.program_id(0) == 0)
        def _():
            loss_ref[...] = jnp.zeros_like(loss_ref)

        loss_ref[...] += jnp.broadcast_to(part, loss_ref.shape)

    return pl.pallas_call(
        body, name=name, grid=(t // min(ROWS, t),),
        in_specs=[_row_spec(t, d), _row_spec(t, d), _vec_spec(d), _row_spec(t, d)],
        out_specs=[_row_spec(t, d), _vec_spec(LANES)],
        out_shape=[jax.ShapeDtypeStruct((t, d), F32), jax.ShapeDtypeStruct((1, LANES), F32)],
        compiler_params=_cparams("arbitrary"),
    )(x, ff, g_post, target)


def _post_bwd(name, dy, ff, g_post, scale):
    t, d = dy.shape

    def body(dy_ref, ff_ref, gp_ref, dff_ref, dg_ref):
        ff_ = ff_ref[...]
        r = _rstd(ff_)
        xh = ff_ * r
        dyn = scale * dy_ref[...]
        dxh = dyn * gp_ref[...]
        dff_ref[...] = (r * (dxh - xh * jnp.mean(dxh * xh, axis=-1, keepdims=True))).astype(BF16)

        @pl.when(pl.program_id(0) == 0)
        def _():
            dg_ref[...] = jnp.zeros_like(dg_ref)

        dg_ref[...] += jnp.sum(dyn * xh, axis=0, keepdims=True)

    return pl.pallas_call(
        body, name=name, grid=(t // min(ROWS, t),),
        in_specs=[_row_spec(t, d), _row_spec(t, d), _vec_spec(d)],
        out_specs=[_row_spec(t, d), _vec_spec(d)],
        out_shape=[jax.ShapeDtypeStruct((t, d), BF16), jax.ShapeDtypeStruct((1, d), F32)],
        compiler_params=_cparams("arbitrary"),
    )(dy, ff, g_post)


def _pre_bwd(name, dn, x, g_pre, dy, deps=()):
    t, d = x.shape

    def body(dn_ref, x_ref, g_ref, dy_ref, *rest):
        dx_ref, dg_ref = rest[-2:]
        xf = x_ref[...]
        r = _rstd(xf)
        xh = xf * r
        dnf = dn_ref[...].astype(F32)
        dxh = dnf * g_ref[...]
        dx_ref[...] = dy_ref[...] + r * (dxh - xh * jnp.mean(dxh * xh, axis=-1, keepdims=True))

        @pl.when(pl.program_id(0) == 0)
        def _():
            dg_ref[...] = jnp.zeros_like(dg_ref)

        dg_ref[...] += jnp.sum(dnf * xh, axis=0, keepdims=True)

    return pl.pallas_call(
        body, name=name, grid=(t // min(ROWS, t),),
        in_specs=[_row_spec(t, d), _row_spec(t, d), _vec_spec(d), _row_spec(t, d)] + [ANY] * len(deps),
        out_specs=[_row_spec(t, d), _vec_spec(d)],
        out_shape=[jax.ShapeDtypeStruct((t, d), F32), jax.ShapeDtypeStruct((1, d), F32)],
        compiler_params=_cparams("arbitrary"),
    )(dn, x, g_pre, dy, *deps)


def _bdot(a, b, ca, cb, precision=None):
    return lax.dot_general(a, b, (((ca,), (cb,)), ((0,), (0,))), preferred_element_type=F32, precision=precision)


def _tri_masks(g):
    row = lax.broadcasted_iota(jnp.int32, (g, CHUNK, CHUNK), 1)
    col = lax.broadcasted_iota(jnp.int32, (g, CHUNK, CHUNK), 2)
    return col <= row, col >= row


def _ones_matmul(ones_mat, val):
    hi = val.astype(BF16)
    lo = (val - hi.astype(F32)).astype(BF16)
    return _bdot(ones_mat, hi, 2, 1) + _bdot(ones_mat, lo, 2, 1)


def _hgrn_block(z, lb, q, v, cum_mat):
    sg = jax.nn.sigmoid(z)
    f = lb + (1.0 - lb) * sg
    lf = jnp.log(f)
    k = 1.0 - f
    a = _ones_matmul(cum_mat, lf)
    last = jnp.sum(lf, axis=1, keepdims=True)
    e_a = jnp.exp(a)
    e_na = jnp.exp(-a)
    e_t = jnp.exp(last - a)
    return dict(sg=sg, f=f, k=k, decay=jnp.exp(last), e_a=e_a, e_na=e_na, e_t=e_t,
                qd=q * e_a, kd=k * e_na, kt=k * e_t)


def _hgrn_states(state, kv, decay, order):
    entering = [None] * len(order)
    for g in order:
        entering[g] = state
        state = decay[g] * state + kv[g]
    return jnp.stack(entering, axis=0), state


def _hgrn_fwd(p, lb_f, lb_b, gain, n_heads):
    t = p.shape[0]
    w = n_heads * HEAD
    blk = min(16, t // CHUNK)
    rows_blk = blk * CHUNK
    n_blocks = t // rows_blk
    fin_rows = min(256, t)

    def body(q_ref, i_ref, zf_ref, zb_ref, g_ref, lbf_ref, lbb_ref, gain_ref, y_ref, o_ref, st_ref):
        low, up = _tri_masks(blk)
        m_low, m_up = low.astype(BF16), up.astype(BF16)
        o_ref[...] = jnp.zeros_like(o_ref)
        st_ref[...] = jnp.zeros_like(st_ref)

        def one(r0, z_ref, lb, slot, rev):
            rows = pl.ds(r0, rows_blk)
            split = lambda ref: ref[rows, :].reshape(blk, CHUNK, HEAD)
            q, v = split(q_ref), split(i_ref)
            c = _hgrn_block(split(z_ref), lb, q, v, m_up if rev else m_low)
            qd, kd, kt, vb = c["qd"].astype(BF16), c["kd"].astype(BF16), c["kt"].astype(BF16), v.astype(BF16)
            pm = jnp.where(up if rev else low, _bdot(qd, kd, 2, 2), 0.0).astype(BF16)
            kv = _bdot(vb, kt, 1, 1)
            order = range(blk - 1, -1, -1) if rev else range(blk)
            entering, st_ref[slot] = _hgrn_states(st_ref[slot], kv, c["decay"], order)
            o = _bdot(pm, vb, 2, 1) + _bdot(qd, entering.astype(BF16), 2, 2)
            o_ref[rows, :] += o.reshape(rows_blk, HEAD)

        def step(n, carry):
            one(pl.multiple_of(n * rows_blk, rows_blk), zf_ref, lbf_ref[...], 0, False)
            one(pl.multiple_of((n_blocks - 1 - n) * rows_blk, rows_blk), zb_ref, lbb_ref[...], 1, True)
            return carry

        lax.fori_loop(0, n_blocks, step, 0)

        def fin(n, carry):
            rows = pl.ds(pl.multiple_of(n * fin_rows, fin_rows), fin_rows)
            o = o_ref[rows, :]
            g = g_ref[rows, :]
            y_ref[rows, :] = (o * _rstd(o) * gain_ref[...] * (g * jax.nn.sigmoid(g))).astype(BF16)
            return carry

        lax.fori_loop(0, t // fin_rows, fin, 0)

    col = lambda grp: pl.BlockSpec((t, HEAD), lambda h: (0, grp * n_heads + h))
    vec = pl.BlockSpec((1, HEAD), lambda h: (0, h))
    out = pl.BlockSpec((t, HEAD), lambda h: (0, h))
    return pl.pallas_call(
        body, name="hgrn_fwd", grid=(n_heads,),
        in_specs=[col(0), col(1), col(2), col(3), col(4), vec, vec, vec],
        out_specs=[out, out],
        out_shape=[jax.ShapeDtypeStruct((t, w), BF16), jax.ShapeDtypeStruct((t, w), F32)],
        scratch_shapes=[pltpu.VMEM((2, HEAD, HEAD), F32)],
        compiler_params=_cparams("parallel"),
    )(p, p, p, p, p, lb_f, lb_b, gain)


def _hgrn_bwd(p, o_raw, dcat, lb_f, lb_b, gain, n_heads):
    t = p.shape[0]
    w = n_heads * HEAD
    n_chunks = t // CHUNK
    blk = min(16, n_chunks)
    rows_blk = blk * CHUNK
    n_blocks = t // rows_blk
    rb = min(256, t)

    def body(q_ref, i_ref, zf_ref, zb_ref, g_ref, o_ref, dy_ref, lbf_ref, lbb_ref, gain_ref,
             dq_ref, di_ref, dzf_ref, dzb_ref, dg_ref, dlbf_ref, dlbb_ref, dgain_ref,
             do_s, dq_s, dv_s, st_s, cur_s):
        low, up = _tri_masks(blk)
        m_low, m_up = low.astype(BF16), up.astype(BF16)
        rowid = lax.broadcasted_iota(jnp.int32, (blk, CHUNK, HEAD), 1)
        gain_v = gain_ref[...]

        def norm_bwd(n, dgain):
            rows = pl.ds(pl.multiple_of(n * rb, rb), rb)
            o = o_ref[rows, :]
            g = g_ref[rows, :]
            dy = dy_ref[rows, :]
            r = _rstd(o)
            oh = o * r
            sg = jax.nn.sigmoid(g)
            dg_ref[rows, :] = (dy * oh * gain_v * (sg * (1.0 + g * (1.0 - sg)))).astype(BF16)
            dno = dy * (g * sg)
            dxh = dno * gain_v
            do_s[rows, :] = r * (dxh - oh * jnp.mean(dxh * oh, axis=-1, keepdims=True))
            return dgain + jnp.sum(dno * oh, axis=0, keepdims=True)

        dgain_ref[...] = lax.fori_loop(0, t // rb, norm_bwd, jnp.zeros((1, HEAD), F32))
        def direction(z_ref, lb_ref, dz_ref, dlb_ref, rev):
            way = int(rev)
            lb = lb_ref[...]
            cum_mat = m_up if rev else m_low
            cum_mat_t = m_low if rev else m_up
            mask = up if rev else low
            last_row = 0 if rev else CHUNK - 1

            order = range(blk - 1, -1, -1) if rev else range(blk)

            def rows_of(j):
                bidx = (n_blocks - 1 - j) if rev else j
                return bidx, pl.ds(pl.multiple_of(bidx * rows_blk, rows_blk), rows_blk)

            def load(rows):
                split = lambda ref: ref[rows, :].reshape(blk, CHUNK, HEAD)
                q, v = split(q_ref), split(i_ref)
                return q, v, _hgrn_block(split(z_ref), lb, q, v, cum_mat)

            def sweep_fwd(j):
                bidx, rows = rows_of(j)
                _, v, c = load(rows)
                kv = _bdot(v.astype(BF16), c["kt"].astype(BF16), 1, 1)
                st_s[way, pl.ds(bidx * blk, blk)], cur_s[2 * way] = _hgrn_states(
                    cur_s[2 * way], kv, c["decay"], order)

            dlb_ref[...] = jnp.zeros_like(dlb_ref)

            def sweep_bwd(jj):
                bidx, rows = rows_of(n_blocks - 1 - jj)
                _, v, c = load(rows)
                st = st_s[way, pl.ds(bidx * blk, blk)]
                do = do_s[rows, :].reshape(blk, CHUNK, HEAD)
                qd, kd, kt, decay = c["qd"], c["kd"], c["kt"], c["decay"]
                qd_b, kd_b, kt_b = qd.astype(BF16), kd.astype(BF16), kt.astype(BF16)
                v_b, do_b, st_b = v.astype(BF16), do.astype(BF16), st.astype(BF16)
                pm = jnp.where(mask, _bdot(qd_b, kd_b, 2, 2), 0.0).astype(BF16)
                dpm = jnp.where(mask, _bdot(do_b, v_b, 2, 2), 0.0).astype(BF16)
                gq = _bdot(do_b, qd_b, 1, 1)
                dstate = cur_s[2 * way + 1]
                after = [None] * blk
                for g in reversed(order):
                    after[g] = dstate
                    dstate = gq[g] + decay[g] * dstate
                cur_s[2 * way + 1] = dstate
                dst = jnp.stack(after, axis=0)
                dst_b = dst.astype(BF16)
                dv = _bdot(pm, do_b, 1, 1) + _bdot(kt_b, dst_b, 2, 2)
                dqd = _bdot(dpm, kd_b, 2, 1) + _bdot(do_b, st_b, 2, 1)
                dkd = _bdot(dpm, qd_b, 1, 1)
                dkt = _bdot(v_b, dst_b, 2, 1)
                dlast = (jnp.sum(dkt * kt, axis=1, keepdims=True)
                         + decay * jnp.sum(dst * st, axis=1, keepdims=True))
                dq_s[way, rows, :] = (dqd * c["e_a"]).reshape(rows_blk, HEAD)
                dv_s[way, rows, :] = dv.reshape(rows_blk, HEAD)
                dk = dkd * c["e_na"] + dkt * c["e_t"]
                da = dqd * qd - dkd * kd - dkt * kt
                da = da + jnp.where(rowid == last_row, dlast, 0.0)
                dlf = _ones_matmul(cum_mat_t, da)
                df = dlf / c["f"] - dk
                sg = c["sg"]
                dz_ref[rows, :] = (df * (1.0 - lb) * (sg * (1.0 - sg))).reshape(rows_blk, HEAD).astype(BF16)
                dlb_ref[...] += jnp.sum((df * (1.0 - sg)).reshape(rows_blk, HEAD), axis=0, keepdims=True)

            return sweep_fwd, sweep_bwd

        ways = [direction(zf_ref, lbf_ref, dzf_ref, dlbf_ref, False),
                direction(zb_ref, lbb_ref, dzb_ref, dlbb_ref, True)]
        cur_s[...] = jnp.zeros_like(cur_s)
        for sweep in range(2):
            def both(j, carry):
                for way in ways:
                    way[sweep](j)
                return carry

            lax.fori_loop(0, n_blocks, both, 0)
        dq_ref[...] = (dq_s[0] + dq_s[1]).astype(BF16)
        di_ref[...] = (dv_s[0] + dv_s[1]).astype(BF16)

    col = lambda grp: pl.BlockSpec((t, HEAD), lambda h: (0, grp * n_heads + h))
    one = pl.BlockSpec((t, HEAD), lambda h: (0, h))
    vec = pl.BlockSpec((1, HEAD), lambda h: (0, h))
    big = jax.ShapeDtypeStruct((t, w), BF16)
    small = jax.ShapeDtypeStruct((1, w), F32)
    return pl.pallas_call(
        body, name="hgrn_bwd", grid=(n_heads,),
        in_specs=[col(0), col(1), col(2), col(3), col(4), one, one, vec, vec, vec],
        out_specs=[one] * 5 + [vec] * 3,
        out_shape=[big] * 5 + [small] * 3,
        scratch_shapes=[pltpu.VMEM((t, HEAD), F32), pltpu.VMEM((2, t, HEAD), F32), pltpu.VMEM((2, t, HEAD), F32),
                        pltpu.VMEM((2, n_chunks, HEAD, HEAD), F32), pltpu.VMEM((4, HEAD, HEAD), F32)],
        compiler_params=_cparams("parallel"),
    )(p, p, p, p, p, o_raw, dcat, lb_f, lb_b, gain)


def _t5_bucket_index():
    c = np.arange(WINDOW)[:, None]
    s = np.arange(KEY_SPAN)[None, :]
    rel = s - WINDOW - c
    nb = REL_BUCKETS // 2
    max_exact = nb // 2
    bucket = (rel > 0).astype(np.int32) * nb
    n = np.abs(rel)
    large = max_exact + (np.log(np.maximum(n, 1) / max_exact) / np.log(REL_MAX_DIST / max_exact)
                         * (nb - max_exact)).astype(np.int32)
    large = np.minimum(large, nb - 1)
    return bucket + np.where(n < max_exact, n, large).astype(np.int32)


def _bias_build(table, idx):
    n_attn = table.shape[1]

    def body(tab_ref, idx_ref, o_ref):
        h = pl.program_id(0)
        idx_v = idx_ref[...]
        acc = jnp.zeros((WINDOW, KEY_SPAN), F32)
        for b in range(REL_BUCKETS):
            acc = jnp.where(idx_v == b, tab_ref[b, h], acc)
        o_ref[...] = acc

    return pl.pallas_call(
        body, name="bias_build", grid=(n_attn,),
        in_specs=[pl.BlockSpec(memory_space=pltpu.SMEM), pl.BlockSpec((WINDOW, KEY_SPAN), lambda h: (0, 0))],
        out_specs=pl.BlockSpec((None, WINDOW, KEY_SPAN), lambda h: (h, 0, 0)),
        out_shape=jax.ShapeDtypeStruct((n_attn, WINDOW, KEY_SPAN), F32), compiler_params=_cparams("parallel"),
    )(table, idx)


def _bias_reduce(dbias, idx):
    n_attn = dbias.shape[0]

    def body(idx_ref, d_ref, o_ref):
        idx_v = idx_ref[...]
        dv = d_ref[...]
        rows = lax.broadcasted_iota(jnp.int32, (REL_BUCKETS, LANES), 0)
        acc = jnp.zeros((REL_BUCKETS, LANES), F32)
        for b in range(REL_BUCKETS):
            part = jnp.sum(jnp.where(idx_v == b, dv, 0.0), axis=1, keepdims=True)
            acc = jnp.where(rows == b, jnp.sum(part, axis=0, keepdims=True), acc)
        o_ref[...] = acc

    return pl.pallas_call(
        body, name="bias_reduce", grid=(n_attn,),
        in_specs=[pl.BlockSpec((WINDOW, KEY_SPAN), lambda h: (0, 0)),
                  pl.BlockSpec((None, WINDOW, KEY_SPAN), lambda h: (h, 0, 0))],
        out_specs=pl.BlockSpec((None, REL_BUCKETS, LANES), lambda h: (h, 0, 0)),
        out_shape=jax.ShapeDtypeStruct((n_attn, REL_BUCKETS, LANES), F32), compiler_params=_cparams("parallel"),
    )(idx, dbias)


def _attn_probs(q, kb, bias, sink, valid):
    s = _dot_nt(q, kb) / math.sqrt(HEAD) + bias
    s = jnp.where(valid, s, NEG_INF)
    m = jnp.maximum(jnp.max(s, axis=-1, keepdims=True), sink)
    e = jnp.exp(s - m)
    e_sink = jnp.exp(sink - m)
    den = jnp.sum(e, axis=-1, keepdims=True) + e_sink
    return e / den, e_sink / den


def _attn_valid(n, t, grp):
    c = lax.broadcasted_iota(jnp.int32, (grp * WINDOW, KEY_SPAN), 0) & (WINDOW - 1)
    s = lax.broadcasted_iota(jnp.int32, (grp * WINDOW, KEY_SPAN), 1)
    rel = s - WINDOW - c
    key_pos = n * WINDOW - WINDOW + s
    return (jnp.abs(rel) <= WINDOW) & (key_pos >= 0) & (key_pos < t)


def _stack_heads(ref, grp):
    return jnp.concatenate([ref[:, g * HEAD:(g + 1) * HEAD] for g in range(grp)], axis=0).astype(BF16)


def _sink_column(sink_ref, x, grp):
    return jnp.concatenate([jnp.full((WINDOW, 1), sink_ref[0, x * grp + g], F32) for g in range(grp)], axis=0)


def _attn_specs(t, n_hgrn, n_attn):
    grp = n_attn // KV_HEADS
    nb = t // WINDOW
    cq = 5 * n_hgrn
    ck = cq + n_attn
    cv = ck + KV_HEADS
    q_spec = pl.BlockSpec((WINDOW, grp * HEAD), lambda x, n: (n, cq // grp + x))
    kv = lambda base, off: pl.BlockSpec(
        (WINDOW, HEAD), lambda x, n: (jnp.clip(n + off, 0, nb - 1), base + x))
    band = [kv(ck, -1), kv(ck, 0), kv(ck, 1), kv(cv, -1), kv(cv, 0), kv(cv, 1)]
    bias_spec = pl.BlockSpec((grp, WINDOW, KEY_SPAN), lambda x, n: (x, 0, 0))
    sink_spec = pl.BlockSpec(memory_space=pltpu.SMEM)
    return grp, nb, q_spec, band, bias_spec, sink_spec


def _attn_fwd(p, bias, sink, n_hgrn, n_attn):
    t = p.shape[0]
    grp, nb, q_spec, band, bias_spec, sink_spec = _attn_specs(t, n_hgrn, n_attn)

    def body(q_ref, kp, kc, kn, vp, vc, vn, bias_ref, sink_ref, y_ref):
        x, n = pl.program_id(0), pl.program_id(1)
        kb = jnp.concatenate([kp[...], kc[...], kn[...]], axis=0).astype(BF16)
        vb = jnp.concatenate([vp[...], vc[...], vn[...]], axis=0).astype(BF16)
        pr, _ = _attn_probs(_stack_heads(q_ref, grp), kb, bias_ref[...].reshape(grp * WINDOW, KEY_SPAN),
                            _sink_column(sink_ref, x, grp), _attn_valid(n, t, grp))
        y = _dot(pr.astype(BF16), vb).astype(BF16)
        for g in range(grp):
            y_ref[:, g * HEAD:(g + 1) * HEAD] = y[g * WINDOW:(g + 1) * WINDOW]

    return pl.pallas_call(
        body, name="attn_fwd", grid=(KV_HEADS, nb),
        in_specs=[q_spec] + band + [bias_spec, sink_spec],
        out_specs=pl.BlockSpec((WINDOW, grp * HEAD), lambda x, n: (n, x)),
        out_shape=jax.ShapeDtypeStruct((t, n_attn * HEAD), BF16),
        compiler_params=_cparams("parallel", "parallel"),
    )(p, p, p, p, p, p, p, bias, sink)


def _attn_bwd(p, dcat, bias, sink, n_hgrn, n_attn, deps=()):
    t = p.shape[0]
    grp, nb, q_spec, band, bias_spec, sink_spec = _attn_specs(t, n_hgrn, n_attn)
    inv = 1.0 / math.sqrt(HEAD)

    def body(q_ref, kp, kc, kn, vp, vc, vn, bias_ref, sink_ref, do_ref, *rest):
        dq_ref, dk_ref, dv_ref, dbias_ref, dsink_ref, dk_s, dv_s = rest[-7:]
        x, n = pl.program_id(0), pl.program_id(1)

        @pl.when(n == 0)
        def _():
            dk_s[...] = jnp.zeros_like(dk_s)
            dv_s[...] = jnp.zeros_like(dv_s)
            dbias_ref[...] = jnp.zeros_like(dbias_ref)
            dsink_ref[...] = jnp.zeros_like(dsink_ref)

        kb = jnp.concatenate([kp[...], kc[...], kn[...]], axis=0).astype(BF16)
        vb = jnp.concatenate([vp[...], vc[...], vn[...]], axis=0).astype(BF16)
        q = _stack_heads(q_ref, grp)
        do = _stack_heads(do_ref, grp)
        pr, p_sink = _attn_probs(q, kb, bias_ref[...].reshape(grp * WINDOW, KEY_SPAN),
                                 _sink_column(sink_ref, x, grp), _attn_valid(n, t, grp))
        dpr = _dot_nt(do, vb)
        delta = jnp.sum(pr * dpr, axis=-1, keepdims=True)
        ds = pr * (dpr - delta)
        ds_b = ds.astype(BF16)
        dq = (_dot(ds_b, kb) * inv).astype(BF16)
        dsink = -p_sink * delta
        for g in range(grp):
            head = slice(g * WINDOW, (g + 1) * WINDOW)
            dq_ref[:, g * HEAD:(g + 1) * HEAD] = dq[head]
            dbias_ref[g] += ds[head]
            dsink_ref[g:g + 1, :] += jnp.broadcast_to(jnp.sum(dsink[head], axis=0, keepdims=True), (1, WINDOW))
        rows = pl.ds(pl.multiple_of(n * WINDOW, WINDOW), KEY_SPAN)
        dk_s[rows, :] += _dot_tn(ds_b, q) * inv
        dv_s[rows, :] += _dot_tn(pr.astype(BF16), do)

        @pl.when(n == nb - 1)
        def _():
            dk_ref[...] = dk_s[pl.ds(WINDOW, t), :].astype(BF16)
            dv_ref[...] = dv_s[pl.ds(WINDOW, t), :].astype(BF16)

    do_spec = pl.BlockSpec((WINDOW, grp * HEAD), lambda x, n: (n, n_hgrn // grp + x))
    kv_out = pl.BlockSpec((t, HEAD), lambda x, n: (0, x))
    return pl.pallas_call(
        body, name="attn_bwd", grid=(KV_HEADS, nb),
        in_specs=[q_spec] + band + [bias_spec, sink_spec, do_spec] + [ANY] * len(deps),
        out_specs=[pl.BlockSpec((WINDOW, grp * HEAD), lambda x, n: (n, x)), kv_out, kv_out,
                   bias_spec, pl.BlockSpec((None, grp, WINDOW), lambda x, n: (x, 0, 0))],
        out_shape=[jax.ShapeDtypeStruct((t, n_attn * HEAD), BF16),
                   jax.ShapeDtypeStruct((t, KV_HEADS * HEAD), BF16),
                   jax.ShapeDtypeStruct((t, KV_HEADS * HEAD), BF16),
                   jax.ShapeDtypeStruct((n_attn, WINDOW, KEY_SPAN), F32),
                   jax.ShapeDtypeStruct((KV_HEADS, grp, WINDOW), F32)],
        scratch_shapes=[pltpu.VMEM((t + 2 * WINDOW, HEAD), F32), pltpu.VMEM((t + 2 * WINDOW, HEAD), F32)],
        compiler_params=_cparams("parallel", "arbitrary"),
    )(p, p, p, p, p, p, p, bias, sink, dcat, *deps)


def _position():
    return lax.axis_index("x"), lax.axis_index("y"), lax.axis_index("c")


def _handshake(peers):
    barrier = pltpu.get_barrier_semaphore()
    for peer in peers:
        pl.semaphore_signal(barrier, inc=1, device_id=peer, device_id_type=MESH)
    pl.semaphore_wait(barrier, len(peers))


def _sequencer(name, collective_id, scratch_types):
    return functools.partial(
        pl.kernel, mesh=plsc.ScalarSubcoreMesh(axis_name="sc", num_cores=1), name=name,
        scratch_types=scratch_types, compiler_params=pltpu.CompilerParams(collective_id=collective_id))


def _all_gather(name, shard, collective_id):
    rows = shard.shape[0]
    assert rows % 2 == 0
    rh = rows // 2
    src = jax.new_ref(shard, memory_space=pltpu.MemorySpace.HBM)
    out = jax.empty_ref(jax.ShapeDtypeStruct((N_DEV,) + shard.shape, shard.dtype),
                        memory_space=pltpu.MemorySpace.HBM)
    n_copies = 11

    @_sequencer(name, collective_id, (pltpu.SemaphoreType.DMA((n_copies,)), pltpu.SemaphoreType.DMA((n_copies,)),
                                      pltpu.SemaphoreType.DMA))
    def launch(send_sems, recv_sems, local_sem):
        x, y, c = _position()
        sibling = (x, y, 1 - c)
        xn, yn, dg = (1 - x, y), (x, 1 - y), (1 - x, 1 - y)
        _handshake([sibling, (*xn, c), (*yn, c)])

        def part(ref, half):
            return ref if half is None else ref.at[pl.ds(half * rh, rh)]

        def slot(chip, core, half=None):
            return part(out.at[4 * chip[0] + 2 * chip[1] + core], half)

        def copy(k, chip, core, half, to, own=False):
            return pltpu.make_async_remote_copy(
                src_ref=part(src, half) if own else slot(chip, core, half), dst_ref=slot(chip, core, half),
                send_sem=send_sems.at[k], recv_sem=recv_sems.at[k], device_id=to, device_id_type=MESH)

        def landed(k, chip, core, half):
            copy(k, chip, core, half, (x, y, c)).wait_recv()

        mine = pltpu.make_async_copy(src, slot((x, y), c), local_sem)
        mine.start()
        sent = [copy(0, (x, y), c, None, sibling, own=True),
                copy(1, (x, y), c, 0, (*xn, c), own=True), copy(3, (x, y), c, 1, (*yn, c), own=True),
                copy(2, (x, y), c, 1, (*xn, c), own=True), copy(4, (x, y), c, 0, (*yn, c), own=True)]
        for cp in sent:
            cp.start()

        def then(cp):
            cp.start()
            sent.append(cp)

        landed(1, xn, c, 0)
        then(copy(5, xn, c, 0, (*yn, c)))
        landed(3, yn, c, 1)
        then(copy(6, yn, c, 1, (*xn, c)))
        landed(2, xn, c, 1)
        then(copy(7, xn, c, None, sibling))
        landed(4, yn, c, 0)
        then(copy(8, yn, c, None, sibling))
        landed(5, dg, c, 0)
        then(copy(9, dg, c, 0, sibling))
        landed(6, dg, c, 1)
        then(copy(10, dg, c, 1, sibling))
        landed(0, (x, y), 1 - c, None)
        landed(7, xn, 1 - c, None)
        landed(8, yn, 1 - c, None)
        landed(9, dg, 1 - c, 0)
        landed(10, dg, 1 - c, 1)
        for cp in sent:
            cp.wait_send()
        mine.wait()

    launch()
    return out[...]


HBM = pl.BlockSpec(memory_space=pltpu.HBM)
SEM = pl.BlockSpec(memory_space=pltpu.SEMAPHORE)
EFFECT = pltpu.SideEffectType.DATAFLOW_SIDE_EFFECTING


def _pair_copies(s_ref, land_ref, send_sems, recv_sems):
    x, y, c = _position()
    return [pltpu.make_async_remote_copy(
        src_ref=s_ref.at[2 * k + (1 - c)], dst_ref=land_ref.at[k], send_sem=send_sems.at[k],
        recv_sem=recv_sems.at[k], device_id=(x, y, 1 - c), device_id_type=MESH) for k in range(4)]


def _pair_start(name, stack):
    land_shape = (4,) + stack.shape[1:]

    def body(s_ref, land_ref, send_sems, recv_sems, s_thru, land_thru, token):
        for cp in _pair_copies(s_ref, land_ref, send_sems, recv_sems):
            cp.start()
        token[...] = jnp.zeros_like(token)

    return pl.pallas_call(
        body, name=name,
        out_shape=(pltpu.SemaphoreType.DMA((4,)), pltpu.SemaphoreType.DMA((4,)),
                   pltpu.HBM(stack.shape, stack.dtype), pltpu.HBM(land_shape, stack.dtype),
                   jax.ShapeDtypeStruct((8, LANES), F32)),
        in_specs=(HBM, HBM), out_specs=(SEM, SEM, HBM, HBM, pl.BlockSpec(memory_space=pltpu.VMEM)),
        input_output_aliases={0: 2, 1: 3}, compiler_params=pltpu.CompilerParams(has_side_effects=EFFECT),
    )(pltpu.with_memory_space_constraint(stack, pltpu.HBM),
      pltpu.with_memory_space_constraint(lax.empty(land_shape, stack.dtype), pltpu.HBM))


def _pair_wait(name, started, after):
    send_sems, recv_sems, s_thru, land_thru, _ = started

    def body(s_ref, land_ref, send_sems, recv_sems, after_ref, s_out, land_out):
        for cp in _pair_copies(s_ref, land_ref, send_sems, recv_sems):
            cp.wait_send()
            cp.wait_recv()

    return pl.pallas_call(
        body, name=name,
        out_shape=(pltpu.HBM(s_thru.shape, s_thru.dtype), pltpu.HBM(land_thru.shape, land_thru.dtype)),
        in_specs=(HBM, HBM, SEM, SEM, ANY), out_specs=(HBM, HBM), input_output_aliases={0: 0, 1: 1},
        compiler_params=pltpu.CompilerParams(has_side_effects=EFFECT),
    )(s_thru, land_thru, send_sems, recv_sems, after)


def _pair_sum(name, stack, other, core):
    _, r, c = stack.shape
    tr = _row_tile(r, 1024)

    def body(core_ref, a_ref, b_ref, o_ref):
        o_ref[...] = (a_ref[...].astype(F32) + b_ref[...].astype(F32)).astype(o_ref.dtype)

    grid_spec = pltpu.PrefetchScalarGridSpec(
        num_scalar_prefetch=1, grid=(4, r // tr),
        in_specs=[pl.BlockSpec((None, tr, c), lambda k, i, core_ref: (2 * k + core_ref[0], i, 0)),
                  pl.BlockSpec((None, tr, c), lambda k, i, core_ref: (k, i, 0))],
        out_specs=pl.BlockSpec((None, tr, c), lambda k, i, core_ref: (k, i, 0)))
    return pl.pallas_call(
        body, name=name, grid_spec=grid_spec, out_shape=jax.ShapeDtypeStruct((4, r, c), stack.dtype),
        compiler_params=_cparams("parallel", "parallel"),
    )(core, stack, other)


def _chip_exchange(name, sums, collective_id):
    src = jax.new_ref(sums, memory_space=pltpu.MemorySpace.HBM)
    out = jax.empty_ref(jax.ShapeDtypeStruct((3,) + sums.shape[1:], sums.dtype),
                        memory_space=pltpu.MemorySpace.HBM)

    @_sequencer(name, collective_id, (pltpu.SemaphoreType.DMA((3,)), pltpu.SemaphoreType.DMA((3,))))
    def launch(send_sems, recv_sems):
        x, y, c = _position()
        chips = [(1 - x, y), (x, 1 - y), (1 - x, 1 - y)]
        _handshake([(*chip, c) for chip in chips])
        copies = [pltpu.make_async_remote_copy(
            src_ref=src.at[2 * px + py], dst_ref=out.at[j], send_sem=send_sems.at[j],
            recv_sem=recv_sems.at[j], device_id=(px, py, c), device_id_type=MESH)
            for j, (px, py) in enumerate(chips)]
        for cp in copies:
            cp.start()
        for cp in copies:
            cp.wait()

    launch()
    return out[...]


def _small_rows(shapes):
    first, row = [], 0
    for r, c in shapes:
        first.append(row)
        row += r * (c // LANES) if c % LANES == 0 else r
        row = -(-row // 8) * 8
    return first, row


def _small_move(packed, row, ref, to_packed):
    r, c = ref.shape
    if c % LANES:
        if to_packed:
            packed[row:row + r, 0:c] = ref[...]
        else:
            ref[...] = packed[row:row + r, 0:c]
        return
    per = c // LANES
    for i in range(r):
        for j in range(per):
            at = row + i * per + j
            if to_packed:
                packed[at:at + 1, :] = ref[i:i + 1, j * LANES:(j + 1) * LANES]
            else:
                ref[i:i + 1, j * LANES:(j + 1) * LANES] = packed[at:at + 1, :]


def _small_step(name, parts, ws, ms, vs):
    n_par = len(ws)
    first, rows = _small_rows([p.shape for p in parts])

    def body(*refs):
        part_refs, refs = refs[:n_par + 1], refs[n_par + 1:]
        w_refs, m_refs, v_refs, refs = refs[:n_par], refs[n_par:2 * n_par], refs[2 * n_par:3 * n_par], refs[3 * n_par:]
        g_out, d_out, m_out, v_out = (refs[i * n_par:(i + 1) * n_par] for i in range(4))
        loss_out = refs[4 * n_par]
        mine, gather, w_p, m_p, v_p, d_p, send_sems, recv_sems = refs[4 * n_par + 1:]
        x, y, c = _position()
        me = 4 * x + 2 * y + c
        for buf in (mine, w_p, m_p, v_p):
            buf[...] = jnp.zeros_like(buf)
        for k, ref in enumerate(part_refs):
            _small_move(mine, first[k], ref, True)
        gather[me] = mine[...]
        copies = []
        for k in range(1, N_DEV):
            peer = (x ^ (k >> 2), y ^ ((k >> 1) & 1), c ^ (k & 1))
            copies.append(pltpu.make_async_remote_copy(
                src_ref=mine, dst_ref=gather.at[me], send_sem=send_sems.at[k - 1],
                recv_sem=recv_sems.at[k - 1], device_id=peer, device_id_type=MESH))
        for cp in copies:
            cp.start()
        for k in range(n_par):
            for packed, src in ((w_p, w_refs[k]), (m_p, m_refs[k]), (v_p, v_refs[k])):
                _small_move(packed, first[k], src, True)
        for k in range(1, N_DEV):
            peer_slot = 4 * (x ^ (k >> 2)) + 2 * (y ^ ((k >> 1) & 1)) + (c ^ (k & 1))
            pltpu.make_async_remote_copy(
                src_ref=mine, dst_ref=gather.at[peer_slot], send_sem=send_sems.at[k - 1],
                recv_sem=recv_sems.at[k - 1], device_id=(x, y, c), device_id_type=MESH).wait()
        acc = gather[0]
        for j in range(1, N_DEV):
            acc = acc + gather[j]
        delta, m_new, v_new = _adam_math(w_p[...], acc, m_p[...], v_p[...])
        mine[...] = acc
        d_p[...] = delta
        m_p[...] = m_new
        v_p[...] = v_new
        for k in range(n_par):
            for packed, dst in ((mine, g_out[k]), (d_p, d_out[k]), (m_p, m_out[k]), (v_p, v_out[k])):
                _small_move(packed, first[k], dst, False)
        _small_move(mine, first[n_par], loss_out, False)

    vm = pl.BlockSpec(memory_space=pltpu.VMEM)
    like = [jax.ShapeDtypeStruct(w.shape, F32) for w in ws]
    buf = pltpu.VMEM((rows, LANES), F32)
    outs = pl.pallas_call(
        body, name=name, in_specs=[vm] * (4 * n_par + 1), out_specs=[vm] * (4 * n_par + 1),
        out_shape=like * 4 + [jax.ShapeDtypeStruct((1, LANES), F32)],
        scratch_shapes=[buf, pltpu.VMEM((N_DEV, rows, LANES), F32), buf, buf, buf, buf,
                        pltpu.SemaphoreType.DMA((7,)), pltpu.SemaphoreType.DMA((7,))],
    )(*parts, *ws, *ms, *vs)
    return (outs[:n_par], outs[n_par:2 * n_par], outs[2 * n_par:3 * n_par], outs[3 * n_par:4 * n_par],
            outs[4 * n_par])


def _adam_math(w, g, m, v):
    m = ADAM_B1 * m + (1.0 - ADAM_B1) * g
    v = ADAM_B2 * v + (1.0 - ADAM_B2) * jnp.square(g)
    m_hat = m / (1.0 - ADAM_B1 ** ADAM_STEP)
    v_hat = v / (1.0 - ADAM_B2 ** ADAM_STEP)
    delta = -ADAM_LR * (m_hat / (jnp.sqrt(v_hat) + ADAM_EPS) + ADAM_WD * w)
    return delta, m, v


def _adam_shard(name, w, m, v, sums, recv, chip, deps=()):
    r, c = w.shape
    tr = _row_tile(r, 256)

    def body(chip_ref, w_ref, m_ref, v_ref, own_ref, r0_ref, r1_ref, r2_ref, *rest):
        g_out, d_out, m_out, v_out = rest[-4:]
        g = ((own_ref[...].astype(F32) + r0_ref[...].astype(F32)) + r1_ref[...].astype(F32)) + r2_ref[...].astype(F32)
        delta, m_new, v_new = _adam_math(w_ref[...], g, m_ref[...], v_ref[...])
        g_out[...] = g
        d_out[...] = delta
        m_out[...] = m_new
        v_out[...] = v_new

    plain = pl.BlockSpec((tr, c), lambda i, chip_ref: (i, 0))
    piece = lambda j: pl.BlockSpec((None, tr, c), lambda i, chip_ref: (j, i, 0))
    grid_spec = pltpu.PrefetchScalarGridSpec(
        num_scalar_prefetch=1, grid=(r // tr,),
        in_specs=[plain, plain, plain,
                  pl.BlockSpec((None, tr, c), lambda i, chip_ref: (chip_ref[0], i, 0)),
                  piece(0), piece(1), piece(2)] + [ANY] * len(deps),
        out_specs=[plain] * 4)
    shape = jax.ShapeDtypeStruct((r, c), F32)
    return pl.pallas_call(
        body, name=name, grid_spec=grid_spec, out_shape=[shape] * 4, compiler_params=_cparams("parallel"),
    )(chip, w, m, v, sums, recv, recv, recv, *deps)


def _reduce_scatter(tag, started, after, core, collective_id):
    grad_stack, other = _pair_wait("rs_pair_wait_" + tag, started, after)
    sums = _pair_sum("rs_sum_" + tag, grad_stack, other, core)
    return sums, _chip_exchange("rs_chip_" + tag, sums, collective_id)


SMALL = ("pre_norm_ffn1", "post_norm_ffn1", "pre_norm_mix", "post_norm_mix", "hgrn_lower_bounds_fwd",
         "hgrn_lower_bounds_bwd", "hgrn_out_norm", "attn_sink", "pre_norm_ffn2", "post_norm_ffn2", "rel_bias_table")
BIG = ("w_ffn1_gate_up", "w_ffn1_down", "w_mix_in", "w_mix_out", "w_ffn2_gate_up", "w_ffn2_down")
AG_ID = {n: 1 + i for i, n in enumerate(BIG)}
RS_ID = {n: 7 + i for i, n in enumerate(BIG)}
ORDER = ("pre_norm_ffn1", "post_norm_ffn1", "w_ffn1_gate_up", "w_ffn1_down", "pre_norm_mix", "post_norm_mix",
         "w_mix_in", "hgrn_lower_bounds_fwd", "hgrn_lower_bounds_bwd", "hgrn_out_norm", "attn_sink", "w_mix_out",
         "pre_norm_ffn2", "post_norm_ffn2", "w_ffn2_gate_up", "w_ffn2_down", "rel_bias_table")


def kernel(x, pre_norm_ffn1, post_norm_ffn1, w_ffn1_gate_up, w_ffn1_down, pre_norm_mix, post_norm_mix, w_mix_in, hgrn_lower_bounds_fwd, hgrn_lower_bounds_bwd, hgrn_out_norm, attn_sink, w_mix_out, pre_norm_ffn2, post_norm_ffn2, w_ffn2_gate_up, w_ffn2_down, rel_bias_table, loss_target, m_pre_norm_ffn1, m_post_norm_ffn1, m_w_ffn1_gate_up, m_w_ffn1_down, m_pre_norm_mix, m_post_norm_mix, m_w_mix_in, m_hgrn_lower_bounds_fwd, m_hgrn_lower_bounds_bwd, m_hgrn_out_norm, m_attn_sink, m_w_mix_out, m_pre_norm_ffn2, m_post_norm_ffn2, m_w_ffn2_gate_up, m_w_ffn2_down, m_rel_bias_table, v_pre_norm_ffn1, v_post_norm_ffn1, v_w_ffn1_gate_up, v_w_ffn1_down, v_pre_norm_mix, v_post_norm_mix, v_w_mix_in, v_hgrn_lower_bounds_fwd, v_hgrn_lower_bounds_bwd, v_hgrn_out_norm, v_attn_sink, v_w_mix_out, v_pre_norm_ffn2, v_post_norm_ffn2, v_w_ffn2_gate_up, v_w_ffn2_down, v_rel_bias_table):
    args = dict(locals())
    wts = {n: args[n] for n in ORDER}
    mom = {n: args["m_" + n] for n in ORDER}
    var = {n: args["v_" + n] for n in ORDER}

    x0 = x[0]
    target = loss_target[0]
    t, d = x0.shape
    n_hgrn = d // 2 // HEAD
    n_attn = (d - d // 2) // HEAD
    core = lax.axis_index("c").astype(jnp.int32).reshape(1)
    chip = (2 * lax.axis_index("x") + lax.axis_index("y")).astype(jnp.int32).reshape(1)

    def local(a, name):
        return jnp.transpose(a[0]) if name == "w_mix_in" else a[0]

    full = {n: _all_gather("ag_" + n, local(wts[n], n).astype(BF16), AG_ID[n]) for n in BIG}
    w_gu1, w_gu2 = full["w_ffn1_gate_up"], full["w_ffn2_gate_up"]
    w_d1 = full["w_ffn1_down"].reshape(-1, d)
    w_d2 = full["w_ffn2_down"].reshape(-1, d)
    w_out = full["w_mix_out"].reshape(-1, d)
    w_in_t = full["w_mix_in"].reshape(-1, d)

    g = {n: wts[n] for n in SMALL}
    lb_f = jax.nn.softmax(g["hgrn_lower_bounds_fwd"], axis=0)[0:1]
    lb_b = jax.nn.softmax(g["hgrn_lower_bounds_bwd"], axis=0)[0:1]
    bucket_idx = jnp.asarray(_t5_bucket_index())
    bias = _bias_build(g["rel_bias_table"], bucket_idx)

    n1 = _pre_norm("pre_norm1", x0, g["pre_norm_ffn1"])
    a1, gu1 = _ffn_up("ffn1_gate_up", n1, w_gu1)
    ff1 = _matmul("ffn1_down", a1, w_d1, mode="nn", out_dtype=F32)
    x1, h = _post_res_pre("res1", x0, ff1, g["post_norm_ffn1"], g["pre_norm_mix"], 0.5)
    p = _matmul("mix_in", h, w_in_t, mode="nt", out_dtype=F32, tm=2048, tn=512)
    y_h, o_raw = _hgrn_fwd(p, lb_f, lb_b, g["hgrn_out_norm"], n_hgrn)
    y_a = _attn_fwd(p, bias, g["attn_sink"], n_hgrn, n_attn)
    cat = jnp.concatenate([y_h, y_a], axis=1)
    mixed = _matmul("mix_out", cat, w_out, mode="nn", out_dtype=F32)
    x2, n2 = _post_res_pre("res2", x1, mixed, g["post_norm_mix"], g["pre_norm_ffn2"], 1.0)
    a2, gu2 = _ffn_up("ffn2_gate_up", n2, w_gu2)
    ff2 = _matmul("ffn2_down", a2, w_d2, mode="nn", out_dtype=F32)
    dy3, loss_part = _post_res_loss("res3_loss", x2, ff2, g["post_norm_ffn2"], target, 0.5)

    small_grad = {}
    scattered = {}

    pending = []

    def scatter(name, grad_stack):
        started = _pair_start("rs_pair_" + name, grad_stack)
        pending.append((name, started))
        return [started[4]]

    def settle(after):
        deps = []
        while pending:
            name, started = pending.pop(0)
            scattered[name] = _reduce_scatter(name, started, after, core, RS_ID[name])
            deps.append(scattered[name][0])
        return deps

    def ffn_bwd(tag, dy, ff, a, gu, n_in, x_in, w_gu, w_d, post_name, pre_name, gu_name, d_name):
        dff, small_grad[post_name] = _post_bwd("post_bwd" + tag, dy, ff, g[post_name], 0.5)
        dep = settle(dff)
        dep = scatter(d_name, _matmul("dw_down" + tag, a, dff, mode="tn", out_dtype=BF16,
                                      deps=dep).reshape(N_DEV, -1, d))
        dgu = _ffn_dact("d_act" + tag, dff, w_d, gu, deps=dep)
        dep = settle(dgu)
        dep = scatter(gu_name, _matmul("dw_gate_up" + tag, n_in, dgu, mode="tn", stack=True, halves=True,
                                       out_dtype=BF16, deps=dep))
        dn = _matmul("d_norm" + tag, dgu, w_gu, mode="nt", stack=True, halves=True, out_dtype=F32, deps=dep)
        dep = settle(dn)
        dx, small_grad[pre_name] = _pre_bwd("pre_bwd" + tag, dn, x_in, g[pre_name], dy, deps=dep)
        return dx

    dx2 = ffn_bwd("2", dy3, ff2, a2, gu2, n2, x2, w_gu2, w_d2, "post_norm_ffn2", "pre_norm_ffn2",
                  "w_ffn2_gate_up", "w_ffn2_down")

    dmixed, small_grad["post_norm_mix"] = _post_bwd("post_bwd_mix", dx2, mixed, g["post_norm_mix"], 1.0)
    dep = settle(dmixed)
    dcat = _matmul("d_cat", dmixed, w_out, mode="nt", out_dtype=F32, deps=dep)
    dep = scatter("w_mix_out", _matmul("dw_mix_out", cat, dmixed, mode="tn", out_dtype=BF16).reshape(N_DEV, -1, d))
    dq_a, dk_a, dv_a, dbias, dsink_rows = _attn_bwd(p, dcat, bias, g["attn_sink"], n_hgrn, n_attn, deps=dep)
    dq_h, di_h, dzf, dzb, dg_h, dlb_f, dlb_b, small_grad["hgrn_out_norm"] = _hgrn_bwd(
        p, o_raw, dcat, lb_f, lb_b, g["hgrn_out_norm"], n_hgrn)
    dp = jnp.concatenate([dq_h, di_h, dzf, dzb, dg_h, dq_a, dk_a, dv_a], axis=1)
    dep = settle(dp)
    dh = _matmul("d_h", dp, w_in_t, mode="nn", out_dtype=F32, deps=dep)
    dep = scatter("w_mix_in", _matmul("dw_mix_in", dp, h, mode="tn", out_dtype=BF16, tm=512,
                                         tn=2048).reshape(N_DEV, -1, d))
    dx1, small_grad["pre_norm_mix"] = _pre_bwd("pre_bwd_mix", dh, x1, g["pre_norm_mix"], dx2, deps=dep)

    dx0 = ffn_bwd("1", dx1, ff1, a1, gu1, n1, x0, w_gu1, w_d1, "post_norm_ffn1", "pre_norm_ffn1",
                  "w_ffn1_gate_up", "w_ffn1_down")
    settle(dx0)

    def lb_grad(dlb, lb):
        da0 = dlb * lb * (1.0 - lb)
        return jnp.concatenate([da0, -da0], axis=0)

    small_grad["hgrn_lower_bounds_fwd"] = lb_grad(dlb_f, lb_f)
    small_grad["hgrn_lower_bounds_bwd"] = lb_grad(dlb_b, lb_b)
    small_grad["attn_sink"] = dsink_rows[:, :, 0].reshape(1, n_attn)
    small_grad["rel_bias_table"] = jnp.transpose(_bias_reduce(dbias, bucket_idx)[:, :, 0])

    g_s, d_s, m_s, v_s, loss_row = _small_step(
        "small_step", [small_grad[n] for n in SMALL] + [loss_part], [wts[n] for n in SMALL],
        [mom[n] for n in SMALL], [var[n] for n in SMALL])
    loss = loss_row[0, 0]
    grads, delta, new_m, new_v = (dict(zip(SMALL, vals)) for vals in (g_s, d_s, m_s, v_s))

    dep = []
    for n in ("w_ffn2_down", "w_ffn2_gate_up", "w_mix_out", "w_mix_in", "w_ffn1_down", "w_ffn1_gate_up"):
        sums, recv = scattered[n]
        outs = _adam_shard("adam_" + n, local(wts[n], n), local(mom[n], n), local(var[n], n), sums, recv, chip,
                           deps=dep)
        dep = [outs[0]]
        grads[n], delta[n], new_m[n], new_v[n] = [local(o[None], n)[None] for o in outs]

    return (loss, dx0[None], *[grads[n] for n in ORDER], *[delta[n] for n in ORDER],
            *[new_m[n] for n in ORDER], *[new_v[n] for n in ORDER])
```

```python
import functools
import math

import numpy as np
import jax
import jax.numpy as jnp
from jax import lax
from jax.experimental import pallas as pl
from jax.experimental.pallas import tpu as pltpu
from jax.experimental.pallas import tpu_sc as plsc

F32 = jnp.float32
BF16 = jnp.bfloat16
MESH = pl.DeviceIdType.MESH

N_DEV = 8
EPS = 1e-6
NEG_INF = -1e30
HEAD = 128
CHUNK = 64
WINDOW = 128
KEY_SPAN = 3 * WINDOW
KV_HEADS = 2
REL_BUCKETS = 32
REL_MAX_DIST = 128
ADAM_LR, ADAM_B1, ADAM_B2, ADAM_EPS, ADAM_WD, ADAM_STEP = 0.001, 0.9, 0.999, 1e-08, 0.01, 10
LANES = 128
VMEM_LIMIT = 56 * 1024 * 1024
ANY = pl.BlockSpec(memory_space=pl.ANY)


def _cparams(*sem):
    return pltpu.CompilerParams(dimension_semantics=sem if sem else None, vmem_limit_bytes=VMEM_LIMIT)


def _dot(a, b):
    return jnp.dot(a, b, preferred_element_type=F32)


def _dot_nt(a, b):
    return lax.dot_general(a, b, (((1,), (1,)), ((), ())), preferred_element_type=F32)


def _dot_tn(a, b):
    return lax.dot_general(a, b, (((0,), (0,)), ((), ())), preferred_element_type=F32)


def _tile(dim, target):
    for c in (target, 1024, 512, 256, 128):
        if c <= target and dim % c == 0:
            return c
    return dim


def _row_tile(rows, target):
    fits = [c for c in range(16, min(rows, target) + 1, 16) if rows % c == 0]
    return max(fits) if fits else rows


K_WHOLE = 2048
K_STEP = 2816


def _k_tile(kd):
    if kd <= K_WHOLE:
        return kd
    return max(c for c in range(LANES, K_STEP + 1, LANES) if kd % c == 0)


def _matmul(name, a, b, *, mode, out_dtype, stack=False, halves=False, tm=1024, tn=1024, deps=()):
    grp = 1
    if mode == "nn":
        m, kd = a.shape
        n = b.shape[0] * b.shape[2] if stack else b.shape[1]
    elif mode == "nt":
        m = a.shape[-2]
        n, kd = (b.shape[1], b.shape[0] * b.shape[2]) if stack else b.shape
    else:
        kd, m = a.shape
        n = b.shape[-1] * (2 if halves else 1)
    if stack:
        n1 = b.shape[2] if mode != "tn" else n // N_DEV
        if mode == "nt":
            grp = 2 if 2 * n1 <= K_STEP else 1
            tk = grp * n1
        else:
            grp = 1 if n1 % LANES == 0 else 2
            tn = grp * n1
        assert (grp * n1) % LANES == 0
    per_half = N_DEV // 2 // grp
    tm = _tile(m, tm)
    if not (stack and mode in ("nn", "tn")):
        tn = _tile(n, tn)
    if not (stack and mode == "nt"):
        tk = _k_tile(kd)
    nk = kd // tk
    lead = None if grp == 1 else grp
    b_outer = nk == 1 and b.size > a.size
    grid = (n // tn, m // tm, nk) if b_outer else (m // tm, n // tn, nk)

    def spec(shape, index):
        return pl.BlockSpec(shape, (lambda g0, g1, k: index(g1, g0, k)) if b_outer else index)

    if mode == "nn":
        a_spec = spec((tm, tk), lambda i, j, k: (i, k))
        if stack:
            b_spec = spec((lead, tk, n1), lambda i, j, k: (j, k, 0))
        else:
            b_spec = spec((tk, tn), lambda i, j, k: (k, j))
        dot = _dot
    elif mode == "nt":
        if halves:
            a_spec = spec((None, tm, tk), lambda i, j, k: (k // per_half, i, k % per_half))
        else:
            a_spec = spec((tm, tk), lambda i, j, k: (i, k))
        if stack:
            b_spec = spec((lead, tn, n1), lambda i, j, k: (k, j, 0))
        else:
            b_spec = spec((tn, tk), lambda i, j, k: (j, k))
        dot = _dot_nt
    else:
        a_spec = spec((tk, tm), lambda i, j, k: (k, i))
        if halves:
            b_spec = spec((None, tk, tn), lambda i, j, k: (j // per_half, k, j % per_half))
        else:
            b_spec = spec((tk, tn), lambda i, j, k: (k, j))
        dot = _dot_tn
    if stack and mode == "tn":
        out_shape = jax.ShapeDtypeStruct((N_DEV, m, n1), out_dtype)
        o_spec = spec((lead, tm, n1), lambda i, j, k: (j, i, 0))
    else:
        out_shape = jax.ShapeDtypeStruct((m, n), out_dtype)
        o_spec = spec((tm, tn), lambda i, j, k: (i, j))
    b_grouped = stack and grp > 1 and mode != "tn"
    o_grouped = stack and grp > 1 and mode == "tn"

    def product(a_ref, b_ref):
        bmat = jnp.concatenate([b_ref[s] for s in range(grp)], axis=1) if b_grouped else b_ref[...]
        return dot(a_ref[...], bmat)

    def store(o_ref, val):
        if o_grouped:
            for s in range(grp):
                o_ref[s] = val[:, s * n1:(s + 1) * n1].astype(o_ref.dtype)
        else:
            o_ref[...] = val.astype(o_ref.dtype)

    def body_whole(a_ref, b_ref, *rest):
        store(rest[-1], product(a_ref, b_ref))

    def body_steps(a_ref, b_ref, *rest):
        o_ref, acc_ref = rest[-2:]
        k = pl.program_id(2)

        @pl.when(k == 0)
        def _():
            acc_ref[...] = product(a_ref, b_ref)

        @pl.when(k > 0)
        def _():
            acc_ref[...] += product(a_ref, b_ref)

        @pl.when(k == nk - 1)
        def _():
            store(o_ref, acc_ref[...])

    return pl.pallas_call(
        body_whole if nk == 1 else body_steps, name=name, grid=grid,
        in_specs=[a_spec, b_spec] + [ANY] * len(deps), out_specs=o_spec, out_shape=out_shape,
        scratch_shapes=[] if nk == 1 else [pltpu.VMEM((tm, tn), F32)],
        compiler_params=_cparams("parallel", "parallel", "arbitrary"),
    )(a, b, *deps)


def _col_parts(width, parts=2):
    groups = width // LANES
    parts = max(1, min(parts, groups // 2))
    bounds = [LANES * (groups * p // parts) for p in range(parts)] + [width]
    return [slice(bounds[p], bounds[p + 1]) for p in range(parts)]


def _ffn_up(name, n, w_stack):
    t, d = n.shape
    s, _, n1 = w_stack.shape
    half = s // 2
    tm = _tile(t, 512)

    def body(n_ref, wg_ref, wu_ref, act_ref, gu_ref):
        nv = n_ref[...]
        for cols in _col_parts(n1):
            gate = _dot(nv, wg_ref[:, cols])
            up = _dot(nv, wu_ref[:, cols])
            sg = jax.nn.sigmoid(gate)
            silu = gate * sg
            act_ref[:, cols] = (silu * up).astype(BF16)
            gu_ref[0, :, cols] = (up * (sg * (1.0 + gate * (1.0 - sg)))).astype(BF16)
            gu_ref[1, :, cols] = silu.astype(BF16)

    return pl.pallas_call(
        body, name=name, grid=(half, t // tm),
        in_specs=[pl.BlockSpec((tm, d), lambda j, i: (i, 0)),
                  pl.BlockSpec((None, d, n1), lambda j, i: (j, 0, 0)),
                  pl.BlockSpec((None, d, n1), lambda j, i: (half + j, 0, 0))],
        out_specs=[pl.BlockSpec((tm, n1), lambda j, i: (i, j)), pl.BlockSpec((2, tm, n1), lambda j, i: (0, i, j))],
        out_shape=[jax.ShapeDtypeStruct((t, half * n1), BF16), jax.ShapeDtypeStruct((2, t, half * n1), BF16)],
        compiler_params=_cparams("parallel", "parallel"),
    )(n, w_stack, w_stack)


def _ffn_dact(name, dff, w_d, gu, deps=()):
    t, d = dff.shape
    f = w_d.shape[0]
    tm = _tile(t, 512)
    tn = _tile(f, 1408)

    def body(dff_ref, w_ref, gu_ref, *rest):
        dgu_ref = rest[-1]
        da = _dot_nt(dff_ref[...], w_ref[...]).astype(BF16)
        dgu_ref[0] = da * gu_ref[0]
        dgu_ref[1] = da * gu_ref[1]

    pair = pl.BlockSpec((2, tm, tn), lambda j, i: (0, i, j))
    return pl.pallas_call(
        body, name=name, grid=(f // tn, t // tm),
        in_specs=[pl.BlockSpec((tm, d), lambda j, i: (i, 0)), pl.BlockSpec((tn, d), lambda j, i: (j, 0)), pair]
        + [ANY] * len(deps),
        out_specs=pair, out_shape=jax.ShapeDtypeStruct((2, t, f), BF16),
        compiler_params=_cparams("parallel", "parallel"),
    )(dff, w_d, gu, *deps)


ROWS = 256


def _rstd(xf):
    return lax.rsqrt(jnp.mean(xf * xf, axis=-1, keepdims=True) + EPS)


def _row_spec(t, d):
    return pl.BlockSpec((min(ROWS, t), d), lambda i: (i, 0))


def _vec_spec(d):
    return pl.BlockSpec((1, d), lambda i: (0, 0))


def _pre_norm(name, x, gain):
    t, d = x.shape

    def body(x_ref, g_ref, n_ref):
        xf = x_ref[...]
        n_ref[...] = (xf * _rstd(xf) * g_ref[...]).astype(BF16)

    return pl.pallas_call(
        body, name=name, grid=(t // min(ROWS, t),), in_specs=[_row_spec(t, d), _vec_spec(d)],
        out_specs=_row_spec(t, d), out_shape=jax.ShapeDtypeStruct((t, d), BF16),
        compiler_params=_cparams("parallel"),
    )(x, gain)


def _post_res_pre(name, x, ff, g_post, g_next, scale):
    t, d = x.shape

    def body(x_ref, ff_ref, gp_ref, gn_ref, xo_ref, n_ref):
        ff_ = ff_ref[...]
        xn = x_ref[...] + scale * (ff_ * _rstd(ff_) * gp_ref[...])
        xo_ref[...] = xn
        n_ref[...] = (xn * _rstd(xn) * gn_ref[...]).astype(BF16)

    return pl.pallas_call(
        body, name=name, grid=(t // min(ROWS, t),),
        in_specs=[_row_spec(t, d), _row_spec(t, d), _vec_spec(d), _vec_spec(d)],
        out_specs=[_row_spec(t, d), _row_spec(t, d)],
        out_shape=[jax.ShapeDtypeStruct((t, d), F32), jax.ShapeDtypeStruct((t, d), BF16)],
        compiler_params=_cparams("parallel"),
    )(x, ff, g_post, g_next)


def _post_res_loss(name, x, ff, g_post, target, scale):
    t, d = x.shape

    def body(x_ref, ff_ref, gp_ref, tg_ref, dy_ref, loss_ref):
        ff_ = ff_ref[...]
        err = x_ref[...] + scale * (ff_ * _rstd(ff_) * gp_ref[...]) - tg_ref[...]
        dy_ref[...] = err / d
        part = 0.5 * jnp.sum(jnp.mean(err * err, axis=-1, keepdims=True), axis=0, keepdims=True)

        @pl.when(pl.program_id(0) == 0)
        def _():
            loss_ref[...] = jnp.zeros_like(loss_ref)

        loss_ref[...] += jnp.broadcast_to(part, loss_ref.shape)

    return pl.pallas_call(
        body, name=name, grid=(t // min(ROWS, t),),
        in_specs=[_row_spec(t, d), _row_spec(t, d), _vec_spec(d), _row_spec(t, d)],
        out_specs=[_row_spec(t, d), _vec_spec(LANES)],
        out_shape=[jax.ShapeDtypeStruct((t, d), F32), jax.ShapeDtypeStruct((1, LANES), F32)],
        compiler_params=_cparams("arbitrary"),
    )(x, ff, g_post, target)


def _post_bwd(name, dy, ff, g_post, scale):
    t, d = dy.shape

    def body(dy_ref, ff_ref, gp_ref, dff_ref, dg_ref):
        ff_ = ff_ref[...]
        r = _rstd(ff_)
        xh = ff_ * r
        dyn = scale * dy_ref[...]
        dxh = dyn * gp_ref[...]
        dff_ref[...] = (r * (dxh - xh * jnp.mean(dxh * xh, axis=-1, keepdims=True))).astype(BF16)

        @pl.when(pl.program_id(0) == 0)
        def _():
            dg_ref[...] = jnp.zeros_like(dg_ref)

        dg_ref[...] += jnp.sum(dyn * xh, axis=0, keepdims=True)

    return pl.pallas_call(
        body, name=name, grid=(t // min(ROWS, t),),
        in_specs=[_row_spec(t, d), _row_spec(t, d), _vec_spec(d)],
        out_specs=[_row_spec(t, d), _vec_spec(d)],
        out_shape=[jax.ShapeDtypeStruct((t, d), BF16), jax.ShapeDtypeStruct((1, d), F32)],
        compiler_params=_cparams("arbitrary"),
    )(dy, ff, g_post)


def _pre_bwd(name, dn, x, g_pre, dy, deps=()):
    t, d = x.shape

    def body(dn_ref, x_ref, g_ref, dy_ref, *rest):
        dx_ref, dg_ref = rest[-2:]
        xf = x_ref[...]
        r = _rstd(xf)
        xh = xf * r
        dnf = dn_ref[...].astype(F32)
        dxh = dnf * g_ref[...]
        dx_ref[...] = dy_ref[...] + r * (dxh - xh * jnp.mean(dxh * xh, axis=-1, keepdims=True))

        @pl.when(pl.program_id(0) == 0)
        def _():
            dg_ref[...] = jnp.zeros_like(dg_ref)

        dg_ref[...] += jnp.sum(dnf * xh, axis=0, keepdims=True)

    return pl.pallas_call(
        body, name=name, grid=(t // min(ROWS, t),),
        in_specs=[_row_spec(t, d), _row_spec(t, d), _vec_spec(d), _row_spec(t, d)] + [ANY] * len(deps),
        out_specs=[_row_spec(t, d), _vec_spec(d)],
        out_shape=[jax.ShapeDtypeStruct((t, d), F32), jax.ShapeDtypeStruct((1, d), F32)],
        compiler_params=_cparams("arbitrary"),
    )(dn, x, g_pre, dy, *deps)


def _bdot(a, b, ca, cb, precision=None):
    return lax.dot_general(a, b, (((ca,), (cb,)), ((0,), (0,))), preferred_element_type=F32, precision=precision)


def _tri_masks(g):
    row = lax.broadcasted_iota(jnp.int32, (g, CHUNK, CHUNK), 1)
    col = lax.broadcasted_iota(jnp.int32, (g, CHUNK, CHUNK), 2)
    return col <= row, col >= row


def _ones_matmul(ones_mat, val):
    hi = val.astype(BF16)
    lo = (val - hi.astype(F32)).astype(BF16)
    return _bdot(ones_mat, hi, 2, 1) + _bdot(ones_mat, lo, 2, 1)


def _hgrn_block(z, lb, q, v, cum_mat):
    sg = jax.nn.sigmoid(z)
    f = lb + (1.0 - lb) * sg
    lf = jnp.log(f)
    k = 1.0 - f
    a = _ones_matmul(cum_mat, lf)
    last = jnp.sum(lf, axis=1, keepdims=True)
    e_a = jnp.exp(a)
    e_na = jnp.exp(-a)
    e_t = jnp.exp(last - a)
    return dict(sg=sg, f=f, k=k, decay=jnp.exp(last), e_a=e_a, e_na=e_na, e_t=e_t,
                qd=q * e_a, kd=k * e_na, kt=k * e_t)


def _hgrn_states(state, kv, decay, order):
    entering = [None] * len(order)
    for g in order:
        entering[g] = state
        state = decay[g] * state + kv[g]
    return jnp.stack(entering, axis=0), state


def _hgrn_fwd(p, lb_f, lb_b, gain, n_heads):
    t = p.shape[0]
    w = n_heads * HEAD
    blk = min(16, t // CHUNK)
    rows_blk = blk * CHUNK
    n_blocks = t // rows_blk
    fin_rows = min(256, t)

    def body(q_ref, i_ref, zf_ref, zb_ref, g_ref, lbf_ref, lbb_ref, gain_ref, y_ref, o_ref, st_ref):
        low, up = _tri_masks(blk)
        m_low, m_up = low.astype(BF16), up.astype(BF16)
        o_ref[...] = jnp.zeros_like(o_ref)
        st_ref[...] = jnp.zeros_like(st_ref)

        def one(r0, z_ref, lb, slot, rev):
            rows = pl.ds(r0, rows_blk)
            split = lambda ref: ref[rows, :].reshape(blk, CHUNK, HEAD)
            q, v = split(q_ref), split(i_ref)
            c = _hgrn_block(split(z_ref), lb, q, v, m_up if rev else m_low)
            qd, kd, kt, vb = c["qd"].astype(BF16), c["kd"].astype(BF16), c["kt"].astype(BF16), v.astype(BF16)
            pm = jnp.where(up if rev else low, _bdot(qd, kd, 2, 2), 0.0).astype(BF16)
            kv = _bdot(vb, kt, 1, 1)
            order = range(blk - 1, -1, -1) if rev else range(blk)
            entering, st_ref[slot] = _hgrn_states(st_ref[slot], kv, c["decay"], order)
            o = _bdot(pm, vb, 2, 1) + _bdot(qd, entering.astype(BF16), 2, 2)
            o_ref[rows, :] += o.reshape(rows_blk, HEAD)

        def step(n, carry):
            one(pl.multiple_of(n * rows_blk, rows_blk), zf_ref, lbf_ref[...], 0, False)
            one(pl.multiple_of((n_blocks - 1 - n) * rows_blk, rows_blk), zb_ref, lbb_ref[...], 1, True)
            return carry

        lax.fori_loop(0, n_blocks, step, 0)

        def fin(n, carry):
            rows = pl.ds(pl.multiple_of(n * fin_rows, fin_rows), fin_rows)
            o = o_ref[rows, :]
            g = g_ref[rows, :]
            y_ref[rows, :] = (o * _rstd(o) * gain_ref[...] * (g * jax.nn.sigmoid(g))).astype(BF16)
            return carry

        lax.fori_loop(0, t // fin_rows, fin, 0)

    col = lambda grp: pl.BlockSpec((t, HEAD), lambda h: (0, grp * n_heads + h))
    vec = pl.BlockSpec((1, HEAD), lambda h: (0, h))
    out = pl.BlockSpec((t, HEAD), lambda h: (0, h))
    return pl.pallas_call(
        body, name="hgrn_fwd", grid=(n_heads,),
        in_specs=[col(0), col(1), col(2), col(3), col(4), vec, vec, vec],
        out_specs=[out, out],
        out_shape=[jax.ShapeDtypeStruct((t, w), BF16), jax.ShapeDtypeStruct((t, w), F32)],
        scratch_shapes=[pltpu.VMEM((2, HEAD, HEAD), F32)],
        compiler_params=_cparams("parallel"),
    )(p, p, p, p, p, lb_f, lb_b, gain)


def _hgrn_bwd(p, o_raw, dcat, lb_f, lb_b, gain, n_heads):
    t = p.shape[0]
    w = n_heads * HEAD
    n_chunks = t // CHUNK
    blk = min(16, n_chunks)
    rows_blk = blk * CHUNK
    n_blocks = t // rows_blk
    rb = min(256, t)

    def body(q_ref, i_ref, zf_ref, zb_ref, g_ref, o_ref, dy_ref, lbf_ref, lbb_ref, gain_ref,
             dq_ref, di_ref, dzf_ref, dzb_ref, dg_ref, dlbf_ref, dlbb_ref, dgain_ref,
             do_s, dq_s, dv_s, st_s, cur_s):
        low, up = _tri_masks(blk)
        m_low, m_up = low.astype(BF16), up.astype(BF16)
        rowid = lax.broadcasted_iota(jnp.int32, (blk, CHUNK, HEAD), 1)
        gain_v = gain_ref[...]

        def norm_bwd(n, dgain):
            rows = pl.ds(pl.multiple_of(n * rb, rb), rb)
            o = o_ref[rows, :]
            g = g_ref[rows, :]
            dy = dy_ref[rows, :]
            r = _rstd(o)
            oh = o * r
            sg = jax.nn.sigmoid(g)
            dg_ref[rows, :] = (dy * oh * gain_v * (sg * (1.0 + g * (1.0 - sg)))).astype(BF16)
            dno = dy * (g * sg)
            dxh = dno * gain_v
            do_s[rows, :] = r * (dxh - oh * jnp.mean(dxh * oh, axis=-1, keepdims=True))
            return dgain + jnp.sum(dno * oh, axis=0, keepdims=True)

        dgain_ref[...] = lax.fori_loop(0, t // rb, norm_bwd, jnp.zeros((1, HEAD), F32))
        def direction(z_ref, lb_ref, dz_ref, dlb_ref, rev):
            way = int(rev)
            lb = lb_ref[...]
            cum_mat = m_up if rev else m_low
            cum_mat_t = m_low if rev else m_up
            mask = up if rev else low
            last_row = 0 if rev else CHUNK - 1

            order = range(blk - 1, -1, -1) if rev else range(blk)

            def rows_of(j):
                bidx = (n_blocks - 1 - j) if rev else j
                return bidx, pl.ds(pl.multiple_of(bidx * rows_blk, rows_blk), rows_blk)

            def load(rows):
                split = lambda ref: ref[rows, :].reshape(blk, CHUNK, HEAD)
                q, v = split(q_ref), split(i_ref)
                return q, v, _hgrn_block(split(z_ref), lb, q, v, cum_mat)

            def sweep_fwd(j):
                bidx, rows = rows_of(j)
                _, v, c = load(rows)
                kv = _bdot(v.astype(BF16), c["kt"].astype(BF16), 1, 1)
                st_s[way, pl.ds(bidx * blk, blk)], cur_s[2 * way] = _hgrn_states(
                    cur_s[2 * way], kv, c["decay"], order)

            dlb_ref[...] = jnp.zeros_like(dlb_ref)

            def sweep_bwd(jj):
                bidx, rows = rows_of(n_blocks - 1 - jj)
                _, v, c = load(rows)
                st = st_s[way, pl.ds(bidx * blk, blk)]
                do = do_s[rows, :].reshape(blk, CHUNK, HEAD)
                qd, kd, kt, decay = c["qd"], c["kd"], c["kt"], c["decay"]
                qd_b, kd_b, kt_b = qd.astype(BF16), kd.astype(BF16), kt.astype(BF16)
                v_b, do_b, st_b = v.astype(BF16), do.astype(BF16), st.astype(BF16)
                pm = jnp.where(mask, _bdot(qd_b, kd_b, 2, 2), 0.0).astype(BF16)
                dpm = jnp.where(mask, _bdot(do_b, v_b, 2, 2), 0.0).astype(BF16)
                gq = _bdot(do_b, qd_b, 1, 1)
                dstate = cur_s[2 * way + 1]
                after = [None] * blk
                for g in reversed(order):
                    after[g] = dstate
                    dstate = gq[g] + decay[g] * dstate
                cur_s[2 * way + 1] = dstate
                dst = jnp.stack(after, axis=0)
                dst_b = dst.astype(BF16)
                dv = _bdot(pm, do_b, 1, 1) + _bdot(kt_b, dst_b, 2, 2)
                dqd = _bdot(dpm, kd_b, 2, 1) + _bdot(do_b, st_b, 2, 1)
                dkd = _bdot(dpm, qd_b, 1, 1)
                dkt = _bdot(v_b, dst_b, 2, 1)
                dlast = (jnp.sum(dkt * kt, axis=1, keepdims=True)
                         + decay * jnp.sum(dst * st, axis=1, keepdims=True))
                dq_s[way, rows, :] = (dqd * c["e_a"]).reshape(rows_blk, HEAD)
                dv_s[way, rows, :] = dv.reshape(rows_blk, HEAD)
                dk = dkd * c["e_na"] + dkt * c["e_t"]
                da = dqd * qd - dkd * kd - dkt * kt
                da = da + jnp.where(rowid == last_row, dlast, 0.0)
                dlf = _ones_matmul(cum_mat_t, da)
                df = dlf / c["f"] - dk
                sg = c["sg"]
                dz_ref[rows, :] = (df * (1.0 - lb) * (sg * (1.0 - sg))).reshape(rows_blk, HEAD).astype(BF16)
                dlb_ref[...] += jnp.sum((df * (1.0 - sg)).reshape(rows_blk, HEAD), axis=0, keepdims=True)

            return sweep_fwd, sweep_bwd

        ways = [direction(zf_ref, lbf_ref, dzf_ref, dlbf_ref, False),
                direction(zb_ref, lbb_ref, dzb_ref, dlbb_ref, True)]
        cur_s[...] = jnp.zeros_like(cur_s)
        for sweep in range(2):
            def both(j, carry):
                for way in ways:
                    way[sweep](j)
                return carry

            lax.fori_loop(0, n_blocks, both, 0)
        dq_ref[...] = (dq_s[0] + dq_s[1]).astype(BF16)
        di_ref[...] = (dv_s[0] + dv_s[1]).astype(BF16)

    col = lambda grp: pl.BlockSpec((t, HEAD), lambda h: (0, grp * n_heads + h))
    one = pl.BlockSpec((t, HEAD), lambda h: (0, h))
    vec = pl.BlockSpec((1, HEAD), lambda h: (0, h))
    big = jax.ShapeDtypeStruct((t, w), BF16)
    small = jax.ShapeDtypeStruct((1, w), F32)
    return pl.pallas_call(
        body, name="hgrn_bwd", grid=(n_heads,),
        in_specs=[col(0), col(1), col(2), col(3), col(4), one, one, vec, vec, vec],
        out_specs=[one] * 5 + [vec] * 3,
        out_shape=[big] * 5 + [small] * 3,
        scratch_shapes=[pltpu.VMEM((t, HEAD), F32), pltpu.VMEM((2, t, HEAD), F32), pltpu.VMEM((2, t, HEAD), F32),
                        pltpu.VMEM((2, n_chunks, HEAD, HEAD), F32), pltpu.VMEM((4, HEAD, HEAD), F32)],
        compiler_params=_cparams("parallel"),
    )(p, p, p, p, p, o_raw, dcat, lb_f, lb_b, gain)


def _t5_bucket_index():
    c = np.arange(WINDOW)[:, None]
    s = np.arange(KEY_SPAN)[None, :]
    rel = s - WINDOW - c
    nb = REL_BUCKETS // 2
    max_exact = nb // 2
    bucket = (rel > 0).astype(np.int32) * nb
    n = np.abs(rel)
    large = max_exact + (np.log(np.maximum(n, 1) / max_exact) / np.log(REL_MAX_DIST / max_exact)
                         * (nb - max_exact)).astype(np.int32)
    large = np.minimum(large, nb - 1)
    return bucket + np.where(n < max_exact, n, large).astype(np.int32)


def _bias_build(table, idx):
    n_attn = table.shape[1]

    def body(tab_ref, idx_ref, o_ref):
        h = pl.program_id(0)
        idx_v = idx_ref[...]
        acc = jnp.zeros((WINDOW, KEY_SPAN), F32)
        for b in range(REL_BUCKETS):
            acc = jnp.where(idx_v == b, tab_ref[b, h], acc)
        o_ref[...] = acc

    return pl.pallas_call(
        body, name="bias_build", grid=(n_attn,),
        in_specs=[pl.BlockSpec(memory_space=pltpu.SMEM), pl.BlockSpec((WINDOW, KEY_SPAN), lambda h: (0, 0))],
        out_specs=pl.BlockSpec((None, WINDOW, KEY_SPAN), lambda h: (h, 0, 0)),
        out_shape=jax.ShapeDtypeStruct((n_attn, WINDOW, KEY_SPAN), F32), compiler_params=_cparams("parallel"),
    )(table, idx)


def _bias_reduce(dbias, idx):
    n_attn = dbias.shape[0]

    def body(idx_ref, d_ref, o_ref):
        idx_v = idx_ref[...]
        dv = d_ref[...]
        rows = lax.broadcasted_iota(jnp.int32, (REL_BUCKETS, LANES), 0)
        acc = jnp.zeros((REL_BUCKETS, LANES), F32)
        for b in range(REL_BUCKETS):
            part = jnp.sum(jnp.where(idx_v == b, dv, 0.0), axis=1, keepdims=True)
            acc = jnp.where(rows == b, jnp.sum(part, axis=0, keepdims=True), acc)
        o_ref[...] = acc

    return pl.pallas_call(
        body, name="bias_reduce", grid=(n_attn,),
        in_specs=[pl.BlockSpec((WINDOW, KEY_SPAN), lambda h: (0, 0)),
                  pl.BlockSpec((None, WINDOW, KEY_SPAN), lambda h: (h, 0, 0))],
        out_specs=pl.BlockSpec((None, REL_BUCKETS, LANES), lambda h: (h, 0, 0)),
        out_shape=jax.ShapeDtypeStruct((n_attn, REL_BUCKETS, LANES), F32), compiler_params=_cparams("parallel"),
    )(idx, dbias)


def _attn_probs(q, kb, bias, sink, valid):
    s = _dot_nt(q, kb) / math.sqrt(HEAD) + bias
    s = jnp.where(valid, s, NEG_INF)
    m = jnp.maximum(jnp.max(s, axis=-1, keepdims=True), sink)
    e = jnp.exp(s - m)
    e_sink = jnp.exp(sink - m)
    den = jnp.sum(e, axis=-1, keepdims=True) + e_sink
    return e / den, e_sink / den


def _attn_valid(n, t, grp):
    c = lax.broadcasted_iota(jnp.int32, (grp * WINDOW, KEY_SPAN), 0) & (WINDOW - 1)
    s = lax.broadcasted_iota(jnp.int32, (grp * WINDOW, KEY_SPAN), 1)
    rel = s - WINDOW - c
    key_pos = n * WINDOW - WINDOW + s
    return (jnp.abs(rel) <= WINDOW) & (key_pos >= 0) & (key_pos < t)


def _stack_heads(ref, grp):
    return jnp.concatenate([ref[:, g * HEAD:(g + 1) * HEAD] for g in range(grp)], axis=0).astype(BF16)


def _sink_column(sink_ref, x, grp):
    return jnp.concatenate([jnp.full((WINDOW, 1), sink_ref[0, x * grp + g], F32) for g in range(grp)], axis=0)


def _attn_specs(t, n_hgrn, n_attn):
    grp = n_attn // KV_HEADS
    nb = t // WINDOW
    cq = 5 * n_hgrn
    ck = cq + n_attn
    cv = ck + KV_HEADS
    q_spec = pl.BlockSpec((WINDOW, grp * HEAD), lambda x, n: (n, cq // grp + x))
    kv = lambda base, off: pl.BlockSpec(
        (WINDOW, HEAD), lambda x, n: (jnp.clip(n + off, 0, nb - 1), base + x))
    band = [kv(ck, -1), kv(ck, 0), kv(ck, 1), kv(cv, -1), kv(cv, 0), kv(cv, 1)]
    bias_spec = pl.BlockSpec((grp, WINDOW, KEY_SPAN), lambda x, n: (x, 0, 0))
    sink_spec = pl.BlockSpec(memory_space=pltpu.SMEM)
    return grp, nb, q_spec, band, bias_spec, sink_spec


def _attn_fwd(p, bias, sink, n_hgrn, n_attn):
    t = p.shape[0]
    grp, nb, q_spec, band, bias_spec, sink_spec = _attn_specs(t, n_hgrn, n_attn)

    def body(q_ref, kp, kc, kn, vp, vc, vn, bias_ref, sink_ref, y_ref):
        x, n = pl.program_id(0), pl.program_id(1)
        kb = jnp.concatenate([kp[...], kc[...], kn[...]], axis=0).astype(BF16)
        vb = jnp.concatenate([vp[...], vc[...], vn[...]], axis=0).astype(BF16)
        pr, _ = _attn_probs(_stack_heads(q_ref, grp), kb, bias_ref[...].reshape(grp * WINDOW, KEY_SPAN),
                            _sink_column(sink_ref, x, grp), _attn_valid(n, t, grp))
        y = _dot(pr.astype(BF16), vb).astype(BF16)
        for g in range(grp):
            y_ref[:, g * HEAD:(g + 1) * HEAD] = y[g * WINDOW:(g + 1) * WINDOW]

    return pl.pallas_call(
        body, name="attn_fwd", grid=(KV_HEADS, nb),
        in_specs=[q_spec] + band + [bias_spec, sink_spec],
        out_specs=pl.BlockSpec((WINDOW, grp * HEAD), lambda x, n: (n, x)),
        out_shape=jax.ShapeDtypeStruct((t, n_attn * HEAD), BF16),
        compiler_params=_cparams("parallel", "parallel"),
    )(p, p, p, p, p, p, p, bias, sink)


def _attn_bwd(p, dcat, bias, sink, n_hgrn, n_attn, deps=()):
    t = p.shape[0]
    grp, nb, q_spec, band, bias_spec, sink_spec = _attn_specs(t, n_hgrn, n_attn)
    inv = 1.0 / math.sqrt(HEAD)

    def body(q_ref, kp, kc, kn, vp, vc, vn, bias_ref, sink_ref, do_ref, *rest):
        dq_ref, dk_ref, dv_ref, dbias_ref, dsink_ref, dk_s, dv_s = rest[-7:]
        x, n = pl.program_id(0), pl.program_id(1)

        @pl.when(n == 0)
        def _():
            dk_s[...] = jnp.zeros_like(dk_s)
            dv_s[...] = jnp.zeros_like(dv_s)
            dbias_ref[...] = jnp.zeros_like(dbias_ref)
            dsink_ref[...] = jnp.zeros_like(dsink_ref)

        kb = jnp.concatenate([kp[...], kc[...], kn[...]], axis=0).astype(BF16)
        vb = jnp.concatenate([vp[...], vc[...], vn[...]], axis=0).astype(BF16)
        q = _stack_heads(q_ref, grp)
        do = _stack_heads(do_ref, grp)
        pr, p_sink = _attn_probs(q, kb, bias_ref[...].reshape(grp * WINDOW, KEY_SPAN),
                                 _sink_column(sink_ref, x, grp), _attn_valid(n, t, grp))
        dpr = _dot_nt(do, vb)
        delta = jnp.sum(pr * dpr, axis=-1, keepdims=True)
        ds = pr * (dpr - delta)
        ds_b = ds.astype(BF16)
        dq = (_dot(ds_b, kb) * inv).astype(BF16)
        dsink = -p_sink * delta
        for g in range(grp):
            head = slice(g * WINDOW, (g + 1) * WINDOW)
            dq_ref[:, g * HEAD:(g + 1) * HEAD] = dq[head]
            dbias_ref[g] += ds[head]
            dsink_ref[g:g + 1, :] += jnp.broadcast_to(jnp.sum(dsink[head], axis=0, keepdims=True), (1, WINDOW))
        rows = pl.ds(pl.multiple_of(n * WINDOW, WINDOW), KEY_SPAN)
        dk_s[rows, :] += _dot_tn(ds_b, q) * inv
        dv_s[rows, :] += _dot_tn(pr.astype(BF16), do)

        @pl.when(n == nb - 1)
        def _():
            dk_ref[...] = dk_s[pl.ds(WINDOW, t), :].astype(BF16)
            dv_ref[...] = dv_s[pl.ds(WINDOW, t), :].astype(BF16)

    do_spec = pl.BlockSpec((WINDOW, grp * HEAD), lambda x, n: (n, n_hgrn // grp + x))
    kv_out = pl.BlockSpec((t, HEAD), lambda x, n: (0, x))
    return pl.pallas_call(
        body, name="attn_bwd", grid=(KV_HEADS, nb),
        in_specs=[q_spec] + band + [bias_spec, sink_spec, do_spec] + [ANY] * len(deps),
        out_specs=[pl.BlockSpec((WINDOW, grp * HEAD), lambda x, n: (n, x)), kv_out, kv_out,
                   bias_spec, pl.BlockSpec((None, grp, WINDOW), lambda x, n: (x, 0, 0))],
        out_shape=[jax.ShapeDtypeStruct((t, n_attn * HEAD), BF16),
                   jax.ShapeDtypeStruct((t, KV_HEADS * HEAD), BF16),
                   jax.ShapeDtypeStruct((t, KV_HEADS * HEAD), BF16),
                   jax.ShapeDtypeStruct((n_attn, WINDOW, KEY_SPAN), F32),
                   jax.ShapeDtypeStruct((KV_HEADS, grp, WINDOW), F32)],
        scratch_shapes=[pltpu.VMEM((t + 2 * WINDOW, HEAD), F32), pltpu.VMEM((t + 2 * WINDOW, HEAD), F32)],
        compiler_params=_cparams("parallel", "arbitrary"),
    )(p, p, p, p, p, p, p, bias, sink, dcat, *deps)


def _position():
    return lax.axis_index("x"), lax.axis_index("y"), lax.axis_index("c")


def _handshake(peers):
    barrier = pltpu.get_barrier_semaphore()
    for peer in peers:
        pl.semaphore_signal(barrier, inc=1, device_id=peer, device_id_type=MESH)
    pl.semaphore_wait(barrier, len(peers))


def _sequencer(name, collective_id, scratch_types):
    return functools.partial(
        pl.kernel, mesh=plsc.ScalarSubcoreMesh(axis_name="sc", num_cores=1), name=name,
        scratch_types=scratch_types, compiler_params=pltpu.CompilerParams(collective_id=collective_id))


def _all_gather(name, shard, collective_id):
    rows = shard.shape[0]
    assert rows % 2 == 0
    rh = rows // 2
    src = jax.new_ref(shard, memory_space=pltpu.MemorySpace.HBM)
    out = jax.empty_ref(jax.ShapeDtypeStruct((N_DEV,) + shard.shape, shard.dtype),
                        memory_space=pltpu.MemorySpace.HBM)
    n_copies = 11

    @_sequencer(name, collective_id, (pltpu.SemaphoreType.DMA((n_copies,)), pltpu.SemaphoreType.DMA((n_copies,)),
                                      pltpu.SemaphoreType.DMA))
    def launch(send_sems, recv_sems, local_sem):
        x, y, c = _position()
        sibling = (x, y, 1 - c)
        xn, yn, dg = (1 - x, y), (x, 1 - y), (1 - x, 1 - y)
        _handshake([sibling, (*xn, c), (*yn, c)])

        def part(ref, half):
            return ref if half is None else ref.at[pl.ds(half * rh, rh)]

        def slot(chip, core, half=None):
            return part(out.at[4 * chip[0] + 2 * chip[1] + core], half)

        def copy(k, chip, core, half, to, own=False):
            return pltpu.make_async_remote_copy(
                src_ref=part(src, half) if own else slot(chip, core, half), dst_ref=slot(chip, core, half),
                send_sem=send_sems.at[k], recv_sem=recv_sems.at[k], device_id=to, device_id_type=MESH)

        def landed(k, chip, core, half):
            copy(k, chip, core, half, (x, y, c)).wait_recv()

        mine = pltpu.make_async_copy(src, slot((x, y), c), local_sem)
        mine.start()
        sent = [copy(0, (x, y), c, None, sibling, own=True),
                copy(1, (x, y), c, 0, (*xn, c), own=True), copy(3, (x, y), c, 1, (*yn, c), own=True),
                copy(2, (x, y), c, 1, (*xn, c), own=True), copy(4, (x, y), c, 0, (*yn, c), own=True)]
        for cp in sent:
            cp.start()

        def then(cp):
            cp.start()
            sent.append(cp)

        landed(1, xn, c, 0)
        then(copy(5, xn, c, 0, (*yn, c)))
        landed(3, yn, c, 1)
        then(copy(6, yn, c, 1, (*xn, c)))
        landed(2, xn, c, 1)
        then(copy(7, xn, c, None, sibling))
        landed(4, yn, c, 0)
        then(copy(8, yn, c, None, sibling))
        landed(5, dg, c, 0)
        then(copy(9, dg, c, 0, sibling))
        landed(6, dg, c, 1)
        then(copy(10, dg, c, 1, sibling))
        landed(0, (x, y), 1 - c, None)
        landed(7, xn, 1 - c, None)
        landed(8, yn, 1 - c, None)
        landed(9, dg, 1 - c, 0)
        landed(10, dg, 1 - c, 1)
        for cp in sent:
            cp.wait_send()
        mine.wait()

    launch()
    return out[...]


HBM = pl.BlockSpec(memory_space=pltpu.HBM)
SEM = pl.BlockSpec(memory_space=pltpu.SEMAPHORE)
EFFECT = pltpu.SideEffectType.DATAFLOW_SIDE_EFFECTING


def _pair_copies(s_ref, land_ref, send_sems, recv_sems):
    x, y, c = _position()
    return [pltpu.make_async_remote_copy(
        src_ref=s_ref.at[2 * k + (1 - c)], dst_ref=land_ref.at[k], send_sem=send_sems.at[k],
        recv_sem=recv_sems.at[k], device_id=(x, y, 1 - c), device_id_type=MESH) for k in range(4)]


def _pair_start(name, stack):
    land_shape = (4,) + stack.shape[1:]

    def body(s_ref, land_ref, send_sems, recv_sems, s_thru, land_thru, token):
        for cp in _pair_copies(s_ref, land_ref, send_sems, recv_sems):
            cp.start()
        token[...] = jnp.zeros_like(token)

    return pl.pallas_call(
        body, name=name,
        out_shape=(pltpu.SemaphoreType.DMA((4,)), pltpu.SemaphoreType.DMA((4,)),
                   pltpu.HBM(stack.shape, stack.dtype), pltpu.HBM(land_shape, stack.dtype),
                   jax.ShapeDtypeStruct((8, LANES), F32)),
        in_specs=(HBM, HBM), out_specs=(SEM, SEM, HBM, HBM, pl.BlockSpec(memory_space=pltpu.VMEM)),
        input_output_aliases={0: 2, 1: 3}, compiler_params=pltpu.CompilerParams(has_side_effects=EFFECT),
    )(pltpu.with_memory_space_constraint(stack, pltpu.HBM),
      pltpu.with_memory_space_constraint(lax.empty(land_shape, stack.dtype), pltpu.HBM))


def _pair_wait(name, started, after):
    send_sems, recv_sems, s_thru, land_thru, _ = started

    def body(s_ref, land_ref, send_sems, recv_sems, after_ref, s_out, land_out):
        for cp in _pair_copies(s_ref, land_ref, send_sems, recv_sems):
            cp.wait_send()
            cp.wait_recv()

    return pl.pallas_call(
        body, name=name,
        out_shape=(pltpu.HBM(s_thru.shape, s_thru.dtype), pltpu.HBM(land_thru.shape, land_thru.dtype)),
        in_specs=(HBM, HBM, SEM, SEM, ANY), out_specs=(HBM, HBM), input_output_aliases={0: 0, 1: 1},
        compiler_params=pltpu.CompilerParams(has_side_effects=EFFECT),
    )(s_thru, land_thru, send_sems, recv_sems, after)


def _pair_sum(name, stack, other, core):
    _, r, c = stack.shape
    tr = _row_tile(r, 1024)

    def body(core_ref, a_ref, b_ref, o_ref):
        o_ref[...] = (a_ref[...].astype(F32) + b_ref[...].astype(F32)).astype(o_ref.dtype)

    grid_spec = pltpu.PrefetchScalarGridSpec(
        num_scalar_prefetch=1, grid=(4, r // tr),
        in_specs=[pl.BlockSpec((None, tr, c), lambda k, i, core_ref: (2 * k + core_ref[0], i, 0)),
                  pl.BlockSpec((None, tr, c), lambda k, i, core_ref: (k, i, 0))],
        out_specs=pl.BlockSpec((None, tr, c), lambda k, i, core_ref: (k, i, 0)))
    return pl.pallas_call(
        body, name=name, grid_spec=grid_spec, out_shape=jax.ShapeDtypeStruct((4, r, c), stack.dtype),
        compiler_params=_cparams("parallel", "parallel"),
    )(core, stack, other)


def _chip_exchange(name, sums, collective_id):
    src = jax.new_ref(sums, memory_space=pltpu.MemorySpace.HBM)
    out = jax.empty_ref(jax.ShapeDtypeStruct((3,) + sums.shape[1:], sums.dtype),
                        memory_space=pltpu.MemorySpace.HBM)

    @_sequencer(name, collective_id, (pltpu.SemaphoreType.DMA((3,)), pltpu.SemaphoreType.DMA((3,))))
    def launch(send_sems, recv_sems):
        x, y, c = _position()
        chips = [(1 - x, y), (x, 1 - y), (1 - x, 1 - y)]
        _handshake([(*chip, c) for chip in chips])
        copies = [pltpu.make_async_remote_copy(
            src_ref=src.at[2 * px + py], dst_ref=out.at[j], send_sem=send_sems.at[j],
            recv_sem=recv_sems.at[j], device_id=(px, py, c), device_id_type=MESH)
            for j, (px, py) in enumerate(chips)]
        for cp in copies:
            cp.start()
        for cp in copies:
            cp.wait()

    launch()
    return out[...]


def _small_rows(shapes):
    first, row = [], 0
    for r, c in shapes:
        first.append(row)
        row += r * (c // LANES) if c % LANES == 0 else r
        row = -(-row // 8) * 8
    return first, row


def _small_move(packed, row, ref, to_packed):
    r, c = ref.shape
    if c % LANES:
        if to_packed:
            packed[row:row + r, 0:c] = ref[...]
        else:
            ref[...] = packed[row:row + r, 0:c]
        return
    per = c // LANES
    for i in range(r):
        for j in range(per):
            at = row + i * per + j
            if to_packed:
                packed[at:at + 1, :] = ref[i:i + 1, j * LANES:(j + 1) * LANES]
            else:
                ref[i:i + 1, j * LANES:(j + 1) * LANES] = packed[at:at + 1, :]


def _small_step(name, parts, ws, ms, vs):
    n_par = len(ws)
    first, rows = _small_rows([p.shape for p in parts])
    vm = pl.BlockSpec(memory_space=pltpu.VMEM)
    buf = pltpu.VMEM((rows, LANES), F32)

    def reduce_body(*refs):
        part_refs, (sum_out, mine, gather, send_sems, recv_sems) = refs[:n_par + 1], refs[n_par + 1:]
        x, y, c = _position()
        me = 4 * x + 2 * y + c
        mine[...] = jnp.zeros_like(mine)
        for k, ref in enumerate(part_refs):
            _small_move(mine, first[k], ref, True)
        gather[me] = mine[...]
        copies = []
        for k in range(1, N_DEV):
            peer = (x ^ (k >> 2), y ^ ((k >> 1) & 1), c ^ (k & 1))
            copies.append(pltpu.make_async_remote_copy(
                src_ref=mine, dst_ref=gather.at[me], send_sem=send_sems.at[k - 1],
                recv_sem=recv_sems.at[k - 1], device_id=peer, device_id_type=MESH))
        for cp in copies:
            cp.start()
        for k in range(1, N_DEV):
            peer_slot = 4 * (x ^ (k >> 2)) + 2 * (y ^ ((k >> 1) & 1)) + (c ^ (k & 1))
            pltpu.make_async_remote_copy(
                src_ref=mine, dst_ref=gather.at[peer_slot], send_sem=send_sems.at[k - 1],
                recv_sem=recv_sems.at[k - 1], device_id=(x, y, c), device_id_type=MESH).wait()
        acc = gather[0]
        for j in range(1, N_DEV):
            acc = acc + gather[j]
        sum_out[...] = acc

    summed = pl.pallas_call(
        reduce_body, name=name + "_reduce", in_specs=[vm] * (n_par + 1), out_specs=vm,
        out_shape=jax.ShapeDtypeStruct((rows, LANES), F32),
        scratch_shapes=[buf, pltpu.VMEM((N_DEV, rows, LANES), F32), pltpu.SemaphoreType.DMA((7,)),
                        pltpu.SemaphoreType.DMA((7,))],
    )(*parts)

    def adam_body(*refs):
        sum_ref, refs = refs[0], refs[1:]
        w_refs, m_refs, v_refs, refs = refs[:n_par], refs[n_par:2 * n_par], refs[2 * n_par:3 * n_par], refs[3 * n_par:]
        g_out, d_out, m_out, v_out = (refs[i * n_par:(i + 1) * n_par] for i in range(4))
        loss_out = refs[4 * n_par]
        w_p, m_p, v_p, d_p = refs[4 * n_par + 1:]
        for packed in (w_p, m_p, v_p):
            packed[...] = jnp.zeros_like(packed)
        for k in range(n_par):
            for packed, src in ((w_p, w_refs[k]), (m_p, m_refs[k]), (v_p, v_refs[k])):
                _small_move(packed, first[k], src, True)
        delta, m_new, v_new = _adam_math(w_p[...], sum_ref[...], m_p[...], v_p[...])
        d_p[...] = delta
        m_p[...] = m_new
        v_p[...] = v_new
        for k in range(n_par):
            for packed, dst in ((sum_ref, g_out[k]), (d_p, d_out[k]), (m_p, m_out[k]), (v_p, v_out[k])):
                _small_move(packed, first[k], dst, False)
        _small_move(sum_ref, first[n_par], loss_out, False)

    like = [jax.ShapeDtypeStruct(w.shape, F32) for w in ws]
    outs = pl.pallas_call(
        adam_body, name=name + "_adam", in_specs=[vm] * (3 * n_par + 1), out_specs=[vm] * (4 * n_par + 1),
        out_shape=like * 4 + [jax.ShapeDtypeStruct((1, LANES), F32)], scratch_shapes=[buf, buf, buf, buf],
    )(summed, *ws, *ms, *vs)
    return (outs[:n_par], outs[n_par:2 * n_par], outs[2 * n_par:3 * n_par], outs[3 * n_par:4 * n_par],
            outs[4 * n_par])


def _adam_math(w, g, m, v):
    m = ADAM_B1 * m + (1.0 - ADAM_B1) * g
    v = ADAM_B2 * v + (1.0 - ADAM_B2) * jnp.square(g)
    m_hat = m / (1.0 - ADAM_B1 ** ADAM_STEP)
    v_hat = v / (1.0 - ADAM_B2 ** ADAM_STEP)
    delta = -ADAM_LR * (m_hat / (jnp.sqrt(v_hat) + ADAM_EPS) + ADAM_WD * w)
    return delta, m, v


def _adam_shard(name, w, m, v, sums, recv, chip, deps=()):
    r, c = w.shape
    tr = _row_tile(r, 256)

    def body(chip_ref, w_ref, m_ref, v_ref, own_ref, r0_ref, r1_ref, r2_ref, *rest):
        g_out, d_out, m_out, v_out = rest[-4:]
        g = ((own_ref[...].astype(F32) + r0_ref[...].astype(F32)) + r1_ref[...].astype(F32)) + r2_ref[...].astype(F32)
        delta, m_new, v_new = _adam_math(w_ref[...], g, m_ref[...], v_ref[...])
        g_out[...] = g
        d_out[...] = delta
        m_out[...] = m_new
        v_out[...] = v_new

    plain = pl.BlockSpec((tr, c), lambda i, chip_ref: (i, 0))
    piece = lambda j: pl.BlockSpec((None, tr, c), lambda i, chip_ref: (j, i, 0))
    grid_spec = pltpu.PrefetchScalarGridSpec(
        num_scalar_prefetch=1, grid=(r // tr,),
        in_specs=[plain, plain, plain,
                  pl.BlockSpec((None, tr, c), lambda i, chip_ref: (chip_ref[0], i, 0)),
                  piece(0), piece(1), piece(2)] + [ANY] * len(deps),
        out_specs=[plain] * 4)
    shape = jax.ShapeDtypeStruct((r, c), F32)
    return pl.pallas_call(
        body, name=name, grid_spec=grid_spec, out_shape=[shape] * 4, compiler_params=_cparams("parallel"),
    )(chip, w, m, v, sums, recv, recv, recv, *deps)


def _reduce_scatter(tag, started, after, core, collective_id):
    grad_stack, other = _pair_wait("rs_pair_wait_" + tag, started, after)
    sums = _pair_sum("rs_sum_" + tag, grad_stack, other, core)
    return sums, _chip_exchange("rs_chip_" + tag, sums, collective_id)


SMALL = ("pre_norm_ffn1", "post_norm_ffn1", "pre_norm_mix", "post_norm_mix", "hgrn_lower_bounds_fwd",
         "hgrn_lower_bounds_bwd", "hgrn_out_norm", "attn_sink", "pre_norm_ffn2", "post_norm_ffn2", "rel_bias_table")
BIG = ("w_ffn1_gate_up", "w_ffn1_down", "w_mix_in", "w_mix_out", "w_ffn2_gate_up", "w_ffn2_down")
AG_ID = {n: 1 + i for i, n in enumerate(BIG)}
RS_ID = {n: 7 + i for i, n in enumerate(BIG)}
ORDER = ("pre_norm_ffn1", "post_norm_ffn1", "w_ffn1_gate_up", "w_ffn1_down", "pre_norm_mix", "post_norm_mix",
         "w_mix_in", "hgrn_lower_bounds_fwd", "hgrn_lower_bounds_bwd", "hgrn_out_norm", "attn_sink", "w_mix_out",
         "pre_norm_ffn2", "post_norm_ffn2", "w_ffn2_gate_up", "w_ffn2_down", "rel_bias_table")


def kernel(x, pre_norm_ffn1, post_norm_ffn1, w_ffn1_gate_up, w_ffn1_down, pre_norm_mix, post_norm_mix, w_mix_in, hgrn_lower_bounds_fwd, hgrn_lower_bounds_bwd, hgrn_out_norm, attn_sink, w_mix_out, pre_norm_ffn2, post_norm_ffn2, w_ffn2_gate_up, w_ffn2_down, rel_bias_table, loss_target, m_pre_norm_ffn1, m_post_norm_ffn1, m_w_ffn1_gate_up, m_w_ffn1_down, m_pre_norm_mix, m_post_norm_mix, m_w_mix_in, m_hgrn_lower_bounds_fwd, m_hgrn_lower_bounds_bwd, m_hgrn_out_norm, m_attn_sink, m_w_mix_out, m_pre_norm_ffn2, m_post_norm_ffn2, m_w_ffn2_gate_up, m_w_ffn2_down, m_rel_bias_table, v_pre_norm_ffn1, v_post_norm_ffn1, v_w_ffn1_gate_up, v_w_ffn1_down, v_pre_norm_mix, v_post_norm_mix, v_w_mix_in, v_hgrn_lower_bounds_fwd, v_hgrn_lower_bounds_bwd, v_hgrn_out_norm, v_attn_sink, v_w_mix_out, v_pre_norm_ffn2, v_post_norm_ffn2, v_w_ffn2_gate_up, v_w_ffn2_down, v_rel_bias_table):
    args = dict(locals())
    wts = {n: args[n] for n in ORDER}
    mom = {n: args["m_" + n] for n in ORDER}
    var = {n: args["v_" + n] for n in ORDER}

    x0 = x[0]
    target = loss_target[0]
    t, d = x0.shape
    n_hgrn = d // 2 // HEAD
    n_attn = (d - d // 2) // HEAD
    core = lax.axis_index("c").astype(jnp.int32).reshape(1)
    chip = (2 * lax.axis_index("x") + lax.axis_index("y")).astype(jnp.int32).reshape(1)

    def local(a, name):
        return jnp.transpose(a[0]) if name == "w_mix_in" else a[0]

    full = {n: _all_gather("ag_" + n, local(wts[n], n).astype(BF16), AG_ID[n]) for n in BIG}
    w_gu1, w_gu2 = full["w_ffn1_gate_up"], full["w_ffn2_gate_up"]
    w_d1 = full["w_ffn1_down"].reshape(-1, d)
    w_d2 = full["w_ffn2_down"].reshape(-1, d)
    w_out = full["w_mix_out"].reshape(-1, d)
    w_in_t = full["w_mix_in"].reshape(-1, d)

    g = {n: wts[n] for n in SMALL}
    lb_f = jax.nn.softmax(g["hgrn_lower_bounds_fwd"], axis=0)[0:1]
    lb_b = jax.nn.softmax(g["hgrn_lower_bounds_bwd"], axis=0)[0:1]
    bucket_idx = jnp.asarray(_t5_bucket_index())
    bias = _bias_build(g["rel_bias_table"], bucket_idx)

    n1 = _pre_norm("pre_norm1", x0, g["pre_norm_ffn1"])
    a1, gu1 = _ffn_up("ffn1_gate_up", n1, w_gu1)
    ff1 = _matmul("ffn1_down", a1, w_d1, mode="nn", out_dtype=F32)
    x1, h = _post_res_pre("res1", x0, ff1, g["post_norm_ffn1"], g["pre_norm_mix"], 0.5)
    p = _matmul("mix_in", h, w_in_t, mode="nt", out_dtype=F32, tm=2048, tn=512)
    y_h, o_raw = _hgrn_fwd(p, lb_f, lb_b, g["hgrn_out_norm"], n_hgrn)
    y_a = _attn_fwd(p, bias, g["attn_sink"], n_hgrn, n_attn)
    cat = jnp.concatenate([y_h, y_a], axis=1)
    mixed = _matmul("mix_out", cat, w_out, mode="nn", out_dtype=F32)
    x2, n2 = _post_res_pre("res2", x1, mixed, g["post_norm_mix"], g["pre_norm_ffn2"], 1.0)
    a2, gu2 = _ffn_up("ffn2_gate_up", n2, w_gu2)
    ff2 = _matmul("ffn2_down", a2, w_d2, mode="nn", out_dtype=F32)
    dy3, loss_part = _post_res_loss("res3_loss", x2, ff2, g["post_norm_ffn2"], target, 0.5)

    small_grad = {}
    scattered = {}

    pending = []

    def scatter(name, grad_stack):
        started = _pair_start("rs_pair_" + name, grad_stack)
        pending.append((name, started))
        return [started[4]]

    def settle(after, count=len(BIG)):
        deps = []
        while pending and count:
            name, started = pending.pop(0)
            scattered[name] = _reduce_scatter(name, started, after, core, RS_ID[name])
            deps.append(scattered[name][0])
            count -= 1
        return deps

    def ffn_bwd(tag, dy, ff, a, gu, n_in, x_in, w_gu, w_d, post_name, pre_name, gu_name, d_name, last):
        dff, small_grad[post_name] = _post_bwd("post_bwd" + tag, dy, ff, g[post_name], 0.5)
        dep = settle(dff)

        def dw_down(deps):
            return scatter(d_name, _matmul("dw_down" + tag, a, dff, mode="tn", out_dtype=BF16,
                                           deps=deps).reshape(N_DEV, -1, d))

        def dw_gate_up(deps):
            return scatter(gu_name, _matmul("dw_gate_up" + tag, n_in, dgu, mode="tn", stack=True, halves=True,
                                            out_dtype=BF16, deps=deps))

        if last:
            dgu = _ffn_dact("d_act" + tag, dff, w_d, gu, deps=dep)
            dep = dw_down(dw_gate_up([]))
            dep = settle(dep[0], count=1) + dep
        else:
            dep = dw_down(dep)
            dgu = _ffn_dact("d_act" + tag, dff, w_d, gu, deps=dep)
            dep = dw_gate_up(settle(dgu))
        dn = _matmul("d_norm" + tag, dgu, w_gu, mode="nt", stack=True, halves=True, out_dtype=F32, deps=dep)
        dep = settle(dn)
        dx, small_grad[pre_name] = _pre_bwd("pre_bwd" + tag, dn, x_in, g[pre_name], dy, deps=dep)
        return dx

    dx2 = ffn_bwd("2", dy3, ff2, a2, gu2, n2, x2, w_gu2, w_d2, "post_norm_ffn2", "pre_norm_ffn2",
                  "w_ffn2_gate_up", "w_ffn2_down", last=False)

    dmixed, small_grad["post_norm_mix"] = _post_bwd("post_bwd_mix", dx2, mixed, g["post_norm_mix"], 1.0)
    dep = settle(dmixed)
    dcat = _matmul("d_cat", dmixed, w_out, mode="nt", out_dtype=F32, deps=dep)
    dep = scatter("w_mix_out", _matmul("dw_mix_out", cat, dmixed, mode="tn", out_dtype=BF16).reshape(N_DEV, -1, d))
    dq_a, dk_a, dv_a, dbias, dsink_rows = _attn_bwd(p, dcat, bias, g["attn_sink"], n_hgrn, n_attn, deps=dep)
    dq_h, di_h, dzf, dzb, dg_h, dlb_f, dlb_b, small_grad["hgrn_out_norm"] = _hgrn_bwd(
        p, o_raw, dcat, lb_f, lb_b, g["hgrn_out_norm"], n_hgrn)
    dp = jnp.concatenate([dq_h, di_h, dzf, dzb, dg_h, dq_a, dk_a, dv_a], axis=1)
    dep = settle(dp)
    dh = _matmul("d_h", dp, w_in_t, mode="nn", out_dtype=F32, deps=dep)
    dep = scatter("w_mix_in", _matmul("dw_mix_in", dp, h, mode="tn", out_dtype=BF16, tm=512,
                                         tn=2048).reshape(N_DEV, -1, d))
    dx1, small_grad["pre_norm_mix"] = _pre_bwd("pre_bwd_mix", dh, x1, g["pre_norm_mix"], dx2, deps=dep)

    dx0 = ffn_bwd("1", dx1, ff1, a1, gu1, n1, x0, w_gu1, w_d1, "post_norm_ffn1", "pre_norm_ffn1",
                  "w_ffn1_gate_up", "w_ffn1_down", last=True)

    def lb_grad(dlb, lb):
        da0 = dlb * lb * (1.0 - lb)
        return jnp.concatenate([da0, -da0], axis=0)

    small_grad["hgrn_lower_bounds_fwd"] = lb_grad(dlb_f, lb_f)
    small_grad["hgrn_lower_bounds_bwd"] = lb_grad(dlb_b, lb_b)
    small_grad["attn_sink"] = dsink_rows[:, :, 0].reshape(1, n_attn)
    small_grad["rel_bias_table"] = jnp.transpose(_bias_reduce(dbias, bucket_idx)[:, :, 0])

    g_s, d_s, m_s, v_s, loss_row = _small_step(
        "small_step", [small_grad[n] for n in SMALL] + [loss_part], [wts[n] for n in SMALL],
        [mom[n] for n in SMALL], [var[n] for n in SMALL])
    loss = loss_row[0, 0]
    grads, delta, new_m, new_v = (dict(zip(SMALL, vals)) for vals in (g_s, d_s, m_s, v_s))

    dep = []
    for n in ("w_ffn2_down", "w_ffn2_gate_up", "w_mix_out", "w_mix_in", "w_ffn1_gate_up", "w_ffn1_down"):
        sums, recv = scattered[n]
        outs = _adam_shard("adam_" + n, local(wts[n], n), local(mom[n], n), local(var[n], n), sums, recv, chip,
                           deps=dep)
        dep = [outs[0]]
        grads[n], delta[n], new_m[n], new_v[n] = [local(o[None], n)[None] for o in outs]

    return (loss, dx0[None], *[grads[n] for n in ORDER], *[delta[n] for n in ORDER],
            *[new_m[n] for n in ORDER], *[new_v[n] for n in ORDER])
```

```python
import functools
import math

import numpy as np
import jax
import jax.numpy as jnp
from jax import lax
from jax.experimental import pallas as pl
from jax.experimental.pallas import tpu as pltpu
from jax.experimental.pallas import tpu_sc as plsc

F32 = jnp.float32
BF16 = jnp.bfloat16
MESH = pl.DeviceIdType.MESH

N_DEV = 8
EPS = 1e-6
NEG_INF = -1e30
HEAD = 128
CHUNK = 64
WINDOW = 128
KEY_SPAN = 3 * WINDOW
KV_HEADS = 2
REL_BUCKETS = 32
REL_MAX_DIST = 128
ADAM_LR, ADAM_B1, ADAM_B2, ADAM_EPS, ADAM_WD, ADAM_STEP = 0.001, 0.9, 0.999, 1e-08, 0.01, 10
LANES = 128
VMEM_LIMIT = 56 * 1024 * 1024
ANY = pl.BlockSpec(memory_space=pl.ANY)


def _cparams(*sem):
    return pltpu.CompilerParams(dimension_semantics=sem if sem else None, vmem_limit_bytes=VMEM_LIMIT)


def _dot(a, b):
    return jnp.dot(a, b, preferred_element_type=F32)


def _dot_nt(a, b):
    return lax.dot_general(a, b, (((1,), (1,)), ((), ())), preferred_element_type=F32)


def _dot_tn(a, b):
    return lax.dot_general(a, b, (((0,), (0,)), ((), ())), preferred_element_type=F32)


def _tile(dim, target):
    for c in (target, 1024, 512, 256, 128):
        if c <= target and dim % c == 0:
            return c
    return dim


def _row_tile(rows, target):
    fits = [c for c in range(16, min(rows, target) + 1, 16) if rows % c == 0]
    return max(fits) if fits else rows


K_WHOLE = 2048
K_STEP = 2816


def _k_tile(kd):
    if kd <= K_WHOLE:
        return kd
    return max(c for c in range(LANES, K_STEP + 1, LANES) if kd % c == 0)


def _matmul(name, a, b, *, mode, out_dtype, stack=False, halves=False, tm=1024, tn=1024, deps=()):
    grp = 1
    if mode == "nn":
        m, kd = a.shape
        n = b.shape[0] * b.shape[2] if stack else b.shape[1]
    elif mode == "nt":
        m = a.shape[-2]
        n, kd = (b.shape[1], b.shape[0] * b.shape[2]) if stack else b.shape
    else:
        kd, m = a.shape
        n = b.shape[-1] * (2 if halves else 1)
    if stack:
        n1 = b.shape[2] if mode != "tn" else n // N_DEV
        if mode == "nt":
            grp = 2 if 2 * n1 <= K_STEP else 1
            tk = grp * n1
        else:
            grp = 1 if n1 % LANES == 0 else 2
            tn = grp * n1
        assert (grp * n1) % LANES == 0
    per_half = N_DEV // 2 // grp
    tm = _tile(m, tm)
    if not (stack and mode in ("nn", "tn")):
        tn = _tile(n, tn)
    if not (stack and mode == "nt"):
        tk = _k_tile(kd)
    nk = kd // tk
    lead = None if grp == 1 else grp
    b_outer = nk == 1 and b.size > a.size
    grid = (n // tn, m // tm, nk) if b_outer else (m // tm, n // tn, nk)

    def spec(shape, index):
        return pl.BlockSpec(shape, (lambda g0, g1, k: index(g1, g0, k)) if b_outer else index)

    if mode == "nn":
        a_spec = spec((tm, tk), lambda i, j, k: (i, k))
        if stack:
            b_spec = spec((lead, tk, n1), lambda i, j, k: (j, k, 0))
        else:
            b_spec = spec((tk, tn), lambda i, j, k: (k, j))
        dot = _dot
    elif mode == "nt":
        if halves:
            a_spec = spec((None, tm, tk), lambda i, j, k: (k // per_half, i, k % per_half))
        else:
            a_spec = spec((tm, tk), lambda i, j, k: (i, k))
        if stack:
            b_spec = spec((lead, tn, n1), lambda i, j, k: (k, j, 0))
        else:
            b_spec = spec((tn, tk), lambda i, j, k: (j, k))
        dot = _dot_nt
    else:
        a_spec = spec((tk, tm), lambda i, j, k: (k, i))
        if halves:
            b_spec = spec((None, tk, tn), lambda i, j, k: (j // per_half, k, j % per_half))
        else:
            b_spec = spec((tk, tn), lambda i, j, k: (k, j))
        dot = _dot_tn
    if stack and mode == "tn":
        out_shape = jax.ShapeDtypeStruct((N_DEV, m, n1), out_dtype)
        o_spec = spec((lead, tm, n1), lambda i, j, k: (j, i, 0))
    else:
        out_shape = jax.ShapeDtypeStruct((m, n), out_dtype)
        o_spec = spec((tm, tn), lambda i, j, k: (i, j))
    b_grouped = stack and grp > 1 and mode != "tn"
    o_grouped = stack and grp > 1 and mode == "tn"

    def product(a_ref, b_ref):
        bmat = jnp.concatenate([b_ref[s] for s in range(grp)], axis=1) if b_grouped else b_ref[...]
        return dot(a_ref[...], bmat)

    def store(o_ref, val):
        if o_grouped:
            for s in range(grp):
                o_ref[s] = val[:, s * n1:(s + 1) * n1].astype(o_ref.dtype)
        else:
            o_ref[...] = val.astype(o_ref.dtype)

    def body_whole(a_ref, b_ref, *rest):
        store(rest[-1], product(a_ref, b_ref))

    def body_steps(a_ref, b_ref, *rest):
        o_ref, acc_ref = rest[-2:]
        k = pl.program_id(2)

        @pl.when(k == 0)
        def _():
            acc_ref[...] = product(a_ref, b_ref)

        @pl.when(k > 0)
        def _():
            acc_ref[...] += product(a_ref, b_ref)

        @pl.when(k == nk - 1)
        def _():
            store(o_ref, acc_ref[...])

    return pl.pallas_call(
        body_whole if nk == 1 else body_steps, name=name, grid=grid,
        in_specs=[a_spec, b_spec] + [ANY] * len(deps), out_specs=o_spec, out_shape=out_shape,
        scratch_shapes=[] if nk == 1 else [pltpu.VMEM((tm, tn), F32)],
        compiler_params=_cparams("parallel", "parallel", "arbitrary"),
    )(a, b, *deps)


def _col_parts(width, parts=2):
    groups = width // LANES
    parts = max(1, min(parts, groups // 2))
    bounds = [LANES * (groups * p // parts) for p in range(parts)] + [width]
    return [slice(bounds[p], bounds[p + 1]) for p in range(parts)]


def _ffn_up(name, n, w_stack):
    t, d = n.shape
    s, _, n1 = w_stack.shape
    half = s // 2
    tm = _tile(t, 512)

    def body(n_ref, wg_ref, wu_ref, act_ref, gu_ref):
        nv = n_ref[...]
        for cols in _col_parts(n1):
            gate = _dot(nv, wg_ref[:, cols])
            up = _dot(nv, wu_ref[:, cols])
            sg = jax.nn.sigmoid(gate)
            silu = gate * sg
            act_ref[:, cols] = (silu * up).astype(BF16)
            gu_ref[0, :, cols] = (up * (sg * (1.0 + gate * (1.0 - sg)))).astype(BF16)
            gu_ref[1, :, cols] = silu.astype(BF16)

    return pl.pallas_call(
        body, name=name, grid=(half, t // tm),
        in_specs=[pl.BlockSpec((tm, d), lambda j, i: (i, 0)),
                  pl.BlockSpec((None, d, n1), lambda j, i: (j, 0, 0)),
                  pl.BlockSpec((None, d, n1), lambda j, i: (half + j, 0, 0))],
        out_specs=[pl.BlockSpec((tm, n1), lambda j, i: (i, j)), pl.BlockSpec((2, tm, n1), lambda j, i: (0, i, j))],
        out_shape=[jax.ShapeDtypeStruct((t, half * n1), BF16), jax.ShapeDtypeStruct((2, t, half * n1), BF16)],
        compiler_params=_cparams("parallel", "parallel"),
    )(n, w_stack, w_stack)


def _ffn_dact(name, dff, w_d, gu, deps=()):
    t, d = dff.shape
    f = w_d.shape[0]
    tm = _tile(t, 512)
    tn = _tile(f, 1408)

    def body(dff_ref, w_ref, gu_ref, *rest):
        dgu_ref = rest[-1]
        da = _dot_nt(dff_ref[...], w_ref[...]).astype(BF16)
        dgu_ref[0] = da * gu_ref[0]
        dgu_ref[1] = da * gu_ref[1]

    pair = pl.BlockSpec((2, tm, tn), lambda j, i: (0, i, j))
    return pl.pallas_call(
        body, name=name, grid=(f // tn, t // tm),
        in_specs=[pl.BlockSpec((tm, d), lambda j, i: (i, 0)), pl.BlockSpec((tn, d), lambda j, i: (j, 0)), pair]
        + [ANY] * len(deps),
        out_specs=pair, out_shape=jax.ShapeDtypeStruct((2, t, f), BF16),
        compiler_params=_cparams("parallel", "parallel"),
    )(dff, w_d, gu, *deps)


ROWS = 256


def _rstd(xf):
    return lax.rsqrt(jnp.mean(xf * xf, axis=-1, keepdims=True) + EPS)


def _row_spec(t, d):
    return pl.BlockSpec((min(ROWS, t), d), lambda i: (i, 0))


def _vec_spec(d):
    return pl.BlockSpec((1, d), lambda i: (0, 0))


def _pre_norm(name, x, gain):
    t, d = x.shape

    def body(x_ref, g_ref, n_ref):
        xf = x_ref[...]
        n_ref[...] = (xf * _rstd(xf) * g_ref[...]).astype(BF16)

    return pl.pallas_call(
        body, name=name, grid=(t // min(ROWS, t),), in_specs=[_row_spec(t, d), _vec_spec(d)],
        out_specs=_row_spec(t, d), out_shape=jax.ShapeDtypeStruct((t, d), BF16),
        compiler_params=_cparams("parallel"),
    )(x, gain)


def _post_res_pre(name, x, ff, g_post, g_next, scale):
    t, d = x.shape

    def body(x_ref, ff_ref, gp_ref, gn_ref, xo_ref, n_ref):
        ff_ = ff_ref[...]
        xn = x_ref[...] + scale * (ff_ * _rstd(ff_) * gp_ref[...])
        xo_ref[...] = xn
        n_ref[...] = (xn * _rstd(xn) * gn_ref[...]).astype(BF16)

    return pl.pallas_call(
        body, name=name, grid=(t // min(ROWS, t),),
        in_specs=[_row_spec(t, d), _row_spec(t, d), _vec_spec(d), _vec_spec(d)],
        out_specs=[_row_spec(t, d), _row_spec(t, d)],
        out_shape=[jax.ShapeDtypeStruct((t, d), F32), jax.ShapeDtypeStruct((t, d), BF16)],
        compiler_params=_cparams("parallel"),
    )(x, ff, g_post, g_next)


def _post_res_loss(name, x, ff, g_post, target, scale):
    t, d = x.shape

    def body(x_ref, ff_ref, gp_ref, tg_ref, dy_ref, loss_ref):
        ff_ = ff_ref[...]
        err = x_ref[...] + scale * (ff_ * _rstd(ff_) * gp_ref[...]) - tg_ref[...]
        dy_ref[...] = err / d
        part = 0.5 * jnp.sum(jnp.mean(err * err, axis=-1, keepdims=True), axis=0, keepdims=True)

        @pl.when(pl.program_id(0) == 0)
        def _():
            loss_ref[...] = jnp.zeros_like(loss_ref)

        loss_ref[...] += jnp.broadcast_to(part, loss_ref.shape)

    return pl.pallas_call(
        body, name=name, grid=(t // min(ROWS, t),),
        in_specs=[_row_spec(t, d), _row_spec(t, d), _vec_spec(d), _row_spec(t, d)],
        out_specs=[_row_spec(t, d), _vec_spec(LANES)],
        out_shape=[jax.ShapeDtypeStruct((t, d), F32), jax.ShapeDtypeStruct((1, LANES), F32)],
        compiler_params=_cparams("arbitrary"),
    )(x, ff, g_post, target)


def _post_bwd(name, dy, ff, g_post, scale):
    t, d = dy.shape

    def body(dy_ref, ff_ref, gp_ref, dff_ref, dg_ref):
        ff_ = ff_ref[...]
        r = _rstd(ff_)
        xh = ff_ * r
        dyn = scale * dy_ref[...]
        dxh = dyn * gp_ref[...]
        dff_ref[...] = (r * (dxh - xh * jnp.mean(dxh * xh, axis=-1, keepdims=True))).astype(BF16)

        @pl.when(pl.program_id(0) == 0)
        def _():
            dg_ref[...] = jnp.zeros_like(dg_ref)

        dg_ref[...] += jnp.sum(dyn * xh, axis=0, keepdims=True)

    return pl.pallas_call(
        body, name=name, grid=(t // min(ROWS, t),),
        in_specs=[_row_spec(t, d), _row_spec(t, d), _vec_spec(d)],
        out_specs=[_row_spec(t, d), _vec_spec(d)],
        out_shape=[jax.ShapeDtypeStruct((t, d), BF16), jax.ShapeDtypeStruct((1, d), F32)],
        compiler_params=_cparams("arbitrary"),
    )(dy, ff, g_post)


def _pre_bwd(name, dn, x, g_pre, dy, deps=()):
    t, d = x.shape

    def body(dn_ref, x_ref, g_ref, dy_ref, *rest):
        dx_ref, dg_ref = rest[-2:]
        xf = x_ref[...]
        r = _rstd(xf)
        xh = xf * r
        dnf = dn_ref[...].astype(F32)
        dxh = dnf * g_ref[...]
        dx_ref[...] = dy_ref[...] + r * (dxh - xh * jnp.mean(dxh * xh, axis=-1, keepdims=True))

        @pl.when(pl.program_id(0) == 0)
        def _():
            dg_ref[...] = jnp.zeros_like(dg_ref)

        dg_ref[...] += jnp.sum(dnf * xh, axis=0, keepdims=True)

    return pl.pallas_call(
        body, name=name, grid=(t // min(ROWS, t),),
        in_specs=[_row_spec(t, d), _row_spec(t, d), _vec_spec(d), _row_spec(t, d)] + [ANY] * len(deps),
        out_specs=[_row_spec(t, d), _vec_spec(d)],
        out_shape=[jax.ShapeDtypeStruct((t, d), F32), jax.ShapeDtypeStruct((1, d), F32)],
        compiler_params=_cparams("arbitrary"),
    )(dn, x, g_pre, dy, *deps)


def _bdot(a, b, ca, cb, precision=None):
    return lax.dot_general(a, b, (((ca,), (cb,)), ((0,), (0,))), preferred_element_type=F32, precision=precision)


def _tri_masks(g):
    row = lax.broadcasted_iota(jnp.int32, (g, CHUNK, CHUNK), 1)
    col = lax.broadcasted_iota(jnp.int32, (g, CHUNK, CHUNK), 2)
    return col <= row, col >= row


def _ones_matmul(ones_mat, val):
    hi = val.astype(BF16)
    lo = (val - hi.astype(F32)).astype(BF16)
    return _bdot(ones_mat, hi, 2, 1) + _bdot(ones_mat, lo, 2, 1)


def _hgrn_block(z, lb, q, v, cum_mat):
    sg = jax.nn.sigmoid(z)
    f = lb + (1.0 - lb) * sg
    lf = jnp.log(f)
    k = 1.0 - f
    a = _ones_matmul(cum_mat, lf)
    last = jnp.sum(lf, axis=1, keepdims=True)
    e_a = jnp.exp(a)
    e_na = jnp.exp(-a)
    e_t = jnp.exp(last - a)
    return dict(sg=sg, f=f, k=k, decay=jnp.exp(last), e_a=e_a, e_na=e_na, e_t=e_t,
                qd=q * e_a, kd=k * e_na, kt=k * e_t)


def _hgrn_states(state, kv, decay, order):
    entering = [None] * len(order)
    for g in order:
        entering[g] = state
        state = decay[g] * state + kv[g]
    return jnp.stack(entering, axis=0), state


def _hgrn_fwd(p, lb_f, lb_b, gain, n_heads):
    t = p.shape[0]
    w = n_heads * HEAD
    blk = min(16, t // CHUNK)
    rows_blk = blk * CHUNK
    n_blocks = t // rows_blk
    fin_rows = min(256, t)

    def body(q_ref, i_ref, zf_ref, zb_ref, g_ref, lbf_ref, lbb_ref, gain_ref, y_ref, o_ref, st_ref):
        low, up = _tri_masks(blk)
        m_low, m_up = low.astype(BF16), up.astype(BF16)
        o_ref[...] = jnp.zeros_like(o_ref)
        st_ref[...] = jnp.zeros_like(st_ref)

        def one(r0, z_ref, lb, slot, rev):
            rows = pl.ds(r0, rows_blk)
            split = lambda ref: ref[rows, :].reshape(blk, CHUNK, HEAD)
            q, v = split(q_ref), split(i_ref)
            c = _hgrn_block(split(z_ref), lb, q, v, m_up if rev else m_low)
            qd, kd, kt, vb = c["qd"].astype(BF16), c["kd"].astype(BF16), c["kt"].astype(BF16), v.astype(BF16)
            pm = jnp.where(up if rev else low, _bdot(qd, kd, 2, 2), 0.0).astype(BF16)
            kv = _bdot(vb, kt, 1, 1)
            order = range(blk - 1, -1, -1) if rev else range(blk)
            entering, st_ref[slot] = _hgrn_states(st_ref[slot], kv, c["decay"], order)
            o = _bdot(pm, vb, 2, 1) + _bdot(qd, entering.astype(BF16), 2, 2)
            o_ref[rows, :] += o.reshape(rows_blk, HEAD)

        def step(n, carry):
            one(pl.multiple_of(n * rows_blk, rows_blk), zf_ref, lbf_ref[...], 0, False)
            one(pl.multiple_of((n_blocks - 1 - n) * rows_blk, rows_blk), zb_ref, lbb_ref[...], 1, True)
            return carry

        lax.fori_loop(0, n_blocks, step, 0)

        def fin(n, carry):
            rows = pl.ds(pl.multiple_of(n * fin_rows, fin_rows), fin_rows)
            o = o_ref[rows, :]
            g = g_ref[rows, :]
            y_ref[rows, :] = (o * _rstd(o) * gain_ref[...] * (g * jax.nn.sigmoid(g))).astype(BF16)
            return carry

        lax.fori_loop(0, t // fin_rows, fin, 0)

    col = lambda grp: pl.BlockSpec((t, HEAD), lambda h: (0, grp * n_heads + h))
    vec = pl.BlockSpec((1, HEAD), lambda h: (0, h))
    out = pl.BlockSpec((t, HEAD), lambda h: (0, h))
    return pl.pallas_call(
        body, name="hgrn_fwd", grid=(n_heads,),
        in_specs=[col(0), col(1), col(2), col(3), col(4), vec, vec, vec],
        out_specs=[out, out],
        out_shape=[jax.ShapeDtypeStruct((t, w), BF16), jax.ShapeDtypeStruct((t, w), F32)],
        scratch_shapes=[pltpu.VMEM((2, HEAD, HEAD), F32)],
        compiler_params=_cparams("parallel"),
    )(p, p, p, p, p, lb_f, lb_b, gain)


def _hgrn_bwd(p, o_raw, dcat, lb_f, lb_b, gain, n_heads):
    t = p.shape[0]
    w = n_heads * HEAD
    n_chunks = t // CHUNK
    blk = min(16, n_chunks)
    rows_blk = blk * CHUNK
    n_blocks = t // rows_blk
    rb = min(256, t)

    def body(q_ref, i_ref, zf_ref, zb_ref, g_ref, o_ref, dy_ref, lbf_ref, lbb_ref, gain_ref,
             dq_ref, di_ref, dzf_ref, dzb_ref, dg_ref, dlbf_ref, dlbb_ref, dgain_ref,
             do_s, dq_s, dv_s, st_s, cur_s):
        low, up = _tri_masks(blk)
        m_low, m_up = low.astype(BF16), up.astype(BF16)
        rowid = lax.broadcasted_iota(jnp.int32, (blk, CHUNK, HEAD), 1)
        gain_v = gain_ref[...]

        def norm_bwd(n, dgain):
            rows = pl.ds(pl.multiple_of(n * rb, rb), rb)
            o = o_ref[rows, :]
            g = g_ref[rows, :]
            dy = dy_ref[rows, :]
            r = _rstd(o)
            oh = o * r
            sg = jax.nn.sigmoid(g)
            dg_ref[rows, :] = (dy * oh * gain_v * (sg * (1.0 + g * (1.0 - sg)))).astype(BF16)
            dno = dy * (g * sg)
            dxh = dno * gain_v
            do_s[rows, :] = r * (dxh - oh * jnp.mean(dxh * oh, axis=-1, keepdims=True))
            return dgain + jnp.sum(dno * oh, axis=0, keepdims=True)

        dgain_ref[...] = lax.fori_loop(0, t // rb, norm_bwd, jnp.zeros((1, HEAD), F32))
        def direction(z_ref, lb_ref, dz_ref, dlb_ref, rev):
            way = int(rev)
            lb = lb_ref[...]
            cum_mat = m_up if rev else m_low
            cum_mat_t = m_low if rev else m_up
            mask = up if rev else low
            last_row = 0 if rev else CHUNK - 1

            order = range(blk - 1, -1, -1) if rev else range(blk)

            def rows_of(j):
                bidx = (n_blocks - 1 - j) if rev else j
                return bidx, pl.ds(pl.multiple_of(bidx * rows_blk, rows_blk), rows_blk)

            def load(rows):
                split = lambda ref: ref[rows, :].reshape(blk, CHUNK, HEAD)
                q, v = split(q_ref), split(i_ref)
                return q, v, _hgrn_block(split(z_ref), lb, q, v, cum_mat)

            def sweep_fwd(j):
                bidx, rows = rows_of(j)
                _, v, c = load(rows)
                kv = _bdot(v.astype(BF16), c["kt"].astype(BF16), 1, 1)
                st_s[way, pl.ds(bidx * blk, blk)], cur_s[2 * way] = _hgrn_states(
                    cur_s[2 * way], kv, c["decay"], order)

            dlb_ref[...] = jnp.zeros_like(dlb_ref)

            def sweep_bwd(jj):
                bidx, rows = rows_of(n_blocks - 1 - jj)
                _, v, c = load(rows)
                st = st_s[way, pl.ds(bidx * blk, blk)]
                do = do_s[rows, :].reshape(blk, CHUNK, HEAD)
                qd, kd, kt, decay = c["qd"], c["kd"], c["kt"], c["decay"]
                qd_b, kd_b, kt_b = qd.astype(BF16), kd.astype(BF16), kt.astype(BF16)
                v_b, do_b, st_b = v.astype(BF16), do.astype(BF16), st.astype(BF16)
                pm = jnp.where(mask, _bdot(qd_b, kd_b, 2, 2), 0.0).astype(BF16)
                dpm = jnp.where(mask, _bdot(do_b, v_b, 2, 2), 0.0).astype(BF16)
                gq = _bdot(do_b, qd_b, 1, 1)
                dstate = cur_s[2 * way + 1]
                after = [None] * blk
                for g in reversed(order):
                    after[g] = dstate
                    dstate = gq[g] + decay[g] * dstate
                cur_s[2 * way + 1] = dstate
                dst = jnp.stack(after, axis=0)
                dst_b = dst.astype(BF16)
                dv = _bdot(pm, do_b, 1, 1) + _bdot(kt_b, dst_b, 2, 2)
                dqd = _bdot(dpm, kd_b, 2, 1) + _bdot(do_b, st_b, 2, 1)
                dkd = _bdot(dpm, qd_b, 1, 1)
                dkt = _bdot(v_b, dst_b, 2, 1)
                dlast = (jnp.sum(dkt * kt, axis=1, keepdims=True)
                         + decay * jnp.sum(dst * st, axis=1, keepdims=True))
                dq_s[way, rows, :] = (dqd * c["e_a"]).reshape(rows_blk, HEAD)
                dv_s[way, rows, :] = dv.reshape(rows_blk, HEAD)
                dk = dkd * c["e_na"] + dkt * c["e_t"]
                da = dqd * qd - dkd * kd - dkt * kt
                da = da + jnp.where(rowid == last_row, dlast, 0.0)
                dlf = _ones_matmul(cum_mat_t, da)
                df = dlf / c["f"] - dk
                sg = c["sg"]
                dz_ref[rows, :] = (df * (1.0 - lb) * (sg * (1.0 - sg))).reshape(rows_blk, HEAD).astype(BF16)
                dlb_ref[...] += jnp.sum((df * (1.0 - sg)).reshape(rows_blk, HEAD), axis=0, keepdims=True)

            return sweep_fwd, sweep_bwd

        ways = [direction(zf_ref, lbf_ref, dzf_ref, dlbf_ref, False),
                direction(zb_ref, lbb_ref, dzb_ref, dlbb_ref, True)]
        cur_s[...] = jnp.zeros_like(cur_s)
        for sweep in range(2):
            def both(j, carry):
                for way in ways:
                    way[sweep](j)
                return carry

            lax.fori_loop(0, n_blocks, both, 0)
        dq_ref[...] = (dq_s[0] + dq_s[1]).astype(BF16)
        di_ref[...] = (dv_s[0] + dv_s[1]).astype(BF16)

    col = lambda grp: pl.BlockSpec((t, HEAD), lambda h: (0, grp * n_heads + h))
    one = pl.BlockSpec((t, HEAD), lambda h: (0, h))
    vec = pl.BlockSpec((1, HEAD), lambda h: (0, h))
    big = jax.ShapeDtypeStruct((t, w), BF16)
    small = jax.ShapeDtypeStruct((1, w), F32)
    return pl.pallas_call(
        body, name="hgrn_bwd", grid=(n_heads,),
        in_specs=[col(0), col(1), col(2), col(3), col(4), one, one, vec, vec, vec],
        out_specs=[one] * 5 + [vec] * 3,
        out_shape=[big] * 5 + [small] * 3,
        scratch_shapes=[pltpu.VMEM((t, HEAD), F32), pltpu.VMEM((2, t, HEAD), F32), pltpu.VMEM((2, t, HEAD), F32),
                        pltpu.VMEM((2, n_chunks, HEAD, HEAD), F32), pltpu.VMEM((4, HEAD, HEAD), F32)],
        compiler_params=_cparams("parallel"),
    )(p, p, p, p, p, o_raw, dcat, lb_f, lb_b, gain)


def _t5_bucket_index():
    c = np.arange(WINDOW)[:, None]
    s = np.arange(KEY_SPAN)[None, :]
    rel = s - WINDOW - c
    nb = REL_BUCKETS // 2
    max_exact = nb // 2
    bucket = (rel > 0).astype(np.int32) * nb
    n = np.abs(rel)
    large = max_exact + (np.log(np.maximum(n, 1) / max_exact) / np.log(REL_MAX_DIST / max_exact)
                         * (nb - max_exact)).astype(np.int32)
    large = np.minimum(large, nb - 1)
    return bucket + np.where(n < max_exact, n, large).astype(np.int32)


def _bias_build(table, idx):
    n_attn = table.shape[1]

    def body(tab_ref, idx_ref, o_ref):
        h = pl.program_id(0)
        idx_v = idx_ref[...]
        acc = jnp.zeros((WINDOW, KEY_SPAN), F32)
        for b in range(REL_BUCKETS):
            acc = jnp.where(idx_v == b, tab_ref[b, h], acc)
        o_ref[...] = acc

    return pl.pallas_call(
        body, name="bias_build", grid=(n_attn,),
        in_specs=[pl.BlockSpec(memory_space=pltpu.SMEM), pl.BlockSpec((WINDOW, KEY_SPAN), lambda h: (0, 0))],
        out_specs=pl.BlockSpec((None, WINDOW, KEY_SPAN), lambda h: (h, 0, 0)),
        out_shape=jax.ShapeDtypeStruct((n_attn, WINDOW, KEY_SPAN), F32), compiler_params=_cparams("parallel"),
    )(table, idx)


def _bias_reduce(dbias, idx):
    n_attn = dbias.shape[0]

    def body(idx_ref, d_ref, o_ref):
        idx_v = idx_ref[...]
        dv = d_ref[...]
        rows = lax.broadcasted_iota(jnp.int32, (REL_BUCKETS, LANES), 0)
        acc = jnp.zeros((REL_BUCKETS, LANES), F32)
        for b in range(REL_BUCKETS):
            part = jnp.sum(jnp.where(idx_v == b, dv, 0.0), axis=1, keepdims=True)
            acc = jnp.where(rows == b, jnp.sum(part, axis=0, keepdims=True), acc)
        o_ref[...] = acc

    return pl.pallas_call(
        body, name="bias_reduce", grid=(n_attn,),
        in_specs=[pl.BlockSpec((WINDOW, KEY_SPAN), lambda h: (0, 0)),
                  pl.BlockSpec((None, WINDOW, KEY_SPAN), lambda h: (h, 0, 0))],
        out_specs=pl.BlockSpec((None, REL_BUCKETS, LANES), lambda h: (h, 0, 0)),
        out_shape=jax.ShapeDtypeStruct((n_attn, REL_BUCKETS, LANES), F32), compiler_params=_cparams("parallel"),
    )(idx, dbias)


def _attn_probs(q, kb, bias, sink, valid):
    s = _dot_nt(q, kb) / math.sqrt(HEAD) + bias
    s = jnp.where(valid, s, NEG_INF)
    m = jnp.maximum(jnp.max(s, axis=-1, keepdims=True), sink)
    e = jnp.exp(s - m)
    e_sink = jnp.exp(sink - m)
    den = jnp.sum(e, axis=-1, keepdims=True) + e_sink
    return e / den, e_sink / den


def _attn_valid(n, t, grp):
    c = lax.broadcasted_iota(jnp.int32, (grp * WINDOW, KEY_SPAN), 0) & (WINDOW - 1)
    s = lax.broadcasted_iota(jnp.int32, (grp * WINDOW, KEY_SPAN), 1)
    rel = s - WINDOW - c
    key_pos = n * WINDOW - WINDOW + s
    return (jnp.abs(rel) <= WINDOW) & (key_pos >= 0) & (key_pos < t)


def _stack_heads(ref, grp):
    return jnp.concatenate([ref[:, g * HEAD:(g + 1) * HEAD] for g in range(grp)], axis=0).astype(BF16)


def _sink_column(sink_ref, x, grp):
    return jnp.concatenate([jnp.full((WINDOW, 1), sink_ref[0, x * grp + g], F32) for g in range(grp)], axis=0)


def _attn_specs(t, n_hgrn, n_attn):
    grp = n_attn // KV_HEADS
    nb = t // WINDOW
    cq = 5 * n_hgrn
    ck = cq + n_attn
    cv = ck + KV_HEADS
    q_spec = pl.BlockSpec((WINDOW, grp * HEAD), lambda x, n: (n, cq // grp + x))
    kv = lambda base, off: pl.BlockSpec(
        (WINDOW, HEAD), lambda x, n: (jnp.clip(n + off, 0, nb - 1), base + x))
    band = [kv(ck, -1), kv(ck, 0), kv(ck, 1), kv(cv, -1), kv(cv, 0), kv(cv, 1)]
    bias_spec = pl.BlockSpec((grp, WINDOW, KEY_SPAN), lambda x, n: (x, 0, 0))
    sink_spec = pl.BlockSpec(memory_space=pltpu.SMEM)
    return grp, nb, q_spec, band, bias_spec, sink_spec


def _attn_fwd(p, bias, sink, n_hgrn, n_attn):
    t = p.shape[0]
    grp, nb, q_spec, band, bias_spec, sink_spec = _attn_specs(t, n_hgrn, n_attn)

    def body(q_ref, kp, kc, kn, vp, vc, vn, bias_ref, sink_ref, y_ref):
        x, n = pl.program_id(0), pl.program_id(1)
        kb = jnp.concatenate([kp[...], kc[...], kn[...]], axis=0).astype(BF16)
        vb = jnp.concatenate([vp[...], vc[...], vn[...]], axis=0).astype(BF16)
        pr, _ = _attn_probs(_stack_heads(q_ref, grp), kb, bias_ref[...].reshape(grp * WINDOW, KEY_SPAN),
                            _sink_column(sink_ref, x, grp), _attn_valid(n, t, grp))
        y = _dot(pr.astype(BF16), vb).astype(BF16)
        for g in range(grp):
            y_ref[:, g * HEAD:(g + 1) * HEAD] = y[g * WINDOW:(g + 1) * WINDOW]

    return pl.pallas_call(
        body, name="attn_fwd", grid=(KV_HEADS, nb),
        in_specs=[q_spec] + band + [bias_spec, sink_spec],
        out_specs=pl.BlockSpec((WINDOW, grp * HEAD), lambda x, n: (n, x)),
        out_shape=jax.ShapeDtypeStruct((t, n_attn * HEAD), BF16),
        compiler_params=_cparams("parallel", "parallel"),
    )(p, p, p, p, p, p, p, bias, sink)


def _attn_bwd(p, dcat, bias, sink, n_hgrn, n_attn, deps=()):
    t = p.shape[0]
    grp, nb, q_spec, band, bias_spec, sink_spec = _attn_specs(t, n_hgrn, n_attn)
    inv = 1.0 / math.sqrt(HEAD)

    def body(q_ref, kp, kc, kn, vp, vc, vn, bias_ref, sink_ref, do_ref, *rest):
        dq_ref, dk_ref, dv_ref, dbias_ref, dsink_ref, dk_s, dv_s = rest[-7:]
        x, n = pl.program_id(0), pl.program_id(1)

        @pl.when(n == 0)
        def _():
            dk_s[...] = jnp.zeros_like(dk_s)
            dv_s[...] = jnp.zeros_like(dv_s)
            dbias_ref[...] = jnp.zeros_like(dbias_ref)
            dsink_ref[...] = jnp.zeros_like(dsink_ref)

        kb = jnp.concatenate([kp[...], kc[...], kn[...]], axis=0).astype(BF16)
        vb = jnp.concatenate([vp[...], vc[...], vn[...]], axis=0).astype(BF16)
        q = _stack_heads(q_ref, grp)
        do = _stack_heads(do_ref, grp)
        pr, p_sink = _attn_probs(q, kb, bias_ref[...].reshape(grp * WINDOW, KEY_SPAN),
                                 _sink_column(sink_ref, x, grp), _attn_valid(n, t, grp))
        dpr = _dot_nt(do, vb)
        delta = jnp.sum(pr * dpr, axis=-1, keepdims=True)
        ds = pr * (dpr - delta)
        ds_b = ds.astype(BF16)
        dq = (_dot(ds_b, kb) * inv).astype(BF16)
        dsink = -p_sink * delta
        for g in range(grp):
            head = slice(g * WINDOW, (g + 1) * WINDOW)
            dq_ref[:, g * HEAD:(g + 1) * HEAD] = dq[head]
            dbias_ref[g] += ds[head]
            dsink_ref[g:g + 1, :] += jnp.broadcast_to(jnp.sum(dsink[head], axis=0, keepdims=True), (1, WINDOW))
        rows = pl.ds(pl.multiple_of(n * WINDOW, WINDOW), KEY_SPAN)
        dk_s[rows, :] += _dot_tn(ds_b, q) * inv
        dv_s[rows, :] += _dot_tn(pr.astype(BF16), do)

        @pl.when(n == nb - 1)
        def _():
            dk_ref[...] = dk_s[pl.ds(WINDOW, t), :].astype(BF16)
            dv_ref[...] = dv_s[pl.ds(WINDOW, t), :].astype(BF16)

    do_spec = pl.BlockSpec((WINDOW, grp * HEAD), lambda x, n: (n, n_hgrn // grp + x))
    kv_out = pl.BlockSpec((t, HEAD), lambda x, n: (0, x))
    return pl.pallas_call(
        body, name="attn_bwd", grid=(KV_HEADS, nb),
        in_specs=[q_spec] + band + [bias_spec, sink_spec, do_spec] + [ANY] * len(deps),
        out_specs=[pl.BlockSpec((WINDOW, grp * HEAD), lambda x, n: (n, x)), kv_out, kv_out,
                   bias_spec, pl.BlockSpec((None, grp, WINDOW), lambda x, n: (x, 0, 0))],
        out_shape=[jax.ShapeDtypeStruct((t, n_attn * HEAD), BF16),
                   jax.ShapeDtypeStruct((t, KV_HEADS * HEAD), BF16),
                   jax.ShapeDtypeStruct((t, KV_HEADS * HEAD), BF16),
                   jax.ShapeDtypeStruct((n_attn, WINDOW, KEY_SPAN), F32),
                   jax.ShapeDtypeStruct((KV_HEADS, grp, WINDOW), F32)],
        scratch_shapes=[pltpu.VMEM((t + 2 * WINDOW, HEAD), F32), pltpu.VMEM((t + 2 * WINDOW, HEAD), F32)],
        compiler_params=_cparams("parallel", "arbitrary"),
    )(p, p, p, p, p, p, p, bias, sink, dcat, *deps)


def _position():
    return lax.axis_index("x"), lax.axis_index("y"), lax.axis_index("c")


def _handshake(peers):
    barrier = pltpu.get_barrier_semaphore()
    for peer in peers:
        pl.semaphore_signal(barrier, inc=1, device_id=peer, device_id_type=MESH)
    pl.semaphore_wait(barrier, len(peers))


def _sequencer(name, collective_id, scratch_types):
    return functools.partial(
        pl.kernel, mesh=plsc.ScalarSubcoreMesh(axis_name="sc", num_cores=1), name=name,
        scratch_types=scratch_types, compiler_params=pltpu.CompilerParams(collective_id=collective_id))


def _all_gather(name, shard, collective_id):
    rows = shard.shape[0]
    assert rows % 2 == 0
    rh = rows // 2
    src = jax.new_ref(shard, memory_space=pltpu.MemorySpace.HBM)
    out = jax.empty_ref(jax.ShapeDtypeStruct((N_DEV,) + shard.shape, shard.dtype),
                        memory_space=pltpu.MemorySpace.HBM)
    n_copies = 11

    @_sequencer(name, collective_id, (pltpu.SemaphoreType.DMA((n_copies,)), pltpu.SemaphoreType.DMA((n_copies,)),
                                      pltpu.SemaphoreType.DMA))
    def launch(send_sems, recv_sems, local_sem):
        x, y, c = _position()
        sibling = (x, y, 1 - c)
        xn, yn, dg = (1 - x, y), (x, 1 - y), (1 - x, 1 - y)
        _handshake([sibling, (*xn, c), (*yn, c)])

        def part(ref, half):
            return ref if half is None else ref.at[pl.ds(half * rh, rh)]

        def slot(chip, core, half=None):
            return part(out.at[4 * chip[0] + 2 * chip[1] + core], half)

        def copy(k, chip, core, half, to, own=False):
            return pltpu.make_async_remote_copy(
                src_ref=part(src, half) if own else slot(chip, core, half), dst_ref=slot(chip, core, half),
                send_sem=send_sems.at[k], recv_sem=recv_sems.at[k], device_id=to, device_id_type=MESH)

        def landed(k, chip, core, half):
            copy(k, chip, core, half, (x, y, c)).wait_recv()

        mine = pltpu.make_async_copy(src, slot((x, y), c), local_sem)
        mine.start()
        sent = [copy(0, (x, y), c, None, sibling, own=True),
                copy(1, (x, y), c, 0, (*xn, c), own=True), copy(3, (x, y), c, 1, (*yn, c), own=True),
                copy(2, (x, y), c, 1, (*xn, c), own=True), copy(4, (x, y), c, 0, (*yn, c), own=True)]
        for cp in sent:
            cp.start()

        def then(cp):
            cp.start()
            sent.append(cp)

        landed(1, xn, c, 0)
        then(copy(5, xn, c, 0, (*yn, c)))
        landed(3, yn, c, 1)
        then(copy(6, yn, c, 1, (*xn, c)))
        landed(2, xn, c, 1)
        then(copy(7, xn, c, None, sibling))
        landed(4, yn, c, 0)
        then(copy(8, yn, c, None, sibling))
        landed(5, dg, c, 0)
        then(copy(9, dg, c, 0, sibling))
        landed(6, dg, c, 1)
        then(copy(10, dg, c, 1, sibling))
        landed(0, (x, y), 1 - c, None)
        landed(7, xn, 1 - c, None)
        landed(8, yn, 1 - c, None)
        landed(9, dg, 1 - c, 0)
        landed(10, dg, 1 - c, 1)
        for cp in sent:
            cp.wait_send()
        mine.wait()

    launch()
    return out[...]


HBM = pl.BlockSpec(memory_space=pltpu.HBM)
SEM = pl.BlockSpec(memory_space=pltpu.SEMAPHORE)
EFFECT = pltpu.SideEffectType.DATAFLOW_SIDE_EFFECTING


def _pair_copies(s_ref, land_ref, send_sems, recv_sems):
    x, y, c = _position()
    return [pltpu.make_async_remote_copy(
        src_ref=s_ref.at[2 * k + (1 - c)], dst_ref=land_ref.at[k], send_sem=send_sems.at[k],
        recv_sem=recv_sems.at[k], device_id=(x, y, 1 - c), device_id_type=MESH) for k in range(4)]


def _pair_start(name, stack):
    land_shape = (4,) + stack.shape[1:]

    def body(s_ref, land_ref, send_sems, recv_sems, s_thru, land_thru, token):
        for cp in _pair_copies(s_ref, land_ref, send_sems, recv_sems):
            cp.start()
        token[...] = jnp.zeros_like(token)

    return pl.pallas_call(
        body, name=name,
        out_shape=(pltpu.SemaphoreType.DMA((4,)), pltpu.SemaphoreType.DMA((4,)),
                   pltpu.HBM(stack.shape, stack.dtype), pltpu.HBM(land_shape, stack.dtype),
                   jax.ShapeDtypeStruct((8, LANES), F32)),
        in_specs=(HBM, HBM), out_specs=(SEM, SEM, HBM, HBM, pl.BlockSpec(memory_space=pltpu.VMEM)),
        input_output_aliases={0: 2, 1: 3}, compiler_params=pltpu.CompilerParams(has_side_effects=EFFECT),
    )(pltpu.with_memory_space_constraint(stack, pltpu.HBM),
      pltpu.with_memory_space_constraint(lax.empty(land_shape, stack.dtype), pltpu.HBM))


def _pair_wait(name, started, after):
    send_sems, recv_sems, s_thru, land_thru, _ = started

    def body(s_ref, land_ref, send_sems, recv_sems, after_ref, s_out, land_out):
        for cp in _pair_copies(s_ref, land_ref, send_sems, recv_sems):
            cp.wait_send()
            cp.wait_recv()

    return pl.pallas_call(
        body, name=name,
        out_shape=(pltpu.HBM(s_thru.shape, s_thru.dtype), pltpu.HBM(land_thru.shape, land_thru.dtype)),
        in_specs=(HBM, HBM, SEM, SEM, ANY), out_specs=(HBM, HBM), input_output_aliases={0: 0, 1: 1},
        compiler_params=pltpu.CompilerParams(has_side_effects=EFFECT),
    )(s_thru, land_thru, send_sems, recv_sems, after)


def _pair_sum(name, stack, other, core):
    _, r, c = stack.shape
    tr = _row_tile(r, 1024)

    def body(core_ref, a_ref, b_ref, o_ref):
        o_ref[...] = (a_ref[...].astype(F32) + b_ref[...].astype(F32)).astype(o_ref.dtype)

    grid_spec = pltpu.PrefetchScalarGridSpec(
        num_scalar_prefetch=1, grid=(4, r // tr),
        in_specs=[pl.BlockSpec((None, tr, c), lambda k, i, core_ref: (2 * k + core_ref[0], i, 0)),
                  pl.BlockSpec((None, tr, c), lambda k, i, core_ref: (k, i, 0))],
        out_specs=pl.BlockSpec((None, tr, c), lambda k, i, core_ref: (k, i, 0)))
    return pl.pallas_call(
        body, name=name, grid_spec=grid_spec, out_shape=jax.ShapeDtypeStruct((4, r, c), stack.dtype),
        compiler_params=_cparams("parallel", "parallel"),
    )(core, stack, other)


def _chip_exchange(name, sums, collective_id):
    src = jax.new_ref(sums, memory_space=pltpu.MemorySpace.HBM)
    out = jax.empty_ref(jax.ShapeDtypeStruct((3,) + sums.shape[1:], sums.dtype),
                        memory_space=pltpu.MemorySpace.HBM)

    @_sequencer(name, collective_id, (pltpu.SemaphoreType.DMA((3,)), pltpu.SemaphoreType.DMA((3,))))
    def launch(send_sems, recv_sems):
        x, y, c = _position()
        chips = [(1 - x, y), (x, 1 - y), (1 - x, 1 - y)]
        _handshake([(*chip, c) for chip in chips])
        copies = [pltpu.make_async_remote_copy(
            src_ref=src.at[2 * px + py], dst_ref=out.at[j], send_sem=send_sems.at[j],
            recv_sem=recv_sems.at[j], device_id=(px, py, c), device_id_type=MESH)
            for j, (px, py) in enumerate(chips)]
        for cp in copies:
            cp.start()
        for cp in copies:
            cp.wait()

    launch()
    return out[...]


def _small_rows(shapes):
    first, row = [], 0
    for r, c in shapes:
        first.append(row)
        row += r * (c // LANES) if c % LANES == 0 else r
        row = -(-row // 8) * 8
    return first, row


def _small_move(packed, row, ref, to_packed):
    r, c = ref.shape
    if c % LANES:
        if to_packed:
            packed[row:row + r, 0:c] = ref[...]
        else:
            ref[...] = packed[row:row + r, 0:c]
        return
    per = c // LANES
    for i in range(r):
        for j in range(per):
            at = row + i * per + j
            if to_packed:
                packed[at:at + 1, :] = ref[i:i + 1, j * LANES:(j + 1) * LANES]
            else:
                ref[i:i + 1, j * LANES:(j + 1) * LANES] = packed[at:at + 1, :]


def _small_step(name, parts, ws, ms, vs, deps=()):
    n_par = len(ws)
    first, rows = _small_rows([p.shape for p in parts])
    vm = pl.BlockSpec(memory_space=pltpu.VMEM)
    buf = pltpu.VMEM((rows, LANES), F32)

    def reduce_body(*refs):
        part_refs, (sum_out, mine, gather, send_sems, recv_sems) = refs[:n_par + 1], refs[-5:]
        x, y, c = _position()
        me = 4 * x + 2 * y + c
        mine[...] = jnp.zeros_like(mine)
        for k, ref in enumerate(part_refs):
            _small_move(mine, first[k], ref, True)
        gather[me] = mine[...]
        copies = []
        for k in range(1, N_DEV):
            peer = (x ^ (k >> 2), y ^ ((k >> 1) & 1), c ^ (k & 1))
            copies.append(pltpu.make_async_remote_copy(
                src_ref=mine, dst_ref=gather.at[me], send_sem=send_sems.at[k - 1],
                recv_sem=recv_sems.at[k - 1], device_id=peer, device_id_type=MESH))
        for cp in copies:
            cp.start()
        for k in range(1, N_DEV):
            peer_slot = 4 * (x ^ (k >> 2)) + 2 * (y ^ ((k >> 1) & 1)) + (c ^ (k & 1))
            pltpu.make_async_remote_copy(
                src_ref=mine, dst_ref=gather.at[peer_slot], send_sem=send_sems.at[k - 1],
                recv_sem=recv_sems.at[k - 1], device_id=(x, y, c), device_id_type=MESH).wait()
        acc = gather[0]
        for j in range(1, N_DEV):
            acc = acc + gather[j]
        sum_out[...] = acc

    summed = pl.pallas_call(
        reduce_body, name=name + "_reduce", in_specs=[vm] * (n_par + 1) + [ANY] * len(deps), out_specs=vm,
        out_shape=jax.ShapeDtypeStruct((rows, LANES), F32),
        scratch_shapes=[buf, pltpu.VMEM((N_DEV, rows, LANES), F32), pltpu.SemaphoreType.DMA((7,)),
                        pltpu.SemaphoreType.DMA((7,))],
    )(*parts, *deps)

    def adam_body(*refs):
        sum_ref, refs = refs[0], refs[1:]
        w_refs, m_refs, v_refs, refs = refs[:n_par], refs[n_par:2 * n_par], refs[2 * n_par:3 * n_par], refs[3 * n_par:]
        g_out, d_out, m_out, v_out = (refs[i * n_par:(i + 1) * n_par] for i in range(4))
        loss_out = refs[4 * n_par]
        w_p, m_p, v_p, d_p = refs[4 * n_par + 1:]
        for packed in (w_p, m_p, v_p):
            packed[...] = jnp.zeros_like(packed)
        for k in range(n_par):
            for packed, src in ((w_p, w_refs[k]), (m_p, m_refs[k]), (v_p, v_refs[k])):
                _small_move(packed, first[k], src, True)
        delta, m_new, v_new = _adam_math(w_p[...], sum_ref[...], m_p[...], v_p[...])
        d_p[...] = delta
        m_p[...] = m_new
        v_p[...] = v_new
        for k in range(n_par):
            for packed, dst in ((sum_ref, g_out[k]), (d_p, d_out[k]), (m_p, m_out[k]), (v_p, v_out[k])):
                _small_move(packed, first[k], dst, False)
        _small_move(sum_ref, first[n_par], loss_out, False)

    like = [jax.ShapeDtypeStruct(w.shape, F32) for w in ws]
    outs = pl.pallas_call(
        adam_body, name=name + "_adam", in_specs=[vm] * (3 * n_par + 1), out_specs=[vm] * (4 * n_par + 1),
        out_shape=like * 4 + [jax.ShapeDtypeStruct((1, LANES), F32)], scratch_shapes=[buf, buf, buf, buf],
    )(summed, *ws, *ms, *vs)
    return (outs[:n_par], outs[n_par:2 * n_par], outs[2 * n_par:3 * n_par], outs[3 * n_par:4 * n_par],
            outs[4 * n_par])


def _adam_math(w, g, m, v):
    m = ADAM_B1 * m + (1.0 - ADAM_B1) * g
    v = ADAM_B2 * v + (1.0 - ADAM_B2) * jnp.square(g)
    m_hat = m / (1.0 - ADAM_B1 ** ADAM_STEP)
    v_hat = v / (1.0 - ADAM_B2 ** ADAM_STEP)
    delta = -ADAM_LR * (m_hat / (jnp.sqrt(v_hat) + ADAM_EPS) + ADAM_WD * w)
    return delta, m, v


def _adam_shard(name, w, m, v, sums, recv, chip, deps=(), first_row=0, earlier=()):
    r, c = w.shape
    rows = sums.shape[1]
    tr = _row_tile(rows, 256)
    assert first_row % tr == 0
    skip = first_row // tr

    def body(chip_ref, w_ref, m_ref, v_ref, own_ref, r0_ref, r1_ref, r2_ref, *rest):
        g_out, d_out, m_out, v_out = rest[-4:]
        g = ((own_ref[...].astype(F32) + r0_ref[...].astype(F32)) + r1_ref[...].astype(F32)) + r2_ref[...].astype(F32)
        delta, m_new, v_new = _adam_math(w_ref[...], g, m_ref[...], v_ref[...])
        g_out[...] = g
        d_out[...] = delta
        m_out[...] = m_new
        v_out[...] = v_new

    plain = pl.BlockSpec((tr, c), lambda i, chip_ref: (skip + i, 0))
    piece = lambda j: pl.BlockSpec((None, tr, c), lambda i, chip_ref: (j, i, 0))
    grid_spec = pltpu.PrefetchScalarGridSpec(
        num_scalar_prefetch=1, grid=(rows // tr,),
        in_specs=[plain, plain, plain,
                  pl.BlockSpec((None, tr, c), lambda i, chip_ref: (chip_ref[0], i, 0)),
                  piece(0), piece(1), piece(2)] + [ANY] * (len(earlier) + len(deps)),
        out_specs=[plain] * 4)
    shape = jax.ShapeDtypeStruct((r, c), F32)
    return pl.pallas_call(
        body, name=name, grid_spec=grid_spec, out_shape=[shape] * 4, compiler_params=_cparams("parallel"),
        input_output_aliases={8 + k: k for k in range(len(earlier))},
    )(chip, w, m, v, sums, recv, recv, recv, *earlier, *deps)


def _reduce_scatter(tag, started, after, core, collective_id):
    grad_stack, other = _pair_wait("rs_pair_wait_" + tag, started, after)
    sums = _pair_sum("rs_sum_" + tag, grad_stack, other, core)
    return sums, _chip_exchange("rs_chip_" + tag, sums, collective_id)


SMALL = ("pre_norm_ffn1", "post_norm_ffn1", "pre_norm_mix", "post_norm_mix", "hgrn_lower_bounds_fwd",
         "hgrn_lower_bounds_bwd", "hgrn_out_norm", "attn_sink", "pre_norm_ffn2", "post_norm_ffn2", "rel_bias_table")
BIG = ("w_ffn1_gate_up", "w_ffn1_down", "w_mix_in", "w_mix_out", "w_ffn2_gate_up", "w_ffn2_down")
AG_ID = {n: 1 + i for i, n in enumerate(BIG)}
RS_ID = {n: 7 + i for i, n in enumerate(BIG)}
RS_ID.update(w_ffn1_gate_up_a=RS_ID["w_ffn1_gate_up"], w_ffn1_gate_up_b=13)
ORDER = ("pre_norm_ffn1", "post_norm_ffn1", "w_ffn1_gate_up", "w_ffn1_down", "pre_norm_mix", "post_norm_mix",
         "w_mix_in", "hgrn_lower_bounds_fwd", "hgrn_lower_bounds_bwd", "hgrn_out_norm", "attn_sink", "w_mix_out",
         "pre_norm_ffn2", "post_norm_ffn2", "w_ffn2_gate_up", "w_ffn2_down", "rel_bias_table")


def kernel(x, pre_norm_ffn1, post_norm_ffn1, w_ffn1_gate_up, w_ffn1_down, pre_norm_mix, post_norm_mix, w_mix_in, hgrn_lower_bounds_fwd, hgrn_lower_bounds_bwd, hgrn_out_norm, attn_sink, w_mix_out, pre_norm_ffn2, post_norm_ffn2, w_ffn2_gate_up, w_ffn2_down, rel_bias_table, loss_target, m_pre_norm_ffn1, m_post_norm_ffn1, m_w_ffn1_gate_up, m_w_ffn1_down, m_pre_norm_mix, m_post_norm_mix, m_w_mix_in, m_hgrn_lower_bounds_fwd, m_hgrn_lower_bounds_bwd, m_hgrn_out_norm, m_attn_sink, m_w_mix_out, m_pre_norm_ffn2, m_post_norm_ffn2, m_w_ffn2_gate_up, m_w_ffn2_down, m_rel_bias_table, v_pre_norm_ffn1, v_post_norm_ffn1, v_w_ffn1_gate_up, v_w_ffn1_down, v_pre_norm_mix, v_post_norm_mix, v_w_mix_in, v_hgrn_lower_bounds_fwd, v_hgrn_lower_bounds_bwd, v_hgrn_out_norm, v_attn_sink, v_w_mix_out, v_pre_norm_ffn2, v_post_norm_ffn2, v_w_ffn2_gate_up, v_w_ffn2_down, v_rel_bias_table):
    args = dict(locals())
    wts = {n: args[n] for n in ORDER}
    mom = {n: args["m_" + n] for n in ORDER}
    var = {n: args["v_" + n] for n in ORDER}

    x0 = x[0]
    target = loss_target[0]
    t, d = x0.shape
    n_hgrn = d // 2 // HEAD
    n_attn = (d - d // 2) // HEAD
    core = lax.axis_index("c").astype(jnp.int32).reshape(1)
    chip = (2 * lax.axis_index("x") + lax.axis_index("y")).astype(jnp.int32).reshape(1)

    def local(a, name):
        return jnp.transpose(a[0]) if name == "w_mix_in" else a[0]

    full = {n: _all_gather("ag_" + n, local(wts[n], n).astype(BF16), AG_ID[n]) for n in BIG}
    w_gu1, w_gu2 = full["w_ffn1_gate_up"], full["w_ffn2_gate_up"]
    w_d1 = full["w_ffn1_down"].reshape(-1, d)
    w_d2 = full["w_ffn2_down"].reshape(-1, d)
    w_out = full["w_mix_out"].reshape(-1, d)
    w_in_t = full["w_mix_in"].reshape(-1, d)

    g = {n: wts[n] for n in SMALL}
    lb_f = jax.nn.softmax(g["hgrn_lower_bounds_fwd"], axis=0)[0:1]
    lb_b = jax.nn.softmax(g["hgrn_lower_bounds_bwd"], axis=0)[0:1]
    bucket_idx = jnp.asarray(_t5_bucket_index())
    bias = _bias_build(g["rel_bias_table"], bucket_idx)

    n1 = _pre_norm("pre_norm1", x0, g["pre_norm_ffn1"])
    a1, gu1 = _ffn_up("ffn1_gate_up", n1, w_gu1)
    ff1 = _matmul("ffn1_down", a1, w_d1, mode="nn", out_dtype=F32)
    x1, h = _post_res_pre("res1", x0, ff1, g["post_norm_ffn1"], g["pre_norm_mix"], 0.5)
    p = _matmul("mix_in", h, w_in_t, mode="nt", out_dtype=F32, tm=2048, tn=512)
    y_h, o_raw = _hgrn_fwd(p, lb_f, lb_b, g["hgrn_out_norm"], n_hgrn)
    y_a = _attn_fwd(p, bias, g["attn_sink"], n_hgrn, n_attn)
    cat = jnp.concatenate([y_h, y_a], axis=1)
    mixed = _matmul("mix_out", cat, w_out, mode="nn", out_dtype=F32)
    x2, n2 = _post_res_pre("res2", x1, mixed, g["post_norm_mix"], g["pre_norm_ffn2"], 1.0)
    a2, gu2 = _ffn_up("ffn2_gate_up", n2, w_gu2)
    ff2 = _matmul("ffn2_down", a2, w_d2, mode="nn", out_dtype=F32)
    dy3, loss_part = _post_res_loss("res3_loss", x2, ff2, g["post_norm_ffn2"], target, 0.5)

    small_grad = {}
    scattered = {}

    pending = []

    def scatter(name, grad_stack):
        started = _pair_start("rs_pair_" + name, grad_stack)
        pending.append((name, started))
        return [started[4]]

    def settle(after, count=len(BIG)):
        deps = []
        while pending and count:
            name, started = pending.pop(0)
            scattered[name] = _reduce_scatter(name, started, after, core, RS_ID[name])
            deps.append(scattered[name][0])
            count -= 1
        return deps

    def ffn_bwd(tag, dy, ff, a, gu, n_in, x_in, w_gu, w_d, post_name, pre_name, gu_name, d_name, last):
        dff, small_grad[post_name] = _post_bwd("post_bwd" + tag, dy, ff, g[post_name], 0.5)
        dep = settle(dff)

        def dw_down(deps):
            return scatter(d_name, _matmul("dw_down" + tag, a, dff, mode="tn", out_dtype=BF16, tn=2048,
                                           deps=deps).reshape(N_DEV, -1, d))

        def dw_gate_up(part, rows, deps):
            return scatter(gu_name + part, _matmul("dw_gate_up" + tag + part, n_in[:, rows], dgu, mode="tn",
                                                   stack=True, halves=True, out_dtype=BF16, deps=deps))

        dep = dw_down(dep)
        dgu = _ffn_dact("d_act" + tag, dff, w_d, gu, deps=dep)
        dep = settle(dgu)
        if last:
            half = d // 2
            dep = dw_gate_up("_a", slice(0, half), dep)
            dep = dw_gate_up("_b", slice(half, d), dep)
            dep = settle(dep[0], count=1) + dep
        else:
            dep = dw_gate_up("", slice(0, d), dep)
        dn = _matmul("d_norm" + tag, dgu, w_gu, mode="nt", stack=True, halves=True, out_dtype=F32, deps=dep)
        dep = settle(dn)
        dx, small_grad[pre_name] = _pre_bwd("pre_bwd" + tag, dn, x_in, g[pre_name], dy, deps=dep)
        return dx

    dx2 = ffn_bwd("2", dy3, ff2, a2, gu2, n2, x2, w_gu2, w_d2, "post_norm_ffn2", "pre_norm_ffn2",
                  "w_ffn2_gate_up", "w_ffn2_down", last=False)

    dmixed, small_grad["post_norm_mix"] = _post_bwd("post_bwd_mix", dx2, mixed, g["post_norm_mix"], 1.0)
    dep = settle(dmixed)
    dcat = _matmul("d_cat", dmixed, w_out, mode="nt", out_dtype=F32, deps=dep)
    dep = scatter("w_mix_out", _matmul("dw_mix_out", cat, dmixed, mode="tn", out_dtype=BF16).reshape(N_DEV, -1, d))
    dq_a, dk_a, dv_a, dbias, dsink_rows = _attn_bwd(p, dcat, bias, g["attn_sink"], n_hgrn, n_attn, deps=dep)
    dq_h, di_h, dzf, dzb, dg_h, dlb_f, dlb_b, small_grad["hgrn_out_norm"] = _hgrn_bwd(
        p, o_raw, dcat, lb_f, lb_b, g["hgrn_out_norm"], n_hgrn)
    dp = jnp.concatenate([dq_h, di_h, dzf, dzb, dg_h, dq_a, dk_a, dv_a], axis=1)
    dep = settle(dp)
    dh = _matmul("d_h", dp, w_in_t, mode="nn", out_dtype=F32, tm=2048, deps=dep)
    dep = scatter("w_mix_in", _matmul("dw_mix_in", dp, h, mode="tn", out_dtype=BF16, tm=512,
                                         tn=2048).reshape(N_DEV, -1, d))
    dx1, small_grad["pre_norm_mix"] = _pre_bwd("pre_bwd_mix", dh, x1, g["pre_norm_mix"], dx2, deps=dep)

    dx0 = ffn_bwd("1", dx1, ff1, a1, gu1, n1, x0, w_gu1, w_d1, "post_norm_ffn1", "pre_norm_ffn1",
                  "w_ffn1_gate_up", "w_ffn1_down", last=True)

    def lb_grad(dlb, lb):
        da0 = dlb * lb * (1.0 - lb)
        return jnp.concatenate([da0, -da0], axis=0)

    small_grad["hgrn_lower_bounds_fwd"] = lb_grad(dlb_f, lb_f)
    small_grad["hgrn_lower_bounds_bwd"] = lb_grad(dlb_b, lb_b)
    small_grad["attn_sink"] = dsink_rows[:, :, 0].reshape(1, n_attn)
    small_grad["rel_bias_table"] = jnp.transpose(_bias_reduce(dbias, bucket_idx)[:, :, 0])

    def adam(n, tag, dep, **rows):
        sums, recv = scattered[n + tag]
        return _adam_shard("adam_" + n + tag, local(wts[n], n), local(mom[n], n), local(var[n], n), sums, recv,
                           chip, deps=dep, **rows)

    big_out = {}
    dep = []
    for n in ("w_ffn2_down", "w_ffn2_gate_up", "w_mix_out", "w_mix_in", "w_ffn1_down"):
        big_out[n] = adam(n, "", dep)
        dep = [big_out[n][0]]
    first_half = adam("w_ffn1_gate_up", "_a", dep)

    g_s, d_s, m_s, v_s, loss_row = _small_step(
        "small_step", [small_grad[n] for n in SMALL] + [loss_part], [wts[n] for n in SMALL],
        [mom[n] for n in SMALL], [var[n] for n in SMALL], deps=[first_half[0]])
    loss = loss_row[0, 0]
    grads, delta, new_m, new_v = (dict(zip(SMALL, vals)) for vals in (g_s, d_s, m_s, v_s))

    big_out["w_ffn1_gate_up"] = adam("w_ffn1_gate_up", "_b", [loss_row], first_row=d // 2, earlier=first_half)
    for n in BIG:
        grads[n], delta[n], new_m[n], new_v[n] = [local(o[None], n)[None] for o in big_out[n]]

    return (loss, dx0[None], *[grads[n] for n in ORDER], *[delta[n] for n in ORDER],
            *[new_m[n] for n in ORDER], *[new_v[n] for n in ORDER])
```

```python
import functools
import math

import numpy as np
import jax
import jax.numpy as jnp
from jax import lax
from jax.experimental import pallas as pl
from jax.experimental.pallas import tpu as pltpu
from jax.experimental.pallas import tpu_sc as plsc

F32 = jnp.float32
BF16 = jnp.bfloat16
MESH = pl.DeviceIdType.MESH

N_DEV = 8
EPS = 1e-6
NEG_INF = -1e30
HEAD = 128
CHUNK = 64
WINDOW = 128
KEY_SPAN = 3 * WINDOW
KV_HEADS = 2
REL_BUCKETS = 32
REL_MAX_DIST = 128
ADAM_LR, ADAM_B1, ADAM_B2, ADAM_EPS, ADAM_WD, ADAM_STEP = 0.001, 0.9, 0.999, 1e-08, 0.01, 10
LANES = 128
VMEM_LIMIT = 56 * 1024 * 1024
ANY = pl.BlockSpec(memory_space=pl.ANY)


def _cparams(*sem):
    return pltpu.CompilerParams(dimension_semantics=sem if sem else None, vmem_limit_bytes=VMEM_LIMIT)


def _dot(a, b):
    return jnp.dot(a, b, preferred_element_type=F32)


def _dot_nt(a, b):
    return lax.dot_general(a, b, (((1,), (1,)), ((), ())), preferred_element_type=F32)


def _dot_tn(a, b):
    return lax.dot_general(a, b, (((0,), (0,)), ((), ())), preferred_element_type=F32)


def _tile(dim, target):
    for c in (target, 1024, 512, 256, 128):
        if c <= target and dim % c == 0:
            return c
    return dim


def _row_tile(rows, target):
    fits = [c for c in range(16, min(rows, target) + 1, 16) if rows % c == 0]
    return max(fits) if fits else rows


K_WHOLE = 2048
K_STEP = 2816


def _k_tile(kd):
    if kd <= K_WHOLE:
        return kd
    return max(c for c in range(LANES, K_STEP + 1, LANES) if kd % c == 0)


def _matmul(name, a, b, *, mode, out_dtype, stack=False, halves=False, tm=1024, tn=1024, deps=()):
    assert not (stack and mode == "nn")
    grp = 1
    if mode == "nn":
        m, kd = a.shape
        n = b.shape[1]
    elif mode == "nt":
        m = a.shape[-2]
        n, kd = (b.shape[1], b.shape[0] * b.shape[2]) if stack else b.shape
    else:
        kd, m = a.shape
        n = b.shape[-1] * (2 if halves else 1)
    if stack:
        n1 = b.shape[2] if mode == "nt" else n // N_DEV
        assert n1 % LANES == 0
        if mode == "nt":
            grp = 2 if 2 * n1 <= K_STEP else 1
            tk = grp * n1
        else:
            tn = n1
    per_half = N_DEV // 2 // grp
    tm = _tile(m, tm)
    if not (stack and mode == "tn"):
        tn = _tile(n, tn)
    if not (stack and mode == "nt"):
        tk = _k_tile(kd)
    nk = kd // tk
    lead = None if grp == 1 else grp
    b_outer = nk == 1 and b.size > a.size
    grid = (n // tn, m // tm, nk) if b_outer else (m // tm, n // tn, nk)

    def spec(shape, index):
        return pl.BlockSpec(shape, (lambda g0, g1, k: index(g1, g0, k)) if b_outer else index)

    if mode == "nn":
        a_spec = spec((tm, tk), lambda i, j, k: (i, k))
        b_spec = spec((tk, tn), lambda i, j, k: (k, j))
        dot = _dot
    elif mode == "nt":
        if halves:
            a_spec = spec((None, tm, tk), lambda i, j, k: (k // per_half, i, k % per_half))
        else:
            a_spec = spec((tm, tk), lambda i, j, k: (i, k))
        if stack:
            b_spec = spec((lead, tn, n1), lambda i, j, k: (k, j, 0))
        else:
            b_spec = spec((tn, tk), lambda i, j, k: (j, k))
        dot = _dot_nt
    else:
        a_spec = spec((tk, tm), lambda i, j, k: (k, i))
        if halves:
            b_spec = spec((None, tk, tn), lambda i, j, k: (j // per_half, k, j % per_half))
        else:
            b_spec = spec((tk, tn), lambda i, j, k: (k, j))
        dot = _dot_tn
    if stack and mode == "tn":
        out_shape = jax.ShapeDtypeStruct((N_DEV, m, n1), out_dtype)
        o_spec = spec((None, tm, n1), lambda i, j, k: (j, i, 0))
    else:
        out_shape = jax.ShapeDtypeStruct((m, n), out_dtype)
        o_spec = spec((tm, tn), lambda i, j, k: (i, j))

    def product(a_ref, b_ref):
        bmat = jnp.concatenate([b_ref[s] for s in range(grp)], axis=1) if grp > 1 else b_ref[...]
        return dot(a_ref[...], bmat)

    def store(o_ref, val):
        o_ref[...] = val.astype(o_ref.dtype)

    def body_whole(a_ref, b_ref, *rest):
        store(rest[-1], product(a_ref, b_ref))

    def body_steps(a_ref, b_ref, *rest):
        o_ref, acc_ref = rest[-2:]
        k = pl.program_id(2)

        @pl.when(k == 0)
        def _():
            acc_ref[...] = product(a_ref, b_ref)

        @pl.when(k > 0)
        def _():
            acc_ref[...] += product(a_ref, b_ref)

        @pl.when(k == nk - 1)
        def _():
            store(o_ref, acc_ref[...])

    return pl.pallas_call(
        body_whole if nk == 1 else body_steps, name=name, grid=grid,
        in_specs=[a_spec, b_spec] + [ANY] * len(deps), out_specs=o_spec, out_shape=out_shape,
        scratch_shapes=[] if nk == 1 else [pltpu.VMEM((tm, tn), F32)],
        compiler_params=_cparams("parallel", "parallel", "arbitrary"),
    )(a, b, *deps)


def _col_parts(width, parts=2):
    groups = width // LANES
    parts = max(1, min(parts, groups // 2))
    bounds = [LANES * (groups * p // parts) for p in range(parts)] + [width]
    return [slice(bounds[p], bounds[p + 1]) for p in range(parts)]


def _ffn_up(name, n, w_stack):
    t, d = n.shape
    s, _, n1 = w_stack.shape
    half = s // 2
    tm = _tile(t, 512)

    def body(n_ref, wg_ref, wu_ref, act_ref, gu_ref):
        nv = n_ref[...]
        for cols in _col_parts(n1):
            gate = _dot(nv, wg_ref[:, cols])
            up = _dot(nv, wu_ref[:, cols])
            sg = jax.nn.sigmoid(gate)
            silu = gate * sg
            act_ref[:, cols] = (silu * up).astype(BF16)
            gu_ref[0, :, cols] = (up * (sg * (1.0 + gate * (1.0 - sg)))).astype(BF16)
            gu_ref[1, :, cols] = silu.astype(BF16)

    return pl.pallas_call(
        body, name=name, grid=(half, t // tm),
        in_specs=[pl.BlockSpec((tm, d), lambda j, i: (i, 0)),
                  pl.BlockSpec((None, d, n1), lambda j, i: (j, 0, 0)),
                  pl.BlockSpec((None, d, n1), lambda j, i: (half + j, 0, 0))],
        out_specs=[pl.BlockSpec((tm, n1), lambda j, i: (i, j)), pl.BlockSpec((2, tm, n1), lambda j, i: (0, i, j))],
        out_shape=[jax.ShapeDtypeStruct((t, half * n1), BF16), jax.ShapeDtypeStruct((2, t, half * n1), BF16)],
        compiler_params=_cparams("parallel", "parallel"),
    )(n, w_stack, w_stack)


def _ffn_dact(name, dff, w_d, gu, deps=()):
    t, d = dff.shape
    f = w_d.shape[0]
    tm = _tile(t, 512)
    tn = _tile(f, 1408)

    def body(dff_ref, w_ref, gu_ref, *rest):
        dgu_ref = rest[-1]
        da = _dot_nt(dff_ref[...], w_ref[...]).astype(BF16)
        dgu_ref[0] = da * gu_ref[0]
        dgu_ref[1] = da * gu_ref[1]

    pair = pl.BlockSpec((2, tm, tn), lambda j, i: (0, i, j))
    return pl.pallas_call(
        body, name=name, grid=(f // tn, t // tm),
        in_specs=[pl.BlockSpec((tm, d), lambda j, i: (i, 0)), pl.BlockSpec((tn, d), lambda j, i: (j, 0)), pair]
        + [ANY] * len(deps),
        out_specs=pair, out_shape=jax.ShapeDtypeStruct((2, t, f), BF16),
        compiler_params=_cparams("parallel", "parallel"),
    )(dff, w_d, gu, *deps)


ROWS = 256


def _rstd(xf):
    return lax.rsqrt(jnp.mean(xf * xf, axis=-1, keepdims=True) + EPS)


def _row_spec(t, d):
    return pl.BlockSpec((min(ROWS, t), d), lambda i: (i, 0))


def _vec_spec(d):
    return pl.BlockSpec((1, d), lambda i: (0, 0))


def _pre_norm(name, x, gain):
    t, d = x.shape

    def body(x_ref, g_ref, n_ref):
        xf = x_ref[...]
        n_ref[...] = (xf * _rstd(xf) * g_ref[...]).astype(BF16)

    return pl.pallas_call(
        body, name=name, grid=(t // min(ROWS, t),), in_specs=[_row_spec(t, d), _vec_spec(d)],
        out_specs=_row_spec(t, d), out_shape=jax.ShapeDtypeStruct((t, d), BF16),
        compiler_params=_cparams("parallel"),
    )(x, gain)


def _post_res_pre(name, x, ff, g_post, g_next, scale):
    t, d = x.shape

    def body(x_ref, ff_ref, gp_ref, gn_ref, xo_ref, n_ref):
        ff_ = ff_ref[...]
        xn = x_ref[...] + scale * (ff_ * _rstd(ff_) * gp_ref[...])
        xo_ref[...] = xn
        n_ref[...] = (xn * _rstd(xn) * gn_ref[...]).astype(BF16)

    return pl.pallas_call(
        body, name=name, grid=(t // min(ROWS, t),),
        in_specs=[_row_spec(t, d), _row_spec(t, d), _vec_spec(d), _vec_spec(d)],
        out_specs=[_row_spec(t, d), _row_spec(t, d)],
        out_shape=[jax.ShapeDtypeStruct((t, d), F32), jax.ShapeDtypeStruct((t, d), BF16)],
        compiler_params=_cparams("parallel"),
    )(x, ff, g_post, g_next)


def _post_res_loss(name, x, ff, g_post, target, scale):
    t, d = x.shape

    def body(x_ref, ff_ref, gp_ref, tg_ref, dy_ref, loss_ref):
        ff_ = ff_ref[...]
        err = x_ref[...] + scale * (ff_ * _rstd(ff_) * gp_ref[...]) - tg_ref[...]
        dy_ref[...] = err / d
        part = 0.5 * jnp.sum(jnp.mean(err * err, axis=-1, keepdims=True), axis=0, keepdims=True)

        @pl.when(pl.program_id(0) == 0)
        def _():
            loss_ref[...] = jnp.zeros_like(loss_ref)

        loss_ref[...] += jnp.broadcast_to(part, loss_ref.shape)

    return pl.pallas_call(
        body, name=name, grid=(t // min(ROWS, t),),
        in_specs=[_row_spec(t, d), _row_spec(t, d), _vec_spec(d), _row_spec(t, d)],
        out_specs=[_row_spec(t, d), _vec_spec(LANES)],
        out_shape=[jax.ShapeDtypeStruct((t, d), F32), jax.ShapeDtypeStruct((1, LANES), F32)],
        compiler_params=_cparams("arbitrary"),
    )(x, ff, g_post, target)


def _post_bwd(name, dy, ff, g_post, scale):
    t, d = dy.shape

    def body(dy_ref, ff_ref, gp_ref, dff_ref, dg_ref):
        ff_ = ff_ref[...]
        r = _rstd(ff_)
        xh = ff_ * r
        dyn = scale * dy_ref[...]
        dxh = dyn * gp_ref[...]
        dff_ref[...] = (r * (dxh - xh * jnp.mean(dxh * xh, axis=-1, keepdims=True))).astype(BF16)

        @pl.when(pl.program_id(0) == 0)
        def _():
            dg_ref[...] = jnp.zeros_like(dg_ref)

        dg_ref[...] += jnp.sum(dyn * xh, axis=0, keepdims=True)

    return pl.pallas_call(
        body, name=name, grid=(t // min(ROWS, t),),
        in_specs=[_row_spec(t, d), _row_spec(t, d), _vec_spec(d)],
        out_specs=[_row_spec(t, d), _vec_spec(d)],
        out_shape=[jax.ShapeDtypeStruct((t, d), BF16), jax.ShapeDtypeStruct((1, d), F32)],
        compiler_params=_cparams("arbitrary"),
    )(dy, ff, g_post)


def _pre_bwd(name, dn, x, g_pre, dy, deps=()):
    t, d = x.shape

    def body(dn_ref, x_ref, g_ref, dy_ref, *rest):
        dx_ref, dg_ref = rest[-2:]
        xf = x_ref[...]
        r = _rstd(xf)
        xh = xf * r
        dnf = dn_ref[...].astype(F32)
        dxh = dnf * g_ref[...]
        dx_ref[...] = dy_ref[...] + r * (dxh - xh * jnp.mean(dxh * xh, axis=-1, keepdims=True))

        @pl.when(pl.program_id(0) == 0)
        def _():
            dg_ref[...] = jnp.zeros_like(dg_ref)

        dg_ref[...] += jnp.sum(dnf * xh, axis=0, keepdims=True)

    return pl.pallas_call(
        body, name=name, grid=(t // min(ROWS, t),),
        in_specs=[_row_spec(t, d), _row_spec(t, d), _vec_spec(d), _row_spec(t, d)] + [ANY] * len(deps),
        out_specs=[_row_spec(t, d), _vec_spec(d)],
        out_shape=[jax.ShapeDtypeStruct((t, d), F32), jax.ShapeDtypeStruct((1, d), F32)],
        compiler_params=_cparams("arbitrary"),
    )(dn, x, g_pre, dy, *deps)


def _bdot(a, b, ca, cb, precision=None):
    return lax.dot_general(a, b, (((ca,), (cb,)), ((0,), (0,))), preferred_element_type=F32, precision=precision)


def _tri_masks(g):
    row = lax.broadcasted_iota(jnp.int32, (g, CHUNK, CHUNK), 1)
    col = lax.broadcasted_iota(jnp.int32, (g, CHUNK, CHUNK), 2)
    return col <= row, col >= row


def _ones_matmul(ones_mat, val):
    hi = val.astype(BF16)
    lo = (val - hi.astype(F32)).astype(BF16)
    return _bdot(ones_mat, hi, 2, 1) + _bdot(ones_mat, lo, 2, 1)


def _hgrn_block(z, lb, q, v, cum_mat):
    sg = jax.nn.sigmoid(z)
    f = lb + (1.0 - lb) * sg
    lf = jnp.log(f)
    k = 1.0 - f
    a = _ones_matmul(cum_mat, lf)
    last = jnp.sum(lf, axis=1, keepdims=True)
    e_a = jnp.exp(a)
    e_na = jnp.exp(-a)
    e_t = jnp.exp(last - a)
    return dict(sg=sg, f=f, k=k, decay=jnp.exp(last), e_a=e_a, e_na=e_na, e_t=e_t,
                qd=q * e_a, kd=k * e_na, kt=k * e_t)


def _hgrn_states(state, kv, decay, order):
    entering = [None] * len(order)
    for g in order:
        entering[g] = state
        state = decay[g] * state + kv[g]
    return jnp.stack(entering, axis=0), state


def _hgrn_fwd(p, lb_f, lb_b, gain, n_heads, width):
    t = p.shape[0]
    w = n_heads * HEAD
    blk = min(16, t // CHUNK)
    rows_blk = blk * CHUNK
    n_blocks = t // rows_blk
    fin_rows = min(256, t)

    def body(q_ref, i_ref, zf_ref, zb_ref, g_ref, lbf_ref, lbb_ref, gain_ref, y_ref, o_ref, st_ref):
        low, up = _tri_masks(blk)
        m_low, m_up = low.astype(BF16), up.astype(BF16)
        o_ref[...] = jnp.zeros_like(o_ref)
        st_ref[...] = jnp.zeros_like(st_ref)

        def one(r0, z_ref, lb, slot, rev):
            rows = pl.ds(r0, rows_blk)
            split = lambda ref: ref[rows, :].reshape(blk, CHUNK, HEAD)
            q, v = split(q_ref), split(i_ref)
            c = _hgrn_block(split(z_ref), lb, q, v, m_up if rev else m_low)
            qd, kd, kt, vb = c["qd"].astype(BF16), c["kd"].astype(BF16), c["kt"].astype(BF16), v.astype(BF16)
            pm = jnp.where(up if rev else low, _bdot(qd, kd, 2, 2), 0.0).astype(BF16)
            kv = _bdot(vb, kt, 1, 1)
            order = range(blk - 1, -1, -1) if rev else range(blk)
            entering, st_ref[slot] = _hgrn_states(st_ref[slot], kv, c["decay"], order)
            o = _bdot(pm, vb, 2, 1) + _bdot(qd, entering.astype(BF16), 2, 2)
            o_ref[rows, :] += o.reshape(rows_blk, HEAD)

        def step(n, carry):
            one(pl.multiple_of(n * rows_blk, rows_blk), zf_ref, lbf_ref[...], 0, False)
            one(pl.multiple_of((n_blocks - 1 - n) * rows_blk, rows_blk), zb_ref, lbb_ref[...], 1, True)
            return carry

        lax.fori_loop(0, n_blocks, step, 0)

        def fin(n, carry):
            rows = pl.ds(pl.multiple_of(n * fin_rows, fin_rows), fin_rows)
            o = o_ref[rows, :]
            g = g_ref[rows, :]
            y_ref[rows, :] = (o * _rstd(o) * gain_ref[...] * (g * jax.nn.sigmoid(g))).astype(BF16)
            return carry

        lax.fori_loop(0, t // fin_rows, fin, 0)

    col = lambda grp: pl.BlockSpec((t, HEAD), lambda h: (0, grp * n_heads + h))
    vec = pl.BlockSpec((1, HEAD), lambda h: (0, h))
    out = pl.BlockSpec((t, HEAD), lambda h: (0, h))
    return pl.pallas_call(
        body, name="hgrn_fwd", grid=(n_heads,),
        in_specs=[col(0), col(1), col(2), col(3), col(4), vec, vec, vec],
        out_specs=[out, out],
        out_shape=[jax.ShapeDtypeStruct((t, width), BF16), jax.ShapeDtypeStruct((t, w), F32)],
        scratch_shapes=[pltpu.VMEM((2, HEAD, HEAD), F32)],
        compiler_params=_cparams("parallel"),
    )(p, p, p, p, p, lb_f, lb_b, gain)


def _hgrn_bwd(p, o_raw, dcat, lb_f, lb_b, gain, n_heads):
    t = p.shape[0]
    w = n_heads * HEAD
    n_chunks = t // CHUNK
    blk = min(16, n_chunks)
    rows_blk = blk * CHUNK
    n_blocks = t // rows_blk
    rb = min(256, t)

    def body(q_ref, i_ref, zf_ref, zb_ref, g_ref, o_ref, dy_ref, lbf_ref, lbb_ref, gain_ref,
             dq_ref, di_ref, dzf_ref, dzb_ref, dg_ref, dlbf_ref, dlbb_ref, dgain_ref,
             do_s, dq_s, dv_s, st_s, cur_s):
        low, up = _tri_masks(blk)
        m_low, m_up = low.astype(BF16), up.astype(BF16)
        rowid = lax.broadcasted_iota(jnp.int32, (blk, CHUNK, HEAD), 1)
        gain_v = gain_ref[...]

        def norm_bwd(n, dgain):
            rows = pl.ds(pl.multiple_of(n * rb, rb), rb)
            o = o_ref[rows, :]
            g = g_ref[rows, :]
            dy = dy_ref[rows, :].astype(F32)
            r = _rstd(o)
            oh = o * r
            sg = jax.nn.sigmoid(g)
            dg_ref[rows, :] = (dy * oh * gain_v * (sg * (1.0 + g * (1.0 - sg)))).astype(BF16)
            dno = dy * (g * sg)
            dxh = dno * gain_v
            do_s[rows, :] = r * (dxh - oh * jnp.mean(dxh * oh, axis=-1, keepdims=True))
            return dgain + jnp.sum(dno * oh, axis=0, keepdims=True)

        dgain_ref[...] = lax.fori_loop(0, t // rb, norm_bwd, jnp.zeros((1, HEAD), F32))
        def direction(z_ref, lb_ref, dz_ref, dlb_ref, rev):
            way = int(rev)
            lb = lb_ref[...]
            cum_mat = m_up if rev else m_low
            cum_mat_t = m_low if rev else m_up
            mask = up if rev else low
            last_row = 0 if rev else CHUNK - 1

            order = range(blk - 1, -1, -1) if rev else range(blk)

            def rows_of(j):
                bidx = (n_blocks - 1 - j) if rev else j
                return bidx, pl.ds(pl.multiple_of(bidx * rows_blk, rows_blk), rows_blk)

            def load(rows):
                split = lambda ref: ref[rows, :].reshape(blk, CHUNK, HEAD)
                q, v = split(q_ref), split(i_ref)
                return q, v, _hgrn_block(split(z_ref), lb, q, v, cum_mat)

            def sweep_fwd(j):
                bidx, rows = rows_of(j)
                _, v, c = load(rows)
                kv = _bdot(v.astype(BF16), c["kt"].astype(BF16), 1, 1)
                st_s[way, pl.ds(bidx * blk, blk)], cur_s[2 * way] = _hgrn_states(
                    cur_s[2 * way], kv, c["decay"], order)

            dlb_ref[...] = jnp.zeros_like(dlb_ref)

            def sweep_bwd(jj):
                bidx, rows = rows_of(n_blocks - 1 - jj)
                _, v, c = load(rows)
                st = st_s[way, pl.ds(bidx * blk, blk)]
                do = do_s[rows, :].reshape(blk, CHUNK, HEAD)
                qd, kd, kt, decay = c["qd"], c["kd"], c["kt"], c["decay"]
                qd_b, kd_b, kt_b = qd.astype(BF16), kd.astype(BF16), kt.astype(BF16)
                v_b, do_b, st_b = v.astype(BF16), do.astype(BF16), st.astype(BF16)
                pm = jnp.where(mask, _bdot(qd_b, kd_b, 2, 2), 0.0).astype(BF16)
                dpm = jnp.where(mask, _bdot(do_b, v_b, 2, 2), 0.0).astype(BF16)
                gq = _bdot(do_b, qd_b, 1, 1)
                dstate = cur_s[2 * way + 1]
                after = [None] * blk
                for g in reversed(order):
                    after[g] = dstate
                    dstate = gq[g] + decay[g] * dstate
                cur_s[2 * way + 1] = dstate
                dst = jnp.stack(after, axis=0)
                dst_b = dst.astype(BF16)
                dv = _bdot(pm, do_b, 1, 1) + _bdot(kt_b, dst_b, 2, 2)
                dqd = _bdot(dpm, kd_b, 2, 1) + _bdot(do_b, st_b, 2, 1)
                dkd = _bdot(dpm, qd_b, 1, 1)
                dkt = _bdot(v_b, dst_b, 2, 1)
                dlast = (jnp.sum(dkt * kt, axis=1, keepdims=True)
                         + decay * jnp.sum(dst * st, axis=1, keepdims=True))
                dq_s[way, rows, :] = (dqd * c["e_a"]).reshape(rows_blk, HEAD)
                dv_s[way, rows, :] = dv.reshape(rows_blk, HEAD)
                dk = dkd * c["e_na"] + dkt * c["e_t"]
                da = dqd * qd - dkd * kd - dkt * kt
                da = da + jnp.where(rowid == last_row, dlast, 0.0)
                dlf = _ones_matmul(cum_mat_t, da)
                df = dlf / c["f"] - dk
                sg = c["sg"]
                dz_ref[rows, :] = (df * (1.0 - lb) * (sg * (1.0 - sg))).reshape(rows_blk, HEAD).astype(BF16)
                dlb_ref[...] += jnp.sum((df * (1.0 - sg)).reshape(rows_blk, HEAD), axis=0, keepdims=True)

            return sweep_fwd, sweep_bwd

        ways = [direction(zf_ref, lbf_ref, dzf_ref, dlbf_ref, False),
                direction(zb_ref, lbb_ref, dzb_ref, dlbb_ref, True)]
        cur_s[...] = jnp.zeros_like(cur_s)
        for sweep in range(2):
            def both(j, carry):
                for way in ways:
                    way[sweep](j)
                return carry

            lax.fori_loop(0, n_blocks, both, 0)
        dq_ref[...] = (dq_s[0] + dq_s[1]).astype(BF16)
        di_ref[...] = (dv_s[0] + dv_s[1]).astype(BF16)

    col = lambda grp: pl.BlockSpec((t, HEAD), lambda h: (0, grp * n_heads + h))
    one = pl.BlockSpec((t, HEAD), lambda h: (0, h))
    vec = pl.BlockSpec((1, HEAD), lambda h: (0, h))
    big = jax.ShapeDtypeStruct((t, w), BF16)
    small = jax.ShapeDtypeStruct((1, w), F32)
    return pl.pallas_call(
        body, name="hgrn_bwd", grid=(n_heads,),
        in_specs=[col(0), col(1), col(2), col(3), col(4), one, one, vec, vec, vec],
        out_specs=[one] * 5 + [vec] * 3,
        out_shape=[big] * 5 + [small] * 3,
        scratch_shapes=[pltpu.VMEM((t, HEAD), F32), pltpu.VMEM((2, t, HEAD), F32), pltpu.VMEM((2, t, HEAD), F32),
                        pltpu.VMEM((2, n_chunks, HEAD, HEAD), F32), pltpu.VMEM((4, HEAD, HEAD), F32)],
        compiler_params=_cparams("parallel"),
    )(p, p, p, p, p, o_raw, dcat, lb_f, lb_b, gain)


def _t5_bucket_index():
    c = np.arange(WINDOW)[:, None]
    s = np.arange(KEY_SPAN)[None, :]
    rel = s - WINDOW - c
    nb = REL_BUCKETS // 2
    max_exact = nb // 2
    bucket = (rel > 0).astype(np.int32) * nb
    n = np.abs(rel)
    large = max_exact + (np.log(np.maximum(n, 1) / max_exact) / np.log(REL_MAX_DIST / max_exact)
                         * (nb - max_exact)).astype(np.int32)
    large = np.minimum(large, nb - 1)
    return bucket + np.where(n < max_exact, n, large).astype(np.int32)


def _bias_build(table, idx):
    n_attn = table.shape[1]

    def body(tab_ref, idx_ref, o_ref):
        h = pl.program_id(0)
        idx_v = idx_ref[...]
        acc = jnp.zeros((WINDOW, KEY_SPAN), F32)
        for b in range(REL_BUCKETS):
            acc = jnp.where(idx_v == b, tab_ref[b, h], acc)
        o_ref[...] = acc

    return pl.pallas_call(
        body, name="bias_build", grid=(n_attn,),
        in_specs=[pl.BlockSpec(memory_space=pltpu.SMEM), pl.BlockSpec((WINDOW, KEY_SPAN), lambda h: (0, 0))],
        out_specs=pl.BlockSpec((None, WINDOW, KEY_SPAN), lambda h: (h, 0, 0)),
        out_shape=jax.ShapeDtypeStruct((n_attn, WINDOW, KEY_SPAN), F32), compiler_params=_cparams("parallel"),
    )(table, idx)


def _bias_reduce(dbias, idx):
    n_attn = dbias.shape[0]

    def body(idx_ref, d_ref, o_ref):
        idx_v = idx_ref[...]
        dv = d_ref[...]
        rows = lax.broadcasted_iota(jnp.int32, (REL_BUCKETS, LANES), 0)
        acc = jnp.zeros((REL_BUCKETS, LANES), F32)
        for b in range(REL_BUCKETS):
            part = jnp.sum(jnp.where(idx_v == b, dv, 0.0), axis=1, keepdims=True)
            acc = jnp.where(rows == b, jnp.sum(part, axis=0, keepdims=True), acc)
        o_ref[...] = acc

    return pl.pallas_call(
        body, name="bias_reduce", grid=(n_attn,),
        in_specs=[pl.BlockSpec((WINDOW, KEY_SPAN), lambda h: (0, 0)),
                  pl.BlockSpec((None, WINDOW, KEY_SPAN), lambda h: (h, 0, 0))],
        out_specs=pl.BlockSpec((None, REL_BUCKETS, LANES), lambda h: (h, 0, 0)),
        out_shape=jax.ShapeDtypeStruct((n_attn, REL_BUCKETS, LANES), F32), compiler_params=_cparams("parallel"),
    )(idx, dbias)


def _attn_probs(q, kb, bias, sink, valid):
    s = _dot_nt(q, kb) / math.sqrt(HEAD) + bias
    s = jnp.where(valid, s, NEG_INF)
    m = jnp.maximum(jnp.max(s, axis=-1, keepdims=True), sink)
    e = jnp.exp(s - m)
    e_sink = jnp.exp(sink - m)
    den = jnp.sum(e, axis=-1, keepdims=True) + e_sink
    return e / den, e_sink / den


def _attn_valid(n, t, grp):
    c = lax.broadcasted_iota(jnp.int32, (grp * WINDOW, KEY_SPAN), 0) & (WINDOW - 1)
    s = lax.broadcasted_iota(jnp.int32, (grp * WINDOW, KEY_SPAN), 1)
    rel = s - WINDOW - c
    key_pos = n * WINDOW - WINDOW + s
    return (jnp.abs(rel) <= WINDOW) & (key_pos >= 0) & (key_pos < t)


def _stack_heads(ref, grp):
    return jnp.concatenate([ref[:, g * HEAD:(g + 1) * HEAD] for g in range(grp)], axis=0).astype(BF16)


def _sink_column(sink_ref, x, grp):
    return jnp.concatenate([jnp.full((WINDOW, 1), sink_ref[0, x * grp + g], F32) for g in range(grp)], axis=0)


def _attn_specs(t, n_hgrn, n_attn):
    grp = n_attn // KV_HEADS
    nb = t // WINDOW
    cq = 5 * n_hgrn
    ck = cq + n_attn
    cv = ck + KV_HEADS
    q_spec = pl.BlockSpec((WINDOW, grp * HEAD), lambda x, n: (n, cq // grp + x))
    kv = lambda base, off: pl.BlockSpec(
        (WINDOW, HEAD), lambda x, n: (jnp.clip(n + off, 0, nb - 1), base + x))
    band = [kv(ck, -1), kv(ck, 0), kv(ck, 1), kv(cv, -1), kv(cv, 0), kv(cv, 1)]
    bias_spec = pl.BlockSpec((grp, WINDOW, KEY_SPAN), lambda x, n: (x, 0, 0))
    sink_spec = pl.BlockSpec(memory_space=pltpu.SMEM)
    return grp, nb, q_spec, band, bias_spec, sink_spec


def _attn_fwd(p, bias, sink, cat, n_hgrn, n_attn):
    t = p.shape[0]
    grp, nb, q_spec, band, bias_spec, sink_spec = _attn_specs(t, n_hgrn, n_attn)

    def body(q_ref, kp, kc, kn, vp, vc, vn, bias_ref, sink_ref, cat_ref, y_ref):
        x, n = pl.program_id(0), pl.program_id(1)
        kb = jnp.concatenate([kp[...], kc[...], kn[...]], axis=0).astype(BF16)
        vb = jnp.concatenate([vp[...], vc[...], vn[...]], axis=0).astype(BF16)
        pr, _ = _attn_probs(_stack_heads(q_ref, grp), kb, bias_ref[...].reshape(grp * WINDOW, KEY_SPAN),
                            _sink_column(sink_ref, x, grp), _attn_valid(n, t, grp))
        y = _dot(pr.astype(BF16), vb).astype(BF16)
        for g in range(grp):
            y_ref[:, g * HEAD:(g + 1) * HEAD] = y[g * WINDOW:(g + 1) * WINDOW]

    return pl.pallas_call(
        body, name="attn_fwd", grid=(KV_HEADS, nb),
        in_specs=[q_spec] + band + [bias_spec, sink_spec, ANY],
        out_specs=pl.BlockSpec((WINDOW, grp * HEAD), lambda x, n: (n, n_hgrn // grp + x)),
        out_shape=jax.ShapeDtypeStruct(cat.shape, BF16), input_output_aliases={9: 0},
        compiler_params=_cparams("parallel", "parallel"),
    )(p, p, p, p, p, p, p, bias, sink, cat)


def _attn_bwd(p, dcat, bias, sink, n_hgrn, n_attn, deps=()):
    t = p.shape[0]
    grp, nb, q_spec, band, bias_spec, sink_spec = _attn_specs(t, n_hgrn, n_attn)
    inv = 1.0 / math.sqrt(HEAD)

    def body(q_ref, kp, kc, kn, vp, vc, vn, bias_ref, sink_ref, do_ref, *rest):
        dq_ref, dk_ref, dv_ref, dbias_ref, dsink_ref, dk_s, dv_s = rest[-7:]
        x, n = pl.program_id(0), pl.program_id(1)

        @pl.when(n == 0)
        def _():
            dk_s[...] = jnp.zeros_like(dk_s)
            dv_s[...] = jnp.zeros_like(dv_s)
            dbias_ref[...] = jnp.zeros_like(dbias_ref)
            dsink_ref[...] = jnp.zeros_like(dsink_ref)

        kb = jnp.concatenate([kp[...], kc[...], kn[...]], axis=0).astype(BF16)
        vb = jnp.concatenate([vp[...], vc[...], vn[...]], axis=0).astype(BF16)
        q = _stack_heads(q_ref, grp)
        do = _stack_heads(do_ref, grp)
        pr, p_sink = _attn_probs(q, kb, bias_ref[...].reshape(grp * WINDOW, KEY_SPAN),
                                 _sink_column(sink_ref, x, grp), _attn_valid(n, t, grp))
        dpr = _dot_nt(do, vb)
        delta = jnp.sum(pr * dpr, axis=-1, keepdims=True)
        ds = pr * (dpr - delta)
        ds_b = ds.astype(BF16)
        dq = (_dot(ds_b, kb) * inv).astype(BF16)
        dsink = -p_sink * delta
        for g in range(grp):
            head = slice(g * WINDOW, (g + 1) * WINDOW)
            dq_ref[:, g * HEAD:(g + 1) * HEAD] = dq[head]
            dbias_ref[g] += ds[head]
            dsink_ref[g:g + 1, :] += jnp.broadcast_to(jnp.sum(dsink[head], axis=0, keepdims=True), (1, WINDOW))
        rows = pl.ds(pl.multiple_of(n * WINDOW, WINDOW), KEY_SPAN)
        dk_s[rows, :] += _dot_tn(ds_b, q) * inv
        dv_s[rows, :] += _dot_tn(pr.astype(BF16), do)

        @pl.when(n == nb - 1)
        def _():
            dk_ref[...] = dk_s[pl.ds(WINDOW, t), :].astype(BF16)
            dv_ref[...] = dv_s[pl.ds(WINDOW, t), :].astype(BF16)

    do_spec = pl.BlockSpec((WINDOW, grp * HEAD), lambda x, n: (n, n_hgrn // grp + x))
    kv_out = pl.BlockSpec((t, HEAD), lambda x, n: (0, x))
    return pl.pallas_call(
        body, name="attn_bwd", grid=(KV_HEADS, nb),
        in_specs=[q_spec] + band + [bias_spec, sink_spec, do_spec] + [ANY] * len(deps),
        out_specs=[pl.BlockSpec((WINDOW, grp * HEAD), lambda x, n: (n, x)), kv_out, kv_out,
                   bias_spec, pl.BlockSpec((None, grp, WINDOW), lambda x, n: (x, 0, 0))],
        out_shape=[jax.ShapeDtypeStruct((t, n_attn * HEAD), BF16),
                   jax.ShapeDtypeStruct((t, KV_HEADS * HEAD), BF16),
                   jax.ShapeDtypeStruct((t, KV_HEADS * HEAD), BF16),
                   jax.ShapeDtypeStruct((n_attn, WINDOW, KEY_SPAN), F32),
                   jax.ShapeDtypeStruct((KV_HEADS, grp, WINDOW), F32)],
        scratch_shapes=[pltpu.VMEM((t + 2 * WINDOW, HEAD), F32), pltpu.VMEM((t + 2 * WINDOW, HEAD), F32)],
        compiler_params=_cparams("parallel", "arbitrary"),
    )(p, p, p, p, p, p, p, bias, sink, dcat, *deps)


def _position():
    return lax.axis_index("x"), lax.axis_index("y"), lax.axis_index("c")


def _handshake(peers):
    barrier = pltpu.get_barrier_semaphore()
    for peer in peers:
        pl.semaphore_signal(barrier, inc=1, device_id=peer, device_id_type=MESH)
    pl.semaphore_wait(barrier, len(peers))


def _sequencer(name, collective_id, scratch_types):
    return functools.partial(
        pl.kernel, mesh=plsc.ScalarSubcoreMesh(axis_name="sc", num_cores=1), name=name,
        scratch_types=scratch_types, compiler_params=pltpu.CompilerParams(collective_id=collective_id))


def _all_gather(name, shard, collective_id):
    rows = shard.shape[0]
    assert rows % 2 == 0
    rh = rows // 2
    src = jax.new_ref(shard, memory_space=pltpu.MemorySpace.HBM)
    out = jax.empty_ref(jax.ShapeDtypeStruct((N_DEV,) + shard.shape, shard.dtype),
                        memory_space=pltpu.MemorySpace.HBM)
    n_copies = 11

    @_sequencer(name, collective_id, (pltpu.SemaphoreType.DMA((n_copies,)), pltpu.SemaphoreType.DMA((n_copies,)),
                                      pltpu.SemaphoreType.DMA))
    def launch(send_sems, recv_sems, local_sem):
        x, y, c = _position()
        sibling = (x, y, 1 - c)
        xn, yn, dg = (1 - x, y), (x, 1 - y), (1 - x, 1 - y)
        _handshake([sibling, (*xn, c), (*yn, c)])

        def part(ref, half):
            return ref if half is None else ref.at[pl.ds(half * rh, rh)]

        def slot(chip, core, half=None):
            return part(out.at[4 * chip[0] + 2 * chip[1] + core], half)

        def copy(k, chip, core, half, to, own=False):
            return pltpu.make_async_remote_copy(
                src_ref=part(src, half) if own else slot(chip, core, half), dst_ref=slot(chip, core, half),
                send_sem=send_sems.at[k], recv_sem=recv_sems.at[k], device_id=to, device_id_type=MESH)

        def landed(k, chip, core, half):
            copy(k, chip, core, half, (x, y, c)).wait_recv()

        mine = pltpu.make_async_copy(src, slot((x, y), c), local_sem)
        mine.start()
        sent = [copy(0, (x, y), c, None, sibling, own=True),
                copy(1, (x, y), c, 0, (*xn, c), own=True), copy(3, (x, y), c, 1, (*yn, c), own=True),
                copy(2, (x, y), c, 1, (*xn, c), own=True), copy(4, (x, y), c, 0, (*yn, c), own=True)]
        for cp in sent:
            cp.start()

        def then(cp):
            cp.start()
            sent.append(cp)

        landed(1, xn, c, 0)
        then(copy(5, xn, c, 0, (*yn, c)))
        landed(3, yn, c, 1)
        then(copy(6, yn, c, 1, (*xn, c)))
        landed(2, xn, c, 1)
        then(copy(7, xn, c, None, sibling))
        landed(4, yn, c, 0)
        then(copy(8, yn, c, None, sibling))
        landed(5, dg, c, 0)
        then(copy(9, dg, c, 0, sibling))
        landed(6, dg, c, 1)
        then(copy(10, dg, c, 1, sibling))
        landed(0, (x, y), 1 - c, None)
        landed(7, xn, 1 - c, None)
        landed(8, yn, 1 - c, None)
        landed(9, dg, 1 - c, 0)
        landed(10, dg, 1 - c, 1)
        for cp in sent:
            cp.wait_send()
        mine.wait()

    launch()
    return out[...]


HBM = pl.BlockSpec(memory_space=pltpu.HBM)
SEM = pl.BlockSpec(memory_space=pltpu.SEMAPHORE)
EFFECT = pltpu.SideEffectType.DATAFLOW_SIDE_EFFECTING


def _pair_copies(s_ref, land_ref, send_sems, recv_sems):
    x, y, c = _position()
    return [pltpu.make_async_remote_copy(
        src_ref=s_ref.at[2 * k + (1 - c)], dst_ref=land_ref.at[k], send_sem=send_sems.at[k],
        recv_sem=recv_sems.at[k], device_id=(x, y, 1 - c), device_id_type=MESH) for k in range(4)]


def _pair_start(name, stack):
    land_shape = (4,) + stack.shape[1:]

    def body(s_ref, land_ref, send_sems, recv_sems, s_thru, land_thru, token):
        for cp in _pair_copies(s_ref, land_ref, send_sems, recv_sems):
            cp.start()
        token[...] = jnp.zeros_like(token)

    return pl.pallas_call(
        body, name=name,
        out_shape=(pltpu.SemaphoreType.DMA((4,)), pltpu.SemaphoreType.DMA((4,)),
                   pltpu.HBM(stack.shape, stack.dtype), pltpu.HBM(land_shape, stack.dtype),
                   jax.ShapeDtypeStruct((8, LANES), F32)),
        in_specs=(HBM, HBM), out_specs=(SEM, SEM, HBM, HBM, pl.BlockSpec(memory_space=pltpu.VMEM)),
        input_output_aliases={0: 2, 1: 3}, compiler_params=pltpu.CompilerParams(has_side_effects=EFFECT),
    )(pltpu.with_memory_space_constraint(stack, pltpu.HBM),
      pltpu.with_memory_space_constraint(lax.empty(land_shape, stack.dtype), pltpu.HBM))


def _pair_wait(name, started, after):
    send_sems, recv_sems, s_thru, land_thru, _ = started

    def body(s_ref, land_ref, send_sems, recv_sems, after_ref, s_out, land_out):
        for cp in _pair_copies(s_ref, land_ref, send_sems, recv_sems):
            cp.wait_send()
            cp.wait_recv()

    return pl.pallas_call(
        body, name=name,
        out_shape=(pltpu.HBM(s_thru.shape, s_thru.dtype), pltpu.HBM(land_thru.shape, land_thru.dtype)),
        in_specs=(HBM, HBM, SEM, SEM, ANY), out_specs=(HBM, HBM), input_output_aliases={0: 0, 1: 1},
        compiler_params=pltpu.CompilerParams(has_side_effects=EFFECT),
    )(s_thru, land_thru, send_sems, recv_sems, after)


def _pair_sum(name, stack, other, core):
    _, r, c = stack.shape
    tr = _row_tile(r, 1024)

    def body(core_ref, a_ref, b_ref, o_ref):
        o_ref[...] = (a_ref[...].astype(F32) + b_ref[...].astype(F32)).astype(o_ref.dtype)

    grid_spec = pltpu.PrefetchScalarGridSpec(
        num_scalar_prefetch=1, grid=(4, r // tr),
        in_specs=[pl.BlockSpec((None, tr, c), lambda k, i, core_ref: (2 * k + core_ref[0], i, 0)),
                  pl.BlockSpec((None, tr, c), lambda k, i, core_ref: (k, i, 0))],
        out_specs=pl.BlockSpec((None, tr, c), lambda k, i, core_ref: (k, i, 0)))
    return pl.pallas_call(
        body, name=name, grid_spec=grid_spec, out_shape=jax.ShapeDtypeStruct((4, r, c), stack.dtype),
        compiler_params=_cparams("parallel", "parallel"),
    )(core, stack, other)


def _chip_exchange(name, sums, collective_id):
    src = jax.new_ref(sums, memory_space=pltpu.MemorySpace.HBM)
    out = jax.empty_ref(jax.ShapeDtypeStruct((3,) + sums.shape[1:], sums.dtype),
                        memory_space=pltpu.MemorySpace.HBM)

    @_sequencer(name, collective_id, (pltpu.SemaphoreType.DMA((3,)), pltpu.SemaphoreType.DMA((3,))))
    def launch(send_sems, recv_sems):
        x, y, c = _position()
        chips = [(1 - x, y), (x, 1 - y), (1 - x, 1 - y)]
        _handshake([(*chip, c) for chip in chips])
        copies = [pltpu.make_async_remote_copy(
            src_ref=src.at[2 * px + py], dst_ref=out.at[j], send_sem=send_sems.at[j],
            recv_sem=recv_sems.at[j], device_id=(px, py, c), device_id_type=MESH)
            for j, (px, py) in enumerate(chips)]
        for cp in copies:
            cp.start()
        for cp in copies:
            cp.wait()

    launch()
    return out[...]


def _small_rows(shapes):
    first, row = [], 0
    for r, c in shapes:
        first.append(row)
        row += r * (c // LANES) if c % LANES == 0 else r
        row = -(-row // 8) * 8
    return first, row


def _small_move(packed, row, ref, to_packed):
    r, c = ref.shape
    if c % LANES:
        if to_packed:
            packed[row:row + r, 0:c] = ref[...]
        else:
            ref[...] = packed[row:row + r, 0:c]
        return
    per = c // LANES
    for i in range(r):
        for j in range(per):
            at = row + i * per + j
            if to_packed:
                packed[at:at + 1, :] = ref[i:i + 1, j * LANES:(j + 1) * LANES]
            else:
                ref[i:i + 1, j * LANES:(j + 1) * LANES] = packed[at:at + 1, :]


def _small_step(name, parts, ws, ms, vs, deps=()):
    n_par = len(ws)
    first, rows = _small_rows([p.shape for p in parts])
    vm = pl.BlockSpec(memory_space=pltpu.VMEM)
    buf = pltpu.VMEM((rows, LANES), F32)

    def reduce_body(*refs):
        part_refs, (sum_out, mine, gather, send_sems, recv_sems) = refs[:n_par + 1], refs[-5:]
        x, y, c = _position()
        me = 4 * x + 2 * y + c
        mine[...] = jnp.zeros_like(mine)
        for k, ref in enumerate(part_refs):
            _small_move(mine, first[k], ref, True)
        gather[me] = mine[...]
        copies = []
        for k in range(1, N_DEV):
            peer = (x ^ (k >> 2), y ^ ((k >> 1) & 1), c ^ (k & 1))
            copies.append(pltpu.make_async_remote_copy(
                src_ref=mine, dst_ref=gather.at[me], send_sem=send_sems.at[k - 1],
                recv_sem=recv_sems.at[k - 1], device_id=peer, device_id_type=MESH))
        for cp in copies:
            cp.start()
        for k in range(1, N_DEV):
            peer_slot = 4 * (x ^ (k >> 2)) + 2 * (y ^ ((k >> 1) & 1)) + (c ^ (k & 1))
            pltpu.make_async_remote_copy(
                src_ref=mine, dst_ref=gather.at[peer_slot], send_sem=send_sems.at[k - 1],
                recv_sem=recv_sems.at[k - 1], device_id=(x, y, c), device_id_type=MESH).wait()
        acc = gather[0]
        for j in range(1, N_DEV):
            acc = acc + gather[j]
        sum_out[...] = acc

    summed = pl.pallas_call(
        reduce_body, name=name + "_reduce", in_specs=[vm] * (n_par + 1) + [ANY] * len(deps), out_specs=vm,
        out_shape=jax.ShapeDtypeStruct((rows, LANES), F32),
        scratch_shapes=[buf, pltpu.VMEM((N_DEV, rows, LANES), F32), pltpu.SemaphoreType.DMA((7,)),
                        pltpu.SemaphoreType.DMA((7,))],
    )(*parts, *deps)

    def adam_body(*refs):
        sum_ref, refs = refs[0], refs[1:]
        w_refs, m_refs, v_refs, refs = refs[:n_par], refs[n_par:2 * n_par], refs[2 * n_par:3 * n_par], refs[3 * n_par:]
        g_out, d_out, m_out, v_out = (refs[i * n_par:(i + 1) * n_par] for i in range(4))
        loss_out = refs[4 * n_par]
        w_p, m_p, v_p, d_p = refs[4 * n_par + 1:]
        for packed in (w_p, m_p, v_p):
            packed[...] = jnp.zeros_like(packed)
        for k in range(n_par):
            for packed, src in ((w_p, w_refs[k]), (m_p, m_refs[k]), (v_p, v_refs[k])):
                _small_move(packed, first[k], src, True)
        delta, m_new, v_new = _adam_math(w_p[...], sum_ref[...], m_p[...], v_p[...])
        d_p[...] = delta
        m_p[...] = m_new
        v_p[...] = v_new
        for k in range(n_par):
            for packed, dst in ((sum_ref, g_out[k]), (d_p, d_out[k]), (m_p, m_out[k]), (v_p, v_out[k])):
                _small_move(packed, first[k], dst, False)
        _small_move(sum_ref, first[n_par], loss_out, False)

    like = [jax.ShapeDtypeStruct(w.shape, F32) for w in ws]
    outs = pl.pallas_call(
        adam_body, name=name + "_adam", in_specs=[vm] * (3 * n_par + 1), out_specs=[vm] * (4 * n_par + 1),
        out_shape=like * 4 + [jax.ShapeDtypeStruct((1, LANES), F32)], scratch_shapes=[buf, buf, buf, buf],
    )(summed, *ws, *ms, *vs)
    return (outs[:n_par], outs[n_par:2 * n_par], outs[2 * n_par:3 * n_par], outs[3 * n_par:4 * n_par],
            outs[4 * n_par])


def _adam_math(w, g, m, v):
    m = ADAM_B1 * m + (1.0 - ADAM_B1) * g
    v = ADAM_B2 * v + (1.0 - ADAM_B2) * jnp.square(g)
    m_hat = m / (1.0 - ADAM_B1 ** ADAM_STEP)
    v_hat = v / (1.0 - ADAM_B2 ** ADAM_STEP)
    delta = -ADAM_LR * (m_hat / (jnp.sqrt(v_hat) + ADAM_EPS) + ADAM_WD * w)
    return delta, m, v


def _adam_shard(name, w, m, v, sums, recv, chip, deps=(), first_row=0, earlier=()):
    r, c = w.shape
    rows = sums.shape[1]
    tr = _row_tile(rows, 256)
    assert first_row % tr == 0
    skip = first_row // tr

    def body(chip_ref, w_ref, m_ref, v_ref, own_ref, r0_ref, r1_ref, r2_ref, *rest):
        g_out, d_out, m_out, v_out = rest[-4:]
        g = ((own_ref[...].astype(F32) + r0_ref[...].astype(F32)) + r1_ref[...].astype(F32)) + r2_ref[...].astype(F32)
        delta, m_new, v_new = _adam_math(w_ref[...], g, m_ref[...], v_ref[...])
        g_out[...] = g
        d_out[...] = delta
        m_out[...] = m_new
        v_out[...] = v_new

    plain = pl.BlockSpec((tr, c), lambda i, chip_ref: (skip + i, 0))
    piece = lambda j: pl.BlockSpec((None, tr, c), lambda i, chip_ref: (j, i, 0))
    grid_spec = pltpu.PrefetchScalarGridSpec(
        num_scalar_prefetch=1, grid=(rows // tr,),
        in_specs=[plain, plain, plain,
                  pl.BlockSpec((None, tr, c), lambda i, chip_ref: (chip_ref[0], i, 0)),
                  piece(0), piece(1), piece(2)] + [ANY] * (len(earlier) + len(deps)),
        out_specs=[plain] * 4)
    shape = jax.ShapeDtypeStruct((r, c), F32)
    return pl.pallas_call(
        body, name=name, grid_spec=grid_spec, out_shape=[shape] * 4, compiler_params=_cparams("parallel"),
        input_output_aliases={8 + k: k for k in range(len(earlier))},
    )(chip, w, m, v, sums, recv, recv, recv, *earlier, *deps)


def _reduce_scatter(tag, started, after, core, collective_id):
    grad_stack, other = _pair_wait("rs_pair_wait_" + tag, started, after)
    sums = _pair_sum("rs_sum_" + tag, grad_stack, other, core)
    return sums, _chip_exchange("rs_chip_" + tag, sums, collective_id)


SMALL = ("pre_norm_ffn1", "post_norm_ffn1", "pre_norm_mix", "post_norm_mix", "hgrn_lower_bounds_fwd",
         "hgrn_lower_bounds_bwd", "hgrn_out_norm", "attn_sink", "pre_norm_ffn2", "post_norm_ffn2", "rel_bias_table")
BIG = ("w_ffn1_gate_up", "w_ffn1_down", "w_mix_in", "w_mix_out", "w_ffn2_gate_up", "w_ffn2_down")
AG_ID = {n: 1 + i for i, n in enumerate(BIG)}
RS_ID = {n: 7 + i for i, n in enumerate(BIG)}
RS_ID.update(w_ffn1_gate_up_a=RS_ID["w_ffn1_gate_up"], w_ffn1_gate_up_b=13)
ORDER = ("pre_norm_ffn1", "post_norm_ffn1", "w_ffn1_gate_up", "w_ffn1_down", "pre_norm_mix", "post_norm_mix",
         "w_mix_in", "hgrn_lower_bounds_fwd", "hgrn_lower_bounds_bwd", "hgrn_out_norm", "attn_sink", "w_mix_out",
         "pre_norm_ffn2", "post_norm_ffn2", "w_ffn2_gate_up", "w_ffn2_down", "rel_bias_table")


def kernel(x, pre_norm_ffn1, post_norm_ffn1, w_ffn1_gate_up, w_ffn1_down, pre_norm_mix, post_norm_mix, w_mix_in, hgrn_lower_bounds_fwd, hgrn_lower_bounds_bwd, hgrn_out_norm, attn_sink, w_mix_out, pre_norm_ffn2, post_norm_ffn2, w_ffn2_gate_up, w_ffn2_down, rel_bias_table, loss_target, m_pre_norm_ffn1, m_post_norm_ffn1, m_w_ffn1_gate_up, m_w_ffn1_down, m_pre_norm_mix, m_post_norm_mix, m_w_mix_in, m_hgrn_lower_bounds_fwd, m_hgrn_lower_bounds_bwd, m_hgrn_out_norm, m_attn_sink, m_w_mix_out, m_pre_norm_ffn2, m_post_norm_ffn2, m_w_ffn2_gate_up, m_w_ffn2_down, m_rel_bias_table, v_pre_norm_ffn1, v_post_norm_ffn1, v_w_ffn1_gate_up, v_w_ffn1_down, v_pre_norm_mix, v_post_norm_mix, v_w_mix_in, v_hgrn_lower_bounds_fwd, v_hgrn_lower_bounds_bwd, v_hgrn_out_norm, v_attn_sink, v_w_mix_out, v_pre_norm_ffn2, v_post_norm_ffn2, v_w_ffn2_gate_up, v_w_ffn2_down, v_rel_bias_table):
    args = dict(locals())
    wts = {n: args[n] for n in ORDER}
    mom = {n: args["m_" + n] for n in ORDER}
    var = {n: args["v_" + n] for n in ORDER}

    x0 = x[0]
    target = loss_target[0]
    t, d = x0.shape
    n_hgrn = d // 2 // HEAD
    n_attn = (d - d // 2) // HEAD
    core = lax.axis_index("c").astype(jnp.int32).reshape(1)
    chip = (2 * lax.axis_index("x") + lax.axis_index("y")).astype(jnp.int32).reshape(1)

    def local(a, name):
        return jnp.transpose(a[0]) if name == "w_mix_in" else a[0]

    full = {n: _all_gather("ag_" + n, local(wts[n], n).astype(BF16), AG_ID[n]) for n in BIG}
    w_gu1, w_gu2 = full["w_ffn1_gate_up"], full["w_ffn2_gate_up"]
    w_d1 = full["w_ffn1_down"].reshape(-1, d)
    w_d2 = full["w_ffn2_down"].reshape(-1, d)
    w_out = full["w_mix_out"].reshape(-1, d)
    w_in_t = full["w_mix_in"].reshape(-1, d)

    g = {n: wts[n] for n in SMALL}
    lb_f = jax.nn.softmax(g["hgrn_lower_bounds_fwd"], axis=0)[0:1]
    lb_b = jax.nn.softmax(g["hgrn_lower_bounds_bwd"], axis=0)[0:1]
    bucket_idx = jnp.asarray(_t5_bucket_index())
    bias = _bias_build(g["rel_bias_table"], bucket_idx)

    n1 = _pre_norm("pre_norm1", x0, g["pre_norm_ffn1"])
    a1, gu1 = _ffn_up("ffn1_gate_up", n1, w_gu1)
    ff1 = _matmul("ffn1_down", a1, w_d1, mode="nn", out_dtype=F32)
    x1, h = _post_res_pre("res1", x0, ff1, g["post_norm_ffn1"], g["pre_norm_mix"], 0.5)
    p = _matmul("mix_in", h, w_in_t, mode="nt", out_dtype=F32, tm=2048, tn=512)
    y_h, o_raw = _hgrn_fwd(p, lb_f, lb_b, g["hgrn_out_norm"], n_hgrn, d)
    cat = _attn_fwd(p, bias, g["attn_sink"], y_h, n_hgrn, n_attn)
    mixed = _matmul("mix_out", cat, w_out, mode="nn", out_dtype=F32)
    x2, n2 = _post_res_pre("res2", x1, mixed, g["post_norm_mix"], g["pre_norm_ffn2"], 1.0)
    a2, gu2 = _ffn_up("ffn2_gate_up", n2, w_gu2)
    ff2 = _matmul("ffn2_down", a2, w_d2, mode="nn", out_dtype=F32)
    dy3, loss_part = _post_res_loss("res3_loss", x2, ff2, g["post_norm_ffn2"], target, 0.5)

    small_grad = {}
    scattered = {}

    pending = []

    def scatter(name, grad_stack):
        started = _pair_start("rs_pair_" + name, grad_stack)
        pending.append((name, started))
        return [started[4]]

    def settle(after, count=len(BIG)):
        deps = []
        while pending and count:
            name, started = pending.pop(0)
            scattered[name] = _reduce_scatter(name, started, after, core, RS_ID[name])
            deps.append(scattered[name][0])
            count -= 1
        return deps

    def ffn_bwd(tag, dy, ff, a, gu, n_in, x_in, w_gu, w_d, post_name, pre_name, gu_name, d_name, last):
        dff, small_grad[post_name] = _post_bwd("post_bwd" + tag, dy, ff, g[post_name], 0.5)
        dep = settle(dff)

        def dw_down(deps):
            return scatter(d_name, _matmul("dw_down" + tag, a, dff, mode="tn", out_dtype=BF16, tn=2048,
                                           deps=deps).reshape(N_DEV, -1, d))

        def dw_gate_up(part, rows, deps):
            return scatter(gu_name + part, _matmul("dw_gate_up" + tag + part, n_in[:, rows], dgu, mode="tn",
                                                   stack=True, halves=True, out_dtype=BF16, deps=deps))

        dep = dw_down(dep)
        dgu = _ffn_dact("d_act" + tag, dff, w_d, gu, deps=dep)
        dep = settle(dgu)
        if last:
            half = d // 2
            dep = dw_gate_up("_a", slice(0, half), dep)
            dep = dw_gate_up("_b", slice(half, d), dep)
            dep = settle(dep[0], count=1) + dep
        else:
            dep = dw_gate_up("", slice(0, d), dep)
        dn = _matmul("d_norm" + tag, dgu, w_gu, mode="nt", stack=True, halves=True, out_dtype=BF16, deps=dep)
        dep = settle(dn)
        dx, small_grad[pre_name] = _pre_bwd("pre_bwd" + tag, dn, x_in, g[pre_name], dy, deps=dep)
        return dx

    dx2 = ffn_bwd("2", dy3, ff2, a2, gu2, n2, x2, w_gu2, w_d2, "post_norm_ffn2", "pre_norm_ffn2",
                  "w_ffn2_gate_up", "w_ffn2_down", last=False)

    dmixed, small_grad["post_norm_mix"] = _post_bwd("post_bwd_mix", dx2, mixed, g["post_norm_mix"], 1.0)
    dep = settle(dmixed)
    dcat = _matmul("d_cat", dmixed, w_out, mode="nt", out_dtype=BF16, deps=dep)
    dep = scatter("w_mix_out", _matmul("dw_mix_out", cat, dmixed, mode="tn", out_dtype=BF16).reshape(N_DEV, -1, d))
    dq_a, dk_a, dv_a, dbias, dsink_rows = _attn_bwd(p, dcat, bias, g["attn_sink"], n_hgrn, n_attn, deps=dep)
    dq_h, di_h, dzf, dzb, dg_h, dlb_f, dlb_b, small_grad["hgrn_out_norm"] = _hgrn_bwd(
        p, o_raw, dcat, lb_f, lb_b, g["hgrn_out_norm"], n_hgrn)
    dp = jnp.concatenate([dq_h, di_h, dzf, dzb, dg_h, dq_a, dk_a, dv_a], axis=1)
    dep = settle(dp)
    dh = _matmul("d_h", dp, w_in_t, mode="nn", out_dtype=BF16, tm=2048, deps=dep)
    dep = scatter("w_mix_in", _matmul("dw_mix_in", dp, h, mode="tn", out_dtype=BF16, tm=512,
                                         tn=2048).reshape(N_DEV, -1, d))
    dx1, small_grad["pre_norm_mix"] = _pre_bwd("pre_bwd_mix", dh, x1, g["pre_norm_mix"], dx2, deps=dep)

    dx0 = ffn_bwd("1", dx1, ff1, a1, gu1, n1, x0, w_gu1, w_d1, "post_norm_ffn1", "pre_norm_ffn1",
                  "w_ffn1_gate_up", "w_ffn1_down", last=True)

    def lb_grad(dlb, lb):
        da0 = dlb * lb * (1.0 - lb)
        return jnp.concatenate([da0, -da0], axis=0)

    small_grad["hgrn_lower_bounds_fwd"] = lb_grad(dlb_f, lb_f)
    small_grad["hgrn_lower_bounds_bwd"] = lb_grad(dlb_b, lb_b)
    small_grad["attn_sink"] = dsink_rows[:, :, 0].reshape(1, n_attn)
    small_grad["rel_bias_table"] = jnp.transpose(_bias_reduce(dbias, bucket_idx)[:, :, 0])

    def adam(n, tag, dep, **rows):
        sums, recv = scattered[n + tag]
        return _adam_shard("adam_" + n + tag, local(wts[n], n), local(mom[n], n), local(var[n], n), sums, recv,
                           chip, deps=dep, **rows)

    big_out = {}
    dep = []
    for n in ("w_ffn2_down", "w_ffn2_gate_up", "w_mix_out", "w_mix_in", "w_ffn1_down"):
        big_out[n] = adam(n, "", dep)
        dep = [big_out[n][0]]
    first_half = adam("w_ffn1_gate_up", "_a", dep)

    g_s, d_s, m_s, v_s, loss_row = _small_step(
        "small_step", [small_grad[n] for n in SMALL] + [loss_part], [wts[n] for n in SMALL],
        [mom[n] for n in SMALL], [var[n] for n in SMALL], deps=[first_half[0]])
    loss = loss_row[0, 0]
    grads, delta, new_m, new_v = (dict(zip(SMALL, vals)) for vals in (g_s, d_s, m_s, v_s))

    big_out["w_ffn1_gate_up"] = adam("w_ffn1_gate_up", "_b", [loss_row], first_row=d // 2, earlier=first_half)
    for n in BIG:
        grads[n], delta[n], new_m[n], new_v[n] = [local(o[None], n)[None] for o in big_out[n]]

    return (loss, dx0[None], *[grads[n] for n in ORDER], *[delta[n] for n in ORDER],
            *[new_m[n] for n in ORDER], *[new_v[n] for n in ORDER])
```

```python
import functools
import math

import numpy as np
import jax
import jax.numpy as jnp
from jax import lax
from jax.experimental import pallas as pl
from jax.experimental.pallas import tpu as pltpu
from jax.experimental.pallas import tpu_sc as plsc

F32 = jnp.float32
BF16 = jnp.bfloat16
MESH = pl.DeviceIdType.MESH

N_DEV = 8
EPS = 1e-6
NEG_INF = -1e30
HEAD = 128
CHUNK = 64
WINDOW = 128
KEY_SPAN = 3 * WINDOW
KV_HEADS = 2
REL_BUCKETS = 32
REL_MAX_DIST = 128
ADAM_LR, ADAM_B1, ADAM_B2, ADAM_EPS, ADAM_WD, ADAM_STEP = 0.001, 0.9, 0.999, 1e-08, 0.01, 10
LANES = 128
VMEM_LIMIT = 56 * 1024 * 1024
ANY = pl.BlockSpec(memory_space=pl.ANY)


def _cparams(*sem):
    return pltpu.CompilerParams(dimension_semantics=sem if sem else None, vmem_limit_bytes=VMEM_LIMIT)


def _dot(a, b):
    return jnp.dot(a, b, preferred_element_type=F32)


def _dot_nt(a, b):
    return lax.dot_general(a, b, (((1,), (1,)), ((), ())), preferred_element_type=F32)


def _dot_tn(a, b):
    return lax.dot_general(a, b, (((0,), (0,)), ((), ())), preferred_element_type=F32)


def _tile(dim, target):
    for c in (target, 1024, 512, 256, 128):
        if c <= target and dim % c == 0:
            return c
    return dim


def _row_tile(rows, target):
    fits = [c for c in range(16, min(rows, target) + 1, 16) if rows % c == 0]
    return max(fits) if fits else rows


K_WHOLE = 2048
K_STEP = 2816


def _k_tile(kd):
    if kd <= K_WHOLE:
        return kd
    return max(c for c in range(LANES, K_STEP + 1, LANES) if kd % c == 0)


def _matmul(name, a, b, *, mode, out_dtype, stack=False, halves=False, tm=1024, tn=1024, deps=()):
    assert not (stack and mode == "nn")
    grp = 1
    if mode == "nn":
        m, kd = a.shape
        n = b.shape[1]
    elif mode == "nt":
        m = a.shape[-2]
        n, kd = (b.shape[1], b.shape[0] * b.shape[2]) if stack else b.shape
    else:
        kd, m = a.shape
        n = b.shape[-1] * (2 if halves else 1)
    if stack:
        n1 = b.shape[2] if mode == "nt" else n // N_DEV
        assert n1 % LANES == 0
        if mode == "nt":
            grp = 2 if 2 * n1 <= K_STEP else 1
            tk = grp * n1
        else:
            tn = n1
    per_half = N_DEV // 2 // grp
    tm = _tile(m, tm)
    if not (stack and mode == "tn"):
        tn = _tile(n, tn)
    if not (stack and mode == "nt"):
        tk = _k_tile(kd)
    nk = kd // tk
    lead = None if grp == 1 else grp
    b_outer = nk == 1 and b.size > a.size
    grid = (n // tn, m // tm, nk) if b_outer else (m // tm, n // tn, nk)

    def spec(shape, index):
        return pl.BlockSpec(shape, (lambda g0, g1, k: index(g1, g0, k)) if b_outer else index)

    if mode == "nn":
        a_spec = spec((tm, tk), lambda i, j, k: (i, k))
        b_spec = spec((tk, tn), lambda i, j, k: (k, j))
        dot = _dot
    elif mode == "nt":
        if halves:
            a_spec = spec((None, tm, tk), lambda i, j, k: (k // per_half, i, k % per_half))
        else:
            a_spec = spec((tm, tk), lambda i, j, k: (i, k))
        if stack:
            b_spec = spec((lead, tn, n1), lambda i, j, k: (k, j, 0))
        else:
            b_spec = spec((tn, tk), lambda i, j, k: (j, k))
        dot = _dot_nt
    else:
        a_spec = spec((tk, tm), lambda i, j, k: (k, i))
        if halves:
            b_spec = spec((None, tk, tn), lambda i, j, k: (j // per_half, k, j % per_half))
        else:
            b_spec = spec((tk, tn), lambda i, j, k: (k, j))
        dot = _dot_tn
    if stack and mode == "tn":
        out_shape = jax.ShapeDtypeStruct((N_DEV, m, n1), out_dtype)
        o_spec = spec((None, tm, n1), lambda i, j, k: (j, i, 0))
    else:
        out_shape = jax.ShapeDtypeStruct((m, n), out_dtype)
        o_spec = spec((tm, tn), lambda i, j, k: (i, j))

    def product(a_ref, b_ref):
        bmat = jnp.concatenate([b_ref[s] for s in range(grp)], axis=1) if grp > 1 else b_ref[...]
        return dot(a_ref[...], bmat)

    def store(o_ref, val):
        o_ref[...] = val.astype(o_ref.dtype)

    def body_whole(a_ref, b_ref, *rest):
        store(rest[-1], product(a_ref, b_ref))

    def body_steps(a_ref, b_ref, *rest):
        o_ref, acc_ref = rest[-2:]
        k = pl.program_id(2)

        @pl.when(k == 0)
        def _():
            acc_ref[...] = product(a_ref, b_ref)

        @pl.when(k > 0)
        def _():
            acc_ref[...] += product(a_ref, b_ref)

        @pl.when(k == nk - 1)
        def _():
            store(o_ref, acc_ref[...])

    return pl.pallas_call(
        body_whole if nk == 1 else body_steps, name=name, grid=grid,
        in_specs=[a_spec, b_spec] + [ANY] * len(deps), out_specs=o_spec, out_shape=out_shape,
        scratch_shapes=[] if nk == 1 else [pltpu.VMEM((tm, tn), F32)],
        compiler_params=_cparams("parallel", "parallel", "arbitrary"),
    )(a, b, *deps)


def _col_parts(width, parts=2):
    groups = width // LANES
    parts = max(1, min(parts, groups // 2))
    bounds = [LANES * (groups * p // parts) for p in range(parts)] + [width]
    return [slice(bounds[p], bounds[p + 1]) for p in range(parts)]


def _ffn_up(name, n, w_stack):
    t, d = n.shape
    s, _, n1 = w_stack.shape
    half = s // 2
    tm = _tile(t, 512)

    def body(n_ref, wg_ref, wu_ref, act_ref, gu_ref):
        nv = n_ref[...]
        for cols in _col_parts(n1):
            gate = _dot(nv, wg_ref[:, cols])
            up = _dot(nv, wu_ref[:, cols])
            sg = jax.nn.sigmoid(gate)
            silu = gate * sg
            act_ref[:, cols] = (silu * up).astype(BF16)
            gu_ref[0, :, cols] = (up * (sg * (1.0 + gate * (1.0 - sg)))).astype(BF16)
            gu_ref[1, :, cols] = silu.astype(BF16)

    return pl.pallas_call(
        body, name=name, grid=(half, t // tm),
        in_specs=[pl.BlockSpec((tm, d), lambda j, i: (i, 0)),
                  pl.BlockSpec((None, d, n1), lambda j, i: (j, 0, 0)),
                  pl.BlockSpec((None, d, n1), lambda j, i: (half + j, 0, 0))],
        out_specs=[pl.BlockSpec((tm, n1), lambda j, i: (i, j)), pl.BlockSpec((2, tm, n1), lambda j, i: (0, i, j))],
        out_shape=[jax.ShapeDtypeStruct((t, half * n1), BF16), jax.ShapeDtypeStruct((2, t, half * n1), BF16)],
        compiler_params=_cparams("parallel", "parallel"),
    )(n, w_stack, w_stack)


def _ffn_dact(name, dff, w_d, gu, deps=()):
    t, d = dff.shape
    f = w_d.shape[0]
    tm = _tile(t, 512)
    tn = _tile(f, 1408)

    def body(dff_ref, w_ref, gu_ref, *rest):
        dgu_ref = rest[-1]
        da = _dot_nt(dff_ref[...], w_ref[...]).astype(BF16)
        dgu_ref[0] = da * gu_ref[0]
        dgu_ref[1] = da * gu_ref[1]

    pair = pl.BlockSpec((2, tm, tn), lambda j, i: (0, i, j))
    return pl.pallas_call(
        body, name=name, grid=(f // tn, t // tm),
        in_specs=[pl.BlockSpec((tm, d), lambda j, i: (i, 0)), pl.BlockSpec((tn, d), lambda j, i: (j, 0)), pair]
        + [ANY] * len(deps),
        out_specs=pair, out_shape=jax.ShapeDtypeStruct((2, t, f), BF16),
        compiler_params=_cparams("parallel", "parallel"),
    )(dff, w_d, gu, *deps)


ROWS = 256


def _rstd(xf):
    return lax.rsqrt(jnp.mean(xf * xf, axis=-1, keepdims=True) + EPS)


def _row_spec(t, d):
    return pl.BlockSpec((min(ROWS, t), d), lambda i: (i, 0))


def _vec_spec(d):
    return pl.BlockSpec((1, d), lambda i: (0, 0))


def _pre_norm(name, x, gain):
    t, d = x.shape

    def body(x_ref, g_ref, n_ref):
        xf = x_ref[...]
        n_ref[...] = (xf * _rstd(xf) * g_ref[...]).astype(BF16)

    return pl.pallas_call(
        body, name=name, grid=(t // min(ROWS, t),), in_specs=[_row_spec(t, d), _vec_spec(d)],
        out_specs=_row_spec(t, d), out_shape=jax.ShapeDtypeStruct((t, d), BF16),
        compiler_params=_cparams("parallel"),
    )(x, gain)


def _post_res_pre(name, x, ff, g_post, g_next, scale):
    t, d = x.shape

    def body(x_ref, ff_ref, gp_ref, gn_ref, xo_ref, n_ref):
        ff_ = ff_ref[...]
        xn = x_ref[...] + scale * (ff_ * _rstd(ff_) * gp_ref[...])
        xo_ref[...] = xn
        n_ref[...] = (xn * _rstd(xn) * gn_ref[...]).astype(BF16)

    return pl.pallas_call(
        body, name=name, grid=(t // min(ROWS, t),),
        in_specs=[_row_spec(t, d), _row_spec(t, d), _vec_spec(d), _vec_spec(d)],
        out_specs=[_row_spec(t, d), _row_spec(t, d)],
        out_shape=[jax.ShapeDtypeStruct((t, d), F32), jax.ShapeDtypeStruct((t, d), BF16)],
        compiler_params=_cparams("parallel"),
    )(x, ff, g_post, g_next)


def _norm_bwd(xf, gain, dy):
    r = _rstd(xf)
    xh = xf * r
    dxh = dy * gain
    return r * (dxh - xh * jnp.mean(dxh * xh, axis=-1, keepdims=True)), jnp.sum(dy * xh, axis=0, keepdims=True)


def _accumulate(ref, part):
    @pl.when(pl.program_id(0) == 0)
    def _():
        ref[...] = jnp.zeros_like(ref)

    ref[...] += jnp.broadcast_to(part, ref.shape)


def _post_res_loss(name, x, ff, g_post, target, scale):
    t, d = x.shape

    def body(x_ref, ff_ref, gp_ref, tg_ref, dy_ref, loss_ref, dff_ref, dg_ref):
        ff_ = ff_ref[...]
        err = x_ref[...] + scale * (ff_ * _rstd(ff_) * gp_ref[...]) - tg_ref[...]
        dy = err / d
        dy_ref[...] = dy
        _accumulate(loss_ref, 0.5 * jnp.sum(jnp.mean(err * err, axis=-1, keepdims=True), axis=0, keepdims=True))
        dff, dg = _norm_bwd(ff_, gp_ref[...], scale * dy)
        dff_ref[...] = dff.astype(BF16)
        _accumulate(dg_ref, dg)

    return pl.pallas_call(
        body, name=name, grid=(t // min(ROWS, t),),
        in_specs=[_row_spec(t, d), _row_spec(t, d), _vec_spec(d), _row_spec(t, d)],
        out_specs=[_row_spec(t, d), _vec_spec(LANES), _row_spec(t, d), _vec_spec(d)],
        out_shape=[jax.ShapeDtypeStruct((t, d), F32), jax.ShapeDtypeStruct((1, LANES), F32),
                   jax.ShapeDtypeStruct((t, d), BF16), jax.ShapeDtypeStruct((1, d), F32)],
        compiler_params=_cparams("arbitrary"),
    )(x, ff, g_post, target)


def _pre_bwd(name, dn, x, g_pre, dy, deps=(), post=None):
    t, d = x.shape
    n_post = 2 if post else 0
    scale = post[2] if post else None

    def body(dn_ref, x_ref, g_ref, dy_ref, *rest):
        outs = rest[len(rest) - 2 - n_post:]
        dnf = dn_ref[...].astype(F32)
        dpre, dg = _norm_bwd(x_ref[...], g_ref[...], dnf)
        dx = dy_ref[...] + dpre
        outs[0][...] = dx
        _accumulate(outs[1], dg)
        if post:
            ff_ref, gp_ref = rest[0], rest[1]
            dff, dgp = _norm_bwd(ff_ref[...], gp_ref[...], scale * dx)
            outs[2][...] = dff.astype(BF16)
            _accumulate(outs[3], dgp)

    extra_in = [_row_spec(t, d), _vec_spec(d)] if post else []
    extra_out = [_row_spec(t, d), _vec_spec(d)] if post else []
    extra_shape = [jax.ShapeDtypeStruct((t, d), BF16), jax.ShapeDtypeStruct((1, d), F32)] if post else []
    return pl.pallas_call(
        body, name=name, grid=(t // min(ROWS, t),),
        in_specs=[_row_spec(t, d), _row_spec(t, d), _vec_spec(d), _row_spec(t, d)] + extra_in + [ANY] * len(deps),
        out_specs=[_row_spec(t, d), _vec_spec(d)] + extra_out,
        out_shape=[jax.ShapeDtypeStruct((t, d), F32), jax.ShapeDtypeStruct((1, d), F32)] + extra_shape,
        compiler_params=_cparams("arbitrary"),
    )(dn, x, g_pre, dy, *(post[:2] if post else ()), *deps)


def _bdot(a, b, ca, cb, precision=None):
    return lax.dot_general(a, b, (((ca,), (cb,)), ((0,), (0,))), preferred_element_type=F32, precision=precision)


def _tri_masks(g):
    row = lax.broadcasted_iota(jnp.int32, (g, CHUNK, CHUNK), 1)
    col = lax.broadcasted_iota(jnp.int32, (g, CHUNK, CHUNK), 2)
    return col <= row, col >= row


def _ones_matmul(ones_mat, val):
    hi = val.astype(BF16)
    lo = (val - hi.astype(F32)).astype(BF16)
    return _bdot(ones_mat, hi, 2, 1) + _bdot(ones_mat, lo, 2, 1)


def _hgrn_block(z, lb, q, v, cum_mat):
    sg = jax.nn.sigmoid(z)
    f = lb + (1.0 - lb) * sg
    lf = jnp.log(f)
    k = 1.0 - f
    a = _ones_matmul(cum_mat, lf)
    last = jnp.sum(lf, axis=1, keepdims=True)
    e_a = jnp.exp(a)
    e_na = jnp.exp(-a)
    e_t = jnp.exp(last - a)
    return dict(sg=sg, f=f, k=k, decay=jnp.exp(last), e_a=e_a, e_na=e_na, e_t=e_t,
                qd=q * e_a, kd=k * e_na, kt=k * e_t)


def _hgrn_states(state, kv, decay, order):
    entering = [None] * len(order)
    for g in order:
        entering[g] = state
        state = decay[g] * state + kv[g]
    return jnp.stack(entering, axis=0), state


def _hgrn_fwd(p, lb_f, lb_b, gain, n_heads, width):
    t = p.shape[0]
    w = n_heads * HEAD
    blk = min(16, t // CHUNK)
    rows_blk = blk * CHUNK
    n_blocks = t // rows_blk
    fin_rows = min(256, t)

    def body(q_ref, i_ref, zf_ref, zb_ref, g_ref, lbf_ref, lbb_ref, gain_ref, y_ref, o_ref, st_ref):
        low, up = _tri_masks(blk)
        m_low, m_up = low.astype(BF16), up.astype(BF16)
        o_ref[...] = jnp.zeros_like(o_ref)
        st_ref[...] = jnp.zeros_like(st_ref)

        def one(r0, z_ref, lb, slot, rev):
            rows = pl.ds(r0, rows_blk)
            split = lambda ref: ref[rows, :].reshape(blk, CHUNK, HEAD)
            q, v = split(q_ref), split(i_ref)
            c = _hgrn_block(split(z_ref), lb, q, v, m_up if rev else m_low)
            qd, kd, kt, vb = c["qd"].astype(BF16), c["kd"].astype(BF16), c["kt"].astype(BF16), v.astype(BF16)
            pm = jnp.where(up if rev else low, _bdot(qd, kd, 2, 2), 0.0).astype(BF16)
            kv = _bdot(vb, kt, 1, 1)
            order = range(blk - 1, -1, -1) if rev else range(blk)
            entering, st_ref[slot] = _hgrn_states(st_ref[slot], kv, c["decay"], order)
            o = _bdot(pm, vb, 2, 1) + _bdot(qd, entering.astype(BF16), 2, 2)
            o_ref[rows, :] += o.reshape(rows_blk, HEAD)

        def step(n, carry):
            one(pl.multiple_of(n * rows_blk, rows_blk), zf_ref, lbf_ref[...], 0, False)
            one(pl.multiple_of((n_blocks - 1 - n) * rows_blk, rows_blk), zb_ref, lbb_ref[...], 1, True)
            return carry

        lax.fori_loop(0, n_blocks, step, 0)

        def fin(n, carry):
            rows = pl.ds(pl.multiple_of(n * fin_rows, fin_rows), fin_rows)
            o = o_ref[rows, :]
            g = g_ref[rows, :]
            y_ref[rows, :] = (o * _rstd(o) * gain_ref[...] * (g * jax.nn.sigmoid(g))).astype(BF16)
            return carry

        lax.fori_loop(0, t // fin_rows, fin, 0)

    col = lambda grp: pl.BlockSpec((t, HEAD), lambda h: (0, grp * n_heads + h))
    vec = pl.BlockSpec((1, HEAD), lambda h: (0, h))
    out = pl.BlockSpec((t, HEAD), lambda h: (0, h))
    return pl.pallas_call(
        body, name="hgrn_fwd", grid=(n_heads,),
        in_specs=[col(0), col(1), col(2), col(3), col(4), vec, vec, vec],
        out_specs=[out, out],
        out_shape=[jax.ShapeDtypeStruct((t, width), BF16), jax.ShapeDtypeStruct((t, w), F32)],
        scratch_shapes=[pltpu.VMEM((2, HEAD, HEAD), F32)],
        compiler_params=_cparams("parallel"),
    )(p, p, p, p, p, lb_f, lb_b, gain)


def _hgrn_bwd(p, o_raw, dcat, lb_f, lb_b, gain, n_heads):
    t = p.shape[0]
    w = n_heads * HEAD
    n_chunks = t // CHUNK
    blk = min(16, n_chunks)
    rows_blk = blk * CHUNK
    n_blocks = t // rows_blk
    rb = min(256, t)

    def body(q_ref, i_ref, zf_ref, zb_ref, g_ref, o_ref, dy_ref, lbf_ref, lbb_ref, gain_ref,
             dq_ref, di_ref, dzf_ref, dzb_ref, dg_ref, dlbf_ref, dlbb_ref, dgain_ref,
             do_s, dq_s, dv_s, st_s, cur_s):
        low, up = _tri_masks(blk)
        m_low, m_up = low.astype(BF16), up.astype(BF16)
        rowid = lax.broadcasted_iota(jnp.int32, (blk, CHUNK, HEAD), 1)
        gain_v = gain_ref[...]

        def norm_bwd(n, dgain):
            rows = pl.ds(pl.multiple_of(n * rb, rb), rb)
            o = o_ref[rows, :]
            g = g_ref[rows, :]
            dy = dy_ref[rows, :].astype(F32)
            r = _rstd(o)
            oh = o * r
            sg = jax.nn.sigmoid(g)
            dg_ref[rows, :] = (dy * oh * gain_v * (sg * (1.0 + g * (1.0 - sg)))).astype(BF16)
            dno = dy * (g * sg)
            dxh = dno * gain_v
            do_s[rows, :] = r * (dxh - oh * jnp.mean(dxh * oh, axis=-1, keepdims=True))
            return dgain + jnp.sum(dno * oh, axis=0, keepdims=True)

        dgain_ref[...] = lax.fori_loop(0, t // rb, norm_bwd, jnp.zeros((1, HEAD), F32))
        def direction(z_ref, lb_ref, dz_ref, dlb_ref, rev):
            way = int(rev)
            lb = lb_ref[...]
            cum_mat = m_up if rev else m_low
            cum_mat_t = m_low if rev else m_up
            mask = up if rev else low
            last_row = 0 if rev else CHUNK - 1

            order = range(blk - 1, -1, -1) if rev else range(blk)

            def rows_of(j):
                bidx = (n_blocks - 1 - j) if rev else j
                return bidx, pl.ds(pl.multiple_of(bidx * rows_blk, rows_blk), rows_blk)

            def load(rows):
                split = lambda ref: ref[rows, :].reshape(blk, CHUNK, HEAD)
                q, v = split(q_ref), split(i_ref)
                return q, v, _hgrn_block(split(z_ref), lb, q, v, cum_mat)

            def sweep_fwd(j):
                bidx, rows = rows_of(j)
                _, v, c = load(rows)
                kv = _bdot(v.astype(BF16), c["kt"].astype(BF16), 1, 1)
                st_s[way, pl.ds(bidx * blk, blk)], cur_s[2 * way] = _hgrn_states(
                    cur_s[2 * way], kv, c["decay"], order)

            dlb_ref[...] = jnp.zeros_like(dlb_ref)

            def sweep_bwd(jj):
                bidx, rows = rows_of(n_blocks - 1 - jj)
                _, v, c = load(rows)
                st = st_s[way, pl.ds(bidx * blk, blk)]
                do = do_s[rows, :].reshape(blk, CHUNK, HEAD)
                qd, kd, kt, decay = c["qd"], c["kd"], c["kt"], c["decay"]
                qd_b, kd_b, kt_b = qd.astype(BF16), kd.astype(BF16), kt.astype(BF16)
                v_b, do_b, st_b = v.astype(BF16), do.astype(BF16), st.astype(BF16)
                pm = jnp.where(mask, _bdot(qd_b, kd_b, 2, 2), 0.0).astype(BF16)
                dpm = jnp.where(mask, _bdot(do_b, v_b, 2, 2), 0.0).astype(BF16)
                gq = _bdot(do_b, qd_b, 1, 1)
                dstate = cur_s[2 * way + 1]
                after = [None] * blk
                for g in reversed(order):
                    after[g] = dstate
                    dstate = gq[g] + decay[g] * dstate
                cur_s[2 * way + 1] = dstate
                dst = jnp.stack(after, axis=0)
                dst_b = dst.astype(BF16)
                dv = _bdot(pm, do_b, 1, 1) + _bdot(kt_b, dst_b, 2, 2)
                dqd = _bdot(dpm, kd_b, 2, 1) + _bdot(do_b, st_b, 2, 1)
                dkd = _bdot(dpm, qd_b, 1, 1)
                dkt = _bdot(v_b, dst_b, 2, 1)
                dlast = (jnp.sum(dkt * kt, axis=1, keepdims=True)
                         + decay * jnp.sum(dst * st, axis=1, keepdims=True))
                dq_s[way, rows, :] = (dqd * c["e_a"]).reshape(rows_blk, HEAD)
                dv_s[way, rows, :] = dv.reshape(rows_blk, HEAD)
                dk = dkd * c["e_na"] + dkt * c["e_t"]
                da = dqd * qd - dkd * kd - dkt * kt
                da = da + jnp.where(rowid == last_row, dlast, 0.0)
                dlf = _ones_matmul(cum_mat_t, da)
                df = dlf / c["f"] - dk
                sg = c["sg"]
                dz_ref[rows, :] = (df * (1.0 - lb) * (sg * (1.0 - sg))).reshape(rows_blk, HEAD).astype(BF16)
                dlb_ref[...] += jnp.sum((df * (1.0 - sg)).reshape(rows_blk, HEAD), axis=0, keepdims=True)

            return sweep_fwd, sweep_bwd

        ways = [direction(zf_ref, lbf_ref, dzf_ref, dlbf_ref, False),
                direction(zb_ref, lbb_ref, dzb_ref, dlbb_ref, True)]
        cur_s[...] = jnp.zeros_like(cur_s)
        for sweep in range(2):
            def both(j, carry):
                for way in ways:
                    way[sweep](j)
                return carry

            lax.fori_loop(0, n_blocks, both, 0)
        dq_ref[...] = (dq_s[0] + dq_s[1]).astype(BF16)
        di_ref[...] = (dv_s[0] + dv_s[1]).astype(BF16)

    col = lambda grp: pl.BlockSpec((t, HEAD), lambda h: (0, grp * n_heads + h))
    one = pl.BlockSpec((t, HEAD), lambda h: (0, h))
    vec = pl.BlockSpec((1, HEAD), lambda h: (0, h))
    big = jax.ShapeDtypeStruct((t, w), BF16)
    small = jax.ShapeDtypeStruct((1, w), F32)
    return pl.pallas_call(
        body, name="hgrn_bwd", grid=(n_heads,),
        in_specs=[col(0), col(1), col(2), col(3), col(4), one, one, vec, vec, vec],
        out_specs=[one] * 5 + [vec] * 3,
        out_shape=[big] * 5 + [small] * 3,
        scratch_shapes=[pltpu.VMEM((t, HEAD), F32), pltpu.VMEM((2, t, HEAD), F32), pltpu.VMEM((2, t, HEAD), F32),
                        pltpu.VMEM((2, n_chunks, HEAD, HEAD), F32), pltpu.VMEM((4, HEAD, HEAD), F32)],
        compiler_params=_cparams("parallel"),
    )(p, p, p, p, p, o_raw, dcat, lb_f, lb_b, gain)


def _t5_bucket_index():
    c = np.arange(WINDOW)[:, None]
    s = np.arange(KEY_SPAN)[None, :]
    rel = s - WINDOW - c
    nb = REL_BUCKETS // 2
    max_exact = nb // 2
    bucket = (rel > 0).astype(np.int32) * nb
    n = np.abs(rel)
    large = max_exact + (np.log(np.maximum(n, 1) / max_exact) / np.log(REL_MAX_DIST / max_exact)
                         * (nb - max_exact)).astype(np.int32)
    large = np.minimum(large, nb - 1)
    return bucket + np.where(n < max_exact, n, large).astype(np.int32)


def _bias_build(table, idx):
    n_attn = table.shape[1]

    def body(tab_ref, idx_ref, o_ref):
        h = pl.program_id(0)
        idx_v = idx_ref[...]
        acc = jnp.zeros((WINDOW, KEY_SPAN), F32)
        for b in range(REL_BUCKETS):
            acc = jnp.where(idx_v == b, tab_ref[b, h], acc)
        o_ref[...] = acc

    return pl.pallas_call(
        body, name="bias_build", grid=(n_attn,),
        in_specs=[pl.BlockSpec(memory_space=pltpu.SMEM), pl.BlockSpec((WINDOW, KEY_SPAN), lambda h: (0, 0))],
        out_specs=pl.BlockSpec((None, WINDOW, KEY_SPAN), lambda h: (h, 0, 0)),
        out_shape=jax.ShapeDtypeStruct((n_attn, WINDOW, KEY_SPAN), F32), compiler_params=_cparams("parallel"),
    )(table, idx)


def _bias_reduce(dbias, idx):
    n_attn = dbias.shape[0]

    def body(idx_ref, d_ref, o_ref):
        idx_v = idx_ref[...]
        dv = d_ref[...]
        rows = lax.broadcasted_iota(jnp.int32, (REL_BUCKETS, LANES), 0)
        acc = jnp.zeros((REL_BUCKETS, LANES), F32)
        for b in range(REL_BUCKETS):
            part = jnp.sum(jnp.where(idx_v == b, dv, 0.0), axis=1, keepdims=True)
            acc = jnp.where(rows == b, jnp.sum(part, axis=0, keepdims=True), acc)
        o_ref[...] = acc

    return pl.pallas_call(
        body, name="bias_reduce", grid=(n_attn,),
        in_specs=[pl.BlockSpec((WINDOW, KEY_SPAN), lambda h: (0, 0)),
                  pl.BlockSpec((None, WINDOW, KEY_SPAN), lambda h: (h, 0, 0))],
        out_specs=pl.BlockSpec((None, REL_BUCKETS, LANES), lambda h: (h, 0, 0)),
        out_shape=jax.ShapeDtypeStruct((n_attn, REL_BUCKETS, LANES), F32), compiler_params=_cparams("parallel"),
    )(idx, dbias)


def _attn_probs(q, kb, bias, sink, valid):
    s = _dot_nt(q, kb) / math.sqrt(HEAD) + bias
    s = jnp.where(valid, s, NEG_INF)
    m = jnp.maximum(jnp.max(s, axis=-1, keepdims=True), sink)
    e = jnp.exp(s - m)
    e_sink = jnp.exp(sink - m)
    den = jnp.sum(e, axis=-1, keepdims=True) + e_sink
    return e / den, e_sink / den


def _attn_valid(n, t, grp):
    c = lax.broadcasted_iota(jnp.int32, (grp * WINDOW, KEY_SPAN), 0) & (WINDOW - 1)
    s = lax.broadcasted_iota(jnp.int32, (grp * WINDOW, KEY_SPAN), 1)
    rel = s - WINDOW - c
    key_pos = n * WINDOW - WINDOW + s
    return (jnp.abs(rel) <= WINDOW) & (key_pos >= 0) & (key_pos < t)


def _stack_heads(ref, grp):
    return jnp.concatenate([ref[:, g * HEAD:(g + 1) * HEAD] for g in range(grp)], axis=0).astype(BF16)


def _sink_column(sink_ref, x, grp):
    return jnp.concatenate([jnp.full((WINDOW, 1), sink_ref[0, x * grp + g], F32) for g in range(grp)], axis=0)


def _attn_specs(t, n_hgrn, n_attn):
    grp = n_attn // KV_HEADS
    nb = t // WINDOW
    cq = 5 * n_hgrn
    ck = cq + n_attn
    cv = ck + KV_HEADS
    q_spec = pl.BlockSpec((WINDOW, grp * HEAD), lambda x, n: (n, cq // grp + x))
    kv = lambda base, off: pl.BlockSpec(
        (WINDOW, HEAD), lambda x, n: (jnp.clip(n + off, 0, nb - 1), base + x))
    band = [kv(ck, -1), kv(ck, 0), kv(ck, 1), kv(cv, -1), kv(cv, 0), kv(cv, 1)]
    bias_spec = pl.BlockSpec((grp, WINDOW, KEY_SPAN), lambda x, n: (x, 0, 0))
    sink_spec = pl.BlockSpec(memory_space=pltpu.SMEM)
    return grp, nb, q_spec, band, bias_spec, sink_spec


def _attn_fwd(p, bias, sink, cat, n_hgrn, n_attn):
    t = p.shape[0]
    grp, nb, q_spec, band, bias_spec, sink_spec = _attn_specs(t, n_hgrn, n_attn)

    def body(q_ref, kp, kc, kn, vp, vc, vn, bias_ref, sink_ref, cat_ref, y_ref):
        x, n = pl.program_id(0), pl.program_id(1)
        kb = jnp.concatenate([kp[...], kc[...], kn[...]], axis=0).astype(BF16)
        vb = jnp.concatenate([vp[...], vc[...], vn[...]], axis=0).astype(BF16)
        pr, _ = _attn_probs(_stack_heads(q_ref, grp), kb, bias_ref[...].reshape(grp * WINDOW, KEY_SPAN),
                            _sink_column(sink_ref, x, grp), _attn_valid(n, t, grp))
        y = _dot(pr.astype(BF16), vb).astype(BF16)
        for g in range(grp):
            y_ref[:, g * HEAD:(g + 1) * HEAD] = y[g * WINDOW:(g + 1) * WINDOW]

    return pl.pallas_call(
        body, name="attn_fwd", grid=(KV_HEADS, nb),
        in_specs=[q_spec] + band + [bias_spec, sink_spec, ANY],
        out_specs=pl.BlockSpec((WINDOW, grp * HEAD), lambda x, n: (n, n_hgrn // grp + x)),
        out_shape=jax.ShapeDtypeStruct(cat.shape, BF16), input_output_aliases={9: 0},
        compiler_params=_cparams("parallel", "parallel"),
    )(p, p, p, p, p, p, p, bias, sink, cat)


def _attn_bwd(p, dcat, bias, sink, n_hgrn, n_attn, deps=()):
    t = p.shape[0]
    grp, nb, q_spec, band, bias_spec, sink_spec = _attn_specs(t, n_hgrn, n_attn)
    inv = 1.0 / math.sqrt(HEAD)

    def body(q_ref, kp, kc, kn, vp, vc, vn, bias_ref, sink_ref, do_ref, *rest):
        dq_ref, dk_ref, dv_ref, dbias_ref, dsink_ref, dk_s, dv_s = rest[-7:]
        x, n = pl.program_id(0), pl.program_id(1)

        @pl.when(n == 0)
        def _():
            dk_s[...] = jnp.zeros_like(dk_s)
            dv_s[...] = jnp.zeros_like(dv_s)
            dbias_ref[...] = jnp.zeros_like(dbias_ref)
            dsink_ref[...] = jnp.zeros_like(dsink_ref)

        kb = jnp.concatenate([kp[...], kc[...], kn[...]], axis=0).astype(BF16)
        vb = jnp.concatenate([vp[...], vc[...], vn[...]], axis=0).astype(BF16)
        q = _stack_heads(q_ref, grp)
        do = _stack_heads(do_ref, grp)
        pr, p_sink = _attn_probs(q, kb, bias_ref[...].reshape(grp * WINDOW, KEY_SPAN),
                                 _sink_column(sink_ref, x, grp), _attn_valid(n, t, grp))
        dpr = _dot_nt(do, vb)
        delta = jnp.sum(pr * dpr, axis=-1, keepdims=True)
        ds = pr * (dpr - delta)
        ds_b = ds.astype(BF16)
        dq = (_dot(ds_b, kb) * inv).astype(BF16)
        dsink = -p_sink * delta
        for g in range(grp):
            head = slice(g * WINDOW, (g + 1) * WINDOW)
            dq_ref[:, g * HEAD:(g + 1) * HEAD] = dq[head]
            dbias_ref[g] += ds[head]
            dsink_ref[g:g + 1, :] += jnp.broadcast_to(jnp.sum(dsink[head], axis=0, keepdims=True), (1, WINDOW))
        rows = pl.ds(pl.multiple_of(n * WINDOW, WINDOW), KEY_SPAN)
        dk_s[rows, :] += _dot_tn(ds_b, q) * inv
        dv_s[rows, :] += _dot_tn(pr.astype(BF16), do)

        @pl.when(n == nb - 1)
        def _():
            dk_ref[...] = dk_s[pl.ds(WINDOW, t), :].astype(BF16)
            dv_ref[...] = dv_s[pl.ds(WINDOW, t), :].astype(BF16)

    do_spec = pl.BlockSpec((WINDOW, grp * HEAD), lambda x, n: (n, n_hgrn // grp + x))
    kv_out = pl.BlockSpec((t, HEAD), lambda x, n: (0, x))
    return pl.pallas_call(
        body, name="attn_bwd", grid=(KV_HEADS, nb),
        in_specs=[q_spec] + band + [bias_spec, sink_spec, do_spec] + [ANY] * len(deps),
        out_specs=[pl.BlockSpec((WINDOW, grp * HEAD), lambda x, n: (n, x)), kv_out, kv_out,
                   bias_spec, pl.BlockSpec((None, grp, WINDOW), lambda x, n: (x, 0, 0))],
        out_shape=[jax.ShapeDtypeStruct((t, n_attn * HEAD), BF16),
                   jax.ShapeDtypeStruct((t, KV_HEADS * HEAD), BF16),
                   jax.ShapeDtypeStruct((t, KV_HEADS * HEAD), BF16),
                   jax.ShapeDtypeStruct((n_attn, WINDOW, KEY_SPAN), F32),
                   jax.ShapeDtypeStruct((KV_HEADS, grp, WINDOW), F32)],
        scratch_shapes=[pltpu.VMEM((t + 2 * WINDOW, HEAD), F32), pltpu.VMEM((t + 2 * WINDOW, HEAD), F32)],
        compiler_params=_cparams("parallel", "arbitrary"),
    )(p, p, p, p, p, p, p, bias, sink, dcat, *deps)


def _position():
    return lax.axis_index("x"), lax.axis_index("y"), lax.axis_index("c")


def _handshake(peers):
    barrier = pltpu.get_barrier_semaphore()
    for peer in peers:
        pl.semaphore_signal(barrier, inc=1, device_id=peer, device_id_type=MESH)
    pl.semaphore_wait(barrier, len(peers))


def _sequencer(name, collective_id, scratch_types):
    return functools.partial(
        pl.kernel, mesh=plsc.ScalarSubcoreMesh(axis_name="sc", num_cores=1), name=name,
        scratch_types=scratch_types, compiler_params=pltpu.CompilerParams(collective_id=collective_id))


def _all_gather(name, shard, collective_id):
    rows = shard.shape[0]
    assert rows % 2 == 0
    rh = rows // 2
    src = jax.new_ref(shard, memory_space=pltpu.MemorySpace.HBM)
    out = jax.empty_ref(jax.ShapeDtypeStruct((N_DEV,) + shard.shape, shard.dtype),
                        memory_space=pltpu.MemorySpace.HBM)
    n_copies = 11

    @_sequencer(name, collective_id, (pltpu.SemaphoreType.DMA((n_copies,)), pltpu.SemaphoreType.DMA((n_copies,)),
                                      pltpu.SemaphoreType.DMA))
    def launch(send_sems, recv_sems, local_sem):
        x, y, c = _position()
        sibling = (x, y, 1 - c)
        xn, yn, dg = (1 - x, y), (x, 1 - y), (1 - x, 1 - y)
        _handshake([sibling, (*xn, c), (*yn, c)])

        def part(ref, half):
            return ref if half is None else ref.at[pl.ds(half * rh, rh)]

        def slot(chip, core, half=None):
            return part(out.at[4 * chip[0] + 2 * chip[1] + core], half)

        def copy(k, chip, core, half, to, own=False):
            return pltpu.make_async_remote_copy(
                src_ref=part(src, half) if own else slot(chip, core, half), dst_ref=slot(chip, core, half),
                send_sem=send_sems.at[k], recv_sem=recv_sems.at[k], device_id=to, device_id_type=MESH)

        def landed(k, chip, core, half):
            copy(k, chip, core, half, (x, y, c)).wait_recv()

        mine = pltpu.make_async_copy(src, slot((x, y), c), local_sem)
        mine.start()
        sent = [copy(0, (x, y), c, None, sibling, own=True),
                copy(1, (x, y), c, 0, (*xn, c), own=True), copy(3, (x, y), c, 1, (*yn, c), own=True),
                copy(2, (x, y), c, 1, (*xn, c), own=True), copy(4, (x, y), c, 0, (*yn, c), own=True)]
        for cp in sent:
            cp.start()

        def then(cp):
            cp.start()
            sent.append(cp)

        landed(1, xn, c, 0)
        then(copy(5, xn, c, 0, (*yn, c)))
        landed(3, yn, c, 1)
        then(copy(6, yn, c, 1, (*xn, c)))
        landed(2, xn, c, 1)
        then(copy(7, xn, c, None, sibling))
        landed(4, yn, c, 0)
        then(copy(8, yn, c, None, sibling))
        landed(5, dg, c, 0)
        then(copy(9, dg, c, 0, sibling))
        landed(6, dg, c, 1)
        then(copy(10, dg, c, 1, sibling))
        landed(0, (x, y), 1 - c, None)
        landed(7, xn, 1 - c, None)
        landed(8, yn, 1 - c, None)
        landed(9, dg, 1 - c, 0)
        landed(10, dg, 1 - c, 1)
        for cp in sent:
            cp.wait_send()
        mine.wait()

    launch()
    return out[...]


HBM = pl.BlockSpec(memory_space=pltpu.HBM)
SEM = pl.BlockSpec(memory_space=pltpu.SEMAPHORE)
EFFECT = pltpu.SideEffectType.DATAFLOW_SIDE_EFFECTING


def _pair_copies(s_ref, land_ref, send_sems, recv_sems):
    x, y, c = _position()
    return [pltpu.make_async_remote_copy(
        src_ref=s_ref.at[2 * k + (1 - c)], dst_ref=land_ref.at[k], send_sem=send_sems.at[k],
        recv_sem=recv_sems.at[k], device_id=(x, y, 1 - c), device_id_type=MESH) for k in range(4)]


def _pair_start(name, stack):
    land_shape = (4,) + stack.shape[1:]

    def body(s_ref, land_ref, send_sems, recv_sems, s_thru, land_thru, token):
        for cp in _pair_copies(s_ref, land_ref, send_sems, recv_sems):
            cp.start()
        token[...] = jnp.zeros_like(token)

    return pl.pallas_call(
        body, name=name,
        out_shape=(pltpu.SemaphoreType.DMA((4,)), pltpu.SemaphoreType.DMA((4,)),
                   pltpu.HBM(stack.shape, stack.dtype), pltpu.HBM(land_shape, stack.dtype),
                   jax.ShapeDtypeStruct((8, LANES), F32)),
        in_specs=(HBM, HBM), out_specs=(SEM, SEM, HBM, HBM, pl.BlockSpec(memory_space=pltpu.VMEM)),
        input_output_aliases={0: 2, 1: 3}, compiler_params=pltpu.CompilerParams(has_side_effects=EFFECT),
    )(pltpu.with_memory_space_constraint(stack, pltpu.HBM),
      pltpu.with_memory_space_constraint(lax.empty(land_shape, stack.dtype), pltpu.HBM))


def _pair_wait(name, started, after):
    send_sems, recv_sems, s_thru, land_thru, _ = started

    def body(s_ref, land_ref, send_sems, recv_sems, after_ref, s_out, land_out):
        for cp in _pair_copies(s_ref, land_ref, send_sems, recv_sems):
            cp.wait_send()
            cp.wait_recv()

    return pl.pallas_call(
        body, name=name,
        out_shape=(pltpu.HBM(s_thru.shape, s_thru.dtype), pltpu.HBM(land_thru.shape, land_thru.dtype)),
        in_specs=(HBM, HBM, SEM, SEM, ANY), out_specs=(HBM, HBM), input_output_aliases={0: 0, 1: 1},
        compiler_params=pltpu.CompilerParams(has_side_effects=EFFECT),
    )(s_thru, land_thru, send_sems, recv_sems, after)


def _pair_sum(name, stack, other, core):
    _, r, c = stack.shape
    tr = _row_tile(r, 1024)

    def body(core_ref, a_ref, b_ref, o_ref):
        o_ref[...] = (a_ref[...].astype(F32) + b_ref[...].astype(F32)).astype(o_ref.dtype)

    grid_spec = pltpu.PrefetchScalarGridSpec(
        num_scalar_prefetch=1, grid=(4, r // tr),
        in_specs=[pl.BlockSpec((None, tr, c), lambda k, i, core_ref: (2 * k + core_ref[0], i, 0)),
                  pl.BlockSpec((None, tr, c), lambda k, i, core_ref: (k, i, 0))],
        out_specs=pl.BlockSpec((None, tr, c), lambda k, i, core_ref: (k, i, 0)))
    return pl.pallas_call(
        body, name=name, grid_spec=grid_spec, out_shape=jax.ShapeDtypeStruct((4, r, c), stack.dtype),
        compiler_params=_cparams("parallel", "parallel"),
    )(core, stack, other)


def _chip_exchange(name, sums, collective_id):
    src = jax.new_ref(sums, memory_space=pltpu.MemorySpace.HBM)
    out = jax.empty_ref(jax.ShapeDtypeStruct((3,) + sums.shape[1:], sums.dtype),
                        memory_space=pltpu.MemorySpace.HBM)

    @_sequencer(name, collective_id, (pltpu.SemaphoreType.DMA((3,)), pltpu.SemaphoreType.DMA((3,))))
    def launch(send_sems, recv_sems):
        x, y, c = _position()
        chips = [(1 - x, y), (x, 1 - y), (1 - x, 1 - y)]
        _handshake([(*chip, c) for chip in chips])
        copies = [pltpu.make_async_remote_copy(
            src_ref=src.at[2 * px + py], dst_ref=out.at[j], send_sem=send_sems.at[j],
            recv_sem=recv_sems.at[j], device_id=(px, py, c), device_id_type=MESH)
            for j, (px, py) in enumerate(chips)]
        for cp in copies:
            cp.start()
        for cp in copies:
            cp.wait()

    launch()
    return out[...]


def _small_rows(shapes):
    first, row = [], 0
    for r, c in shapes:
        first.append(row)
        row += r * (c // LANES) if c % LANES == 0 else r
        row = -(-row // 8) * 8
    return first, row


def _small_move(packed, row, ref, to_packed):
    r, c = ref.shape
    if c % LANES:
        if to_packed:
            packed[row:row + r, 0:c] = ref[...]
        else:
            ref[...] = packed[row:row + r, 0:c]
        return
    per = c // LANES
    for i in range(r):
        for j in range(per):
            at = row + i * per + j
            if to_packed:
                packed[at:at + 1, :] = ref[i:i + 1, j * LANES:(j + 1) * LANES]
            else:
                ref[i:i + 1, j * LANES:(j + 1) * LANES] = packed[at:at + 1, :]


def _small_step(name, parts, ws, ms, vs, deps=()):
    n_par = len(ws)
    first, rows = _small_rows([p.shape for p in parts])
    vm = pl.BlockSpec(memory_space=pltpu.VMEM)
    buf = pltpu.VMEM((rows, LANES), F32)

    def reduce_body(*refs):
        part_refs, (sum_out, mine, gather, send_sems, recv_sems) = refs[:n_par + 1], refs[-5:]
        x, y, c = _position()
        me = 4 * x + 2 * y + c
        mine[...] = jnp.zeros_like(mine)
        for k, ref in enumerate(part_refs):
            _small_move(mine, first[k], ref, True)
        gather[me] = mine[...]
        copies = []
        for k in range(1, N_DEV):
            peer = (x ^ (k >> 2), y ^ ((k >> 1) & 1), c ^ (k & 1))
            copies.append(pltpu.make_async_remote_copy(
                src_ref=mine, dst_ref=gather.at[me], send_sem=send_sems.at[k - 1],
                recv_sem=recv_sems.at[k - 1], device_id=peer, device_id_type=MESH))
        for cp in copies:
            cp.start()
        for k in range(1, N_DEV):
            peer_slot = 4 * (x ^ (k >> 2)) + 2 * (y ^ ((k >> 1) & 1)) + (c ^ (k & 1))
            pltpu.make_async_remote_copy(
                src_ref=mine, dst_ref=gather.at[peer_slot], send_sem=send_sems.at[k - 1],
                recv_sem=recv_sems.at[k - 1], device_id=(x, y, c), device_id_type=MESH).wait()
        acc = gather[0]
        for j in range(1, N_DEV):
            acc = acc + gather[j]
        sum_out[...] = acc

    summed = pl.pallas_call(
        reduce_body, name=name + "_reduce", in_specs=[vm] * (n_par + 1) + [ANY] * len(deps), out_specs=vm,
        out_shape=jax.ShapeDtypeStruct((rows, LANES), F32),
        scratch_shapes=[buf, pltpu.VMEM((N_DEV, rows, LANES), F32), pltpu.SemaphoreType.DMA((7,)),
                        pltpu.SemaphoreType.DMA((7,))],
    )(*parts, *deps)

    def adam_body(*refs):
        sum_ref, refs = refs[0], refs[1:]
        w_refs, m_refs, v_refs, refs = refs[:n_par], refs[n_par:2 * n_par], refs[2 * n_par:3 * n_par], refs[3 * n_par:]
        g_out, d_out, m_out, v_out = (refs[i * n_par:(i + 1) * n_par] for i in range(4))
        loss_out = refs[4 * n_par]
        w_p, m_p, v_p, d_p = refs[4 * n_par + 1:]
        for packed in (w_p, m_p, v_p):
            packed[...] = jnp.zeros_like(packed)
        for k in range(n_par):
            for packed, src in ((w_p, w_refs[k]), (m_p, m_refs[k]), (v_p, v_refs[k])):
                _small_move(packed, first[k], src, True)
        delta, m_new, v_new = _adam_math(w_p[...], sum_ref[...], m_p[...], v_p[...])
        d_p[...] = delta
        m_p[...] = m_new
        v_p[...] = v_new
        for k in range(n_par):
            for packed, dst in ((sum_ref, g_out[k]), (d_p, d_out[k]), (m_p, m_out[k]), (v_p, v_out[k])):
                _small_move(packed, first[k], dst, False)
        _small_move(sum_ref, first[n_par], loss_out, False)

    like = [jax.ShapeDtypeStruct(w.shape, F32) for w in ws]
    outs = pl.pallas_call(
        adam_body, name=name + "_adam", in_specs=[vm] * (3 * n_par + 1), out_specs=[vm] * (4 * n_par + 1),
        out_shape=like * 4 + [jax.ShapeDtypeStruct((1, LANES), F32)], scratch_shapes=[buf, buf, buf, buf],
    )(summed, *ws, *ms, *vs)
    return (outs[:n_par], outs[n_par:2 * n_par], outs[2 * n_par:3 * n_par], outs[3 * n_par:4 * n_par],
            outs[4 * n_par])


def _adam_math(w, g, m, v):
    m = ADAM_B1 * m + (1.0 - ADAM_B1) * g
    v = ADAM_B2 * v + (1.0 - ADAM_B2) * jnp.square(g)
    m_hat = m / (1.0 - ADAM_B1 ** ADAM_STEP)
    v_hat = v / (1.0 - ADAM_B2 ** ADAM_STEP)
    delta = -ADAM_LR * (m_hat / (jnp.sqrt(v_hat) + ADAM_EPS) + ADAM_WD * w)
    return delta, m, v


def _adam_shard(name, w, m, v, sums, recv, chip, deps=(), first_row=0, earlier=()):
    r, c = w.shape
    rows = sums.shape[1]
    tr = _row_tile(rows, 256)
    assert first_row % tr == 0
    skip = first_row // tr

    def body(chip_ref, w_ref, m_ref, v_ref, own_ref, r0_ref, r1_ref, r2_ref, *rest):
        g_out, d_out, m_out, v_out = rest[-4:]
        g = ((own_ref[...].astype(F32) + r0_ref[...].astype(F32)) + r1_ref[...].astype(F32)) + r2_ref[...].astype(F32)
        delta, m_new, v_new = _adam_math(w_ref[...], g, m_ref[...], v_ref[...])
        g_out[...] = g
        d_out[...] = delta
        m_out[...] = m_new
        v_out[...] = v_new

    plain = pl.BlockSpec((tr, c), lambda i, chip_ref: (skip + i, 0))
    piece = lambda j: pl.BlockSpec((None, tr, c), lambda i, chip_ref: (j, i, 0))
    grid_spec = pltpu.PrefetchScalarGridSpec(
        num_scalar_prefetch=1, grid=(rows // tr,),
        in_specs=[plain, plain, plain,
                  pl.BlockSpec((None, tr, c), lambda i, chip_ref: (chip_ref[0], i, 0)),
                  piece(0), piece(1), piece(2)] + [ANY] * (len(earlier) + len(deps)),
        out_specs=[plain] * 4)
    shape = jax.ShapeDtypeStruct((r, c), F32)
    return pl.pallas_call(
        body, name=name, grid_spec=grid_spec, out_shape=[shape] * 4, compiler_params=_cparams("parallel"),
        input_output_aliases={8 + k: k for k in range(len(earlier))},
    )(chip, w, m, v, sums, recv, recv, recv, *earlier, *deps)


def _reduce_scatter(tag, started, after, core, collective_id):
    grad_stack, other = _pair_wait("rs_pair_wait_" + tag, started, after)
    sums = _pair_sum("rs_sum_" + tag, grad_stack, other, core)
    return sums, _chip_exchange("rs_chip_" + tag, sums, collective_id)


SMALL = ("pre_norm_ffn1", "post_norm_ffn1", "pre_norm_mix", "post_norm_mix", "hgrn_lower_bounds_fwd",
         "hgrn_lower_bounds_bwd", "hgrn_out_norm", "attn_sink", "pre_norm_ffn2", "post_norm_ffn2", "rel_bias_table")
BIG = ("w_ffn1_gate_up", "w_ffn1_down", "w_mix_in", "w_mix_out", "w_ffn2_gate_up", "w_ffn2_down")
AG_ID = {n: 1 + i for i, n in enumerate(BIG)}
RS_ID = {n: 7 + i for i, n in enumerate(BIG)}
RS_ID.update(w_ffn1_gate_up_a=RS_ID["w_ffn1_gate_up"], w_ffn1_gate_up_b=13)
ORDER = ("pre_norm_ffn1", "post_norm_ffn1", "w_ffn1_gate_up", "w_ffn1_down", "pre_norm_mix", "post_norm_mix",
         "w_mix_in", "hgrn_lower_bounds_fwd", "hgrn_lower_bounds_bwd", "hgrn_out_norm", "attn_sink", "w_mix_out",
         "pre_norm_ffn2", "post_norm_ffn2", "w_ffn2_gate_up", "w_ffn2_down", "rel_bias_table")


def kernel(x, pre_norm_ffn1, post_norm_ffn1, w_ffn1_gate_up, w_ffn1_down, pre_norm_mix, post_norm_mix, w_mix_in, hgrn_lower_bounds_fwd, hgrn_lower_bounds_bwd, hgrn_out_norm, attn_sink, w_mix_out, pre_norm_ffn2, post_norm_ffn2, w_ffn2_gate_up, w_ffn2_down, rel_bias_table, loss_target, m_pre_norm_ffn1, m_post_norm_ffn1, m_w_ffn1_gate_up, m_w_ffn1_down, m_pre_norm_mix, m_post_norm_mix, m_w_mix_in, m_hgrn_lower_bounds_fwd, m_hgrn_lower_bounds_bwd, m_hgrn_out_norm, m_attn_sink, m_w_mix_out, m_pre_norm_ffn2, m_post_norm_ffn2, m_w_ffn2_gate_up, m_w_ffn2_down, m_rel_bias_table, v_pre_norm_ffn1, v_post_norm_ffn1, v_w_ffn1_gate_up, v_w_ffn1_down, v_pre_norm_mix, v_post_norm_mix, v_w_mix_in, v_hgrn_lower_bounds_fwd, v_hgrn_lower_bounds_bwd, v_hgrn_out_norm, v_attn_sink, v_w_mix_out, v_pre_norm_ffn2, v_post_norm_ffn2, v_w_ffn2_gate_up, v_w_ffn2_down, v_rel_bias_table):
    args = dict(locals())
    wts = {n: args[n] for n in ORDER}
    mom = {n: args["m_" + n] for n in ORDER}
    var = {n: args["v_" + n] for n in ORDER}

    x0 = x[0]
    target = loss_target[0]
    t, d = x0.shape
    n_hgrn = d // 2 // HEAD
    n_attn = (d - d // 2) // HEAD
    core = lax.axis_index("c").astype(jnp.int32).reshape(1)
    chip = (2 * lax.axis_index("x") + lax.axis_index("y")).astype(jnp.int32).reshape(1)

    def local(a, name):
        return jnp.transpose(a[0]) if name == "w_mix_in" else a[0]

    full = {n: _all_gather("ag_" + n, local(wts[n], n).astype(BF16), AG_ID[n]) for n in BIG}
    w_gu1, w_gu2 = full["w_ffn1_gate_up"], full["w_ffn2_gate_up"]
    w_d1 = full["w_ffn1_down"].reshape(-1, d)
    w_d2 = full["w_ffn2_down"].reshape(-1, d)
    w_out = full["w_mix_out"].reshape(-1, d)
    w_in_t = full["w_mix_in"].reshape(-1, d)

    g = {n: wts[n] for n in SMALL}
    lb_f = jax.nn.softmax(g["hgrn_lower_bounds_fwd"], axis=0)[0:1]
    lb_b = jax.nn.softmax(g["hgrn_lower_bounds_bwd"], axis=0)[0:1]
    bucket_idx = jnp.asarray(_t5_bucket_index())
    bias = _bias_build(g["rel_bias_table"], bucket_idx)

    n1 = _pre_norm("pre_norm1", x0, g["pre_norm_ffn1"])
    a1, gu1 = _ffn_up("ffn1_gate_up", n1, w_gu1)
    ff1 = _matmul("ffn1_down", a1, w_d1, mode="nn", out_dtype=F32)
    x1, h = _post_res_pre("res1", x0, ff1, g["post_norm_ffn1"], g["pre_norm_mix"], 0.5)
    p = _matmul("mix_in", h, w_in_t, mode="nt", out_dtype=F32, tm=2048, tn=512)
    y_h, o_raw = _hgrn_fwd(p, lb_f, lb_b, g["hgrn_out_norm"], n_hgrn, d)
    cat = _attn_fwd(p, bias, g["attn_sink"], y_h, n_hgrn, n_attn)
    mixed = _matmul("mix_out", cat, w_out, mode="nn", out_dtype=F32)
    x2, n2 = _post_res_pre("res2", x1, mixed, g["post_norm_mix"], g["pre_norm_ffn2"], 1.0)
    a2, gu2 = _ffn_up("ffn2_gate_up", n2, w_gu2)
    ff2 = _matmul("ffn2_down", a2, w_d2, mode="nn", out_dtype=F32)
    small_grad = {}
    dy3, loss_part, dff2, small_grad["post_norm_ffn2"] = _post_res_loss(
        "res3_loss", x2, ff2, g["post_norm_ffn2"], target, 0.5)

    scattered = {}

    pending = []

    def scatter(name, grad_stack):
        started = _pair_start("rs_pair_" + name, grad_stack)
        pending.append((name, started))
        return [started[4]]

    def settle(after, count=len(BIG)):
        deps = []
        while pending and count:
            name, started = pending.pop(0)
            scattered[name] = _reduce_scatter(name, started, after, core, RS_ID[name])
            deps.append(scattered[name][0])
            count -= 1
        return deps

    def ffn_bwd(tag, dff, a, gu, n_in, w_gu, w_d, gu_name, d_name, last):
        dep = settle(dff)

        def dw_down(deps):
            return scatter(d_name, _matmul("dw_down" + tag, a, dff, mode="tn", out_dtype=BF16, tn=2048,
                                           deps=deps).reshape(N_DEV, -1, d))

        def dw_gate_up(part, rows, deps):
            return scatter(gu_name + part, _matmul("dw_gate_up" + tag + part, n_in[:, rows], dgu, mode="tn",
                                                   stack=True, halves=True, out_dtype=BF16, deps=deps))

        dep = dw_down(dep)
        dgu = _ffn_dact("d_act" + tag, dff, w_d, gu, deps=dep)
        dep = settle(dgu)
        if last:
            half = d // 2
            dep = dw_gate_up("_a", slice(0, half), dep)
            dep = dw_gate_up("_b", slice(half, d), dep)
            dep = settle(dep[0], count=1) + dep
        else:
            dep = dw_gate_up("", slice(0, d), dep)
        dn = _matmul("d_norm" + tag, dgu, w_gu, mode="nt", stack=True, halves=True, out_dtype=BF16, deps=dep)
        return dn, settle(dn)

    dn2, dep = ffn_bwd("2", dff2, a2, gu2, n2, w_gu2, w_d2, "w_ffn2_gate_up", "w_ffn2_down", last=False)
    dx2, small_grad["pre_norm_ffn2"], dmixed, small_grad["post_norm_mix"] = _pre_bwd(
        "pre_bwd2", dn2, x2, g["pre_norm_ffn2"], dy3, deps=dep, post=(mixed, g["post_norm_mix"], 1.0))
    dep = settle(dmixed)
    dcat = _matmul("d_cat", dmixed, w_out, mode="nt", out_dtype=BF16, deps=dep)
    dep = scatter("w_mix_out", _matmul("dw_mix_out", cat, dmixed, mode="tn", out_dtype=BF16).reshape(N_DEV, -1, d))
    dq_a, dk_a, dv_a, dbias, dsink_rows = _attn_bwd(p, dcat, bias, g["attn_sink"], n_hgrn, n_attn, deps=dep)
    dq_h, di_h, dzf, dzb, dg_h, dlb_f, dlb_b, small_grad["hgrn_out_norm"] = _hgrn_bwd(
        p, o_raw, dcat, lb_f, lb_b, g["hgrn_out_norm"], n_hgrn)
    dp = jnp.concatenate([dq_h, di_h, dzf, dzb, dg_h, dq_a, dk_a, dv_a], axis=1)
    dep = settle(dp)
    dh = _matmul("d_h", dp, w_in_t, mode="nn", out_dtype=BF16, tm=2048, deps=dep)
    dep = scatter("w_mix_in", _matmul("dw_mix_in", dp, h, mode="tn", out_dtype=BF16, tm=512,
                                         tn=2048).reshape(N_DEV, -1, d))
    dx1, small_grad["pre_norm_mix"], dff1, small_grad["post_norm_ffn1"] = _pre_bwd(
        "pre_bwd_mix", dh, x1, g["pre_norm_mix"], dx2, deps=dep, post=(ff1, g["post_norm_ffn1"], 0.5))

    dn1, dep = ffn_bwd("1", dff1, a1, gu1, n1, w_gu1, w_d1, "w_ffn1_gate_up", "w_ffn1_down", last=True)
    dx0, small_grad["pre_norm_ffn1"] = _pre_bwd("pre_bwd1", dn1, x0, g["pre_norm_ffn1"], dx1, deps=dep)

    def lb_grad(dlb, lb):
        da0 = dlb * lb * (1.0 - lb)
        return jnp.concatenate([da0, -da0], axis=0)

    small_grad["hgrn_lower_bounds_fwd"] = lb_grad(dlb_f, lb_f)
    small_grad["hgrn_lower_bounds_bwd"] = lb_grad(dlb_b, lb_b)
    small_grad["attn_sink"] = dsink_rows[:, :, 0].reshape(1, n_attn)
    small_grad["rel_bias_table"] = jnp.transpose(_bias_reduce(dbias, bucket_idx)[:, :, 0])

    def adam(n, tag, dep, **rows):
        sums, recv = scattered[n + tag]
        return _adam_shard("adam_" + n + tag, local(wts[n], n), local(mom[n], n), local(var[n], n), sums, recv,
                           chip, deps=dep, **rows)

    big_out = {}
    dep = []
    for n in ("w_ffn2_down", "w_ffn2_gate_up", "w_mix_out", "w_mix_in", "w_ffn1_down"):
        big_out[n] = adam(n, "", dep)
        dep = [big_out[n][0]]
    first_half = adam("w_ffn1_gate_up", "_a", dep)

    g_s, d_s, m_s, v_s, loss_row = _small_step(
        "small_step", [small_grad[n] for n in SMALL] + [loss_part], [wts[n] for n in SMALL],
        [mom[n] for n in SMALL], [var[n] for n in SMALL], deps=[first_half[0]])
    loss = loss_row[0, 0]
    grads, delta, new_m, new_v = (dict(zip(SMALL, vals)) for vals in (g_s, d_s, m_s, v_s))

    big_out["w_ffn1_gate_up"] = adam("w_ffn1_gate_up", "_b", [loss_row], first_row=d // 2, earlier=first_half)
    for n in BIG:
        grads[n], delta[n], new_m[n], new_v[n] = [local(o[None], n)[None] for o in big_out[n]]

    return (loss, dx0[None], *[grads[n] for n in ORDER], *[delta[n] for n in ORDER],
            *[new_m[n] for n in ORDER], *[new_v[n] for n in ORDER])
```

```python
import functools
import math

import numpy as np
import jax
import jax.numpy as jnp
from jax import lax
from jax.experimental import pallas as pl
from jax.experimental.pallas import tpu as pltpu
from jax.experimental.pallas import tpu_sc as plsc

F32 = jnp.float32
BF16 = jnp.bfloat16
MESH = pl.DeviceIdType.MESH

N_DEV = 8
EPS = 1e-6
NEG_INF = -1e30
HEAD = 128
CHUNK = 64
WINDOW = 128
KEY_SPAN = 3 * WINDOW
KV_HEADS = 2
REL_BUCKETS = 32
REL_MAX_DIST = 128
ADAM_LR, ADAM_B1, ADAM_B2, ADAM_EPS, ADAM_WD, ADAM_STEP = 0.001, 0.9, 0.999, 1e-08, 0.01, 10
LANES = 128
VMEM_LIMIT = 56 * 1024 * 1024
ANY = pl.BlockSpec(memory_space=pl.ANY)


def _cparams(*sem):
    return pltpu.CompilerParams(dimension_semantics=sem if sem else None, vmem_limit_bytes=VMEM_LIMIT)


def _dot(a, b):
    return jnp.dot(a, b, preferred_element_type=F32)


def _dot_nt(a, b):
    return lax.dot_general(a, b, (((1,), (1,)), ((), ())), preferred_element_type=F32)


def _dot_tn(a, b):
    return lax.dot_general(a, b, (((0,), (0,)), ((), ())), preferred_element_type=F32)


def _tile(dim, target):
    for c in (target, 1024, 512, 256, 128):
        if c <= target and dim % c == 0:
            return c
    return dim


def _row_tile(rows, target):
    fits = [c for c in range(16, min(rows, target) + 1, 16) if rows % c == 0]
    return max(fits) if fits else rows


K_WHOLE = 2048
K_STEP = 2816


def _k_tile(kd):
    if kd <= K_WHOLE:
        return kd
    return max(c for c in range(LANES, K_STEP + 1, LANES) if kd % c == 0)


def _matmul(name, a, b, *, mode, out_dtype, stack=False, halves=False, tm=1024, tn=1024, deps=()):
    assert not (stack and mode == "nn")
    grp = 1
    if mode == "nn":
        m, kd = a.shape
        n = b.shape[1]
    elif mode == "nt":
        m = a.shape[-2]
        n, kd = (b.shape[1], b.shape[0] * b.shape[2]) if stack else b.shape
    else:
        kd, m = a.shape
        n = b.shape[-1] * (2 if halves else 1)
    if stack:
        n1 = b.shape[2] if mode == "nt" else n // N_DEV
        assert n1 % LANES == 0
        if mode == "nt":
            grp = 2 if 2 * n1 <= K_STEP else 1
            tk = grp * n1
        else:
            tn = n1
    per_half = N_DEV // 2 // grp
    tm = _tile(m, tm)
    if not (stack and mode == "tn"):
        tn = _tile(n, tn)
    if not (stack and mode == "nt"):
        tk = _k_tile(kd)
    nk = kd // tk
    lead = None if grp == 1 else grp
    b_outer = nk == 1 and b.size > a.size
    grid = (n // tn, m // tm, nk) if b_outer else (m // tm, n // tn, nk)

    def spec(shape, index):
        return pl.BlockSpec(shape, (lambda g0, g1, k: index(g1, g0, k)) if b_outer else index)

    if mode == "nn":
        a_spec = spec((tm, tk), lambda i, j, k: (i, k))
        b_spec = spec((tk, tn), lambda i, j, k: (k, j))
        dot = _dot
    elif mode == "nt":
        if halves:
            a_spec = spec((None, tm, tk), lambda i, j, k: (k // per_half, i, k % per_half))
        else:
            a_spec = spec((tm, tk), lambda i, j, k: (i, k))
        if stack:
            b_spec = spec((lead, tn, n1), lambda i, j, k: (k, j, 0))
        else:
            b_spec = spec((tn, tk), lambda i, j, k: (j, k))
        dot = _dot_nt
    else:
        a_spec = spec((tk, tm), lambda i, j, k: (k, i))
        if halves:
            b_spec = spec((None, tk, tn), lambda i, j, k: (j // per_half, k, j % per_half))
        else:
            b_spec = spec((tk, tn), lambda i, j, k: (k, j))
        dot = _dot_tn
    if stack and mode == "tn":
        out_shape = jax.ShapeDtypeStruct((N_DEV, m, n1), out_dtype)
        o_spec = spec((None, tm, n1), lambda i, j, k: (j, i, 0))
    else:
        out_shape = jax.ShapeDtypeStruct((m, n), out_dtype)
        o_spec = spec((tm, tn), lambda i, j, k: (i, j))

    def product(a_ref, b_ref):
        bmat = jnp.concatenate([b_ref[s] for s in range(grp)], axis=1) if grp > 1 else b_ref[...]
        return dot(a_ref[...], bmat)

    def store(o_ref, val):
        o_ref[...] = val.astype(o_ref.dtype)

    def body_whole(a_ref, b_ref, *rest):
        store(rest[-1], product(a_ref, b_ref))

    def body_steps(a_ref, b_ref, *rest):
        o_ref, acc_ref = rest[-2:]
        k = pl.program_id(2)

        @pl.when(k == 0)
        def _():
            acc_ref[...] = product(a_ref, b_ref)

        @pl.when(k > 0)
        def _():
            acc_ref[...] += product(a_ref, b_ref)

        @pl.when(k == nk - 1)
        def _():
            store(o_ref, acc_ref[...])

    return pl.pallas_call(
        body_whole if nk == 1 else body_steps, name=name, grid=grid,
        in_specs=[a_spec, b_spec] + [ANY] * len(deps), out_specs=o_spec, out_shape=out_shape,
        scratch_shapes=[] if nk == 1 else [pltpu.VMEM((tm, tn), F32)],
        compiler_params=_cparams("parallel", "parallel", "arbitrary"),
    )(a, b, *deps)


def _col_parts(width, parts=2):
    groups = width // LANES
    parts = max(1, min(parts, groups // 2))
    bounds = [LANES * (groups * p // parts) for p in range(parts)] + [width]
    return [slice(bounds[p], bounds[p + 1]) for p in range(parts)]


def _ffn_up(name, n, w_stack):
    t, d = n.shape
    s, _, n1 = w_stack.shape
    half = s // 2
    tm = _tile(t, 512)

    def body(n_ref, wg_ref, wu_ref, act_ref, gu_ref):
        nv = n_ref[...]
        for cols in _col_parts(n1):
            gate = _dot(nv, wg_ref[:, cols])
            up = _dot(nv, wu_ref[:, cols])
            sg = jax.nn.sigmoid(gate)
            silu = gate * sg
            act_ref[:, cols] = (silu * up).astype(BF16)
            gu_ref[0, :, cols] = (up * (sg * (1.0 + gate * (1.0 - sg)))).astype(BF16)
            gu_ref[1, :, cols] = silu.astype(BF16)

    return pl.pallas_call(
        body, name=name, grid=(half, t // tm),
        in_specs=[pl.BlockSpec((tm, d), lambda j, i: (i, 0)),
                  pl.BlockSpec((None, d, n1), lambda j, i: (j, 0, 0)),
                  pl.BlockSpec((None, d, n1), lambda j, i: (half + j, 0, 0))],
        out_specs=[pl.BlockSpec((tm, n1), lambda j, i: (i, j)), pl.BlockSpec((2, tm, n1), lambda j, i: (0, i, j))],
        out_shape=[jax.ShapeDtypeStruct((t, half * n1), BF16), jax.ShapeDtypeStruct((2, t, half * n1), BF16)],
        compiler_params=_cparams("parallel", "parallel"),
    )(n, w_stack, w_stack)


def _ffn_dact(name, dff, w_d, gu, deps=()):
    t, d = dff.shape
    f = w_d.shape[0]
    tm = _tile(t, 1024)
    tn = _tile(f, 1408)

    def body(dff_ref, w_ref, gu_ref, *rest):
        dgu_ref = rest[-1]
        da = _dot_nt(dff_ref[...], w_ref[...]).astype(BF16)
        dgu_ref[0] = da * gu_ref[0]
        dgu_ref[1] = da * gu_ref[1]

    pair = pl.BlockSpec((2, tm, tn), lambda j, i: (0, i, j))
    return pl.pallas_call(
        body, name=name, grid=(f // tn, t // tm),
        in_specs=[pl.BlockSpec((tm, d), lambda j, i: (i, 0)), pl.BlockSpec((tn, d), lambda j, i: (j, 0)), pair]
        + [ANY] * len(deps),
        out_specs=pair, out_shape=jax.ShapeDtypeStruct((2, t, f), BF16),
        compiler_params=_cparams("parallel", "parallel"),
    )(dff, w_d, gu, *deps)


ROWS = 256


def _rstd(xf):
    return lax.rsqrt(jnp.mean(xf * xf, axis=-1, keepdims=True) + EPS)


def _row_spec(t, d):
    return pl.BlockSpec((min(ROWS, t), d), lambda i: (i, 0))


def _vec_spec(d):
    return pl.BlockSpec((1, d), lambda i: (0, 0))


def _pre_norm(name, x, gain):
    t, d = x.shape

    def body(x_ref, g_ref, n_ref):
        xf = x_ref[...]
        n_ref[...] = (xf * _rstd(xf) * g_ref[...]).astype(BF16)

    return pl.pallas_call(
        body, name=name, grid=(t // min(ROWS, t),), in_specs=[_row_spec(t, d), _vec_spec(d)],
        out_specs=_row_spec(t, d), out_shape=jax.ShapeDtypeStruct((t, d), BF16),
        compiler_params=_cparams("parallel"),
    )(x, gain)


def _post_res_pre(name, x, ff, g_post, g_next, scale):
    t, d = x.shape

    def body(x_ref, ff_ref, gp_ref, gn_ref, xo_ref, n_ref):
        ff_ = ff_ref[...]
        xn = x_ref[...] + scale * (ff_ * _rstd(ff_) * gp_ref[...])
        xo_ref[...] = xn
        n_ref[...] = (xn * _rstd(xn) * gn_ref[...]).astype(BF16)

    return pl.pallas_call(
        body, name=name, grid=(t // min(ROWS, t),),
        in_specs=[_row_spec(t, d), _row_spec(t, d), _vec_spec(d), _vec_spec(d)],
        out_specs=[_row_spec(t, d), _row_spec(t, d)],
        out_shape=[jax.ShapeDtypeStruct((t, d), F32), jax.ShapeDtypeStruct((t, d), BF16)],
        compiler_params=_cparams("parallel"),
    )(x, ff, g_post, g_next)


def _norm_bwd(xf, gain, dy):
    r = _rstd(xf)
    xh = xf * r
    dxh = dy * gain
    return r * (dxh - xh * jnp.mean(dxh * xh, axis=-1, keepdims=True)), jnp.sum(dy * xh, axis=0, keepdims=True)


def _accumulate(ref, part):
    @pl.when(pl.program_id(0) == 0)
    def _():
        ref[...] = jnp.zeros_like(ref)

    ref[...] += jnp.broadcast_to(part, ref.shape)


def _post_res_loss(name, x, ff, g_post, target, scale):
    t, d = x.shape

    def body(x_ref, ff_ref, gp_ref, tg_ref, dy_ref, loss_ref, dff_ref, dg_ref):
        ff_ = ff_ref[...]
        err = x_ref[...] + scale * (ff_ * _rstd(ff_) * gp_ref[...]) - tg_ref[...]
        dy = err / d
        dy_ref[...] = dy
        _accumulate(loss_ref, 0.5 * jnp.sum(jnp.mean(err * err, axis=-1, keepdims=True), axis=0, keepdims=True))
        dff, dg = _norm_bwd(ff_, gp_ref[...], scale * dy)
        dff_ref[...] = dff.astype(BF16)
        _accumulate(dg_ref, dg)

    return pl.pallas_call(
        body, name=name, grid=(t // min(ROWS, t),),
        in_specs=[_row_spec(t, d), _row_spec(t, d), _vec_spec(d), _row_spec(t, d)],
        out_specs=[_row_spec(t, d), _vec_spec(LANES), _row_spec(t, d), _vec_spec(d)],
        out_shape=[jax.ShapeDtypeStruct((t, d), F32), jax.ShapeDtypeStruct((1, LANES), F32),
                   jax.ShapeDtypeStruct((t, d), BF16), jax.ShapeDtypeStruct((1, d), F32)],
        compiler_params=_cparams("arbitrary"),
    )(x, ff, g_post, target)


def _pre_bwd(name, dn, x, g_pre, dy, deps=(), post=None):
    t, d = x.shape
    n_post = 2 if post else 0
    scale = post[2] if post else None

    def body(dn_ref, x_ref, g_ref, dy_ref, *rest):
        outs = rest[len(rest) - 2 - n_post:]
        dnf = dn_ref[...].astype(F32)
        dpre, dg = _norm_bwd(x_ref[...], g_ref[...], dnf)
        dx = dy_ref[...] + dpre
        outs[0][...] = dx
        _accumulate(outs[1], dg)
        if post:
            ff_ref, gp_ref = rest[0], rest[1]
            dff, dgp = _norm_bwd(ff_ref[...], gp_ref[...], scale * dx)
            outs[2][...] = dff.astype(BF16)
            _accumulate(outs[3], dgp)

    extra_in = [_row_spec(t, d), _vec_spec(d)] if post else []
    extra_out = [_row_spec(t, d), _vec_spec(d)] if post else []
    extra_shape = [jax.ShapeDtypeStruct((t, d), BF16), jax.ShapeDtypeStruct((1, d), F32)] if post else []
    return pl.pallas_call(
        body, name=name, grid=(t // min(ROWS, t),),
        in_specs=[_row_spec(t, d), _row_spec(t, d), _vec_spec(d), _row_spec(t, d)] + extra_in + [ANY] * len(deps),
        out_specs=[_row_spec(t, d), _vec_spec(d)] + extra_out,
        out_shape=[jax.ShapeDtypeStruct((t, d), F32), jax.ShapeDtypeStruct((1, d), F32)] + extra_shape,
        compiler_params=_cparams("arbitrary"),
    )(dn, x, g_pre, dy, *(post[:2] if post else ()), *deps)


def _bdot(a, b, ca, cb, precision=None):
    return lax.dot_general(a, b, (((ca,), (cb,)), ((0,), (0,))), preferred_element_type=F32, precision=precision)


def _tri_masks(g):
    row = lax.broadcasted_iota(jnp.int32, (g, CHUNK, CHUNK), 1)
    col = lax.broadcasted_iota(jnp.int32, (g, CHUNK, CHUNK), 2)
    return col <= row, col >= row


def _ones_matmul(ones_mat, val):
    hi = val.astype(BF16)
    lo = (val - hi.astype(F32)).astype(BF16)
    return _bdot(ones_mat, hi, 2, 1) + _bdot(ones_mat, lo, 2, 1)


def _hgrn_block(z, lb, q, v, cum_mat):
    sg = jax.nn.sigmoid(z)
    f = lb + (1.0 - lb) * sg
    lf = jnp.log(f)
    k = 1.0 - f
    a = _ones_matmul(cum_mat, lf)
    last = jnp.sum(lf, axis=1, keepdims=True)
    e_a = jnp.exp(a)
    e_na = jnp.exp(-a)
    e_t = jnp.exp(last - a)
    return dict(sg=sg, f=f, k=k, decay=jnp.exp(last), e_a=e_a, e_na=e_na, e_t=e_t,
                qd=q * e_a, kd=k * e_na, kt=k * e_t)


def _hgrn_states(state, kv, decay, order):
    entering = [None] * len(order)
    for g in order:
        entering[g] = state
        state = decay[g] * state + kv[g]
    return jnp.stack(entering, axis=0), state


def _hgrn_fwd(p, lb_f, lb_b, gain, n_heads, width):
    t = p.shape[0]
    w = n_heads * HEAD
    blk = min(16, t // CHUNK)
    rows_blk = blk * CHUNK
    n_blocks = t // rows_blk
    fin_rows = min(256, t)

    def body(q_ref, i_ref, zf_ref, zb_ref, g_ref, lbf_ref, lbb_ref, gain_ref, y_ref, o_ref, st_ref):
        low, up = _tri_masks(blk)
        m_low, m_up = low.astype(BF16), up.astype(BF16)
        o_ref[...] = jnp.zeros_like(o_ref)
        st_ref[...] = jnp.zeros_like(st_ref)

        def one(r0, z_ref, lb, slot, rev):
            rows = pl.ds(r0, rows_blk)
            split = lambda ref: ref[rows, :].reshape(blk, CHUNK, HEAD)
            q, v = split(q_ref), split(i_ref)
            c = _hgrn_block(split(z_ref), lb, q, v, m_up if rev else m_low)
            qd, kd, kt, vb = c["qd"].astype(BF16), c["kd"].astype(BF16), c["kt"].astype(BF16), v.astype(BF16)
            pm = jnp.where(up if rev else low, _bdot(qd, kd, 2, 2), 0.0).astype(BF16)
            kv = _bdot(vb, kt, 1, 1)
            order = range(blk - 1, -1, -1) if rev else range(blk)
            entering, st_ref[slot] = _hgrn_states(st_ref[slot], kv, c["decay"], order)
            o = _bdot(pm, vb, 2, 1) + _bdot(qd, entering.astype(BF16), 2, 2)
            o_ref[rows, :] += o.reshape(rows_blk, HEAD)

        def step(n, carry):
            one(pl.multiple_of(n * rows_blk, rows_blk), zf_ref, lbf_ref[...], 0, False)
            one(pl.multiple_of((n_blocks - 1 - n) * rows_blk, rows_blk), zb_ref, lbb_ref[...], 1, True)
            return carry

        lax.fori_loop(0, n_blocks, step, 0)

        def fin(n, carry):
            rows = pl.ds(pl.multiple_of(n * fin_rows, fin_rows), fin_rows)
            o = o_ref[rows, :]
            g = g_ref[rows, :]
            y_ref[rows, :] = (o * _rstd(o) * gain_ref[...] * (g * jax.nn.sigmoid(g))).astype(BF16)
            return carry

        lax.fori_loop(0, t // fin_rows, fin, 0)

    col = lambda grp: pl.BlockSpec((t, HEAD), lambda h: (0, grp * n_heads + h))
    vec = pl.BlockSpec((1, HEAD), lambda h: (0, h))
    out = pl.BlockSpec((t, HEAD), lambda h: (0, h))
    return pl.pallas_call(
        body, name="hgrn_fwd", grid=(n_heads,),
        in_specs=[col(0), col(1), col(2), col(3), col(4), vec, vec, vec],
        out_specs=[out, out],
        out_shape=[jax.ShapeDtypeStruct((t, width), BF16), jax.ShapeDtypeStruct((t, w), F32)],
        scratch_shapes=[pltpu.VMEM((2, HEAD, HEAD), F32)],
        compiler_params=_cparams("parallel"),
    )(p, p, p, p, p, lb_f, lb_b, gain)


def _hgrn_bwd(p, o_raw, dcat, lb_f, lb_b, gain, n_heads):
    t = p.shape[0]
    w = n_heads * HEAD
    n_chunks = t // CHUNK
    blk = min(16, n_chunks)
    rows_blk = blk * CHUNK
    n_blocks = t // rows_blk
    rb = min(256, t)

    def body(q_ref, i_ref, zf_ref, zb_ref, g_ref, o_ref, dy_ref, lbf_ref, lbb_ref, gain_ref,
             dq_ref, di_ref, dzf_ref, dzb_ref, dg_ref, dlbf_ref, dlbb_ref, dgain_ref,
             do_s, dq_s, dv_s, st_s, cur_s):
        low, up = _tri_masks(blk)
        m_low, m_up = low.astype(BF16), up.astype(BF16)
        rowid = lax.broadcasted_iota(jnp.int32, (blk, CHUNK, HEAD), 1)
        gain_v = gain_ref[...]

        def norm_bwd(n, dgain):
            rows = pl.ds(pl.multiple_of(n * rb, rb), rb)
            o = o_ref[rows, :]
            g = g_ref[rows, :]
            dy = dy_ref[rows, :].astype(F32)
            r = _rstd(o)
            oh = o * r
            sg = jax.nn.sigmoid(g)
            dg_ref[rows, :] = (dy * oh * gain_v * (sg * (1.0 + g * (1.0 - sg)))).astype(BF16)
            dno = dy * (g * sg)
            dxh = dno * gain_v
            do_s[rows, :] = r * (dxh - oh * jnp.mean(dxh * oh, axis=-1, keepdims=True))
            return dgain + jnp.sum(dno * oh, axis=0, keepdims=True)

        dgain_ref[...] = lax.fori_loop(0, t // rb, norm_bwd, jnp.zeros((1, HEAD), F32))
        def direction(z_ref, lb_ref, dz_ref, dlb_ref, rev):
            way = int(rev)
            lb = lb_ref[...]
            cum_mat = m_up if rev else m_low
            cum_mat_t = m_low if rev else m_up
            mask = up if rev else low
            last_row = 0 if rev else CHUNK - 1

            order = range(blk - 1, -1, -1) if rev else range(blk)

            def rows_of(j):
                bidx = (n_blocks - 1 - j) if rev else j
                return bidx, pl.ds(pl.multiple_of(bidx * rows_blk, rows_blk), rows_blk)

            def load(rows):
                split = lambda ref: ref[rows, :].reshape(blk, CHUNK, HEAD)
                q, v = split(q_ref), split(i_ref)
                return q, v, _hgrn_block(split(z_ref), lb, q, v, cum_mat)

            def sweep_fwd(j):
                bidx, rows = rows_of(j)
                _, v, c = load(rows)
                kv = _bdot(v.astype(BF16), c["kt"].astype(BF16), 1, 1)
                st_s[way, pl.ds(bidx * blk, blk)], cur_s[2 * way] = _hgrn_states(
                    cur_s[2 * way], kv, c["decay"], order)

            dlb_ref[...] = jnp.zeros_like(dlb_ref)

            def sweep_bwd(jj):
                bidx, rows = rows_of(n_blocks - 1 - jj)
                _, v, c = load(rows)
                st = st_s[way, pl.ds(bidx * blk, blk)]
                do = do_s[rows, :].reshape(blk, CHUNK, HEAD)
                qd, kd, kt, decay = c["qd"], c["kd"], c["kt"], c["decay"]
                qd_b, kd_b, kt_b = qd.astype(BF16), kd.astype(BF16), kt.astype(BF16)
                v_b, do_b, st_b = v.astype(BF16), do.astype(BF16), st.astype(BF16)
                pm = jnp.where(mask, _bdot(qd_b, kd_b, 2, 2), 0.0).astype(BF16)
                dpm = jnp.where(mask, _bdot(do_b, v_b, 2, 2), 0.0).astype(BF16)
                gq = _bdot(do_b, qd_b, 1, 1)
                dstate = cur_s[2 * way + 1]
                after = [None] * blk
                for g in reversed(order):
                    after[g] = dstate
                    dstate = gq[g] + decay[g] * dstate
                cur_s[2 * way + 1] = dstate
                dst = jnp.stack(after, axis=0)
                dst_b = dst.astype(BF16)
                dv = _bdot(pm, do_b, 1, 1) + _bdot(kt_b, dst_b, 2, 2)
                dqd = _bdot(dpm, kd_b, 2, 1) + _bdot(do_b, st_b, 2, 1)
                dkd = _bdot(dpm, qd_b, 1, 1)
                dkt = _bdot(v_b, dst_b, 2, 1)
                dlast = (jnp.sum(dkt * kt, axis=1, keepdims=True)
                         + decay * jnp.sum(dst * st, axis=1, keepdims=True))
                dq_s[way, rows, :] = (dqd * c["e_a"]).reshape(rows_blk, HEAD)
                dv_s[way, rows, :] = dv.reshape(rows_blk, HEAD)
                dk = dkd * c["e_na"] + dkt * c["e_t"]
                da = dqd * qd - dkd * kd - dkt * kt
                da = da + jnp.where(rowid == last_row, dlast, 0.0)
                dlf = _ones_matmul(cum_mat_t, da)
                df = dlf / c["f"] - dk
                sg = c["sg"]
                dz_ref[rows, :] = (df * (1.0 - lb) * (sg * (1.0 - sg))).reshape(rows_blk, HEAD).astype(BF16)
                dlb_ref[...] += jnp.sum((df * (1.0 - sg)).reshape(rows_blk, HEAD), axis=0, keepdims=True)

            return sweep_fwd, sweep_bwd

        ways = [direction(zf_ref, lbf_ref, dzf_ref, dlbf_ref, False),
                direction(zb_ref, lbb_ref, dzb_ref, dlbb_ref, True)]
        cur_s[...] = jnp.zeros_like(cur_s)
        for sweep in range(2):
            def both(j, carry):
                for way in ways:
                    way[sweep](j)
                return carry

            lax.fori_loop(0, n_blocks, both, 0)
        dq_ref[...] = (dq_s[0] + dq_s[1]).astype(BF16)
        di_ref[...] = (dv_s[0] + dv_s[1]).astype(BF16)

    col = lambda grp: pl.BlockSpec((t, HEAD), lambda h: (0, grp * n_heads + h))
    one = pl.BlockSpec((t, HEAD), lambda h: (0, h))
    vec = pl.BlockSpec((1, HEAD), lambda h: (0, h))
    big = jax.ShapeDtypeStruct((t, w), BF16)
    small = jax.ShapeDtypeStruct((1, w), F32)
    return pl.pallas_call(
        body, name="hgrn_bwd", grid=(n_heads,),
        in_specs=[col(0), col(1), col(2), col(3), col(4), one, one, vec, vec, vec],
        out_specs=[one] * 5 + [vec] * 3,
        out_shape=[big] * 5 + [small] * 3,
        scratch_shapes=[pltpu.VMEM((t, HEAD), F32), pltpu.VMEM((2, t, HEAD), F32), pltpu.VMEM((2, t, HEAD), F32),
                        pltpu.VMEM((2, n_chunks, HEAD, HEAD), F32), pltpu.VMEM((4, HEAD, HEAD), F32)],
        compiler_params=_cparams("parallel"),
    )(p, p, p, p, p, o_raw, dcat, lb_f, lb_b, gain)


def _t5_bucket_index():
    c = np.arange(WINDOW)[:, None]
    s = np.arange(KEY_SPAN)[None, :]
    rel = s - WINDOW - c
    nb = REL_BUCKETS // 2
    max_exact = nb // 2
    bucket = (rel > 0).astype(np.int32) * nb
    n = np.abs(rel)
    large = max_exact + (np.log(np.maximum(n, 1) / max_exact) / np.log(REL_MAX_DIST / max_exact)
                         * (nb - max_exact)).astype(np.int32)
    large = np.minimum(large, nb - 1)
    return bucket + np.where(n < max_exact, n, large).astype(np.int32)


def _bias_build(table, idx):
    n_attn = table.shape[1]

    def body(tab_ref, idx_ref, o_ref):
        h = pl.program_id(0)
        idx_v = idx_ref[...]
        acc = jnp.zeros((WINDOW, KEY_SPAN), F32)
        for b in range(REL_BUCKETS):
            acc = jnp.where(idx_v == b, tab_ref[b, h], acc)
        o_ref[...] = acc

    return pl.pallas_call(
        body, name="bias_build", grid=(n_attn,),
        in_specs=[pl.BlockSpec(memory_space=pltpu.SMEM), pl.BlockSpec((WINDOW, KEY_SPAN), lambda h: (0, 0))],
        out_specs=pl.BlockSpec((None, WINDOW, KEY_SPAN), lambda h: (h, 0, 0)),
        out_shape=jax.ShapeDtypeStruct((n_attn, WINDOW, KEY_SPAN), F32), compiler_params=_cparams("parallel"),
    )(table, idx)


def _bias_reduce(dbias, idx):
    n_attn = dbias.shape[0]

    def body(idx_ref, d_ref, o_ref):
        idx_v = idx_ref[...]
        dv = d_ref[...]
        rows = lax.broadcasted_iota(jnp.int32, (REL_BUCKETS, LANES), 0)
        acc = jnp.zeros((REL_BUCKETS, LANES), F32)
        for b in range(REL_BUCKETS):
            part = jnp.sum(jnp.where(idx_v == b, dv, 0.0), axis=1, keepdims=True)
            acc = jnp.where(rows == b, jnp.sum(part, axis=0, keepdims=True), acc)
        o_ref[...] = acc

    return pl.pallas_call(
        body, name="bias_reduce", grid=(n_attn,),
        in_specs=[pl.BlockSpec((WINDOW, KEY_SPAN), lambda h: (0, 0)),
                  pl.BlockSpec((None, WINDOW, KEY_SPAN), lambda h: (h, 0, 0))],
        out_specs=pl.BlockSpec((None, REL_BUCKETS, LANES), lambda h: (h, 0, 0)),
        out_shape=jax.ShapeDtypeStruct((n_attn, REL_BUCKETS, LANES), F32), compiler_params=_cparams("parallel"),
    )(idx, dbias)


def _attn_probs(q, kb, bias, sink, valid):
    s = _dot_nt(q, kb) / math.sqrt(HEAD) + bias
    s = jnp.where(valid, s, NEG_INF)
    m = jnp.maximum(jnp.max(s, axis=-1, keepdims=True), sink)
    e = jnp.exp(s - m)
    e_sink = jnp.exp(sink - m)
    den = jnp.sum(e, axis=-1, keepdims=True) + e_sink
    return e / den, e_sink / den


def _attn_valid(n, t, grp):
    c = lax.broadcasted_iota(jnp.int32, (grp * WINDOW, KEY_SPAN), 0) & (WINDOW - 1)
    s = lax.broadcasted_iota(jnp.int32, (grp * WINDOW, KEY_SPAN), 1)
    rel = s - WINDOW - c
    key_pos = n * WINDOW - WINDOW + s
    return (jnp.abs(rel) <= WINDOW) & (key_pos >= 0) & (key_pos < t)


def _stack_heads(ref, grp):
    return jnp.concatenate([ref[:, g * HEAD:(g + 1) * HEAD] for g in range(grp)], axis=0).astype(BF16)


def _sink_column(sink_ref, x, grp):
    return jnp.concatenate([jnp.full((WINDOW, 1), sink_ref[0, x * grp + g], F32) for g in range(grp)], axis=0)


def _attn_specs(t, n_hgrn, n_attn):
    grp = n_attn // KV_HEADS
    nb = t // WINDOW
    cq = 5 * n_hgrn
    ck = cq + n_attn
    cv = ck + KV_HEADS
    q_spec = pl.BlockSpec((WINDOW, grp * HEAD), lambda x, n: (n, cq // grp + x))
    kv = lambda base, off: pl.BlockSpec(
        (WINDOW, HEAD), lambda x, n: (jnp.clip(n + off, 0, nb - 1), base + x))
    band = [kv(ck, -1), kv(ck, 0), kv(ck, 1), kv(cv, -1), kv(cv, 0), kv(cv, 1)]
    bias_spec = pl.BlockSpec((grp, WINDOW, KEY_SPAN), lambda x, n: (x, 0, 0))
    sink_spec = pl.BlockSpec(memory_space=pltpu.SMEM)
    return grp, nb, q_spec, band, bias_spec, sink_spec


def _attn_fwd(p, bias, sink, cat, n_hgrn, n_attn):
    t = p.shape[0]
    grp, nb, q_spec, band, bias_spec, sink_spec = _attn_specs(t, n_hgrn, n_attn)

    def body(q_ref, kp, kc, kn, vp, vc, vn, bias_ref, sink_ref, cat_ref, y_ref):
        x, n = pl.program_id(0), pl.program_id(1)
        kb = jnp.concatenate([kp[...], kc[...], kn[...]], axis=0).astype(BF16)
        vb = jnp.concatenate([vp[...], vc[...], vn[...]], axis=0).astype(BF16)
        pr, _ = _attn_probs(_stack_heads(q_ref, grp), kb, bias_ref[...].reshape(grp * WINDOW, KEY_SPAN),
                            _sink_column(sink_ref, x, grp), _attn_valid(n, t, grp))
        y = _dot(pr.astype(BF16), vb).astype(BF16)
        for g in range(grp):
            y_ref[:, g * HEAD:(g + 1) * HEAD] = y[g * WINDOW:(g + 1) * WINDOW]

    return pl.pallas_call(
        body, name="attn_fwd", grid=(KV_HEADS, nb),
        in_specs=[q_spec] + band + [bias_spec, sink_spec, ANY],
        out_specs=pl.BlockSpec((WINDOW, grp * HEAD), lambda x, n: (n, n_hgrn // grp + x)),
        out_shape=jax.ShapeDtypeStruct(cat.shape, BF16), input_output_aliases={9: 0},
        compiler_params=_cparams("parallel", "parallel"),
    )(p, p, p, p, p, p, p, bias, sink, cat)


def _attn_bwd(p, dcat, bias, sink, n_hgrn, n_attn, deps=()):
    t = p.shape[0]
    grp, nb, q_spec, band, bias_spec, sink_spec = _attn_specs(t, n_hgrn, n_attn)
    inv = 1.0 / math.sqrt(HEAD)

    def body(q_ref, kp, kc, kn, vp, vc, vn, bias_ref, sink_ref, do_ref, *rest):
        dq_ref, dk_ref, dv_ref, dbias_ref, dsink_ref, dk_s, dv_s = rest[-7:]
        x, n = pl.program_id(0), pl.program_id(1)

        @pl.when(n == 0)
        def _():
            dk_s[...] = jnp.zeros_like(dk_s)
            dv_s[...] = jnp.zeros_like(dv_s)
            dbias_ref[...] = jnp.zeros_like(dbias_ref)
            dsink_ref[...] = jnp.zeros_like(dsink_ref)

        kb = jnp.concatenate([kp[...], kc[...], kn[...]], axis=0).astype(BF16)
        vb = jnp.concatenate([vp[...], vc[...], vn[...]], axis=0).astype(BF16)
        q = _stack_heads(q_ref, grp)
        do = _stack_heads(do_ref, grp)
        pr, p_sink = _attn_probs(q, kb, bias_ref[...].reshape(grp * WINDOW, KEY_SPAN),
                                 _sink_column(sink_ref, x, grp), _attn_valid(n, t, grp))
        dpr = _dot_nt(do, vb)
        delta = jnp.sum(pr * dpr, axis=-1, keepdims=True)
        ds = pr * (dpr - delta)
        ds_b = ds.astype(BF16)
        dq = (_dot(ds_b, kb) * inv).astype(BF16)
        dsink = -p_sink * delta
        for g in range(grp):
            head = slice(g * WINDOW, (g + 1) * WINDOW)
            dq_ref[:, g * HEAD:(g + 1) * HEAD] = dq[head]
            dbias_ref[g] += ds[head]
            dsink_ref[g:g + 1, :] += jnp.broadcast_to(jnp.sum(dsink[head], axis=0, keepdims=True), (1, WINDOW))
        rows = pl.ds(pl.multiple_of(n * WINDOW, WINDOW), KEY_SPAN)
        dk_s[rows, :] += _dot_tn(ds_b, q) * inv
        dv_s[rows, :] += _dot_tn(pr.astype(BF16), do)

        @pl.when(n == nb - 1)
        def _():
            dk_ref[...] = dk_s[pl.ds(WINDOW, t), :].astype(BF16)
            dv_ref[...] = dv_s[pl.ds(WINDOW, t), :].astype(BF16)

    do_spec = pl.BlockSpec((WINDOW, grp * HEAD), lambda x, n: (n, n_hgrn // grp + x))
    kv_out = pl.BlockSpec((t, HEAD), lambda x, n: (0, x))
    return pl.pallas_call(
        body, name="attn_bwd", grid=(KV_HEADS, nb),
        in_specs=[q_spec] + band + [bias_spec, sink_spec, do_spec] + [ANY] * len(deps),
        out_specs=[pl.BlockSpec((WINDOW, grp * HEAD), lambda x, n: (n, x)), kv_out, kv_out,
                   bias_spec, pl.BlockSpec((None, grp, WINDOW), lambda x, n: (x, 0, 0))],
        out_shape=[jax.ShapeDtypeStruct((t, n_attn * HEAD), BF16),
                   jax.ShapeDtypeStruct((t, KV_HEADS * HEAD), BF16),
                   jax.ShapeDtypeStruct((t, KV_HEADS * HEAD), BF16),
                   jax.ShapeDtypeStruct((n_attn, WINDOW, KEY_SPAN), F32),
                   jax.ShapeDtypeStruct((KV_HEADS, grp, WINDOW), F32)],
        scratch_shapes=[pltpu.VMEM((t + 2 * WINDOW, HEAD), F32), pltpu.VMEM((t + 2 * WINDOW, HEAD), F32)],
        compiler_params=_cparams("parallel", "arbitrary"),
    )(p, p, p, p, p, p, p, bias, sink, dcat, *deps)


def _position():
    return lax.axis_index("x"), lax.axis_index("y"), lax.axis_index("c")


def _handshake(peers):
    barrier = pltpu.get_barrier_semaphore()
    for peer in peers:
        pl.semaphore_signal(barrier, inc=1, device_id=peer, device_id_type=MESH)
    pl.semaphore_wait(barrier, len(peers))


def _sequencer(name, collective_id, scratch_types):
    return functools.partial(
        pl.kernel, mesh=plsc.ScalarSubcoreMesh(axis_name="sc", num_cores=1), name=name,
        scratch_types=scratch_types, compiler_params=pltpu.CompilerParams(collective_id=collective_id))


def _all_gather(name, shard, collective_id):
    rows = shard.shape[0]
    assert rows % 2 == 0
    rh = rows // 2
    src = jax.new_ref(shard, memory_space=pltpu.MemorySpace.HBM)
    out = jax.empty_ref(jax.ShapeDtypeStruct((N_DEV,) + shard.shape, shard.dtype),
                        memory_space=pltpu.MemorySpace.HBM)
    n_copies = 11

    @_sequencer(name, collective_id, (pltpu.SemaphoreType.DMA((n_copies,)), pltpu.SemaphoreType.DMA((n_copies,)),
                                      pltpu.SemaphoreType.DMA))
    def launch(send_sems, recv_sems, local_sem):
        x, y, c = _position()
        sibling = (x, y, 1 - c)
        xn, yn, dg = (1 - x, y), (x, 1 - y), (1 - x, 1 - y)
        _handshake([sibling, (*xn, c), (*yn, c)])

        def part(ref, half):
            return ref if half is None else ref.at[pl.ds(half * rh, rh)]

        def slot(chip, core, half=None):
            return part(out.at[4 * chip[0] + 2 * chip[1] + core], half)

        def copy(k, chip, core, half, to, own=False):
            return pltpu.make_async_remote_copy(
                src_ref=part(src, half) if own else slot(chip, core, half), dst_ref=slot(chip, core, half),
                send_sem=send_sems.at[k], recv_sem=recv_sems.at[k], device_id=to, device_id_type=MESH)

        def landed(k, chip, core, half):
            copy(k, chip, core, half, (x, y, c)).wait_recv()

        mine = pltpu.make_async_copy(src, slot((x, y), c), local_sem)
        mine.start()
        sent = [copy(0, (x, y), c, None, sibling, own=True),
                copy(1, (x, y), c, 0, (*xn, c), own=True), copy(3, (x, y), c, 1, (*yn, c), own=True),
                copy(2, (x, y), c, 1, (*xn, c), own=True), copy(4, (x, y), c, 0, (*yn, c), own=True)]
        for cp in sent:
            cp.start()

        def then(cp):
            cp.start()
            sent.append(cp)

        landed(1, xn, c, 0)
        then(copy(5, xn, c, 0, (*yn, c)))
        landed(3, yn, c, 1)
        then(copy(6, yn, c, 1, (*xn, c)))
        landed(2, xn, c, 1)
        then(copy(7, xn, c, None, sibling))
        landed(4, yn, c, 0)
        then(copy(8, yn, c, None, sibling))
        landed(5, dg, c, 0)
        then(copy(9, dg, c, 0, sibling))
        landed(6, dg, c, 1)
        then(copy(10, dg, c, 1, sibling))
        landed(0, (x, y), 1 - c, None)
        landed(7, xn, 1 - c, None)
        landed(8, yn, 1 - c, None)
        landed(9, dg, 1 - c, 0)
        landed(10, dg, 1 - c, 1)
        for cp in sent:
            cp.wait_send()
        mine.wait()

    launch()
    return out[...]


HBM = pl.BlockSpec(memory_space=pltpu.HBM)
SEM = pl.BlockSpec(memory_space=pltpu.SEMAPHORE)
EFFECT = pltpu.SideEffectType.DATAFLOW_SIDE_EFFECTING


def _pair_copies(s_ref, land_ref, send_sems, recv_sems):
    x, y, c = _position()
    return [pltpu.make_async_remote_copy(
        src_ref=s_ref.at[2 * k + (1 - c)], dst_ref=land_ref.at[k], send_sem=send_sems.at[k],
        recv_sem=recv_sems.at[k], device_id=(x, y, 1 - c), device_id_type=MESH) for k in range(4)]


def _pair_start(name, stack):
    land_shape = (4,) + stack.shape[1:]

    def body(s_ref, land_ref, send_sems, recv_sems, s_thru, land_thru, token):
        for cp in _pair_copies(s_ref, land_ref, send_sems, recv_sems):
            cp.start()
        token[...] = jnp.zeros_like(token)

    return pl.pallas_call(
        body, name=name,
        out_shape=(pltpu.SemaphoreType.DMA((4,)), pltpu.SemaphoreType.DMA((4,)),
                   pltpu.HBM(stack.shape, stack.dtype), pltpu.HBM(land_shape, stack.dtype),
                   jax.ShapeDtypeStruct((8, LANES), F32)),
        in_specs=(HBM, HBM), out_specs=(SEM, SEM, HBM, HBM, pl.BlockSpec(memory_space=pltpu.VMEM)),
        input_output_aliases={0: 2, 1: 3}, compiler_params=pltpu.CompilerParams(has_side_effects=EFFECT),
    )(pltpu.with_memory_space_constraint(stack, pltpu.HBM),
      pltpu.with_memory_space_constraint(lax.empty(land_shape, stack.dtype), pltpu.HBM))


def _pair_wait(name, started, after):
    send_sems, recv_sems, s_thru, land_thru, _ = started

    def body(s_ref, land_ref, send_sems, recv_sems, after_ref, s_out, land_out):
        for cp in _pair_copies(s_ref, land_ref, send_sems, recv_sems):
            cp.wait_send()
            cp.wait_recv()

    return pl.pallas_call(
        body, name=name,
        out_shape=(pltpu.HBM(s_thru.shape, s_thru.dtype), pltpu.HBM(land_thru.shape, land_thru.dtype)),
        in_specs=(HBM, HBM, SEM, SEM, ANY), out_specs=(HBM, HBM), input_output_aliases={0: 0, 1: 1},
        compiler_params=pltpu.CompilerParams(has_side_effects=EFFECT),
    )(s_thru, land_thru, send_sems, recv_sems, after)


def _pair_sum(name, stack, other, core):
    _, r, c = stack.shape
    tr = _row_tile(r, 1024)

    def body(core_ref, a_ref, b_ref, o_ref):
        o_ref[...] = (a_ref[...].astype(F32) + b_ref[...].astype(F32)).astype(o_ref.dtype)

    grid_spec = pltpu.PrefetchScalarGridSpec(
        num_scalar_prefetch=1, grid=(4, r // tr),
        in_specs=[pl.BlockSpec((None, tr, c), lambda k, i, core_ref: (2 * k + core_ref[0], i, 0)),
                  pl.BlockSpec((None, tr, c), lambda k, i, core_ref: (k, i, 0))],
        out_specs=pl.BlockSpec((None, tr, c), lambda k, i, core_ref: (k, i, 0)))
    return pl.pallas_call(
        body, name=name, grid_spec=grid_spec, out_shape=jax.ShapeDtypeStruct((4, r, c), stack.dtype),
        compiler_params=_cparams("parallel", "parallel"),
    )(core, stack, other)


def _chip_exchange(name, sums, collective_id):
    src = jax.new_ref(sums, memory_space=pltpu.MemorySpace.HBM)
    out = jax.empty_ref(jax.ShapeDtypeStruct((3,) + sums.shape[1:], sums.dtype),
                        memory_space=pltpu.MemorySpace.HBM)

    @_sequencer(name, collective_id, (pltpu.SemaphoreType.DMA((3,)), pltpu.SemaphoreType.DMA((3,))))
    def launch(send_sems, recv_sems):
        x, y, c = _position()
        chips = [(1 - x, y), (x, 1 - y), (1 - x, 1 - y)]
        _handshake([(*chip, c) for chip in chips])
        copies = [pltpu.make_async_remote_copy(
            src_ref=src.at[2 * px + py], dst_ref=out.at[j], send_sem=send_sems.at[j],
            recv_sem=recv_sems.at[j], device_id=(px, py, c), device_id_type=MESH)
            for j, (px, py) in enumerate(chips)]
        for cp in copies:
            cp.start()
        for cp in copies:
            cp.wait()

    launch()
    return out[...]


def _small_rows(shapes):
    first, row = [], 0
    for r, c in shapes:
        first.append(row)
        row += r * (c // LANES) if c % LANES == 0 else r
        row = -(-row // 8) * 8
    return first, row


def _small_move(packed, row, ref, to_packed):
    r, c = ref.shape
    if c % LANES:
        if to_packed:
            packed[row:row + r, 0:c] = ref[...]
        else:
            ref[...] = packed[row:row + r, 0:c]
        return
    per = c // LANES
    for i in range(r):
        for j in range(per):
            at = row + i * per + j
            if to_packed:
                packed[at:at + 1, :] = ref[i:i + 1, j * LANES:(j + 1) * LANES]
            else:
                ref[i:i + 1, j * LANES:(j + 1) * LANES] = packed[at:at + 1, :]


def _small_step(name, parts, ws, ms, vs, deps=()):
    n_par = len(ws)
    first, rows = _small_rows([p.shape for p in parts])
    vm = pl.BlockSpec(memory_space=pltpu.VMEM)
    buf = pltpu.VMEM((rows, LANES), F32)

    def reduce_body(*refs):
        part_refs, (sum_out, mine, gather, send_sems, recv_sems) = refs[:n_par + 1], refs[-5:]
        x, y, c = _position()
        me = 4 * x + 2 * y + c
        mine[...] = jnp.zeros_like(mine)
        for k, ref in enumerate(part_refs):
            _small_move(mine, first[k], ref, True)
        gather[me] = mine[...]
        copies = []
        for k in range(1, N_DEV):
            peer = (x ^ (k >> 2), y ^ ((k >> 1) & 1), c ^ (k & 1))
            copies.append(pltpu.make_async_remote_copy(
                src_ref=mine, dst_ref=gather.at[me], send_sem=send_sems.at[k - 1],
                recv_sem=recv_sems.at[k - 1], device_id=peer, device_id_type=MESH))
        for cp in copies:
            cp.start()
        for k in range(1, N_DEV):
            peer_slot = 4 * (x ^ (k >> 2)) + 2 * (y ^ ((k >> 1) & 1)) + (c ^ (k & 1))
            pltpu.make_async_remote_copy(
                src_ref=mine, dst_ref=gather.at[peer_slot], send_sem=send_sems.at[k - 1],
                recv_sem=recv_sems.at[k - 1], device_id=(x, y, c), device_id_type=MESH).wait()
        acc = gather[0]
        for j in range(1, N_DEV):
            acc = acc + gather[j]
        sum_out[...] = acc

    summed = pl.pallas_call(
        reduce_body, name=name + "_reduce", in_specs=[vm] * (n_par + 1) + [ANY] * len(deps), out_specs=vm,
        out_shape=jax.ShapeDtypeStruct((rows, LANES), F32),
        scratch_shapes=[buf, pltpu.VMEM((N_DEV, rows, LANES), F32), pltpu.SemaphoreType.DMA((7,)),
                        pltpu.SemaphoreType.DMA((7,))],
    )(*parts, *deps)

    def adam_body(*refs):
        sum_ref, refs = refs[0], refs[1:]
        w_refs, m_refs, v_refs, refs = refs[:n_par], refs[n_par:2 * n_par], refs[2 * n_par:3 * n_par], refs[3 * n_par:]
        g_out, d_out, m_out, v_out = (refs[i * n_par:(i + 1) * n_par] for i in range(4))
        loss_out = refs[4 * n_par]
        w_p, m_p, v_p, d_p = refs[4 * n_par + 1:]
        for packed in (w_p, m_p, v_p):
            packed[...] = jnp.zeros_like(packed)
        for k in range(n_par):
            for packed, src in ((w_p, w_refs[k]), (m_p, m_refs[k]), (v_p, v_refs[k])):
                _small_move(packed, first[k], src, True)
        delta, m_new, v_new = _adam_math(w_p[...], sum_ref[...], m_p[...], v_p[...])
        d_p[...] = delta
        m_p[...] = m_new
        v_p[...] = v_new
        for k in range(n_par):
            for packed, dst in ((sum_ref, g_out[k]), (d_p, d_out[k]), (m_p, m_out[k]), (v_p, v_out[k])):
                _small_move(packed, first[k], dst, False)
        _small_move(sum_ref, first[n_par], loss_out, False)

    like = [jax.ShapeDtypeStruct(w.shape, F32) for w in ws]
    outs = pl.pallas_call(
        adam_body, name=name + "_adam", in_specs=[vm] * (3 * n_par + 1), out_specs=[vm] * (4 * n_par + 1),
        out_shape=like * 4 + [jax.ShapeDtypeStruct((1, LANES), F32)], scratch_shapes=[buf, buf, buf, buf],
    )(summed, *ws, *ms, *vs)
    return (outs[:n_par], outs[n_par:2 * n_par], outs[2 * n_par:3 * n_par], outs[3 * n_par:4 * n_par],
            outs[4 * n_par])


def _adam_math(w, g, m, v):
    m = ADAM_B1 * m + (1.0 - ADAM_B1) * g
    v = ADAM_B2 * v + (1.0 - ADAM_B2) * jnp.square(g)
    m_hat = m / (1.0 - ADAM_B1 ** ADAM_STEP)
    v_hat = v / (1.0 - ADAM_B2 ** ADAM_STEP)
    delta = -ADAM_LR * (m_hat / (jnp.sqrt(v_hat) + ADAM_EPS) + ADAM_WD * w)
    return delta, m, v


def _adam_shard(name, w, m, v, sums, recv, chip, deps=(), first_row=0, earlier=()):
    r, c = w.shape
    rows = sums.shape[1]
    tr = _row_tile(rows, 256)
    assert first_row % tr == 0
    skip = first_row // tr

    def body(chip_ref, w_ref, m_ref, v_ref, own_ref, r0_ref, r1_ref, r2_ref, *rest):
        g_out, d_out, m_out, v_out = rest[-4:]
        g = ((own_ref[...].astype(F32) + r0_ref[...].astype(F32)) + r1_ref[...].astype(F32)) + r2_ref[...].astype(F32)
        delta, m_new, v_new = _adam_math(w_ref[...], g, m_ref[...], v_ref[...])
        g_out[...] = g
        d_out[...] = delta
        m_out[...] = m_new
        v_out[...] = v_new

    plain = pl.BlockSpec((tr, c), lambda i, chip_ref: (skip + i, 0))
    piece = lambda j: pl.BlockSpec((None, tr, c), lambda i, chip_ref: (j, i, 0))
    grid_spec = pltpu.PrefetchScalarGridSpec(
        num_scalar_prefetch=1, grid=(rows // tr,),
        in_specs=[plain, plain, plain,
                  pl.BlockSpec((None, tr, c), lambda i, chip_ref: (chip_ref[0], i, 0)),
                  piece(0), piece(1), piece(2)] + [ANY] * (len(earlier) + len(deps)),
        out_specs=[plain] * 4)
    shape = jax.ShapeDtypeStruct((r, c), F32)
    return pl.pallas_call(
        body, name=name, grid_spec=grid_spec, out_shape=[shape] * 4, compiler_params=_cparams("parallel"),
        input_output_aliases={8 + k: k for k in range(len(earlier))},
    )(chip, w, m, v, sums, recv, recv, recv, *earlier, *deps)


def _reduce_scatter(tag, started, after, core, collective_id):
    grad_stack, other = _pair_wait("rs_pair_wait_" + tag, started, after)
    sums = _pair_sum("rs_sum_" + tag, grad_stack, other, core)
    return sums, _chip_exchange("rs_chip_" + tag, sums, collective_id)


SMALL = ("pre_norm_ffn1", "post_norm_ffn1", "pre_norm_mix", "post_norm_mix", "hgrn_lower_bounds_fwd",
         "hgrn_lower_bounds_bwd", "hgrn_out_norm", "attn_sink", "pre_norm_ffn2", "post_norm_ffn2", "rel_bias_table")
BIG = ("w_ffn1_gate_up", "w_ffn1_down", "w_mix_in", "w_mix_out", "w_ffn2_gate_up", "w_ffn2_down")
AG_ID = {n: 1 + i for i, n in enumerate(BIG)}
RS_ID = {n: 7 + i for i, n in enumerate(BIG)}
RS_ID.update(w_ffn1_gate_up_a=RS_ID["w_ffn1_gate_up"], w_ffn1_gate_up_b=13)
ORDER = ("pre_norm_ffn1", "post_norm_ffn1", "w_ffn1_gate_up", "w_ffn1_down", "pre_norm_mix", "post_norm_mix",
         "w_mix_in", "hgrn_lower_bounds_fwd", "hgrn_lower_bounds_bwd", "hgrn_out_norm", "attn_sink", "w_mix_out",
         "pre_norm_ffn2", "post_norm_ffn2", "w_ffn2_gate_up", "w_ffn2_down", "rel_bias_table")


def kernel(x, pre_norm_ffn1, post_norm_ffn1, w_ffn1_gate_up, w_ffn1_down, pre_norm_mix, post_norm_mix, w_mix_in, hgrn_lower_bounds_fwd, hgrn_lower_bounds_bwd, hgrn_out_norm, attn_sink, w_mix_out, pre_norm_ffn2, post_norm_ffn2, w_ffn2_gate_up, w_ffn2_down, rel_bias_table, loss_target, m_pre_norm_ffn1, m_post_norm_ffn1, m_w_ffn1_gate_up, m_w_ffn1_down, m_pre_norm_mix, m_post_norm_mix, m_w_mix_in, m_hgrn_lower_bounds_fwd, m_hgrn_lower_bounds_bwd, m_hgrn_out_norm, m_attn_sink, m_w_mix_out, m_pre_norm_ffn2, m_post_norm_ffn2, m_w_ffn2_gate_up, m_w_ffn2_down, m_rel_bias_table, v_pre_norm_ffn1, v_post_norm_ffn1, v_w_ffn1_gate_up, v_w_ffn1_down, v_pre_norm_mix, v_post_norm_mix, v_w_mix_in, v_hgrn_lower_bounds_fwd, v_hgrn_lower_bounds_bwd, v_hgrn_out_norm, v_attn_sink, v_w_mix_out, v_pre_norm_ffn2, v_post_norm_ffn2, v_w_ffn2_gate_up, v_w_ffn2_down, v_rel_bias_table):
    args = dict(locals())
    wts = {n: args[n] for n in ORDER}
    mom = {n: args["m_" + n] for n in ORDER}
    var = {n: args["v_" + n] for n in ORDER}

    x0 = x[0]
    target = loss_target[0]
    t, d = x0.shape
    n_hgrn = d // 2 // HEAD
    n_attn = (d - d // 2) // HEAD
    core = lax.axis_index("c").astype(jnp.int32).reshape(1)
    chip = (2 * lax.axis_index("x") + lax.axis_index("y")).astype(jnp.int32).reshape(1)

    def local(a, name):
        return jnp.transpose(a[0]) if name == "w_mix_in" else a[0]

    full = {n: _all_gather("ag_" + n, local(wts[n], n).astype(BF16), AG_ID[n]) for n in BIG}
    w_gu1, w_gu2 = full["w_ffn1_gate_up"], full["w_ffn2_gate_up"]
    w_d1 = full["w_ffn1_down"].reshape(-1, d)
    w_d2 = full["w_ffn2_down"].reshape(-1, d)
    w_out = full["w_mix_out"].reshape(-1, d)
    w_in_t = full["w_mix_in"].reshape(-1, d)

    g = {n: wts[n] for n in SMALL}
    lb_f = jax.nn.softmax(g["hgrn_lower_bounds_fwd"], axis=0)[0:1]
    lb_b = jax.nn.softmax(g["hgrn_lower_bounds_bwd"], axis=0)[0:1]
    bucket_idx = jnp.asarray(_t5_bucket_index())
    bias = _bias_build(g["rel_bias_table"], bucket_idx)

    n1 = _pre_norm("pre_norm1", x0, g["pre_norm_ffn1"])
    a1, gu1 = _ffn_up("ffn1_gate_up", n1, w_gu1)
    ff1 = _matmul("ffn1_down", a1, w_d1, mode="nn", out_dtype=F32)
    x1, h = _post_res_pre("res1", x0, ff1, g["post_norm_ffn1"], g["pre_norm_mix"], 0.5)
    p = _matmul("mix_in", h, w_in_t, mode="nt", out_dtype=F32, tm=2048, tn=512)
    y_h, o_raw = _hgrn_fwd(p, lb_f, lb_b, g["hgrn_out_norm"], n_hgrn, d)
    cat = _attn_fwd(p, bias, g["attn_sink"], y_h, n_hgrn, n_attn)
    mixed = _matmul("mix_out", cat, w_out, mode="nn", out_dtype=F32)
    x2, n2 = _post_res_pre("res2", x1, mixed, g["post_norm_mix"], g["pre_norm_ffn2"], 1.0)
    a2, gu2 = _ffn_up("ffn2_gate_up", n2, w_gu2)
    ff2 = _matmul("ffn2_down", a2, w_d2, mode="nn", out_dtype=F32)
    small_grad = {}
    dy3, loss_part, dff2, small_grad["post_norm_ffn2"] = _post_res_loss(
        "res3_loss", x2, ff2, g["post_norm_ffn2"], target, 0.5)

    scattered = {}

    pending = []

    def scatter(name, grad_stack):
        started = _pair_start("rs_pair_" + name, grad_stack)
        pending.append((name, started))
        return [started[4]]

    def settle(after, count=len(BIG)):
        deps = []
        while pending and count:
            name, started = pending.pop(0)
            scattered[name] = _reduce_scatter(name, started, after, core, RS_ID[name])
            deps.append(scattered[name][0])
            count -= 1
        return deps

    def ffn_bwd(tag, dff, a, gu, n_in, w_gu, w_d, gu_name, d_name, last):
        dep = settle(dff)

        def dw_down(deps):
            return scatter(d_name, _matmul("dw_down" + tag, a, dff, mode="tn", out_dtype=BF16, tn=2048,
                                           deps=deps).reshape(N_DEV, -1, d))

        def dw_gate_up(part, rows, deps):
            return scatter(gu_name + part, _matmul("dw_gate_up" + tag + part, n_in[:, rows], dgu, mode="tn",
                                                   stack=True, halves=True, out_dtype=BF16, deps=deps))

        dep = dw_down(dep)
        dgu = _ffn_dact("d_act" + tag, dff, w_d, gu, deps=dep)
        dep = settle(dgu)
        if last:
            half = d // 2
            dep = dw_gate_up("_a", slice(0, half), dep)
            dep = dw_gate_up("_b", slice(half, d), dep)
            dep = settle(dep[0], count=1) + dep
        else:
            dep = dw_gate_up("", slice(0, d), dep)
        dn = _matmul("d_norm" + tag, dgu, w_gu, mode="nt", stack=True, halves=True, out_dtype=BF16, deps=dep)
        return dn, settle(dn)

    dn2, dep = ffn_bwd("2", dff2, a2, gu2, n2, w_gu2, w_d2, "w_ffn2_gate_up", "w_ffn2_down", last=False)
    dx2, small_grad["pre_norm_ffn2"], dmixed, small_grad["post_norm_mix"] = _pre_bwd(
        "pre_bwd2", dn2, x2, g["pre_norm_ffn2"], dy3, deps=dep, post=(mixed, g["post_norm_mix"], 1.0))
    dep = settle(dmixed)
    dcat = _matmul("d_cat", dmixed, w_out, mode="nt", out_dtype=BF16, deps=dep)
    dep = scatter("w_mix_out", _matmul("dw_mix_out", cat, dmixed, mode="tn", out_dtype=BF16).reshape(N_DEV, -1, d))
    dq_a, dk_a, dv_a, dbias, dsink_rows = _attn_bwd(p, dcat, bias, g["attn_sink"], n_hgrn, n_attn, deps=dep)
    dq_h, di_h, dzf, dzb, dg_h, dlb_f, dlb_b, small_grad["hgrn_out_norm"] = _hgrn_bwd(
        p, o_raw, dcat, lb_f, lb_b, g["hgrn_out_norm"], n_hgrn)
    dp = jnp.concatenate([dq_h, di_h, dzf, dzb, dg_h, dq_a, dk_a, dv_a], axis=1)
    dep = settle(dp)
    dh = _matmul("d_h", dp, w_in_t, mode="nn", out_dtype=BF16, tm=2048, deps=dep)
    dep = scatter("w_mix_in", _matmul("dw_mix_in", dp, h, mode="tn", out_dtype=BF16, tm=512,
                                         tn=2048).reshape(N_DEV, -1, d))
    dx1, small_grad["pre_norm_mix"], dff1, small_grad["post_norm_ffn1"] = _pre_bwd(
        "pre_bwd_mix", dh, x1, g["pre_norm_mix"], dx2, deps=dep, post=(ff1, g["post_norm_ffn1"], 0.5))

    dn1, dep = ffn_bwd("1", dff1, a1, gu1, n1, w_gu1, w_d1, "w_ffn1_gate_up", "w_ffn1_down", last=True)
    dx0, small_grad["pre_norm_ffn1"] = _pre_bwd("pre_bwd1", dn1, x0, g["pre_norm_ffn1"], dx1, deps=dep)

    def lb_grad(dlb, lb):
        da0 = dlb * lb * (1.0 - lb)
        return jnp.concatenate([da0, -da0], axis=0)

    small_grad["hgrn_lower_bounds_fwd"] = lb_grad(dlb_f, lb_f)
    small_grad["hgrn_lower_bounds_bwd"] = lb_grad(dlb_b, lb_b)
    small_grad["attn_sink"] = dsink_rows[:, :, 0].reshape(1, n_attn)
    small_grad["rel_bias_table"] = jnp.transpose(_bias_reduce(dbias, bucket_idx)[:, :, 0])

    def adam(n, tag, dep, **rows):
        sums, recv = scattered[n + tag]
        return _adam_shard("adam_" + n + tag, local(wts[n], n), local(mom[n], n), local(var[n], n), sums, recv,
                           chip, deps=dep, **rows)

    big_out = {}
    dep = []
    for n in ("w_ffn2_down", "w_ffn2_gate_up", "w_mix_out", "w_mix_in", "w_ffn1_down"):
        big_out[n] = adam(n, "", dep)
        dep = [big_out[n][0]]
    first_half = adam("w_ffn1_gate_up", "_a", dep)

    g_s, d_s, m_s, v_s, loss_row = _small_step(
        "small_step", [small_grad[n] for n in SMALL] + [loss_part], [wts[n] for n in SMALL],
        [mom[n] for n in SMALL], [var[n] for n in SMALL], deps=[first_half[0]])
    loss = loss_row[0, 0]
    grads, delta, new_m, new_v = (dict(zip(SMALL, vals)) for vals in (g_s, d_s, m_s, v_s))

    big_out["w_ffn1_gate_up"] = adam("w_ffn1_gate_up", "_b", [loss_row], first_row=d // 2, earlier=first_half)
    for n in BIG:
        grads[n], delta[n], new_m[n], new_v[n] = [local(o[None], n)[None] for o in big_out[n]]

    return (loss, dx0[None], *[grads[n] for n in ORDER], *[delta[n] for n in ORDER],
            *[new_m[n] for n in ORDER], *[new_v[n] for n in ORDER])
```

```python
import functools
import math

import numpy as np
import jax
import jax.numpy as jnp
from jax import lax
from jax.experimental import pallas as pl
from jax.experimental.pallas import tpu as pltpu
from jax.experimental.pallas import tpu_sc as plsc

F32 = jnp.float32
BF16 = jnp.bfloat16
MESH = pl.DeviceIdType.MESH

N_DEV = 8
EPS = 1e-6
NEG_INF = -1e30
HEAD = 128
CHUNK = 64
WINDOW = 128
KEY_SPAN = 3 * WINDOW
KV_HEADS = 2
REL_BUCKETS = 32
REL_MAX_DIST = 128
ADAM_LR, ADAM_B1, ADAM_B2, ADAM_EPS, ADAM_WD, ADAM_STEP = 0.001, 0.9, 0.999, 1e-08, 0.01, 10
LANES = 128
VMEM_LIMIT = 56 * 1024 * 1024
ANY = pl.BlockSpec(memory_space=pl.ANY)


def _cparams(*sem):
    return pltpu.CompilerParams(dimension_semantics=sem if sem else None, vmem_limit_bytes=VMEM_LIMIT)


def _dot(a, b):
    return jnp.dot(a, b, preferred_element_type=F32)


def _dot_nt(a, b):
    return lax.dot_general(a, b, (((1,), (1,)), ((), ())), preferred_element_type=F32)


def _dot_tn(a, b):
    return lax.dot_general(a, b, (((0,), (0,)), ((), ())), preferred_element_type=F32)


def _tile(dim, target):
    for c in (target, 1024, 512, 256, 128):
        if c <= target and dim % c == 0:
            return c
    return dim


def _row_tile(rows, target):
    fits = [c for c in range(16, min(rows, target) + 1, 16) if rows % c == 0]
    return max(fits) if fits else rows


K_WHOLE = 2048
K_STEP = 2816


def _k_tile(kd):
    if kd <= K_WHOLE:
        return kd
    return max(c for c in range(LANES, K_STEP + 1, LANES) if kd % c == 0)


def _matmul(name, a, b, *, mode, out_dtype, stack=False, halves=False, tm=1024, tn=1024, deps=(), a_cols=None):
    assert not (stack and mode == "nn")
    grp = 1
    if mode == "nn":
        m, kd = a.shape
        n = b.shape[1]
    elif mode == "nt":
        m = a.shape[-2]
        n, kd = (b.shape[1], b.shape[0] * b.shape[2]) if stack else b.shape
    else:
        kd, m = a.shape
        if a_cols:
            m = a_cols.stop - a_cols.start
        n = b.shape[-1] * (2 if halves else 1)
    if stack:
        n1 = b.shape[2] if mode == "nt" else n // N_DEV
        assert n1 % LANES == 0
        if mode == "nt":
            grp = 2 if 2 * n1 <= K_STEP else 1
            tk = grp * n1
        else:
            tn = n1
    per_half = N_DEV // 2 // grp
    tm = _tile(m, tm)
    if not (stack and mode == "tn"):
        tn = _tile(n, tn)
    if not (stack and mode == "nt"):
        tk = _k_tile(kd)
    nk = kd // tk
    lead = None if grp == 1 else grp
    b_outer = nk == 1 and b.size > a.size
    grid = (n // tn, m // tm, nk) if b_outer else (m // tm, n // tn, nk)

    def spec(shape, index):
        return pl.BlockSpec(shape, (lambda g0, g1, k: index(g1, g0, k)) if b_outer else index)

    if mode == "nn":
        a_spec = spec((tm, tk), lambda i, j, k: (i, k))
        b_spec = spec((tk, tn), lambda i, j, k: (k, j))
        dot = _dot
    elif mode == "nt":
        if halves:
            a_spec = spec((None, tm, tk), lambda i, j, k: (k // per_half, i, k % per_half))
        else:
            a_spec = spec((tm, tk), lambda i, j, k: (i, k))
        if stack:
            b_spec = spec((lead, tn, n1), lambda i, j, k: (k, j, 0))
        else:
            b_spec = spec((tn, tk), lambda i, j, k: (j, k))
        dot = _dot_nt
    else:
        first = a_cols.start // tm if a_cols else 0
        assert not a_cols or a_cols.start % tm == 0
        a_spec = spec((tk, tm), lambda i, j, k: (k, first + i))
        if halves:
            b_spec = spec((None, tk, tn), lambda i, j, k: (j // per_half, k, j % per_half))
        else:
            b_spec = spec((tk, tn), lambda i, j, k: (k, j))
        dot = _dot_tn
    if stack and mode == "tn":
        out_shape = jax.ShapeDtypeStruct((N_DEV, m, n1), out_dtype)
        o_spec = spec((None, tm, n1), lambda i, j, k: (j, i, 0))
    else:
        out_shape = jax.ShapeDtypeStruct((m, n), out_dtype)
        o_spec = spec((tm, tn), lambda i, j, k: (i, j))

    def product(a_ref, b_ref):
        bmat = jnp.concatenate([b_ref[s] for s in range(grp)], axis=1) if grp > 1 else b_ref[...]
        return dot(a_ref[...], bmat)

    def store(o_ref, val):
        o_ref[...] = val.astype(o_ref.dtype)

    def body_whole(a_ref, b_ref, *rest):
        store(rest[-1], product(a_ref, b_ref))

    def body_steps(a_ref, b_ref, *rest):
        o_ref, acc_ref = rest[-2:]
        k = pl.program_id(2)

        @pl.when(k == 0)
        def _():
            acc_ref[...] = product(a_ref, b_ref)

        @pl.when(k > 0)
        def _():
            acc_ref[...] += product(a_ref, b_ref)

        @pl.when(k == nk - 1)
        def _():
            store(o_ref, acc_ref[...])

    return pl.pallas_call(
        body_whole if nk == 1 else body_steps, name=name, grid=grid,
        in_specs=[a_spec, b_spec] + [ANY] * len(deps), out_specs=o_spec, out_shape=out_shape,
        scratch_shapes=[] if nk == 1 else [pltpu.VMEM((tm, tn), F32)],
        compiler_params=_cparams("parallel", "parallel", "arbitrary"),
    )(a, b, *deps)


def _col_parts(width, parts=2):
    groups = width // LANES
    parts = max(1, min(parts, groups // 2))
    bounds = [LANES * (groups * p // parts) for p in range(parts)] + [width]
    return [slice(bounds[p], bounds[p + 1]) for p in range(parts)]


def _ffn_up(name, n, w_stack):
    t, d = n.shape
    s, _, n1 = w_stack.shape
    half = s // 2
    tm = _tile(t, 512)

    def body(n_ref, wg_ref, wu_ref, act_ref, gu_ref):
        nv = n_ref[...]
        for cols in _col_parts(n1):
            gate = _dot(nv, wg_ref[:, cols])
            up = _dot(nv, wu_ref[:, cols])
            sg = jax.nn.sigmoid(gate)
            silu = gate * sg
            act_ref[:, cols] = (silu * up).astype(BF16)
            gu_ref[0, :, cols] = (up * (sg * (1.0 + gate * (1.0 - sg)))).astype(BF16)
            gu_ref[1, :, cols] = silu.astype(BF16)

    return pl.pallas_call(
        body, name=name, grid=(half, t // tm),
        in_specs=[pl.BlockSpec((tm, d), lambda j, i: (i, 0)),
                  pl.BlockSpec((None, d, n1), lambda j, i: (j, 0, 0)),
                  pl.BlockSpec((None, d, n1), lambda j, i: (half + j, 0, 0))],
        out_specs=[pl.BlockSpec((tm, n1), lambda j, i: (i, j)), pl.BlockSpec((2, tm, n1), lambda j, i: (0, i, j))],
        out_shape=[jax.ShapeDtypeStruct((t, half * n1), BF16), jax.ShapeDtypeStruct((2, t, half * n1), BF16)],
        compiler_params=_cparams("parallel", "parallel"),
    )(n, w_stack, w_stack)


def _ffn_dact(name, dff, w_d, gu, deps=()):
    t, d = dff.shape
    f = w_d.shape[0]
    tm = _tile(t, 1024)
    tn = _tile(f, 1408)

    def body(dff_ref, w_ref, gu_ref, *rest):
        dgu_ref = rest[-1]
        da = _dot_nt(dff_ref[...], w_ref[...]).astype(BF16)
        dgu_ref[0] = da * gu_ref[0]
        dgu_ref[1] = da * gu_ref[1]

    pair = pl.BlockSpec((2, tm, tn), lambda j, i: (0, i, j))
    return pl.pallas_call(
        body, name=name, grid=(f // tn, t // tm),
        in_specs=[pl.BlockSpec((tm, d), lambda j, i: (i, 0)), pl.BlockSpec((tn, d), lambda j, i: (j, 0)), pair]
        + [ANY] * len(deps),
        out_specs=pair, out_shape=jax.ShapeDtypeStruct((2, t, f), BF16),
        compiler_params=_cparams("parallel", "parallel"),
    )(dff, w_d, gu, *deps)


ROWS = 256


def _rstd(xf):
    return lax.rsqrt(jnp.mean(xf * xf, axis=-1, keepdims=True) + EPS)


def _row_spec(t, d):
    return pl.BlockSpec((min(ROWS, t), d), lambda i: (i, 0))


def _vec_spec(d):
    return pl.BlockSpec((1, d), lambda i: (0, 0))


def _pre_norm(name, x, gain):
    t, d = x.shape

    def body(x_ref, g_ref, n_ref):
        xf = x_ref[...]
        n_ref[...] = (xf * _rstd(xf) * g_ref[...]).astype(BF16)

    return pl.pallas_call(
        body, name=name, grid=(t // min(ROWS, t),), in_specs=[_row_spec(t, d), _vec_spec(d)],
        out_specs=_row_spec(t, d), out_shape=jax.ShapeDtypeStruct((t, d), BF16),
        compiler_params=_cparams("parallel"),
    )(x, gain)


def _post_res_pre(name, x, ff, g_post, g_next, scale):
    t, d = x.shape

    def body(x_ref, ff_ref, gp_ref, gn_ref, xo_ref, n_ref):
        ff_ = ff_ref[...]
        xn = x_ref[...] + scale * (ff_ * _rstd(ff_) * gp_ref[...])
        xo_ref[...] = xn
        n_ref[...] = (xn * _rstd(xn) * gn_ref[...]).astype(BF16)

    return pl.pallas_call(
        body, name=name, grid=(t // min(ROWS, t),),
        in_specs=[_row_spec(t, d), _row_spec(t, d), _vec_spec(d), _vec_spec(d)],
        out_specs=[_row_spec(t, d), _row_spec(t, d)],
        out_shape=[jax.ShapeDtypeStruct((t, d), F32), jax.ShapeDtypeStruct((t, d), BF16)],
        compiler_params=_cparams("parallel"),
    )(x, ff, g_post, g_next)


def _norm_bwd(xf, gain, dy):
    r = _rstd(xf)
    xh = xf * r
    dxh = dy * gain
    return r * (dxh - xh * jnp.mean(dxh * xh, axis=-1, keepdims=True)), jnp.sum(dy * xh, axis=0, keepdims=True)


def _accumulate(ref, part):
    @pl.when(pl.program_id(0) == 0)
    def _():
        ref[...] = jnp.zeros_like(ref)

    ref[...] += jnp.broadcast_to(part, ref.shape)


def _post_res_loss(name, x, ff, g_post, target, scale):
    t, d = x.shape

    def body(x_ref, ff_ref, gp_ref, tg_ref, dy_ref, loss_ref, dff_ref, dg_ref):
        ff_ = ff_ref[...]
        err = x_ref[...] + scale * (ff_ * _rstd(ff_) * gp_ref[...]) - tg_ref[...]
        dy = err / d
        dy_ref[...] = dy
        _accumulate(loss_ref, 0.5 * jnp.sum(jnp.mean(err * err, axis=-1, keepdims=True), axis=0, keepdims=True))
        dff, dg = _norm_bwd(ff_, gp_ref[...], scale * dy)
        dff_ref[...] = dff.astype(BF16)
        _accumulate(dg_ref, dg)

    return pl.pallas_call(
        body, name=name, grid=(t // min(ROWS, t),),
        in_specs=[_row_spec(t, d), _row_spec(t, d), _vec_spec(d), _row_spec(t, d)],
        out_specs=[_row_spec(t, d), _vec_spec(LANES), _row_spec(t, d), _vec_spec(d)],
        out_shape=[jax.ShapeDtypeStruct((t, d), F32), jax.ShapeDtypeStruct((1, LANES), F32),
                   jax.ShapeDtypeStruct((t, d), BF16), jax.ShapeDtypeStruct((1, d), F32)],
        compiler_params=_cparams("arbitrary"),
    )(x, ff, g_post, target)


def _pre_bwd(name, dn, x, g_pre, dy, deps=(), post=None):
    t, d = x.shape
    n_post = 2 if post else 0
    scale = post[2] if post else None

    def body(dn_ref, x_ref, g_ref, dy_ref, *rest):
        outs = rest[len(rest) - 2 - n_post:]
        dnf = dn_ref[...].astype(F32)
        dpre, dg = _norm_bwd(x_ref[...], g_ref[...], dnf)
        dx = dy_ref[...] + dpre
        outs[0][...] = dx
        _accumulate(outs[1], dg)
        if post:
            ff_ref, gp_ref = rest[0], rest[1]
            dff, dgp = _norm_bwd(ff_ref[...], gp_ref[...], scale * dx)
            outs[2][...] = dff.astype(BF16)
            _accumulate(outs[3], dgp)

    extra_in = [_row_spec(t, d), _vec_spec(d)] if post else []
    extra_out = [_row_spec(t, d), _vec_spec(d)] if post else []
    extra_shape = [jax.ShapeDtypeStruct((t, d), BF16), jax.ShapeDtypeStruct((1, d), F32)] if post else []
    return pl.pallas_call(
        body, name=name, grid=(t // min(ROWS, t),),
        in_specs=[_row_spec(t, d), _row_spec(t, d), _vec_spec(d), _row_spec(t, d)] + extra_in + [ANY] * len(deps),
        out_specs=[_row_spec(t, d), _vec_spec(d)] + extra_out,
        out_shape=[jax.ShapeDtypeStruct((t, d), F32), jax.ShapeDtypeStruct((1, d), F32)] + extra_shape,
        compiler_params=_cparams("arbitrary"),
    )(dn, x, g_pre, dy, *(post[:2] if post else ()), *deps)


def _bdot(a, b, ca, cb, precision=None):
    return lax.dot_general(a, b, (((ca,), (cb,)), ((0,), (0,))), preferred_element_type=F32, precision=precision)


def _tri_masks(g):
    row = lax.broadcasted_iota(jnp.int32, (g, CHUNK, CHUNK), 1)
    col = lax.broadcasted_iota(jnp.int32, (g, CHUNK, CHUNK), 2)
    return col <= row, col >= row


def _ones_matmul(ones_mat, val):
    hi = val.astype(BF16)
    lo = (val - hi.astype(F32)).astype(BF16)
    return _bdot(ones_mat, hi, 2, 1) + _bdot(ones_mat, lo, 2, 1)


def _hgrn_block(z, lb, q, v, cum_mat):
    sg = jax.nn.sigmoid(z)
    f = lb + (1.0 - lb) * sg
    lf = jnp.log(f)
    k = 1.0 - f
    a = _ones_matmul(cum_mat, lf)
    last = jnp.sum(lf, axis=1, keepdims=True)
    e_a = jnp.exp(a)
    e_na = jnp.exp(-a)
    e_t = jnp.exp(last - a)
    return dict(sg=sg, f=f, k=k, decay=jnp.exp(last), e_a=e_a, e_na=e_na, e_t=e_t,
                qd=q * e_a, kd=k * e_na, kt=k * e_t)


def _hgrn_states(state, kv, decay, order):
    entering = [None] * len(order)
    for g in order:
        entering[g] = state
        state = decay[g] * state + kv[g]
    return jnp.stack(entering, axis=0), state


def _hgrn_fwd(p, lb_f, lb_b, gain, n_heads, width):
    t = p.shape[0]
    w = n_heads * HEAD
    blk = min(16, t // CHUNK)
    rows_blk = blk * CHUNK
    n_blocks = t // rows_blk
    fin_rows = min(256, t)

    def body(q_ref, i_ref, zf_ref, zb_ref, g_ref, lbf_ref, lbb_ref, gain_ref, y_ref, o_ref, st_ref):
        low, up = _tri_masks(blk)
        m_low, m_up = low.astype(BF16), up.astype(BF16)
        o_ref[...] = jnp.zeros_like(o_ref)
        st_ref[...] = jnp.zeros_like(st_ref)

        def one(r0, z_ref, lb, slot, rev):
            rows = pl.ds(r0, rows_blk)
            split = lambda ref: ref[rows, :].reshape(blk, CHUNK, HEAD)
            q, v = split(q_ref), split(i_ref)
            c = _hgrn_block(split(z_ref), lb, q, v, m_up if rev else m_low)
            qd, kd, kt, vb = c["qd"].astype(BF16), c["kd"].astype(BF16), c["kt"].astype(BF16), v.astype(BF16)
            pm = jnp.where(up if rev else low, _bdot(qd, kd, 2, 2), 0.0).astype(BF16)
            kv = _bdot(vb, kt, 1, 1)
            order = range(blk - 1, -1, -1) if rev else range(blk)
            entering, st_ref[slot] = _hgrn_states(st_ref[slot], kv, c["decay"], order)
            o = _bdot(pm, vb, 2, 1) + _bdot(qd, entering.astype(BF16), 2, 2)
            o_ref[rows, :] += o.reshape(rows_blk, HEAD)

        def step(n, carry):
            one(pl.multiple_of(n * rows_blk, rows_blk), zf_ref, lbf_ref[...], 0, False)
            one(pl.multiple_of((n_blocks - 1 - n) * rows_blk, rows_blk), zb_ref, lbb_ref[...], 1, True)
            return carry

        lax.fori_loop(0, n_blocks, step, 0)

        def fin(n, carry):
            rows = pl.ds(pl.multiple_of(n * fin_rows, fin_rows), fin_rows)
            o = o_ref[rows, :]
            g = g_ref[rows, :]
            y_ref[rows, :] = (o * _rstd(o) * gain_ref[...] * (g * jax.nn.sigmoid(g))).astype(BF16)
            return carry

        lax.fori_loop(0, t // fin_rows, fin, 0)

    col = lambda grp: pl.BlockSpec((t, HEAD), lambda h: (0, grp * n_heads + h))
    vec = pl.BlockSpec((1, HEAD), lambda h: (0, h))
    out = pl.BlockSpec((t, HEAD), lambda h: (0, h))
    return pl.pallas_call(
        body, name="hgrn_fwd", grid=(n_heads,),
        in_specs=[col(0), col(1), col(2), col(3), col(4), vec, vec, vec],
        out_specs=[out, out],
        out_shape=[jax.ShapeDtypeStruct((t, width), BF16), jax.ShapeDtypeStruct((t, w), F32)],
        scratch_shapes=[pltpu.VMEM((2, HEAD, HEAD), F32)],
        compiler_params=_cparams("parallel"),
    )(p, p, p, p, p, lb_f, lb_b, gain)


def _hgrn_bwd(p, o_raw, dcat, lb_f, lb_b, gain, n_heads):
    t = p.shape[0]
    w = n_heads * HEAD
    n_chunks = t // CHUNK
    blk = min(16, n_chunks)
    rows_blk = blk * CHUNK
    n_blocks = t // rows_blk
    rb = min(256, t)

    def body(q_ref, i_ref, zf_ref, zb_ref, g_ref, o_ref, dy_ref, lbf_ref, lbb_ref, gain_ref,
             dq_ref, di_ref, dzf_ref, dzb_ref, dg_ref, dlbf_ref, dlbb_ref, dgain_ref,
             do_s, dq_s, dv_s, st_s, cur_s):
        low, up = _tri_masks(blk)
        m_low, m_up = low.astype(BF16), up.astype(BF16)
        rowid = lax.broadcasted_iota(jnp.int32, (blk, CHUNK, HEAD), 1)
        gain_v = gain_ref[...]

        def norm_bwd(n, dgain):
            rows = pl.ds(pl.multiple_of(n * rb, rb), rb)
            o = o_ref[rows, :]
            g = g_ref[rows, :]
            dy = dy_ref[rows, :].astype(F32)
            r = _rstd(o)
            oh = o * r
            sg = jax.nn.sigmoid(g)
            dg_ref[rows, :] = (dy * oh * gain_v * (sg * (1.0 + g * (1.0 - sg)))).astype(BF16)
            dno = dy * (g * sg)
            dxh = dno * gain_v
            do_s[rows, :] = r * (dxh - oh * jnp.mean(dxh * oh, axis=-1, keepdims=True))
            return dgain + jnp.sum(dno * oh, axis=0, keepdims=True)

        dgain_ref[...] = lax.fori_loop(0, t // rb, norm_bwd, jnp.zeros((1, HEAD), F32))
        def direction(z_ref, lb_ref, dz_ref, dlb_ref, rev):
            way = int(rev)
            lb = lb_ref[...]
            cum_mat = m_up if rev else m_low
            cum_mat_t = m_low if rev else m_up
            mask = up if rev else low
            last_row = 0 if rev else CHUNK - 1

            order = range(blk - 1, -1, -1) if rev else range(blk)

            def rows_of(j):
                bidx = (n_blocks - 1 - j) if rev else j
                return bidx, pl.ds(pl.multiple_of(bidx * rows_blk, rows_blk), rows_blk)

            def load(rows):
                split = lambda ref: ref[rows, :].reshape(blk, CHUNK, HEAD)
                q, v = split(q_ref), split(i_ref)
                return q, v, _hgrn_block(split(z_ref), lb, q, v, cum_mat)

            def sweep_fwd(j):
                bidx, rows = rows_of(j)
                _, v, c = load(rows)
                kv = _bdot(v.astype(BF16), c["kt"].astype(BF16), 1, 1)
                st_s[way, pl.ds(bidx * blk, blk)], cur_s[2 * way] = _hgrn_states(
                    cur_s[2 * way], kv, c["decay"], order)

            dlb_ref[...] = jnp.zeros_like(dlb_ref)

            def sweep_bwd(jj):
                bidx, rows = rows_of(n_blocks - 1 - jj)
                _, v, c = load(rows)
                st = st_s[way, pl.ds(bidx * blk, blk)]
                do = do_s[rows, :].reshape(blk, CHUNK, HEAD)
                qd, kd, kt, decay = c["qd"], c["kd"], c["kt"], c["decay"]
                qd_b, kd_b, kt_b = qd.astype(BF16), kd.astype(BF16), kt.astype(BF16)
                v_b, do_b, st_b = v.astype(BF16), do.astype(BF16), st.astype(BF16)
                pm = jnp.where(mask, _bdot(qd_b, kd_b, 2, 2), 0.0).astype(BF16)
                dpm = jnp.where(mask, _bdot(do_b, v_b, 2, 2), 0.0).astype(BF16)
                gq = _bdot(do_b, qd_b, 1, 1)
                dstate = cur_s[2 * way + 1]
                after = [None] * blk
                for g in reversed(order):
                    after[g] = dstate
                    dstate = gq[g] + decay[g] * dstate
                cur_s[2 * way + 1] = dstate
                dst = jnp.stack(after, axis=0)
                dst_b = dst.astype(BF16)
                dv = _bdot(pm, do_b, 1, 1) + _bdot(kt_b, dst_b, 2, 2)
                dqd = _bdot(dpm, kd_b, 2, 1) + _bdot(do_b, st_b, 2, 1)
                dkd = _bdot(dpm, qd_b, 1, 1)
                dkt = _bdot(v_b, dst_b, 2, 1)
                dlast = (jnp.sum(dkt * kt, axis=1, keepdims=True)
                         + decay * jnp.sum(dst * st, axis=1, keepdims=True))
                dq_s[way, rows, :] = (dqd * c["e_a"]).reshape(rows_blk, HEAD)
                dv_s[way, rows, :] = dv.reshape(rows_blk, HEAD)
                dk = dkd * c["e_na"] + dkt * c["e_t"]
                da = dqd * qd - dkd * kd - dkt * kt
                da = da + jnp.where(rowid == last_row, dlast, 0.0)
                dlf = _ones_matmul(cum_mat_t, da)
                df = dlf / c["f"] - dk
                sg = c["sg"]
                dz_ref[rows, :] = (df * (1.0 - lb) * (sg * (1.0 - sg))).reshape(rows_blk, HEAD).astype(BF16)
                dlb_ref[...] += jnp.sum((df * (1.0 - sg)).reshape(rows_blk, HEAD), axis=0, keepdims=True)

            return sweep_fwd, sweep_bwd

        ways = [direction(zf_ref, lbf_ref, dzf_ref, dlbf_ref, False),
                direction(zb_ref, lbb_ref, dzb_ref, dlbb_ref, True)]
        cur_s[...] = jnp.zeros_like(cur_s)
        for sweep in range(2):
            def both(j, carry):
                for way in ways:
                    way[sweep](j)
                return carry

            lax.fori_loop(0, n_blocks, both, 0)
        dq_ref[...] = (dq_s[0] + dq_s[1]).astype(BF16)
        di_ref[...] = (dv_s[0] + dv_s[1]).astype(BF16)

    col = lambda grp: pl.BlockSpec((t, HEAD), lambda h: (0, grp * n_heads + h))
    one = pl.BlockSpec((t, HEAD), lambda h: (0, h))
    vec = pl.BlockSpec((1, HEAD), lambda h: (0, h))
    big = jax.ShapeDtypeStruct((t, w), BF16)
    small = jax.ShapeDtypeStruct((1, w), F32)
    return pl.pallas_call(
        body, name="hgrn_bwd", grid=(n_heads,),
        in_specs=[col(0), col(1), col(2), col(3), col(4), one, one, vec, vec, vec],
        out_specs=[one] * 5 + [vec] * 3,
        out_shape=[big] * 5 + [small] * 3,
        scratch_shapes=[pltpu.VMEM((t, HEAD), F32), pltpu.VMEM((2, t, HEAD), F32), pltpu.VMEM((2, t, HEAD), F32),
                        pltpu.VMEM((2, n_chunks, HEAD, HEAD), F32), pltpu.VMEM((4, HEAD, HEAD), F32)],
        compiler_params=_cparams("parallel"),
    )(p, p, p, p, p, o_raw, dcat, lb_f, lb_b, gain)


def _t5_bucket_index():
    c = np.arange(WINDOW)[:, None]
    s = np.arange(KEY_SPAN)[None, :]
    rel = s - WINDOW - c
    nb = REL_BUCKETS // 2
    max_exact = nb // 2
    bucket = (rel > 0).astype(np.int32) * nb
    n = np.abs(rel)
    large = max_exact + (np.log(np.maximum(n, 1) / max_exact) / np.log(REL_MAX_DIST / max_exact)
                         * (nb - max_exact)).astype(np.int32)
    large = np.minimum(large, nb - 1)
    return bucket + np.where(n < max_exact, n, large).astype(np.int32)


def _bias_build(table, idx):
    n_attn = table.shape[1]

    def body(tab_ref, idx_ref, o_ref):
        h = pl.program_id(0)
        idx_v = idx_ref[...]
        acc = jnp.zeros((WINDOW, KEY_SPAN), F32)
        for b in range(REL_BUCKETS):
            acc = jnp.where(idx_v == b, tab_ref[b, h], acc)
        o_ref[...] = acc

    return pl.pallas_call(
        body, name="bias_build", grid=(n_attn,),
        in_specs=[pl.BlockSpec(memory_space=pltpu.SMEM), pl.BlockSpec((WINDOW, KEY_SPAN), lambda h: (0, 0))],
        out_specs=pl.BlockSpec((None, WINDOW, KEY_SPAN), lambda h: (h, 0, 0)),
        out_shape=jax.ShapeDtypeStruct((n_attn, WINDOW, KEY_SPAN), F32), compiler_params=_cparams("parallel"),
    )(table, idx)


def _bias_reduce(dbias, idx):
    n_attn = dbias.shape[0]

    def body(idx_ref, d_ref, o_ref):
        idx_v = idx_ref[...]
        dv = d_ref[...]
        rows = lax.broadcasted_iota(jnp.int32, (REL_BUCKETS, LANES), 0)
        acc = jnp.zeros((REL_BUCKETS, LANES), F32)
        for b in range(REL_BUCKETS):
            part = jnp.sum(jnp.where(idx_v == b, dv, 0.0), axis=1, keepdims=True)
            acc = jnp.where(rows == b, jnp.sum(part, axis=0, keepdims=True), acc)
        o_ref[...] = acc

    return pl.pallas_call(
        body, name="bias_reduce", grid=(n_attn,),
        in_specs=[pl.BlockSpec((WINDOW, KEY_SPAN), lambda h: (0, 0)),
                  pl.BlockSpec((None, WINDOW, KEY_SPAN), lambda h: (h, 0, 0))],
        out_specs=pl.BlockSpec((None, REL_BUCKETS, LANES), lambda h: (h, 0, 0)),
        out_shape=jax.ShapeDtypeStruct((n_attn, REL_BUCKETS, LANES), F32), compiler_params=_cparams("parallel"),
    )(idx, dbias)


def _attn_probs(q, kb, bias, sink, valid):
    s = _dot_nt(q, kb) / math.sqrt(HEAD) + bias
    s = jnp.where(valid, s, NEG_INF)
    m = jnp.maximum(jnp.max(s, axis=-1, keepdims=True), sink)
    e = jnp.exp(s - m)
    e_sink = jnp.exp(sink - m)
    den = jnp.sum(e, axis=-1, keepdims=True) + e_sink
    return e / den, e_sink / den


def _attn_valid(n, t, grp):
    c = lax.broadcasted_iota(jnp.int32, (grp * WINDOW, KEY_SPAN), 0) & (WINDOW - 1)
    s = lax.broadcasted_iota(jnp.int32, (grp * WINDOW, KEY_SPAN), 1)
    rel = s - WINDOW - c
    key_pos = n * WINDOW - WINDOW + s
    return (jnp.abs(rel) <= WINDOW) & (key_pos >= 0) & (key_pos < t)


def _stack_heads(ref, grp):
    return jnp.concatenate([ref[:, g * HEAD:(g + 1) * HEAD] for g in range(grp)], axis=0).astype(BF16)


def _sink_column(sink_ref, x, grp):
    return jnp.concatenate([jnp.full((WINDOW, 1), sink_ref[0, x * grp + g], F32) for g in range(grp)], axis=0)


def _attn_specs(t, n_hgrn, n_attn):
    grp = n_attn // KV_HEADS
    nb = t // WINDOW
    cq = 5 * n_hgrn
    ck = cq + n_attn
    cv = ck + KV_HEADS
    q_spec = pl.BlockSpec((WINDOW, grp * HEAD), lambda x, n: (n, cq // grp + x))
    kv = lambda base, off: pl.BlockSpec(
        (WINDOW, HEAD), lambda x, n: (jnp.clip(n + off, 0, nb - 1), base + x))
    band = [kv(ck, -1), kv(ck, 0), kv(ck, 1), kv(cv, -1), kv(cv, 0), kv(cv, 1)]
    bias_spec = pl.BlockSpec((grp, WINDOW, KEY_SPAN), lambda x, n: (x, 0, 0))
    sink_spec = pl.BlockSpec(memory_space=pltpu.SMEM)
    return grp, nb, q_spec, band, bias_spec, sink_spec


def _attn_fwd(p, bias, sink, cat, n_hgrn, n_attn):
    t = p.shape[0]
    grp, nb, q_spec, band, bias_spec, sink_spec = _attn_specs(t, n_hgrn, n_attn)

    def body(q_ref, kp, kc, kn, vp, vc, vn, bias_ref, sink_ref, cat_ref, y_ref):
        x, n = pl.program_id(0), pl.program_id(1)
        kb = jnp.concatenate([kp[...], kc[...], kn[...]], axis=0).astype(BF16)
        vb = jnp.concatenate([vp[...], vc[...], vn[...]], axis=0).astype(BF16)
        pr, _ = _attn_probs(_stack_heads(q_ref, grp), kb, bias_ref[...].reshape(grp * WINDOW, KEY_SPAN),
                            _sink_column(sink_ref, x, grp), _attn_valid(n, t, grp))
        y = _dot(pr.astype(BF16), vb).astype(BF16)
        for g in range(grp):
            y_ref[:, g * HEAD:(g + 1) * HEAD] = y[g * WINDOW:(g + 1) * WINDOW]

    return pl.pallas_call(
        body, name="attn_fwd", grid=(KV_HEADS, nb),
        in_specs=[q_spec] + band + [bias_spec, sink_spec, ANY],
        out_specs=pl.BlockSpec((WINDOW, grp * HEAD), lambda x, n: (n, n_hgrn // grp + x)),
        out_shape=jax.ShapeDtypeStruct(cat.shape, BF16), input_output_aliases={9: 0},
        compiler_params=_cparams("parallel", "parallel"),
    )(p, p, p, p, p, p, p, bias, sink, cat)


def _attn_bwd(p, dcat, bias, sink, n_hgrn, n_attn, deps=()):
    t = p.shape[0]
    grp, nb, q_spec, band, bias_spec, sink_spec = _attn_specs(t, n_hgrn, n_attn)
    inv = 1.0 / math.sqrt(HEAD)

    def body(q_ref, kp, kc, kn, vp, vc, vn, bias_ref, sink_ref, do_ref, *rest):
        dq_ref, dk_ref, dv_ref, dbias_ref, dsink_ref, dk_s, dv_s = rest[-7:]
        x, n = pl.program_id(0), pl.program_id(1)

        @pl.when(n == 0)
        def _():
            dk_s[...] = jnp.zeros_like(dk_s)
            dv_s[...] = jnp.zeros_like(dv_s)
            dbias_ref[...] = jnp.zeros_like(dbias_ref)
            dsink_ref[...] = jnp.zeros_like(dsink_ref)

        kb = jnp.concatenate([kp[...], kc[...], kn[...]], axis=0).astype(BF16)
        vb = jnp.concatenate([vp[...], vc[...], vn[...]], axis=0).astype(BF16)
        q = _stack_heads(q_ref, grp)
        do = _stack_heads(do_ref, grp)
        pr, p_sink = _attn_probs(q, kb, bias_ref[...].reshape(grp * WINDOW, KEY_SPAN),
                                 _sink_column(sink_ref, x, grp), _attn_valid(n, t, grp))
        dpr = _dot_nt(do, vb)
        delta = jnp.sum(pr * dpr, axis=-1, keepdims=True)
        ds = pr * (dpr - delta)
        ds_b = ds.astype(BF16)
        dq = (_dot(ds_b, kb) * inv).astype(BF16)
        dsink = -p_sink * delta
        for g in range(grp):
            head = slice(g * WINDOW, (g + 1) * WINDOW)
            dq_ref[:, g * HEAD:(g + 1) * HEAD] = dq[head]
            dbias_ref[g] += ds[head]
            dsink_ref[g:g + 1, :] += jnp.broadcast_to(jnp.sum(dsink[head], axis=0, keepdims=True), (1, WINDOW))
        rows = pl.ds(pl.multiple_of(n * WINDOW, WINDOW), KEY_SPAN)
        dk_s[rows, :] += _dot_tn(ds_b, q) * inv
        dv_s[rows, :] += _dot_tn(pr.astype(BF16), do)

        @pl.when(n == nb - 1)
        def _():
            dk_ref[...] = dk_s[pl.ds(WINDOW, t), :].astype(BF16)
            dv_ref[...] = dv_s[pl.ds(WINDOW, t), :].astype(BF16)

    do_spec = pl.BlockSpec((WINDOW, grp * HEAD), lambda x, n: (n, n_hgrn // grp + x))
    kv_out = pl.BlockSpec((t, HEAD), lambda x, n: (0, x))
    return pl.pallas_call(
        body, name="attn_bwd", grid=(KV_HEADS, nb),
        in_specs=[q_spec] + band + [bias_spec, sink_spec, do_spec] + [ANY] * len(deps),
        out_specs=[pl.BlockSpec((WINDOW, grp * HEAD), lambda x, n: (n, x)), kv_out, kv_out,
                   bias_spec, pl.BlockSpec((None, grp, WINDOW), lambda x, n: (x, 0, 0))],
        out_shape=[jax.ShapeDtypeStruct((t, n_attn * HEAD), BF16),
                   jax.ShapeDtypeStruct((t, KV_HEADS * HEAD), BF16),
                   jax.ShapeDtypeStruct((t, KV_HEADS * HEAD), BF16),
                   jax.ShapeDtypeStruct((n_attn, WINDOW, KEY_SPAN), F32),
                   jax.ShapeDtypeStruct((KV_HEADS, grp, WINDOW), F32)],
        scratch_shapes=[pltpu.VMEM((t + 2 * WINDOW, HEAD), F32), pltpu.VMEM((t + 2 * WINDOW, HEAD), F32)],
        compiler_params=_cparams("parallel", "arbitrary"),
    )(p, p, p, p, p, p, p, bias, sink, dcat, *deps)


def _position():
    return lax.axis_index("x"), lax.axis_index("y"), lax.axis_index("c")


def _handshake(peers):
    barrier = pltpu.get_barrier_semaphore()
    for peer in peers:
        pl.semaphore_signal(barrier, inc=1, device_id=peer, device_id_type=MESH)
    pl.semaphore_wait(barrier, len(peers))


def _sequencer(name, collective_id, scratch_types):
    return functools.partial(
        pl.kernel, mesh=plsc.ScalarSubcoreMesh(axis_name="sc", num_cores=1), name=name,
        scratch_types=scratch_types, compiler_params=pltpu.CompilerParams(collective_id=collective_id))


def _all_gather(name, shard, collective_id):
    rows = shard.shape[0]
    assert rows % 2 == 0
    rh = rows // 2
    src = jax.new_ref(shard, memory_space=pltpu.MemorySpace.HBM)
    out = jax.empty_ref(jax.ShapeDtypeStruct((N_DEV,) + shard.shape, shard.dtype),
                        memory_space=pltpu.MemorySpace.HBM)
    n_copies = 11

    @_sequencer(name, collective_id, (pltpu.SemaphoreType.DMA((n_copies,)), pltpu.SemaphoreType.DMA((n_copies,)),
                                      pltpu.SemaphoreType.DMA))
    def launch(send_sems, recv_sems, local_sem):
        x, y, c = _position()
        sibling = (x, y, 1 - c)
        xn, yn, dg = (1 - x, y), (x, 1 - y), (1 - x, 1 - y)
        _handshake([sibling, (*xn, c), (*yn, c)])

        def part(ref, half):
            return ref if half is None else ref.at[pl.ds(half * rh, rh)]

        def slot(chip, core, half=None):
            return part(out.at[4 * chip[0] + 2 * chip[1] + core], half)

        def copy(k, chip, core, half, to, own=False):
            return pltpu.make_async_remote_copy(
                src_ref=part(src, half) if own else slot(chip, core, half), dst_ref=slot(chip, core, half),
                send_sem=send_sems.at[k], recv_sem=recv_sems.at[k], device_id=to, device_id_type=MESH)

        def landed(k, chip, core, half):
            copy(k, chip, core, half, (x, y, c)).wait_recv()

        mine = pltpu.make_async_copy(src, slot((x, y), c), local_sem)
        mine.start()
        sent = [copy(0, (x, y), c, None, sibling, own=True),
                copy(1, (x, y), c, 0, (*xn, c), own=True), copy(3, (x, y), c, 1, (*yn, c), own=True),
                copy(2, (x, y), c, 1, (*xn, c), own=True), copy(4, (x, y), c, 0, (*yn, c), own=True)]
        for cp in sent:
            cp.start()

        def then(cp):
            cp.start()
            sent.append(cp)

        landed(1, xn, c, 0)
        then(copy(5, xn, c, 0, (*yn, c)))
        landed(3, yn, c, 1)
        then(copy(6, yn, c, 1, (*xn, c)))
        landed(2, xn, c, 1)
        then(copy(7, xn, c, None, sibling))
        landed(4, yn, c, 0)
        then(copy(8, yn, c, None, sibling))
        landed(5, dg, c, 0)
        then(copy(9, dg, c, 0, sibling))
        landed(6, dg, c, 1)
        then(copy(10, dg, c, 1, sibling))
        landed(0, (x, y), 1 - c, None)
        landed(7, xn, 1 - c, None)
        landed(8, yn, 1 - c, None)
        landed(9, dg, 1 - c, 0)
        landed(10, dg, 1 - c, 1)
        for cp in sent:
            cp.wait_send()
        mine.wait()

    launch()
    return out[...]


HBM = pl.BlockSpec(memory_space=pltpu.HBM)
SEM = pl.BlockSpec(memory_space=pltpu.SEMAPHORE)
EFFECT = pltpu.SideEffectType.DATAFLOW_SIDE_EFFECTING


def _pair_copies(s_ref, land_ref, send_sems, recv_sems):
    x, y, c = _position()
    return [pltpu.make_async_remote_copy(
        src_ref=s_ref.at[2 * k + (1 - c)], dst_ref=land_ref.at[k], send_sem=send_sems.at[k],
        recv_sem=recv_sems.at[k], device_id=(x, y, 1 - c), device_id_type=MESH) for k in range(4)]


def _pair_start(name, stack):
    land_shape = (4,) + stack.shape[1:]

    def body(s_ref, land_ref, send_sems, recv_sems, s_thru, land_thru, token):
        for cp in _pair_copies(s_ref, land_ref, send_sems, recv_sems):
            cp.start()
        token[...] = jnp.zeros_like(token)

    return pl.pallas_call(
        body, name=name,
        out_shape=(pltpu.SemaphoreType.DMA((4,)), pltpu.SemaphoreType.DMA((4,)),
                   pltpu.HBM(stack.shape, stack.dtype), pltpu.HBM(land_shape, stack.dtype),
                   jax.ShapeDtypeStruct((8, LANES), F32)),
        in_specs=(HBM, HBM), out_specs=(SEM, SEM, HBM, HBM, pl.BlockSpec(memory_space=pltpu.VMEM)),
        input_output_aliases={0: 2, 1: 3}, compiler_params=pltpu.CompilerParams(has_side_effects=EFFECT),
    )(pltpu.with_memory_space_constraint(stack, pltpu.HBM),
      pltpu.with_memory_space_constraint(lax.empty(land_shape, stack.dtype), pltpu.HBM))


def _pair_wait(name, started, after):
    send_sems, recv_sems, s_thru, land_thru, _ = started

    def body(s_ref, land_ref, send_sems, recv_sems, after_ref, s_out, land_out):
        for cp in _pair_copies(s_ref, land_ref, send_sems, recv_sems):
            cp.wait_send()
            cp.wait_recv()

    return pl.pallas_call(
        body, name=name,
        out_shape=(pltpu.HBM(s_thru.shape, s_thru.dtype), pltpu.HBM(land_thru.shape, land_thru.dtype)),
        in_specs=(HBM, HBM, SEM, SEM, ANY), out_specs=(HBM, HBM), input_output_aliases={0: 0, 1: 1},
        compiler_params=pltpu.CompilerParams(has_side_effects=EFFECT),
    )(s_thru, land_thru, send_sems, recv_sems, after)


def _pair_sum(name, stack, other, core):
    _, r, c = stack.shape
    tr = _row_tile(r, 1024)

    def body(core_ref, a_ref, b_ref, o_ref):
        o_ref[...] = (a_ref[...].astype(F32) + b_ref[...].astype(F32)).astype(o_ref.dtype)

    grid_spec = pltpu.PrefetchScalarGridSpec(
        num_scalar_prefetch=1, grid=(4, r // tr),
        in_specs=[pl.BlockSpec((None, tr, c), lambda k, i, core_ref: (2 * k + core_ref[0], i, 0)),
                  pl.BlockSpec((None, tr, c), lambda k, i, core_ref: (k, i, 0))],
        out_specs=pl.BlockSpec((None, tr, c), lambda k, i, core_ref: (k, i, 0)))
    return pl.pallas_call(
        body, name=name, grid_spec=grid_spec, out_shape=jax.ShapeDtypeStruct((4, r, c), stack.dtype),
        compiler_params=_cparams("parallel", "parallel"),
    )(core, stack, other)


def _chip_exchange(name, sums, collective_id):
    src = jax.new_ref(sums, memory_space=pltpu.MemorySpace.HBM)
    out = jax.empty_ref(jax.ShapeDtypeStruct((3,) + sums.shape[1:], sums.dtype),
                        memory_space=pltpu.MemorySpace.HBM)

    @_sequencer(name, collective_id, (pltpu.SemaphoreType.DMA((3,)), pltpu.SemaphoreType.DMA((3,))))
    def launch(send_sems, recv_sems):
        x, y, c = _position()
        chips = [(1 - x, y), (x, 1 - y), (1 - x, 1 - y)]
        _handshake([(*chip, c) for chip in chips])
        copies = [pltpu.make_async_remote_copy(
            src_ref=src.at[2 * px + py], dst_ref=out.at[j], send_sem=send_sems.at[j],
            recv_sem=recv_sems.at[j], device_id=(px, py, c), device_id_type=MESH)
            for j, (px, py) in enumerate(chips)]
        for cp in copies:
            cp.start()
        for cp in copies:
            cp.wait()

    launch()
    return out[...]


def _small_rows(shapes):
    first, row = [], 0
    for r, c in shapes:
        first.append(row)
        row += r * (c // LANES) if c % LANES == 0 else r
        row = -(-row // 8) * 8
    return first, row


def _small_move(packed, row, ref, to_packed):
    r, c = ref.shape
    if c % LANES:
        if to_packed:
            packed[row:row + r, 0:c] = ref[...]
        else:
            ref[...] = packed[row:row + r, 0:c]
        return
    per = c // LANES
    for i in range(r):
        for j in range(per):
            at = row + i * per + j
            if to_packed:
                packed[at:at + 1, :] = ref[i:i + 1, j * LANES:(j + 1) * LANES]
            else:
                ref[i:i + 1, j * LANES:(j + 1) * LANES] = packed[at:at + 1, :]


def _small_step(name, parts, ws, ms, vs, deps=()):
    n_par = len(ws)
    first, rows = _small_rows([p.shape for p in parts])
    vm = pl.BlockSpec(memory_space=pltpu.VMEM)
    buf = pltpu.VMEM((rows, LANES), F32)

    def reduce_body(*refs):
        part_refs, (sum_out, mine, gather, send_sems, recv_sems) = refs[:n_par + 1], refs[-5:]
        x, y, c = _position()
        me = 4 * x + 2 * y + c
        mine[...] = jnp.zeros_like(mine)
        for k, ref in enumerate(part_refs):
            _small_move(mine, first[k], ref, True)
        gather[me] = mine[...]
        copies = []
        for k in range(1, N_DEV):
            peer = (x ^ (k >> 2), y ^ ((k >> 1) & 1), c ^ (k & 1))
            copies.append(pltpu.make_async_remote_copy(
                src_ref=mine, dst_ref=gather.at[me], send_sem=send_sems.at[k - 1],
                recv_sem=recv_sems.at[k - 1], device_id=peer, device_id_type=MESH))
        for cp in copies:
            cp.start()
        for k in range(1, N_DEV):
            peer_slot = 4 * (x ^ (k >> 2)) + 2 * (y ^ ((k >> 1) & 1)) + (c ^ (k & 1))
            pltpu.make_async_remote_copy(
                src_ref=mine, dst_ref=gather.at[peer_slot], send_sem=send_sems.at[k - 1],
                recv_sem=recv_sems.at[k - 1], device_id=(x, y, c), device_id_type=MESH).wait()
        acc = gather[0]
        for j in range(1, N_DEV):
            acc = acc + gather[j]
        sum_out[...] = acc

    summed = pl.pallas_call(
        reduce_body, name=name + "_reduce", in_specs=[vm] * (n_par + 1) + [ANY] * len(deps), out_specs=vm,
        out_shape=jax.ShapeDtypeStruct((rows, LANES), F32),
        scratch_shapes=[buf, pltpu.VMEM((N_DEV, rows, LANES), F32), pltpu.SemaphoreType.DMA((7,)),
                        pltpu.SemaphoreType.DMA((7,))],
    )(*parts, *deps)

    def adam_body(*refs):
        sum_ref, refs = refs[0], refs[1:]
        w_refs, m_refs, v_refs, refs = refs[:n_par], refs[n_par:2 * n_par], refs[2 * n_par:3 * n_par], refs[3 * n_par:]
        g_out, d_out, m_out, v_out = (refs[i * n_par:(i + 1) * n_par] for i in range(4))
        loss_out = refs[4 * n_par]
        w_p, m_p, v_p, d_p = refs[4 * n_par + 1:]
        for packed in (w_p, m_p, v_p):
            packed[...] = jnp.zeros_like(packed)
        for k in range(n_par):
            for packed, src in ((w_p, w_refs[k]), (m_p, m_refs[k]), (v_p, v_refs[k])):
                _small_move(packed, first[k], src, True)
        delta, m_new, v_new = _adam_math(w_p[...], sum_ref[...], m_p[...], v_p[...])
        d_p[...] = delta
        m_p[...] = m_new
        v_p[...] = v_new
        for k in range(n_par):
            for packed, dst in ((sum_ref, g_out[k]), (d_p, d_out[k]), (m_p, m_out[k]), (v_p, v_out[k])):
                _small_move(packed, first[k], dst, False)
        _small_move(sum_ref, first[n_par], loss_out, False)

    like = [jax.ShapeDtypeStruct(w.shape, F32) for w in ws]
    outs = pl.pallas_call(
        adam_body, name=name + "_adam", in_specs=[vm] * (3 * n_par + 1), out_specs=[vm] * (4 * n_par + 1),
        out_shape=like * 4 + [jax.ShapeDtypeStruct((1, LANES), F32)], scratch_shapes=[buf, buf, buf, buf],
    )(summed, *ws, *ms, *vs)
    return (outs[:n_par], outs[n_par:2 * n_par], outs[2 * n_par:3 * n_par], outs[3 * n_par:4 * n_par],
            outs[4 * n_par])


def _adam_math(w, g, m, v):
    m = ADAM_B1 * m + (1.0 - ADAM_B1) * g
    v = ADAM_B2 * v + (1.0 - ADAM_B2) * jnp.square(g)
    m_hat = m / (1.0 - ADAM_B1 ** ADAM_STEP)
    v_hat = v / (1.0 - ADAM_B2 ** ADAM_STEP)
    delta = -ADAM_LR * (m_hat / (jnp.sqrt(v_hat) + ADAM_EPS) + ADAM_WD * w)
    return delta, m, v


def _adam_shard(name, w, m, v, sums, recv, chip, deps=(), first_row=0, earlier=()):
    r, c = w.shape
    rows = sums.shape[1]
    tr = _row_tile(rows, 256)
    assert first_row % tr == 0
    skip = first_row // tr

    def body(chip_ref, w_ref, m_ref, v_ref, own_ref, r0_ref, r1_ref, r2_ref, *rest):
        g_out, d_out, m_out, v_out = rest[-4:]
        g = ((own_ref[...].astype(F32) + r0_ref[...].astype(F32)) + r1_ref[...].astype(F32)) + r2_ref[...].astype(F32)
        delta, m_new, v_new = _adam_math(w_ref[...], g, m_ref[...], v_ref[...])
        g_out[...] = g
        d_out[...] = delta
        m_out[...] = m_new
        v_out[...] = v_new

    plain = pl.BlockSpec((tr, c), lambda i, chip_ref: (skip + i, 0))
    piece = lambda j: pl.BlockSpec((None, tr, c), lambda i, chip_ref: (j, i, 0))
    grid_spec = pltpu.PrefetchScalarGridSpec(
        num_scalar_prefetch=1, grid=(rows // tr,),
        in_specs=[plain, plain, plain,
                  pl.BlockSpec((None, tr, c), lambda i, chip_ref: (chip_ref[0], i, 0)),
                  piece(0), piece(1), piece(2)] + [ANY] * (len(earlier) + len(deps)),
        out_specs=[plain] * 4)
    shape = jax.ShapeDtypeStruct((r, c), F32)
    return pl.pallas_call(
        body, name=name, grid_spec=grid_spec, out_shape=[shape] * 4, compiler_params=_cparams("parallel"),
        input_output_aliases={8 + k: k for k in range(len(earlier))},
    )(chip, w, m, v, sums, recv, recv, recv, *earlier, *deps)


def _reduce_scatter(tag, started, after, core, collective_id):
    grad_stack, other = _pair_wait("rs_pair_wait_" + tag, started, after)
    sums = _pair_sum("rs_sum_" + tag, grad_stack, other, core)
    return sums, _chip_exchange("rs_chip_" + tag, sums, collective_id)


SMALL = ("pre_norm_ffn1", "post_norm_ffn1", "pre_norm_mix", "post_norm_mix", "hgrn_lower_bounds_fwd",
         "hgrn_lower_bounds_bwd", "hgrn_out_norm", "attn_sink", "pre_norm_ffn2", "post_norm_ffn2", "rel_bias_table")
BIG = ("w_ffn1_gate_up", "w_ffn1_down", "w_mix_in", "w_mix_out", "w_ffn2_gate_up", "w_ffn2_down")
AG_ID = {n: 1 + i for i, n in enumerate(BIG)}
RS_ID = {n: 7 + i for i, n in enumerate(BIG)}
RS_ID.update(w_ffn1_gate_up_a=RS_ID["w_ffn1_gate_up"], w_ffn1_gate_up_b=13)
ORDER = ("pre_norm_ffn1", "post_norm_ffn1", "w_ffn1_gate_up", "w_ffn1_down", "pre_norm_mix", "post_norm_mix",
         "w_mix_in", "hgrn_lower_bounds_fwd", "hgrn_lower_bounds_bwd", "hgrn_out_norm", "attn_sink", "w_mix_out",
         "pre_norm_ffn2", "post_norm_ffn2", "w_ffn2_gate_up", "w_ffn2_down", "rel_bias_table")


def kernel(x, pre_norm_ffn1, post_norm_ffn1, w_ffn1_gate_up, w_ffn1_down, pre_norm_mix, post_norm_mix, w_mix_in, hgrn_lower_bounds_fwd, hgrn_lower_bounds_bwd, hgrn_out_norm, attn_sink, w_mix_out, pre_norm_ffn2, post_norm_ffn2, w_ffn2_gate_up, w_ffn2_down, rel_bias_table, loss_target, m_pre_norm_ffn1, m_post_norm_ffn1, m_w_ffn1_gate_up, m_w_ffn1_down, m_pre_norm_mix, m_post_norm_mix, m_w_mix_in, m_hgrn_lower_bounds_fwd, m_hgrn_lower_bounds_bwd, m_hgrn_out_norm, m_attn_sink, m_w_mix_out, m_pre_norm_ffn2, m_post_norm_ffn2, m_w_ffn2_gate_up, m_w_ffn2_down, m_rel_bias_table, v_pre_norm_ffn1, v_post_norm_ffn1, v_w_ffn1_gate_up, v_w_ffn1_down, v_pre_norm_mix, v_post_norm_mix, v_w_mix_in, v_hgrn_lower_bounds_fwd, v_hgrn_lower_bounds_bwd, v_hgrn_out_norm, v_attn_sink, v_w_mix_out, v_pre_norm_ffn2, v_post_norm_ffn2, v_w_ffn2_gate_up, v_w_ffn2_down, v_rel_bias_table):
    args = dict(locals())
    wts = {n: args[n] for n in ORDER}
    mom = {n: args["m_" + n] for n in ORDER}
    var = {n: args["v_" + n] for n in ORDER}

    x0 = x[0]
    target = loss_target[0]
    t, d = x0.shape
    n_hgrn = d // 2 // HEAD
    n_attn = (d - d // 2) // HEAD
    core = lax.axis_index("c").astype(jnp.int32).reshape(1)
    chip = (2 * lax.axis_index("x") + lax.axis_index("y")).astype(jnp.int32).reshape(1)

    def local(a, name):
        return jnp.transpose(a[0]) if name == "w_mix_in" else a[0]

    full = {n: _all_gather("ag_" + n, local(wts[n], n).astype(BF16), AG_ID[n]) for n in BIG}
    w_gu1, w_gu2 = full["w_ffn1_gate_up"], full["w_ffn2_gate_up"]
    w_d1 = full["w_ffn1_down"].reshape(-1, d)
    w_d2 = full["w_ffn2_down"].reshape(-1, d)
    w_out = full["w_mix_out"].reshape(-1, d)
    w_in_t = full["w_mix_in"].reshape(-1, d)

    g = {n: wts[n] for n in SMALL}
    lb_f = jax.nn.softmax(g["hgrn_lower_bounds_fwd"], axis=0)[0:1]
    lb_b = jax.nn.softmax(g["hgrn_lower_bounds_bwd"], axis=0)[0:1]
    bucket_idx = jnp.asarray(_t5_bucket_index())
    bias = _bias_build(g["rel_bias_table"], bucket_idx)

    n1 = _pre_norm("pre_norm1", x0, g["pre_norm_ffn1"])
    a1, gu1 = _ffn_up("ffn1_gate_up", n1, w_gu1)
    ff1 = _matmul("ffn1_down", a1, w_d1, mode="nn", out_dtype=F32)
    x1, h = _post_res_pre("res1", x0, ff1, g["post_norm_ffn1"], g["pre_norm_mix"], 0.5)
    p = _matmul("mix_in", h, w_in_t, mode="nt", out_dtype=F32, tm=2048, tn=512)
    y_h, o_raw = _hgrn_fwd(p, lb_f, lb_b, g["hgrn_out_norm"], n_hgrn, d)
    cat = _attn_fwd(p, bias, g["attn_sink"], y_h, n_hgrn, n_attn)
    mixed = _matmul("mix_out", cat, w_out, mode="nn", out_dtype=F32)
    x2, n2 = _post_res_pre("res2", x1, mixed, g["post_norm_mix"], g["pre_norm_ffn2"], 1.0)
    a2, gu2 = _ffn_up("ffn2_gate_up", n2, w_gu2)
    ff2 = _matmul("ffn2_down", a2, w_d2, mode="nn", out_dtype=F32)
    small_grad = {}
    dy3, loss_part, dff2, small_grad["post_norm_ffn2"] = _post_res_loss(
        "res3_loss", x2, ff2, g["post_norm_ffn2"], target, 0.5)

    scattered = {}

    pending = []

    def scatter(name, grad_stack):
        started = _pair_start("rs_pair_" + name, grad_stack)
        pending.append((name, started))
        return [started[4]]

    def settle(after, count=len(BIG)):
        deps = []
        while pending and count:
            name, started = pending.pop(0)
            scattered[name] = _reduce_scatter(name, started, after, core, RS_ID[name])
            deps.append(scattered[name][0])
            count -= 1
        return deps

    def ffn_bwd(tag, dff, a, gu, n_in, w_gu, w_d, gu_name, d_name, last):
        dep = settle(dff)

        def dw_down(deps):
            return scatter(d_name, _matmul("dw_down" + tag, a, dff, mode="tn", out_dtype=BF16, tn=2048,
                                           deps=deps).reshape(N_DEV, -1, d))

        def dw_gate_up(part, rows, deps):
            return scatter(gu_name + part, _matmul("dw_gate_up" + tag + part, n_in, dgu, mode="tn", a_cols=rows,
                                                   stack=True, halves=True, out_dtype=BF16, deps=deps))

        dep = dw_down(dep)
        dgu = _ffn_dact("d_act" + tag, dff, w_d, gu, deps=dep)
        dep = settle(dgu)
        if last:
            half = d // 2
            dep = dw_gate_up("_a", slice(0, half), dep)
            dep = dw_gate_up("_b", slice(half, d), dep)
            dep = settle(dep[0], count=1) + dep
        else:
            dep = dw_gate_up("", slice(0, d), dep)
        dn = _matmul("d_norm" + tag, dgu, w_gu, mode="nt", stack=True, halves=True, out_dtype=BF16, deps=dep)
        return dn, settle(dn)

    dn2, dep = ffn_bwd("2", dff2, a2, gu2, n2, w_gu2, w_d2, "w_ffn2_gate_up", "w_ffn2_down", last=False)
    dx2, small_grad["pre_norm_ffn2"], dmixed, small_grad["post_norm_mix"] = _pre_bwd(
        "pre_bwd2", dn2, x2, g["pre_norm_ffn2"], dy3, deps=dep, post=(mixed, g["post_norm_mix"], 1.0))
    dep = settle(dmixed)
    dcat = _matmul("d_cat", dmixed, w_out, mode="nt", out_dtype=BF16, deps=dep)
    dep = scatter("w_mix_out", _matmul("dw_mix_out", cat, dmixed, mode="tn", out_dtype=BF16).reshape(N_DEV, -1, d))
    dq_a, dk_a, dv_a, dbias, dsink_rows = _attn_bwd(p, dcat, bias, g["attn_sink"], n_hgrn, n_attn, deps=dep)
    dq_h, di_h, dzf, dzb, dg_h, dlb_f, dlb_b, small_grad["hgrn_out_norm"] = _hgrn_bwd(
        p, o_raw, dcat, lb_f, lb_b, g["hgrn_out_norm"], n_hgrn)
    dp = jnp.concatenate([dq_h, di_h, dzf, dzb, dg_h, dq_a, dk_a, dv_a], axis=1)
    dep = settle(dp)
    dh = _matmul("d_h", dp, w_in_t, mode="nn", out_dtype=BF16, tm=2048, deps=dep)
    dep = scatter("w_mix_in", _matmul("dw_mix_in", dp, h, mode="tn", out_dtype=BF16, tm=512,
                                         tn=2048).reshape(N_DEV, -1, d))
    dx1, small_grad["pre_norm_mix"], dff1, small_grad["post_norm_ffn1"] = _pre_bwd(
        "pre_bwd_mix", dh, x1, g["pre_norm_mix"], dx2, deps=dep, post=(ff1, g["post_norm_ffn1"], 0.5))

    dn1, dep = ffn_bwd("1", dff1, a1, gu1, n1, w_gu1, w_d1, "w_ffn1_gate_up", "w_ffn1_down", last=True)
    dx0, small_grad["pre_norm_ffn1"] = _pre_bwd("pre_bwd1", dn1, x0, g["pre_norm_ffn1"], dx1, deps=dep)

    def lb_grad(dlb, lb):
        da0 = dlb * lb * (1.0 - lb)
        return jnp.concatenate([da0, -da0], axis=0)

    small_grad["hgrn_lower_bounds_fwd"] = lb_grad(dlb_f, lb_f)
    small_grad["hgrn_lower_bounds_bwd"] = lb_grad(dlb_b, lb_b)
    small_grad["attn_sink"] = dsink_rows[:, :, 0].reshape(1, n_attn)
    small_grad["rel_bias_table"] = jnp.transpose(_bias_reduce(dbias, bucket_idx)[:, :, 0])

    def adam(n, tag, dep, **rows):
        sums, recv = scattered[n + tag]
        return _adam_shard("adam_" + n + tag, local(wts[n], n), local(mom[n], n), local(var[n], n), sums, recv,
                           chip, deps=dep, **rows)

    big_out = {}
    dep = []
    for n in ("w_ffn2_down", "w_ffn2_gate_up", "w_mix_out", "w_mix_in", "w_ffn1_down"):
        big_out[n] = adam(n, "", dep)
        dep = [big_out[n][0]]
    first_half = adam("w_ffn1_gate_up", "_a", dep)

    g_s, d_s, m_s, v_s, loss_row = _small_step(
        "small_step", [small_grad[n] for n in SMALL] + [loss_part], [wts[n] for n in SMALL],
        [mom[n] for n in SMALL], [var[n] for n in SMALL], deps=[first_half[0]])
    loss = loss_row[0, 0]
    grads, delta, new_m, new_v = (dict(zip(SMALL, vals)) for vals in (g_s, d_s, m_s, v_s))

    big_out["w_ffn1_gate_up"] = adam("w_ffn1_gate_up", "_b", [loss_row], first_row=d // 2, earlier=first_half)
    for n in BIG:
        grads[n], delta[n], new_m[n], new_v[n] = [local(o[None], n)[None] for o in big_out[n]]

    return (loss, dx0[None], *[grads[n] for n in ORDER], *[delta[n] for n in ORDER],
            *[new_m[n] for n in ORDER], *[new_v[n] for n in ORDER])
```

```python
import functools
import math

import numpy as np
import jax
import jax.numpy as jnp
from jax import lax
from jax.experimental import pallas as pl
from jax.experimental.pallas import tpu as pltpu
from jax.experimental.pallas import tpu_sc as plsc

F32 = jnp.float32
BF16 = jnp.bfloat16
MESH = pl.DeviceIdType.MESH

N_DEV = 8
EPS = 1e-6
NEG_INF = -1e30
HEAD = 128
CHUNK = 64
WINDOW = 128
KEY_SPAN = 3 * WINDOW
KV_HEADS = 2
REL_BUCKETS = 32
REL_MAX_DIST = 128
ADAM_LR, ADAM_B1, ADAM_B2, ADAM_EPS, ADAM_WD, ADAM_STEP = 0.001, 0.9, 0.999, 1e-08, 0.01, 10
LANES = 128
VMEM_LIMIT = 56 * 1024 * 1024
ANY = pl.BlockSpec(memory_space=pl.ANY)


def _cparams(*sem):
    return pltpu.CompilerParams(dimension_semantics=sem if sem else None, vmem_limit_bytes=VMEM_LIMIT)


def _dot(a, b):
    return jnp.dot(a, b, preferred_element_type=F32)


def _dot_nt(a, b):
    return lax.dot_general(a, b, (((1,), (1,)), ((), ())), preferred_element_type=F32)


def _dot_tn(a, b):
    return lax.dot_general(a, b, (((0,), (0,)), ((), ())), preferred_element_type=F32)


def _tile(dim, target):
    for c in (target, 1024, 512, 256, 128):
        if c <= target and dim % c == 0:
            return c
    return dim


def _row_tile(rows, target):
    fits = [c for c in range(16, min(rows, target) + 1, 16) if rows % c == 0]
    return max(fits) if fits else rows


K_WHOLE = 2048
K_STEP = 2816


def _k_tile(kd):
    if kd <= K_WHOLE:
        return kd
    return max(c for c in range(LANES, K_STEP + 1, LANES) if kd % c == 0)


def _matmul(name, a, b, *, mode, out_dtype, stack=False, halves=False, tm=1024, tn=1024, deps=(), a_cols=None):
    assert not (stack and mode == "nn")
    grp = 1
    if mode == "nn":
        m, kd = a.shape
        n = b.shape[1]
    elif mode == "nt":
        m = a.shape[-2]
        n, kd = (b.shape[1], b.shape[0] * b.shape[2]) if stack else b.shape
    else:
        kd, m = a.shape
        if a_cols:
            m = a_cols.stop - a_cols.start
        n = b.shape[-1] * (2 if halves else 1)
    if stack:
        n1 = b.shape[2] if mode == "nt" else n // N_DEV
        assert n1 % LANES == 0
        if mode == "nt":
            grp = 2 if 2 * n1 <= K_STEP else 1
            tk = grp * n1
        else:
            tn = n1
    per_half = N_DEV // 2 // grp
    tm = _tile(m, tm)
    if not (stack and mode == "tn"):
        tn = _tile(n, tn)
    if not (stack and mode == "nt"):
        tk = _k_tile(kd)
    nk = kd // tk
    lead = None if grp == 1 else grp
    b_outer = nk == 1 and b.size > a.size
    grid = (n // tn, m // tm, nk) if b_outer else (m // tm, n // tn, nk)

    def spec(shape, index):
        return pl.BlockSpec(shape, (lambda g0, g1, k: index(g1, g0, k)) if b_outer else index)

    if mode == "nn":
        a_spec = spec((tm, tk), lambda i, j, k: (i, k))
        b_spec = spec((tk, tn), lambda i, j, k: (k, j))
        dot = _dot
    elif mode == "nt":
        if halves:
            a_spec = spec((None, tm, tk), lambda i, j, k: (k // per_half, i, k % per_half))
        else:
            a_spec = spec((tm, tk), lambda i, j, k: (i, k))
        if stack:
            b_spec = spec((lead, tn, n1), lambda i, j, k: (k, j, 0))
        else:
            b_spec = spec((tn, tk), lambda i, j, k: (j, k))
        dot = _dot_nt
    else:
        first = a_cols.start // tm if a_cols else 0
        assert not a_cols or a_cols.start % tm == 0
        a_spec = spec((tk, tm), lambda i, j, k: (k, first + i))
        if halves:
            b_spec = spec((None, tk, tn), lambda i, j, k: (j // per_half, k, j % per_half))
        else:
            b_spec = spec((tk, tn), lambda i, j, k: (k, j))
        dot = _dot_tn
    if stack and mode == "tn":
        out_shape = jax.ShapeDtypeStruct((N_DEV, m, n1), out_dtype)
        o_spec = spec((None, tm, n1), lambda i, j, k: (j, i, 0))
    else:
        out_shape = jax.ShapeDtypeStruct((m, n), out_dtype)
        o_spec = spec((tm, tn), lambda i, j, k: (i, j))

    def product(a_ref, b_ref):
        bmat = jnp.concatenate([b_ref[s] for s in range(grp)], axis=1) if grp > 1 else b_ref[...]
        return dot(a_ref[...], bmat)

    def store(o_ref, val):
        o_ref[...] = val.astype(o_ref.dtype)

    def body_whole(a_ref, b_ref, *rest):
        store(rest[-1], product(a_ref, b_ref))

    def body_steps(a_ref, b_ref, *rest):
        o_ref, acc_ref = rest[-2:]
        k = pl.program_id(2)

        @pl.when(k == 0)
        def _():
            acc_ref[...] = product(a_ref, b_ref)

        @pl.when(k > 0)
        def _():
            acc_ref[...] += product(a_ref, b_ref)

        @pl.when(k == nk - 1)
        def _():
            store(o_ref, acc_ref[...])

    return pl.pallas_call(
        body_whole if nk == 1 else body_steps, name=name, grid=grid,
        in_specs=[a_spec, b_spec] + [ANY] * len(deps), out_specs=o_spec, out_shape=out_shape,
        scratch_shapes=[] if nk == 1 else [pltpu.VMEM((tm, tn), F32)],
        compiler_params=_cparams("parallel", "parallel", "arbitrary"),
    )(a, b, *deps)


def _col_parts(width, parts=2):
    groups = width // LANES
    parts = max(1, min(parts, groups // 2))
    bounds = [LANES * (groups * p // parts) for p in range(parts)] + [width]
    return [slice(bounds[p], bounds[p + 1]) for p in range(parts)]


def _ffn_up(name, n, w_stack):
    t, d = n.shape
    s, _, n1 = w_stack.shape
    half = s // 2
    tm = _tile(t, 512)

    def body(n_ref, wg_ref, wu_ref, act_ref, gu_ref):
        nv = n_ref[...]
        for cols in _col_parts(n1):
            gate = _dot(nv, wg_ref[:, cols])
            up = _dot(nv, wu_ref[:, cols])
            sg = jax.nn.sigmoid(gate)
            silu = gate * sg
            act_ref[:, cols] = (silu * up).astype(BF16)
            gu_ref[0, :, cols] = (up * (sg * (1.0 + gate * (1.0 - sg)))).astype(BF16)
            gu_ref[1, :, cols] = silu.astype(BF16)

    return pl.pallas_call(
        body, name=name, grid=(half, t // tm),
        in_specs=[pl.BlockSpec((tm, d), lambda j, i: (i, 0)),
                  pl.BlockSpec((None, d, n1), lambda j, i: (j, 0, 0)),
                  pl.BlockSpec((None, d, n1), lambda j, i: (half + j, 0, 0))],
        out_specs=[pl.BlockSpec((tm, n1), lambda j, i: (i, j)), pl.BlockSpec((2, tm, n1), lambda j, i: (0, i, j))],
        out_shape=[jax.ShapeDtypeStruct((t, half * n1), BF16), jax.ShapeDtypeStruct((2, t, half * n1), BF16)],
        compiler_params=_cparams("parallel", "parallel"),
    )(n, w_stack, w_stack)


def _ffn_dact(name, dff, w_d, gu, deps=()):
    t, d = dff.shape
    f = w_d.shape[0]
    tm = _tile(t, 1024)
    tn = _tile(f, 1408)

    def body(dff_ref, w_ref, gu_ref, *rest):
        dgu_ref = rest[-1]
        da = _dot_nt(dff_ref[...], w_ref[...]).astype(BF16)
        dgu_ref[0] = da * gu_ref[0]
        dgu_ref[1] = da * gu_ref[1]

    pair = pl.BlockSpec((2, tm, tn), lambda j, i: (0, i, j))
    return pl.pallas_call(
        body, name=name, grid=(f // tn, t // tm),
        in_specs=[pl.BlockSpec((tm, d), lambda j, i: (i, 0)), pl.BlockSpec((tn, d), lambda j, i: (j, 0)), pair]
        + [ANY] * len(deps),
        out_specs=pair, out_shape=jax.ShapeDtypeStruct((2, t, f), BF16),
        compiler_params=_cparams("parallel", "parallel"),
    )(dff, w_d, gu, *deps)


ROWS = 256


def _rstd(xf):
    return lax.rsqrt(jnp.mean(xf * xf, axis=-1, keepdims=True) + EPS)


def _row_spec(t, d):
    return pl.BlockSpec((min(ROWS, t), d), lambda i: (i, 0))


def _vec_spec(d):
    return pl.BlockSpec((1, d), lambda i: (0, 0))


def _pre_norm(name, x, gain):
    t, d = x.shape

    def body(x_ref, g_ref, n_ref):
        xf = x_ref[...]
        n_ref[...] = (xf * _rstd(xf) * g_ref[...]).astype(BF16)

    return pl.pallas_call(
        body, name=name, grid=(t // min(ROWS, t),), in_specs=[_row_spec(t, d), _vec_spec(d)],
        out_specs=_row_spec(t, d), out_shape=jax.ShapeDtypeStruct((t, d), BF16),
        compiler_params=_cparams("parallel"),
    )(x, gain)


def _post_res_pre(name, x, ff, g_post, g_next, scale):
    t, d = x.shape

    def body(x_ref, ff_ref, gp_ref, gn_ref, xo_ref, n_ref):
        ff_ = ff_ref[...]
        xn = x_ref[...] + scale * (ff_ * _rstd(ff_) * gp_ref[...])
        xo_ref[...] = xn
        n_ref[...] = (xn * _rstd(xn) * gn_ref[...]).astype(BF16)

    return pl.pallas_call(
        body, name=name, grid=(t // min(ROWS, t),),
        in_specs=[_row_spec(t, d), _row_spec(t, d), _vec_spec(d), _vec_spec(d)],
        out_specs=[_row_spec(t, d), _row_spec(t, d)],
        out_shape=[jax.ShapeDtypeStruct((t, d), F32), jax.ShapeDtypeStruct((t, d), BF16)],
        compiler_params=_cparams("parallel"),
    )(x, ff, g_post, g_next)


def _norm_bwd(xf, gain, dy):
    r = _rstd(xf)
    xh = xf * r
    dxh = dy * gain
    return r * (dxh - xh * jnp.mean(dxh * xh, axis=-1, keepdims=True)), jnp.sum(dy * xh, axis=0, keepdims=True)


def _accumulate(ref, part):
    @pl.when(pl.program_id(0) == 0)
    def _():
        ref[...] = jnp.zeros_like(ref)

    ref[...] += jnp.broadcast_to(part, ref.shape)


def _post_res_loss(name, x, ff, g_post, target, scale):
    t, d = x.shape

    def body(x_ref, ff_ref, gp_ref, tg_ref, dy_ref, loss_ref, dff_ref, dg_ref):
        ff_ = ff_ref[...]
        err = x_ref[...] + scale * (ff_ * _rstd(ff_) * gp_ref[...]) - tg_ref[...]
        dy = err / d
        dy_ref[...] = dy
        _accumulate(loss_ref, 0.5 * jnp.sum(jnp.mean(err * err, axis=-1, keepdims=True), axis=0, keepdims=True))
        dff, dg = _norm_bwd(ff_, gp_ref[...], scale * dy)
        dff_ref[...] = dff.astype(BF16)
        _accumulate(dg_ref, dg)

    return pl.pallas_call(
        body, name=name, grid=(t // min(ROWS, t),),
        in_specs=[_row_spec(t, d), _row_spec(t, d), _vec_spec(d), _row_spec(t, d)],
        out_specs=[_row_spec(t, d), _vec_spec(LANES), _row_spec(t, d), _vec_spec(d)],
        out_shape=[jax.ShapeDtypeStruct((t, d), F32), jax.ShapeDtypeStruct((1, LANES), F32),
                   jax.ShapeDtypeStruct((t, d), BF16), jax.ShapeDtypeStruct((1, d), F32)],
        compiler_params=_cparams("arbitrary"),
    )(x, ff, g_post, target)


def _pre_bwd(name, dn, x, g_pre, dy, deps=(), post=None):
    t, d = x.shape
    n_post = 2 if post else 0
    scale = post[2] if post else None

    def body(dn_ref, x_ref, g_ref, dy_ref, *rest):
        outs = rest[len(rest) - 2 - n_post:]
        dnf = dn_ref[...].astype(F32)
        dpre, dg = _norm_bwd(x_ref[...], g_ref[...], dnf)
        dx = dy_ref[...] + dpre
        outs[0][...] = dx
        _accumulate(outs[1], dg)
        if post:
            ff_ref, gp_ref = rest[0], rest[1]
            dff, dgp = _norm_bwd(ff_ref[...], gp_ref[...], scale * dx)
            outs[2][...] = dff.astype(BF16)
            _accumulate(outs[3], dgp)

    extra_in = [_row_spec(t, d), _vec_spec(d)] if post else []
    extra_out = [_row_spec(t, d), _vec_spec(d)] if post else []
    extra_shape = [jax.ShapeDtypeStruct((t, d), BF16), jax.ShapeDtypeStruct((1, d), F32)] if post else []
    return pl.pallas_call(
        body, name=name, grid=(t // min(ROWS, t),),
        in_specs=[_row_spec(t, d), _row_spec(t, d), _vec_spec(d), _row_spec(t, d)] + extra_in + [ANY] * len(deps),
        out_specs=[_row_spec(t, d), _vec_spec(d)] + extra_out,
        out_shape=[jax.ShapeDtypeStruct((t, d), F32), jax.ShapeDtypeStruct((1, d), F32)] + extra_shape,
        compiler_params=_cparams("arbitrary"),
    )(dn, x, g_pre, dy, *(post[:2] if post else ()), *deps)


def _bdot(a, b, ca, cb, precision=None):
    return lax.dot_general(a, b, (((ca,), (cb,)), ((0,), (0,))), preferred_element_type=F32, precision=precision)


def _tri_masks(g):
    row = lax.broadcasted_iota(jnp.int32, (g, CHUNK, CHUNK), 1)
    col = lax.broadcasted_iota(jnp.int32, (g, CHUNK, CHUNK), 2)
    return col <= row, col >= row


def _ones_matmul(ones_mat, val):
    hi = val.astype(BF16)
    lo = (val - hi.astype(F32)).astype(BF16)
    return _bdot(ones_mat, hi, 2, 1) + _bdot(ones_mat, lo, 2, 1)


def _hgrn_block(z, lb, q, v, cum_mat):
    sg = jax.nn.sigmoid(z)
    f = lb + (1.0 - lb) * sg
    lf = jnp.log(f)
    k = 1.0 - f
    a = _ones_matmul(cum_mat, lf)
    last = jnp.sum(lf, axis=1, keepdims=True)
    e_a = jnp.exp(a)
    e_na = jnp.exp(-a)
    e_t = jnp.exp(last - a)
    return dict(sg=sg, f=f, k=k, decay=jnp.exp(last), e_a=e_a, e_na=e_na, e_t=e_t,
                qd=q * e_a, kd=k * e_na, kt=k * e_t)


def _hgrn_states(state, kv, decay, order):
    entering = [None] * len(order)
    for g in order:
        entering[g] = state
        state = decay[g] * state + kv[g]
    return jnp.stack(entering, axis=0), state


def _hgrn_fwd(p, lb_f, lb_b, gain, n_heads, width):
    t = p.shape[0]
    w = n_heads * HEAD
    blk = min(16, t // CHUNK)
    rows_blk = blk * CHUNK
    n_blocks = t // rows_blk
    fin_rows = min(256, t)

    def body(q_ref, i_ref, zf_ref, zb_ref, g_ref, lbf_ref, lbb_ref, gain_ref, y_ref, o_ref, st_ref):
        low, up = _tri_masks(blk)
        m_low, m_up = low.astype(BF16), up.astype(BF16)
        o_ref[...] = jnp.zeros_like(o_ref)
        st_ref[...] = jnp.zeros_like(st_ref)

        def one(r0, z_ref, lb, slot, rev):
            rows = pl.ds(r0, rows_blk)
            split = lambda ref: ref[rows, :].reshape(blk, CHUNK, HEAD)
            q, v = split(q_ref), split(i_ref)
            c = _hgrn_block(split(z_ref), lb, q, v, m_up if rev else m_low)
            qd, kd, kt, vb = c["qd"].astype(BF16), c["kd"].astype(BF16), c["kt"].astype(BF16), v.astype(BF16)
            pm = jnp.where(up if rev else low, _bdot(qd, kd, 2, 2), 0.0).astype(BF16)
            kv = _bdot(vb, kt, 1, 1)
            order = range(blk - 1, -1, -1) if rev else range(blk)
            entering, st_ref[slot] = _hgrn_states(st_ref[slot], kv, c["decay"], order)
            o = _bdot(pm, vb, 2, 1) + _bdot(qd, entering.astype(BF16), 2, 2)
            o_ref[rows, :] += o.reshape(rows_blk, HEAD)

        def step(n, carry):
            one(pl.multiple_of(n * rows_blk, rows_blk), zf_ref, lbf_ref[...], 0, False)
            one(pl.multiple_of((n_blocks - 1 - n) * rows_blk, rows_blk), zb_ref, lbb_ref[...], 1, True)
            return carry

        lax.fori_loop(0, n_blocks, step, 0)

        def fin(n, carry):
            rows = pl.ds(pl.multiple_of(n * fin_rows, fin_rows), fin_rows)
            o = o_ref[rows, :]
            g = g_ref[rows, :]
            y_ref[rows, :] = (o * _rstd(o) * gain_ref[...] * (g * jax.nn.sigmoid(g))).astype(BF16)
            return carry

        lax.fori_loop(0, t // fin_rows, fin, 0)

    col = lambda grp: pl.BlockSpec((t, HEAD), lambda h: (0, grp * n_heads + h))
    vec = pl.BlockSpec((1, HEAD), lambda h: (0, h))
    out = pl.BlockSpec((t, HEAD), lambda h: (0, h))
    return pl.pallas_call(
        body, name="hgrn_fwd", grid=(n_heads,),
        in_specs=[col(0), col(1), col(2), col(3), col(4), vec, vec, vec],
        out_specs=[out, out],
        out_shape=[jax.ShapeDtypeStruct((t, width), BF16), jax.ShapeDtypeStruct((t, w), F32)],
        scratch_shapes=[pltpu.VMEM((2, HEAD, HEAD), F32)],
        compiler_params=_cparams("parallel"),
    )(p, p, p, p, p, lb_f, lb_b, gain)


def _hgrn_bwd(p, o_raw, dcat, lb_f, lb_b, gain, n_heads):
    t = p.shape[0]
    w = n_heads * HEAD
    n_chunks = t // CHUNK
    blk = min(16, n_chunks)
    rows_blk = blk * CHUNK
    n_blocks = t // rows_blk
    rb = min(256, t)

    def body(q_ref, i_ref, zf_ref, zb_ref, g_ref, o_ref, dy_ref, lbf_ref, lbb_ref, gain_ref,
             dq_ref, di_ref, dzf_ref, dzb_ref, dg_ref, dlbf_ref, dlbb_ref, dgain_ref,
             do_s, dq_s, dv_s, st_s, cur_s):
        low, up = _tri_masks(blk)
        m_low, m_up = low.astype(BF16), up.astype(BF16)
        rowid = lax.broadcasted_iota(jnp.int32, (blk, CHUNK, HEAD), 1)
        gain_v = gain_ref[...]

        def norm_bwd(n, dgain):
            rows = pl.ds(pl.multiple_of(n * rb, rb), rb)
            o = o_ref[rows, :]
            g = g_ref[rows, :]
            dy = dy_ref[rows, :].astype(F32)
            r = _rstd(o)
            oh = o * r
            sg = jax.nn.sigmoid(g)
            dg_ref[rows, :] = (dy * oh * gain_v * (sg * (1.0 + g * (1.0 - sg)))).astype(BF16)
            dno = dy * (g * sg)
            dxh = dno * gain_v
            do_s[rows, :] = r * (dxh - oh * jnp.mean(dxh * oh, axis=-1, keepdims=True))
            return dgain + jnp.sum(dno * oh, axis=0, keepdims=True)

        dgain_ref[...] = lax.fori_loop(0, t // rb, norm_bwd, jnp.zeros((1, HEAD), F32))
        def direction(z_ref, lb_ref, dz_ref, dlb_ref, rev):
            way = int(rev)
            lb = lb_ref[...]
            cum_mat = m_up if rev else m_low
            cum_mat_t = m_low if rev else m_up
            mask = up if rev else low
            last_row = 0 if rev else CHUNK - 1

            order = range(blk - 1, -1, -1) if rev else range(blk)

            def rows_of(j):
                bidx = (n_blocks - 1 - j) if rev else j
                return bidx, pl.ds(pl.multiple_of(bidx * rows_blk, rows_blk), rows_blk)

            def load(rows):
                split = lambda ref: ref[rows, :].reshape(blk, CHUNK, HEAD)
                q, v = split(q_ref), split(i_ref)
                return q, v, _hgrn_block(split(z_ref), lb, q, v, cum_mat)

            def sweep_fwd(j):
                bidx, rows = rows_of(j)
                _, v, c = load(rows)
                kv = _bdot(v.astype(BF16), c["kt"].astype(BF16), 1, 1)
                st_s[way, pl.ds(bidx * blk, blk)], cur_s[2 * way] = _hgrn_states(
                    cur_s[2 * way], kv, c["decay"], order)

            dlb_ref[...] = jnp.zeros_like(dlb_ref)

            def sweep_bwd(jj):
                bidx, rows = rows_of(n_blocks - 1 - jj)
                _, v, c = load(rows)
                st = st_s[way, pl.ds(bidx * blk, blk)]
                do = do_s[rows, :].reshape(blk, CHUNK, HEAD)
                qd, kd, kt, decay = c["qd"], c["kd"], c["kt"], c["decay"]
                qd_b, kd_b, kt_b = qd.astype(BF16), kd.astype(BF16), kt.astype(BF16)
                v_b, do_b, st_b = v.astype(BF16), do.astype(BF16), st.astype(BF16)
                pm = jnp.where(mask, _bdot(qd_b, kd_b, 2, 2), 0.0).astype(BF16)
                dpm = jnp.where(mask, _bdot(do_b, v_b, 2, 2), 0.0).astype(BF16)
                gq = _bdot(do_b, qd_b, 1, 1)
                dstate = cur_s[2 * way + 1]
                after = [None] * blk
                for g in reversed(order):
                    after[g] = dstate
                    dstate = gq[g] + decay[g] * dstate
                cur_s[2 * way + 1] = dstate
                dst = jnp.stack(after, axis=0)
                dst_b = dst.astype(BF16)
                dv = _bdot(pm, do_b, 1, 1) + _bdot(kt_b, dst_b, 2, 2)
                dqd = _bdot(dpm, kd_b, 2, 1) + _bdot(do_b, st_b, 2, 1)
                dkd = _bdot(dpm, qd_b, 1, 1)
                dkt = _bdot(v_b, dst_b, 2, 1)
                dlast = (jnp.sum(dkt * kt, axis=1, keepdims=True)
                         + decay * jnp.sum(dst * st, axis=1, keepdims=True))
                dq_s[way, rows, :] = (dqd * c["e_a"]).reshape(rows_blk, HEAD)
                dv_s[way, rows, :] = dv.reshape(rows_blk, HEAD)
                dk = dkd * c["e_na"] + dkt * c["e_t"]
                da = dqd * qd - dkd * kd - dkt * kt
                da = da + jnp.where(rowid == last_row, dlast, 0.0)
                dlf = _ones_matmul(cum_mat_t, da)
                df = dlf / c["f"] - dk
                sg = c["sg"]
                dz_ref[rows, :] = (df * (1.0 - lb) * (sg * (1.0 - sg))).reshape(rows_blk, HEAD).astype(BF16)
                dlb_ref[...] += jnp.sum((df * (1.0 - sg)).reshape(rows_blk, HEAD), axis=0, keepdims=True)

            return sweep_fwd, sweep_bwd

        ways = [direction(zf_ref, lbf_ref, dzf_ref, dlbf_ref, False),
                direction(zb_ref, lbb_ref, dzb_ref, dlbb_ref, True)]
        cur_s[...] = jnp.zeros_like(cur_s)
        for sweep in range(2):
            def both(j, carry):
                for way in ways:
                    way[sweep](j)
                return carry

            lax.fori_loop(0, n_blocks, both, 0)
        dq_ref[...] = (dq_s[0] + dq_s[1]).astype(BF16)
        di_ref[...] = (dv_s[0] + dv_s[1]).astype(BF16)

    col = lambda grp: pl.BlockSpec((t, HEAD), lambda h: (0, grp * n_heads + h))
    one = pl.BlockSpec((t, HEAD), lambda h: (0, h))
    vec = pl.BlockSpec((1, HEAD), lambda h: (0, h))
    big = jax.ShapeDtypeStruct((t, w), BF16)
    small = jax.ShapeDtypeStruct((1, w), F32)
    return pl.pallas_call(
        body, name="hgrn_bwd", grid=(n_heads,),
        in_specs=[col(0), col(1), col(2), col(3), col(4), one, one, vec, vec, vec],
        out_specs=[one] * 5 + [vec] * 3,
        out_shape=[big] * 5 + [small] * 3,
        scratch_shapes=[pltpu.VMEM((t, HEAD), F32), pltpu.VMEM((2, t, HEAD), F32), pltpu.VMEM((2, t, HEAD), F32),
                        pltpu.VMEM((2, n_chunks, HEAD, HEAD), F32), pltpu.VMEM((4, HEAD, HEAD), F32)],
        compiler_params=_cparams("parallel"),
    )(p, p, p, p, p, o_raw, dcat, lb_f, lb_b, gain)


def _t5_bucket_index():
    c = np.arange(WINDOW)[:, None]
    s = np.arange(KEY_SPAN)[None, :]
    rel = s - WINDOW - c
    nb = REL_BUCKETS // 2
    max_exact = nb // 2
    bucket = (rel > 0).astype(np.int32) * nb
    n = np.abs(rel)
    large = max_exact + (np.log(np.maximum(n, 1) / max_exact) / np.log(REL_MAX_DIST / max_exact)
                         * (nb - max_exact)).astype(np.int32)
    large = np.minimum(large, nb - 1)
    return bucket + np.where(n < max_exact, n, large).astype(np.int32)


def _bias_build(table, idx):
    n_attn = table.shape[1]

    def body(tab_ref, idx_ref, o_ref):
        h = pl.program_id(0)
        idx_v = idx_ref[...]
        acc = jnp.zeros((WINDOW, KEY_SPAN), F32)
        for b in range(REL_BUCKETS):
            acc = jnp.where(idx_v == b, tab_ref[b, h], acc)
        o_ref[...] = acc

    return pl.pallas_call(
        body, name="bias_build", grid=(n_attn,),
        in_specs=[pl.BlockSpec(memory_space=pltpu.SMEM), pl.BlockSpec((WINDOW, KEY_SPAN), lambda h: (0, 0))],
        out_specs=pl.BlockSpec((None, WINDOW, KEY_SPAN), lambda h: (h, 0, 0)),
        out_shape=jax.ShapeDtypeStruct((n_attn, WINDOW, KEY_SPAN), F32), compiler_params=_cparams("parallel"),
    )(table, idx)


def _bias_reduce(dbias, idx):
    n_attn = dbias.shape[0]

    def body(idx_ref, d_ref, o_ref):
        idx_v = idx_ref[...]
        dv = d_ref[...]
        rows = lax.broadcasted_iota(jnp.int32, (REL_BUCKETS, LANES), 0)
        acc = jnp.zeros((REL_BUCKETS, LANES), F32)
        for b in range(REL_BUCKETS):
            part = jnp.sum(jnp.where(idx_v == b, dv, 0.0), axis=1, keepdims=True)
            acc = jnp.where(rows == b, jnp.sum(part, axis=0, keepdims=True), acc)
        o_ref[...] = acc

    return pl.pallas_call(
        body, name="bias_reduce", grid=(n_attn,),
        in_specs=[pl.BlockSpec((WINDOW, KEY_SPAN), lambda h: (0, 0)),
                  pl.BlockSpec((None, WINDOW, KEY_SPAN), lambda h: (h, 0, 0))],
        out_specs=pl.BlockSpec((None, REL_BUCKETS, LANES), lambda h: (h, 0, 0)),
        out_shape=jax.ShapeDtypeStruct((n_attn, REL_BUCKETS, LANES), F32), compiler_params=_cparams("parallel"),
    )(idx, dbias)


def _attn_probs(q, kb, bias, sink, valid):
    s = _dot_nt(q, kb) / math.sqrt(HEAD) + bias
    s = jnp.where(valid, s, NEG_INF)
    m = jnp.maximum(jnp.max(s, axis=-1, keepdims=True), sink)
    e = jnp.exp(s - m)
    e_sink = jnp.exp(sink - m)
    den = jnp.sum(e, axis=-1, keepdims=True) + e_sink
    return e / den, e_sink / den


def _attn_valid(n, t, grp):
    c = lax.broadcasted_iota(jnp.int32, (grp * WINDOW, KEY_SPAN), 0) & (WINDOW - 1)
    s = lax.broadcasted_iota(jnp.int32, (grp * WINDOW, KEY_SPAN), 1)
    rel = s - WINDOW - c
    key_pos = n * WINDOW - WINDOW + s
    return (jnp.abs(rel) <= WINDOW) & (key_pos >= 0) & (key_pos < t)


def _stack_heads(ref, x, grp):
    heads = range(x * grp, (x + 1) * grp)
    return jnp.concatenate([ref[:, h * HEAD:(h + 1) * HEAD] for h in heads], axis=0).astype(BF16)


def _key_band(prev, cur, nxt, x):
    cols = slice(x * HEAD, (x + 1) * HEAD)
    return jnp.concatenate([prev[:, cols], cur[:, cols], nxt[:, cols]], axis=0).astype(BF16)


def _sink_column(sink_ref, x, grp):
    return jnp.concatenate([jnp.full((WINDOW, 1), sink_ref[0, x * grp + g], F32) for g in range(grp)], axis=0)


def _attn_specs(t, n_hgrn, n_attn):
    grp = n_attn // KV_HEADS
    nb = t // WINDOW
    cq = 5 * n_hgrn
    ck = cq + n_attn
    cv = ck + KV_HEADS
    assert cq % n_attn == 0 and ck % KV_HEADS == 0 and cv % KV_HEADS == 0 and n_hgrn % n_attn == 0
    q_spec = pl.BlockSpec((WINDOW, n_attn * HEAD), lambda n: (n, cq // n_attn))
    kv = lambda base, off: pl.BlockSpec(
        (WINDOW, KV_HEADS * HEAD), lambda n: (jnp.clip(n + off, 0, nb - 1), base // KV_HEADS))
    band = [kv(ck, -1), kv(ck, 0), kv(ck, 1), kv(cv, -1), kv(cv, 0), kv(cv, 1)]
    bias_spec = pl.BlockSpec((n_attn, WINDOW, KEY_SPAN), lambda n: (0, 0, 0))
    sink_spec = pl.BlockSpec(memory_space=pltpu.SMEM)
    attn_cols = pl.BlockSpec((WINDOW, n_attn * HEAD), lambda n: (n, n_hgrn // n_attn))
    return grp, nb, q_spec, band, bias_spec, sink_spec, attn_cols


def _attn_fwd(p, bias, sink, cat, n_hgrn, n_attn):
    t = p.shape[0]
    grp, nb, q_spec, band, bias_spec, sink_spec, attn_cols = _attn_specs(t, n_hgrn, n_attn)

    def body(q_ref, kp, kc, kn, vp, vc, vn, bias_ref, sink_ref, cat_ref, y_ref):
        n = pl.program_id(0)
        valid = _attn_valid(n, t, grp)
        for x in range(KV_HEADS):
            kb, vb = _key_band(kp, kc, kn, x), _key_band(vp, vc, vn, x)
            pr, _ = _attn_probs(_stack_heads(q_ref, x, grp), kb,
                                bias_ref[x * grp:(x + 1) * grp].reshape(grp * WINDOW, KEY_SPAN),
                                _sink_column(sink_ref, x, grp), valid)
            y = _dot(pr.astype(BF16), vb).astype(BF16)
            for g in range(grp):
                h = x * grp + g
                y_ref[:, h * HEAD:(h + 1) * HEAD] = y[g * WINDOW:(g + 1) * WINDOW]

    return pl.pallas_call(
        body, name="attn_fwd", grid=(nb,),
        in_specs=[q_spec] + band + [bias_spec, sink_spec, ANY], out_specs=attn_cols,
        out_shape=jax.ShapeDtypeStruct(cat.shape, BF16), input_output_aliases={9: 0},
        compiler_params=_cparams("parallel"),
    )(p, p, p, p, p, p, p, bias, sink, cat)


def _attn_bwd(p, dcat, bias, sink, n_hgrn, n_attn, deps=()):
    t = p.shape[0]
    grp, nb, q_spec, band, bias_spec, sink_spec, attn_cols = _attn_specs(t, n_hgrn, n_attn)
    inv = 1.0 / math.sqrt(HEAD)

    def body(q_ref, kp, kc, kn, vp, vc, vn, bias_ref, sink_ref, do_ref, *rest):
        dq_ref, dk_ref, dv_ref, dbias_ref, dsink_ref, dk_s, dv_s = rest[-7:]
        n = pl.program_id(0)

        @pl.when(n == 0)
        def _():
            dk_s[...] = jnp.zeros_like(dk_s)
            dv_s[...] = jnp.zeros_like(dv_s)
            dbias_ref[...] = jnp.zeros_like(dbias_ref)
            dsink_ref[...] = jnp.zeros_like(dsink_ref)

        valid = _attn_valid(n, t, grp)
        rows = pl.ds(pl.multiple_of(n * WINDOW, WINDOW), KEY_SPAN)
        for x in range(KV_HEADS):
            kb, vb = _key_band(kp, kc, kn, x), _key_band(vp, vc, vn, x)
            q = _stack_heads(q_ref, x, grp)
            do = _stack_heads(do_ref, x, grp)
            pr, p_sink = _attn_probs(q, kb, bias_ref[x * grp:(x + 1) * grp].reshape(grp * WINDOW, KEY_SPAN),
                                     _sink_column(sink_ref, x, grp), valid)
            dpr = _dot_nt(do, vb)
            delta = jnp.sum(pr * dpr, axis=-1, keepdims=True)
            ds = pr * (dpr - delta)
            ds_b = ds.astype(BF16)
            dq = (_dot(ds_b, kb) * inv).astype(BF16)
            dsink = -p_sink * delta
            for g in range(grp):
                h = x * grp + g
                head = slice(g * WINDOW, (g + 1) * WINDOW)
                dq_ref[:, h * HEAD:(h + 1) * HEAD] = dq[head]
                dbias_ref[h] += ds[head]
                dsink_ref[x, g:g + 1, :] += jnp.broadcast_to(jnp.sum(dsink[head], axis=0, keepdims=True),
                                                             (1, WINDOW))
            cols = slice(x * HEAD, (x + 1) * HEAD)
            dk_s[rows, cols] += _dot_tn(ds_b, q) * inv
            dv_s[rows, cols] += _dot_tn(pr.astype(BF16), do)

        @pl.when(n == nb - 1)
        def _():
            dk_ref[...] = dk_s[pl.ds(WINDOW, t), :].astype(BF16)
            dv_ref[...] = dv_s[pl.ds(WINDOW, t), :].astype(BF16)

    whole = lambda *shape: pl.BlockSpec(shape, lambda n: (0,) * len(shape))
    return pl.pallas_call(
        body, name="attn_bwd", grid=(nb,),
        in_specs=[q_spec] + band + [bias_spec, sink_spec, attn_cols] + [ANY] * len(deps),
        out_specs=[pl.BlockSpec((WINDOW, n_attn * HEAD), lambda n: (n, 0)), whole(t, KV_HEADS * HEAD),
                   whole(t, KV_HEADS * HEAD), bias_spec, whole(KV_HEADS, grp, WINDOW)],
        out_shape=[jax.ShapeDtypeStruct((t, n_attn * HEAD), BF16),
                   jax.ShapeDtypeStruct((t, KV_HEADS * HEAD), BF16),
                   jax.ShapeDtypeStruct((t, KV_HEADS * HEAD), BF16),
                   jax.ShapeDtypeStruct((n_attn, WINDOW, KEY_SPAN), F32),
                   jax.ShapeDtypeStruct((KV_HEADS, grp, WINDOW), F32)],
        scratch_shapes=[pltpu.VMEM((t + 2 * WINDOW, KV_HEADS * HEAD), F32),
                        pltpu.VMEM((t + 2 * WINDOW, KV_HEADS * HEAD), F32)],
        compiler_params=_cparams("arbitrary"),
    )(p, p, p, p, p, p, p, bias, sink, dcat, *deps)


def _position():
    return lax.axis_index("x"), lax.axis_index("y"), lax.axis_index("c")


def _handshake(peers):
    barrier = pltpu.get_barrier_semaphore()
    for peer in peers:
        pl.semaphore_signal(barrier, inc=1, device_id=peer, device_id_type=MESH)
    pl.semaphore_wait(barrier, len(peers))


def _sequencer(name, collective_id, scratch_types):
    return functools.partial(
        pl.kernel, mesh=plsc.ScalarSubcoreMesh(axis_name="sc", num_cores=1), name=name,
        scratch_types=scratch_types, compiler_params=pltpu.CompilerParams(collective_id=collective_id))


def _all_gather(name, shard, collective_id):
    rows = shard.shape[0]
    assert rows % 2 == 0
    rh = rows // 2
    src = jax.new_ref(shard, memory_space=pltpu.MemorySpace.HBM)
    out = jax.empty_ref(jax.ShapeDtypeStruct((N_DEV,) + shard.shape, shard.dtype),
                        memory_space=pltpu.MemorySpace.HBM)
    n_copies = 11

    @_sequencer(name, collective_id, (pltpu.SemaphoreType.DMA((n_copies,)), pltpu.SemaphoreType.DMA((n_copies,)),
                                      pltpu.SemaphoreType.DMA))
    def launch(send_sems, recv_sems, local_sem):
        x, y, c = _position()
        sibling = (x, y, 1 - c)
        xn, yn, dg = (1 - x, y), (x, 1 - y), (1 - x, 1 - y)
        _handshake([sibling, (*xn, c), (*yn, c)])

        def part(ref, half):
            return ref if half is None else ref.at[pl.ds(half * rh, rh)]

        def slot(chip, core, half=None):
            return part(out.at[4 * chip[0] + 2 * chip[1] + core], half)

        def copy(k, chip, core, half, to, own=False):
            return pltpu.make_async_remote_copy(
                src_ref=part(src, half) if own else slot(chip, core, half), dst_ref=slot(chip, core, half),
                send_sem=send_sems.at[k], recv_sem=recv_sems.at[k], device_id=to, device_id_type=MESH)

        def landed(k, chip, core, half):
            copy(k, chip, core, half, (x, y, c)).wait_recv()

        mine = pltpu.make_async_copy(src, slot((x, y), c), local_sem)
        mine.start()
        sent = [copy(0, (x, y), c, None, sibling, own=True),
                copy(1, (x, y), c, 0, (*xn, c), own=True), copy(3, (x, y), c, 1, (*yn, c), own=True),
                copy(2, (x, y), c, 1, (*xn, c), own=True), copy(4, (x, y), c, 0, (*yn, c), own=True)]
        for cp in sent:
            cp.start()

        def then(cp):
            cp.start()
            sent.append(cp)

        landed(1, xn, c, 0)
        then(copy(5, xn, c, 0, (*yn, c)))
        landed(3, yn, c, 1)
        then(copy(6, yn, c, 1, (*xn, c)))
        landed(2, xn, c, 1)
        then(copy(7, xn, c, None, sibling))
        landed(4, yn, c, 0)
        then(copy(8, yn, c, None, sibling))
        landed(5, dg, c, 0)
        then(copy(9, dg, c, 0, sibling))
        landed(6, dg, c, 1)
        then(copy(10, dg, c, 1, sibling))
        landed(0, (x, y), 1 - c, None)
        landed(7, xn, 1 - c, None)
        landed(8, yn, 1 - c, None)
        landed(9, dg, 1 - c, 0)
        landed(10, dg, 1 - c, 1)
        for cp in sent:
            cp.wait_send()
        mine.wait()

    launch()
    return out[...]


HBM = pl.BlockSpec(memory_space=pltpu.HBM)
SEM = pl.BlockSpec(memory_space=pltpu.SEMAPHORE)
EFFECT = pltpu.SideEffectType.DATAFLOW_SIDE_EFFECTING


def _pair_copies(s_ref, land_ref, send_sems, recv_sems):
    x, y, c = _position()
    return [pltpu.make_async_remote_copy(
        src_ref=s_ref.at[2 * k + (1 - c)], dst_ref=land_ref.at[k], send_sem=send_sems.at[k],
        recv_sem=recv_sems.at[k], device_id=(x, y, 1 - c), device_id_type=MESH) for k in range(4)]


def _pair_start(name, stack):
    land_shape = (4,) + stack.shape[1:]

    def body(s_ref, land_ref, send_sems, recv_sems, s_thru, land_thru, token):
        for cp in _pair_copies(s_ref, land_ref, send_sems, recv_sems):
            cp.start()
        token[...] = jnp.zeros_like(token)

    return pl.pallas_call(
        body, name=name,
        out_shape=(pltpu.SemaphoreType.DMA((4,)), pltpu.SemaphoreType.DMA((4,)),
                   pltpu.HBM(stack.shape, stack.dtype), pltpu.HBM(land_shape, stack.dtype),
                   jax.ShapeDtypeStruct((8, LANES), F32)),
        in_specs=(HBM, HBM), out_specs=(SEM, SEM, HBM, HBM, pl.BlockSpec(memory_space=pltpu.VMEM)),
        input_output_aliases={0: 2, 1: 3}, compiler_params=pltpu.CompilerParams(has_side_effects=EFFECT),
    )(pltpu.with_memory_space_constraint(stack, pltpu.HBM),
      pltpu.with_memory_space_constraint(lax.empty(land_shape, stack.dtype), pltpu.HBM))


def _pair_wait(name, started, after):
    send_sems, recv_sems, s_thru, land_thru, _ = started

    def body(s_ref, land_ref, send_sems, recv_sems, after_ref, s_out, land_out):
        for cp in _pair_copies(s_ref, land_ref, send_sems, recv_sems):
            cp.wait_send()
            cp.wait_recv()

    return pl.pallas_call(
        body, name=name,
        out_shape=(pltpu.HBM(s_thru.shape, s_thru.dtype), pltpu.HBM(land_thru.shape, land_thru.dtype)),
        in_specs=(HBM, HBM, SEM, SEM, ANY), out_specs=(HBM, HBM), input_output_aliases={0: 0, 1: 1},
        compiler_params=pltpu.CompilerParams(has_side_effects=EFFECT),
    )(s_thru, land_thru, send_sems, recv_sems, after)


def _pair_sum(name, stack, other, core):
    _, r, c = stack.shape
    tr = _row_tile(r, 1024)

    def body(core_ref, a_ref, b_ref, o_ref):
        o_ref[...] = (a_ref[...].astype(F32) + b_ref[...].astype(F32)).astype(o_ref.dtype)

    grid_spec = pltpu.PrefetchScalarGridSpec(
        num_scalar_prefetch=1, grid=(4, r // tr),
        in_specs=[pl.BlockSpec((None, tr, c), lambda k, i, core_ref: (2 * k + core_ref[0], i, 0)),
                  pl.BlockSpec((None, tr, c), lambda k, i, core_ref: (k, i, 0))],
        out_specs=pl.BlockSpec((None, tr, c), lambda k, i, core_ref: (k, i, 0)))
    return pl.pallas_call(
        body, name=name, grid_spec=grid_spec, out_shape=jax.ShapeDtypeStruct((4, r, c), stack.dtype),
        compiler_params=_cparams("parallel", "parallel"),
    )(core, stack, other)


def _chip_exchange(name, sums, collective_id):
    src = jax.new_ref(sums, memory_space=pltpu.MemorySpace.HBM)
    out = jax.empty_ref(jax.ShapeDtypeStruct((3,) + sums.shape[1:], sums.dtype),
                        memory_space=pltpu.MemorySpace.HBM)

    @_sequencer(name, collective_id, (pltpu.SemaphoreType.DMA((3,)), pltpu.SemaphoreType.DMA((3,))))
    def launch(send_sems, recv_sems):
        x, y, c = _position()
        chips = [(1 - x, y), (x, 1 - y), (1 - x, 1 - y)]
        _handshake([(*chip, c) for chip in chips])
        copies = [pltpu.make_async_remote_copy(
            src_ref=src.at[2 * px + py], dst_ref=out.at[j], send_sem=send_sems.at[j],
            recv_sem=recv_sems.at[j], device_id=(px, py, c), device_id_type=MESH)
            for j, (px, py) in enumerate(chips)]
        for cp in copies:
            cp.start()
        for cp in copies:
            cp.wait()

    launch()
    return out[...]


def _small_rows(shapes):
    first, row = [], 0
    for r, c in shapes:
        first.append(row)
        row += r * (c // LANES) if c % LANES == 0 else r
        row = -(-row // 8) * 8
    return first, row


def _small_move(packed, row, ref, to_packed):
    r, c = ref.shape
    if c % LANES:
        if to_packed:
            packed[row:row + r, 0:c] = ref[...]
        else:
            ref[...] = packed[row:row + r, 0:c]
        return
    per = c // LANES
    for i in range(r):
        for j in range(per):
            at = row + i * per + j
            if to_packed:
                packed[at:at + 1, :] = ref[i:i + 1, j * LANES:(j + 1) * LANES]
            else:
                ref[i:i + 1, j * LANES:(j + 1) * LANES] = packed[at:at + 1, :]


def _small_step(name, parts, ws, ms, vs, deps=()):
    n_par = len(ws)
    first, rows = _small_rows([p.shape for p in parts])
    vm = pl.BlockSpec(memory_space=pltpu.VMEM)
    buf = pltpu.VMEM((rows, LANES), F32)

    def reduce_body(*refs):
        part_refs, (sum_out, mine, gather, send_sems, recv_sems) = refs[:n_par + 1], refs[-5:]
        x, y, c = _position()
        me = 4 * x + 2 * y + c
        mine[...] = jnp.zeros_like(mine)
        for k, ref in enumerate(part_refs):
            _small_move(mine, first[k], ref, True)
        gather[me] = mine[...]
        copies = []
        for k in range(1, N_DEV):
            peer = (x ^ (k >> 2), y ^ ((k >> 1) & 1), c ^ (k & 1))
            copies.append(pltpu.make_async_remote_copy(
                src_ref=mine, dst_ref=gather.at[me], send_sem=send_sems.at[k - 1],
                recv_sem=recv_sems.at[k - 1], device_id=peer, device_id_type=MESH))
        for cp in copies:
            cp.start()
        for k in range(1, N_DEV):
            peer_slot = 4 * (x ^ (k >> 2)) + 2 * (y ^ ((k >> 1) & 1)) + (c ^ (k & 1))
            pltpu.make_async_remote_copy(
                src_ref=mine, dst_ref=gather.at[peer_slot], send_sem=send_sems.at[k - 1],
                recv_sem=recv_sems.at[k - 1], device_id=(x, y, c), device_id_type=MESH).wait()
        acc = gather[0]
        for j in range(1, N_DEV):
            acc = acc + gather[j]
        sum_out[...] = acc

    summed = pl.pallas_call(
        reduce_body, name=name + "_reduce", in_specs=[vm] * (n_par + 1) + [ANY] * len(deps), out_specs=vm,
        out_shape=jax.ShapeDtypeStruct((rows, LANES), F32),
        scratch_shapes=[buf, pltpu.VMEM((N_DEV, rows, LANES), F32), pltpu.SemaphoreType.DMA((7,)),
                        pltpu.SemaphoreType.DMA((7,))],
    )(*parts, *deps)

    def adam_body(*refs):
        sum_ref, refs = refs[0], refs[1:]
        w_refs, m_refs, v_refs, refs = refs[:n_par], refs[n_par:2 * n_par], refs[2 * n_par:3 * n_par], refs[3 * n_par:]
        g_out, d_out, m_out, v_out = (refs[i * n_par:(i + 1) * n_par] for i in range(4))
        loss_out = refs[4 * n_par]
        w_p, m_p, v_p, d_p = refs[4 * n_par + 1:]
        for packed in (w_p, m_p, v_p):
            packed[...] = jnp.zeros_like(packed)
        for k in range(n_par):
            for packed, src in ((w_p, w_refs[k]), (m_p, m_refs[k]), (v_p, v_refs[k])):
                _small_move(packed, first[k], src, True)
        delta, m_new, v_new = _adam_math(w_p[...], sum_ref[...], m_p[...], v_p[...])
        d_p[...] = delta
        m_p[...] = m_new
        v_p[...] = v_new
        for k in range(n_par):
            for packed, dst in ((sum_ref, g_out[k]), (d_p, d_out[k]), (m_p, m_out[k]), (v_p, v_out[k])):
                _small_move(packed, first[k], dst, False)
        _small_move(sum_ref, first[n_par], loss_out, False)

    like = [jax.ShapeDtypeStruct(w.shape, F32) for w in ws]
    outs = pl.pallas_call(
        adam_body, name=name + "_adam", in_specs=[vm] * (3 * n_par + 1), out_specs=[vm] * (4 * n_par + 1),
        out_shape=like * 4 + [jax.ShapeDtypeStruct((1, LANES), F32)], scratch_shapes=[buf, buf, buf, buf],
    )(summed, *ws, *ms, *vs)
    return (outs[:n_par], outs[n_par:2 * n_par], outs[2 * n_par:3 * n_par], outs[3 * n_par:4 * n_par],
            outs[4 * n_par])


def _adam_math(w, g, m, v):
    m = ADAM_B1 * m + (1.0 - ADAM_B1) * g
    v = ADAM_B2 * v + (1.0 - ADAM_B2) * jnp.square(g)
    m_hat = m / (1.0 - ADAM_B1 ** ADAM_STEP)
    v_hat = v / (1.0 - ADAM_B2 ** ADAM_STEP)
    delta = -ADAM_LR * (m_hat / (jnp.sqrt(v_hat) + ADAM_EPS) + ADAM_WD * w)
    return delta, m, v


def _adam_shard(name, w, m, v, sums, recv, chip, deps=(), first_row=0, earlier=()):
    r, c = w.shape
    rows = sums.shape[1]
    tr = _row_tile(rows, 256)
    assert first_row % tr == 0
    skip = first_row // tr

    def body(chip_ref, w_ref, m_ref, v_ref, own_ref, r0_ref, r1_ref, r2_ref, *rest):
        g_out, d_out, m_out, v_out = rest[-4:]
        g = ((own_ref[...].astype(F32) + r0_ref[...].astype(F32)) + r1_ref[...].astype(F32)) + r2_ref[...].astype(F32)
        delta, m_new, v_new = _adam_math(w_ref[...], g, m_ref[...], v_ref[...])
        g_out[...] = g
        d_out[...] = delta
        m_out[...] = m_new
        v_out[...] = v_new

    plain = pl.BlockSpec((tr, c), lambda i, chip_ref: (skip + i, 0))
    piece = lambda j: pl.BlockSpec((None, tr, c), lambda i, chip_ref: (j, i, 0))
    grid_spec = pltpu.PrefetchScalarGridSpec(
        num_scalar_prefetch=1, grid=(rows // tr,),
        in_specs=[plain, plain, plain,
                  pl.BlockSpec((None, tr, c), lambda i, chip_ref: (chip_ref[0], i, 0)),
                  piece(0), piece(1), piece(2)] + [ANY] * (len(earlier) + len(deps)),
        out_specs=[plain] * 4)
    shape = jax.ShapeDtypeStruct((r, c), F32)
    return pl.pallas_call(
        body, name=name, grid_spec=grid_spec, out_shape=[shape] * 4, compiler_params=_cparams("parallel"),
        input_output_aliases={8 + k: k for k in range(len(earlier))},
    )(chip, w, m, v, sums, recv, recv, recv, *earlier, *deps)


def _reduce_scatter(tag, started, after, core, collective_id):
    grad_stack, other = _pair_wait("rs_pair_wait_" + tag, started, after)
    sums = _pair_sum("rs_sum_" + tag, grad_stack, other, core)
    return sums, _chip_exchange("rs_chip_" + tag, sums, collective_id)


SMALL = ("pre_norm_ffn1", "post_norm_ffn1", "pre_norm_mix", "post_norm_mix", "hgrn_lower_bounds_fwd",
         "hgrn_lower_bounds_bwd", "hgrn_out_norm", "attn_sink", "pre_norm_ffn2", "post_norm_ffn2", "rel_bias_table")
BIG = ("w_ffn1_gate_up", "w_ffn1_down", "w_mix_in", "w_mix_out", "w_ffn2_gate_up", "w_ffn2_down")
AG_ID = {n: 1 + i for i, n in enumerate(BIG)}
RS_ID = {n: 7 + i for i, n in enumerate(BIG)}
RS_ID.update(w_ffn1_gate_up_a=RS_ID["w_ffn1_gate_up"], w_ffn1_gate_up_b=13)
ORDER = ("pre_norm_ffn1", "post_norm_ffn1", "w_ffn1_gate_up", "w_ffn1_down", "pre_norm_mix", "post_norm_mix",
         "w_mix_in", "hgrn_lower_bounds_fwd", "hgrn_lower_bounds_bwd", "hgrn_out_norm", "attn_sink", "w_mix_out",
         "pre_norm_ffn2", "post_norm_ffn2", "w_ffn2_gate_up", "w_ffn2_down", "rel_bias_table")


def kernel(x, pre_norm_ffn1, post_norm_ffn1, w_ffn1_gate_up, w_ffn1_down, pre_norm_mix, post_norm_mix, w_mix_in, hgrn_lower_bounds_fwd, hgrn_lower_bounds_bwd, hgrn_out_norm, attn_sink, w_mix_out, pre_norm_ffn2, post_norm_ffn2, w_ffn2_gate_up, w_ffn2_down, rel_bias_table, loss_target, m_pre_norm_ffn1, m_post_norm_ffn1, m_w_ffn1_gate_up, m_w_ffn1_down, m_pre_norm_mix, m_post_norm_mix, m_w_mix_in, m_hgrn_lower_bounds_fwd, m_hgrn_lower_bounds_bwd, m_hgrn_out_norm, m_attn_sink, m_w_mix_out, m_pre_norm_ffn2, m_post_norm_ffn2, m_w_ffn2_gate_up, m_w_ffn2_down, m_rel_bias_table, v_pre_norm_ffn1, v_post_norm_ffn1, v_w_ffn1_gate_up, v_w_ffn1_down, v_pre_norm_mix, v_post_norm_mix, v_w_mix_in, v_hgrn_lower_bounds_fwd, v_hgrn_lower_bounds_bwd, v_hgrn_out_norm, v_attn_sink, v_w_mix_out, v_pre_norm_ffn2, v_post_norm_ffn2, v_w_ffn2_gate_up, v_w_ffn2_down, v_rel_bias_table):
    args = dict(locals())
    wts = {n: args[n] for n in ORDER}
    mom = {n: args["m_" + n] for n in ORDER}
    var = {n: args["v_" + n] for n in ORDER}

    x0 = x[0]
    target = loss_target[0]
    t, d = x0.shape
    n_hgrn = d // 2 // HEAD
    n_attn = (d - d // 2) // HEAD
    core = lax.axis_index("c").astype(jnp.int32).reshape(1)
    chip = (2 * lax.axis_index("x") + lax.axis_index("y")).astype(jnp.int32).reshape(1)

    def local(a, name):
        return jnp.transpose(a[0]) if name == "w_mix_in" else a[0]

    full = {n: _all_gather("ag_" + n, local(wts[n], n).astype(BF16), AG_ID[n]) for n in BIG}
    w_gu1, w_gu2 = full["w_ffn1_gate_up"], full["w_ffn2_gate_up"]
    w_d1 = full["w_ffn1_down"].reshape(-1, d)
    w_d2 = full["w_ffn2_down"].reshape(-1, d)
    w_out = full["w_mix_out"].reshape(-1, d)
    w_in_t = full["w_mix_in"].reshape(-1, d)

    g = {n: wts[n] for n in SMALL}
    lb_f = jax.nn.softmax(g["hgrn_lower_bounds_fwd"], axis=0)[0:1]
    lb_b = jax.nn.softmax(g["hgrn_lower_bounds_bwd"], axis=0)[0:1]
    bucket_idx = jnp.asarray(_t5_bucket_index())
    bias = _bias_build(g["rel_bias_table"], bucket_idx)

    n1 = _pre_norm("pre_norm1", x0, g["pre_norm_ffn1"])
    a1, gu1 = _ffn_up("ffn1_gate_up", n1, w_gu1)
    ff1 = _matmul("ffn1_down", a1, w_d1, mode="nn", out_dtype=F32)
    x1, h = _post_res_pre("res1", x0, ff1, g["post_norm_ffn1"], g["pre_norm_mix"], 0.5)
    p = _matmul("mix_in", h, w_in_t, mode="nt", out_dtype=F32, tm=2048, tn=512)
    y_h, o_raw = _hgrn_fwd(p, lb_f, lb_b, g["hgrn_out_norm"], n_hgrn, d)
    cat = _attn_fwd(p, bias, g["attn_sink"], y_h, n_hgrn, n_attn)
    mixed = _matmul("mix_out", cat, w_out, mode="nn", out_dtype=F32)
    x2, n2 = _post_res_pre("res2", x1, mixed, g["post_norm_mix"], g["pre_norm_ffn2"], 1.0)
    a2, gu2 = _ffn_up("ffn2_gate_up", n2, w_gu2)
    ff2 = _matmul("ffn2_down", a2, w_d2, mode="nn", out_dtype=F32)
    small_grad = {}
    dy3, loss_part, dff2, small_grad["post_norm_ffn2"] = _post_res_loss(
        "res3_loss", x2, ff2, g["post_norm_ffn2"], target, 0.5)

    scattered = {}

    pending = []

    def scatter(name, grad_stack):
        started = _pair_start("rs_pair_" + name, grad_stack)
        pending.append((name, started))
        return [started[4]]

    def settle(after, count=len(BIG)):
        deps = []
        while pending and count:
            name, started = pending.pop(0)
            scattered[name] = _reduce_scatter(name, started, after, core, RS_ID[name])
            deps.append(scattered[name][0])
            count -= 1
        return deps

    def ffn_bwd(tag, dff, a, gu, n_in, w_gu, w_d, gu_name, d_name, last):
        dep = settle(dff)

        def dw_down(deps):
            return scatter(d_name, _matmul("dw_down" + tag, a, dff, mode="tn", out_dtype=BF16, tn=2048,
                                           deps=deps).reshape(N_DEV, -1, d))

        def dw_gate_up(part, rows, deps):
            return scatter(gu_name + part, _matmul("dw_gate_up" + tag + part, n_in, dgu, mode="tn", a_cols=rows,
                                                   stack=True, halves=True, out_dtype=BF16, deps=deps))

        dep = dw_down(dep)
        dgu = _ffn_dact("d_act" + tag, dff, w_d, gu, deps=dep)
        dep = settle(dgu)
        if last:
            half = d // 2
            dep = dw_gate_up("_a", slice(0, half), dep)
            dep = dw_gate_up("_b", slice(half, d), dep)
            dep = settle(dep[0], count=1) + dep
        else:
            dep = dw_gate_up("", slice(0, d), dep)
        dn = _matmul("d_norm" + tag, dgu, w_gu, mode="nt", stack=True, halves=True, out_dtype=BF16, deps=dep)
        return dn, settle(dn)

    dn2, dep = ffn_bwd("2", dff2, a2, gu2, n2, w_gu2, w_d2, "w_ffn2_gate_up", "w_ffn2_down", last=False)
    dx2, small_grad["pre_norm_ffn2"], dmixed, small_grad["post_norm_mix"] = _pre_bwd(
        "pre_bwd2", dn2, x2, g["pre_norm_ffn2"], dy3, deps=dep, post=(mixed, g["post_norm_mix"], 1.0))
    dep = settle(dmixed)
    dcat = _matmul("d_cat", dmixed, w_out, mode="nt", out_dtype=BF16, deps=dep)
    dep = scatter("w_mix_out", _matmul("dw_mix_out", cat, dmixed, mode="tn", out_dtype=BF16).reshape(N_DEV, -1, d))
    dq_a, dk_a, dv_a, dbias, dsink_rows = _attn_bwd(p, dcat, bias, g["attn_sink"], n_hgrn, n_attn, deps=dep)
    dq_h, di_h, dzf, dzb, dg_h, dlb_f, dlb_b, small_grad["hgrn_out_norm"] = _hgrn_bwd(
        p, o_raw, dcat, lb_f, lb_b, g["hgrn_out_norm"], n_hgrn)
    dp = jnp.concatenate([dq_h, di_h, dzf, dzb, dg_h, dq_a, dk_a, dv_a], axis=1)
    dep = settle(dp)
    dh = _matmul("d_h", dp, w_in_t, mode="nn", out_dtype=BF16, tm=2048, deps=dep)
    dep = scatter("w_mix_in", _matmul("dw_mix_in", dp, h, mode="tn", out_dtype=BF16, tm=512,
                                         tn=2048).reshape(N_DEV, -1, d))
    dx1, small_grad["pre_norm_mix"], dff1, small_grad["post_norm_ffn1"] = _pre_bwd(
        "pre_bwd_mix", dh, x1, g["pre_norm_mix"], dx2, deps=dep, post=(ff1, g["post_norm_ffn1"], 0.5))

    dn1, dep = ffn_bwd("1", dff1, a1, gu1, n1, w_gu1, w_d1, "w_ffn1_gate_up", "w_ffn1_down", last=True)
    dx0, small_grad["pre_norm_ffn1"] = _pre_bwd("pre_bwd1", dn1, x0, g["pre_norm_ffn1"], dx1, deps=dep)

    def lb_grad(dlb, lb):
        da0 = dlb * lb * (1.0 - lb)
        return jnp.concatenate([da0, -da0], axis=0)

    small_grad["hgrn_lower_bounds_fwd"] = lb_grad(dlb_f, lb_f)
    small_grad["hgrn_lower_bounds_bwd"] = lb_grad(dlb_b, lb_b)
    small_grad["attn_sink"] = dsink_rows[:, :, 0].reshape(1, n_attn)
    small_grad["rel_bias_table"] = jnp.transpose(_bias_reduce(dbias, bucket_idx)[:, :, 0])

    def adam(n, tag, dep, **rows):
        sums, recv = scattered[n + tag]
        return _adam_shard("adam_" + n + tag, local(wts[n], n), local(mom[n], n), local(var[n], n), sums, recv,
                           chip, deps=dep, **rows)

    big_out = {}
    dep = []
    for n in ("w_ffn2_down", "w_ffn2_gate_up", "w_mix_out", "w_mix_in", "w_ffn1_down"):
        big_out[n] = adam(n, "", dep)
        dep = [big_out[n][0]]
    first_half = adam("w_ffn1_gate_up", "_a", dep)

    g_s, d_s, m_s, v_s, loss_row = _small_step(
        "small_step", [small_grad[n] for n in SMALL] + [loss_part], [wts[n] for n in SMALL],
        [mom[n] for n in SMALL], [var[n] for n in SMALL], deps=[first_half[0]])
    loss = loss_row[0, 0]
    grads, delta, new_m, new_v = (dict(zip(SMALL, vals)) for vals in (g_s, d_s, m_s, v_s))

    big_out["w_ffn1_gate_up"] = adam("w_ffn1_gate_up", "_b", [loss_row], first_row=d // 2, earlier=first_half)
    for n in BIG:
        grads[n], delta[n], new_m[n], new_v[n] = [local(o[None], n)[None] for o in big_out[n]]

    return (loss, dx0[None], *[grads[n] for n in ORDER], *[delta[n] for n in ORDER],
            *[new_m[n] for n in ORDER], *[new_v[n] for n in ORDER])
```

```python
import functools
import math

import numpy as np
import jax
import jax.numpy as jnp
from jax import lax
from jax.experimental import pallas as pl
from jax.experimental.pallas import tpu as pltpu
from jax.experimental.pallas import tpu_sc as plsc

F32 = jnp.float32
BF16 = jnp.bfloat16
MESH = pl.DeviceIdType.MESH

N_DEV = 8
EPS = 1e-6
NEG_INF = -1e30
HEAD = 128
CHUNK = 64
WINDOW = 128
KEY_SPAN = 3 * WINDOW
KV_HEADS = 2
REL_BUCKETS = 32
REL_MAX_DIST = 128
ADAM_LR, ADAM_B1, ADAM_B2, ADAM_EPS, ADAM_WD, ADAM_STEP = 0.001, 0.9, 0.999, 1e-08, 0.01, 10
LANES = 128
VMEM_LIMIT = 56 * 1024 * 1024
ANY = pl.BlockSpec(memory_space=pl.ANY)


def _cparams(*sem):
    return pltpu.CompilerParams(dimension_semantics=sem if sem else None, vmem_limit_bytes=VMEM_LIMIT)


def _dot(a, b):
    return jnp.dot(a, b, preferred_element_type=F32)


def _dot_nt(a, b):
    return lax.dot_general(a, b, (((1,), (1,)), ((), ())), preferred_element_type=F32)


def _dot_tn(a, b):
    return lax.dot_general(a, b, (((0,), (0,)), ((), ())), preferred_element_type=F32)


def _tile(dim, target):
    for c in (target, 1024, 512, 256, 128):
        if c <= target and dim % c == 0:
            return c
    return dim


def _row_tile(rows, target):
    fits = [c for c in range(16, min(rows, target) + 1, 16) if rows % c == 0]
    return max(fits) if fits else rows


K_WHOLE = 2048
K_STEP = 2816


def _k_tile(kd):
    if kd <= K_WHOLE:
        return kd
    return max(c for c in range(LANES, K_STEP + 1, LANES) if kd % c == 0)


def _matmul(name, a, b, *, mode, out_dtype, stack=False, halves=False, tm=1024, tn=1024, deps=(), a_cols=None):
    assert not (stack and mode == "nn")
    grp = 1
    if mode == "nn":
        m, kd = a.shape
        n = b.shape[1]
    elif mode == "nt":
        m = a.shape[-2]
        n, kd = (b.shape[1], b.shape[0] * b.shape[2]) if stack else b.shape
    else:
        kd, m = a.shape
        if a_cols:
            m = a_cols.stop - a_cols.start
        n = b.shape[-1] * (2 if halves else 1)
    if stack:
        n1 = b.shape[2] if mode == "nt" else n // N_DEV
        assert n1 % LANES == 0
        if mode == "nt":
            grp = 2 if 2 * n1 <= K_STEP else 1
            tk = grp * n1
        else:
            tn = n1
    per_half = N_DEV // 2 // grp
    tm = _tile(m, tm)
    if not (stack and mode == "tn"):
        tn = _tile(n, tn)
    if not (stack and mode == "nt"):
        tk = _k_tile(kd)
    nk = kd // tk
    lead = None if grp == 1 else grp
    b_outer = nk == 1 and b.size > a.size
    grid = (n // tn, m // tm, nk) if b_outer else (m // tm, n // tn, nk)

    def spec(shape, index):
        return pl.BlockSpec(shape, (lambda g0, g1, k: index(g1, g0, k)) if b_outer else index)

    if mode == "nn":
        a_spec = spec((tm, tk), lambda i, j, k: (i, k))
        b_spec = spec((tk, tn), lambda i, j, k: (k, j))
        dot = _dot
    elif mode == "nt":
        if halves:
            a_spec = spec((None, tm, tk), lambda i, j, k: (k // per_half, i, k % per_half))
        else:
            a_spec = spec((tm, tk), lambda i, j, k: (i, k))
        if stack:
            b_spec = spec((lead, tn, n1), lambda i, j, k: (k, j, 0))
        else:
            b_spec = spec((tn, tk), lambda i, j, k: (j, k))
        dot = _dot_nt
    else:
        first = a_cols.start // tm if a_cols else 0
        assert not a_cols or a_cols.start % tm == 0
        a_spec = spec((tk, tm), lambda i, j, k: (k, first + i))
        if halves:
            b_spec = spec((None, tk, tn), lambda i, j, k: (j // per_half, k, j % per_half))
        else:
            b_spec = spec((tk, tn), lambda i, j, k: (k, j))
        dot = _dot_tn
    if stack and mode == "tn":
        out_shape = jax.ShapeDtypeStruct((N_DEV, m, n1), out_dtype)
        o_spec = spec((None, tm, n1), lambda i, j, k: (j, i, 0))
    else:
        out_shape = jax.ShapeDtypeStruct((m, n), out_dtype)
        o_spec = spec((tm, tn), lambda i, j, k: (i, j))

    def product(a_ref, b_ref):
        bmat = jnp.concatenate([b_ref[s] for s in range(grp)], axis=1) if grp > 1 else b_ref[...]
        return dot(a_ref[...], bmat)

    def store(o_ref, val):
        o_ref[...] = val.astype(o_ref.dtype)

    def body_whole(a_ref, b_ref, *rest):
        store(rest[-1], product(a_ref, b_ref))

    def body_steps(a_ref, b_ref, *rest):
        o_ref, acc_ref = rest[-2:]
        k = pl.program_id(2)

        @pl.when(k == 0)
        def _():
            acc_ref[...] = product(a_ref, b_ref)

        @pl.when(k > 0)
        def _():
            acc_ref[...] += product(a_ref, b_ref)

        @pl.when(k == nk - 1)
        def _():
            store(o_ref, acc_ref[...])

    return pl.pallas_call(
        body_whole if nk == 1 else body_steps, name=name, grid=grid,
        in_specs=[a_spec, b_spec] + [ANY] * len(deps), out_specs=o_spec, out_shape=out_shape,
        scratch_shapes=[] if nk == 1 else [pltpu.VMEM((tm, tn), F32)],
        compiler_params=_cparams("parallel", "parallel", "arbitrary"),
    )(a, b, *deps)


def _col_parts(width, parts=2):
    groups = width // LANES
    parts = max(1, min(parts, groups // 2))
    bounds = [LANES * (groups * p // parts) for p in range(parts)] + [width]
    return [slice(bounds[p], bounds[p + 1]) for p in range(parts)]


def _ffn_up(name, n, w_stack):
    t, d = n.shape
    s, _, n1 = w_stack.shape
    half = s // 2
    tm = _tile(t, 512)

    def body(n_ref, wg_ref, wu_ref, act_ref, gu_ref):
        nv = n_ref[...]
        for cols in _col_parts(n1):
            gate = _dot(nv, wg_ref[:, cols])
            up = _dot(nv, wu_ref[:, cols])
            sg = jax.nn.sigmoid(gate)
            silu = gate * sg
            act_ref[:, cols] = (silu * up).astype(BF16)
            gu_ref[0, :, cols] = (up * (sg * (1.0 + gate * (1.0 - sg)))).astype(BF16)
            gu_ref[1, :, cols] = silu.astype(BF16)

    return pl.pallas_call(
        body, name=name, grid=(half, t // tm),
        in_specs=[pl.BlockSpec((tm, d), lambda j, i: (i, 0)),
                  pl.BlockSpec((None, d, n1), lambda j, i: (j, 0, 0)),
                  pl.BlockSpec((None, d, n1), lambda j, i: (half + j, 0, 0))],
        out_specs=[pl.BlockSpec((tm, n1), lambda j, i: (i, j)), pl.BlockSpec((2, tm, n1), lambda j, i: (0, i, j))],
        out_shape=[jax.ShapeDtypeStruct((t, half * n1), BF16), jax.ShapeDtypeStruct((2, t, half * n1), BF16)],
        compiler_params=_cparams("parallel", "parallel"),
    )(n, w_stack, w_stack)


def _ffn_dact(name, dff, w_d, gu, deps=()):
    t, d = dff.shape
    f = w_d.shape[0]
    tm = _tile(t, 1024)
    tn = _tile(f, 1408)

    def body(dff_ref, w_ref, gu_ref, *rest):
        dgu_ref = rest[-1]
        da = _dot_nt(dff_ref[...], w_ref[...]).astype(BF16)
        dgu_ref[0] = da * gu_ref[0]
        dgu_ref[1] = da * gu_ref[1]

    pair = pl.BlockSpec((2, tm, tn), lambda j, i: (0, i, j))
    return pl.pallas_call(
        body, name=name, grid=(f // tn, t // tm),
        in_specs=[pl.BlockSpec((tm, d), lambda j, i: (i, 0)), pl.BlockSpec((tn, d), lambda j, i: (j, 0)), pair]
        + [ANY] * len(deps),
        out_specs=pair, out_shape=jax.ShapeDtypeStruct((2, t, f), BF16),
        compiler_params=_cparams("parallel", "parallel"),
    )(dff, w_d, gu, *deps)


ROWS = 256


def _rstd(xf):
    return lax.rsqrt(jnp.mean(xf * xf, axis=-1, keepdims=True) + EPS)


def _row_spec(t, d):
    return pl.BlockSpec((min(ROWS, t), d), lambda i: (i, 0))


def _vec_spec(d):
    return pl.BlockSpec((1, d), lambda i: (0, 0))


def _pre_norm(name, x, gain):
    t, d = x.shape

    def body(x_ref, g_ref, n_ref):
        xf = x_ref[...]
        n_ref[...] = (xf * _rstd(xf) * g_ref[...]).astype(BF16)

    return pl.pallas_call(
        body, name=name, grid=(t // min(ROWS, t),), in_specs=[_row_spec(t, d), _vec_spec(d)],
        out_specs=_row_spec(t, d), out_shape=jax.ShapeDtypeStruct((t, d), BF16),
        compiler_params=_cparams("parallel"),
    )(x, gain)


def _post_res_pre(name, x, ff, g_post, g_next, scale):
    t, d = x.shape

    def body(x_ref, ff_ref, gp_ref, gn_ref, xo_ref, n_ref):
        ff_ = ff_ref[...]
        xn = x_ref[...] + scale * (ff_ * _rstd(ff_) * gp_ref[...])
        xo_ref[...] = xn
        n_ref[...] = (xn * _rstd(xn) * gn_ref[...]).astype(BF16)

    return pl.pallas_call(
        body, name=name, grid=(t // min(ROWS, t),),
        in_specs=[_row_spec(t, d), _row_spec(t, d), _vec_spec(d), _vec_spec(d)],
        out_specs=[_row_spec(t, d), _row_spec(t, d)],
        out_shape=[jax.ShapeDtypeStruct((t, d), F32), jax.ShapeDtypeStruct((t, d), BF16)],
        compiler_params=_cparams("parallel"),
    )(x, ff, g_post, g_next)


def _norm_bwd(xf, gain, dy):
    r = _rstd(xf)
    xh = xf * r
    dxh = dy * gain
    return r * (dxh - xh * jnp.mean(dxh * xh, axis=-1, keepdims=True)), jnp.sum(dy * xh, axis=0, keepdims=True)


def _accumulate(ref, part):
    @pl.when(pl.program_id(0) == 0)
    def _():
        ref[...] = jnp.zeros_like(ref)

    ref[...] += jnp.broadcast_to(part, ref.shape)


def _post_res_loss(name, x, ff, g_post, target, scale):
    t, d = x.shape

    def body(x_ref, ff_ref, gp_ref, tg_ref, dy_ref, loss_ref, dff_ref, dg_ref):
        ff_ = ff_ref[...]
        err = x_ref[...] + scale * (ff_ * _rstd(ff_) * gp_ref[...]) - tg_ref[...]
        dy = err / d
        dy_ref[...] = dy
        _accumulate(loss_ref, 0.5 * jnp.sum(jnp.mean(err * err, axis=-1, keepdims=True), axis=0, keepdims=True))
        dff, dg = _norm_bwd(ff_, gp_ref[...], scale * dy)
        dff_ref[...] = dff.astype(BF16)
        _accumulate(dg_ref, dg)

    return pl.pallas_call(
        body, name=name, grid=(t // min(ROWS, t),),
        in_specs=[_row_spec(t, d), _row_spec(t, d), _vec_spec(d), _row_spec(t, d)],
        out_specs=[_row_spec(t, d), _vec_spec(LANES), _row_spec(t, d), _vec_spec(d)],
        out_shape=[jax.ShapeDtypeStruct((t, d), F32), jax.ShapeDtypeStruct((1, LANES), F32),
                   jax.ShapeDtypeStruct((t, d), BF16), jax.ShapeDtypeStruct((1, d), F32)],
        compiler_params=_cparams("arbitrary"),
    )(x, ff, g_post, target)


def _pre_bwd(name, dn, x, g_pre, dy, deps=(), post=None):
    t, d = x.shape
    n_post = 2 if post else 0
    scale = post[2] if post else None

    def body(dn_ref, x_ref, g_ref, dy_ref, *rest):
        outs = rest[len(rest) - 2 - n_post:]
        dnf = dn_ref[...].astype(F32)
        dpre, dg = _norm_bwd(x_ref[...], g_ref[...], dnf)
        dx = dy_ref[...] + dpre
        outs[0][...] = dx
        _accumulate(outs[1], dg)
        if post:
            ff_ref, gp_ref = rest[0], rest[1]
            dff, dgp = _norm_bwd(ff_ref[...], gp_ref[...], scale * dx)
            outs[2][...] = dff.astype(BF16)
            _accumulate(outs[3], dgp)

    extra_in = [_row_spec(t, d), _vec_spec(d)] if post else []
    extra_out = [_row_spec(t, d), _vec_spec(d)] if post else []
    extra_shape = [jax.ShapeDtypeStruct((t, d), BF16), jax.ShapeDtypeStruct((1, d), F32)] if post else []
    return pl.pallas_call(
        body, name=name, grid=(t // min(ROWS, t),),
        in_specs=[_row_spec(t, d), _row_spec(t, d), _vec_spec(d), _row_spec(t, d)] + extra_in + [ANY] * len(deps),
        out_specs=[_row_spec(t, d), _vec_spec(d)] + extra_out,
        out_shape=[jax.ShapeDtypeStruct((t, d), F32), jax.ShapeDtypeStruct((1, d), F32)] + extra_shape,
        compiler_params=_cparams("arbitrary"),
    )(dn, x, g_pre, dy, *(post[:2] if post else ()), *deps)


def _bdot(a, b, ca, cb, precision=None):
    return lax.dot_general(a, b, (((ca,), (cb,)), ((0,), (0,))), preferred_element_type=F32, precision=precision)


def _tri_masks(g):
    row = lax.broadcasted_iota(jnp.int32, (g, CHUNK, CHUNK), 1)
    col = lax.broadcasted_iota(jnp.int32, (g, CHUNK, CHUNK), 2)
    return col <= row, col >= row


def _ones_matmul(ones_mat, val):
    hi = val.astype(BF16)
    lo = (val - hi.astype(F32)).astype(BF16)
    return _bdot(ones_mat, hi, 2, 1) + _bdot(ones_mat, lo, 2, 1)


def _hgrn_block(z, lb, q, v, cum_mat):
    sg = jax.nn.sigmoid(z)
    f = lb + (1.0 - lb) * sg
    lf = jnp.log(f)
    k = 1.0 - f
    a = _ones_matmul(cum_mat, lf)
    last = jnp.sum(lf, axis=1, keepdims=True)
    e_a = jnp.exp(a)
    e_na = jnp.exp(-a)
    e_t = jnp.exp(last - a)
    return dict(sg=sg, f=f, k=k, decay=jnp.exp(last), e_a=e_a, e_na=e_na, e_t=e_t,
                qd=q * e_a, kd=k * e_na, kt=k * e_t)


def _hgrn_states(state, kv, decay, order):
    entering = [None] * len(order)
    for g in order:
        entering[g] = state
        state = decay[g] * state + kv[g]
    return jnp.stack(entering, axis=0), state


def _hgrn_fwd(p, lb_f, lb_b, gain, n_heads, width):
    t = p.shape[0]
    w = n_heads * HEAD
    blk = min(32, t // CHUNK)
    rows_blk = blk * CHUNK
    n_blocks = t // rows_blk
    fin_rows = min(256, t)

    def body(q_ref, i_ref, zf_ref, zb_ref, g_ref, lbf_ref, lbb_ref, gain_ref, y_ref, o_ref, st_ref):
        low, up = _tri_masks(blk)
        m_low, m_up = low.astype(BF16), up.astype(BF16)
        o_ref[...] = jnp.zeros_like(o_ref)
        st_ref[...] = jnp.zeros_like(st_ref)

        def one(r0, z_ref, lb, slot, rev):
            rows = pl.ds(r0, rows_blk)
            split = lambda ref: ref[rows, :].reshape(blk, CHUNK, HEAD)
            q, v = split(q_ref), split(i_ref)
            c = _hgrn_block(split(z_ref), lb, q, v, m_up if rev else m_low)
            qd, kd, kt, vb = c["qd"].astype(BF16), c["kd"].astype(BF16), c["kt"].astype(BF16), v.astype(BF16)
            pm = jnp.where(up if rev else low, _bdot(qd, kd, 2, 2), 0.0).astype(BF16)
            kv = _bdot(vb, kt, 1, 1)
            order = range(blk - 1, -1, -1) if rev else range(blk)
            entering, st_ref[slot] = _hgrn_states(st_ref[slot], kv, c["decay"], order)
            o = _bdot(pm, vb, 2, 1) + _bdot(qd, entering.astype(BF16), 2, 2)
            o_ref[rows, :] += o.reshape(rows_blk, HEAD)

        def step(n, carry):
            one(pl.multiple_of(n * rows_blk, rows_blk), zf_ref, lbf_ref[...], 0, False)
            one(pl.multiple_of((n_blocks - 1 - n) * rows_blk, rows_blk), zb_ref, lbb_ref[...], 1, True)
            return carry

        lax.fori_loop(0, n_blocks, step, 0)

        def fin(n, carry):
            rows = pl.ds(pl.multiple_of(n * fin_rows, fin_rows), fin_rows)
            o = o_ref[rows, :]
            g = g_ref[rows, :]
            y_ref[rows, :] = (o * _rstd(o) * gain_ref[...] * (g * jax.nn.sigmoid(g))).astype(BF16)
            return carry

        lax.fori_loop(0, t // fin_rows, fin, 0)

    col = lambda grp: pl.BlockSpec((t, HEAD), lambda h: (0, grp * n_heads + h))
    vec = pl.BlockSpec((1, HEAD), lambda h: (0, h))
    out = pl.BlockSpec((t, HEAD), lambda h: (0, h))
    return pl.pallas_call(
        body, name="hgrn_fwd", grid=(n_heads,),
        in_specs=[col(0), col(1), col(2), col(3), col(4), vec, vec, vec],
        out_specs=[out, out],
        out_shape=[jax.ShapeDtypeStruct((t, width), BF16), jax.ShapeDtypeStruct((t, w), F32)],
        scratch_shapes=[pltpu.VMEM((2, HEAD, HEAD), F32)],
        compiler_params=_cparams("parallel"),
    )(p, p, p, p, p, lb_f, lb_b, gain)


def _hgrn_bwd(p, o_raw, dcat, lb_f, lb_b, gain, n_heads):
    t = p.shape[0]
    w = n_heads * HEAD
    n_chunks = t // CHUNK
    blk = min(16, n_chunks)
    rows_blk = blk * CHUNK
    n_blocks = t // rows_blk
    rb = min(256, t)

    def body(q_ref, i_ref, zf_ref, zb_ref, g_ref, o_ref, dy_ref, lbf_ref, lbb_ref, gain_ref,
             dq_ref, di_ref, dzf_ref, dzb_ref, dg_ref, dlbf_ref, dlbb_ref, dgain_ref,
             do_s, dq_s, dv_s, st_s, cur_s):
        low, up = _tri_masks(blk)
        m_low, m_up = low.astype(BF16), up.astype(BF16)
        rowid = lax.broadcasted_iota(jnp.int32, (blk, CHUNK, HEAD), 1)
        gain_v = gain_ref[...]

        def norm_bwd(n, dgain):
            rows = pl.ds(pl.multiple_of(n * rb, rb), rb)
            o = o_ref[rows, :]
            g = g_ref[rows, :]
            dy = dy_ref[rows, :].astype(F32)
            r = _rstd(o)
            oh = o * r
            sg = jax.nn.sigmoid(g)
            dg_ref[rows, :] = (dy * oh * gain_v * (sg * (1.0 + g * (1.0 - sg)))).astype(BF16)
            dno = dy * (g * sg)
            dxh = dno * gain_v
            do_s[rows, :] = r * (dxh - oh * jnp.mean(dxh * oh, axis=-1, keepdims=True))
            return dgain + jnp.sum(dno * oh, axis=0, keepdims=True)

        dgain_ref[...] = lax.fori_loop(0, t // rb, norm_bwd, jnp.zeros((1, HEAD), F32))
        def direction(z_ref, lb_ref, dz_ref, dlb_ref, rev):
            way = int(rev)
            lb = lb_ref[...]
            cum_mat = m_up if rev else m_low
            cum_mat_t = m_low if rev else m_up
            mask = up if rev else low
            last_row = 0 if rev else CHUNK - 1

            order = range(blk - 1, -1, -1) if rev else range(blk)

            def rows_of(j):
                bidx = (n_blocks - 1 - j) if rev else j
                return bidx, pl.ds(pl.multiple_of(bidx * rows_blk, rows_blk), rows_blk)

            def load(rows):
                split = lambda ref: ref[rows, :].reshape(blk, CHUNK, HEAD)
                q, v = split(q_ref), split(i_ref)
                return q, v, _hgrn_block(split(z_ref), lb, q, v, cum_mat)

            def sweep_fwd(j):
                bidx, rows = rows_of(j)
                _, v, c = load(rows)
                kv = _bdot(v.astype(BF16), c["kt"].astype(BF16), 1, 1)
                st_s[way, pl.ds(bidx * blk, blk)], cur_s[2 * way] = _hgrn_states(
                    cur_s[2 * way], kv, c["decay"], order)

            dlb_ref[...] = jnp.zeros_like(dlb_ref)

            def sweep_bwd(jj):
                bidx, rows = rows_of(n_blocks - 1 - jj)
                _, v, c = load(rows)
                st = st_s[way, pl.ds(bidx * blk, blk)]
                do = do_s[rows, :].reshape(blk, CHUNK, HEAD)
                qd, kd, kt, decay = c["qd"], c["kd"], c["kt"], c["decay"]
                qd_b, kd_b, kt_b = qd.astype(BF16), kd.astype(BF16), kt.astype(BF16)
                v_b, do_b, st_b = v.astype(BF16), do.astype(BF16), st.astype(BF16)
                pm = jnp.where(mask, _bdot(qd_b, kd_b, 2, 2), 0.0).astype(BF16)
                dpm = jnp.where(mask, _bdot(do_b, v_b, 2, 2), 0.0).astype(BF16)
                gq = _bdot(do_b, qd_b, 1, 1)
                dstate = cur_s[2 * way + 1]
                after = [None] * blk
                for g in reversed(order):
                    after[g] = dstate
                    dstate = gq[g] + decay[g] * dstate
                cur_s[2 * way + 1] = dstate
                dst = jnp.stack(after, axis=0)
                dst_b = dst.astype(BF16)
                dv = _bdot(pm, do_b, 1, 1) + _bdot(kt_b, dst_b, 2, 2)
                dqd = _bdot(dpm, kd_b, 2, 1) + _bdot(do_b, st_b, 2, 1)
                dkd = _bdot(dpm, qd_b, 1, 1)
                dkt = _bdot(v_b, dst_b, 2, 1)
                dlast = (jnp.sum(dkt * kt, axis=1, keepdims=True)
                         + decay * jnp.sum(dst * st, axis=1, keepdims=True))
                dq_s[way, rows, :] = (dqd * c["e_a"]).reshape(rows_blk, HEAD)
                dv_s[way, rows, :] = dv.reshape(rows_blk, HEAD)
                dk = dkd * c["e_na"] + dkt * c["e_t"]
                da = dqd * qd - dkd * kd - dkt * kt
                da = da + jnp.where(rowid == last_row, dlast, 0.0)
                dlf = _ones_matmul(cum_mat_t, da)
                df = dlf / c["f"] - dk
                sg = c["sg"]
                dz_ref[rows, :] = (df * (1.0 - lb) * (sg * (1.0 - sg))).reshape(rows_blk, HEAD).astype(BF16)
                dlb_ref[...] += jnp.sum((df * (1.0 - sg)).reshape(rows_blk, HEAD), axis=0, keepdims=True)

            return sweep_fwd, sweep_bwd

        ways = [direction(zf_ref, lbf_ref, dzf_ref, dlbf_ref, False),
                direction(zb_ref, lbb_ref, dzb_ref, dlbb_ref, True)]
        cur_s[...] = jnp.zeros_like(cur_s)
        for sweep in range(2):
            def both(j, carry):
                for way in ways:
                    way[sweep](j)
                return carry

            lax.fori_loop(0, n_blocks, both, 0)
        dq_ref[...] = (dq_s[0] + dq_s[1]).astype(BF16)
        di_ref[...] = (dv_s[0] + dv_s[1]).astype(BF16)

    col = lambda grp: pl.BlockSpec((t, HEAD), lambda h: (0, grp * n_heads + h))
    one = pl.BlockSpec((t, HEAD), lambda h: (0, h))
    vec = pl.BlockSpec((1, HEAD), lambda h: (0, h))
    big = jax.ShapeDtypeStruct((t, w), BF16)
    small = jax.ShapeDtypeStruct((1, w), F32)
    return pl.pallas_call(
        body, name="hgrn_bwd", grid=(n_heads,),
        in_specs=[col(0), col(1), col(2), col(3), col(4), one, one, vec, vec, vec],
        out_specs=[one] * 5 + [vec] * 3,
        out_shape=[big] * 5 + [small] * 3,
        scratch_shapes=[pltpu.VMEM((t, HEAD), F32), pltpu.VMEM((2, t, HEAD), F32), pltpu.VMEM((2, t, HEAD), F32),
                        pltpu.VMEM((2, n_chunks, HEAD, HEAD), F32), pltpu.VMEM((4, HEAD, HEAD), F32)],
        compiler_params=_cparams("parallel"),
    )(p, p, p, p, p, o_raw, dcat, lb_f, lb_b, gain)


def _t5_bucket_index():
    c = np.arange(WINDOW)[:, None]
    s = np.arange(KEY_SPAN)[None, :]
    rel = s - WINDOW - c
    nb = REL_BUCKETS // 2
    max_exact = nb // 2
    bucket = (rel > 0).astype(np.int32) * nb
    n = np.abs(rel)
    large = max_exact + (np.log(np.maximum(n, 1) / max_exact) / np.log(REL_MAX_DIST / max_exact)
                         * (nb - max_exact)).astype(np.int32)
    large = np.minimum(large, nb - 1)
    return bucket + np.where(n < max_exact, n, large).astype(np.int32)


def _bias_build(table, idx):
    n_attn = table.shape[1]

    def body(tab_ref, idx_ref, o_ref):
        h = pl.program_id(0)
        idx_v = idx_ref[...]
        acc = jnp.zeros((WINDOW, KEY_SPAN), F32)
        for b in range(REL_BUCKETS):
            acc = jnp.where(idx_v == b, tab_ref[b, h], acc)
        o_ref[...] = acc

    return pl.pallas_call(
        body, name="bias_build", grid=(n_attn,),
        in_specs=[pl.BlockSpec(memory_space=pltpu.SMEM), pl.BlockSpec((WINDOW, KEY_SPAN), lambda h: (0, 0))],
        out_specs=pl.BlockSpec((None, WINDOW, KEY_SPAN), lambda h: (h, 0, 0)),
        out_shape=jax.ShapeDtypeStruct((n_attn, WINDOW, KEY_SPAN), F32), compiler_params=_cparams("parallel"),
    )(table, idx)


def _bias_reduce(dbias, idx):
    n_attn = dbias.shape[0]

    def body(idx_ref, d_ref, o_ref):
        idx_v = idx_ref[...]
        dv = d_ref[...]
        rows = lax.broadcasted_iota(jnp.int32, (REL_BUCKETS, LANES), 0)
        acc = jnp.zeros((REL_BUCKETS, LANES), F32)
        for b in range(REL_BUCKETS):
            part = jnp.sum(jnp.where(idx_v == b, dv, 0.0), axis=1, keepdims=True)
            acc = jnp.where(rows == b, jnp.sum(part, axis=0, keepdims=True), acc)
        o_ref[...] = acc

    return pl.pallas_call(
        body, name="bias_reduce", grid=(n_attn,),
        in_specs=[pl.BlockSpec((WINDOW, KEY_SPAN), lambda h: (0, 0)),
                  pl.BlockSpec((None, WINDOW, KEY_SPAN), lambda h: (h, 0, 0))],
        out_specs=pl.BlockSpec((None, REL_BUCKETS, LANES), lambda h: (h, 0, 0)),
        out_shape=jax.ShapeDtypeStruct((n_attn, REL_BUCKETS, LANES), F32), compiler_params=_cparams("parallel"),
    )(idx, dbias)


def _attn_probs(q, kb, bias, sink, valid):
    s = _dot_nt(q, kb) / math.sqrt(HEAD) + bias
    s = jnp.where(valid, s, NEG_INF)
    m = jnp.maximum(jnp.max(s, axis=-1, keepdims=True), sink)
    e = jnp.exp(s - m)
    e_sink = jnp.exp(sink - m)
    den = jnp.sum(e, axis=-1, keepdims=True) + e_sink
    return e / den, e_sink / den


def _attn_valid(n, t, grp):
    c = lax.broadcasted_iota(jnp.int32, (grp * WINDOW, KEY_SPAN), 0) & (WINDOW - 1)
    s = lax.broadcasted_iota(jnp.int32, (grp * WINDOW, KEY_SPAN), 1)
    rel = s - WINDOW - c
    key_pos = n * WINDOW - WINDOW + s
    return (jnp.abs(rel) <= WINDOW) & (key_pos >= 0) & (key_pos < t)


def _stack_heads(ref, grp):
    return jnp.concatenate([ref[:, g * HEAD:(g + 1) * HEAD] for g in range(grp)], axis=0).astype(BF16)


def _sink_column(sink_ref, x, grp):
    return jnp.concatenate([jnp.full((WINDOW, 1), sink_ref[0, x * grp + g], F32) for g in range(grp)], axis=0)


def _attn_specs(t, n_hgrn, n_attn):
    grp = n_attn // KV_HEADS
    nb = t // WINDOW
    cq = 5 * n_hgrn
    ck = cq + n_attn
    cv = ck + KV_HEADS
    q_spec = pl.BlockSpec((WINDOW, grp * HEAD), lambda x, n: (n, cq // grp + x))
    kv = lambda base, off: pl.BlockSpec(
        (WINDOW, HEAD), lambda x, n: (jnp.clip(n + off, 0, nb - 1), base + x))
    band = [kv(ck, -1), kv(ck, 0), kv(ck, 1), kv(cv, -1), kv(cv, 0), kv(cv, 1)]
    bias_spec = pl.BlockSpec((grp, WINDOW, KEY_SPAN), lambda x, n: (x, 0, 0))
    sink_spec = pl.BlockSpec(memory_space=pltpu.SMEM)
    return grp, nb, q_spec, band, bias_spec, sink_spec


def _attn_fwd(p, bias, sink, cat, n_hgrn, n_attn):
    t = p.shape[0]
    grp, nb, q_spec, band, bias_spec, sink_spec = _attn_specs(t, n_hgrn, n_attn)

    def body(q_ref, kp, kc, kn, vp, vc, vn, bias_ref, sink_ref, cat_ref, y_ref):
        x, n = pl.program_id(0), pl.program_id(1)
        kb = jnp.concatenate([kp[...], kc[...], kn[...]], axis=0).astype(BF16)
        vb = jnp.concatenate([vp[...], vc[...], vn[...]], axis=0).astype(BF16)
        pr, _ = _attn_probs(_stack_heads(q_ref, grp), kb, bias_ref[...].reshape(grp * WINDOW, KEY_SPAN),
                            _sink_column(sink_ref, x, grp), _attn_valid(n, t, grp))
        y = _dot(pr.astype(BF16), vb).astype(BF16)
        for g in range(grp):
            y_ref[:, g * HEAD:(g + 1) * HEAD] = y[g * WINDOW:(g + 1) * WINDOW]

    return pl.pallas_call(
        body, name="attn_fwd", grid=(KV_HEADS, nb),
        in_specs=[q_spec] + band + [bias_spec, sink_spec, ANY],
        out_specs=pl.BlockSpec((WINDOW, grp * HEAD), lambda x, n: (n, n_hgrn // grp + x)),
        out_shape=jax.ShapeDtypeStruct(cat.shape, BF16), input_output_aliases={9: 0},
        compiler_params=_cparams("parallel", "parallel"),
    )(p, p, p, p, p, p, p, bias, sink, cat)


def _attn_bwd(p, dcat, bias, sink, n_hgrn, n_attn, deps=()):
    t = p.shape[0]
    grp, nb, q_spec, band, bias_spec, sink_spec = _attn_specs(t, n_hgrn, n_attn)
    inv = 1.0 / math.sqrt(HEAD)

    def body(q_ref, kp, kc, kn, vp, vc, vn, bias_ref, sink_ref, do_ref, *rest):
        dq_ref, dk_ref, dv_ref, dbias_ref, dsink_ref, dk_s, dv_s = rest[-7:]
        x, n = pl.program_id(0), pl.program_id(1)

        @pl.when(n == 0)
        def _():
            dk_s[...] = jnp.zeros_like(dk_s)
            dv_s[...] = jnp.zeros_like(dv_s)
            dbias_ref[...] = jnp.zeros_like(dbias_ref)
            dsink_ref[...] = jnp.zeros_like(dsink_ref)

        kb = jnp.concatenate([kp[...], kc[...], kn[...]], axis=0).astype(BF16)
        vb = jnp.concatenate([vp[...], vc[...], vn[...]], axis=0).astype(BF16)
        q = _stack_heads(q_ref, grp)
        do = _stack_heads(do_ref, grp)
        pr, p_sink = _attn_probs(q, kb, bias_ref[...].reshape(grp * WINDOW, KEY_SPAN),
                                 _sink_column(sink_ref, x, grp), _attn_valid(n, t, grp))
        dpr = _dot_nt(do, vb)
        delta = jnp.sum(pr * dpr, axis=-1, keepdims=True)
        ds = pr * (dpr - delta)
        ds_b = ds.astype(BF16)
        dq = (_dot(ds_b, kb) * inv).astype(BF16)
        dsink = -p_sink * delta
        for g in range(grp):
            head = slice(g * WINDOW, (g + 1) * WINDOW)
            dq_ref[:, g * HEAD:(g + 1) * HEAD] = dq[head]
            dbias_ref[g] += ds[head]
            dsink_ref[g:g + 1, :] += jnp.broadcast_to(jnp.sum(dsink[head], axis=0, keepdims=True), (1, WINDOW))
        rows = pl.ds(pl.multiple_of(n * WINDOW, WINDOW), KEY_SPAN)
        dk_s[rows, :] += _dot_tn(ds_b, q) * inv
        dv_s[rows, :] += _dot_tn(pr.astype(BF16), do)

        @pl.when(n == nb - 1)
        def _():
            dk_ref[...] = dk_s[pl.ds(WINDOW, t), :].astype(BF16)
            dv_ref[...] = dv_s[pl.ds(WINDOW, t), :].astype(BF16)

    do_spec = pl.BlockSpec((WINDOW, grp * HEAD), lambda x, n: (n, n_hgrn // grp + x))
    kv_out = pl.BlockSpec((t, HEAD), lambda x, n: (0, x))
    return pl.pallas_call(
        body, name="attn_bwd", grid=(KV_HEADS, nb),
        in_specs=[q_spec] + band + [bias_spec, sink_spec, do_spec] + [ANY] * len(deps),
        out_specs=[pl.BlockSpec((WINDOW, grp * HEAD), lambda x, n: (n, x)), kv_out, kv_out,
                   bias_spec, pl.BlockSpec((None, grp, WINDOW), lambda x, n: (x, 0, 0))],
        out_shape=[jax.ShapeDtypeStruct((t, n_attn * HEAD), BF16),
                   jax.ShapeDtypeStruct((t, KV_HEADS * HEAD), BF16),
                   jax.ShapeDtypeStruct((t, KV_HEADS * HEAD), BF16),
                   jax.ShapeDtypeStruct((n_attn, WINDOW, KEY_SPAN), F32),
                   jax.ShapeDtypeStruct((KV_HEADS, grp, WINDOW), F32)],
        scratch_shapes=[pltpu.VMEM((t + 2 * WINDOW, HEAD), F32), pltpu.VMEM((t + 2 * WINDOW, HEAD), F32)],
        compiler_params=_cparams("parallel", "arbitrary"),
    )(p, p, p, p, p, p, p, bias, sink, dcat, *deps)


def _position():
    return lax.axis_index("x"), lax.axis_index("y"), lax.axis_index("c")


def _handshake(peers):
    barrier = pltpu.get_barrier_semaphore()
    for peer in peers:
        pl.semaphore_signal(barrier, inc=1, device_id=peer, device_id_type=MESH)
    pl.semaphore_wait(barrier, len(peers))


def _sequencer(name, collective_id, scratch_types):
    return functools.partial(
        pl.kernel, mesh=plsc.ScalarSubcoreMesh(axis_name="sc", num_cores=1), name=name,
        scratch_types=scratch_types, compiler_params=pltpu.CompilerParams(collective_id=collective_id))


def _all_gather(name, shard, collective_id):
    rows = shard.shape[0]
    assert rows % 2 == 0
    rh = rows // 2
    src = jax.new_ref(shard, memory_space=pltpu.MemorySpace.HBM)
    out = jax.empty_ref(jax.ShapeDtypeStruct((N_DEV,) + shard.shape, shard.dtype),
                        memory_space=pltpu.MemorySpace.HBM)
    n_copies = 11

    @_sequencer(name, collective_id, (pltpu.SemaphoreType.DMA((n_copies,)), pltpu.SemaphoreType.DMA((n_copies,)),
                                      pltpu.SemaphoreType.DMA))
    def launch(send_sems, recv_sems, local_sem):
        x, y, c = _position()
        sibling = (x, y, 1 - c)
        xn, yn, dg = (1 - x, y), (x, 1 - y), (1 - x, 1 - y)
        _handshake([sibling, (*xn, c), (*yn, c)])

        def part(ref, half):
            return ref if half is None else ref.at[pl.ds(half * rh, rh)]

        def slot(chip, core, half=None):
            return part(out.at[4 * chip[0] + 2 * chip[1] + core], half)

        def copy(k, chip, core, half, to, own=False):
            return pltpu.make_async_remote_copy(
                src_ref=part(src, half) if own else slot(chip, core, half), dst_ref=slot(chip, core, half),
                send_sem=send_sems.at[k], recv_sem=recv_sems.at[k], device_id=to, device_id_type=MESH)

        def landed(k, chip, core, half):
            copy(k, chip, core, half, (x, y, c)).wait_recv()

        mine = pltpu.make_async_copy(src, slot((x, y), c), local_sem)
        mine.start()
        sent = [copy(0, (x, y), c, None, sibling, own=True),
                copy(1, (x, y), c, 0, (*xn, c), own=True), copy(3, (x, y), c, 1, (*yn, c), own=True),
                copy(2, (x, y), c, 1, (*xn, c), own=True), copy(4, (x, y), c, 0, (*yn, c), own=True)]
        for cp in sent:
            cp.start()

        def then(cp):
            cp.start()
            sent.append(cp)

        landed(1, xn, c, 0)
        then(copy(5, xn, c, 0, (*yn, c)))
        landed(3, yn, c, 1)
        then(copy(6, yn, c, 1, (*xn, c)))
        landed(2, xn, c, 1)
        then(copy(7, xn, c, None, sibling))
        landed(4, yn, c, 0)
        then(copy(8, yn, c, None, sibling))
        landed(5, dg, c, 0)
        then(copy(9, dg, c, 0, sibling))
        landed(6, dg, c, 1)
        then(copy(10, dg, c, 1, sibling))
        landed(0, (x, y), 1 - c, None)
        landed(7, xn, 1 - c, None)
        landed(8, yn, 1 - c, None)
        landed(9, dg, 1 - c, 0)
        landed(10, dg, 1 - c, 1)
        for cp in sent:
            cp.wait_send()
        mine.wait()

    launch()
    return out[...]


HBM = pl.BlockSpec(memory_space=pltpu.HBM)
SEM = pl.BlockSpec(memory_space=pltpu.SEMAPHORE)
EFFECT = pltpu.SideEffectType.DATAFLOW_SIDE_EFFECTING


def _pair_copies(s_ref, land_ref, send_sems, recv_sems):
    x, y, c = _position()
    return [pltpu.make_async_remote_copy(
        src_ref=s_ref.at[2 * k + (1 - c)], dst_ref=land_ref.at[k], send_sem=send_sems.at[k],
        recv_sem=recv_sems.at[k], device_id=(x, y, 1 - c), device_id_type=MESH) for k in range(4)]


def _pair_start(name, stack):
    land_shape = (4,) + stack.shape[1:]

    def body(s_ref, land_ref, send_sems, recv_sems, s_thru, land_thru, token):
        for cp in _pair_copies(s_ref, land_ref, send_sems, recv_sems):
            cp.start()
        token[...] = jnp.zeros_like(token)

    return pl.pallas_call(
        body, name=name,
        out_shape=(pltpu.SemaphoreType.DMA((4,)), pltpu.SemaphoreType.DMA((4,)),
                   pltpu.HBM(stack.shape, stack.dtype), pltpu.HBM(land_shape, stack.dtype),
                   jax.ShapeDtypeStruct((8, LANES), F32)),
        in_specs=(HBM, HBM), out_specs=(SEM, SEM, HBM, HBM, pl.BlockSpec(memory_space=pltpu.VMEM)),
        input_output_aliases={0: 2, 1: 3}, compiler_params=pltpu.CompilerParams(has_side_effects=EFFECT),
    )(pltpu.with_memory_space_constraint(stack, pltpu.HBM),
      pltpu.with_memory_space_constraint(lax.empty(land_shape, stack.dtype), pltpu.HBM))


def _pair_wait(name, started, after):
    send_sems, recv_sems, s_thru, land_thru, _ = started

    def body(s_ref, land_ref, send_sems, recv_sems, after_ref, s_out, land_out):
        for cp in _pair_copies(s_ref, land_ref, send_sems, recv_sems):
            cp.wait_send()
            cp.wait_recv()

    return pl.pallas_call(
        body, name=name,
        out_shape=(pltpu.HBM(s_thru.shape, s_thru.dtype), pltpu.HBM(land_thru.shape, land_thru.dtype)),
        in_specs=(HBM, HBM, SEM, SEM, ANY), out_specs=(HBM, HBM), input_output_aliases={0: 0, 1: 1},
        compiler_params=pltpu.CompilerParams(has_side_effects=EFFECT),
    )(s_thru, land_thru, send_sems, recv_sems, after)


def _pair_sum(name, stack, other, core):
    _, r, c = stack.shape
    tr = _row_tile(r, 1024)

    def body(core_ref, a_ref, b_ref, o_ref):
        o_ref[...] = (a_ref[...].astype(F32) + b_ref[...].astype(F32)).astype(o_ref.dtype)

    grid_spec = pltpu.PrefetchScalarGridSpec(
        num_scalar_prefetch=1, grid=(4, r // tr),
        in_specs=[pl.BlockSpec((None, tr, c), lambda k, i, core_ref: (2 * k + core_ref[0], i, 0)),
                  pl.BlockSpec((None, tr, c), lambda k, i, core_ref: (k, i, 0))],
        out_specs=pl.BlockSpec((None, tr, c), lambda k, i, core_ref: (k, i, 0)))
    return pl.pallas_call(
        body, name=name, grid_spec=grid_spec, out_shape=jax.ShapeDtypeStruct((4, r, c), stack.dtype),
        compiler_params=_cparams("parallel", "parallel"),
    )(core, stack, other)


def _chip_exchange(name, sums, collective_id):
    src = jax.new_ref(sums, memory_space=pltpu.MemorySpace.HBM)
    out = jax.empty_ref(jax.ShapeDtypeStruct((3,) + sums.shape[1:], sums.dtype),
                        memory_space=pltpu.MemorySpace.HBM)

    @_sequencer(name, collective_id, (pltpu.SemaphoreType.DMA((3,)), pltpu.SemaphoreType.DMA((3,))))
    def launch(send_sems, recv_sems):
        x, y, c = _position()
        chips = [(1 - x, y), (x, 1 - y), (1 - x, 1 - y)]
        _handshake([(*chip, c) for chip in chips])
        copies = [pltpu.make_async_remote_copy(
            src_ref=src.at[2 * px + py], dst_ref=out.at[j], send_sem=send_sems.at[j],
            recv_sem=recv_sems.at[j], device_id=(px, py, c), device_id_type=MESH)
            for j, (px, py) in enumerate(chips)]
        for cp in copies:
            cp.start()
        for cp in copies:
            cp.wait()

    launch()
    return out[...]


def _small_rows(shapes):
    first, row = [], 0
    for r, c in shapes:
        first.append(row)
        row += r * (c // LANES) if c % LANES == 0 else r
        row = -(-row // 8) * 8
    return first, row


def _small_move(packed, row, ref, to_packed):
    r, c = ref.shape
    if c % LANES:
        if to_packed:
            packed[row:row + r, 0:c] = ref[...]
        else:
            ref[...] = packed[row:row + r, 0:c]
        return
    per = c // LANES
    for i in range(r):
        for j in range(per):
            at = row + i * per + j
            if to_packed:
                packed[at:at + 1, :] = ref[i:i + 1, j * LANES:(j + 1) * LANES]
            else:
                ref[i:i + 1, j * LANES:(j + 1) * LANES] = packed[at:at + 1, :]


def _small_step(name, parts, ws, ms, vs, deps=()):
    n_par = len(ws)
    first, rows = _small_rows([p.shape for p in parts])
    vm = pl.BlockSpec(memory_space=pltpu.VMEM)
    buf = pltpu.VMEM((rows, LANES), F32)

    def reduce_body(*refs):
        part_refs, (sum_out, mine, gather, send_sems, recv_sems) = refs[:n_par + 1], refs[-5:]
        x, y, c = _position()
        me = 4 * x + 2 * y + c
        mine[...] = jnp.zeros_like(mine)
        for k, ref in enumerate(part_refs):
            _small_move(mine, first[k], ref, True)
        gather[me] = mine[...]
        copies = []
        for k in range(1, N_DEV):
            peer = (x ^ (k >> 2), y ^ ((k >> 1) & 1), c ^ (k & 1))
            copies.append(pltpu.make_async_remote_copy(
                src_ref=mine, dst_ref=gather.at[me], send_sem=send_sems.at[k - 1],
                recv_sem=recv_sems.at[k - 1], device_id=peer, device_id_type=MESH))
        for cp in copies:
            cp.start()
        for k in range(1, N_DEV):
            peer_slot = 4 * (x ^ (k >> 2)) + 2 * (y ^ ((k >> 1) & 1)) + (c ^ (k & 1))
            pltpu.make_async_remote_copy(
                src_ref=mine, dst_ref=gather.at[peer_slot], send_sem=send_sems.at[k - 1],
                recv_sem=recv_sems.at[k - 1], device_id=(x, y, c), device_id_type=MESH).wait()
        acc = gather[0]
        for j in range(1, N_DEV):
            acc = acc + gather[j]
        sum_out[...] = acc

    summed = pl.pallas_call(
        reduce_body, name=name + "_reduce", in_specs=[vm] * (n_par + 1) + [ANY] * len(deps), out_specs=vm,
        out_shape=jax.ShapeDtypeStruct((rows, LANES), F32),
        scratch_shapes=[buf, pltpu.VMEM((N_DEV, rows, LANES), F32), pltpu.SemaphoreType.DMA((7,)),
                        pltpu.SemaphoreType.DMA((7,))],
    )(*parts, *deps)

    def adam_body(*refs):
        sum_ref, refs = refs[0], refs[1:]
        w_refs, m_refs, v_refs, refs = refs[:n_par], refs[n_par:2 * n_par], refs[2 * n_par:3 * n_par], refs[3 * n_par:]
        g_out, d_out, m_out, v_out = (refs[i * n_par:(i + 1) * n_par] for i in range(4))
        loss_out = refs[4 * n_par]
        w_p, m_p, v_p, d_p = refs[4 * n_par + 1:]
        for packed in (w_p, m_p, v_p):
            packed[...] = jnp.zeros_like(packed)
        for k in range(n_par):
            for packed, src in ((w_p, w_refs[k]), (m_p, m_refs[k]), (v_p, v_refs[k])):
                _small_move(packed, first[k], src, True)
        delta, m_new, v_new = _adam_math(w_p[...], sum_ref[...], m_p[...], v_p[...])
        d_p[...] = delta
        m_p[...] = m_new
        v_p[...] = v_new
        for k in range(n_par):
            for packed, dst in ((sum_ref, g_out[k]), (d_p, d_out[k]), (m_p, m_out[k]), (v_p, v_out[k])):
                _small_move(packed, first[k], dst, False)
        _small_move(sum_ref, first[n_par], loss_out, False)

    like = [jax.ShapeDtypeStruct(w.shape, F32) for w in ws]
    outs = pl.pallas_call(
        adam_body, name=name + "_adam", in_specs=[vm] * (3 * n_par + 1), out_specs=[vm] * (4 * n_par + 1),
        out_shape=like * 4 + [jax.ShapeDtypeStruct((1, LANES), F32)], scratch_shapes=[buf, buf, buf, buf],
    )(summed, *ws, *ms, *vs)
    return (outs[:n_par], outs[n_par:2 * n_par], outs[2 * n_par:3 * n_par], outs[3 * n_par:4 * n_par],
            outs[4 * n_par])


def _adam_math(w, g, m, v):
    m = ADAM_B1 * m + (1.0 - ADAM_B1) * g
    v = ADAM_B2 * v + (1.0 - ADAM_B2) * jnp.square(g)
    m_hat = m / (1.0 - ADAM_B1 ** ADAM_STEP)
    v_hat = v / (1.0 - ADAM_B2 ** ADAM_STEP)
    delta = -ADAM_LR * (m_hat / (jnp.sqrt(v_hat) + ADAM_EPS) + ADAM_WD * w)
    return delta, m, v


def _adam_shard(name, w, m, v, sums, recv, chip, deps=(), first_row=0, earlier=()):
    r, c = w.shape
    rows = sums.shape[1]
    tr = _row_tile(rows, 256)
    assert first_row % tr == 0
    skip = first_row // tr

    def body(chip_ref, w_ref, m_ref, v_ref, own_ref, r0_ref, r1_ref, r2_ref, *rest):
        g_out, d_out, m_out, v_out = rest[-4:]
        g = ((own_ref[...].astype(F32) + r0_ref[...].astype(F32)) + r1_ref[...].astype(F32)) + r2_ref[...].astype(F32)
        delta, m_new, v_new = _adam_math(w_ref[...], g, m_ref[...], v_ref[...])
        g_out[...] = g
        d_out[...] = delta
        m_out[...] = m_new
        v_out[...] = v_new

    plain = pl.BlockSpec((tr, c), lambda i, chip_ref: (skip + i, 0))
    piece = lambda j: pl.BlockSpec((None, tr, c), lambda i, chip_ref: (j, i, 0))
    grid_spec = pltpu.PrefetchScalarGridSpec(
        num_scalar_prefetch=1, grid=(rows // tr,),
        in_specs=[plain, plain, plain,
                  pl.BlockSpec((None, tr, c), lambda i, chip_ref: (chip_ref[0], i, 0)),
                  piece(0), piece(1), piece(2)] + [ANY] * (len(earlier) + len(deps)),
        out_specs=[plain] * 4)
    shape = jax.ShapeDtypeStruct((r, c), F32)
    return pl.pallas_call(
        body, name=name, grid_spec=grid_spec, out_shape=[shape] * 4, compiler_params=_cparams("parallel"),
        input_output_aliases={8 + k: k for k in range(len(earlier))},
    )(chip, w, m, v, sums, recv, recv, recv, *earlier, *deps)


def _reduce_scatter(tag, started, after, core, collective_id):
    grad_stack, other = _pair_wait("rs_pair_wait_" + tag, started, after)
    sums = _pair_sum("rs_sum_" + tag, grad_stack, other, core)
    return sums, _chip_exchange("rs_chip_" + tag, sums, collective_id)


SMALL = ("pre_norm_ffn1", "post_norm_ffn1", "pre_norm_mix", "post_norm_mix", "hgrn_lower_bounds_fwd",
         "hgrn_lower_bounds_bwd", "hgrn_out_norm", "attn_sink", "pre_norm_ffn2", "post_norm_ffn2", "rel_bias_table")
BIG = ("w_ffn1_gate_up", "w_ffn1_down", "w_mix_in", "w_mix_out", "w_ffn2_gate_up", "w_ffn2_down")
AG_ID = {n: 1 + i for i, n in enumerate(BIG)}
RS_ID = {n: 7 + i for i, n in enumerate(BIG)}
RS_ID.update(w_ffn1_gate_up_a=RS_ID["w_ffn1_gate_up"], w_ffn1_gate_up_b=13)
ORDER = ("pre_norm_ffn1", "post_norm_ffn1", "w_ffn1_gate_up", "w_ffn1_down", "pre_norm_mix", "post_norm_mix",
         "w_mix_in", "hgrn_lower_bounds_fwd", "hgrn_lower_bounds_bwd", "hgrn_out_norm", "attn_sink", "w_mix_out",
         "pre_norm_ffn2", "post_norm_ffn2", "w_ffn2_gate_up", "w_ffn2_down", "rel_bias_table")


def kernel(x, pre_norm_ffn1, post_norm_ffn1, w_ffn1_gate_up, w_ffn1_down, pre_norm_mix, post_norm_mix, w_mix_in, hgrn_lower_bounds_fwd, hgrn_lower_bounds_bwd, hgrn_out_norm, attn_sink, w_mix_out, pre_norm_ffn2, post_norm_ffn2, w_ffn2_gate_up, w_ffn2_down, rel_bias_table, loss_target, m_pre_norm_ffn1, m_post_norm_ffn1, m_w_ffn1_gate_up, m_w_ffn1_down, m_pre_norm_mix, m_post_norm_mix, m_w_mix_in, m_hgrn_lower_bounds_fwd, m_hgrn_lower_bounds_bwd, m_hgrn_out_norm, m_attn_sink, m_w_mix_out, m_pre_norm_ffn2, m_post_norm_ffn2, m_w_ffn2_gate_up, m_w_ffn2_down, m_rel_bias_table, v_pre_norm_ffn1, v_post_norm_ffn1, v_w_ffn1_gate_up, v_w_ffn1_down, v_pre_norm_mix, v_post_norm_mix, v_w_mix_in, v_hgrn_lower_bounds_fwd, v_hgrn_lower_bounds_bwd, v_hgrn_out_norm, v_attn_sink, v_w_mix_out, v_pre_norm_ffn2, v_post_norm_ffn2, v_w_ffn2_gate_up, v_w_ffn2_down, v_rel_bias_table):
    args = dict(locals())
    wts = {n: args[n] for n in ORDER}
    mom = {n: args["m_" + n] for n in ORDER}
    var = {n: args["v_" + n] for n in ORDER}

    x0 = x[0]
    target = loss_target[0]
    t, d = x0.shape
    n_hgrn = d // 2 // HEAD
    n_attn = (d - d // 2) // HEAD
    core = lax.axis_index("c").astype(jnp.int32).reshape(1)
    chip = (2 * lax.axis_index("x") + lax.axis_index("y")).astype(jnp.int32).reshape(1)

    def local(a, name):
        return jnp.transpose(a[0]) if name == "w_mix_in" else a[0]

    full = {n: _all_gather("ag_" + n, local(wts[n], n).astype(BF16), AG_ID[n]) for n in BIG}
    w_gu1, w_gu2 = full["w_ffn1_gate_up"], full["w_ffn2_gate_up"]
    w_d1 = full["w_ffn1_down"].reshape(-1, d)
    w_d2 = full["w_ffn2_down"].reshape(-1, d)
    w_out = full["w_mix_out"].reshape(-1, d)
    w_in_t = full["w_mix_in"].reshape(-1, d)

    g = {n: wts[n] for n in SMALL}
    lb_f = jax.nn.softmax(g["hgrn_lower_bounds_fwd"], axis=0)[0:1]
    lb_b = jax.nn.softmax(g["hgrn_lower_bounds_bwd"], axis=0)[0:1]
    bucket_idx = jnp.asarray(_t5_bucket_index())
    bias = _bias_build(g["rel_bias_table"], bucket_idx)

    n1 = _pre_norm("pre_norm1", x0, g["pre_norm_ffn1"])
    a1, gu1 = _ffn_up("ffn1_gate_up", n1, w_gu1)
    ff1 = _matmul("ffn1_down", a1, w_d1, mode="nn", out_dtype=F32)
    x1, h = _post_res_pre("res1", x0, ff1, g["post_norm_ffn1"], g["pre_norm_mix"], 0.5)
    p = _matmul("mix_in", h, w_in_t, mode="nt", out_dtype=F32, tm=2048, tn=512)
    y_h, o_raw = _hgrn_fwd(p, lb_f, lb_b, g["hgrn_out_norm"], n_hgrn, d)
    cat = _attn_fwd(p, bias, g["attn_sink"], y_h, n_hgrn, n_attn)
    mixed = _matmul("mix_out", cat, w_out, mode="nn", out_dtype=F32)
    x2, n2 = _post_res_pre("res2", x1, mixed, g["post_norm_mix"], g["pre_norm_ffn2"], 1.0)
    a2, gu2 = _ffn_up("ffn2_gate_up", n2, w_gu2)
    ff2 = _matmul("ffn2_down", a2, w_d2, mode="nn", out_dtype=F32)
    small_grad = {}
    dy3, loss_part, dff2, small_grad["post_norm_ffn2"] = _post_res_loss(
        "res3_loss", x2, ff2, g["post_norm_ffn2"], target, 0.5)

    scattered = {}

    pending = []

    def scatter(name, grad_stack):
        started = _pair_start("rs_pair_" + name, grad_stack)
        pending.append((name, started))
        return [started[4]]

    def settle(after, count=len(BIG)):
        deps = []
        while pending and count:
            name, started = pending.pop(0)
            scattered[name] = _reduce_scatter(name, started, after, core, RS_ID[name])
            deps.append(scattered[name][0])
            count -= 1
        return deps

    def ffn_bwd(tag, dff, a, gu, n_in, w_gu, w_d, gu_name, d_name, last):
        dep = settle(dff)

        def dw_down(deps):
            return scatter(d_name, _matmul("dw_down" + tag, a, dff, mode="tn", out_dtype=BF16, tn=2048,
                                           deps=deps).reshape(N_DEV, -1, d))

        def dw_gate_up(part, rows, deps):
            return scatter(gu_name + part, _matmul("dw_gate_up" + tag + part, n_in, dgu, mode="tn", a_cols=rows,
                                                   stack=True, halves=True, out_dtype=BF16, deps=deps))

        dep = dw_down(dep)
        dgu = _ffn_dact("d_act" + tag, dff, w_d, gu, deps=dep)
        dep = settle(dgu)
        if last:
            half = d // 2
            dep = dw_gate_up("_a", slice(0, half), dep)
            dep = dw_gate_up("_b", slice(half, d), dep)
            dep = settle(dep[0], count=1) + dep
        else:
            dep = dw_gate_up("", slice(0, d), dep)
        dn = _matmul("d_norm" + tag, dgu, w_gu, mode="nt", stack=True, halves=True, out_dtype=BF16, deps=dep)
        return dn, settle(dn)

    dn2, dep = ffn_bwd("2", dff2, a2, gu2, n2, w_gu2, w_d2, "w_ffn2_gate_up", "w_ffn2_down", last=False)
    dx2, small_grad["pre_norm_ffn2"], dmixed, small_grad["post_norm_mix"] = _pre_bwd(
        "pre_bwd2", dn2, x2, g["pre_norm_ffn2"], dy3, deps=dep, post=(mixed, g["post_norm_mix"], 1.0))
    dep = settle(dmixed)
    dcat = _matmul("d_cat", dmixed, w_out, mode="nt", out_dtype=BF16, deps=dep)
    dep = scatter("w_mix_out", _matmul("dw_mix_out", cat, dmixed, mode="tn", out_dtype=BF16).reshape(N_DEV, -1, d))
    dq_a, dk_a, dv_a, dbias, dsink_rows = _attn_bwd(p, dcat, bias, g["attn_sink"], n_hgrn, n_attn, deps=dep)
    dq_h, di_h, dzf, dzb, dg_h, dlb_f, dlb_b, small_grad["hgrn_out_norm"] = _hgrn_bwd(
        p, o_raw, dcat, lb_f, lb_b, g["hgrn_out_norm"], n_hgrn)
    dp = jnp.concatenate([dq_h, di_h, dzf, dzb, dg_h, dq_a, dk_a, dv_a], axis=1)
    dep = settle(dp)
    dh = _matmul("d_h", dp, w_in_t, mode="nn", out_dtype=BF16, tm=2048, deps=dep)
    dep = scatter("w_mix_in", _matmul("dw_mix_in", dp, h, mode="tn", out_dtype=BF16, tm=512,
                                         tn=2048).reshape(N_DEV, -1, d))
    dx1, small_grad["pre_norm_mix"], dff1, small_grad["post_norm_ffn1"] = _pre_bwd(
        "pre_bwd_mix", dh, x1, g["pre_norm_mix"], dx2, deps=dep, post=(ff1, g["post_norm_ffn1"], 0.5))

    dn1, dep = ffn_bwd("1", dff1, a1, gu1, n1, w_gu1, w_d1, "w_ffn1_gate_up", "w_ffn1_down", last=True)
    dx0, small_grad["pre_norm_ffn1"] = _pre_bwd("pre_bwd1", dn1, x0, g["pre_norm_ffn1"], dx1, deps=dep)

    def lb_grad(dlb, lb):
        da0 = dlb * lb * (1.0 - lb)
        return jnp.concatenate([da0, -da0], axis=0)

    small_grad["hgrn_lower_bounds_fwd"] = lb_grad(dlb_f, lb_f)
    small_grad["hgrn_lower_bounds_bwd"] = lb_grad(dlb_b, lb_b)
    small_grad["attn_sink"] = dsink_rows[:, :, 0].reshape(1, n_attn)
    small_grad["rel_bias_table"] = jnp.transpose(_bias_reduce(dbias, bucket_idx)[:, :, 0])

    def adam(n, tag, dep, **rows):
        sums, recv = scattered[n + tag]
        return _adam_shard("adam_" + n + tag, local(wts[n], n), local(mom[n], n), local(var[n], n), sums, recv,
                           chip, deps=dep, **rows)

    big_out = {}
    dep = []
    for n in ("w_ffn2_down", "w_ffn2_gate_up", "w_mix_out", "w_mix_in", "w_ffn1_down"):
        big_out[n] = adam(n, "", dep)
        dep = [big_out[n][0]]
    first_half = adam("w_ffn1_gate_up", "_a", dep)

    g_s, d_s, m_s, v_s, loss_row = _small_step(
        "small_step", [small_grad[n] for n in SMALL] + [loss_part], [wts[n] for n in SMALL],
        [mom[n] for n in SMALL], [var[n] for n in SMALL], deps=[first_half[0]])
    loss = loss_row[0, 0]
    grads, delta, new_m, new_v = (dict(zip(SMALL, vals)) for vals in (g_s, d_s, m_s, v_s))

    big_out["w_ffn1_gate_up"] = adam("w_ffn1_gate_up", "_b", [loss_row], first_row=d // 2, earlier=first_half)
    for n in BIG:
        grads[n], delta[n], new_m[n], new_v[n] = [local(o[None], n)[None] for o in big_out[n]]

    return (loss, dx0[None], *[grads[n] for n in ORDER], *[delta[n] for n in ORDER],
            *[new_m[n] for n in ORDER], *[new_v[n] for n in ORDER])
```

```python
import functools
import math

import numpy as np
import jax
import jax.numpy as jnp
from jax import lax
from jax.experimental import pallas as pl
from jax.experimental.pallas import tpu as pltpu
from jax.experimental.pallas import tpu_sc as plsc

F32 = jnp.float32
BF16 = jnp.bfloat16
MESH = pl.DeviceIdType.MESH

N_DEV = 8
EPS = 1e-6
NEG_INF = -1e30
HEAD = 128
CHUNK = 64
WINDOW = 128
KEY_SPAN = 3 * WINDOW
KV_HEADS = 2
REL_BUCKETS = 32
REL_MAX_DIST = 128
ADAM_LR, ADAM_B1, ADAM_B2, ADAM_EPS, ADAM_WD, ADAM_STEP = 0.001, 0.9, 0.999, 1e-08, 0.01, 10
LANES = 128
VMEM_LIMIT = 56 * 1024 * 1024
ANY = pl.BlockSpec(memory_space=pl.ANY)


def _cparams(*sem):
    return pltpu.CompilerParams(dimension_semantics=sem if sem else None, vmem_limit_bytes=VMEM_LIMIT)


def _dot(a, b):
    return jnp.dot(a, b, preferred_element_type=F32)


def _dot_nt(a, b):
    return lax.dot_general(a, b, (((1,), (1,)), ((), ())), preferred_element_type=F32)


def _dot_tn(a, b):
    return lax.dot_general(a, b, (((0,), (0,)), ((), ())), preferred_element_type=F32)


def _tile(dim, target):
    for c in (target, 1024, 512, 256, 128):
        if c <= target and dim % c == 0:
            return c
    return dim


def _row_tile(rows, target):
    fits = [c for c in range(16, min(rows, target) + 1, 16) if rows % c == 0]
    return max(fits) if fits else rows


K_WHOLE = 2048
K_STEP = 2816


def _k_tile(kd):
    if kd <= K_WHOLE:
        return kd
    return max(c for c in range(LANES, K_STEP + 1, LANES) if kd % c == 0)


def _matmul(name, a, b, *, mode, out_dtype, stack=False, halves=False, tm=1024, tn=1024, deps=(), a_cols=None):
    assert not (stack and mode == "nn")
    grp = 1
    if mode == "nn":
        m, kd = a.shape
        n = b.shape[1]
    elif mode == "nt":
        m = a.shape[-2]
        n, kd = (b.shape[1], b.shape[0] * b.shape[2]) if stack else b.shape
    else:
        kd, m = a.shape
        if a_cols:
            m = a_cols.stop - a_cols.start
        n = b.shape[-1] * (2 if halves else 1)
    if stack:
        n1 = b.shape[2] if mode == "nt" else n // N_DEV
        assert n1 % LANES == 0
        if mode == "nt":
            grp = 2 if 2 * n1 <= K_STEP else 1
            tk = grp * n1
        else:
            tn = n1
    per_half = N_DEV // 2 // grp
    tm = _tile(m, tm)
    if not (stack and mode == "tn"):
        tn = _tile(n, tn)
    if not (stack and mode == "nt"):
        tk = _k_tile(kd)
    nk = kd // tk
    lead = None if grp == 1 else grp
    b_outer = nk == 1 and b.size > a.size
    grid = (n // tn, m // tm, nk) if b_outer else (m // tm, n // tn, nk)

    def spec(shape, index):
        return pl.BlockSpec(shape, (lambda g0, g1, k: index(g1, g0, k)) if b_outer else index)

    if mode == "nn":
        a_spec = spec((tm, tk), lambda i, j, k: (i, k))
        b_spec = spec((tk, tn), lambda i, j, k: (k, j))
        dot = _dot
    elif mode == "nt":
        if halves:
            a_spec = spec((None, tm, tk), lambda i, j, k: (k // per_half, i, k % per_half))
        else:
            a_spec = spec((tm, tk), lambda i, j, k: (i, k))
        if stack:
            b_spec = spec((lead, tn, n1), lambda i, j, k: (k, j, 0))
        else:
            b_spec = spec((tn, tk), lambda i, j, k: (j, k))
        dot = _dot_nt
    else:
        first = a_cols.start // tm if a_cols else 0
        assert not a_cols or a_cols.start % tm == 0
        a_spec = spec((tk, tm), lambda i, j, k: (k, first + i))
        if halves:
            b_spec = spec((None, tk, tn), lambda i, j, k: (j // per_half, k, j % per_half))
        else:
            b_spec = spec((tk, tn), lambda i, j, k: (k, j))
        dot = _dot_tn
    if stack and mode == "tn":
        out_shape = jax.ShapeDtypeStruct((N_DEV, m, n1), out_dtype)
        o_spec = spec((None, tm, n1), lambda i, j, k: (j, i, 0))
    else:
        out_shape = jax.ShapeDtypeStruct((m, n), out_dtype)
        o_spec = spec((tm, tn), lambda i, j, k: (i, j))

    def product(a_ref, b_ref):
        bmat = jnp.concatenate([b_ref[s] for s in range(grp)], axis=1) if grp > 1 else b_ref[...]
        return dot(a_ref[...], bmat)

    def store(o_ref, val):
        o_ref[...] = val.astype(o_ref.dtype)

    def body_whole(a_ref, b_ref, *rest):
        store(rest[-1], product(a_ref, b_ref))

    def body_steps(a_ref, b_ref, *rest):
        o_ref, acc_ref = rest[-2:]
        k = pl.program_id(2)

        @pl.when(k == 0)
        def _():
            acc_ref[...] = product(a_ref, b_ref)

        @pl.when(k > 0)
        def _():
            acc_ref[...] += product(a_ref, b_ref)

        @pl.when(k == nk - 1)
        def _():
            store(o_ref, acc_ref[...])

    return pl.pallas_call(
        body_whole if nk == 1 else body_steps, name=name, grid=grid,
        in_specs=[a_spec, b_spec] + [ANY] * len(deps), out_specs=o_spec, out_shape=out_shape,
        scratch_shapes=[] if nk == 1 else [pltpu.VMEM((tm, tn), F32)],
        compiler_params=_cparams("parallel", "parallel", "arbitrary"),
    )(a, b, *deps)


def _col_parts(width, parts=2):
    groups = width // LANES
    parts = max(1, min(parts, groups // 2))
    bounds = [LANES * (groups * p // parts) for p in range(parts)] + [width]
    return [slice(bounds[p], bounds[p + 1]) for p in range(parts)]


def _ffn_up(name, n, w_stack):
    t, d = n.shape
    s, _, n1 = w_stack.shape
    half = s // 2
    tm = _tile(t, 512)

    def body(n_ref, wg_ref, wu_ref, act_ref, gu_ref):
        nv = n_ref[...]
        for cols in _col_parts(n1):
            gate = _dot(nv, wg_ref[:, cols])
            up = _dot(nv, wu_ref[:, cols])
            sg = jax.nn.sigmoid(gate)
            silu = gate * sg
            act_ref[:, cols] = (silu * up).astype(BF16)
            gu_ref[0, :, cols] = (up * (sg * (1.0 + gate * (1.0 - sg)))).astype(BF16)
            gu_ref[1, :, cols] = silu.astype(BF16)

    return pl.pallas_call(
        body, name=name, grid=(half, t // tm),
        in_specs=[pl.BlockSpec((tm, d), lambda j, i: (i, 0)),
                  pl.BlockSpec((None, d, n1), lambda j, i: (j, 0, 0)),
                  pl.BlockSpec((None, d, n1), lambda j, i: (half + j, 0, 0))],
        out_specs=[pl.BlockSpec((tm, n1), lambda j, i: (i, j)), pl.BlockSpec((2, tm, n1), lambda j, i: (0, i, j))],
        out_shape=[jax.ShapeDtypeStruct((t, half * n1), BF16), jax.ShapeDtypeStruct((2, t, half * n1), BF16)],
        compiler_params=_cparams("parallel", "parallel"),
    )(n, w_stack, w_stack)


def _ffn_dact(name, dff, w_d, gu, deps=()):
    t, d = dff.shape
    f = w_d.shape[0]
    tm = _tile(t, 1024)
    tn = _tile(f, 1408)

    def body(dff_ref, w_ref, gu_ref, *rest):
        dgu_ref = rest[-1]
        da = _dot_nt(dff_ref[...], w_ref[...]).astype(BF16)
        dgu_ref[0] = da * gu_ref[0]
        dgu_ref[1] = da * gu_ref[1]

    pair = pl.BlockSpec((2, tm, tn), lambda j, i: (0, i, j))
    return pl.pallas_call(
        body, name=name, grid=(f // tn, t // tm),
        in_specs=[pl.BlockSpec((tm, d), lambda j, i: (i, 0)), pl.BlockSpec((tn, d), lambda j, i: (j, 0)), pair]
        + [ANY] * len(deps),
        out_specs=pair, out_shape=jax.ShapeDtypeStruct((2, t, f), BF16),
        compiler_params=_cparams("parallel", "parallel"),
    )(dff, w_d, gu, *deps)


ROWS = 256


def _rstd(xf):
    return lax.rsqrt(jnp.mean(xf * xf, axis=-1, keepdims=True) + EPS)


def _row_spec(t, d):
    return pl.BlockSpec((min(ROWS, t), d), lambda i: (i, 0))


def _vec_spec(d):
    return pl.BlockSpec((1, d), lambda i: (0, 0))


def _pre_norm(name, x, gain):
    t, d = x.shape

    def body(x_ref, g_ref, n_ref):
        xf = x_ref[...]
        n_ref[...] = (xf * _rstd(xf) * g_ref[...]).astype(BF16)

    return pl.pallas_call(
        body, name=name, grid=(t // min(ROWS, t),), in_specs=[_row_spec(t, d), _vec_spec(d)],
        out_specs=_row_spec(t, d), out_shape=jax.ShapeDtypeStruct((t, d), BF16),
        compiler_params=_cparams("parallel"),
    )(x, gain)


def _post_res_pre(name, x, ff, g_post, g_next, scale):
    t, d = x.shape

    def body(x_ref, ff_ref, gp_ref, gn_ref, xo_ref, n_ref):
        ff_ = ff_ref[...]
        xn = x_ref[...] + scale * (ff_ * _rstd(ff_) * gp_ref[...])
        xo_ref[...] = xn
        n_ref[...] = (xn * _rstd(xn) * gn_ref[...]).astype(BF16)

    return pl.pallas_call(
        body, name=name, grid=(t // min(ROWS, t),),
        in_specs=[_row_spec(t, d), _row_spec(t, d), _vec_spec(d), _vec_spec(d)],
        out_specs=[_row_spec(t, d), _row_spec(t, d)],
        out_shape=[jax.ShapeDtypeStruct((t, d), F32), jax.ShapeDtypeStruct((t, d), BF16)],
        compiler_params=_cparams("parallel"),
    )(x, ff, g_post, g_next)


def _norm_bwd(xf, gain, dy):
    r = _rstd(xf)
    xh = xf * r
    dxh = dy * gain
    return r * (dxh - xh * jnp.mean(dxh * xh, axis=-1, keepdims=True)), jnp.sum(dy * xh, axis=0, keepdims=True)


def _accumulate(ref, part):
    @pl.when(pl.program_id(0) == 0)
    def _():
        ref[...] = jnp.zeros_like(ref)

    ref[...] += jnp.broadcast_to(part, ref.shape)


def _post_res_loss(name, x, ff, g_post, target, scale):
    t, d = x.shape

    def body(x_ref, ff_ref, gp_ref, tg_ref, dy_ref, loss_ref, dff_ref, dg_ref):
        ff_ = ff_ref[...]
        err = x_ref[...] + scale * (ff_ * _rstd(ff_) * gp_ref[...]) - tg_ref[...]
        dy = err / d
        dy_ref[...] = dy
        _accumulate(loss_ref, 0.5 * jnp.sum(jnp.mean(err * err, axis=-1, keepdims=True), axis=0, keepdims=True))
        dff, dg = _norm_bwd(ff_, gp_ref[...], scale * dy)
        dff_ref[...] = dff.astype(BF16)
        _accumulate(dg_ref, dg)

    return pl.pallas_call(
        body, name=name, grid=(t // min(ROWS, t),),
        in_specs=[_row_spec(t, d), _row_spec(t, d), _vec_spec(d), _row_spec(t, d)],
        out_specs=[_row_spec(t, d), _vec_spec(LANES), _row_spec(t, d), _vec_spec(d)],
        out_shape=[jax.ShapeDtypeStruct((t, d), F32), jax.ShapeDtypeStruct((1, LANES), F32),
                   jax.ShapeDtypeStruct((t, d), BF16), jax.ShapeDtypeStruct((1, d), F32)],
        compiler_params=_cparams("arbitrary"),
    )(x, ff, g_post, target)


def _pre_bwd(name, dn, x, g_pre, dy, deps=(), post=None):
    t, d = x.shape
    n_post = 2 if post else 0
    scale = post[2] if post else None

    def body(dn_ref, x_ref, g_ref, dy_ref, *rest):
        outs = rest[len(rest) - 2 - n_post:]
        dnf = dn_ref[...].astype(F32)
        dpre, dg = _norm_bwd(x_ref[...], g_ref[...], dnf)
        dx = dy_ref[...] + dpre
        outs[0][...] = dx
        _accumulate(outs[1], dg)
        if post:
            ff_ref, gp_ref = rest[0], rest[1]
            dff, dgp = _norm_bwd(ff_ref[...], gp_ref[...], scale * dx)
            outs[2][...] = dff.astype(BF16)
            _accumulate(outs[3], dgp)

    extra_in = [_row_spec(t, d), _vec_spec(d)] if post else []
    extra_out = [_row_spec(t, d), _vec_spec(d)] if post else []
    extra_shape = [jax.ShapeDtypeStruct((t, d), BF16), jax.ShapeDtypeStruct((1, d), F32)] if post else []
    return pl.pallas_call(
        body, name=name, grid=(t // min(ROWS, t),),
        in_specs=[_row_spec(t, d), _row_spec(t, d), _vec_spec(d), _row_spec(t, d)] + extra_in + [ANY] * len(deps),
        out_specs=[_row_spec(t, d), _vec_spec(d)] + extra_out,
        out_shape=[jax.ShapeDtypeStruct((t, d), F32), jax.ShapeDtypeStruct((1, d), F32)] + extra_shape,
        compiler_params=_cparams("arbitrary"),
    )(dn, x, g_pre, dy, *(post[:2] if post else ()), *deps)


def _bdot(a, b, ca, cb, precision=None):
    return lax.dot_general(a, b, (((ca,), (cb,)), ((0,), (0,))), preferred_element_type=F32, precision=precision)


def _tri_masks(g):
    row = lax.broadcasted_iota(jnp.int32, (g, CHUNK, CHUNK), 1)
    col = lax.broadcasted_iota(jnp.int32, (g, CHUNK, CHUNK), 2)
    return col <= row, col >= row


def _ones_matmul(ones_mat, val):
    hi = val.astype(BF16)
    lo = (val - hi.astype(F32)).astype(BF16)
    return _bdot(ones_mat, hi, 2, 1) + _bdot(ones_mat, lo, 2, 1)


def _hgrn_block(z, lb, q, v, cum_mat):
    sg = jax.nn.sigmoid(z)
    f = lb + (1.0 - lb) * sg
    lf = jnp.log(f)
    k = 1.0 - f
    a = _ones_matmul(cum_mat, lf)
    last = jnp.sum(lf, axis=1, keepdims=True)
    e_a = jnp.exp(a)
    e_na = jnp.exp(-a)
    e_t = jnp.exp(last - a)
    return dict(sg=sg, f=f, k=k, decay=jnp.exp(last), e_a=e_a, e_na=e_na, e_t=e_t,
                qd=q * e_a, kd=k * e_na, kt=k * e_t)


def _hgrn_states(state, kv, decay, order):
    entering = [None] * len(order)
    for g in order:
        entering[g] = state
        state = decay[g] * state + kv[g]
    return jnp.stack(entering, axis=0), state


def _hgrn_fwd(p, lb_f, lb_b, gain, n_heads, width):
    t = p.shape[0]
    w = n_heads * HEAD
    blk = min(32, t // CHUNK)
    rows_blk = blk * CHUNK
    n_blocks = t // rows_blk
    fin_rows = min(256, t)

    def body(q_ref, i_ref, zf_ref, zb_ref, g_ref, lbf_ref, lbb_ref, gain_ref, y_ref, o_ref, st_ref):
        low, up = _tri_masks(blk)
        m_low, m_up = low.astype(BF16), up.astype(BF16)
        o_ref[...] = jnp.zeros_like(o_ref)
        st_ref[...] = jnp.zeros_like(st_ref)

        def one(r0, z_ref, lb, slot, rev):
            rows = pl.ds(r0, rows_blk)
            split = lambda ref: ref[rows, :].reshape(blk, CHUNK, HEAD)
            q, v = split(q_ref), split(i_ref)
            c = _hgrn_block(split(z_ref), lb, q, v, m_up if rev else m_low)
            qd, kd, kt, vb = c["qd"].astype(BF16), c["kd"].astype(BF16), c["kt"].astype(BF16), v.astype(BF16)
            pm = jnp.where(up if rev else low, _bdot(qd, kd, 2, 2), 0.0).astype(BF16)
            kv = _bdot(vb, kt, 1, 1)
            order = range(blk - 1, -1, -1) if rev else range(blk)
            entering, st_ref[slot] = _hgrn_states(st_ref[slot], kv, c["decay"], order)
            o = _bdot(pm, vb, 2, 1) + _bdot(qd, entering.astype(BF16), 2, 2)
            o_ref[rows, :] += o.reshape(rows_blk, HEAD)

        def step(n, carry):
            one(pl.multiple_of(n * rows_blk, rows_blk), zf_ref, lbf_ref[...], 0, False)
            one(pl.multiple_of((n_blocks - 1 - n) * rows_blk, rows_blk), zb_ref, lbb_ref[...], 1, True)
            return carry

        lax.fori_loop(0, n_blocks, step, 0)

        def fin(n, carry):
            rows = pl.ds(pl.multiple_of(n * fin_rows, fin_rows), fin_rows)
            o = o_ref[rows, :]
            g = g_ref[rows, :]
            y_ref[rows, :] = (o * _rstd(o) * gain_ref[...] * (g * jax.nn.sigmoid(g))).astype(BF16)
            return carry

        lax.fori_loop(0, t // fin_rows, fin, 0)

    col = lambda grp: pl.BlockSpec((t, HEAD), lambda h: (0, grp * n_heads + h))
    vec = pl.BlockSpec((1, HEAD), lambda h: (0, h))
    out = pl.BlockSpec((t, HEAD), lambda h: (0, h))
    return pl.pallas_call(
        body, name="hgrn_fwd", grid=(n_heads,),
        in_specs=[col(0), col(1), col(2), col(3), col(4), vec, vec, vec],
        out_specs=[out, out],
        out_shape=[jax.ShapeDtypeStruct((t, width), BF16), jax.ShapeDtypeStruct((t, w), F32)],
        scratch_shapes=[pltpu.VMEM((2, HEAD, HEAD), F32)],
        compiler_params=_cparams("parallel"),
    )(p, p, p, p, p, lb_f, lb_b, gain)


def _hgrn_bwd(p, o_raw, dcat, lb_f, lb_b, gain, n_heads):
    t = p.shape[0]
    w = n_heads * HEAD
    n_chunks = t // CHUNK
    blk = min(32, n_chunks)
    rows_blk = blk * CHUNK
    n_blocks = t // rows_blk
    rb = min(256, t)

    def body(q_ref, i_ref, zf_ref, zb_ref, g_ref, o_ref, dy_ref, lbf_ref, lbb_ref, gain_ref,
             dq_ref, di_ref, dzf_ref, dzb_ref, dg_ref, dlbf_ref, dlbb_ref, dgain_ref,
             do_s, dq_s, dv_s, st_s, cur_s):
        low, up = _tri_masks(blk)
        m_low, m_up = low.astype(BF16), up.astype(BF16)
        rowid = lax.broadcasted_iota(jnp.int32, (blk, CHUNK, HEAD), 1)
        gain_v = gain_ref[...]

        def norm_bwd(n, dgain):
            rows = pl.ds(pl.multiple_of(n * rb, rb), rb)
            o = o_ref[rows, :]
            g = g_ref[rows, :]
            dy = dy_ref[rows, :].astype(F32)
            r = _rstd(o)
            oh = o * r
            sg = jax.nn.sigmoid(g)
            dg_ref[rows, :] = (dy * oh * gain_v * (sg * (1.0 + g * (1.0 - sg)))).astype(BF16)
            dno = dy * (g * sg)
            dxh = dno * gain_v
            do_s[rows, :] = r * (dxh - oh * jnp.mean(dxh * oh, axis=-1, keepdims=True))
            return dgain + jnp.sum(dno * oh, axis=0, keepdims=True)

        dgain_ref[...] = lax.fori_loop(0, t // rb, norm_bwd, jnp.zeros((1, HEAD), F32))
        def direction(z_ref, lb_ref, dz_ref, dlb_ref, rev):
            way = int(rev)
            lb = lb_ref[...]
            cum_mat = m_up if rev else m_low
            cum_mat_t = m_low if rev else m_up
            mask = up if rev else low
            last_row = 0 if rev else CHUNK - 1

            order = range(blk - 1, -1, -1) if rev else range(blk)

            def rows_of(j):
                bidx = (n_blocks - 1 - j) if rev else j
                return bidx, pl.ds(pl.multiple_of(bidx * rows_blk, rows_blk), rows_blk)

            def load(rows):
                split = lambda ref: ref[rows, :].reshape(blk, CHUNK, HEAD)
                q, v = split(q_ref), split(i_ref)
                return q, v, _hgrn_block(split(z_ref), lb, q, v, cum_mat)

            def sweep_fwd(j):
                bidx, rows = rows_of(j)
                _, v, c = load(rows)
                kv = _bdot(v.astype(BF16), c["kt"].astype(BF16), 1, 1)
                st_s[way, pl.ds(bidx * blk, blk)], cur_s[2 * way] = _hgrn_states(
                    cur_s[2 * way], kv, c["decay"], order)

            dlb_ref[...] = jnp.zeros_like(dlb_ref)

            def sweep_bwd(jj):
                bidx, rows = rows_of(n_blocks - 1 - jj)
                _, v, c = load(rows)
                st = st_s[way, pl.ds(bidx * blk, blk)]
                do = do_s[rows, :].reshape(blk, CHUNK, HEAD)
                qd, kd, kt, decay = c["qd"], c["kd"], c["kt"], c["decay"]
                qd_b, kd_b, kt_b = qd.astype(BF16), kd.astype(BF16), kt.astype(BF16)
                v_b, do_b, st_b = v.astype(BF16), do.astype(BF16), st.astype(BF16)
                pm = jnp.where(mask, _bdot(qd_b, kd_b, 2, 2), 0.0).astype(BF16)
                dpm = jnp.where(mask, _bdot(do_b, v_b, 2, 2), 0.0).astype(BF16)
                gq = _bdot(do_b, qd_b, 1, 1)
                dstate = cur_s[2 * way + 1]
                after = [None] * blk
                for g in reversed(order):
                    after[g] = dstate
                    dstate = gq[g] + decay[g] * dstate
                cur_s[2 * way + 1] = dstate
                dst = jnp.stack(after, axis=0)
                dst_b = dst.astype(BF16)
                dv = _bdot(pm, do_b, 1, 1) + _bdot(kt_b, dst_b, 2, 2)
                dqd = _bdot(dpm, kd_b, 2, 1) + _bdot(do_b, st_b, 2, 1)
                dkd = _bdot(dpm, qd_b, 1, 1)
                dkt = _bdot(v_b, dst_b, 2, 1)
                dlast = (jnp.sum(dkt * kt, axis=1, keepdims=True)
                         + decay * jnp.sum(dst * st, axis=1, keepdims=True))
                dq_s[way, rows, :] = (dqd * c["e_a"]).reshape(rows_blk, HEAD)
                dv_s[way, rows, :] = dv.reshape(rows_blk, HEAD)
                dk = dkd * c["e_na"] + dkt * c["e_t"]
                da = dqd * qd - dkd * kd - dkt * kt
                da = da + jnp.where(rowid == last_row, dlast, 0.0)
                dlf = _ones_matmul(cum_mat_t, da)
                df = dlf / c["f"] - dk
                sg = c["sg"]
                dz_ref[rows, :] = (df * (1.0 - lb) * (sg * (1.0 - sg))).reshape(rows_blk, HEAD).astype(BF16)
                dlb_ref[...] += jnp.sum((df * (1.0 - sg)).reshape(rows_blk, HEAD), axis=0, keepdims=True)

            return sweep_fwd, sweep_bwd

        ways = [direction(zf_ref, lbf_ref, dzf_ref, dlbf_ref, False),
                direction(zb_ref, lbb_ref, dzb_ref, dlbb_ref, True)]
        cur_s[...] = jnp.zeros_like(cur_s)
        for sweep in range(2):
            def both(j, carry):
                for way in ways:
                    way[sweep](j)
                return carry

            lax.fori_loop(0, n_blocks, both, 0)
        dq_ref[...] = (dq_s[0] + dq_s[1]).astype(BF16)
        di_ref[...] = (dv_s[0] + dv_s[1]).astype(BF16)

    col = lambda grp: pl.BlockSpec((t, HEAD), lambda h: (0, grp * n_heads + h))
    one = pl.BlockSpec((t, HEAD), lambda h: (0, h))
    vec = pl.BlockSpec((1, HEAD), lambda h: (0, h))
    big = jax.ShapeDtypeStruct((t, w), BF16)
    small = jax.ShapeDtypeStruct((1, w), F32)
    return pl.pallas_call(
        body, name="hgrn_bwd", grid=(n_heads,),
        in_specs=[col(0), col(1), col(2), col(3), col(4), one, one, vec, vec, vec],
        out_specs=[one] * 5 + [vec] * 3,
        out_shape=[big] * 5 + [small] * 3,
        scratch_shapes=[pltpu.VMEM((t, HEAD), F32), pltpu.VMEM((2, t, HEAD), F32), pltpu.VMEM((2, t, HEAD), F32),
                        pltpu.VMEM((2, n_chunks, HEAD, HEAD), F32), pltpu.VMEM((4, HEAD, HEAD), F32)],
        compiler_params=_cparams("parallel"),
    )(p, p, p, p, p, o_raw, dcat, lb_f, lb_b, gain)


def _t5_bucket_index():
    c = np.arange(WINDOW)[:, None]
    s = np.arange(KEY_SPAN)[None, :]
    rel = s - WINDOW - c
    nb = REL_BUCKETS // 2
    max_exact = nb // 2
    bucket = (rel > 0).astype(np.int32) * nb
    n = np.abs(rel)
    large = max_exact + (np.log(np.maximum(n, 1) / max_exact) / np.log(REL_MAX_DIST / max_exact)
                         * (nb - max_exact)).astype(np.int32)
    large = np.minimum(large, nb - 1)
    return bucket + np.where(n < max_exact, n, large).astype(np.int32)


def _bias_build(table, idx):
    n_attn = table.shape[1]

    def body(tab_ref, idx_ref, o_ref):
        h = pl.program_id(0)
        idx_v = idx_ref[...]
        acc = jnp.zeros((WINDOW, KEY_SPAN), F32)
        for b in range(REL_BUCKETS):
            acc = jnp.where(idx_v == b, tab_ref[b, h], acc)
        o_ref[...] = acc

    return pl.pallas_call(
        body, name="bias_build", grid=(n_attn,),
        in_specs=[pl.BlockSpec(memory_space=pltpu.SMEM), pl.BlockSpec((WINDOW, KEY_SPAN), lambda h: (0, 0))],
        out_specs=pl.BlockSpec((None, WINDOW, KEY_SPAN), lambda h: (h, 0, 0)),
        out_shape=jax.ShapeDtypeStruct((n_attn, WINDOW, KEY_SPAN), F32), compiler_params=_cparams("parallel"),
    )(table, idx)


def _bias_reduce(dbias, idx):
    n_attn = dbias.shape[0]

    def body(idx_ref, d_ref, o_ref):
        idx_v = idx_ref[...]
        dv = d_ref[...]
        rows = lax.broadcasted_iota(jnp.int32, (REL_BUCKETS, LANES), 0)
        acc = jnp.zeros((REL_BUCKETS, LANES), F32)
        for b in range(REL_BUCKETS):
            part = jnp.sum(jnp.where(idx_v == b, dv, 0.0), axis=1, keepdims=True)
            acc = jnp.where(rows == b, jnp.sum(part, axis=0, keepdims=True), acc)
        o_ref[...] = acc

    return pl.pallas_call(
        body, name="bias_reduce", grid=(n_attn,),
        in_specs=[pl.BlockSpec((WINDOW, KEY_SPAN), lambda h: (0, 0)),
                  pl.BlockSpec((None, WINDOW, KEY_SPAN), lambda h: (h, 0, 0))],
        out_specs=pl.BlockSpec((None, REL_BUCKETS, LANES), lambda h: (h, 0, 0)),
        out_shape=jax.ShapeDtypeStruct((n_attn, REL_BUCKETS, LANES), F32), compiler_params=_cparams("parallel"),
    )(idx, dbias)


def _attn_probs(q, kb, bias, sink, valid):
    s = _dot_nt(q, kb) / math.sqrt(HEAD) + bias
    s = jnp.where(valid, s, NEG_INF)
    m = jnp.maximum(jnp.max(s, axis=-1, keepdims=True), sink)
    e = jnp.exp(s - m)
    e_sink = jnp.exp(sink - m)
    den = jnp.sum(e, axis=-1, keepdims=True) + e_sink
    return e / den, e_sink / den


def _attn_valid(n, t, grp):
    c = lax.broadcasted_iota(jnp.int32, (grp * WINDOW, KEY_SPAN), 0) & (WINDOW - 1)
    s = lax.broadcasted_iota(jnp.int32, (grp * WINDOW, KEY_SPAN), 1)
    rel = s - WINDOW - c
    key_pos = n * WINDOW - WINDOW + s
    return (jnp.abs(rel) <= WINDOW) & (key_pos >= 0) & (key_pos < t)


def _stack_heads(ref, grp):
    return jnp.concatenate([ref[:, g * HEAD:(g + 1) * HEAD] for g in range(grp)], axis=0).astype(BF16)


def _sink_column(sink_ref, x, grp):
    return jnp.concatenate([jnp.full((WINDOW, 1), sink_ref[0, x * grp + g], F32) for g in range(grp)], axis=0)


def _attn_specs(t, n_hgrn, n_attn):
    grp = n_attn // KV_HEADS
    nb = t // WINDOW
    cq = 5 * n_hgrn
    ck = cq + n_attn
    cv = ck + KV_HEADS
    q_spec = pl.BlockSpec((WINDOW, grp * HEAD), lambda x, n: (n, cq // grp + x))
    kv = lambda base, off: pl.BlockSpec(
        (WINDOW, HEAD), lambda x, n: (jnp.clip(n + off, 0, nb - 1), base + x))
    band = [kv(ck, -1), kv(ck, 0), kv(ck, 1), kv(cv, -1), kv(cv, 0), kv(cv, 1)]
    bias_spec = pl.BlockSpec((grp, WINDOW, KEY_SPAN), lambda x, n: (x, 0, 0))
    sink_spec = pl.BlockSpec(memory_space=pltpu.SMEM)
    return grp, nb, q_spec, band, bias_spec, sink_spec


def _attn_fwd(p, bias, sink, cat, n_hgrn, n_attn):
    t = p.shape[0]
    grp, nb, q_spec, band, bias_spec, sink_spec = _attn_specs(t, n_hgrn, n_attn)

    def body(q_ref, kp, kc, kn, vp, vc, vn, bias_ref, sink_ref, cat_ref, y_ref):
        x, n = pl.program_id(0), pl.program_id(1)
        kb = jnp.concatenate([kp[...], kc[...], kn[...]], axis=0).astype(BF16)
        vb = jnp.concatenate([vp[...], vc[...], vn[...]], axis=0).astype(BF16)
        pr, _ = _attn_probs(_stack_heads(q_ref, grp), kb, bias_ref[...].reshape(grp * WINDOW, KEY_SPAN),
                            _sink_column(sink_ref, x, grp), _attn_valid(n, t, grp))
        y = _dot(pr.astype(BF16), vb).astype(BF16)
        for g in range(grp):
            y_ref[:, g * HEAD:(g + 1) * HEAD] = y[g * WINDOW:(g + 1) * WINDOW]

    return pl.pallas_call(
        body, name="attn_fwd", grid=(KV_HEADS, nb),
        in_specs=[q_spec] + band + [bias_spec, sink_spec, ANY],
        out_specs=pl.BlockSpec((WINDOW, grp * HEAD), lambda x, n: (n, n_hgrn // grp + x)),
        out_shape=jax.ShapeDtypeStruct(cat.shape, BF16), input_output_aliases={9: 0},
        compiler_params=_cparams("parallel", "parallel"),
    )(p, p, p, p, p, p, p, bias, sink, cat)


def _attn_bwd(p, dcat, bias, sink, n_hgrn, n_attn, deps=()):
    t = p.shape[0]
    grp, nb, q_spec, band, bias_spec, sink_spec = _attn_specs(t, n_hgrn, n_attn)
    inv = 1.0 / math.sqrt(HEAD)

    def body(q_ref, kp, kc, kn, vp, vc, vn, bias_ref, sink_ref, do_ref, *rest):
        dq_ref, dk_ref, dv_ref, dbias_ref, dsink_ref, dk_s, dv_s = rest[-7:]
        x, n = pl.program_id(0), pl.program_id(1)

        @pl.when(n == 0)
        def _():
            dk_s[...] = jnp.zeros_like(dk_s)
            dv_s[...] = jnp.zeros_like(dv_s)
            dbias_ref[...] = jnp.zeros_like(dbias_ref)
            dsink_ref[...] = jnp.zeros_like(dsink_ref)

        kb = jnp.concatenate([kp[...], kc[...], kn[...]], axis=0).astype(BF16)
        vb = jnp.concatenate([vp[...], vc[...], vn[...]], axis=0).astype(BF16)
        q = _stack_heads(q_ref, grp)
        do = _stack_heads(do_ref, grp)
        pr, p_sink = _attn_probs(q, kb, bias_ref[...].reshape(grp * WINDOW, KEY_SPAN),
                                 _sink_column(sink_ref, x, grp), _attn_valid(n, t, grp))
        dpr = _dot_nt(do, vb)
        delta = jnp.sum(pr * dpr, axis=-1, keepdims=True)
        ds = pr * (dpr - delta)
        ds_b = ds.astype(BF16)
        dq = (_dot(ds_b, kb) * inv).astype(BF16)
        dsink = -p_sink * delta
        for g in range(grp):
            head = slice(g * WINDOW, (g + 1) * WINDOW)
            dq_ref[:, g * HEAD:(g + 1) * HEAD] = dq[head]
            dbias_ref[g] += ds[head]
            dsink_ref[g:g + 1, :] += jnp.broadcast_to(jnp.sum(dsink[head], axis=0, keepdims=True), (1, WINDOW))
        rows = pl.ds(pl.multiple_of(n * WINDOW, WINDOW), KEY_SPAN)
        dk_s[rows, :] += _dot_tn(ds_b, q) * inv
        dv_s[rows, :] += _dot_tn(pr.astype(BF16), do)

        @pl.when(n == nb - 1)
        def _():
            dk_ref[...] = dk_s[pl.ds(WINDOW, t), :].astype(BF16)
            dv_ref[...] = dv_s[pl.ds(WINDOW, t), :].astype(BF16)

    do_spec = pl.BlockSpec((WINDOW, grp * HEAD), lambda x, n: (n, n_hgrn // grp + x))
    kv_out = pl.BlockSpec((t, HEAD), lambda x, n: (0, x))
    return pl.pallas_call(
        body, name="attn_bwd", grid=(KV_HEADS, nb),
        in_specs=[q_spec] + band + [bias_spec, sink_spec, do_spec] + [ANY] * len(deps),
        out_specs=[pl.BlockSpec((WINDOW, grp * HEAD), lambda x, n: (n, x)), kv_out, kv_out,
                   bias_spec, pl.BlockSpec((None, grp, WINDOW), lambda x, n: (x, 0, 0))],
        out_shape=[jax.ShapeDtypeStruct((t, n_attn * HEAD), BF16),
                   jax.ShapeDtypeStruct((t, KV_HEADS * HEAD), BF16),
                   jax.ShapeDtypeStruct((t, KV_HEADS * HEAD), BF16),
                   jax.ShapeDtypeStruct((n_attn, WINDOW, KEY_SPAN), F32),
                   jax.ShapeDtypeStruct((KV_HEADS, grp, WINDOW), F32)],
        scratch_shapes=[pltpu.VMEM((t + 2 * WINDOW, HEAD), F32), pltpu.VMEM((t + 2 * WINDOW, HEAD), F32)],
        compiler_params=_cparams("parallel", "arbitrary"),
    )(p, p, p, p, p, p, p, bias, sink, dcat, *deps)


def _position():
    return lax.axis_index("x"), lax.axis_index("y"), lax.axis_index("c")


def _handshake(peers):
    barrier = pltpu.get_barrier_semaphore()
    for peer in peers:
        pl.semaphore_signal(barrier, inc=1, device_id=peer, device_id_type=MESH)
    pl.semaphore_wait(barrier, len(peers))


def _sequencer(name, collective_id, scratch_types):
    return functools.partial(
        pl.kernel, mesh=plsc.ScalarSubcoreMesh(axis_name="sc", num_cores=1), name=name,
        scratch_types=scratch_types, compiler_params=pltpu.CompilerParams(collective_id=collective_id))


def _all_gather(name, shard, collective_id):
    rows = shard.shape[0]
    assert rows % 2 == 0
    rh = rows // 2
    src = jax.new_ref(shard, memory_space=pltpu.MemorySpace.HBM)
    out = jax.empty_ref(jax.ShapeDtypeStruct((N_DEV,) + shard.shape, shard.dtype),
                        memory_space=pltpu.MemorySpace.HBM)
    n_copies = 11

    @_sequencer(name, collective_id, (pltpu.SemaphoreType.DMA((n_copies,)), pltpu.SemaphoreType.DMA((n_copies,)),
                                      pltpu.SemaphoreType.DMA))
    def launch(send_sems, recv_sems, local_sem):
        x, y, c = _position()
        sibling = (x, y, 1 - c)
        xn, yn, dg = (1 - x, y), (x, 1 - y), (1 - x, 1 - y)
        _handshake([sibling, (*xn, c), (*yn, c)])

        def part(ref, half):
            return ref if half is None else ref.at[pl.ds(half * rh, rh)]

        def slot(chip, core, half=None):
            return part(out.at[4 * chip[0] + 2 * chip[1] + core], half)

        def copy(k, chip, core, half, to, own=False):
            return pltpu.make_async_remote_copy(
                src_ref=part(src, half) if own else slot(chip, core, half), dst_ref=slot(chip, core, half),
                send_sem=send_sems.at[k], recv_sem=recv_sems.at[k], device_id=to, device_id_type=MESH)

        def landed(k, chip, core, half):
            copy(k, chip, core, half, (x, y, c)).wait_recv()

        mine = pltpu.make_async_copy(src, slot((x, y), c), local_sem)
        mine.start()
        sent = [copy(0, (x, y), c, None, sibling, own=True),
                copy(1, (x, y), c, 0, (*xn, c), own=True), copy(3, (x, y), c, 1, (*yn, c), own=True),
                copy(2, (x, y), c, 1, (*xn, c), own=True), copy(4, (x, y), c, 0, (*yn, c), own=True)]
        for cp in sent:
            cp.start()

        def then(cp):
            cp.start()
            sent.append(cp)

        landed(1, xn, c, 0)
        then(copy(5, xn, c, 0, (*yn, c)))
        landed(3, yn, c, 1)
        then(copy(6, yn, c, 1, (*xn, c)))
        landed(2, xn, c, 1)
        then(copy(7, xn, c, None, sibling))
        landed(4, yn, c, 0)
        then(copy(8, yn, c, None, sibling))
        landed(5, dg, c, 0)
        then(copy(9, dg, c, 0, sibling))
        landed(6, dg, c, 1)
        then(copy(10, dg, c, 1, sibling))
        landed(0, (x, y), 1 - c, None)
        landed(7, xn, 1 - c, None)
        landed(8, yn, 1 - c, None)
        landed(9, dg, 1 - c, 0)
        landed(10, dg, 1 - c, 1)
        for cp in sent:
            cp.wait_send()
        mine.wait()

    launch()
    return out[...]


HBM = pl.BlockSpec(memory_space=pltpu.HBM)
SEM = pl.BlockSpec(memory_space=pltpu.SEMAPHORE)
EFFECT = pltpu.SideEffectType.DATAFLOW_SIDE_EFFECTING


def _pair_copies(s_ref, land_ref, send_sems, recv_sems):
    x, y, c = _position()
    return [pltpu.make_async_remote_copy(
        src_ref=s_ref.at[2 * k + (1 - c)], dst_ref=land_ref.at[k], send_sem=send_sems.at[k],
        recv_sem=recv_sems.at[k], device_id=(x, y, 1 - c), device_id_type=MESH) for k in range(4)]


def _pair_start(name, stack):
    land_shape = (4,) + stack.shape[1:]

    def body(s_ref, land_ref, send_sems, recv_sems, s_thru, land_thru, token):
        for cp in _pair_copies(s_ref, land_ref, send_sems, recv_sems):
            cp.start()
        token[...] = jnp.zeros_like(token)

    return pl.pallas_call(
        body, name=name,
        out_shape=(pltpu.SemaphoreType.DMA((4,)), pltpu.SemaphoreType.DMA((4,)),
                   pltpu.HBM(stack.shape, stack.dtype), pltpu.HBM(land_shape, stack.dtype),
                   jax.ShapeDtypeStruct((8, LANES), F32)),
        in_specs=(HBM, HBM), out_specs=(SEM, SEM, HBM, HBM, pl.BlockSpec(memory_space=pltpu.VMEM)),
        input_output_aliases={0: 2, 1: 3}, compiler_params=pltpu.CompilerParams(has_side_effects=EFFECT),
    )(pltpu.with_memory_space_constraint(stack, pltpu.HBM),
      pltpu.with_memory_space_constraint(lax.empty(land_shape, stack.dtype), pltpu.HBM))


def _pair_wait(name, started, after):
    send_sems, recv_sems, s_thru, land_thru, _ = started

    def body(s_ref, land_ref, send_sems, recv_sems, after_ref, s_out, land_out):
        for cp in _pair_copies(s_ref, land_ref, send_sems, recv_sems):
            cp.wait_send()
            cp.wait_recv()

    return pl.pallas_call(
        body, name=name,
        out_shape=(pltpu.HBM(s_thru.shape, s_thru.dtype), pltpu.HBM(land_thru.shape, land_thru.dtype)),
        in_specs=(HBM, HBM, SEM, SEM, ANY), out_specs=(HBM, HBM), input_output_aliases={0: 0, 1: 1},
        compiler_params=pltpu.CompilerParams(has_side_effects=EFFECT),
    )(s_thru, land_thru, send_sems, recv_sems, after)


def _pair_sum(name, stack, other, core):
    _, r, c = stack.shape
    tr = _row_tile(r, 1024)

    def body(core_ref, a_ref, b_ref, o_ref):
        o_ref[...] = (a_ref[...].astype(F32) + b_ref[...].astype(F32)).astype(o_ref.dtype)

    grid_spec = pltpu.PrefetchScalarGridSpec(
        num_scalar_prefetch=1, grid=(4, r // tr),
        in_specs=[pl.BlockSpec((None, tr, c), lambda k, i, core_ref: (2 * k + core_ref[0], i, 0)),
                  pl.BlockSpec((None, tr, c), lambda k, i, core_ref: (k, i, 0))],
        out_specs=pl.BlockSpec((None, tr, c), lambda k, i, core_ref: (k, i, 0)))
    return pl.pallas_call(
        body, name=name, grid_spec=grid_spec, out_shape=jax.ShapeDtypeStruct((4, r, c), stack.dtype),
        compiler_params=_cparams("parallel", "parallel"),
    )(core, stack, other)


def _chip_exchange(name, sums, collective_id):
    src = jax.new_ref(sums, memory_space=pltpu.MemorySpace.HBM)
    out = jax.empty_ref(jax.ShapeDtypeStruct((3,) + sums.shape[1:], sums.dtype),
                        memory_space=pltpu.MemorySpace.HBM)

    @_sequencer(name, collective_id, (pltpu.SemaphoreType.DMA((3,)), pltpu.SemaphoreType.DMA((3,))))
    def launch(send_sems, recv_sems):
        x, y, c = _position()
        chips = [(1 - x, y), (x, 1 - y), (1 - x, 1 - y)]
        _handshake([(*chip, c) for chip in chips])
        copies = [pltpu.make_async_remote_copy(
            src_ref=src.at[2 * px + py], dst_ref=out.at[j], send_sem=send_sems.at[j],
            recv_sem=recv_sems.at[j], device_id=(px, py, c), device_id_type=MESH)
            for j, (px, py) in enumerate(chips)]
        for cp in copies:
            cp.start()
        for cp in copies:
            cp.wait()

    launch()
    return out[...]


def _small_rows(shapes):
    first, row = [], 0
    for r, c in shapes:
        first.append(row)
        row += r * (c // LANES) if c % LANES == 0 else r
        row = -(-row // 8) * 8
    return first, row


def _small_move(packed, row, ref, to_packed):
    r, c = ref.shape
    if c % LANES:
        if to_packed:
            packed[row:row + r, 0:c] = ref[...]
        else:
            ref[...] = packed[row:row + r, 0:c]
        return
    per = c // LANES
    for i in range(r):
        for j in range(per):
            at = row + i * per + j
            if to_packed:
                packed[at:at + 1, :] = ref[i:i + 1, j * LANES:(j + 1) * LANES]
            else:
                ref[i:i + 1, j * LANES:(j + 1) * LANES] = packed[at:at + 1, :]


def _small_step(name, parts, ws, ms, vs, deps=()):
    n_par = len(ws)
    first, rows = _small_rows([p.shape for p in parts])
    vm = pl.BlockSpec(memory_space=pltpu.VMEM)
    buf = pltpu.VMEM((rows, LANES), F32)

    def reduce_body(*refs):
        part_refs, (sum_out, mine, gather, send_sems, recv_sems) = refs[:n_par + 1], refs[-5:]
        x, y, c = _position()
        me = 4 * x + 2 * y + c
        mine[...] = jnp.zeros_like(mine)
        for k, ref in enumerate(part_refs):
            _small_move(mine, first[k], ref, True)
        gather[me] = mine[...]
        copies = []
        for k in range(1, N_DEV):
            peer = (x ^ (k >> 2), y ^ ((k >> 1) & 1), c ^ (k & 1))
            copies.append(pltpu.make_async_remote_copy(
                src_ref=mine, dst_ref=gather.at[me], send_sem=send_sems.at[k - 1],
                recv_sem=recv_sems.at[k - 1], device_id=peer, device_id_type=MESH))
        for cp in copies:
            cp.start()
        for k in range(1, N_DEV):
            peer_slot = 4 * (x ^ (k >> 2)) + 2 * (y ^ ((k >> 1) & 1)) + (c ^ (k & 1))
            pltpu.make_async_remote_copy(
                src_ref=mine, dst_ref=gather.at[peer_slot], send_sem=send_sems.at[k - 1],
                recv_sem=recv_sems.at[k - 1], device_id=(x, y, c), device_id_type=MESH).wait()
        acc = gather[0]
        for j in range(1, N_DEV):
            acc = acc + gather[j]
        sum_out[...] = acc

    summed = pl.pallas_call(
        reduce_body, name=name + "_reduce", in_specs=[vm] * (n_par + 1) + [ANY] * len(deps), out_specs=vm,
        out_shape=jax.ShapeDtypeStruct((rows, LANES), F32),
        scratch_shapes=[buf, pltpu.VMEM((N_DEV, rows, LANES), F32), pltpu.SemaphoreType.DMA((7,)),
                        pltpu.SemaphoreType.DMA((7,))],
    )(*parts, *deps)

    def adam_body(*refs):
        sum_ref, refs = refs[0], refs[1:]
        w_refs, m_refs, v_refs, refs = refs[:n_par], refs[n_par:2 * n_par], refs[2 * n_par:3 * n_par], refs[3 * n_par:]
        g_out, d_out, m_out, v_out = (refs[i * n_par:(i + 1) * n_par] for i in range(4))
        loss_out = refs[4 * n_par]
        w_p, m_p, v_p, d_p = refs[4 * n_par + 1:]
        for packed in (w_p, m_p, v_p):
            packed[...] = jnp.zeros_like(packed)
        for k in range(n_par):
            for packed, src in ((w_p, w_refs[k]), (m_p, m_refs[k]), (v_p, v_refs[k])):
                _small_move(packed, first[k], src, True)
        delta, m_new, v_new = _adam_math(w_p[...], sum_ref[...], m_p[...], v_p[...])
        d_p[...] = delta
        m_p[...] = m_new
        v_p[...] = v_new
        for k in range(n_par):
            for packed, dst in ((sum_ref, g_out[k]), (d_p, d_out[k]), (m_p, m_out[k]), (v_p, v_out[k])):
                _small_move(packed, first[k], dst, False)
        _small_move(sum_ref, first[n_par], loss_out, False)

    like = [jax.ShapeDtypeStruct(w.shape, F32) for w in ws]
    outs = pl.pallas_call(
        adam_body, name=name + "_adam", in_specs=[vm] * (3 * n_par + 1), out_specs=[vm] * (4 * n_par + 1),
        out_shape=like * 4 + [jax.ShapeDtypeStruct((1, LANES), F32)], scratch_shapes=[buf, buf, buf, buf],
    )(summed, *ws, *ms, *vs)
    return (outs[:n_par], outs[n_par:2 * n_par], outs[2 * n_par:3 * n_par], outs[3 * n_par:4 * n_par],
            outs[4 * n_par])


def _adam_math(w, g, m, v):
    m = ADAM_B1 * m + (1.0 - ADAM_B1) * g
    v = ADAM_B2 * v + (1.0 - ADAM_B2) * jnp.square(g)
    m_hat = m / (1.0 - ADAM_B1 ** ADAM_STEP)
    v_hat = v / (1.0 - ADAM_B2 ** ADAM_STEP)
    delta = -ADAM_LR * (m_hat / (jnp.sqrt(v_hat) + ADAM_EPS) + ADAM_WD * w)
    return delta, m, v


def _adam_shard(name, w, m, v, sums, recv, chip, deps=(), first_row=0, earlier=()):
    r, c = w.shape
    rows = sums.shape[1]
    tr = _row_tile(rows, 256)
    assert first_row % tr == 0
    skip = first_row // tr

    def body(chip_ref, w_ref, m_ref, v_ref, own_ref, r0_ref, r1_ref, r2_ref, *rest):
        g_out, d_out, m_out, v_out = rest[-4:]
        g = ((own_ref[...].astype(F32) + r0_ref[...].astype(F32)) + r1_ref[...].astype(F32)) + r2_ref[...].astype(F32)
        delta, m_new, v_new = _adam_math(w_ref[...], g, m_ref[...], v_ref[...])
        g_out[...] = g
        d_out[...] = delta
        m_out[...] = m_new
        v_out[...] = v_new

    plain = pl.BlockSpec((tr, c), lambda i, chip_ref: (skip + i, 0))
    piece = lambda j: pl.BlockSpec((None, tr, c), lambda i, chip_ref: (j, i, 0))
    grid_spec = pltpu.PrefetchScalarGridSpec(
        num_scalar_prefetch=1, grid=(rows // tr,),
        in_specs=[plain, plain, plain,
                  pl.BlockSpec((None, tr, c), lambda i, chip_ref: (chip_ref[0], i, 0)),
                  piece(0), piece(1), piece(2)] + [ANY] * (len(earlier) + len(deps)),
        out_specs=[plain] * 4)
    shape = jax.ShapeDtypeStruct((r, c), F32)
    return pl.pallas_call(
        body, name=name, grid_spec=grid_spec, out_shape=[shape] * 4, compiler_params=_cparams("parallel"),
        input_output_aliases={8 + k: k for k in range(len(earlier))},
    )(chip, w, m, v, sums, recv, recv, recv, *earlier, *deps)


def _reduce_scatter(tag, started, after, core, collective_id):
    grad_stack, other = _pair_wait("rs_pair_wait_" + tag, started, after)
    sums = _pair_sum("rs_sum_" + tag, grad_stack, other, core)
    return sums, _chip_exchange("rs_chip_" + tag, sums, collective_id)


SMALL = ("pre_norm_ffn1", "post_norm_ffn1", "pre_norm_mix", "post_norm_mix", "hgrn_lower_bounds_fwd",
         "hgrn_lower_bounds_bwd", "hgrn_out_norm", "attn_sink", "pre_norm_ffn2", "post_norm_ffn2", "rel_bias_table")
BIG = ("w_ffn1_gate_up", "w_ffn1_down", "w_mix_in", "w_mix_out", "w_ffn2_gate_up", "w_ffn2_down")
AG_ID = {n: 1 + i for i, n in enumerate(BIG)}
RS_ID = {n: 7 + i for i, n in enumerate(BIG)}
RS_ID.update(w_ffn1_gate_up_a=RS_ID["w_ffn1_gate_up"], w_ffn1_gate_up_b=13)
ORDER = ("pre_norm_ffn1", "post_norm_ffn1", "w_ffn1_gate_up", "w_ffn1_down", "pre_norm_mix", "post_norm_mix",
         "w_mix_in", "hgrn_lower_bounds_fwd", "hgrn_lower_bounds_bwd", "hgrn_out_norm", "attn_sink", "w_mix_out",
         "pre_norm_ffn2", "post_norm_ffn2", "w_ffn2_gate_up", "w_ffn2_down", "rel_bias_table")


def kernel(x, pre_norm_ffn1, post_norm_ffn1, w_ffn1_gate_up, w_ffn1_down, pre_norm_mix, post_norm_mix, w_mix_in, hgrn_lower_bounds_fwd, hgrn_lower_bounds_bwd, hgrn_out_norm, attn_sink, w_mix_out, pre_norm_ffn2, post_norm_ffn2, w_ffn2_gate_up, w_ffn2_down, rel_bias_table, loss_target, m_pre_norm_ffn1, m_post_norm_ffn1, m_w_ffn1_gate_up, m_w_ffn1_down, m_pre_norm_mix, m_post_norm_mix, m_w_mix_in, m_hgrn_lower_bounds_fwd, m_hgrn_lower_bounds_bwd, m_hgrn_out_norm, m_attn_sink, m_w_mix_out, m_pre_norm_ffn2, m_post_norm_ffn2, m_w_ffn2_gate_up, m_w_ffn2_down, m_rel_bias_table, v_pre_norm_ffn1, v_post_norm_ffn1, v_w_ffn1_gate_up, v_w_ffn1_down, v_pre_norm_mix, v_post_norm_mix, v_w_mix_in, v_hgrn_lower_bounds_fwd, v_hgrn_lower_bounds_bwd, v_hgrn_out_norm, v_attn_sink, v_w_mix_out, v_pre_norm_ffn2, v_post_norm_ffn2, v_w_ffn2_gate_up, v_w_ffn2_down, v_rel_bias_table):
    args = dict(locals())
    wts = {n: args[n] for n in ORDER}
    mom = {n: args["m_" + n] for n in ORDER}
    var = {n: args["v_" + n] for n in ORDER}

    x0 = x[0]
    target = loss_target[0]
    t, d = x0.shape
    n_hgrn = d // 2 // HEAD
    n_attn = (d - d // 2) // HEAD
    core = lax.axis_index("c").astype(jnp.int32).reshape(1)
    chip = (2 * lax.axis_index("x") + lax.axis_index("y")).astype(jnp.int32).reshape(1)

    def local(a, name):
        return jnp.transpose(a[0]) if name == "w_mix_in" else a[0]

    full = {n: _all_gather("ag_" + n, local(wts[n], n).astype(BF16), AG_ID[n]) for n in BIG}
    w_gu1, w_gu2 = full["w_ffn1_gate_up"], full["w_ffn2_gate_up"]
    w_d1 = full["w_ffn1_down"].reshape(-1, d)
    w_d2 = full["w_ffn2_down"].reshape(-1, d)
    w_out = full["w_mix_out"].reshape(-1, d)
    w_in_t = full["w_mix_in"].reshape(-1, d)

    g = {n: wts[n] for n in SMALL}
    lb_f = jax.nn.softmax(g["hgrn_lower_bounds_fwd"], axis=0)[0:1]
    lb_b = jax.nn.softmax(g["hgrn_lower_bounds_bwd"], axis=0)[0:1]
    bucket_idx = jnp.asarray(_t5_bucket_index())
    bias = _bias_build(g["rel_bias_table"], bucket_idx)

    n1 = _pre_norm("pre_norm1", x0, g["pre_norm_ffn1"])
    a1, gu1 = _ffn_up("ffn1_gate_up", n1, w_gu1)
    ff1 = _matmul("ffn1_down", a1, w_d1, mode="nn", out_dtype=F32)
    x1, h = _post_res_pre("res1", x0, ff1, g["post_norm_ffn1"], g["pre_norm_mix"], 0.5)
    p = _matmul("mix_in", h, w_in_t, mode="nt", out_dtype=F32, tm=2048, tn=512)
    y_h, o_raw = _hgrn_fwd(p, lb_f, lb_b, g["hgrn_out_norm"], n_hgrn, d)
    cat = _attn_fwd(p, bias, g["attn_sink"], y_h, n_hgrn, n_attn)
    mixed = _matmul("mix_out", cat, w_out, mode="nn", out_dtype=F32)
    x2, n2 = _post_res_pre("res2", x1, mixed, g["post_norm_mix"], g["pre_norm_ffn2"], 1.0)
    a2, gu2 = _ffn_up("ffn2_gate_up", n2, w_gu2)
    ff2 = _matmul("ffn2_down", a2, w_d2, mode="nn", out_dtype=F32)
    small_grad = {}
    dy3, loss_part, dff2, small_grad["post_norm_ffn2"] = _post_res_loss(
        "res3_loss", x2, ff2, g["post_norm_ffn2"], target, 0.5)

    scattered = {}

    pending = []

    def scatter(name, grad_stack):
        started = _pair_start("rs_pair_" + name, grad_stack)
        pending.append((name, started))
        return [started[4]]

    def settle(after, count=len(BIG)):
        deps = []
        while pending and count:
            name, started = pending.pop(0)
            scattered[name] = _reduce_scatter(name, started, after, core, RS_ID[name])
            deps.append(scattered[name][0])
            count -= 1
        return deps

    def ffn_bwd(tag, dff, a, gu, n_in, w_gu, w_d, gu_name, d_name, last):
        dep = settle(dff)

        def dw_down(deps):
            return scatter(d_name, _matmul("dw_down" + tag, a, dff, mode="tn", out_dtype=BF16, tn=2048,
                                           deps=deps).reshape(N_DEV, -1, d))

        def dw_gate_up(part, rows, deps):
            return scatter(gu_name + part, _matmul("dw_gate_up" + tag + part, n_in, dgu, mode="tn", a_cols=rows,
                                                   stack=True, halves=True, out_dtype=BF16, deps=deps))

        dep = dw_down(dep)
        dgu = _ffn_dact("d_act" + tag, dff, w_d, gu, deps=dep)
        dep = settle(dgu)
        if last:
            half = d // 2
            dep = dw_gate_up("_a", slice(0, half), dep)
            dep = dw_gate_up("_b", slice(half, d), dep)
            dep = settle(dep[0], count=1) + dep
        else:
            dep = dw_gate_up("", slice(0, d), dep)
        dn = _matmul("d_norm" + tag, dgu, w_gu, mode="nt", stack=True, halves=True, out_dtype=BF16, deps=dep)
        return dn, settle(dn)

    dn2, dep = ffn_bwd("2", dff2, a2, gu2, n2, w_gu2, w_d2, "w_ffn2_gate_up", "w_ffn2_down", last=False)
    dx2, small_grad["pre_norm_ffn2"], dmixed, small_grad["post_norm_mix"] = _pre_bwd(
        "pre_bwd2", dn2, x2, g["pre_norm_ffn2"], dy3, deps=dep, post=(mixed, g["post_norm_mix"], 1.0))
    dep = settle(dmixed)
    dcat = _matmul("d_cat", dmixed, w_out, mode="nt", out_dtype=BF16, deps=dep)
    dep = scatter("w_mix_out", _matmul("dw_mix_out", cat, dmixed, mode="tn", out_dtype=BF16).reshape(N_DEV, -1, d))
    dq_a, dk_a, dv_a, dbias, dsink_rows = _attn_bwd(p, dcat, bias, g["attn_sink"], n_hgrn, n_attn, deps=dep)
    dq_h, di_h, dzf, dzb, dg_h, dlb_f, dlb_b, small_grad["hgrn_out_norm"] = _hgrn_bwd(
        p, o_raw, dcat, lb_f, lb_b, g["hgrn_out_norm"], n_hgrn)
    dp = jnp.concatenate([dq_h, di_h, dzf, dzb, dg_h, dq_a, dk_a, dv_a], axis=1)
    dep = settle(dp)
    dh = _matmul("d_h", dp, w_in_t, mode="nn", out_dtype=BF16, tm=2048, deps=dep)
    dep = scatter("w_mix_in", _matmul("dw_mix_in", dp, h, mode="tn", out_dtype=BF16, tm=512,
                                         tn=2048).reshape(N_DEV, -1, d))
    dx1, small_grad["pre_norm_mix"], dff1, small_grad["post_norm_ffn1"] = _pre_bwd(
        "pre_bwd_mix", dh, x1, g["pre_norm_mix"], dx2, deps=dep, post=(ff1, g["post_norm_ffn1"], 0.5))

    dn1, dep = ffn_bwd("1", dff1, a1, gu1, n1, w_gu1, w_d1, "w_ffn1_gate_up", "w_ffn1_down", last=True)
    dx0, small_grad["pre_norm_ffn1"] = _pre_bwd("pre_bwd1", dn1, x0, g["pre_norm_ffn1"], dx1, deps=dep)

    def lb_grad(dlb, lb):
        da0 = dlb * lb * (1.0 - lb)
        return jnp.concatenate([da0, -da0], axis=0)

    small_grad["hgrn_lower_bounds_fwd"] = lb_grad(dlb_f, lb_f)
    small_grad["hgrn_lower_bounds_bwd"] = lb_grad(dlb_b, lb_b)
    small_grad["attn_sink"] = dsink_rows[:, :, 0].reshape(1, n_attn)
    small_grad["rel_bias_table"] = jnp.transpose(_bias_reduce(dbias, bucket_idx)[:, :, 0])

    def adam(n, tag, dep, **rows):
        sums, recv = scattered[n + tag]
        return _adam_shard("adam_" + n + tag, local(wts[n], n), local(mom[n], n), local(var[n], n), sums, recv,
                           chip, deps=dep, **rows)

    big_out = {}
    dep = []
    for n in ("w_ffn2_down", "w_ffn2_gate_up", "w_mix_out", "w_mix_in", "w_ffn1_down"):
        big_out[n] = adam(n, "", dep)
        dep = [big_out[n][0]]
    first_half = adam("w_ffn1_gate_up", "_a", dep)

    g_s, d_s, m_s, v_s, loss_row = _small_step(
        "small_step", [small_grad[n] for n in SMALL] + [loss_part], [wts[n] for n in SMALL],
        [mom[n] for n in SMALL], [var[n] for n in SMALL], deps=[first_half[0]])
    loss = loss_row[0, 0]
    grads, delta, new_m, new_v = (dict(zip(SMALL, vals)) for vals in (g_s, d_s, m_s, v_s))

    big_out["w_ffn1_gate_up"] = adam("w_ffn1_gate_up", "_b", [loss_row], first_row=d // 2, earlier=first_half)
    for n in BIG:
        grads[n], delta[n], new_m[n], new_v[n] = [local(o[None], n)[None] for o in big_out[n]]

    return (loss, dx0[None], *[grads[n] for n in ORDER], *[delta[n] for n in ORDER],
            *[new_m[n] for n in ORDER], *[new_v[n] for n in ORDER])
```
